```python
import jax, jax.numpy as jnp
from jax import lax
import numpy as np

D_MODEL = 1024
BATCH = 8
SEQ = 2048
DEPTH = 2
DEC_BATCH = 128
DEC_SEQ = 1
PAST_LEN = 16384
PAGE_SIZE = 128

N_LAYERS_A = (DEPTH + 1) // 2
N_LAYERS_B = DEPTH // 2
DN_HEADS = 8
DN_DK = 128
DN_DV = 128
DN_QK_W = DN_HEADS * DN_DK
DN_V_W = DN_HEADS * DN_DV
CONV_WIDTH = 4
CONV_CH = 2 * DN_QK_W + DN_V_W
CHUNK = 64
POOL_WINDOWS = (2, 4, 8, 16)
POOL_GROUPS = len(POOL_WINDOWS)
POOL_W = D_MODEL
POOL_GW = POOL_W // POOL_GROUPS
POOL_BUF = max(POOL_WINDOWS) - 1
N_MEM = 256
X_HEADS = 4
X_DH = 128
X_W = X_HEADS * X_DH
D_FF = 2816
N_EXPERTS = 8
TOP_K = 2
D_FF_EXPERT = 3584
A_IN_W = CONV_CH + DN_V_W + 2 * DN_HEADS + X_W
A_MIX_W = DN_V_W + X_W
B_IN_W = POOL_W + X_W
B_MIX_W = POOL_W + X_W
EPS = 1e-6

kernel_name = "hybrid_deltanet_pool_memxattn_decode_step"


def rmsnorm(x, w):
    xf = x.astype(jnp.float32)
    y = xf * lax.rsqrt(jnp.mean(xf * xf, axis=-1, keepdims=True) + EPS)
    return (y * w.astype(jnp.float32)).astype(x.dtype)


def l2norm(x):
    return x * lax.rsqrt(jnp.sum(x * x, axis=-1, keepdims=True) + EPS)


def swiglu(x, wi, wo):
    a, b = jnp.split(x @ wi, 2, axis=-1)
    return (jax.nn.silu(a) * b) @ wo


def causal_short_conv(xs_new, buf, w):
    t = xs_new.shape[1]
    xc = jnp.concatenate([buf.astype(xs_new.dtype), xs_new], axis=1)
    y = xc[:, 0:t] * w[0]
    for j in range(1, CONV_WIDTH):
        y = y + xc[:, j:j + t] * w[j]
    return jax.nn.silu(y), xc[:, -(CONV_WIDTH - 1):]


def gated_delta_chunked(q, k, v, g, beta, s0):
    b, t, h, _ = q.shape
    dv = v.shape[-1]
    n = t // CHUNK

    def chunks(a):
        a = a.reshape(b, n, CHUNK, h, *a.shape[3:])
        return jnp.moveaxis(a, (1, 3), (0, 2))

    qc, kc, vc, gc, bc = (chunks(a) for a in (q, k, v, g, beta))
    G = jnp.cumsum(gc, axis=-1)
    idx = jnp.arange(CHUNK)
    incl = idx[:, None] >= idx[None, :]
    strict = idx[:, None] > idx[None, :]
    decay = jnp.exp(jnp.where(incl, G[..., :, None] - G[..., None, :], -jnp.inf))
    kk = jnp.einsum('nbhid,nbhjd->nbhij', kc, kc)
    m = jnp.where(strict, bc[..., :, None] * kk * decay, 0.0)
    a_mat = m + jnp.eye(CHUNK, dtype=m.dtype)
    rhs = jnp.concatenate([vc * bc[..., None], kc * (bc * jnp.exp(G))[..., None]], axis=-1)
    sol = lax.linalg.triangular_solve(a_mat, rhs, left_side=True, lower=True, unit_diagonal=True)
    w_val, k_cd = sol[..., :dv], sol[..., dv:]
    qk = jnp.einsum('nbhid,nbhjd->nbhij', qc, kc) * decay
    q_dec = qc * jnp.exp(G)[..., None]
    g_last = G[..., -1:]
    k_tail = kc * jnp.exp(g_last - G)[..., None]
    c_decay = jnp.exp(g_last[..., 0])

    def step(s, xs):
        w_n, kcd_n, qk_n, qd_n, kt_n, cd_n = xs
        u = w_n - jnp.einsum('bhck,bhkv->bhcv', kcd_n, s)
        o = jnp.einsum('bhck,bhkv->bhcv', qd_n, s) + jnp.einsum('bhij,bhjv->bhiv', qk_n, u)
        s = s * cd_n[..., None, None] + jnp.einsum('bhck,bhcv->bhkv', kt_n, u)
        return s, o

    s_fin, o = lax.scan(step, s0, (w_val, k_cd, qk, q_dec, k_tail, c_decay))
    o = jnp.moveaxis(o, (0, 2), (1, 3)).reshape(b, t, h, dv)
    return o, s_fin


def gated_delta_recurrent(q, k, v, g, beta, s0):
    def step(s, xs):
        q_t, k_t, v_t, g_t, b_t = xs
        s = s * jnp.exp(g_t)[..., None, None]
        u = b_t[..., None] * (v_t - jnp.einsum('bhk,bhkv->bhv', k_t, s))
        s = s + jnp.einsum('bhk,bhv->bhkv', k_t, u)
        return s, jnp.einsum('bhk,bhkv->bhv', q_t, s)

    xs = tuple(jnp.moveaxis(a, 1, 0) for a in (q, k, v, g, beta))
    s_fin, o = lax.scan(step, s0, xs)
    return jnp.moveaxis(o, 0, 1), s_fin


def gated_deltanet(qkv, z, b_logit, a_logit, conv_buf, s0, conv_w, a_log, dt_bias, o_norm, chunked):
    bsz, t, _ = qkv.shape
    act, conv_new = causal_short_conv(qkv, conv_buf, conv_w)
    act = act.astype(jnp.float32)
    q, k, v = jnp.split(act, [DN_QK_W, 2 * DN_QK_W], axis=-1)
    q = l2norm(q.reshape(bsz, t, DN_HEADS, DN_DK)) * (DN_DK ** -0.5)
    k = l2norm(k.reshape(bsz, t, DN_HEADS, DN_DK))
    v = v.reshape(bsz, t, DN_HEADS, DN_DV)
    beta = jax.nn.sigmoid(b_logit.astype(jnp.float32))
    g = -jnp.exp(a_log.astype(jnp.float32)) * jax.nn.softplus(
        a_logit.astype(jnp.float32) + dt_bias.astype(jnp.float32))
    s0 = s0.astype(jnp.float32)
    if chunked:
        o, s_fin = gated_delta_chunked(q, k, v, g, beta, s0)
    else:
        o, s_fin = gated_delta_recurrent(q, k, v, g, beta, s0)
    o = o * lax.rsqrt(jnp.mean(o * o, axis=-1, keepdims=True) + EPS) * o_norm.astype(jnp.float32)
    o = o * jax.nn.silu(z.reshape(bsz, t, DN_HEADS, DN_DV).astype(jnp.float32))
    return o.reshape(bsz, t, DN_V_W).astype(qkv.dtype), s_fin, conv_new


def multiscale_pool(u_new, u_prev, start_pos, pool_w, pool_scale):
    bsz, t, c = u_new.shape
    p = u_prev.shape[1]
    u = jnp.concatenate([u_prev.astype(u_new.dtype), u_new], axis=1)
    cs = jnp.concatenate([jnp.zeros((bsz, 1, c), jnp.float32),
                          jnp.cumsum(u.astype(jnp.float32), axis=1)], axis=1)
    hi = p + 1 + jnp.arange(t)
    pos = start_pos + jnp.arange(t)
    outs = []
    for gi, w in enumerate(POOL_WINDOWS):
        sl = slice(gi * POOL_GW, (gi + 1) * POOL_GW)
        lo = jnp.maximum(hi - w, 0)
        cnt = jnp.minimum(pos + 1, w).astype(jnp.float32)
        mean = (cs[:, hi, sl] - cs[:, lo, sl]) / cnt[None, :, None]
        outs.append(mean - u_new[..., sl].astype(jnp.float32))
    d = jnp.stack(outs, axis=2)
    mixed = jnp.einsum('btgc,gcd->btgd', d, pool_w.astype(jnp.float32)).reshape(bsz, t, c)
    return (mixed * pool_scale.astype(jnp.float32)).astype(u_new.dtype), u[:, -POOL_BUF:]


def mem_kv(mem, norm_w, w_k, w_v):
    bsz, n, _ = mem.shape
    m = rmsnorm(mem, norm_w)
    return (m @ w_k).reshape(bsz, n, X_HEADS, X_DH), (m @ w_v).reshape(bsz, n, X_HEADS, X_DH)


def memory_attention(cq, mk, mv):
    bsz, t, _ = cq.shape
    q = cq.reshape(bsz, t, X_HEADS, X_DH)
    s = jnp.einsum('bthd,bmhd->bhtm', q, mk.astype(q.dtype)).astype(jnp.float32) * (X_DH ** -0.5)
    p = jax.nn.softmax(s, axis=-1).astype(q.dtype)
    o = jnp.einsum('bhtm,bmhd->bthd', p, mv.astype(q.dtype))
    return o.reshape(bsz, t, X_W)


def moe_swiglu(x, router_w, wi, wo):
    logits = (x @ router_w).astype(jnp.float32)
    top_v, top_i = lax.top_k(logits, TOP_K)
    gates = jax.nn.softmax(top_v, axis=-1)
    dispatch = jnp.sum(jax.nn.one_hot(top_i, N_EXPERTS, dtype=jnp.float32) * gates[..., None], axis=-2)
    y = jnp.zeros_like(x)
    for e in range(N_EXPERTS):
        y = y + dispatch[..., e:e + 1].astype(x.dtype) * swiglu(x, wi[e], wo[e])
    return y


def layer_a(x, mk, mv, s0, conv_buf, norm1, w_in, conv_w, a_log, dt_bias, o_norm, w_out,
            norm2, ffn_wi, ffn_wo, chunked):
    u = rmsnorm(x, norm1)
    proj = u @ w_in
    o1 = CONV_CH
    o2 = o1 + DN_V_W
    o3 = o2 + DN_HEADS
    o4 = o3 + DN_HEADS
    qkv, z, b_logit, a_logit, cq = jnp.split(proj, [o1, o2, o3, o4], axis=-1)
    dn, s_fin, conv_new = gated_deltanet(qkv, z, b_logit, a_logit, conv_buf, s0, conv_w,
                                         a_log, dt_bias, o_norm, chunked)
    xa = memory_attention(cq, mk, mv)
    h = x + jnp.concatenate([dn, xa], axis=-1) @ w_out
    h = h + swiglu(rmsnorm(h, norm2), ffn_wi, ffn_wo)
    return h, s_fin, conv_new


def layer_b(x, mk, mv, pool_prev, start_pos, norm1, w_in, pool_w, pool_scale, w_out,
            norm2, router_w, moe_wi, moe_wo):
    u = rmsnorm(x, norm1)
    proj = u @ w_in
    up, cq = jnp.split(proj, [POOL_W], axis=-1)
    pm, buf = multiscale_pool(up, pool_prev, start_pos, pool_w, pool_scale)
    xa = memory_attention(cq, mk, mv)
    h = x + jnp.concatenate([pm, xa], axis=-1) @ w_out
    h = h + moe_swiglu(rmsnorm(h, norm2), router_w, moe_wi, moe_wo)
    return h, buf


def setup_inputs(seed: int = 0) -> dict:
    key = jax.random.key(seed)
    ks = iter(jax.random.split(key, 40))

    def nrm(shape, scale):
        return jax.random.normal(next(ks), shape, jnp.float32) * scale

    def gain(shape):
        return 1.0 + nrm(shape, 0.1)

    na, nb = N_LAYERS_A, N_LAYERS_B
    return {
        'x_prompt': nrm((BATCH, SEQ, D_MODEL), 1.0),
        'x_sample': nrm((DEC_BATCH, DEC_SEQ, D_MODEL), 1.0),
        'state_dn': nrm((na, DEC_BATCH, DN_HEADS, DN_DK, DN_DV), 0.1),
        'state_dn_conv': nrm((na, DEC_BATCH, CONV_WIDTH - 1, CONV_CH), 1.0),
        'state_pool': nrm((nb, DEC_BATCH, POOL_BUF, POOL_W), 1.0),
        'cache_mem_k': nrm((DEPTH, DEC_BATCH, N_MEM, X_HEADS, X_DH), 1.0),
        'cache_mem_v': nrm((DEPTH, DEC_BATCH, N_MEM, X_HEADS, X_DH), 1.0),
        'mem_prompt': nrm((BATCH, N_MEM, D_MODEL), 1.0),
        'a_norm1': gain((na, D_MODEL)),
        'a_w_in': nrm((na, D_MODEL, A_IN_W), D_MODEL ** -0.5),
        'a_conv_w': nrm((na, CONV_WIDTH, CONV_CH), CONV_WIDTH ** -0.5),
        'a_a_log': jnp.log(jax.random.uniform(next(ks), (na, DN_HEADS), jnp.float32, 1.0, 16.0)),
        'a_dt_bias': -4.0 + nrm((na, DN_HEADS), 0.5),
        'a_o_norm': gain((na, DN_DV)),
        'a_w_out': nrm((na, A_MIX_W, D_MODEL), A_MIX_W ** -0.5),
        'a_norm2': gain((na, D_MODEL)),
        'a_ffn_wi': nrm((na, D_MODEL, 2 * D_FF), D_MODEL ** -0.5),
        'a_ffn_wo': nrm((na, D_FF, D_MODEL), D_FF ** -0.5),
        'b_norm1': gain((nb, D_MODEL)),
        'b_w_in': nrm((nb, D_MODEL, B_IN_W), D_MODEL ** -0.5),
        'b_pool_w': nrm((nb, POOL_GROUPS, POOL_GW, POOL_GW), POOL_GW ** -0.5),
        'b_pool_scale': gain((nb, POOL_W)),
        'b_w_out': nrm((nb, B_MIX_W, D_MODEL), B_MIX_W ** -0.5),
        'b_norm2': gain((nb, D_MODEL)),
        'b_router': nrm((nb, D_MODEL, N_EXPERTS), D_MODEL ** -0.5),
        'b_moe_wi': nrm((nb, N_EXPERTS, D_MODEL, 2 * D_FF_EXPERT), D_MODEL ** -0.5),
        'b_moe_wo': nrm((nb, N_EXPERTS, D_FF_EXPERT, D_MODEL), D_FF_EXPERT ** -0.5),
        'm_norm': gain((DEPTH, D_MODEL)),
        'm_w_k': nrm((DEPTH, D_MODEL, X_W), D_MODEL ** -0.5),
        'm_w_v': nrm((DEPTH, D_MODEL, X_W), D_MODEL ** -0.5),
        'final_norm': gain((D_MODEL,)),
    }


def reference(x_prompt, x_sample, state_dn, state_dn_conv, state_pool, cache_mem_k, cache_mem_v,
              mem_prompt, a_norm1, a_w_in, a_conv_w, a_a_log, a_dt_bias, a_o_norm, a_w_out,
              a_norm2, a_ffn_wi, a_ffn_wo, b_norm1, b_w_in, b_pool_w, b_pool_scale, b_w_out,
              b_norm2, b_router, b_moe_wi, b_moe_wo, m_norm, m_w_k, m_w_v, final_norm):
    bp = x_prompt.shape[0]
    hp, hs = x_prompt, x_sample
    dn_p, conv_p, pool_p, mk_p_all, mv_p_all = [], [], [], [], []
    dn_s, conv_s, pool_s = [], [], []
    for i in range(DEPTH):
        j = i // 2
        mk_p, mv_p = mem_kv(mem_prompt, m_norm[i], m_w_k[i], m_w_v[i])
        mk_p_all.append(mk_p)
        mv_p_all.append(mv_p)
        mk_s, mv_s = cache_mem_k[i], cache_mem_v[i]
        if i % 2 == 0:
            wa = (a_norm1[j], a_w_in[j], a_conv_w[j], a_a_log[j], a_dt_bias[j], a_o_norm[j],
                  a_w_out[j], a_norm2[j], a_ffn_wi[j], a_ffn_wo[j])
            s0_p = jnp.zeros((bp, DN_HEADS, DN_DK, DN_DV), jnp.float32)
            c0_p = jnp.zeros((bp, CONV_WIDTH - 1, CONV_CH), hp.dtype)
            hp, s_p, c_p = layer_a(hp, mk_p, mv_p, s0_p, c0_p, *wa, chunked=True)
            hs, s_s, c_s = layer_a(hs, mk_s, mv_s, state_dn[j], state_dn_conv[j], *wa, chunked=False)
            dn_p.append(s_p.astype(hp.dtype))
            conv_p.append(c_p)
            dn_s.append(s_s.astype(hs.dtype))
            conv_s.append(c_s)
        else:
            wb = (b_norm1[j], b_w_in[j], b_pool_w[j], b_pool_scale[j], b_w_out[j], b_norm2[j],
                  b_router[j], b_moe_wi[j], b_moe_wo[j])
            p0_p = jnp.zeros((bp, 0, POOL_W), hp.dtype)
            hp, buf_p = layer_b(hp, mk_p, mv_p, p0_p, 0, *wb)
            hs, buf_s = layer_b(hs, mk_s, mv_s, state_pool[j], PAST_LEN, *wb)
            pool_p.append(buf_p)
            pool_s.append(buf_s)
    y_prompt = rmsnorm(hp, final_norm)
    y_sample = rmsnorm(hs, final_norm)
    new_state_dn_prompt = jnp.stack(dn_p)
    new_state_dn_conv_prompt = jnp.stack(conv_p)
    new_state_pool_prompt = jnp.stack(pool_p)
    new_cache_mem_k_prompt = jnp.stack(mk_p_all)
    new_cache_mem_v_prompt = jnp.stack(mv_p_all)
    new_state_dn_sample = jnp.stack(dn_s)
    new_state_dn_conv_sample = jnp.stack(conv_s)
    new_state_pool_sample = jnp.stack(pool_s)
    return (y_prompt, y_sample, new_state_dn_prompt, new_state_dn_conv_prompt, new_state_pool_prompt,
            new_cache_mem_k_prompt, new_cache_mem_v_prompt, new_state_dn_sample,
            new_state_dn_conv_sample, new_state_pool_sample)
```

```python
import functools

import jax
import jax.numpy as jnp
from jax import lax
from jax.experimental import pallas as pl
from jax.experimental.pallas import tpu as pltpu

F32 = jnp.float32
BF16 = jnp.bfloat16
HIGHEST = lax.Precision.HIGHEST

EPS = 1e-6
CHUNK = 64
CONV_WIDTH = 4
POOL_WINDOWS = (2, 4, 8, 16)
PAST_LEN = 16384
TOP_K = 2

LANE = 128
SUBLANE = 8
VMEM_LIMIT_BYTES = 56 * 2**20

INV_BLOCK = 16
POOL_HALO = 32


def _params(*sem):
    return pltpu.CompilerParams(dimension_semantics=sem, vmem_limit_bytes=VMEM_LIMIT_BYTES)


def _resident(shape):
    nd = len(shape)
    return pl.BlockSpec(shape, lambda *_: (0,) * nd, pipeline_mode=pl.Buffered(1))


def _rms(x, w):
    return x * lax.rsqrt(jnp.mean(x * x, axis=-1, keepdims=True) + EPS) * w


def _silu(x):
    return x * jax.nn.sigmoid(x)


def _dot(a, b):
    return jnp.dot(a, b, preferred_element_type=F32)


def _wdot(a, w):
    if w.dtype == F32:
        return jnp.dot(a, w, precision=HIGHEST, preferred_element_type=F32)
    return jnp.dot(a.astype(BF16), w, preferred_element_type=F32)


def _bdot(a, b):
    return lax.dot_general(a, b, (((2,), (1,)), ((0,), (0,))), preferred_element_type=F32)


def _bdot_nt(a, b):
    return lax.dot_general(a, b, (((2,), (2,)), ((0,), (0,))), preferred_element_type=F32)


def _bdot_tn(a, b):
    return lax.dot_general(a, b, (((1,), (1,)), ((0,), (0,))), preferred_element_type=F32)


def _split_bf16(a):
    hi = a.astype(BF16)
    lo = (a - hi.astype(F32)).astype(BF16)
    return hi, lo


def _bdot3(a, b):
    ah, al = _split_bf16(a)
    bh, bl = _split_bf16(b)
    return _bdot(ah, bh) + (_bdot(ah, bl) + _bdot(al, bh))


def _rms_matmul_body(x_ref, g_ref, w_ref, o_ref):
    xn = _rms(x_ref[...], g_ref[...])
    o_ref[...] = _wdot(xn, w_ref[...])


def _wspec(shape, index_map, steps):
    if steps == 1:
        return pl.BlockSpec(shape, index_map, pipeline_mode=pl.Buffered(1))
    return pl.BlockSpec(shape, index_map)


def rms_matmul(x, g, w, tm, name, tn=None):
    m, d = x.shape
    n = w.shape[1]
    tn = n if tn is None else tn
    nj = n // tn
    return pl.pallas_call(
        _rms_matmul_body,
        out_shape=jax.ShapeDtypeStruct((m, n), F32),
        grid=(m // tm, nj),
        in_specs=[pl.BlockSpec((tm, d), lambda i, j: (i, 0)), _resident((1, d)),
                  _wspec((d, tn), lambda i, j: (0, j), nj)],
        out_specs=pl.BlockSpec((tm, tn), lambda i, j: (i, j)),
        compiler_params=_params("parallel", "arbitrary"),
        name=name,
    )(x, g.reshape(1, d), w)


def _out_proj_body(x_ref, a1_ref, a2_ref, w1_ref, w2_ref, o_ref):
    acc = _wdot(a1_ref[...], w1_ref[...]) + _wdot(a2_ref[...], w2_ref[...])
    o_ref[...] = x_ref[...] + acc


def out_proj(x, a1, a2, w, tm, name):
    m, d = x.shape
    k1, k2 = a1.shape[1], a2.shape[1]
    assert w.shape[0] == k1 + k2 and k1 % k2 == 0
    return pl.pallas_call(
        _out_proj_body,
        out_shape=jax.ShapeDtypeStruct((m, d), F32),
        grid=(m // tm,),
        in_specs=[
            pl.BlockSpec((tm, d), lambda i: (i, 0)),
            pl.BlockSpec((tm, k1), lambda i: (i, 0)),
            pl.BlockSpec((tm, k2), lambda i: (i, 0)),
            pl.BlockSpec((k1, d), lambda i: (0, 0), pipeline_mode=pl.Buffered(1)),
            pl.BlockSpec((k2, d), lambda i: (k1 // k2, 0), pipeline_mode=pl.Buffered(1)),
        ],
        out_specs=pl.BlockSpec((tm, d), lambda i: (i, 0)),
        compiler_params=_params("parallel"),
        name=name,
    )(x, a1, a2, w, w)


def _ffn_body(h_ref, g_ref, wg_ref, wu_ref, wo_ref, o_ref, xn_ref):
    @pl.when(pl.program_id(1) == 0)
    def _():
        h = h_ref[...]
        xn_ref[...] = _rms(h, g_ref[...])
        o_ref[...] = h

    xn = xn_ref[...]
    a = _wdot(xn, wg_ref[...])
    b = _wdot(xn, wu_ref[...])
    o_ref[...] += _wdot(_silu(a) * b, wo_ref[...])


def ffn_dense(h, g, wi, wo, tm, name, tf=None):
    m, d = h.shape
    f = wo.shape[0]
    tf = f if tf is None else tf
    nj = f // tf
    return pl.pallas_call(
        _ffn_body,
        out_shape=jax.ShapeDtypeStruct((m, d), F32),
        grid=(m // tm, nj),
        in_specs=[
            pl.BlockSpec((tm, d), lambda i, j: (i, 0)),
            _resident((1, d)),
            _wspec((d, tf), lambda i, j: (0, j), nj),
            _wspec((d, tf), lambda i, j: (0, nj + j), nj),
            _wspec((tf, d), lambda i, j: (j, 0), nj),
        ],
        out_specs=pl.BlockSpec((tm, d), lambda i, j: (i, 0)),
        scratch_shapes=[pltpu.VMEM((tm, d), F32)],
        compiler_params=_params("parallel", "arbitrary"),
        name=name,
    )(h, g.reshape(1, d), wi, wi, wo)


def _unit_lower_inverse(mx, eye, blk):
    md = jnp.where(blk, mx, 0.0)
    c = mx - md
    p = eye - md
    m2 = _bdot3(md, md)
    p = p + _bdot3(p, m2)
    m4 = _bdot3(m2, m2)
    p = p + _bdot3(p, m4)
    m8 = _bdot3(m4, m4)
    td = p + _bdot3(p, m8)
    n = _bdot3(td, c)
    n2 = _bdot3(n, n)
    q = eye - n
    q = q + _bdot3(q, n2)
    return _bdot3(q, td)


def _dn_prompt_body(qkv_ref, z_ref, bl_ref, al_ref, cw_ref, alog_ref, dtb_ref, on_ref,
                    dn_ref, sfin_ref, s_ref, xbuf_ref, act_ref, *, rows, heads, dk, dv):
    c = pl.program_id(1)
    wq = heads * dk

    @pl.when(c == 0)
    def _():
        s_ref[...] = jnp.zeros_like(s_ref)
        xbuf_ref[0:SUBLANE, :] = jnp.zeros((SUBLANE, xbuf_ref.shape[1]), F32)

    xbuf_ref[SUBLANE:SUBLANE + rows, :] = qkv_ref[...]
    first = SUBLANE - (CONV_WIDTH - 1)
    y = xbuf_ref[pl.ds(first, rows), :] * cw_ref[0:1, :]
    for j in range(1, CONV_WIDTH):
        y = y + xbuf_ref[pl.ds(first + j, rows), :] * cw_ref[j:j + 1, :]
    act_ref[...] = _silu(y)
    xbuf_ref[0:SUBLANE, :] = xbuf_ref[rows:rows + SUBLANE, :]

    ri = lax.broadcasted_iota(jnp.int32, (CHUNK, CHUNK), 0)
    cj = lax.broadcasted_iota(jnp.int32, (CHUNK, CHUNK), 1)
    incl = (ri >= cj)[None]
    strict = (ri > cj)[None]
    blk = ((ri // INV_BLOCK) == (cj // INV_BLOCK))[None]
    tri = (ri >= cj).astype(F32)
    eye = (ri == cj).astype(F32)[None]

    def per_head(fn):
        return jnp.stack([fn(h) for h in range(heads)], axis=0)

    for cc in range(rows // CHUNK):
        r0 = cc * CHUNK
        rs = slice(r0, r0 + CHUNK)
        q3 = per_head(lambda h: act_ref[rs, h * dk:(h + 1) * dk])
        k3 = per_head(lambda h: act_ref[rs, wq + h * dk:wq + (h + 1) * dk])
        v3 = per_head(lambda h: act_ref[rs, 2 * wq + h * dv:2 * wq + (h + 1) * dv])
        qn = q3 * lax.rsqrt(jnp.sum(q3 * q3, axis=-1, keepdims=True) + EPS) * (dk ** -0.5)
        kn = k3 * lax.rsqrt(jnp.sum(k3 * k3, axis=-1, keepdims=True) + EPS)

        beta = jax.nn.sigmoid(bl_ref[rs, :])
        g = -jnp.exp(alog_ref[...]) * jax.nn.softplus(al_ref[rs, :] + dtb_ref[...])
        gcum = jnp.dot(tri, g, precision=HIGHEST, preferred_element_type=F32)
        gcum_t = gcum.T
        gc = per_head(lambda h: gcum[:, h:h + 1])
        gr = per_head(lambda h: gcum_t[h:h + 1, :])
        bc = per_head(lambda h: beta[:, h:h + 1])
        decay = jnp.where(incl, jnp.exp(gc - gr), 0.0)

        kb = kn.astype(BF16)
        qb = qn.astype(BF16)
        kk = _bdot_nt(kb, kb)
        mx = jnp.where(strict, bc * kk * decay, 0.0)
        tinv = _unit_lower_inverse(mx, eye, blk)
        eg = jnp.exp(gc)
        rhs = jnp.concatenate([v3 * bc, kn * (bc * eg)], axis=-1)
        sol = _bdot3(tinv, rhs)
        w_val = sol[..., :dv]
        k_cd = sol[..., dv:]
        qk = _bdot_nt(qb, kb) * decay
        q_dec = qn * eg
        g_last = gc[:, CHUNK - 1:CHUNK, :]
        k_tail = kn * jnp.exp(g_last - gc)
        c_decay = jnp.exp(g_last)

        s = s_ref[...]
        sb = s.astype(BF16)
        u = w_val - _bdot(k_cd.astype(BF16), sb)
        ub = u.astype(BF16)
        o = _bdot(q_dec.astype(BF16), sb) + _bdot(qk.astype(BF16), ub)
        s_ref[...] = s * c_decay + _bdot_tn(k_tail.astype(BF16), ub)

        o = o * lax.rsqrt(jnp.mean(o * o, axis=-1, keepdims=True) + EPS) * on_ref[...]
        z3 = per_head(lambda h: z_ref[rs, h * dv:(h + 1) * dv])
        o = o * _silu(z3)
        for h in range(heads):
            dn_ref[rs, h * dv:(h + 1) * dv] = o[h]

    @pl.when(c == pl.num_programs(1) - 1)
    def _():
        sfin_ref[0] = s_ref[...]


def deltanet_prompt(proj, col, conv_w, a_log, dt_bias, o_norm, batch, seq, heads, dk, dv, rows):
    conv_ch = 2 * heads * dk + heads * dv
    zw = heads * dv
    steps = seq // rows

    def row_map(width_off):
        return lambda b, c: (b * steps + c, width_off)

    body = functools.partial(_dn_prompt_body, rows=rows, heads=heads, dk=dk, dv=dv)
    return pl.pallas_call(
        body,
        out_shape=(jax.ShapeDtypeStruct((batch * seq, zw), F32),
                   jax.ShapeDtypeStruct((batch, heads, dk, dv), F32)),
        grid=(batch, steps),
        in_specs=[
            pl.BlockSpec((rows, conv_ch), row_map(col["qkv"] // conv_ch)),
            pl.BlockSpec((rows, zw), row_map(col["z"] // zw)),
            pl.BlockSpec((rows, LANE), row_map(col["b"] // LANE)),
            pl.BlockSpec((rows, LANE), row_map(col["a"] // LANE)),
            _resident((CONV_WIDTH, conv_ch)),
            _resident((1, LANE)),
            _resident((1, LANE)),
            _resident((1, dv)),
        ],
        out_specs=(pl.BlockSpec((rows, zw), lambda b, c: (b * steps + c, 0)),
                   pl.BlockSpec((1, heads, dk, dv), lambda b, c: (b, 0, 0, 0))),
        scratch_shapes=[
            pltpu.VMEM((heads, dk, dv), F32),
            pltpu.VMEM((rows + SUBLANE, conv_ch), F32),
            pltpu.VMEM((rows, conv_ch), F32),
        ],
        compiler_params=_params("parallel", "arbitrary"),
        name="deltanet_prompt",
    )(proj, proj, proj, proj, conv_w, a_log, dt_bias, o_norm)


def _dn_sample_body(qkv_ref, z_ref, bl_ref, al_ref, cbuf_ref, st_ref, cw_ref, alog_ref, dtb_ref, on_ref,
                    dn_ref, sto_ref, cbo_ref, act_ref, beta_ref, dec_ref, *, nb, heads, dk, dv):
    wq = heads * dk
    new = qkv_ref[...]
    y = cbuf_ref[:, 0, :] * cw_ref[0:1, :]
    for j in range(1, CONV_WIDTH - 1):
        y = y + cbuf_ref[:, j, :] * cw_ref[j:j + 1, :]
    y = y + new * cw_ref[CONV_WIDTH - 1:CONV_WIDTH, :]
    act_ref[...] = _silu(y)
    for j in range(CONV_WIDTH - 2):
        cbo_ref[:, j, :] = cbuf_ref[:, j + 1, :]
    cbo_ref[:, CONV_WIDTH - 2, :] = new

    beta_ref[...] = jax.nn.sigmoid(bl_ref[...])
    g = -jnp.exp(alog_ref[...]) * jax.nn.softplus(al_ref[...] + dtb_ref[...])
    dec_ref[...] = jnp.exp(g)

    ri = lax.broadcasted_iota(jnp.int32, (dk, dk), 0)
    cj = lax.broadcasted_iota(jnp.int32, (dk, dk), 1)
    diag = ri == cj

    def to_col(row):
        return jnp.sum(jnp.where(diag, jnp.broadcast_to(row, (dk, dk)), 0.0), axis=1, keepdims=True)

    for i in range(nb):
        row = slice(i, i + 1)
        for h in range(heads):
            q = act_ref[row, h * dk:(h + 1) * dk]
            k = act_ref[row, wq + h * dk:wq + (h + 1) * dk]
            v = act_ref[row, 2 * wq + h * dv:2 * wq + (h + 1) * dv]
            qn = q * lax.rsqrt(jnp.sum(q * q, axis=-1, keepdims=True) + EPS) * (dk ** -0.5)
            kn = k * lax.rsqrt(jnp.sum(k * k, axis=-1, keepdims=True) + EPS)
            b_ih = beta_ref[row, h:h + 1]
            d_ih = dec_ref[row, h:h + 1]
            kcol = to_col(kn)
            qcol = to_col(qn)
            s = st_ref[i, h] * d_ih
            u = b_ih * (v - jnp.sum(kcol * s, axis=0, keepdims=True))
            s = s + kcol * u
            o = jnp.sum(qcol * s, axis=0, keepdims=True)
            sto_ref[i, h] = s
            o = o * lax.rsqrt(jnp.mean(o * o, axis=-1, keepdims=True) + EPS) * on_ref[...]
            dn_ref[row, h * dv:(h + 1) * dv] = o * _silu(z_ref[row, h * dv:(h + 1) * dv])


def deltanet_sample(proj, col, conv_buf, state, conv_w, a_log, dt_bias, o_norm, heads, dk, dv, nb):
    n_seq = proj.shape[0]
    conv_ch = 2 * heads * dk + heads * dv
    zw = heads * dv
    body = functools.partial(_dn_sample_body, nb=nb, heads=heads, dk=dk, dv=dv)
    return pl.pallas_call(
        body,
        out_shape=(jax.ShapeDtypeStruct((n_seq, zw), F32),
                   jax.ShapeDtypeStruct(state.shape, F32),
                   jax.ShapeDtypeStruct(conv_buf.shape, F32)),
        grid=(n_seq // nb,),
        in_specs=[
            pl.BlockSpec((nb, conv_ch), lambda i: (i, col["qkv"] // conv_ch)),
            pl.BlockSpec((nb, zw), lambda i: (i, col["z"] // zw)),
            pl.BlockSpec((nb, LANE), lambda i: (i, col["b"] // LANE)),
            pl.BlockSpec((nb, LANE), lambda i: (i, col["a"] // LANE)),
            pl.BlockSpec((nb, CONV_WIDTH - 1, conv_ch), lambda i: (i, 0, 0)),
            pl.BlockSpec((nb, heads, dk, dv), lambda i: (i, 0, 0, 0)),
            _resident((CONV_WIDTH, conv_ch)),
            _resident((1, LANE)),
            _resident((1, LANE)),
            _resident((1, dv)),
        ],
        out_specs=(pl.BlockSpec((nb, zw), lambda i: (i, 0)),
                   pl.BlockSpec((nb, heads, dk, dv), lambda i: (i, 0, 0, 0)),
                   pl.BlockSpec((nb, CONV_WIDTH - 1, conv_ch), lambda i: (i, 0, 0))),
        scratch_shapes=[
            pltpu.VMEM((nb, conv_ch), F32),
            pltpu.VMEM((nb, LANE), F32),
            pltpu.VMEM((nb, LANE), F32),
        ],
        compiler_params=_params("parallel"),
        name="deltanet_sample",
    )(proj, proj, proj, proj, conv_buf, state, conv_w, a_log, dt_bias, o_norm)


def _xattn_prompt_body(q_ref, k_ref, v_ref, o_ref, *, heads, dh):
    for h in range(heads):
        sl = slice(h * dh, (h + 1) * dh)
        q = q_ref[:, sl].astype(BF16)
        k = k_ref[:, sl].astype(BF16)
        v = v_ref[:, sl].astype(BF16)
        s = lax.dot_general(q, k, (((1,), (1,)), ((), ())), preferred_element_type=F32) * (dh ** -0.5)
        e = jnp.exp(s - jnp.max(s, axis=-1, keepdims=True))
        p = e / jnp.sum(e, axis=-1, keepdims=True)
        o_ref[:, sl] = _dot(p.astype(BF16), v)


def xattn_prompt(proj, cq_col, kv, batch, seq, n_mem, heads, dh, tq):
    xw = heads * dh
    steps = seq // tq
    body = functools.partial(_xattn_prompt_body, heads=heads, dh=dh)
    return pl.pallas_call(
        body,
        out_shape=jax.ShapeDtypeStruct((batch * seq, xw), F32),
        grid=(batch, steps),
        in_specs=[
            pl.BlockSpec((tq, xw), lambda b, i: (b * steps + i, cq_col // xw)),
            pl.BlockSpec((n_mem, xw), lambda b, i: (b, 0)),
            pl.BlockSpec((n_mem, xw), lambda b, i: (b, 1)),
        ],
        out_specs=pl.BlockSpec((tq, xw), lambda b, i: (b * steps + i, 0)),
        compiler_params=_params("parallel", "parallel"),
        name="xattn_prompt",
    )(proj, kv, kv)


def _xattn_sample_body(q_ref, k_ref, v_ref, o_ref, *, nb, heads, dh):
    for i in range(nb):
        row = slice(i, i + 1)
        for h in range(heads):
            sl = slice(h * dh, (h + 1) * dh)
            q = q_ref[row, sl]
            k = k_ref[i, :, sl]
            v = v_ref[i, :, sl]
            s = jnp.sum(k * q, axis=1, keepdims=True) * (dh ** -0.5)
            e = jnp.exp(s - jnp.max(s, axis=0, keepdims=True))
            p = e / jnp.sum(e, axis=0, keepdims=True)
            o_ref[row, sl] = jnp.sum(p * v, axis=0, keepdims=True)


def xattn_sample(proj, cq_col, cache_k, cache_v, heads, dh, nb):
    n_seq, n_mem = cache_k.shape[0], cache_k.shape[1]
    xw = heads * dh
    body = functools.partial(_xattn_sample_body, nb=nb, heads=heads, dh=dh)
    return pl.pallas_call(
        body,
        out_shape=jax.ShapeDtypeStruct((n_seq, xw), F32),
        grid=(n_seq // nb,),
        in_specs=[
            pl.BlockSpec((nb, xw), lambda i: (i, cq_col // xw)),
            pl.BlockSpec((nb, n_mem, xw), lambda i: (i, 0, 0)),
            pl.BlockSpec((nb, n_mem, xw), lambda i: (i, 0, 0)),
        ],
        out_specs=pl.BlockSpec((nb, xw), lambda i: (i, 0)),
        compiler_params=_params("parallel"),
        name="xattn_sample",
    )(proj, cache_k.reshape(n_seq, n_mem, xw), cache_v.reshape(n_seq, n_mem, xw))


def _pool_mix(sums, cnts, u_of, pw_ref, ps_ref, o_ref, gw):
    for gi in range(len(POOL_WINDOWS)):
        cols = slice(gi * gw, (gi + 1) * gw)
        d = sums[gi] / cnts[gi] - u_of(cols)
        o_ref[:, cols] = _wdot(d, pw_ref[gi]) * ps_ref[:, cols]


def _pool_prompt_body(u_ref, pw_ref, ps_ref, o_ref, xbuf, s2, s4, s8, *, tt, gw):
    t = pl.program_id(1)
    n = POOL_HALO + tt

    @pl.when(t == 0)
    def _():
        xbuf[0:POOL_HALO, :] = jnp.zeros((POOL_HALO, xbuf.shape[1]), F32)

    xbuf[POOL_HALO:n, :] = u_ref[...]
    s2[8:n, :] = xbuf[8:n, :] + xbuf[7:n - 1, :]
    s4[16:n, :] = s2[16:n, gw:] + s2[14:n - 2, gw:]
    s8[24:n, :] = s4[24:n, gw:] + s4[20:n - 4, gw:]
    s16 = s8[32:n, gw:] + s8[24:n - 8, gw:]
    sums = [s2[POOL_HALO:n, 0:gw], s4[POOL_HALO:n, 0:gw], s8[POOL_HALO:n, 0:gw], s16]
    pos = t * tt + lax.broadcasted_iota(jnp.int32, (tt, 1), 0)
    cnts = [jnp.minimum(pos + 1, w).astype(F32) for w in POOL_WINDOWS]
    _pool_mix(sums, cnts, lambda cols: u_ref[:, cols], pw_ref, ps_ref, o_ref, gw)
    xbuf[0:POOL_HALO, :] = xbuf[tt:n, :]


def pool_prompt(proj, pool_w, pool_scale, batch, seq, tt):
    pw_total = pool_scale.shape[-1]
    gw = pw_total // len(POOL_WINDOWS)
    steps = seq // tt
    n = POOL_HALO + tt
    body = functools.partial(_pool_prompt_body, tt=tt, gw=gw)
    return pl.pallas_call(
        body,
        out_shape=jax.ShapeDtypeStruct((batch * seq, pw_total), F32),
        grid=(batch, steps),
        in_specs=[
            pl.BlockSpec((tt, pw_total), lambda b, t: (b * steps + t, 0)),
            _resident(pool_w.shape),
            _resident((1, pw_total)),
        ],
        out_specs=pl.BlockSpec((tt, pw_total), lambda b, t: (b * steps + t, 0)),
        scratch_shapes=[
            pltpu.VMEM((n, pw_total), F32),
            pltpu.VMEM((n, pw_total), F32),
            pltpu.VMEM((n, pw_total - gw), F32),
            pltpu.VMEM((n, pw_total - 2 * gw), F32),
        ],
        compiler_params=_params("parallel", "arbitrary"),
        name="pool_prompt",
    )(proj, pool_w, pool_scale)


def _pool_sample_body(u_ref, st_ref, pw_ref, ps_ref, o_ref, sto_ref, *, gw, buf):
    pw_total = gw * len(POOL_WINDOWS)
    new = u_ref[:, 0:pw_total]
    sums, cnts = [], []
    for gi, w in enumerate(POOL_WINDOWS):
        cols = slice(gi * gw, (gi + 1) * gw)
        s = new[:, cols]
        for i in range(1, w):
            s = s + st_ref[:, buf - i, cols]
        sums.append(s)
        cnts.append(float(min(PAST_LEN + 1, w)))
    _pool_mix(sums, cnts, lambda cols: new[:, cols], pw_ref, ps_ref, o_ref, gw)
    sto_ref[:, 0:buf - 1, :] = st_ref[:, 1:buf, :]
    sto_ref[:, buf - 1, :] = new


def pool_sample(proj, state, pool_w, pool_scale):
    n_seq, buf, pw_total = state.shape
    gw = pw_total // len(POOL_WINDOWS)
    body = functools.partial(_pool_sample_body, gw=gw, buf=buf)
    return pl.pallas_call(
        body,
        out_shape=(jax.ShapeDtypeStruct((n_seq, pw_total), F32),
                   jax.ShapeDtypeStruct(state.shape, F32)),
        compiler_params=pltpu.CompilerParams(vmem_limit_bytes=VMEM_LIMIT_BYTES),
        name="pool_sample",
    )(proj, state, pool_w, pool_scale)


def _moe_body(h_ref, g_ref, r_ref, wg_ref, wu_ref, wo_ref, fn_ref, o_ref,
              xn_ref, disp_ref, acc_ref, tot_ref, *, n_experts, final_norm):
    e = pl.program_id(1)
    j = pl.program_id(2)
    nj = pl.num_programs(2)
    lane = lax.broadcasted_iota(jnp.int32, disp_ref.shape, 1)

    @pl.when((e == 0) & (j == 0))
    def _():
        xn = _rms(h_ref[...], g_ref[...])
        xn_ref[...] = xn.astype(BF16)
        logits = jnp.dot(xn, r_ref[...], precision=HIGHEST, preferred_element_type=F32)
        lm = jnp.where(lane < n_experts, logits, -jnp.inf)
        m1 = jnp.max(lm, axis=1, keepdims=True)
        i1 = jnp.min(jnp.where(lm == m1, lane, LANE), axis=1, keepdims=True)
        lm2 = jnp.where(lane == i1, -jnp.inf, lm)
        m2 = jnp.max(lm2, axis=1, keepdims=True)
        i2 = jnp.min(jnp.where(lm2 == m2, lane, LANE), axis=1, keepdims=True)
        e2 = jnp.exp(m2 - m1)
        den = 1.0 + e2
        disp_ref[...] = jnp.where(lane == i1, 1.0 / den, 0.0) + jnp.where(lane == i2, e2 / den, 0.0)
        tot_ref[...] = jnp.zeros_like(tot_ref)

    xn = xn_ref[...]
    a = _dot(xn, wg_ref[0].astype(BF16))
    b = _dot(xn, wu_ref[0].astype(BF16))
    part = _dot((_silu(a) * b).astype(BF16), wo_ref[0].astype(BF16))

    @pl.when(j == 0)
    def _():
        acc_ref[...] = part

    @pl.when(j > 0)
    def _():
        acc_ref[...] += part

    @pl.when(j == nj - 1)
    def _():
        dcol = jnp.sum(jnp.where(lane == e, disp_ref[...], 0.0), axis=1, keepdims=True)
        tot_ref[...] += dcol * acc_ref[...]

    @pl.when((e == n_experts - 1) & (j == nj - 1))
    def _():
        out = h_ref[...] + tot_ref[...]
        if final_norm:
            out = _rms(out, fn_ref[...])
        o_ref[...] = out


def moe_block(h, g, router, wi, wo, fn, tm, tf, final_norm, name):
    m, d = h.shape
    n_experts, f = wo.shape[0], wo.shape[1]
    nj = f // tf
    r_pad = jnp.zeros((d, LANE), F32).at[:, :n_experts].set(router)
    body = functools.partial(_moe_body, n_experts=n_experts, final_norm=final_norm)
    return pl.pallas_call(
        body,
        out_shape=jax.ShapeDtypeStruct((m, d), F32),
        grid=(m // tm, n_experts, nj),
        in_specs=[
            pl.BlockSpec((tm, d), lambda i, e, j: (i, 0)),
            _resident((1, d)),
            _resident((d, LANE)),
            pl.BlockSpec((1, d, tf), lambda i, e, j: (e, 0, j)),
            pl.BlockSpec((1, d, tf), lambda i, e, j: (e, 0, nj + j)),
            pl.BlockSpec((1, tf, d), lambda i, e, j: (e, j, 0)),
            _resident((1, d)),
        ],
        out_specs=pl.BlockSpec((tm, d), lambda i, e, j: (i, 0)),
        scratch_shapes=[
            pltpu.VMEM((tm, d), BF16),
            pltpu.VMEM((tm, LANE), F32),
            pltpu.VMEM((tm, d), F32),
            pltpu.VMEM((tm, d), F32),
        ],
        compiler_params=_params("parallel", "arbitrary", "arbitrary"),
        name=name,
    )(h, g.reshape(1, d), r_pad, wi, wi, wo, fn.reshape(1, d))


def _pad_lanes(v):
    return jnp.zeros((1, LANE), F32).at[0, :v.shape[0]].set(v)


def kernel(x_prompt, x_sample, state_dn, state_dn_conv, state_pool, cache_mem_k, cache_mem_v, mem_prompt, a_norm1, a_w_in, a_conv_w, a_a_log, a_dt_bias, a_o_norm, a_w_out, a_norm2, a_ffn_wi, a_ffn_wo, b_norm1, b_w_in, b_pool_w, b_pool_scale, b_w_out, b_norm2, b_router, b_moe_wi, b_moe_wo, m_norm, m_w_k, m_w_v, final_norm):
    bp, seq, d = x_prompt.shape
    bs = x_sample.shape[0]
    assert x_sample.shape[1] == 1
    depth = m_norm.shape[0]
    assert depth % 2 == 0, "the output norm is fused into the expert block of the last (odd) layer"
    heads, dk, dv = state_dn.shape[2], state_dn.shape[3], state_dn.shape[4]
    conv_ch = state_dn_conv.shape[-1]
    qk_w, v_w = heads * dk, heads * dv
    assert conv_ch == 2 * qk_w + v_w and dk == LANE and dv == LANE and heads <= LANE
    n_mem, x_heads, x_dh = cache_mem_k.shape[2], cache_mem_k.shape[3], cache_mem_k.shape[4]
    xw = x_heads * x_dh
    pool_w_total = state_pool.shape[-1]
    pool_buf = state_pool.shape[2]
    d_ff = a_ffn_wo.shape[1]

    hp = x_prompt.reshape(bp * seq, d)
    hs = x_sample.reshape(bs, d)

    o1, o2, o3, o4 = conv_ch, conv_ch + v_w, conv_ch + v_w + heads, conv_ch + v_w + 2 * heads
    col_a = {"qkv": 0, "z": conv_ch, "cq": conv_ch + v_w, "b": conv_ch + v_w + xw, "a": conv_ch + v_w + xw + LANE}
    n_a = -(-(col_a["a"] + LANE) // 512) * 512

    dn_p, conv_p, pool_p, mk_all, mv_all = [], [], [], [], []
    dn_s, conv_s, pool_s = [], [], []
    for i in range(depth):
        j = i // 2
        w_kv = jnp.concatenate([m_w_k[i], m_w_v[i]], axis=1).astype(BF16)
        kv = rms_matmul(mem_prompt.reshape(bp * n_mem, d), m_norm[i], w_kv, 256, f"mem_kv_{i}")
        mk_all.append(kv[:, :xw].reshape(bp, n_mem, x_heads, x_dh))
        mv_all.append(kv[:, xw:].reshape(bp, n_mem, x_heads, x_dh))
        if i % 2 == 0:
            w = a_w_in[j]
            w_in = jnp.zeros((d, n_a), F32)
            w_in = w_in.at[:, :o2].set(w[:, :o2])
            w_in = w_in.at[:, col_a["cq"]:col_a["cq"] + xw].set(w[:, o4:])
            w_in = w_in.at[:, col_a["b"]:col_a["b"] + heads].set(w[:, o2:o3])
            w_in = w_in.at[:, col_a["a"]:col_a["a"] + heads].set(w[:, o3:o4])
            a_log, dt_b = _pad_lanes(a_a_log[j]), _pad_lanes(a_dt_bias[j])
            o_norm = a_o_norm[j].reshape(1, dv)

            proj_p = rms_matmul(hp, a_norm1[j], w_in.astype(BF16), 256, f"a_in_p_{i}")
            proj_s = rms_matmul(hs, a_norm1[j], w_in, bs, f"a_in_s_{i}", tn=512)
            mix_p, s_p = deltanet_prompt(proj_p, col_a, a_conv_w[j], a_log, dt_b, o_norm,
                                         bp, seq, heads, dk, dv, rows=CHUNK)
            mix_s, s_s, c_s = deltanet_sample(proj_s, col_a, state_dn_conv[j], state_dn[j], a_conv_w[j],
                                              a_log, dt_b, o_norm, heads, dk, dv, nb=SUBLANE)
            xa_p = xattn_prompt(proj_p, col_a["cq"], kv, bp, seq, n_mem, x_heads, x_dh, tq=512)
            xa_s = xattn_sample(proj_s, col_a["cq"], cache_mem_k[i], cache_mem_v[i], x_heads, x_dh, nb=SUBLANE)
            hp = out_proj(hp, mix_p, xa_p, a_w_out[j].astype(BF16), 512, f"a_out_p_{i}")
            hs = out_proj(hs, mix_s, xa_s, a_w_out[j], bs, f"a_out_s_{i}")
            hp = ffn_dense(hp, a_norm2[j], a_ffn_wi[j].astype(BF16), a_ffn_wo[j].astype(BF16), 256,
                           f"a_ffn_p_{i}")
            hs = ffn_dense(hs, a_norm2[j], a_ffn_wi[j], a_ffn_wo[j], bs, f"a_ffn_s_{i}", tf=256)
            dn_p.append(s_p)
            conv_p.append(proj_p.reshape(bp, seq, n_a)[:, seq - (CONV_WIDTH - 1):, :conv_ch])
            dn_s.append(s_s)
            conv_s.append(c_s)
        else:
            ps = b_pool_scale[j].reshape(1, pool_w_total)
            last = i == depth - 1

            proj_p = rms_matmul(hp, b_norm1[j], b_w_in[j].astype(BF16), 512, f"b_in_p_{i}")
            proj_s = rms_matmul(hs, b_norm1[j], b_w_in[j], bs, f"b_in_s_{i}")
            mix_p = pool_prompt(proj_p, b_pool_w[j].astype(BF16), ps, bp, seq, tt=512)
            mix_s, buf_s = pool_sample(proj_s, state_pool[j], b_pool_w[j], ps)
            xa_p = xattn_prompt(proj_p, pool_w_total, kv, bp, seq, n_mem, x_heads, x_dh, tq=512)
            xa_s = xattn_sample(proj_s, pool_w_total, cache_mem_k[i], cache_mem_v[i], x_heads, x_dh, nb=SUBLANE)
            hp = out_proj(hp, mix_p, xa_p, b_w_out[j].astype(BF16), 512, f"b_out_p_{i}")
            hs = out_proj(hs, mix_s, xa_s, b_w_out[j], bs, f"b_out_s_{i}")
            hp = moe_block(hp, b_norm2[j], b_router[j], b_moe_wi[j], b_moe_wo[j], final_norm,
                           1024, 512, last, f"b_moe_p_{i}")
            hs = moe_block(hs, b_norm2[j], b_router[j], b_moe_wi[j], b_moe_wo[j], final_norm,
                           bs, 512, last, f"b_moe_s_{i}")
            pool_p.append(proj_p.reshape(bp, seq, -1)[:, seq - pool_buf:, :pool_w_total])
            pool_s.append(buf_s)

    y_prompt = hp.reshape(bp, seq, d)
    y_sample = hs.reshape(bs, 1, d)
    return (y_prompt, y_sample, jnp.stack(dn_p), jnp.stack(conv_p), jnp.stack(pool_p),
            jnp.stack(mk_all), jnp.stack(mv_all), jnp.stack(dn_s), jnp.stack(conv_s), jnp.stack(pool_s))
```

```python
import functools

import jax
import jax.numpy as jnp
from jax import lax
from jax.experimental import pallas as pl
from jax.experimental.pallas import tpu as pltpu

F32 = jnp.float32
BF16 = jnp.bfloat16
HIGHEST = lax.Precision.HIGHEST

EPS = 1e-6
CHUNK = 64
CONV_WIDTH = 4
POOL_WINDOWS = (2, 4, 8, 16)
PAST_LEN = 16384
TOP_K = 2

LANE = 128
SUBLANE = 8
VMEM_LIMIT_BYTES = 56 * 2**20

MOE_TILE = 1024
INV_BLOCK = 16
POOL_HALO = 32


def _params(*sem):
    return pltpu.CompilerParams(dimension_semantics=sem, vmem_limit_bytes=VMEM_LIMIT_BYTES)


def _resident(shape):
    nd = len(shape)
    return pl.BlockSpec(shape, lambda *_: (0,) * nd, pipeline_mode=pl.Buffered(1))


def _rms(x, w):
    return x * lax.rsqrt(jnp.mean(x * x, axis=-1, keepdims=True) + EPS) * w


def _silu(x):
    return x * jax.nn.sigmoid(x)


def _dot(a, b):
    return jnp.dot(a, b, preferred_element_type=F32)


def _wdot(a, w):
    if w.dtype == F32:
        return jnp.dot(a, w, precision=HIGHEST, preferred_element_type=F32)
    return jnp.dot(a.astype(BF16), w, preferred_element_type=F32)


def _bdot(a, b):
    return lax.dot_general(a, b, (((2,), (1,)), ((0,), (0,))), preferred_element_type=F32)


def _bdot_nt(a, b):
    return lax.dot_general(a, b, (((2,), (2,)), ((0,), (0,))), preferred_element_type=F32)


def _bdot_tn(a, b):
    return lax.dot_general(a, b, (((1,), (1,)), ((0,), (0,))), preferred_element_type=F32)


def _split_bf16(a):
    hi = a.astype(BF16)
    lo = (a - hi.astype(F32)).astype(BF16)
    return hi, lo


def _bdot3(a, b):
    ah, al = _split_bf16(a)
    bh, bl = _split_bf16(b)
    return _bdot(ah, bh) + (_bdot(ah, bl) + _bdot(al, bh))


def _rms_matmul_body(x_ref, g_ref, w_ref, o_ref):
    xn = _rms(x_ref[...], g_ref[...])
    o_ref[...] = _wdot(xn, w_ref[...])


def _wspec(shape, index_map, steps):
    if steps == 1:
        return pl.BlockSpec(shape, index_map, pipeline_mode=pl.Buffered(1))
    return pl.BlockSpec(shape, index_map)


def rms_matmul(x, g, w, tm, name, tn=None):
    m, d = x.shape
    n = w.shape[1]
    tn = n if tn is None else tn
    nj = n // tn
    return pl.pallas_call(
        _rms_matmul_body,
        out_shape=jax.ShapeDtypeStruct((m, n), F32),
        grid=(m // tm, nj),
        in_specs=[pl.BlockSpec((tm, d), lambda i, j: (i, 0)), _resident((1, d)),
                  _wspec((d, tn), lambda i, j: (0, j), nj)],
        out_specs=pl.BlockSpec((tm, tn), lambda i, j: (i, j)),
        compiler_params=_params("parallel", "arbitrary"),
        name=name,
    )(x, g.reshape(1, d), w)


def _out_proj_body(x_ref, a1_ref, a2_ref, w1_ref, w2_ref, *rest, n_main):
    o_ref = rest[-1]

    @pl.when(pl.program_id(0) < n_main)
    def _():
        acc = _wdot(a1_ref[...], w1_ref[...]) + _wdot(a2_ref[...], w2_ref[...])
        o_ref[...] = x_ref[...] + acc

    @pl.when(pl.program_id(0) >= n_main)
    def _():
        o_ref[...] = jnp.zeros_like(o_ref)


def out_proj(x, a1, a2, w, tm, name, out_rows=None, into=None, row_offset=0):
    m, d = x.shape
    k1, k2 = a1.shape[1], a2.shape[1]
    assert w.shape[0] == k1 + k2 and k1 % k2 == 0 and row_offset % tm == 0
    out_rows = m if out_rows is None else out_rows
    off = row_offset // tm
    n_main = m // tm
    n_extra = 0 if (into is not None or out_rows == m) else 1
    assert out_rows - m <= n_extra * tm or into is not None
    last = n_main - 1
    in_specs = [
        pl.BlockSpec((tm, d), lambda i: (jnp.minimum(i, last), 0)),
        pl.BlockSpec((tm, k1), lambda i: (jnp.minimum(i, last), 0)),
        pl.BlockSpec((tm, k2), lambda i: (jnp.minimum(i, last), 0)),
        pl.BlockSpec((k1, d), lambda i: (0, 0), pipeline_mode=pl.Buffered(1)),
        pl.BlockSpec((k2, d), lambda i: (k1 // k2, 0), pipeline_mode=pl.Buffered(1)),
    ]
    args = [x, a1, a2, w, w]
    aliases = {}
    if into is not None:
        in_specs.append(pl.BlockSpec(memory_space=pl.ANY))
        args.append(into)
        aliases = {len(args) - 1: 0}
    return pl.pallas_call(
        functools.partial(_out_proj_body, n_main=n_main),
        out_shape=jax.ShapeDtypeStruct((out_rows, d), F32),
        grid=(n_main + n_extra,),
        in_specs=in_specs,
        out_specs=pl.BlockSpec((tm, d), lambda i: (i + off, 0)),
        input_output_aliases=aliases,
        compiler_params=_params("parallel"),
        name=name,
    )(*args)


def _ffn_body(h_ref, g_ref, wg_ref, wu_ref, wo_ref, o_ref, xn_ref):
    @pl.when(pl.program_id(1) == 0)
    def _():
        h = h_ref[...]
        xn_ref[...] = _rms(h, g_ref[...])
        o_ref[...] = h

    xn = xn_ref[...]
    a = _wdot(xn, wg_ref[...])
    b = _wdot(xn, wu_ref[...])
    o_ref[...] += _wdot(_silu(a) * b, wo_ref[...])


def ffn_dense(h, g, wi, wo, tm, name, tf=None):
    m, d = h.shape
    f = wo.shape[0]
    tf = f if tf is None else tf
    nj = f // tf
    return pl.pallas_call(
        _ffn_body,
        out_shape=jax.ShapeDtypeStruct((m, d), F32),
        grid=(m // tm, nj),
        in_specs=[
            pl.BlockSpec((tm, d), lambda i, j: (i, 0)),
            _resident((1, d)),
            _wspec((d, tf), lambda i, j: (0, j), nj),
            _wspec((d, tf), lambda i, j: (0, nj + j), nj),
            _wspec((tf, d), lambda i, j: (j, 0), nj),
        ],
        out_specs=pl.BlockSpec((tm, d), lambda i, j: (i, 0)),
        scratch_shapes=[pltpu.VMEM((tm, d), F32)],
        compiler_params=_params("parallel", "arbitrary"),
        name=name,
    )(h, g.reshape(1, d), wi, wi, wo)


def _unit_lower_inverse(mx, eye, blk):
    md = jnp.where(blk, mx, 0.0)
    c = mx - md
    p = eye - md
    m2 = _bdot3(md, md)
    p = p + _bdot3(p, m2)
    m4 = _bdot3(m2, m2)
    p = p + _bdot3(p, m4)
    m8 = _bdot3(m4, m4)
    td = p + _bdot3(p, m8)
    n = _bdot3(td, c)
    n2 = _bdot3(n, n)
    q = eye - n
    q = q + _bdot3(q, n2)
    return _bdot3(q, td)


def _dn_prompt_body(qkv_ref, z_ref, bl_ref, al_ref, cw_ref, alog_ref, dtb_ref, on_ref,
                    dn_ref, sfin_ref, s_ref, xbuf_ref, act_ref, *, rows, heads, dk, dv):
    c = pl.program_id(1)
    wq = heads * dk

    @pl.when(c == 0)
    def _():
        s_ref[...] = jnp.zeros_like(s_ref)
        xbuf_ref[0:SUBLANE, :] = jnp.zeros((SUBLANE, xbuf_ref.shape[1]), F32)

    xbuf_ref[SUBLANE:SUBLANE + rows, :] = qkv_ref[...]
    first = SUBLANE - (CONV_WIDTH - 1)
    y = xbuf_ref[pl.ds(first, rows), :] * cw_ref[0:1, :]
    for j in range(1, CONV_WIDTH):
        y = y + xbuf_ref[pl.ds(first + j, rows), :] * cw_ref[j:j + 1, :]
    act_ref[...] = _silu(y)
    xbuf_ref[0:SUBLANE, :] = xbuf_ref[rows:rows + SUBLANE, :]

    ri = lax.broadcasted_iota(jnp.int32, (CHUNK, CHUNK), 0)
    cj = lax.broadcasted_iota(jnp.int32, (CHUNK, CHUNK), 1)
    incl = (ri >= cj)[None]
    strict = (ri > cj)[None]
    blk = ((ri // INV_BLOCK) == (cj // INV_BLOCK))[None]
    tri = (ri >= cj).astype(F32)
    eye = (ri == cj).astype(F32)[None]

    def per_head(fn):
        return jnp.stack([fn(h) for h in range(heads)], axis=0)

    for cc in range(rows // CHUNK):
        r0 = cc * CHUNK
        rs = slice(r0, r0 + CHUNK)
        q3 = per_head(lambda h: act_ref[rs, h * dk:(h + 1) * dk])
        k3 = per_head(lambda h: act_ref[rs, wq + h * dk:wq + (h + 1) * dk])
        v3 = per_head(lambda h: act_ref[rs, 2 * wq + h * dv:2 * wq + (h + 1) * dv])
        qn = q3 * lax.rsqrt(jnp.sum(q3 * q3, axis=-1, keepdims=True) + EPS) * (dk ** -0.5)
        kn = k3 * lax.rsqrt(jnp.sum(k3 * k3, axis=-1, keepdims=True) + EPS)

        beta = jax.nn.sigmoid(bl_ref[rs, :])
        g = -jnp.exp(alog_ref[...]) * jax.nn.softplus(al_ref[rs, :] + dtb_ref[...])
        gcum = jnp.dot(tri, g, precision=HIGHEST, preferred_element_type=F32)
        gcum_t = gcum.T
        gc = per_head(lambda h: gcum[:, h:h + 1])
        gr = per_head(lambda h: gcum_t[h:h + 1, :])
        bc = per_head(lambda h: beta[:, h:h + 1])
        decay = jnp.where(incl, jnp.exp(gc - gr), 0.0)

        kb = kn.astype(BF16)
        qb = qn.astype(BF16)
        kk = _bdot_nt(kb, kb)
        mx = jnp.where(strict, bc * kk * decay, 0.0)
        tinv = _unit_lower_inverse(mx, eye, blk)
        eg = jnp.exp(gc)
        rhs = jnp.concatenate([v3 * bc, kn * (bc * eg)], axis=-1)
        sol = _bdot3(tinv, rhs)
        w_val = sol[..., :dv]
        k_cd = sol[..., dv:]
        qk = _bdot_nt(qb, kb) * decay
        q_dec = qn * eg
        g_last = gc[:, CHUNK - 1:CHUNK, :]
        k_tail = kn * jnp.exp(g_last - gc)
        c_decay = jnp.exp(g_last)

        s = s_ref[...]
        sb = s.astype(BF16)
        u = w_val - _bdot(k_cd.astype(BF16), sb)
        ub = u.astype(BF16)
        o = _bdot(q_dec.astype(BF16), sb) + _bdot(qk.astype(BF16), ub)
        s_ref[...] = s * c_decay + _bdot_tn(k_tail.astype(BF16), ub)

        o = o * lax.rsqrt(jnp.mean(o * o, axis=-1, keepdims=True) + EPS) * on_ref[...]
        z3 = per_head(lambda h: z_ref[rs, h * dv:(h + 1) * dv])
        o = o * _silu(z3)
        for h in range(heads):
            dn_ref[rs, h * dv:(h + 1) * dv] = o[h]

    @pl.when(c == pl.num_programs(1) - 1)
    def _():
        sfin_ref[0] = s_ref[...]


def deltanet_prompt(proj, col, conv_w, a_log, dt_bias, o_norm, batch, seq, heads, dk, dv, rows):
    conv_ch = 2 * heads * dk + heads * dv
    zw = heads * dv
    steps = seq // rows

    def row_map(width_off):
        return lambda b, c: (b * steps + c, width_off)

    body = functools.partial(_dn_prompt_body, rows=rows, heads=heads, dk=dk, dv=dv)
    return pl.pallas_call(
        body,
        out_shape=(jax.ShapeDtypeStruct((batch * seq, zw), F32),
                   jax.ShapeDtypeStruct((batch, heads, dk, dv), F32)),
        grid=(batch, steps),
        in_specs=[
            pl.BlockSpec((rows, conv_ch), row_map(col["qkv"] // conv_ch)),
            pl.BlockSpec((rows, zw), row_map(col["z"] // zw)),
            pl.BlockSpec((rows, LANE), row_map(col["b"] // LANE)),
            pl.BlockSpec((rows, LANE), row_map(col["a"] // LANE)),
            _resident((CONV_WIDTH, conv_ch)),
            _resident((1, LANE)),
            _resident((1, LANE)),
            _resident((1, dv)),
        ],
        out_specs=(pl.BlockSpec((rows, zw), lambda b, c: (b * steps + c, 0)),
                   pl.BlockSpec((1, heads, dk, dv), lambda b, c: (b, 0, 0, 0))),
        scratch_shapes=[
            pltpu.VMEM((heads, dk, dv), F32),
            pltpu.VMEM((rows + SUBLANE, conv_ch), F32),
            pltpu.VMEM((rows, conv_ch), F32),
        ],
        compiler_params=_params("parallel", "arbitrary"),
        name="deltanet_prompt",
    )(proj, proj, proj, proj, conv_w, a_log, dt_bias, o_norm)


def _dn_sample_body(qkv_ref, z_ref, bl_ref, al_ref, cbuf_ref, st_ref, cw_ref, alog_ref, dtb_ref, on_ref,
                    dn_ref, sto_ref, cbo_ref, act_ref, beta_ref, dec_ref, *, nb, heads, dk, dv):
    wq = heads * dk
    new = qkv_ref[...]
    y = cbuf_ref[:, 0, :] * cw_ref[0:1, :]
    for j in range(1, CONV_WIDTH - 1):
        y = y + cbuf_ref[:, j, :] * cw_ref[j:j + 1, :]
    y = y + new * cw_ref[CONV_WIDTH - 1:CONV_WIDTH, :]
    act_ref[...] = _silu(y)
    for j in range(CONV_WIDTH - 2):
        cbo_ref[:, j, :] = cbuf_ref[:, j + 1, :]
    cbo_ref[:, CONV_WIDTH - 2, :] = new

    beta_ref[...] = jax.nn.sigmoid(bl_ref[...])
    g = -jnp.exp(alog_ref[...]) * jax.nn.softplus(al_ref[...] + dtb_ref[...])
    dec_ref[...] = jnp.exp(g)

    ri = lax.broadcasted_iota(jnp.int32, (dk, dk), 0)
    cj = lax.broadcasted_iota(jnp.int32, (dk, dk), 1)
    diag = ri == cj

    def to_col(row):
        return jnp.sum(jnp.where(diag, jnp.broadcast_to(row, (dk, dk)), 0.0), axis=1, keepdims=True)

    for i in range(nb):
        row = slice(i, i + 1)
        for h in range(heads):
            q = act_ref[row, h * dk:(h + 1) * dk]
            k = act_ref[row, wq + h * dk:wq + (h + 1) * dk]
            v = act_ref[row, 2 * wq + h * dv:2 * wq + (h + 1) * dv]
            qn = q * lax.rsqrt(jnp.sum(q * q, axis=-1, keepdims=True) + EPS) * (dk ** -0.5)
            kn = k * lax.rsqrt(jnp.sum(k * k, axis=-1, keepdims=True) + EPS)
            b_ih = beta_ref[row, h:h + 1]
            d_ih = dec_ref[row, h:h + 1]
            kcol = to_col(kn)
            qcol = to_col(qn)
            s = st_ref[i, h] * d_ih
            u = b_ih * (v - jnp.sum(kcol * s, axis=0, keepdims=True))
            s = s + kcol * u
            o = jnp.sum(qcol * s, axis=0, keepdims=True)
            sto_ref[i, h] = s
            o = o * lax.rsqrt(jnp.mean(o * o, axis=-1, keepdims=True) + EPS) * on_ref[...]
            dn_ref[row, h * dv:(h + 1) * dv] = o * _silu(z_ref[row, h * dv:(h + 1) * dv])


def deltanet_sample(proj, col, conv_buf, state, conv_w, a_log, dt_bias, o_norm, heads, dk, dv, nb):
    n_seq = proj.shape[0]
    conv_ch = 2 * heads * dk + heads * dv
    zw = heads * dv
    body = functools.partial(_dn_sample_body, nb=nb, heads=heads, dk=dk, dv=dv)
    return pl.pallas_call(
        body,
        out_shape=(jax.ShapeDtypeStruct((n_seq, zw), F32),
                   jax.ShapeDtypeStruct(state.shape, F32),
                   jax.ShapeDtypeStruct(conv_buf.shape, F32)),
        grid=(n_seq // nb,),
        in_specs=[
            pl.BlockSpec((nb, conv_ch), lambda i: (i, col["qkv"] // conv_ch)),
            pl.BlockSpec((nb, zw), lambda i: (i, col["z"] // zw)),
            pl.BlockSpec((nb, LANE), lambda i: (i, col["b"] // LANE)),
            pl.BlockSpec((nb, LANE), lambda i: (i, col["a"] // LANE)),
            pl.BlockSpec((nb, CONV_WIDTH - 1, conv_ch), lambda i: (i, 0, 0)),
            pl.BlockSpec((nb, heads, dk, dv), lambda i: (i, 0, 0, 0)),
            _resident((CONV_WIDTH, conv_ch)),
            _resident((1, LANE)),
            _resident((1, LANE)),
            _resident((1, dv)),
        ],
        out_specs=(pl.BlockSpec((nb, zw), lambda i: (i, 0)),
                   pl.BlockSpec((nb, heads, dk, dv), lambda i: (i, 0, 0, 0)),
                   pl.BlockSpec((nb, CONV_WIDTH - 1, conv_ch), lambda i: (i, 0, 0))),
        scratch_shapes=[
            pltpu.VMEM((nb, conv_ch), F32),
            pltpu.VMEM((nb, LANE), F32),
            pltpu.VMEM((nb, LANE), F32),
        ],
        compiler_params=_params("parallel"),
        name="deltanet_sample",
    )(proj, proj, proj, proj, conv_buf, state, conv_w, a_log, dt_bias, o_norm)


def _xattn_prompt_body(q_ref, k_ref, v_ref, o_ref, *, heads, dh):
    for h in range(heads):
        sl = slice(h * dh, (h + 1) * dh)
        q = q_ref[:, sl].astype(BF16)
        k = k_ref[:, sl].astype(BF16)
        v = v_ref[:, sl].astype(BF16)
        s = lax.dot_general(q, k, (((1,), (1,)), ((), ())), preferred_element_type=F32) * (dh ** -0.5)
        e = jnp.exp(s - jnp.max(s, axis=-1, keepdims=True))
        p = e / jnp.sum(e, axis=-1, keepdims=True)
        o_ref[:, sl] = _dot(p.astype(BF16), v)


def xattn_prompt(proj, cq_col, kv, batch, seq, n_mem, heads, dh, tq):
    xw = heads * dh
    steps = seq // tq
    body = functools.partial(_xattn_prompt_body, heads=heads, dh=dh)
    return pl.pallas_call(
        body,
        out_shape=jax.ShapeDtypeStruct((batch * seq, xw), F32),
        grid=(batch, steps),
        in_specs=[
            pl.BlockSpec((tq, xw), lambda b, i: (b * steps + i, cq_col // xw)),
            pl.BlockSpec((n_mem, xw), lambda b, i: (b, 0)),
            pl.BlockSpec((n_mem, xw), lambda b, i: (b, 1)),
        ],
        out_specs=pl.BlockSpec((tq, xw), lambda b, i: (b * steps + i, 0)),
        compiler_params=_params("parallel", "parallel"),
        name="xattn_prompt",
    )(proj, kv, kv)


def _xattn_sample_body(q_ref, k_ref, v_ref, o_ref, *, nb, heads, dh):
    for i in range(nb):
        row = slice(i, i + 1)
        for h in range(heads):
            sl = slice(h * dh, (h + 1) * dh)
            q = q_ref[row, sl]
            k = k_ref[0, i, :, h, :]
            v = v_ref[0, i, :, h, :]
            s = jnp.sum(k * q, axis=1, keepdims=True) * (dh ** -0.5)
            e = jnp.exp(s - jnp.max(s, axis=0, keepdims=True))
            p = e / jnp.sum(e, axis=0, keepdims=True)
            o_ref[row, sl] = jnp.sum(p * v, axis=0, keepdims=True)


def xattn_sample(proj, cq_col, cache_k, cache_v, layer, nb):
    _, n_seq, n_mem, heads, dh = cache_k.shape
    xw = heads * dh
    body = functools.partial(_xattn_sample_body, nb=nb, heads=heads, dh=dh)
    cache_spec = pl.BlockSpec((1, nb, n_mem, heads, dh), lambda i: (layer, i, 0, 0, 0))
    return pl.pallas_call(
        body,
        out_shape=jax.ShapeDtypeStruct((n_seq, xw), F32),
        grid=(n_seq // nb,),
        in_specs=[pl.BlockSpec((nb, xw), lambda i: (i, cq_col // xw)), cache_spec, cache_spec],
        out_specs=pl.BlockSpec((nb, xw), lambda i: (i, 0)),
        compiler_params=_params("parallel"),
        name="xattn_sample",
    )(proj, cache_k, cache_v)


def _pool_mix(sums, cnts, u_of, pw_ref, ps_ref, o_ref, gw):
    for gi in range(len(POOL_WINDOWS)):
        cols = slice(gi * gw, (gi + 1) * gw)
        d = sums[gi] / cnts[gi] - u_of(cols)
        o_ref[:, cols] = _wdot(d, pw_ref[gi]) * ps_ref[:, cols]


def _pool_prompt_body(u_ref, pw_ref, ps_ref, o_ref, xbuf, s2, s4, s8, *, tt, gw):
    t = pl.program_id(1)
    n = POOL_HALO + tt

    @pl.when(t == 0)
    def _():
        xbuf[0:POOL_HALO, :] = jnp.zeros((POOL_HALO, xbuf.shape[1]), F32)

    xbuf[POOL_HALO:n, :] = u_ref[...]
    s2[8:n, :] = xbuf[8:n, :] + xbuf[7:n - 1, :]
    s4[16:n, :] = s2[16:n, gw:] + s2[14:n - 2, gw:]
    s8[24:n, :] = s4[24:n, gw:] + s4[20:n - 4, gw:]
    s16 = s8[32:n, gw:] + s8[24:n - 8, gw:]
    sums = [s2[POOL_HALO:n, 0:gw], s4[POOL_HALO:n, 0:gw], s8[POOL_HALO:n, 0:gw], s16]
    pos = t * tt + lax.broadcasted_iota(jnp.int32, (tt, 1), 0)
    cnts = [jnp.minimum(pos + 1, w).astype(F32) for w in POOL_WINDOWS]
    _pool_mix(sums, cnts, lambda cols: u_ref[:, cols], pw_ref, ps_ref, o_ref, gw)
    xbuf[0:POOL_HALO, :] = xbuf[tt:n, :]


def pool_prompt(proj, pool_w, pool_scale, batch, seq, tt):
    pw_total = pool_scale.shape[-1]
    gw = pw_total // len(POOL_WINDOWS)
    steps = seq // tt
    n = POOL_HALO + tt
    body = functools.partial(_pool_prompt_body, tt=tt, gw=gw)
    return pl.pallas_call(
        body,
        out_shape=jax.ShapeDtypeStruct((batch * seq, pw_total), F32),
        grid=(batch, steps),
        in_specs=[
            pl.BlockSpec((tt, pw_total), lambda b, t: (b * steps + t, 0)),
            _resident(pool_w.shape),
            _resident((1, pw_total)),
        ],
        out_specs=pl.BlockSpec((tt, pw_total), lambda b, t: (b * steps + t, 0)),
        scratch_shapes=[
            pltpu.VMEM((n, pw_total), F32),
            pltpu.VMEM((n, pw_total), F32),
            pltpu.VMEM((n, pw_total - gw), F32),
            pltpu.VMEM((n, pw_total - 2 * gw), F32),
        ],
        compiler_params=_params("parallel", "arbitrary"),
        name="pool_prompt",
    )(proj, pool_w, pool_scale)


def _pool_sample_body(u_ref, st_ref, pw_ref, ps_ref, o_ref, sto_ref, *, gw, buf):
    pw_total = gw * len(POOL_WINDOWS)
    new = u_ref[:, 0:pw_total]
    sums, cnts = [], []
    for gi, w in enumerate(POOL_WINDOWS):
        cols = slice(gi * gw, (gi + 1) * gw)
        s = new[:, cols]
        for i in range(1, w):
            s = s + st_ref[:, buf - i, cols]
        sums.append(s)
        cnts.append(float(min(PAST_LEN + 1, w)))
    _pool_mix(sums, cnts, lambda cols: new[:, cols], pw_ref, ps_ref, o_ref, gw)
    sto_ref[:, 0:buf - 1, :] = st_ref[:, 1:buf, :]
    sto_ref[:, buf - 1, :] = new


def pool_sample(proj, state, pool_w, pool_scale):
    n_seq, buf, pw_total = state.shape
    gw = pw_total // len(POOL_WINDOWS)
    body = functools.partial(_pool_sample_body, gw=gw, buf=buf)
    return pl.pallas_call(
        body,
        out_shape=(jax.ShapeDtypeStruct((n_seq, pw_total), F32),
                   jax.ShapeDtypeStruct(state.shape, F32)),
        compiler_params=pltpu.CompilerParams(vmem_limit_bytes=VMEM_LIMIT_BYTES),
        name="pool_sample",
    )(proj, state, pool_w, pool_scale)


def _route_body(h_ref, g_ref, r_ref, idx_ref, gate_ref, *, n_experts):
    xn = _rms(h_ref[...], g_ref[...])
    logits = jnp.dot(xn, r_ref[...], precision=HIGHEST, preferred_element_type=F32)
    lane = lax.broadcasted_iota(jnp.int32, logits.shape, 1)
    lm = jnp.where(lane < n_experts, logits, -jnp.inf)
    m1 = jnp.max(lm, axis=1, keepdims=True)
    i1 = jnp.min(jnp.where(lm == m1, lane, LANE), axis=1, keepdims=True)
    lm2 = jnp.where(lane == i1, -jnp.inf, lm)
    m2 = jnp.max(lm2, axis=1, keepdims=True)
    i2 = jnp.min(jnp.where(lm2 == m2, lane, LANE), axis=1, keepdims=True)
    e2 = jnp.exp(m2 - m1)
    den = 1.0 + e2
    idx_ref[...] = jnp.where(lane == 0, i1, jnp.where(lane == 1, i2, 0))
    gate_ref[...] = jnp.where(lane == 0, 1.0 / den, jnp.where(lane == 1, e2 / den, 0.0))


def moe_route(h_all, row_offset, rows, g, r_pad, tm, n_experts, name):
    d = h_all.shape[1]
    off = row_offset // tm
    body = functools.partial(_route_body, n_experts=n_experts)
    return pl.pallas_call(
        body,
        out_shape=(jax.ShapeDtypeStruct((rows, LANE), jnp.int32), jax.ShapeDtypeStruct((rows, LANE), F32)),
        grid=(rows // tm,),
        in_specs=[pl.BlockSpec((tm, d), lambda i: (i + off, 0)), _resident((1, d)), _resident((d, LANE))],
        out_specs=(pl.BlockSpec((tm, LANE), lambda i: (i, 0)), pl.BlockSpec((tm, LANE), lambda i: (i, 0))),
        compiler_params=_params("parallel"),
        name=name,
    )(h_all, g.reshape(1, d), r_pad)


def _row_gather_start(src_hbm, idx_ref, dst, sem, n):
    def body(r, carry):
        pltpu.make_async_copy(src_hbm.at[pl.ds(idx_ref[0, 0, r], 1)], dst.at[pl.ds(r, 1)], sem).start()
        return carry

    lax.fori_loop(0, n, body, 0, unroll=8)


def _row_gather_wait(src_hbm, dst, sem, n):
    def body(r, carry):
        pltpu.make_async_copy(src_hbm.at[pl.ds(0, 1)], dst.at[pl.ds(r, 1)], sem).wait()
        return carry

    lax.fori_loop(0, n, body, 0, unroll=8)


def _experts_body(te_ref, nu_ref, src_cur_ref, src_nxt_ref, h_hbm, g_ref, wg_ref, wu_ref, wo_ref,
                  o_ref, xbuf, xn_ref, sem, *, tm):
    i = pl.program_id(0)
    j = pl.program_id(1)
    n_used = nu_ref[0]
    slot = i % 2
    valid = i < n_used

    @pl.when(j == 0)
    def _():
        @pl.when(i == 0)
        def _():
            _row_gather_start(h_hbm, src_cur_ref, xbuf.at[0], sem.at[0], tm)

        @pl.when(i + 1 < n_used)
        def _():
            _row_gather_start(h_hbm, src_nxt_ref, xbuf.at[1 - slot], sem.at[1 - slot], tm)

        @pl.when(valid)
        def _():
            _row_gather_wait(h_hbm, xbuf.at[slot], sem.at[slot], tm)
            xn_ref[...] = _rms(xbuf[slot], g_ref[...]).astype(BF16)

        @pl.when(jnp.logical_not(valid))
        def _():
            o_ref[...] = jnp.zeros_like(o_ref)

    @pl.when(valid)
    def _():
        xn = xn_ref[...]
        a = _dot(xn, wg_ref[0].astype(BF16))
        b = _dot(xn, wu_ref[0].astype(BF16))
        part = _dot((_silu(a) * b).astype(BF16), wo_ref[0].astype(BF16))

        @pl.when(j == 0)
        def _():
            o_ref[...] = part

        @pl.when(j > 0)
        def _():
            o_ref[...] += part


def moe_experts(h_all, g, wi, wo, src, tile_expert, n_used, tm, tf):
    d = h_all.shape[1]
    f = wo.shape[1]
    nj = f // tf
    n_tiles = src.shape[0]

    def wblock(col_off):
        def index_map(i, j, te, nu):
            jj = jnp.where(i < nu[0], j, nj - 1)
            return (te[i], 0, col_off + jj)
        return index_map

    def woblock(i, j, te, nu):
        return (te[i], jnp.where(i < nu[0], j, nj - 1), 0)

    def smem_tile(index_map):
        return pl.BlockSpec((1, 1, tm), index_map, memory_space=pltpu.SMEM)

    grid_spec = pltpu.PrefetchScalarGridSpec(
        num_scalar_prefetch=2,
        grid=(n_tiles, nj),
        in_specs=[
            smem_tile(lambda i, j, te, nu: (i, 0, 0)),
            smem_tile(lambda i, j, te, nu: (jnp.minimum(i + 1, n_tiles - 1), 0, 0)),
            pl.BlockSpec(memory_space=pl.ANY),
            pl.BlockSpec((1, d), lambda i, j, te, nu: (0, 0)),
            pl.BlockSpec((1, d, tf), wblock(0)),
            pl.BlockSpec((1, d, tf), wblock(nj)),
            pl.BlockSpec((1, tf, d), woblock),
        ],
        out_specs=pl.BlockSpec((tm, d), lambda i, j, te, nu: (i, 0)),
        scratch_shapes=[
            pltpu.VMEM((2, tm, d), F32),
            pltpu.VMEM((tm, d), BF16),
            pltpu.SemaphoreType.DMA((2,)),
        ],
    )
    return pl.pallas_call(
        functools.partial(_experts_body, tm=tm),
        out_shape=jax.ShapeDtypeStruct((n_tiles * tm, d), F32),
        grid_spec=grid_spec,
        compiler_params=_params("arbitrary", "arbitrary"),
        name="moe_experts",
    )(tile_expert, n_used, src, src, h_all, g.reshape(1, d), wi, wi, wo)


def _combine_body(d1c_ref, d2c_ref, d1n_ref, d2n_ref, h_ref, gate_ref, fn_ref, ys_hbm, o_ref, ybuf, sem,
                  *, tb, final_norm):
    i = pl.program_id(0)
    n = pl.num_programs(0)
    slot = i % 2

    def start(d1_ref, d2_ref, s):
        _row_gather_start(ys_hbm, d1_ref, ybuf.at[s, 0], sem.at[s], tb)
        _row_gather_start(ys_hbm, d2_ref, ybuf.at[s, 1], sem.at[s], tb)

    @pl.when(i == 0)
    def _():
        start(d1c_ref, d2c_ref, 0)

    @pl.when(i + 1 < n)
    def _():
        start(d1n_ref, d2n_ref, 1 - slot)

    _row_gather_wait(ys_hbm, ybuf.at[slot, 0], sem.at[slot], tb)
    _row_gather_wait(ys_hbm, ybuf.at[slot, 1], sem.at[slot], tb)
    gate = gate_ref[...]
    out = h_ref[...] + (gate[:, 0:1] * ybuf[slot, 0] + gate[:, 1:2] * ybuf[slot, 1])
    if final_norm:
        out = _rms(out, fn_ref[...])
    o_ref[...] = out


def moe_combine(h_all, row_offset, rows, gates, dest1, dest2, ys, fn, tb, final_norm, name):
    d = h_all.shape[1]
    off = row_offset // tb
    n = rows // tb

    def smem_tile(index_map):
        return pl.BlockSpec((1, 1, tb), index_map, memory_space=pltpu.SMEM)

    def cur(i):
        return (i, 0, 0)

    def nxt(i):
        return (jnp.minimum(i + 1, n - 1), 0, 0)

    d1 = dest1.reshape(n, 1, tb)
    d2 = dest2.reshape(n, 1, tb)
    body = functools.partial(_combine_body, tb=tb, final_norm=final_norm)
    return pl.pallas_call(
        body,
        out_shape=jax.ShapeDtypeStruct((rows, d), F32),
        grid=(n,),
        in_specs=[
            smem_tile(cur), smem_tile(cur), smem_tile(nxt), smem_tile(nxt),
            pl.BlockSpec((tb, d), lambda i: (i + off, 0)),
            pl.BlockSpec((tb, LANE), lambda i: (i, 0)),
            _resident((1, d)),
            pl.BlockSpec(memory_space=pl.ANY),
        ],
        out_specs=pl.BlockSpec((tb, d), lambda i: (i, 0)),
        scratch_shapes=[pltpu.VMEM((2, 2, tb, d), F32), pltpu.SemaphoreType.DMA((2,))],
        compiler_params=_params("arbitrary"),
        name=name,
    )(d1, d2, d1, d2, h_all, gates, fn.reshape(1, d), ys)


def moe_plan(idx, n_experts, tm):
    n_tok, k = idx.shape
    n_slots = n_tok * k
    n_tiles = n_slots // tm + n_experts
    e_flat = idx.reshape(n_slots)
    onehot = (e_flat[:, None] == jnp.arange(n_experts, dtype=jnp.int32)[None, :]).astype(jnp.int32)
    csum = jnp.cumsum(onehot, axis=0)
    counts = csum[-1]
    rank = jnp.sum(csum * onehot, axis=1) - 1
    tiles_e = (counts + tm - 1) // tm
    tile_end = jnp.cumsum(tiles_e)
    pstart = (tile_end - tiles_e) * tm
    dest = jnp.sum(onehot * pstart[None, :], axis=1) + rank
    n_used = tile_end[-1]
    tile_ids = jnp.arange(n_tiles, dtype=jnp.int32)
    tile_expert = jnp.sum((tile_ids[:, None] >= tile_end[None, :]).astype(jnp.int32), axis=1)
    last_expert = jnp.sum((n_used - 1 >= tile_end).astype(jnp.int32))
    tile_expert = jnp.where(tile_ids < n_used, tile_expert, last_expert).astype(jnp.int32)
    tok = jnp.arange(n_slots, dtype=jnp.int32) // k
    src = jnp.zeros((n_tiles * tm,), jnp.int32).at[dest].set(tok, unique_indices=True)
    return (dest.reshape(n_tok, k).astype(jnp.int32), src.reshape(n_tiles, 1, tm), tile_expert,
            n_used.reshape(1).astype(jnp.int32))


def _pad_lanes(v):
    return jnp.zeros((1, LANE), F32).at[0, :v.shape[0]].set(v)


def kernel(x_prompt, x_sample, state_dn, state_dn_conv, state_pool, cache_mem_k, cache_mem_v, mem_prompt, a_norm1, a_w_in, a_conv_w, a_a_log, a_dt_bias, a_o_norm, a_w_out, a_norm2, a_ffn_wi, a_ffn_wo, b_norm1, b_w_in, b_pool_w, b_pool_scale, b_w_out, b_norm2, b_router, b_moe_wi, b_moe_wo, m_norm, m_w_k, m_w_v, final_norm):
    bp, seq, d = x_prompt.shape
    bs = x_sample.shape[0]
    assert x_sample.shape[1] == 1
    depth = m_norm.shape[0]
    assert depth % 2 == 0, "the output norm is fused into the expert block of the last (odd) layer"
    heads, dk, dv = state_dn.shape[2], state_dn.shape[3], state_dn.shape[4]
    conv_ch = state_dn_conv.shape[-1]
    qk_w, v_w = heads * dk, heads * dv
    assert conv_ch == 2 * qk_w + v_w and dk == LANE and dv == LANE and heads <= LANE
    n_mem, x_heads, x_dh = cache_mem_k.shape[2], cache_mem_k.shape[3], cache_mem_k.shape[4]
    xw = x_heads * x_dh
    pool_w_total = state_pool.shape[-1]
    pool_buf = state_pool.shape[2]
    d_ff = a_ffn_wo.shape[1]

    hp = x_prompt.reshape(bp * seq, d)
    hs = x_sample.reshape(bs, d)

    o1, o2, o3, o4 = conv_ch, conv_ch + v_w, conv_ch + v_w + heads, conv_ch + v_w + 2 * heads
    col_a = {"qkv": 0, "z": conv_ch, "cq": conv_ch + v_w, "b": conv_ch + v_w + xw, "a": conv_ch + v_w + xw + LANE}
    n_a = -(-(col_a["a"] + LANE) // 512) * 512

    dn_p, conv_p, pool_p, mk_all, mv_all = [], [], [], [], []
    dn_s, conv_s, pool_s = [], [], []
    for i in range(depth):
        j = i // 2
        w_kv = jnp.concatenate([m_w_k[i], m_w_v[i]], axis=1).astype(BF16)
        kv = rms_matmul(mem_prompt.reshape(bp * n_mem, d), m_norm[i], w_kv, 256, f"mem_kv_{i}")
        mk_all.append(kv[:, :xw].reshape(bp, n_mem, x_heads, x_dh))
        mv_all.append(kv[:, xw:].reshape(bp, n_mem, x_heads, x_dh))
        if i % 2 == 0:
            w = a_w_in[j]
            w_in = jnp.zeros((d, n_a), F32)
            w_in = w_in.at[:, :o2].set(w[:, :o2])
            w_in = w_in.at[:, col_a["cq"]:col_a["cq"] + xw].set(w[:, o4:])
            w_in = w_in.at[:, col_a["b"]:col_a["b"] + heads].set(w[:, o2:o3])
            w_in = w_in.at[:, col_a["a"]:col_a["a"] + heads].set(w[:, o3:o4])
            a_log, dt_b = _pad_lanes(a_a_log[j]), _pad_lanes(a_dt_bias[j])
            o_norm = a_o_norm[j].reshape(1, dv)

            proj_p = rms_matmul(hp, a_norm1[j], w_in.astype(BF16), 256, f"a_in_p_{i}")
            proj_s = rms_matmul(hs, a_norm1[j], w_in, bs, f"a_in_s_{i}", tn=512)
            mix_p, s_p = deltanet_prompt(proj_p, col_a, a_conv_w[j], a_log, dt_b, o_norm,
                                         bp, seq, heads, dk, dv, rows=CHUNK)
            mix_s, s_s, c_s = deltanet_sample(proj_s, col_a, state_dn_conv[j], state_dn[j], a_conv_w[j],
                                              a_log, dt_b, o_norm, heads, dk, dv, nb=SUBLANE)
            xa_p = xattn_prompt(proj_p, col_a["cq"], kv, bp, seq, n_mem, x_heads, x_dh, tq=512)
            xa_s = xattn_sample(proj_s, col_a["cq"], cache_mem_k, cache_mem_v, i, nb=SUBLANE)
            hp = out_proj(hp, mix_p, xa_p, a_w_out[j].astype(BF16), 512, f"a_out_p_{i}")
            hs = out_proj(hs, mix_s, xa_s, a_w_out[j], bs, f"a_out_s_{i}")
            hp = ffn_dense(hp, a_norm2[j], a_ffn_wi[j].astype(BF16), a_ffn_wo[j].astype(BF16), 256,
                           f"a_ffn_p_{i}")
            hs = ffn_dense(hs, a_norm2[j], a_ffn_wi[j], a_ffn_wo[j], bs, f"a_ffn_s_{i}", tf=256)
            dn_p.append(s_p)
            conv_p.append(proj_p.reshape(bp, seq, n_a)[:, seq - (CONV_WIDTH - 1):, :conv_ch])
            dn_s.append(s_s)
            conv_s.append(c_s)
        else:
            ps = b_pool_scale[j].reshape(1, pool_w_total)
            last = i == depth - 1

            proj_p = rms_matmul(hp, b_norm1[j], b_w_in[j].astype(BF16), 512, f"b_in_p_{i}")
            proj_s = rms_matmul(hs, b_norm1[j], b_w_in[j], bs, f"b_in_s_{i}")
            mix_p = pool_prompt(proj_p, b_pool_w[j].astype(BF16), ps, bp, seq, tt=512)
            mix_s, buf_s = pool_sample(proj_s, state_pool[j], b_pool_w[j], ps)
            xa_p = xattn_prompt(proj_p, pool_w_total, kv, bp, seq, n_mem, x_heads, x_dh, tq=512)
            xa_s = xattn_sample(proj_s, pool_w_total, cache_mem_k, cache_mem_v, i, nb=SUBLANE)
            n_p = bp * seq
            h_all = out_proj(hp, mix_p, xa_p, b_w_out[j].astype(BF16), 512, f"b_out_p_{i}", out_rows=n_p + bs)
            h_all = out_proj(hs, mix_s, xa_s, b_w_out[j], bs, f"b_out_s_{i}", out_rows=n_p + bs,
                             into=h_all, row_offset=n_p)
            n_experts = b_router.shape[-1]
            r_pad = jnp.zeros((d, LANE), F32).at[:, :n_experts].set(b_router[j])
            idx_p, gate_p = moe_route(h_all, 0, n_p, b_norm2[j], r_pad, 512, n_experts, f"b_route_p_{i}")
            idx_s, gate_s = moe_route(h_all, n_p, bs, b_norm2[j], r_pad, bs, n_experts, f"b_route_s_{i}")
            idx = jnp.concatenate([idx_p[:, :TOP_K], idx_s[:, :TOP_K]], axis=0)
            dest, src, tile_expert, n_used = moe_plan(idx, n_experts, MOE_TILE)
            ys = moe_experts(h_all, b_norm2[j], b_moe_wi[j], b_moe_wo[j], src, tile_expert, n_used,
                             MOE_TILE, 512)
            hp = moe_combine(h_all, 0, n_p, gate_p, dest[:n_p, 0], dest[:n_p, 1], ys, final_norm,
                             256, last, f"b_combine_p_{i}")
            hs = moe_combine(h_all, n_p, bs, gate_s, dest[n_p:, 0], dest[n_p:, 1], ys, final_norm,
                             bs, last, f"b_combine_s_{i}")
            pool_p.append(proj_p.reshape(bp, seq, -1)[:, seq - pool_buf:, :pool_w_total])
            pool_s.append(buf_s)

    y_prompt = hp.reshape(bp, seq, d)
    y_sample = hs.reshape(bs, 1, d)
    return (y_prompt, y_sample, jnp.stack(dn_p), jnp.stack(conv_p), jnp.stack(pool_p),
            jnp.stack(mk_all), jnp.stack(mv_all), jnp.stack(dn_s), jnp.stack(conv_s), jnp.stack(pool_s))
```

```python
import functools

import jax
import jax.numpy as jnp
from jax import lax
from jax.experimental import pallas as pl
from jax.experimental.pallas import tpu as pltpu

F32 = jnp.float32
BF16 = jnp.bfloat16
HIGHEST = lax.Precision.HIGHEST

EPS = 1e-6
CHUNK = 64
CONV_WIDTH = 4
POOL_WINDOWS = (2, 4, 8, 16)
PAST_LEN = 16384
TOP_K = 2

LANE = 128
SUBLANE = 8
VMEM_LIMIT_BYTES = 56 * 2**20

MOE_TILE = 1024
DN_STEP_CHUNKS = 4
INV_BLOCK = 16
POOL_HALO = 32


def _params(*sem):
    return pltpu.CompilerParams(dimension_semantics=sem, vmem_limit_bytes=VMEM_LIMIT_BYTES)


def _resident(shape):
    nd = len(shape)
    return pl.BlockSpec(shape, lambda *_: (0,) * nd, pipeline_mode=pl.Buffered(1))


def _rms(x, w):
    return x * lax.rsqrt(jnp.mean(x * x, axis=-1, keepdims=True) + EPS) * w


def _silu(x):
    return x * jax.nn.sigmoid(x)


def _dot(a, b):
    return jnp.dot(a, b, preferred_element_type=F32)


def _wdot(a, w):
    if w.dtype == F32:
        return jnp.dot(a, w, precision=HIGHEST, preferred_element_type=F32)
    return jnp.dot(a.astype(BF16), w, preferred_element_type=F32)


def _bdot(a, b):
    return lax.dot_general(a, b, (((2,), (1,)), ((0,), (0,))), preferred_element_type=F32)


def _bdot_nt(a, b):
    return lax.dot_general(a, b, (((2,), (2,)), ((0,), (0,))), preferred_element_type=F32)


def _bdot_tn(a, b):
    return lax.dot_general(a, b, (((1,), (1,)), ((0,), (0,))), preferred_element_type=F32)


def _rms_matmul_body(x_ref, g_ref, w_ref, o_ref):
    xn = _rms(x_ref[...], g_ref[...])
    o_ref[...] = _wdot(xn, w_ref[...])


def _wspec(shape, index_map, steps):
    if steps == 1:
        return pl.BlockSpec(shape, index_map, pipeline_mode=pl.Buffered(1))
    return pl.BlockSpec(shape, index_map)


def rms_matmul(x, g, w, tm, name, tn=None):
    m, d = x.shape
    n = w.shape[1]
    tn = n if tn is None else tn
    nj = n // tn
    return pl.pallas_call(
        _rms_matmul_body,
        out_shape=jax.ShapeDtypeStruct((m, n), F32),
        grid=(m // tm, nj),
        in_specs=[pl.BlockSpec((tm, d), lambda i, j: (i, 0)), _resident((1, d)),
                  _wspec((d, tn), lambda i, j: (0, j), nj)],
        out_specs=pl.BlockSpec((tm, tn), lambda i, j: (i, j)),
        compiler_params=_params("parallel", "arbitrary"),
        name=name,
    )(x, g.reshape(1, d), w)


def _out_proj_body(x_ref, a1_ref, a2_ref, w1_ref, w2_ref, *rest, n_main):
    o_ref = rest[-1]

    @pl.when(pl.program_id(0) < n_main)
    def _():
        acc = _wdot(a1_ref[...], w1_ref[...]) + _wdot(a2_ref[...], w2_ref[...])
        o_ref[...] = x_ref[...] + acc

    @pl.when(pl.program_id(0) >= n_main)
    def _():
        o_ref[...] = jnp.zeros_like(o_ref)


def out_proj(x, a1, a2, w, tm, name, out_rows=None, into=None, row_offset=0):
    m, d = x.shape
    k1, k2 = a1.shape[1], a2.shape[1]
    assert w.shape[0] == k1 + k2 and k1 % k2 == 0 and row_offset % tm == 0
    out_rows = m if out_rows is None else out_rows
    off = row_offset // tm
    n_main = m // tm
    n_extra = 0 if (into is not None or out_rows == m) else 1
    assert out_rows - m <= n_extra * tm or into is not None
    last = n_main - 1
    in_specs = [
        pl.BlockSpec((tm, d), lambda i: (jnp.minimum(i, last), 0)),
        pl.BlockSpec((tm, k1), lambda i: (jnp.minimum(i, last), 0)),
        pl.BlockSpec((tm, k2), lambda i: (jnp.minimum(i, last), 0)),
        pl.BlockSpec((k1, d), lambda i: (0, 0), pipeline_mode=pl.Buffered(1)),
        pl.BlockSpec((k2, d), lambda i: (k1 // k2, 0), pipeline_mode=pl.Buffered(1)),
    ]
    args = [x, a1, a2, w, w]
    aliases = {}
    if into is not None:
        in_specs.append(pl.BlockSpec(memory_space=pl.ANY))
        args.append(into)
        aliases = {len(args) - 1: 0}
    return pl.pallas_call(
        functools.partial(_out_proj_body, n_main=n_main),
        out_shape=jax.ShapeDtypeStruct((out_rows, d), F32),
        grid=(n_main + n_extra,),
        in_specs=in_specs,
        out_specs=pl.BlockSpec((tm, d), lambda i: (i + off, 0)),
        input_output_aliases=aliases,
        compiler_params=_params("parallel"),
        name=name,
    )(*args)


def _ffn_body(h_ref, g_ref, wg_ref, wu_ref, wo_ref, o_ref, xn_ref):
    @pl.when(pl.program_id(1) == 0)
    def _():
        h = h_ref[...]
        xn_ref[...] = _rms(h, g_ref[...])
        o_ref[...] = h

    xn = xn_ref[...]
    a = _wdot(xn, wg_ref[...])
    b = _wdot(xn, wu_ref[...])
    o_ref[...] += _wdot(_silu(a) * b, wo_ref[...])


def ffn_dense(h, g, wi, wo, tm, name, tf=None):
    m, d = h.shape
    f = wo.shape[0]
    tf = f if tf is None else tf
    nj = f // tf
    return pl.pallas_call(
        _ffn_body,
        out_shape=jax.ShapeDtypeStruct((m, d), F32),
        grid=(m // tm, nj),
        in_specs=[
            pl.BlockSpec((tm, d), lambda i, j: (i, 0)),
            _resident((1, d)),
            _wspec((d, tf), lambda i, j: (0, j), nj),
            _wspec((d, tf), lambda i, j: (0, nj + j), nj),
            _wspec((tf, d), lambda i, j: (j, 0), nj),
        ],
        out_specs=pl.BlockSpec((tm, d), lambda i, j: (i, 0)),
        scratch_shapes=[pltpu.VMEM((tm, d), F32)],
        compiler_params=_params("parallel", "arbitrary"),
        name=name,
    )(h, g.reshape(1, d), wi, wi, wo)


def _pair_blockdiag(y, left):
    yb = y.astype(BF16)
    zero = jnp.zeros_like(yb)
    return jnp.concatenate([jnp.where(left, yb, zero), jnp.where(left, zero, yb)], axis=1)


def _pair_mm(x, y, left):
    return _bdot(x.astype(BF16), _pair_blockdiag(y, left))


def _unit_lower_inverse(mx, eye, blk, left):
    mm = functools.partial(_pair_mm, left=left)
    md = jnp.where(blk, mx, 0.0)
    c = mx - md
    p = eye - md
    m2 = mm(md, md)
    p = p + mm(p, m2)
    m4 = mm(m2, m2)
    p = p + mm(p, m4)
    m8 = mm(m4, m4)
    td = p + mm(p, m8)
    n = mm(td, c)
    n2 = mm(n, n)
    q = eye - n
    q = q + mm(q, n2)
    return mm(q, td)


def _dn_prompt_body(qkv_ref, z_ref, bl_ref, al_ref, cw_ref, alog_ref, dtb_ref, on_ref,
                    dn_ref, sfin_ref, s_ref, xbuf_ref, act_ref, *, rows, heads, dk, dv):
    c = pl.program_id(1)
    wq = heads * dk

    @pl.when(c == 0)
    def _():
        s_ref[...] = jnp.zeros_like(s_ref)
        xbuf_ref[0:SUBLANE, :] = jnp.zeros((SUBLANE, xbuf_ref.shape[1]), F32)

    xbuf_ref[SUBLANE:SUBLANE + rows, :] = qkv_ref[...]
    first = SUBLANE - (CONV_WIDTH - 1)
    y = xbuf_ref[pl.ds(first, rows), :] * cw_ref[0:1, :]
    for j in range(1, CONV_WIDTH):
        y = y + xbuf_ref[pl.ds(first + j, rows), :] * cw_ref[j:j + 1, :]
    act_ref[...] = _silu(y)
    xbuf_ref[0:SUBLANE, :] = xbuf_ref[rows:rows + SUBLANE, :]

    pairs = heads // 2
    ri = lax.broadcasted_iota(jnp.int32, (CHUNK, 2 * CHUNK), 0)
    lane = lax.broadcasted_iota(jnp.int32, (CHUNK, 2 * CHUNK), 1)
    cj = lane % CHUNK
    left = (lane < CHUNK)[None]
    incl = (ri >= cj)[None]
    strict = (ri > cj)[None]
    blk = ((ri // INV_BLOCK) == (cj // INV_BLOCK))[None]
    eye = (ri == cj).astype(F32)[None]
    tr = lax.broadcasted_iota(jnp.int32, (CHUNK, CHUNK), 0)
    tc = lax.broadcasted_iota(jnp.int32, (CHUNK, CHUNK), 1)
    tri = (tr >= tc).astype(F32)

    n_chunks = rows // CHUNK
    zero = jnp.zeros((CHUNK, dk), BF16)

    def chunk_rows(cc):
        return slice(cc * CHUNK, (cc + 1) * CHUNK)

    def per_head(fn):
        return jnp.stack([fn(chunk_rows(cc), h) for cc in range(n_chunks) for h in range(heads)], axis=0)

    def per_pair(fn):
        return jnp.stack([fn(cc, p) for cc in range(n_chunks) for p in range(pairs)], axis=0)

    q3 = per_head(lambda rs, h: act_ref[rs, h * dk:(h + 1) * dk])
    k3 = per_head(lambda rs, h: act_ref[rs, wq + h * dk:wq + (h + 1) * dk])
    v3 = per_head(lambda rs, h: act_ref[rs, 2 * wq + h * dv:2 * wq + (h + 1) * dv])
    qn = q3 * lax.rsqrt(jnp.sum(q3 * q3, axis=-1, keepdims=True) + EPS) * (dk ** -0.5)
    kn = k3 * lax.rsqrt(jnp.sum(k3 * k3, axis=-1, keepdims=True) + EPS)

    beta = jax.nn.sigmoid(bl_ref[...])
    g = -jnp.exp(alog_ref[...]) * jax.nn.softplus(al_ref[...] + dtb_ref[...])
    gcum = [jnp.dot(tri, g[chunk_rows(cc)], precision=HIGHEST, preferred_element_type=F32)
            for cc in range(n_chunks)]
    gcum_t = [jnp.concatenate([gc_, pltpu.roll(gc_, LANE - 1, 1)], axis=0).T for gc_ in gcum]

    def pair_cols(x_of):
        return per_pair(lambda cc, p: jnp.where(left[0], x_of(cc)[:, 2 * p:2 * p + 1],
                                                x_of(cc)[:, 2 * p + 1:2 * p + 2]))

    gc2 = pair_cols(lambda cc: gcum[cc])
    bc2 = pair_cols(lambda cc: beta[chunk_rows(cc)])
    gr2 = per_pair(lambda cc, p: gcum_t[cc][2 * p:2 * p + 1, :])
    decay = jnp.where(incl, jnp.exp(gc2 - gr2), 0.0)

    kb = kn.astype(BF16)
    qb = qn.astype(BF16)

    def pair_of(x, cc, p):
        return x[cc * heads + 2 * p], x[cc * heads + 2 * p + 1]

    k_lhs = per_pair(lambda cc, p: jnp.concatenate(pair_of(kb, cc, p), axis=1))
    q_lhs = per_pair(lambda cc, p: jnp.concatenate(pair_of(qb, cc, p), axis=1))
    k_bd = per_pair(lambda cc, p: jnp.concatenate(
        [jnp.concatenate([pair_of(kb, cc, p)[0], zero], axis=1),
         jnp.concatenate([zero, pair_of(kb, cc, p)[1]], axis=1)], axis=0))
    kk = _bdot_nt(k_lhs, k_bd)
    mx = jnp.where(strict, bc2 * kk * decay, 0.0)
    tinv = _unit_lower_inverse(mx, eye, blk, left)

    gcum_rows = jnp.concatenate(gcum, axis=0)
    gc = per_head(lambda rs, h: gcum_rows[rs, h:h + 1])
    bc = per_head(lambda rs, h: beta[rs, h:h + 1])
    eg = jnp.exp(gc)
    rhs = jnp.concatenate([v3 * bc, kn * (bc * eg)], axis=-1)
    sol = _bdot(_pair_blockdiag(tinv, left), rhs.astype(BF16).reshape(n_chunks * pairs, 2 * CHUNK, dv + dk))
    sol = sol.reshape(n_chunks * heads, CHUNK, dv + dk)
    w_val = sol[..., :dv]
    k_cd = sol[..., dv:].astype(BF16)
    qk_bd = _pair_blockdiag(_bdot_nt(q_lhs, k_bd) * decay, left)
    q_dec = (qn * eg).astype(BF16)
    g_last = gc[:, CHUNK - 1:CHUNK, :]
    k_tail = (kn * jnp.exp(g_last - gc)).astype(BF16)
    c_decay = jnp.exp(g_last)

    for cc in range(n_chunks):
        hs = slice(cc * heads, (cc + 1) * heads)
        ps = slice(cc * pairs, (cc + 1) * pairs)
        rs = chunk_rows(cc)
        s = s_ref[...]
        sb = s.astype(BF16)
        u = w_val[hs] - _bdot(k_cd[hs], sb)
        ub = u.astype(BF16)
        intra = _bdot(qk_bd[ps], ub.reshape(pairs, 2 * CHUNK, dv))
        o = _bdot(q_dec[hs], sb) + intra.reshape(heads, CHUNK, dv)
        s_ref[...] = s * c_decay[hs] + _bdot_tn(k_tail[hs], ub)

        o = o * lax.rsqrt(jnp.mean(o * o, axis=-1, keepdims=True) + EPS) * on_ref[...]
        for h in range(heads):
            z = z_ref[rs, h * dv:(h + 1) * dv]
            dn_ref[rs, h * dv:(h + 1) * dv] = o[h] * _silu(z)

    @pl.when(c == pl.num_programs(1) - 1)
    def _():
        sfin_ref[0] = s_ref[...]


def deltanet_prompt(proj, col, conv_w, a_log, dt_bias, o_norm, batch, seq, heads, dk, dv, rows):
    conv_ch = 2 * heads * dk + heads * dv
    zw = heads * dv
    steps = seq // rows

    def row_map(width_off):
        return lambda b, c: (b * steps + c, width_off)

    body = functools.partial(_dn_prompt_body, rows=rows, heads=heads, dk=dk, dv=dv)
    return pl.pallas_call(
        body,
        out_shape=(jax.ShapeDtypeStruct((batch * seq, zw), F32),
                   jax.ShapeDtypeStruct((batch, heads, dk, dv), F32)),
        grid=(batch, steps),
        in_specs=[
            pl.BlockSpec((rows, conv_ch), row_map(col["qkv"] // conv_ch)),
            pl.BlockSpec((rows, zw), row_map(col["z"] // zw)),
            pl.BlockSpec((rows, LANE), row_map(col["b"] // LANE)),
            pl.BlockSpec((rows, LANE), row_map(col["a"] // LANE)),
            _resident((CONV_WIDTH, conv_ch)),
            _resident((1, LANE)),
            _resident((1, LANE)),
            _resident((1, dv)),
        ],
        out_specs=(pl.BlockSpec((rows, zw), lambda b, c: (b * steps + c, 0)),
                   pl.BlockSpec((1, heads, dk, dv), lambda b, c: (b, 0, 0, 0))),
        scratch_shapes=[
            pltpu.VMEM((heads, dk, dv), F32),
            pltpu.VMEM((rows + SUBLANE, conv_ch), F32),
            pltpu.VMEM((rows, conv_ch), F32),
        ],
        compiler_params=_params("parallel", "arbitrary"),
        name="deltanet_prompt",
    )(proj, proj, proj, proj, conv_w, a_log, dt_bias, o_norm)


def _dn_sample_body(qkv_ref, z_ref, bl_ref, al_ref, cbuf_ref, st_ref, cw_ref, alog_ref, dtb_ref, on_ref,
                    dn_ref, sto_ref, cbo_ref, act_ref, *, nb, heads, dk, dv):
    wq = heads * dk
    new = qkv_ref[...]
    y = cbuf_ref[:, 0, :] * cw_ref[0:1, :]
    for j in range(1, CONV_WIDTH - 1):
        y = y + cbuf_ref[:, j, :] * cw_ref[j:j + 1, :]
    y = y + new * cw_ref[CONV_WIDTH - 1:CONV_WIDTH, :]
    act_ref[...] = _silu(y)
    for j in range(CONV_WIDTH - 2):
        cbo_ref[:, j, :] = cbuf_ref[:, j + 1, :]
    cbo_ref[:, CONV_WIDTH - 2, :] = new

    beta = jax.nn.sigmoid(bl_ref[...])
    g = -jnp.exp(alog_ref[...]) * jax.nn.softplus(al_ref[...] + dtb_ref[...])
    dec = jnp.exp(g)

    q_t, k_t = [], []
    for h in range(heads):
        q = act_ref[:, h * dk:(h + 1) * dk]
        k = act_ref[:, wq + h * dk:wq + (h + 1) * dk]
        qn = q * lax.rsqrt(jnp.sum(q * q, axis=-1, keepdims=True) + EPS) * (dk ** -0.5)
        kn = k * lax.rsqrt(jnp.sum(k * k, axis=-1, keepdims=True) + EPS)
        q_t.append(qn.T)
        k_t.append(kn.T)

    def per_head(fn):
        return jnp.stack([fn(h) for h in range(heads)], axis=0)

    for i in range(nb):
        row = slice(i, i + 1)
        kcol = per_head(lambda h: k_t[h][:, i:i + 1])
        qcol = per_head(lambda h: q_t[h][:, i:i + 1])
        v = per_head(lambda h: act_ref[row, 2 * wq + h * dv:2 * wq + (h + 1) * dv])
        z = per_head(lambda h: z_ref[row, h * dv:(h + 1) * dv])
        b_i = per_head(lambda h: beta[row, h:h + 1])
        d_i = per_head(lambda h: dec[row, h:h + 1])
        s = st_ref[i] * d_i
        u = b_i * (v - jnp.sum(kcol * s, axis=1, keepdims=True))
        s = s + kcol * u
        o = jnp.sum(qcol * s, axis=1, keepdims=True)
        sto_ref[i] = s
        o = o * lax.rsqrt(jnp.mean(o * o, axis=-1, keepdims=True) + EPS) * on_ref[...]
        o = o * _silu(z)
        for h in range(heads):
            dn_ref[row, h * dv:(h + 1) * dv] = o[h]


def deltanet_sample(proj, col, conv_buf, state, conv_w, a_log, dt_bias, o_norm, heads, dk, dv, nb):
    n_seq = proj.shape[0]
    conv_ch = 2 * heads * dk + heads * dv
    zw = heads * dv
    body = functools.partial(_dn_sample_body, nb=nb, heads=heads, dk=dk, dv=dv)
    return pl.pallas_call(
        body,
        out_shape=(jax.ShapeDtypeStruct((n_seq, zw), F32),
                   jax.ShapeDtypeStruct(state.shape, F32),
                   jax.ShapeDtypeStruct(conv_buf.shape, F32)),
        grid=(n_seq // nb,),
        in_specs=[
            pl.BlockSpec((nb, conv_ch), lambda i: (i, col["qkv"] // conv_ch)),
            pl.BlockSpec((nb, zw), lambda i: (i, col["z"] // zw)),
            pl.BlockSpec((nb, LANE), lambda i: (i, col["b"] // LANE)),
            pl.BlockSpec((nb, LANE), lambda i: (i, col["a"] // LANE)),
            pl.BlockSpec((nb, CONV_WIDTH - 1, conv_ch), lambda i: (i, 0, 0)),
            pl.BlockSpec((nb, heads, dk, dv), lambda i: (i, 0, 0, 0)),
            _resident((CONV_WIDTH, conv_ch)),
            _resident((1, LANE)),
            _resident((1, LANE)),
            _resident((1, dv)),
        ],
        out_specs=(pl.BlockSpec((nb, zw), lambda i: (i, 0)),
                   pl.BlockSpec((nb, heads, dk, dv), lambda i: (i, 0, 0, 0)),
                   pl.BlockSpec((nb, CONV_WIDTH - 1, conv_ch), lambda i: (i, 0, 0))),
        scratch_shapes=[
            pltpu.VMEM((nb, conv_ch), F32),
        ],
        compiler_params=_params("parallel"),
        name="deltanet_sample",
    )(proj, proj, proj, proj, conv_buf, state, conv_w, a_log, dt_bias, o_norm)


def _xattn_prompt_body(q_ref, k_ref, v_ref, o_ref, *, heads, dh):
    for h in range(heads):
        sl = slice(h * dh, (h + 1) * dh)
        q = q_ref[:, sl].astype(BF16)
        k = k_ref[:, sl].astype(BF16)
        v = v_ref[:, sl].astype(BF16)
        s = lax.dot_general(q, k, (((1,), (1,)), ((), ())), preferred_element_type=F32) * (dh ** -0.5)
        e = jnp.exp(s - jnp.max(s, axis=-1, keepdims=True))
        p = e / jnp.sum(e, axis=-1, keepdims=True)
        o_ref[:, sl] = _dot(p.astype(BF16), v)


def xattn_prompt(proj, cq_col, kv, batch, seq, n_mem, heads, dh, tq):
    xw = heads * dh
    steps = seq // tq
    body = functools.partial(_xattn_prompt_body, heads=heads, dh=dh)
    return pl.pallas_call(
        body,
        out_shape=jax.ShapeDtypeStruct((batch * seq, xw), F32),
        grid=(batch, steps),
        in_specs=[
            pl.BlockSpec((tq, xw), lambda b, i: (b * steps + i, cq_col // xw)),
            pl.BlockSpec((n_mem, xw), lambda b, i: (b, 0)),
            pl.BlockSpec((n_mem, xw), lambda b, i: (b, 1)),
        ],
        out_specs=pl.BlockSpec((tq, xw), lambda b, i: (b * steps + i, 0)),
        compiler_params=_params("parallel", "parallel"),
        name="xattn_prompt",
    )(proj, kv, kv)


def _xattn_sample_body(q_ref, k_ref, v_ref, o_ref, *, nb, heads, dh):
    rep = SUBLANE // heads
    n_t = k_ref.shape[2] // SUBLANE

    def over_groups(x, op):
        out = x
        for t in range(1, rep):
            out = op(out, pltpu.roll(x, t * heads, 0))
        return out

    for i in range(nb):
        row = slice(i, i + 1)
        q8 = jnp.concatenate([q_ref[row, h * dh:(h + 1) * dh] for h in range(heads)] * rep, axis=0)
        k3 = k_ref[0, i].reshape(n_t, SUBLANE, dh)
        v3 = v_ref[0, i].reshape(n_t, SUBLANE, dh)
        s = jnp.sum(k3 * q8[None], axis=-1, keepdims=True) * (dh ** -0.5)
        m = over_groups(jnp.max(s, axis=0), jnp.maximum)
        e = jnp.exp(s - m[None])
        den = over_groups(jnp.sum(e, axis=0), jnp.add)
        p = e / den[None]
        acc = over_groups(jnp.sum(p * v3, axis=0), jnp.add)
        for h in range(heads):
            o_ref[row, h * dh:(h + 1) * dh] = acc[h:h + 1, :]


def xattn_sample(proj, cq_col, cache_k, cache_v, layer, nb):
    n_layers, n_seq, n_mem, heads, dh = cache_k.shape
    assert SUBLANE % heads == 0 and (n_mem * heads) % SUBLANE == 0
    xw = heads * dh
    body = functools.partial(_xattn_sample_body, nb=nb, heads=heads, dh=dh)
    cache_spec = pl.BlockSpec((1, nb, n_mem * heads, dh), lambda i: (layer, i, 0, 0))
    flat = (n_layers, n_seq, n_mem * heads, dh)
    return pl.pallas_call(
        body,
        out_shape=jax.ShapeDtypeStruct((n_seq, xw), F32),
        grid=(n_seq // nb,),
        in_specs=[pl.BlockSpec((nb, xw), lambda i: (i, cq_col // xw)), cache_spec, cache_spec],
        out_specs=pl.BlockSpec((nb, xw), lambda i: (i, 0)),
        compiler_params=_params("parallel"),
        name="xattn_sample",
    )(proj, cache_k.reshape(flat), cache_v.reshape(flat))


def _pool_mix(sums, cnts, u_of, pw_ref, ps_ref, o_ref, gw):
    for gi in range(len(POOL_WINDOWS)):
        cols = slice(gi * gw, (gi + 1) * gw)
        d = sums[gi] / cnts[gi] - u_of(cols)
        o_ref[:, cols] = _wdot(d, pw_ref[gi]) * ps_ref[:, cols]


def _pool_prompt_body(u_ref, pw_ref, ps_ref, o_ref, xbuf, s2, s4, s8, *, tt, gw):
    t = pl.program_id(1)
    n = POOL_HALO + tt

    @pl.when(t == 0)
    def _():
        xbuf[0:POOL_HALO, :] = jnp.zeros((POOL_HALO, xbuf.shape[1]), F32)

    xbuf[POOL_HALO:n, :] = u_ref[...]
    s2[8:n, :] = xbuf[8:n, :] + xbuf[7:n - 1, :]
    s4[16:n, :] = s2[16:n, gw:] + s2[14:n - 2, gw:]
    s8[24:n, :] = s4[24:n, gw:] + s4[20:n - 4, gw:]
    s16 = s8[32:n, gw:] + s8[24:n - 8, gw:]
    sums = [s2[POOL_HALO:n, 0:gw], s4[POOL_HALO:n, 0:gw], s8[POOL_HALO:n, 0:gw], s16]
    pos = t * tt + lax.broadcasted_iota(jnp.int32, (tt, 1), 0)
    cnts = [jnp.minimum(pos + 1, w).astype(F32) for w in POOL_WINDOWS]
    _pool_mix(sums, cnts, lambda cols: u_ref[:, cols], pw_ref, ps_ref, o_ref, gw)
    xbuf[0:POOL_HALO, :] = xbuf[tt:n, :]


def pool_prompt(proj, pool_w, pool_scale, batch, seq, tt):
    pw_total = pool_scale.shape[-1]
    gw = pw_total // len(POOL_WINDOWS)
    steps = seq // tt
    n = POOL_HALO + tt
    body = functools.partial(_pool_prompt_body, tt=tt, gw=gw)
    return pl.pallas_call(
        body,
        out_shape=jax.ShapeDtypeStruct((batch * seq, pw_total), F32),
        grid=(batch, steps),
        in_specs=[
            pl.BlockSpec((tt, pw_total), lambda b, t: (b * steps + t, 0)),
            _resident(pool_w.shape),
            _resident((1, pw_total)),
        ],
        out_specs=pl.BlockSpec((tt, pw_total), lambda b, t: (b * steps + t, 0)),
        scratch_shapes=[
            pltpu.VMEM((n, pw_total), F32),
            pltpu.VMEM((n, pw_total), F32),
            pltpu.VMEM((n, pw_total - gw), F32),
            pltpu.VMEM((n, pw_total - 2 * gw), F32),
        ],
        compiler_params=_params("parallel", "arbitrary"),
        name="pool_prompt",
    )(proj, pool_w, pool_scale)


def _pool_sample_body(u_ref, st_ref, pw_ref, ps_ref, o_ref, sto_ref, *, gw, buf):
    pw_total = gw * len(POOL_WINDOWS)
    new = u_ref[:, 0:pw_total]
    sums, cnts = [], []
    for gi, w in enumerate(POOL_WINDOWS):
        cols = slice(gi * gw, (gi + 1) * gw)
        s = new[:, cols]
        for i in range(1, w):
            s = s + st_ref[:, buf - i, cols]
        sums.append(s)
        cnts.append(float(min(PAST_LEN + 1, w)))
    _pool_mix(sums, cnts, lambda cols: new[:, cols], pw_ref, ps_ref, o_ref, gw)
    sto_ref[:, 0:buf - 1, :] = st_ref[:, 1:buf, :]
    sto_ref[:, buf - 1, :] = new


def pool_sample(proj, state, pool_w, pool_scale):
    n_seq, buf, pw_total = state.shape
    gw = pw_total // len(POOL_WINDOWS)
    body = functools.partial(_pool_sample_body, gw=gw, buf=buf)
    return pl.pallas_call(
        body,
        out_shape=(jax.ShapeDtypeStruct((n_seq, pw_total), F32),
                   jax.ShapeDtypeStruct(state.shape, F32)),
        compiler_params=pltpu.CompilerParams(vmem_limit_bytes=VMEM_LIMIT_BYTES),
        name="pool_sample",
    )(proj, state, pool_w, pool_scale)


def _route_body(h_ref, g_ref, r_ref, idx_ref, gate_ref, *, n_experts):
    xn = _rms(h_ref[...], g_ref[...])
    logits = jnp.dot(xn, r_ref[...], precision=HIGHEST, preferred_element_type=F32)
    lane = lax.broadcasted_iota(jnp.int32, logits.shape, 1)
    lm = jnp.where(lane < n_experts, logits, -jnp.inf)
    m1 = jnp.max(lm, axis=1, keepdims=True)
    i1 = jnp.min(jnp.where(lm == m1, lane, LANE), axis=1, keepdims=True)
    lm2 = jnp.where(lane == i1, -jnp.inf, lm)
    m2 = jnp.max(lm2, axis=1, keepdims=True)
    i2 = jnp.min(jnp.where(lm2 == m2, lane, LANE), axis=1, keepdims=True)
    e2 = jnp.exp(m2 - m1)
    den = 1.0 + e2
    idx_ref[...] = jnp.where(lane == 0, i1, jnp.where(lane == 1, i2, 0))
    gate_ref[...] = jnp.where(lane == 0, 1.0 / den, jnp.where(lane == 1, e2 / den, 0.0))


def moe_route(h_all, row_offset, rows, g, r_pad, tm, n_experts, name):
    d = h_all.shape[1]
    off = row_offset // tm
    body = functools.partial(_route_body, n_experts=n_experts)
    return pl.pallas_call(
        body,
        out_shape=(jax.ShapeDtypeStruct((rows, LANE), jnp.int32), jax.ShapeDtypeStruct((rows, LANE), F32)),
        grid=(rows // tm,),
        in_specs=[pl.BlockSpec((tm, d), lambda i: (i + off, 0)), _resident((1, d)), _resident((d, LANE))],
        out_specs=(pl.BlockSpec((tm, LANE), lambda i: (i, 0)), pl.BlockSpec((tm, LANE), lambda i: (i, 0))),
        compiler_params=_params("parallel"),
        name=name,
    )(h_all, g.reshape(1, d), r_pad)


def _row_gather_start(src_hbm, idx_ref, dst, sem):
    def body(t, carry):
        for u in range(SUBLANE):
            row = idx_ref[0, 0, t * SUBLANE + u]
            pltpu.make_async_copy(src_hbm.at[pl.ds(row, 1)], dst.at[t, pl.ds(u, 1)], sem).start()
        return carry

    lax.fori_loop(0, dst.shape[0], body, 0)


def _row_gather_wait(src_hbm, dst, sem):
    def body(t, carry):
        pltpu.make_async_copy(src_hbm.at[pl.ds(0, SUBLANE)], dst.at[t], sem).wait()
        return carry

    lax.fori_loop(0, dst.shape[0], body, 0)


def _experts_body(te_ref, nu_ref, src_cur_ref, src_nxt_ref, h_hbm, g_ref, wg_ref, wu_ref, wo_ref,
                  o_ref, xbuf, xn_ref, sem, *, tm):
    i = pl.program_id(0)
    j = pl.program_id(1)
    n_used = nu_ref[0]
    slot = i % 2
    valid = i < n_used

    @pl.when(j == 0)
    def _():
        @pl.when(i == 0)
        def _():
            _row_gather_start(h_hbm, src_cur_ref, xbuf.at[0], sem.at[0])

        @pl.when(i + 1 < n_used)
        def _():
            _row_gather_start(h_hbm, src_nxt_ref, xbuf.at[1 - slot], sem.at[1 - slot])

        @pl.when(valid)
        def _():
            _row_gather_wait(h_hbm, xbuf.at[slot], sem.at[slot])
            xn_ref[...] = _rms(xbuf[slot].reshape(xn_ref.shape), g_ref[...]).astype(BF16)

        @pl.when(jnp.logical_not(valid))
        def _():
            o_ref[...] = jnp.zeros_like(o_ref)

    @pl.when(valid)
    def _():
        xn = xn_ref[...]
        a = _dot(xn, wg_ref[0].astype(BF16))
        b = _dot(xn, wu_ref[0].astype(BF16))
        part = _dot((_silu(a) * b).astype(BF16), wo_ref[0].astype(BF16))

        @pl.when(j == 0)
        def _():
            o_ref[...] = part

        @pl.when(j > 0)
        def _():
            o_ref[...] += part


def moe_experts(h_all, g, wi, wo, src, tile_expert, n_used, tm, tf):
    d = h_all.shape[1]
    f = wo.shape[1]
    nj = f // tf
    n_tiles = src.shape[0]

    def wblock(col_off):
        def index_map(i, j, te, nu):
            jj = jnp.where(i < nu[0], j, nj - 1)
            return (te[i], 0, col_off + jj)
        return index_map

    def woblock(i, j, te, nu):
        return (te[i], jnp.where(i < nu[0], j, nj - 1), 0)

    def smem_tile(index_map):
        return pl.BlockSpec((1, 1, tm), index_map, memory_space=pltpu.SMEM)

    grid_spec = pltpu.PrefetchScalarGridSpec(
        num_scalar_prefetch=2,
        grid=(n_tiles, nj),
        in_specs=[
            smem_tile(lambda i, j, te, nu: (i, 0, 0)),
            smem_tile(lambda i, j, te, nu: (jnp.minimum(i + 1, n_tiles - 1), 0, 0)),
            pl.BlockSpec(memory_space=pl.ANY),
            pl.BlockSpec((1, d), lambda i, j, te, nu: (0, 0)),
            pl.BlockSpec((1, d, tf), wblock(0)),
            pl.BlockSpec((1, d, tf), wblock(nj)),
            pl.BlockSpec((1, tf, d), woblock),
        ],
        out_specs=pl.BlockSpec((tm, d), lambda i, j, te, nu: (i, 0)),
        scratch_shapes=[
            pltpu.VMEM((2, tm // SUBLANE, SUBLANE, d), F32),
            pltpu.VMEM((tm, d), BF16),
            pltpu.SemaphoreType.DMA((2,)),
        ],
    )
    return pl.pallas_call(
        functools.partial(_experts_body, tm=tm),
        out_shape=jax.ShapeDtypeStruct((n_tiles * tm, d), F32),
        grid_spec=grid_spec,
        compiler_params=_params("arbitrary", "arbitrary"),
        name="moe_experts",
    )(tile_expert, n_used, src, src, h_all, g.reshape(1, d), wi, wi, wo)


def _combine_body(d1c_ref, d2c_ref, d1n_ref, d2n_ref, h_ref, gate_ref, fn_ref, ys_hbm, o_ref, ybuf, sem,
                  *, tb, final_norm):
    i = pl.program_id(0)
    n = pl.num_programs(0)
    slot = i % 2

    def start(d1_ref, d2_ref, s):
        _row_gather_start(ys_hbm, d1_ref, ybuf.at[s, 0], sem.at[s])
        _row_gather_start(ys_hbm, d2_ref, ybuf.at[s, 1], sem.at[s])

    @pl.when(i == 0)
    def _():
        start(d1c_ref, d2c_ref, 0)

    @pl.when(i + 1 < n)
    def _():
        start(d1n_ref, d2n_ref, 1 - slot)

    _row_gather_wait(ys_hbm, ybuf.at[slot, 0], sem.at[slot])
    _row_gather_wait(ys_hbm, ybuf.at[slot, 1], sem.at[slot])
    gate = gate_ref[...]
    y1 = ybuf[slot, 0].reshape(o_ref.shape)
    y2 = ybuf[slot, 1].reshape(o_ref.shape)
    out = h_ref[...] + (gate[:, 0:1] * y1 + gate[:, 1:2] * y2)
    if final_norm:
        out = _rms(out, fn_ref[...])
    o_ref[...] = out


def moe_combine(h_all, row_offset, rows, gates, dest1, dest2, ys, fn, tb, final_norm, name):
    d = h_all.shape[1]
    off = row_offset // tb
    n = rows // tb

    def smem_tile(index_map):
        return pl.BlockSpec((1, 1, tb), index_map, memory_space=pltpu.SMEM)

    def cur(i):
        return (i, 0, 0)

    def nxt(i):
        return (jnp.minimum(i + 1, n - 1), 0, 0)

    d1 = dest1.reshape(n, 1, tb)
    d2 = dest2.reshape(n, 1, tb)
    body = functools.partial(_combine_body, tb=tb, final_norm=final_norm)
    return pl.pallas_call(
        body,
        out_shape=jax.ShapeDtypeStruct((rows, d), F32),
        grid=(n,),
        in_specs=[
            smem_tile(cur), smem_tile(cur), smem_tile(nxt), smem_tile(nxt),
            pl.BlockSpec((tb, d), lambda i: (i + off, 0)),
            pl.BlockSpec((tb, LANE), lambda i: (i, 0)),
            _resident((1, d)),
            pl.BlockSpec(memory_space=pl.ANY),
        ],
        out_specs=pl.BlockSpec((tb, d), lambda i: (i, 0)),
        scratch_shapes=[pltpu.VMEM((2, 2, tb // SUBLANE, SUBLANE, d), F32), pltpu.SemaphoreType.DMA((2,))],
        compiler_params=_params("arbitrary"),
        name=name,
    )(d1, d2, d1, d2, h_all, gates, fn.reshape(1, d), ys)


def moe_plan(idx, n_experts, tm):
    n_tok, k = idx.shape
    n_slots = n_tok * k
    n_tiles = n_slots // tm + n_experts
    e_flat = idx.reshape(n_slots)
    onehot = (e_flat[:, None] == jnp.arange(n_experts, dtype=jnp.int32)[None, :]).astype(jnp.int32)
    csum = jnp.cumsum(onehot, axis=0)
    counts = csum[-1]
    rank = jnp.sum(csum * onehot, axis=1) - 1
    tiles_e = (counts + tm - 1) // tm
    tile_end = jnp.cumsum(tiles_e)
    pstart = (tile_end - tiles_e) * tm
    dest = jnp.sum(onehot * pstart[None, :], axis=1) + rank
    n_used = tile_end[-1]
    tile_ids = jnp.arange(n_tiles, dtype=jnp.int32)
    tile_expert = jnp.sum((tile_ids[:, None] >= tile_end[None, :]).astype(jnp.int32), axis=1)
    last_expert = jnp.sum((n_used - 1 >= tile_end).astype(jnp.int32))
    tile_expert = jnp.where(tile_ids < n_used, tile_expert, last_expert).astype(jnp.int32)
    tok = jnp.arange(n_slots, dtype=jnp.int32) // k
    src = jnp.zeros((n_tiles * tm,), jnp.int32).at[dest].set(tok, unique_indices=True)
    return (dest.reshape(n_tok, k).astype(jnp.int32), src.reshape(n_tiles, 1, tm), tile_expert,
            n_used.reshape(1).astype(jnp.int32))


def _pad_lanes(v):
    return jnp.zeros((1, LANE), F32).at[0, :v.shape[0]].set(v)


def kernel(x_prompt, x_sample, state_dn, state_dn_conv, state_pool, cache_mem_k, cache_mem_v, mem_prompt, a_norm1, a_w_in, a_conv_w, a_a_log, a_dt_bias, a_o_norm, a_w_out, a_norm2, a_ffn_wi, a_ffn_wo, b_norm1, b_w_in, b_pool_w, b_pool_scale, b_w_out, b_norm2, b_router, b_moe_wi, b_moe_wo, m_norm, m_w_k, m_w_v, final_norm):
    bp, seq, d = x_prompt.shape
    bs = x_sample.shape[0]
    assert x_sample.shape[1] == 1
    depth = m_norm.shape[0]
    assert depth % 2 == 0, "the output norm is fused into the expert block of the last (odd) layer"
    heads, dk, dv = state_dn.shape[2], state_dn.shape[3], state_dn.shape[4]
    conv_ch = state_dn_conv.shape[-1]
    qk_w, v_w = heads * dk, heads * dv
    assert conv_ch == 2 * qk_w + v_w and dk == LANE and dv == LANE and heads <= LANE
    n_mem, x_heads, x_dh = cache_mem_k.shape[2], cache_mem_k.shape[3], cache_mem_k.shape[4]
    xw = x_heads * x_dh
    pool_w_total = state_pool.shape[-1]
    pool_buf = state_pool.shape[2]
    d_ff = a_ffn_wo.shape[1]

    hp = x_prompt.reshape(bp * seq, d)
    hs = x_sample.reshape(bs, d)

    o1, o2, o3, o4 = conv_ch, conv_ch + v_w, conv_ch + v_w + heads, conv_ch + v_w + 2 * heads
    col_a = {"qkv": 0, "z": conv_ch, "cq": conv_ch + v_w, "b": conv_ch + v_w + xw, "a": conv_ch + v_w + xw + LANE}
    n_a = -(-(col_a["a"] + LANE) // 512) * 512

    dn_p, conv_p, pool_p, mk_all, mv_all = [], [], [], [], []
    dn_s, conv_s, pool_s = [], [], []
    for i in range(depth):
        j = i // 2
        w_kv = jnp.concatenate([m_w_k[i], m_w_v[i]], axis=1).astype(BF16)
        kv = rms_matmul(mem_prompt.reshape(bp * n_mem, d), m_norm[i], w_kv, 256, f"mem_kv_{i}")
        mk_all.append(kv[:, :xw].reshape(bp, n_mem, x_heads, x_dh))
        mv_all.append(kv[:, xw:].reshape(bp, n_mem, x_heads, x_dh))
        if i % 2 == 0:
            w = a_w_in[j]
            w_in = jnp.zeros((d, n_a), F32)
            w_in = w_in.at[:, :o2].set(w[:, :o2])
            w_in = w_in.at[:, col_a["cq"]:col_a["cq"] + xw].set(w[:, o4:])
            w_in = w_in.at[:, col_a["b"]:col_a["b"] + heads].set(w[:, o2:o3])
            w_in = w_in.at[:, col_a["a"]:col_a["a"] + heads].set(w[:, o3:o4])
            a_log, dt_b = _pad_lanes(a_a_log[j]), _pad_lanes(a_dt_bias[j])
            o_norm = a_o_norm[j].reshape(1, dv)

            proj_p = rms_matmul(hp, a_norm1[j], w_in.astype(BF16), 256, f"a_in_p_{i}")
            proj_s = rms_matmul(hs, a_norm1[j], w_in, bs, f"a_in_s_{i}", tn=512)
            mix_p, s_p = deltanet_prompt(proj_p, col_a, a_conv_w[j], a_log, dt_b, o_norm,
                                         bp, seq, heads, dk, dv, rows=DN_STEP_CHUNKS * CHUNK)
            mix_s, s_s, c_s = deltanet_sample(proj_s, col_a, state_dn_conv[j], state_dn[j], a_conv_w[j],
                                              a_log, dt_b, o_norm, heads, dk, dv, nb=SUBLANE)
            xa_p = xattn_prompt(proj_p, col_a["cq"], kv, bp, seq, n_mem, x_heads, x_dh, tq=512)
            xa_s = xattn_sample(proj_s, col_a["cq"], cache_mem_k, cache_mem_v, i, nb=SUBLANE)
            hp = out_proj(hp, mix_p, xa_p, a_w_out[j].astype(BF16), 512, f"a_out_p_{i}")
            hs = out_proj(hs, mix_s, xa_s, a_w_out[j], bs, f"a_out_s_{i}")
            hp = ffn_dense(hp, a_norm2[j], a_ffn_wi[j].astype(BF16), a_ffn_wo[j].astype(BF16), 256,
                           f"a_ffn_p_{i}")
            hs = ffn_dense(hs, a_norm2[j], a_ffn_wi[j], a_ffn_wo[j], bs, f"a_ffn_s_{i}", tf=256)
            dn_p.append(s_p)
            conv_p.append(proj_p.reshape(bp, seq, n_a)[:, seq - (CONV_WIDTH - 1):, :conv_ch])
            dn_s.append(s_s)
            conv_s.append(c_s)
        else:
            ps = b_pool_scale[j].reshape(1, pool_w_total)
            last = i == depth - 1

            proj_p = rms_matmul(hp, b_norm1[j], b_w_in[j].astype(BF16), 512, f"b_in_p_{i}")
            proj_s = rms_matmul(hs, b_norm1[j], b_w_in[j], bs, f"b_in_s_{i}")
            mix_p = pool_prompt(proj_p, b_pool_w[j].astype(BF16), ps, bp, seq, tt=512)
            mix_s, buf_s = pool_sample(proj_s, state_pool[j], b_pool_w[j], ps)
            xa_p = xattn_prompt(proj_p, pool_w_total, kv, bp, seq, n_mem, x_heads, x_dh, tq=512)
            xa_s = xattn_sample(proj_s, pool_w_total, cache_mem_k, cache_mem_v, i, nb=SUBLANE)
            n_p = bp * seq
            h_all = out_proj(hp, mix_p, xa_p, b_w_out[j].astype(BF16), 512, f"b_out_p_{i}", out_rows=n_p + bs)
            h_all = out_proj(hs, mix_s, xa_s, b_w_out[j], bs, f"b_out_s_{i}", out_rows=n_p + bs,
                             into=h_all, row_offset=n_p)
            n_experts = b_router.shape[-1]
            r_pad = jnp.zeros((d, LANE), F32).at[:, :n_experts].set(b_router[j])
            idx_p, gate_p = moe_route(h_all, 0, n_p, b_norm2[j], r_pad, 512, n_experts, f"b_route_p_{i}")
            idx_s, gate_s = moe_route(h_all, n_p, bs, b_norm2[j], r_pad, bs, n_experts, f"b_route_s_{i}")
            idx = jnp.concatenate([idx_p[:, :TOP_K], idx_s[:, :TOP_K]], axis=0)
            dest, src, tile_expert, n_used = moe_plan(idx, n_experts, MOE_TILE)
            ys = moe_experts(h_all, b_norm2[j], b_moe_wi[j], b_moe_wo[j], src, tile_expert, n_used,
                             MOE_TILE, 512)
            hp = moe_combine(h_all, 0, n_p, gate_p, dest[:n_p, 0], dest[:n_p, 1], ys, final_norm,
                             256, last, f"b_combine_p_{i}")
            hs = moe_combine(h_all, n_p, bs, gate_s, dest[n_p:, 0], dest[n_p:, 1], ys, final_norm,
                             bs, last, f"b_combine_s_{i}")
            pool_p.append(proj_p.reshape(bp, seq, -1)[:, seq - pool_buf:, :pool_w_total])
            pool_s.append(buf_s)

    y_prompt = hp.reshape(bp, seq, d)
    y_sample = hs.reshape(bs, 1, d)
    return (y_prompt, y_sample, jnp.stack(dn_p), jnp.stack(conv_p), jnp.stack(pool_p),
            jnp.stack(mk_all), jnp.stack(mv_all), jnp.stack(dn_s), jnp.stack(conv_s), jnp.stack(pool_s))
```

```python
import functools

import jax
import jax.numpy as jnp
from jax import lax
from jax.experimental import pallas as pl
from jax.experimental.pallas import tpu as pltpu

F32 = jnp.float32
BF16 = jnp.bfloat16
HIGHEST = lax.Precision.HIGHEST

EPS = 1e-6
CHUNK = 64
CONV_WIDTH = 4
POOL_WINDOWS = (2, 4, 8, 16)
PAST_LEN = 16384
TOP_K = 2

LANE = 128
SUBLANE = 8
VMEM_LIMIT_BYTES = 56 * 2**20

MOE_TILE = 1024
MOE_FF_TILE = 512
DN_STEP_CHUNKS = 4
INV_BLOCK = 16
POOL_HALO = 32


def _params(*sem):
    return pltpu.CompilerParams(dimension_semantics=sem, vmem_limit_bytes=VMEM_LIMIT_BYTES)


def _resident(shape):
    nd = len(shape)
    return pl.BlockSpec(shape, lambda *_: (0,) * nd, pipeline_mode=pl.Buffered(1))


def _rms(x, w):
    return x * lax.rsqrt(jnp.mean(x * x, axis=-1, keepdims=True) + EPS) * w


def _silu(x):
    return x * jax.nn.sigmoid(x)


def _dot(a, b):
    return jnp.dot(a, b, preferred_element_type=F32)


def _wdot(a, w):
    if w.dtype == F32:
        return jnp.dot(a, w, precision=HIGHEST, preferred_element_type=F32)
    return jnp.dot(a.astype(BF16), w, preferred_element_type=F32)


def _bdot(a, b):
    return lax.dot_general(a, b, (((2,), (1,)), ((0,), (0,))), preferred_element_type=F32)


def _bdot_nt(a, b):
    return lax.dot_general(a, b, (((2,), (2,)), ((0,), (0,))), preferred_element_type=F32)


def _bdot_tn(a, b):
    return lax.dot_general(a, b, (((1,), (1,)), ((0,), (0,))), preferred_element_type=F32)


def _rms_matmul_body(x_ref, g_ref, w_ref, o_ref):
    xn = _rms(x_ref[...], g_ref[...])
    o_ref[...] = _wdot(xn, w_ref[...])


def _wspec(shape, index_map, steps):
    if steps == 1:
        return pl.BlockSpec(shape, index_map, pipeline_mode=pl.Buffered(1))
    return pl.BlockSpec(shape, index_map)


def rms_matmul(x, g, w, tm, name, tn=None):
    m, d = x.shape
    n = w.shape[1]
    tn = n if tn is None else tn
    nj = n // tn
    return pl.pallas_call(
        _rms_matmul_body,
        out_shape=jax.ShapeDtypeStruct((m, n), F32),
        grid=(m // tm, nj),
        in_specs=[pl.BlockSpec((tm, d), lambda i, j: (i, 0)), _resident((1, d)),
                  _wspec((d, tn), lambda i, j: (0, j), nj)],
        out_specs=pl.BlockSpec((tm, tn), lambda i, j: (i, j)),
        compiler_params=_params("parallel", "arbitrary"),
        name=name,
    )(x, g.reshape(1, d), w)


def _out_proj_body(x_ref, a1_ref, a2_ref, w1_ref, w2_ref, *rest, n_main):
    o_ref = rest[-1]

    @pl.when(pl.program_id(0) < n_main)
    def _():
        acc = _wdot(a1_ref[...], w1_ref[...]) + _wdot(a2_ref[...], w2_ref[...])
        o_ref[...] = x_ref[...] + acc

    @pl.when(pl.program_id(0) >= n_main)
    def _():
        o_ref[...] = jnp.zeros_like(o_ref)


def out_proj(x, a1, a2, w, tm, name, out_rows=None, into=None, row_offset=0):
    m, d = x.shape
    k1, k2 = a1.shape[1], a2.shape[1]
    assert w.shape[0] == k1 + k2 and k1 % k2 == 0 and row_offset % tm == 0
    out_rows = m if out_rows is None else out_rows
    off = row_offset // tm
    n_main = m // tm
    n_extra = 0 if (into is not None or out_rows == m) else 1
    assert out_rows - m <= n_extra * tm or into is not None
    last = n_main - 1
    in_specs = [
        pl.BlockSpec((tm, d), lambda i: (jnp.minimum(i, last), 0)),
        pl.BlockSpec((tm, k1), lambda i: (jnp.minimum(i, last), 0)),
        pl.BlockSpec((tm, k2), lambda i: (jnp.minimum(i, last), 0)),
        pl.BlockSpec((k1, d), lambda i: (0, 0), pipeline_mode=pl.Buffered(1)),
        pl.BlockSpec((k2, d), lambda i: (k1 // k2, 0), pipeline_mode=pl.Buffered(1)),
    ]
    args = [x, a1, a2, w, w]
    aliases = {}
    if into is not None:
        in_specs.append(pl.BlockSpec(memory_space=pl.ANY))
        args.append(into)
        aliases = {len(args) - 1: 0}
    return pl.pallas_call(
        functools.partial(_out_proj_body, n_main=n_main),
        out_shape=jax.ShapeDtypeStruct((out_rows, d), F32),
        grid=(n_main + n_extra,),
        in_specs=in_specs,
        out_specs=pl.BlockSpec((tm, d), lambda i: (i + off, 0)),
        input_output_aliases=aliases,
        compiler_params=_params("parallel"),
        name=name,
    )(*args)


def _ffn_body(h_ref, g_ref, wg_ref, wu_ref, wo_ref, o_ref, xn_ref):
    @pl.when(pl.program_id(1) == 0)
    def _():
        h = h_ref[...]
        xn_ref[...] = _rms(h, g_ref[...])
        o_ref[...] = h

    xn = xn_ref[...]
    a = _wdot(xn, wg_ref[...])
    b = _wdot(xn, wu_ref[...])
    o_ref[...] += _wdot(_silu(a) * b, wo_ref[...])


def ffn_dense(h, g, wi, wo, tm, name, tf=None):
    m, d = h.shape
    f = wo.shape[0]
    tf = f if tf is None else tf
    nj = f // tf
    return pl.pallas_call(
        _ffn_body,
        out_shape=jax.ShapeDtypeStruct((m, d), F32),
        grid=(m // tm, nj),
        in_specs=[
            pl.BlockSpec((tm, d), lambda i, j: (i, 0)),
            _resident((1, d)),
            _wspec((d, tf), lambda i, j: (0, j), nj),
            _wspec((d, tf), lambda i, j: (0, nj + j), nj),
            _wspec((tf, d), lambda i, j: (j, 0), nj),
        ],
        out_specs=pl.BlockSpec((tm, d), lambda i, j: (i, 0)),
        scratch_shapes=[pltpu.VMEM((tm, d), F32)],
        compiler_params=_params("parallel", "arbitrary"),
        name=name,
    )(h, g.reshape(1, d), wi, wi, wo)


def _pair_blockdiag(y, left):
    yb = y.astype(BF16)
    zero = jnp.zeros_like(yb)
    return jnp.concatenate([jnp.where(left, yb, zero), jnp.where(left, zero, yb)], axis=1)


def _pair_mm(x, y, left):
    return _bdot(x.astype(BF16), _pair_blockdiag(y, left))


def _unit_lower_inverse(mx, eye, blk, left):
    mm = functools.partial(_pair_mm, left=left)
    md = jnp.where(blk, mx, 0.0)
    c = mx - md
    p = eye - md
    m2 = mm(md, md)
    p = p + mm(p, m2)
    m4 = mm(m2, m2)
    p = p + mm(p, m4)
    m8 = mm(m4, m4)
    td = p + mm(p, m8)
    n = mm(td, c)
    n2 = mm(n, n)
    q = eye - n
    q = q + mm(q, n2)
    return mm(q, td)


def _dn_prompt_body(qkv_ref, z_ref, bl_ref, al_ref, cw_ref, alog_ref, dtb_ref, on_ref,
                    dn_ref, sfin_ref, s_ref, xbuf_ref, act_ref, *, rows, heads, dk, dv):
    c = pl.program_id(1)
    wq = heads * dk

    @pl.when(c == 0)
    def _():
        s_ref[...] = jnp.zeros_like(s_ref)
        xbuf_ref[0:SUBLANE, :] = jnp.zeros((SUBLANE, xbuf_ref.shape[1]), F32)

    xbuf_ref[SUBLANE:SUBLANE + rows, :] = qkv_ref[...]
    first = SUBLANE - (CONV_WIDTH - 1)
    y = xbuf_ref[pl.ds(first, rows), :] * cw_ref[0:1, :]
    for j in range(1, CONV_WIDTH):
        y = y + xbuf_ref[pl.ds(first + j, rows), :] * cw_ref[j:j + 1, :]
    act_ref[...] = _silu(y)
    xbuf_ref[0:SUBLANE, :] = xbuf_ref[rows:rows + SUBLANE, :]

    pairs = heads // 2
    ri = lax.broadcasted_iota(jnp.int32, (CHUNK, 2 * CHUNK), 0)
    lane = lax.broadcasted_iota(jnp.int32, (CHUNK, 2 * CHUNK), 1)
    cj = lane % CHUNK
    left = (lane < CHUNK)[None]
    incl = (ri >= cj)[None]
    strict = (ri > cj)[None]
    blk = ((ri // INV_BLOCK) == (cj // INV_BLOCK))[None]
    eye = (ri == cj).astype(F32)[None]
    tr = lax.broadcasted_iota(jnp.int32, (CHUNK, CHUNK), 0)
    tc = lax.broadcasted_iota(jnp.int32, (CHUNK, CHUNK), 1)
    tri = (tr >= tc).astype(F32)

    n_chunks = rows // CHUNK
    zero = jnp.zeros((CHUNK, dk), BF16)

    def chunk_rows(cc):
        return slice(cc * CHUNK, (cc + 1) * CHUNK)

    def per_head(fn):
        return jnp.stack([fn(chunk_rows(cc), h) for cc in range(n_chunks) for h in range(heads)], axis=0)

    def per_pair(fn):
        return jnp.stack([fn(cc, p) for cc in range(n_chunks) for p in range(pairs)], axis=0)

    q3 = per_head(lambda rs, h: act_ref[rs, h * dk:(h + 1) * dk])
    k3 = per_head(lambda rs, h: act_ref[rs, wq + h * dk:wq + (h + 1) * dk])
    v3 = per_head(lambda rs, h: act_ref[rs, 2 * wq + h * dv:2 * wq + (h + 1) * dv])
    qn = q3 * lax.rsqrt(jnp.sum(q3 * q3, axis=-1, keepdims=True) + EPS) * (dk ** -0.5)
    kn = k3 * lax.rsqrt(jnp.sum(k3 * k3, axis=-1, keepdims=True) + EPS)

    beta = jax.nn.sigmoid(bl_ref[...])
    g = -jnp.exp(alog_ref[...]) * jax.nn.softplus(al_ref[...] + dtb_ref[...])
    gcum = [jnp.dot(tri, g[chunk_rows(cc)], precision=HIGHEST, preferred_element_type=F32)
            for cc in range(n_chunks)]
    gcum_t = [jnp.concatenate([gc_, pltpu.roll(gc_, LANE - 1, 1)], axis=0).T for gc_ in gcum]

    def pair_cols(x_of):
        return per_pair(lambda cc, p: jnp.where(left[0], x_of(cc)[:, 2 * p:2 * p + 1],
                                                x_of(cc)[:, 2 * p + 1:2 * p + 2]))

    gc2 = pair_cols(lambda cc: gcum[cc])
    bc2 = pair_cols(lambda cc: beta[chunk_rows(cc)])
    gr2 = per_pair(lambda cc, p: gcum_t[cc][2 * p:2 * p + 1, :])
    decay = jnp.where(incl, jnp.exp(gc2 - gr2), 0.0)

    kb = kn.astype(BF16)
    qb = qn.astype(BF16)

    def pair_of(x, cc, p):
        return x[cc * heads + 2 * p], x[cc * heads + 2 * p + 1]

    k_lhs = per_pair(lambda cc, p: jnp.concatenate(pair_of(kb, cc, p), axis=1))
    q_lhs = per_pair(lambda cc, p: jnp.concatenate(pair_of(qb, cc, p), axis=1))
    k_bd = per_pair(lambda cc, p: jnp.concatenate(
        [jnp.concatenate([pair_of(kb, cc, p)[0], zero], axis=1),
         jnp.concatenate([zero, pair_of(kb, cc, p)[1]], axis=1)], axis=0))
    kk = _bdot_nt(k_lhs, k_bd)
    mx = jnp.where(strict, bc2 * kk * decay, 0.0)
    tinv = _unit_lower_inverse(mx, eye, blk, left)

    gcum_rows = jnp.concatenate(gcum, axis=0)
    gc = per_head(lambda rs, h: gcum_rows[rs, h:h + 1])
    bc = per_head(lambda rs, h: beta[rs, h:h + 1])
    eg = jnp.exp(gc)
    rhs = jnp.concatenate([v3 * bc, kn * (bc * eg)], axis=-1)
    sol = _bdot(_pair_blockdiag(tinv, left), rhs.astype(BF16).reshape(n_chunks * pairs, 2 * CHUNK, dv + dk))
    sol = sol.reshape(n_chunks * heads, CHUNK, dv + dk)
    w_val = sol[..., :dv]
    k_cd = sol[..., dv:].astype(BF16)
    qk_bd = _pair_blockdiag(_bdot_nt(q_lhs, k_bd) * decay, left)
    q_dec = (qn * eg).astype(BF16)
    g_last = gc[:, CHUNK - 1:CHUNK, :]
    k_tail = (kn * jnp.exp(g_last - gc)).astype(BF16)
    c_decay = jnp.exp(g_last)

    for cc in range(n_chunks):
        hs = slice(cc * heads, (cc + 1) * heads)
        ps = slice(cc * pairs, (cc + 1) * pairs)
        rs = chunk_rows(cc)
        s = s_ref[...]
        sb = s.astype(BF16)
        u = w_val[hs] - _bdot(k_cd[hs], sb)
        ub = u.astype(BF16)
        intra = _bdot(qk_bd[ps], ub.reshape(pairs, 2 * CHUNK, dv))
        o = _bdot(q_dec[hs], sb) + intra.reshape(heads, CHUNK, dv)
        s_ref[...] = s * c_decay[hs] + _bdot_tn(k_tail[hs], ub)

        o = o * lax.rsqrt(jnp.mean(o * o, axis=-1, keepdims=True) + EPS) * on_ref[...]
        for h in range(heads):
            z = z_ref[rs, h * dv:(h + 1) * dv]
            dn_ref[rs, h * dv:(h + 1) * dv] = o[h] * _silu(z)

    @pl.when(c == pl.num_programs(1) - 1)
    def _():
        sfin_ref[0] = s_ref[...]


def deltanet_prompt(proj, col, conv_w, a_log, dt_bias, o_norm, batch, seq, heads, dk, dv, rows):
    conv_ch = 2 * heads * dk + heads * dv
    zw = heads * dv
    steps = seq // rows

    def row_map(width_off):
        return lambda b, c: (b * steps + c, width_off)

    body = functools.partial(_dn_prompt_body, rows=rows, heads=heads, dk=dk, dv=dv)
    return pl.pallas_call(
        body,
        out_shape=(jax.ShapeDtypeStruct((batch * seq, zw), F32),
                   jax.ShapeDtypeStruct((batch, heads, dk, dv), F32)),
        grid=(batch, steps),
        in_specs=[
            pl.BlockSpec((rows, conv_ch), row_map(col["qkv"] // conv_ch)),
            pl.BlockSpec((rows, zw), row_map(col["z"] // zw)),
            pl.BlockSpec((rows, LANE), row_map(col["b"] // LANE)),
            pl.BlockSpec((rows, LANE), row_map(col["a"] // LANE)),
            _resident((CONV_WIDTH, conv_ch)),
            _resident((1, LANE)),
            _resident((1, LANE)),
            _resident((1, dv)),
        ],
        out_specs=(pl.BlockSpec((rows, zw), lambda b, c: (b * steps + c, 0)),
                   pl.BlockSpec((1, heads, dk, dv), lambda b, c: (b, 0, 0, 0))),
        scratch_shapes=[
            pltpu.VMEM((heads, dk, dv), F32),
            pltpu.VMEM((rows + SUBLANE, conv_ch), F32),
            pltpu.VMEM((rows, conv_ch), F32),
        ],
        compiler_params=_params("parallel", "arbitrary"),
        name="deltanet_prompt",
    )(proj, proj, proj, proj, conv_w, a_log, dt_bias, o_norm)


def _dn_sample_body(qkv_ref, z_ref, bl_ref, al_ref, cbuf_ref, st_ref, cw_ref, alog_ref, dtb_ref, on_ref,
                    dn_ref, sto_ref, cbo_ref, act_ref, *, nb, heads, dk, dv):
    wq = heads * dk
    new = qkv_ref[...]
    y = cbuf_ref[:, 0, :] * cw_ref[0:1, :]
    for j in range(1, CONV_WIDTH - 1):
        y = y + cbuf_ref[:, j, :] * cw_ref[j:j + 1, :]
    y = y + new * cw_ref[CONV_WIDTH - 1:CONV_WIDTH, :]
    act_ref[...] = _silu(y)
    for j in range(CONV_WIDTH - 2):
        cbo_ref[:, j, :] = cbuf_ref[:, j + 1, :]
    cbo_ref[:, CONV_WIDTH - 2, :] = new

    beta = jax.nn.sigmoid(bl_ref[...])
    g = -jnp.exp(alog_ref[...]) * jax.nn.softplus(al_ref[...] + dtb_ref[...])
    dec = jnp.exp(g)

    q_t, k_t = [], []
    for h in range(heads):
        q = act_ref[:, h * dk:(h + 1) * dk]
        k = act_ref[:, wq + h * dk:wq + (h + 1) * dk]
        qn = q * lax.rsqrt(jnp.sum(q * q, axis=-1, keepdims=True) + EPS) * (dk ** -0.5)
        kn = k * lax.rsqrt(jnp.sum(k * k, axis=-1, keepdims=True) + EPS)
        q_t.append(qn.T)
        k_t.append(kn.T)

    def per_head(fn):
        return jnp.stack([fn(h) for h in range(heads)], axis=0)

    for i in range(nb):
        row = slice(i, i + 1)
        kcol = per_head(lambda h: k_t[h][:, i:i + 1])
        qcol = per_head(lambda h: q_t[h][:, i:i + 1])
        v = per_head(lambda h: act_ref[row, 2 * wq + h * dv:2 * wq + (h + 1) * dv])
        z = per_head(lambda h: z_ref[row, h * dv:(h + 1) * dv])
        b_i = per_head(lambda h: beta[row, h:h + 1])
        d_i = per_head(lambda h: dec[row, h:h + 1])
        s = st_ref[i] * d_i
        u = b_i * (v - jnp.sum(kcol * s, axis=1, keepdims=True))
        s = s + kcol * u
        o = jnp.sum(qcol * s, axis=1, keepdims=True)
        sto_ref[i] = s
        o = o * lax.rsqrt(jnp.mean(o * o, axis=-1, keepdims=True) + EPS) * on_ref[...]
        o = o * _silu(z)
        for h in range(heads):
            dn_ref[row, h * dv:(h + 1) * dv] = o[h]


def deltanet_sample(proj, col, conv_buf, state, conv_w, a_log, dt_bias, o_norm, heads, dk, dv, nb):
    n_seq = proj.shape[0]
    conv_ch = 2 * heads * dk + heads * dv
    zw = heads * dv
    body = functools.partial(_dn_sample_body, nb=nb, heads=heads, dk=dk, dv=dv)
    return pl.pallas_call(
        body,
        out_shape=(jax.ShapeDtypeStruct((n_seq, zw), F32),
                   jax.ShapeDtypeStruct(state.shape, F32),
                   jax.ShapeDtypeStruct(conv_buf.shape, F32)),
        grid=(n_seq // nb,),
        in_specs=[
            pl.BlockSpec((nb, conv_ch), lambda i: (i, col["qkv"] // conv_ch)),
            pl.BlockSpec((nb, zw), lambda i: (i, col["z"] // zw)),
            pl.BlockSpec((nb, LANE), lambda i: (i, col["b"] // LANE)),
            pl.BlockSpec((nb, LANE), lambda i: (i, col["a"] // LANE)),
            pl.BlockSpec((nb, CONV_WIDTH - 1, conv_ch), lambda i: (i, 0, 0)),
            pl.BlockSpec((nb, heads, dk, dv), lambda i: (i, 0, 0, 0)),
            _resident((CONV_WIDTH, conv_ch)),
            _resident((1, LANE)),
            _resident((1, LANE)),
            _resident((1, dv)),
        ],
        out_specs=(pl.BlockSpec((nb, zw), lambda i: (i, 0)),
                   pl.BlockSpec((nb, heads, dk, dv), lambda i: (i, 0, 0, 0)),
                   pl.BlockSpec((nb, CONV_WIDTH - 1, conv_ch), lambda i: (i, 0, 0))),
        scratch_shapes=[
            pltpu.VMEM((nb, conv_ch), F32),
        ],
        compiler_params=_params("parallel"),
        name="deltanet_sample",
    )(proj, proj, proj, proj, conv_buf, state, conv_w, a_log, dt_bias, o_norm)


def _xattn_prompt_body(q_ref, k_ref, v_ref, o_ref, *, heads, dh):
    for h in range(heads):
        sl = slice(h * dh, (h + 1) * dh)
        q = q_ref[:, sl].astype(BF16)
        k = k_ref[:, sl].astype(BF16)
        v = v_ref[:, sl].astype(BF16)
        s = lax.dot_general(q, k, (((1,), (1,)), ((), ())), preferred_element_type=F32) * (dh ** -0.5)
        e = jnp.exp(s - jnp.max(s, axis=-1, keepdims=True))
        p = e / jnp.sum(e, axis=-1, keepdims=True)
        o_ref[:, sl] = _dot(p.astype(BF16), v)


def xattn_prompt(proj, cq_col, kv, batch, seq, n_mem, heads, dh, tq):
    xw = heads * dh
    steps = seq // tq
    body = functools.partial(_xattn_prompt_body, heads=heads, dh=dh)
    return pl.pallas_call(
        body,
        out_shape=jax.ShapeDtypeStruct((batch * seq, xw), F32),
        grid=(batch, steps),
        in_specs=[
            pl.BlockSpec((tq, xw), lambda b, i: (b * steps + i, cq_col // xw)),
            pl.BlockSpec((n_mem, xw), lambda b, i: (b, 0)),
            pl.BlockSpec((n_mem, xw), lambda b, i: (b, 1)),
        ],
        out_specs=pl.BlockSpec((tq, xw), lambda b, i: (b * steps + i, 0)),
        compiler_params=_params("parallel", "parallel"),
        name="xattn_prompt",
    )(proj, kv, kv)


def _xattn_sample_body(q_ref, k_ref, v_ref, o_ref, *, nb, heads, dh):
    rep = SUBLANE // heads
    n_t = k_ref.shape[2] // SUBLANE

    def over_groups(x, op):
        out = x
        for t in range(1, rep):
            out = op(out, pltpu.roll(x, t * heads, 0))
        return out

    for i in range(nb):
        row = slice(i, i + 1)
        q8 = jnp.concatenate([q_ref[row, h * dh:(h + 1) * dh] for h in range(heads)] * rep, axis=0)
        k3 = k_ref[0, i].reshape(n_t, SUBLANE, dh)
        v3 = v_ref[0, i].reshape(n_t, SUBLANE, dh)
        s = jnp.sum(k3 * q8[None], axis=-1, keepdims=True) * (dh ** -0.5)
        m = over_groups(jnp.max(s, axis=0), jnp.maximum)
        e = jnp.exp(s - m[None])
        den = over_groups(jnp.sum(e, axis=0), jnp.add)
        p = e / den[None]
        acc = over_groups(jnp.sum(p * v3, axis=0), jnp.add)
        for h in range(heads):
            o_ref[row, h * dh:(h + 1) * dh] = acc[h:h + 1, :]


def xattn_sample(proj, cq_col, cache_k, cache_v, layer, nb):
    n_layers, n_seq, n_mem, heads, dh = cache_k.shape
    assert SUBLANE % heads == 0 and (n_mem * heads) % SUBLANE == 0
    xw = heads * dh
    body = functools.partial(_xattn_sample_body, nb=nb, heads=heads, dh=dh)
    cache_spec = pl.BlockSpec((1, nb, n_mem * heads, dh), lambda i: (layer, i, 0, 0))
    flat = (n_layers, n_seq, n_mem * heads, dh)
    return pl.pallas_call(
        body,
        out_shape=jax.ShapeDtypeStruct((n_seq, xw), F32),
        grid=(n_seq // nb,),
        in_specs=[pl.BlockSpec((nb, xw), lambda i: (i, cq_col // xw)), cache_spec, cache_spec],
        out_specs=pl.BlockSpec((nb, xw), lambda i: (i, 0)),
        compiler_params=_params("parallel"),
        name="xattn_sample",
    )(proj, cache_k.reshape(flat), cache_v.reshape(flat))


def _pool_mix(sums, cnts, u_of, pw_ref, ps_ref, o_ref, gw):
    for gi in range(len(POOL_WINDOWS)):
        cols = slice(gi * gw, (gi + 1) * gw)
        d = sums[gi] / cnts[gi] - u_of(cols)
        o_ref[:, cols] = _wdot(d, pw_ref[gi]) * ps_ref[:, cols]


def _pool_prompt_body(u_ref, pw_ref, ps_ref, o_ref, xbuf, s2, s4, s8, *, tt, gw):
    t = pl.program_id(1)
    n = POOL_HALO + tt

    @pl.when(t == 0)
    def _():
        xbuf[0:POOL_HALO, :] = jnp.zeros((POOL_HALO, xbuf.shape[1]), F32)

    xbuf[POOL_HALO:n, :] = u_ref[...]
    s2[8:n, :] = xbuf[8:n, :] + xbuf[7:n - 1, :]
    s4[16:n, :] = s2[16:n, gw:] + s2[14:n - 2, gw:]
    s8[24:n, :] = s4[24:n, gw:] + s4[20:n - 4, gw:]
    s16 = s8[32:n, gw:] + s8[24:n - 8, gw:]
    sums = [s2[POOL_HALO:n, 0:gw], s4[POOL_HALO:n, 0:gw], s8[POOL_HALO:n, 0:gw], s16]
    pos = t * tt + lax.broadcasted_iota(jnp.int32, (tt, 1), 0)
    cnts = [jnp.minimum(pos + 1, w).astype(F32) for w in POOL_WINDOWS]
    _pool_mix(sums, cnts, lambda cols: u_ref[:, cols], pw_ref, ps_ref, o_ref, gw)
    xbuf[0:POOL_HALO, :] = xbuf[tt:n, :]


def pool_prompt(proj, pool_w, pool_scale, batch, seq, tt):
    pw_total = pool_scale.shape[-1]
    gw = pw_total // len(POOL_WINDOWS)
    steps = seq // tt
    n = POOL_HALO + tt
    body = functools.partial(_pool_prompt_body, tt=tt, gw=gw)
    return pl.pallas_call(
        body,
        out_shape=jax.ShapeDtypeStruct((batch * seq, pw_total), F32),
        grid=(batch, steps),
        in_specs=[
            pl.BlockSpec((tt, pw_total), lambda b, t: (b * steps + t, 0)),
            _resident(pool_w.shape),
            _resident((1, pw_total)),
        ],
        out_specs=pl.BlockSpec((tt, pw_total), lambda b, t: (b * steps + t, 0)),
        scratch_shapes=[
            pltpu.VMEM((n, pw_total), F32),
            pltpu.VMEM((n, pw_total), F32),
            pltpu.VMEM((n, pw_total - gw), F32),
            pltpu.VMEM((n, pw_total - 2 * gw), F32),
        ],
        compiler_params=_params("parallel", "arbitrary"),
        name="pool_prompt",
    )(proj, pool_w, pool_scale)


def _pool_sample_body(u_ref, st_ref, pw_ref, ps_ref, o_ref, sto_ref, *, gw, buf):
    pw_total = gw * len(POOL_WINDOWS)
    new = u_ref[:, 0:pw_total]
    sums, cnts = [], []
    for gi, w in enumerate(POOL_WINDOWS):
        cols = slice(gi * gw, (gi + 1) * gw)
        s = new[:, cols]
        for i in range(1, w):
            s = s + st_ref[:, buf - i, cols]
        sums.append(s)
        cnts.append(float(min(PAST_LEN + 1, w)))
    _pool_mix(sums, cnts, lambda cols: new[:, cols], pw_ref, ps_ref, o_ref, gw)
    sto_ref[:, 0:buf - 1, :] = st_ref[:, 1:buf, :]
    sto_ref[:, buf - 1, :] = new


def pool_sample(proj, state, pool_w, pool_scale):
    n_seq, buf, pw_total = state.shape
    gw = pw_total // len(POOL_WINDOWS)
    body = functools.partial(_pool_sample_body, gw=gw, buf=buf)
    return pl.pallas_call(
        body,
        out_shape=(jax.ShapeDtypeStruct((n_seq, pw_total), F32),
                   jax.ShapeDtypeStruct(state.shape, F32)),
        compiler_params=pltpu.CompilerParams(vmem_limit_bytes=VMEM_LIMIT_BYTES),
        name="pool_sample",
    )(proj, state, pool_w, pool_scale)


def _route_body(h_ref, g_ref, r_ref, idx_ref, gate_ref, *, n_experts):
    xn = _rms(h_ref[...], g_ref[...])
    logits = jnp.dot(xn, r_ref[...], precision=HIGHEST, preferred_element_type=F32)
    lane = lax.broadcasted_iota(jnp.int32, logits.shape, 1)
    lm = jnp.where(lane < n_experts, logits, -jnp.inf)
    m1 = jnp.max(lm, axis=1, keepdims=True)
    i1 = jnp.min(jnp.where(lm == m1, lane, LANE), axis=1, keepdims=True)
    lm2 = jnp.where(lane == i1, -jnp.inf, lm)
    m2 = jnp.max(lm2, axis=1, keepdims=True)
    i2 = jnp.min(jnp.where(lm2 == m2, lane, LANE), axis=1, keepdims=True)
    e2 = jnp.exp(m2 - m1)
    den = 1.0 + e2
    idx_ref[...] = jnp.where(lane == 0, i1, jnp.where(lane == 1, i2, 0))
    gate_ref[...] = jnp.where(lane == 0, 1.0 / den, jnp.where(lane == 1, e2 / den, 0.0))


def moe_route(h_all, row_offset, rows, g, r_pad, tm, n_experts, name):
    d = h_all.shape[1]
    off = row_offset // tm
    body = functools.partial(_route_body, n_experts=n_experts)
    return pl.pallas_call(
        body,
        out_shape=(jax.ShapeDtypeStruct((rows, LANE), jnp.int32), jax.ShapeDtypeStruct((rows, LANE), F32)),
        grid=(rows // tm,),
        in_specs=[pl.BlockSpec((tm, d), lambda i: (i + off, 0)), _resident((1, d)), _resident((d, LANE))],
        out_specs=(pl.BlockSpec((tm, LANE), lambda i: (i, 0)), pl.BlockSpec((tm, LANE), lambda i: (i, 0))),
        compiler_params=_params("parallel"),
        name=name,
    )(h_all, g.reshape(1, d), r_pad)


def _start_row_copies(idx_ref, g0, n_groups, make_copy):
    for t in range(n_groups):
        for u in range(SUBLANE):
            make_copy(idx_ref[0, 0, (g0 + t) * SUBLANE + u], g0 + t, u).start()


def _experts_body(te_ref, nu_ref, g_first_ref, g_next_ref, s_prev_ref, s_cur_ref,
                  h_hbm, g_ref, wg_ref, wu_ref, wo_ref, y_hbm,
                  xbuf, acc, xn_ref, gsem, ssem, *, tm, gps):
    i = pl.program_id(0)
    j = pl.program_id(1)
    n_tiles = pl.num_programs(0)
    nj = pl.num_programs(1)
    n_used = nu_ref[0]
    slot = i % 2
    other = 1 - slot
    valid = i < n_used
    groups = tm // SUBLANE
    groups_pad = xbuf.shape[1]
    d = xn_ref.shape[1]

    def gather(s):
        return lambda row, grp, u: pltpu.make_async_copy(
            h_hbm.at[pl.ds(row, 1)], xbuf.at[s, grp, pl.ds(u, 1)], gsem.at[s])

    def scatter(s):
        return lambda row, grp, u: pltpu.make_async_copy(
            acc.at[s, grp, pl.ds(u, 1)], y_hbm.at[pl.ds(row, 1)], ssem.at[s])

    def start_all(idx_ref, make_copy):
        def body(t, carry):
            _start_row_copies(idx_ref, t, 1, make_copy)
            return carry
        lax.fori_loop(0, groups_pad, body, 0)

    def wait_gather(s):
        def body(t, carry):
            pltpu.make_async_copy(h_hbm.at[pl.ds(0, SUBLANE)], xbuf.at[s, t], gsem.at[s]).wait()
            return carry
        lax.fori_loop(0, groups_pad, body, 0)

    def wait_scatter(s):
        def body(t, carry):
            pltpu.make_async_copy(acc.at[s, t], y_hbm.at[pl.ds(0, SUBLANE)], ssem.at[s]).wait()
            return carry
        lax.fori_loop(0, groups_pad, body, 0)

    @pl.when(j == 0)
    def _():
        @pl.when(i == 0)
        def _():
            acc[...] = jnp.zeros_like(acc)
            start_all(g_first_ref, gather(0))

        @pl.when(i <= n_used)
        def _():
            wait_gather(slot)

        @pl.when((i >= 1) & (i <= n_used))
        def _():
            wait_scatter(slot)

        @pl.when(valid)
        def _():
            xn_ref[...] = _rms(xbuf[slot, 0:groups].reshape(tm, d), g_ref[...]).astype(BF16)

        @pl.when(i == n_used)
        def _():
            start_all(s_prev_ref, scatter(other))
            wait_scatter(other)

    @pl.when(valid)
    def _():
        g0 = j * gps
        _start_row_copies(g_next_ref, g0, gps, gather(other))
        _start_row_copies(s_prev_ref, g0, gps, scatter(other))

        xn = xn_ref[...]
        a = _dot(xn, wg_ref[0].astype(BF16))
        b = _dot(xn, wu_ref[0].astype(BF16))
        part = _dot((_silu(a) * b).astype(BF16), wo_ref[0].astype(BF16)).reshape(groups, SUBLANE, d)

        @pl.when(j == 0)
        def _():
            acc[slot, 0:groups] = part

        @pl.when(j > 0)
        def _():
            acc[slot, 0:groups] += part

    @pl.when(valid & (i == n_tiles - 1) & (j == nj - 1))
    def _():
        wait_gather(other)
        wait_scatter(other)
        start_all(s_cur_ref, scatter(slot))
        wait_scatter(slot)


def moe_experts(h_all, g, wi, wo, gather_rows, scatter_rows, tile_expert, n_used, y_rows, tm, tf):
    d = h_all.shape[1]
    f = wo.shape[1]
    nj = f // tf
    n_tiles = gather_rows.shape[0]
    rows_pad = gather_rows.shape[2]
    gps = rows_pad // SUBLANE // nj
    assert gps * nj * SUBLANE == rows_pad and rows_pad >= tm

    def wblock(col_off):
        def index_map(i, j, te, nu):
            jj = jnp.where(i < nu[0], j, nj - 1)
            return (te[i], 0, col_off + jj)
        return index_map

    def woblock(i, j, te, nu):
        return (te[i], jnp.where(i < nu[0], j, nj - 1), 0)

    def smem_tile(index_map):
        return pl.BlockSpec((1, 1, rows_pad), index_map, memory_space=pltpu.SMEM)

    grid_spec = pltpu.PrefetchScalarGridSpec(
        num_scalar_prefetch=2,
        grid=(n_tiles, nj),
        in_specs=[
            smem_tile(lambda i, j, te, nu: (0, 0, 0)),
            smem_tile(lambda i, j, te, nu: (jnp.minimum(i + 1, n_tiles - 1), 0, 0)),
            smem_tile(lambda i, j, te, nu: (i, 0, 0)),
            smem_tile(lambda i, j, te, nu: (i + 1, 0, 0)),
            pl.BlockSpec(memory_space=pl.ANY),
            pl.BlockSpec((1, d), lambda i, j, te, nu: (0, 0)),
            pl.BlockSpec((1, d, tf), wblock(0)),
            pl.BlockSpec((1, d, tf), wblock(nj)),
            pl.BlockSpec((1, tf, d), woblock),
        ],
        out_specs=pl.BlockSpec(memory_space=pl.ANY),
        scratch_shapes=[
            pltpu.VMEM((2, rows_pad // SUBLANE, SUBLANE, d), F32),
            pltpu.VMEM((2, rows_pad // SUBLANE, SUBLANE, d), F32),
            pltpu.VMEM((tm, d), BF16),
            pltpu.SemaphoreType.DMA((2,)),
            pltpu.SemaphoreType.DMA((2,)),
        ],
    )
    return pl.pallas_call(
        functools.partial(_experts_body, tm=tm, gps=gps),
        out_shape=jax.ShapeDtypeStruct((y_rows, d), F32),
        grid_spec=grid_spec,
        compiler_params=_params("arbitrary", "arbitrary"),
        name="moe_experts",
    )(tile_expert, n_used, gather_rows, gather_rows, scatter_rows, scatter_rows,
      h_all, g.reshape(1, d), wi, wi, wo)


def _combine_body(h_ref, gate_ref, y1_ref, y2_ref, fn_ref, o_ref, *, final_norm):
    gate = gate_ref[...]
    out = h_ref[...] + (gate[:, 0:1] * y1_ref[...] + gate[:, 1:2] * y2_ref[...])
    if final_norm:
        out = _rms(out, fn_ref[...])
    o_ref[...] = out


def moe_combine(h_all, row_offset, rows, gates, y, choice_stride, fn, tb, final_norm, name):
    d = h_all.shape[1]
    off = row_offset // tb
    off2 = (choice_stride + row_offset) // tb
    assert row_offset % tb == 0 and choice_stride % tb == 0
    body = functools.partial(_combine_body, final_norm=final_norm)
    return pl.pallas_call(
        body,
        out_shape=jax.ShapeDtypeStruct((rows, d), F32),
        grid=(rows // tb,),
        in_specs=[
            pl.BlockSpec((tb, d), lambda i: (i + off, 0)),
            pl.BlockSpec((tb, LANE), lambda i: (i, 0)),
            pl.BlockSpec((tb, d), lambda i: (i + off, 0)),
            pl.BlockSpec((tb, d), lambda i: (i + off2, 0)),
            _resident((1, d)),
        ],
        out_specs=pl.BlockSpec((tb, d), lambda i: (i, 0)),
        compiler_params=_params("parallel"),
        name=name,
    )(h_all, gates, y, y, fn.reshape(1, d))


def moe_plan(idx, n_experts, tm, rows_pad, choice_stride):
    n_tok, k = idx.shape
    n_slots = n_tok * k
    n_tiles = n_slots // tm + n_experts
    e_flat = idx.reshape(n_slots)
    onehot = (e_flat[:, None] == jnp.arange(n_experts, dtype=jnp.int32)[None, :]).astype(jnp.int32)
    csum = jnp.cumsum(onehot, axis=0)
    counts = csum[-1]
    rank = jnp.sum(csum * onehot, axis=1) - 1
    tiles_e = (counts + tm - 1) // tm
    tile_end = jnp.cumsum(tiles_e)
    pstart = (tile_end - tiles_e) * tm
    dest = jnp.sum(onehot * pstart[None, :], axis=1) + rank
    n_used = tile_end[-1]
    tile_ids = jnp.arange(n_tiles, dtype=jnp.int32)
    tile_expert = jnp.sum((tile_ids[:, None] >= tile_end[None, :]).astype(jnp.int32), axis=1)
    last_expert = jnp.sum((n_used - 1 >= tile_end).astype(jnp.int32))
    tile_expert = jnp.where(tile_ids < n_used, tile_expert, last_expert).astype(jnp.int32)
    slot_of = jnp.zeros((n_tiles * tm,), jnp.int32).at[dest].set(
        jnp.arange(1, n_slots + 1, dtype=jnp.int32), unique_indices=True).reshape(n_tiles, tm)
    filled = slot_of > 0
    tok = (slot_of - 1) // k
    choice = (slot_of - 1) % k
    gap = jnp.concatenate([jnp.arange(c * choice_stride + n_tok, (c + 1) * choice_stride, dtype=jnp.int32)
                           for c in range(k)])
    spare = jnp.concatenate([gap, k * choice_stride + jnp.arange(rows_pad, dtype=jnp.int32)])[:rows_pad]
    gather_rows = jnp.where(filled, tok, 0)
    scatter_rows = jnp.where(filled, choice * choice_stride + tok, spare[None, :tm])
    pad = rows_pad - tm
    gather_rows = jnp.concatenate([gather_rows, jnp.zeros((n_tiles, pad), jnp.int32)], axis=1)
    scatter_rows = jnp.concatenate(
        [scatter_rows, jnp.broadcast_to(spare[None, tm:], (n_tiles, pad))], axis=1)
    scatter_rows = jnp.concatenate([spare[None, :], scatter_rows], axis=0)
    return (gather_rows.reshape(n_tiles, 1, rows_pad).astype(jnp.int32),
            scatter_rows.reshape(n_tiles + 1, 1, rows_pad).astype(jnp.int32),
            tile_expert, n_used.reshape(1).astype(jnp.int32),
            k * choice_stride + max(0, rows_pad - k * (choice_stride - n_tok)))


def _pad_lanes(v):
    return jnp.zeros((1, LANE), F32).at[0, :v.shape[0]].set(v)


def kernel(x_prompt, x_sample, state_dn, state_dn_conv, state_pool, cache_mem_k, cache_mem_v, mem_prompt, a_norm1, a_w_in, a_conv_w, a_a_log, a_dt_bias, a_o_norm, a_w_out, a_norm2, a_ffn_wi, a_ffn_wo, b_norm1, b_w_in, b_pool_w, b_pool_scale, b_w_out, b_norm2, b_router, b_moe_wi, b_moe_wo, m_norm, m_w_k, m_w_v, final_norm):
    bp, seq, d = x_prompt.shape
    bs = x_sample.shape[0]
    assert x_sample.shape[1] == 1
    depth = m_norm.shape[0]
    assert depth % 2 == 0, "the output norm is fused into the expert block of the last (odd) layer"
    heads, dk, dv = state_dn.shape[2], state_dn.shape[3], state_dn.shape[4]
    conv_ch = state_dn_conv.shape[-1]
    qk_w, v_w = heads * dk, heads * dv
    assert conv_ch == 2 * qk_w + v_w and dk == LANE and dv == LANE and heads <= LANE
    n_mem, x_heads, x_dh = cache_mem_k.shape[2], cache_mem_k.shape[3], cache_mem_k.shape[4]
    xw = x_heads * x_dh
    pool_w_total = state_pool.shape[-1]
    pool_buf = state_pool.shape[2]
    d_ff = a_ffn_wo.shape[1]

    hp = x_prompt.reshape(bp * seq, d)
    hs = x_sample.reshape(bs, d)

    o1, o2, o3, o4 = conv_ch, conv_ch + v_w, conv_ch + v_w + heads, conv_ch + v_w + 2 * heads
    col_a = {"qkv": 0, "z": conv_ch, "cq": conv_ch + v_w, "b": conv_ch + v_w + xw, "a": conv_ch + v_w + xw + LANE}
    n_a = -(-(col_a["a"] + LANE) // 512) * 512

    dn_p, conv_p, pool_p, mk_all, mv_all = [], [], [], [], []
    dn_s, conv_s, pool_s = [], [], []
    for i in range(depth):
        j = i // 2
        w_kv = jnp.concatenate([m_w_k[i], m_w_v[i]], axis=1).astype(BF16)
        kv = rms_matmul(mem_prompt.reshape(bp * n_mem, d), m_norm[i], w_kv, 256, f"mem_kv_{i}")
        mk_all.append(kv[:, :xw].reshape(bp, n_mem, x_heads, x_dh))
        mv_all.append(kv[:, xw:].reshape(bp, n_mem, x_heads, x_dh))
        if i % 2 == 0:
            w = a_w_in[j]
            w_in = jnp.zeros((d, n_a), F32)
            w_in = w_in.at[:, :o2].set(w[:, :o2])
            w_in = w_in.at[:, col_a["cq"]:col_a["cq"] + xw].set(w[:, o4:])
            w_in = w_in.at[:, col_a["b"]:col_a["b"] + heads].set(w[:, o2:o3])
            w_in = w_in.at[:, col_a["a"]:col_a["a"] + heads].set(w[:, o3:o4])
            a_log, dt_b = _pad_lanes(a_a_log[j]), _pad_lanes(a_dt_bias[j])
            o_norm = a_o_norm[j].reshape(1, dv)

            proj_p = rms_matmul(hp, a_norm1[j], w_in.astype(BF16), 256, f"a_in_p_{i}")
            proj_s = rms_matmul(hs, a_norm1[j], w_in, bs, f"a_in_s_{i}", tn=512)
            mix_p, s_p = deltanet_prompt(proj_p, col_a, a_conv_w[j], a_log, dt_b, o_norm,
                                         bp, seq, heads, dk, dv, rows=DN_STEP_CHUNKS * CHUNK)
            mix_s, s_s, c_s = deltanet_sample(proj_s, col_a, state_dn_conv[j], state_dn[j], a_conv_w[j],
                                              a_log, dt_b, o_norm, heads, dk, dv, nb=SUBLANE)
            xa_p = xattn_prompt(proj_p, col_a["cq"], kv, bp, seq, n_mem, x_heads, x_dh, tq=512)
            xa_s = xattn_sample(proj_s, col_a["cq"], cache_mem_k, cache_mem_v, i, nb=SUBLANE)
            hp = out_proj(hp, mix_p, xa_p, a_w_out[j].astype(BF16), 512, f"a_out_p_{i}")
            hs = out_proj(hs, mix_s, xa_s, a_w_out[j], bs, f"a_out_s_{i}")
            hp = ffn_dense(hp, a_norm2[j], a_ffn_wi[j].astype(BF16), a_ffn_wo[j].astype(BF16), 256,
                           f"a_ffn_p_{i}")
            hs = ffn_dense(hs, a_norm2[j], a_ffn_wi[j], a_ffn_wo[j], bs, f"a_ffn_s_{i}", tf=256)
            dn_p.append(s_p)
            conv_p.append(proj_p.reshape(bp, seq, n_a)[:, seq - (CONV_WIDTH - 1):, :conv_ch])
            dn_s.append(s_s)
            conv_s.append(c_s)
        else:
            ps = b_pool_scale[j].reshape(1, pool_w_total)
            last = i == depth - 1

            proj_p = rms_matmul(hp, b_norm1[j], b_w_in[j].astype(BF16), 512, f"b_in_p_{i}")
            proj_s = rms_matmul(hs, b_norm1[j], b_w_in[j], bs, f"b_in_s_{i}")
            mix_p = pool_prompt(proj_p, b_pool_w[j].astype(BF16), ps, bp, seq, tt=512)
            mix_s, buf_s = pool_sample(proj_s, state_pool[j], b_pool_w[j], ps)
            xa_p = xattn_prompt(proj_p, pool_w_total, kv, bp, seq, n_mem, x_heads, x_dh, tq=512)
            xa_s = xattn_sample(proj_s, pool_w_total, cache_mem_k, cache_mem_v, i, nb=SUBLANE)
            n_p = bp * seq
            h_all = out_proj(hp, mix_p, xa_p, b_w_out[j].astype(BF16), 512, f"b_out_p_{i}", out_rows=n_p + bs)
            h_all = out_proj(hs, mix_s, xa_s, b_w_out[j], bs, f"b_out_s_{i}", out_rows=n_p + bs,
                             into=h_all, row_offset=n_p)
            n_experts = b_router.shape[-1]
            r_pad = jnp.zeros((d, LANE), F32).at[:, :n_experts].set(b_router[j])
            idx_p, gate_p = moe_route(h_all, 0, n_p, b_norm2[j], r_pad, 512, n_experts, f"b_route_p_{i}")
            idx_s, gate_s = moe_route(h_all, n_p, bs, b_norm2[j], r_pad, bs, n_experts, f"b_route_s_{i}")
            idx = jnp.concatenate([idx_p[:, :TOP_K], idx_s[:, :TOP_K]], axis=0)
            nj = b_moe_wo.shape[2] // MOE_FF_TILE
            rows_pad = -(-(MOE_TILE // SUBLANE) // nj) * nj * SUBLANE
            choice_stride = -(-(n_p + bs) // 512) * 512
            gather_rows, scatter_rows, tile_expert, n_used, y_rows = moe_plan(
                idx, n_experts, MOE_TILE, rows_pad, choice_stride)
            y = moe_experts(h_all, b_norm2[j], b_moe_wi[j], b_moe_wo[j], gather_rows, scatter_rows,
                            tile_expert, n_used, y_rows, MOE_TILE, MOE_FF_TILE)
            hp = moe_combine(h_all, 0, n_p, gate_p, y, choice_stride, final_norm, 512, last,
                             f"b_combine_p_{i}")
            hs = moe_combine(h_all, n_p, bs, gate_s, y, choice_stride, final_norm, bs, last,
                             f"b_combine_s_{i}")
            pool_p.append(proj_p.reshape(bp, seq, -1)[:, seq - pool_buf:, :pool_w_total])
            pool_s.append(buf_s)

    y_prompt = hp.reshape(bp, seq, d)
    y_sample = hs.reshape(bs, 1, d)
    return (y_prompt, y_sample, jnp.stack(dn_p), jnp.stack(conv_p), jnp.stack(pool_p),
            jnp.stack(mk_all), jnp.stack(mv_all), jnp.stack(dn_s), jnp.stack(conv_s), jnp.stack(pool_s))
```

```python
import functools

import jax
import jax.numpy as jnp
from jax import lax
from jax.experimental import pallas as pl
from jax.experimental.pallas import tpu as pltpu

F32 = jnp.float32
BF16 = jnp.bfloat16
HIGHEST = lax.Precision.HIGHEST

EPS = 1e-6
CHUNK = 64
CONV_WIDTH = 4
POOL_WINDOWS = (2, 4, 8, 16)
PAST_LEN = 16384
TOP_K = 2

LANE = 128
SUBLANE = 8
VMEM_LIMIT_BYTES = 56 * 2**20

MOE_TILE = 1024
MOE_FF_TILE = 512
DN_STEP_CHUNKS = 4
INV_BLOCK = 16
POOL_HALO = 32


def _params(*sem):
    return pltpu.CompilerParams(dimension_semantics=sem, vmem_limit_bytes=VMEM_LIMIT_BYTES)


def _resident(shape):
    nd = len(shape)
    return pl.BlockSpec(shape, lambda *_: (0,) * nd, pipeline_mode=pl.Buffered(1))


def _rms(x, w):
    return x * lax.rsqrt(jnp.mean(x * x, axis=-1, keepdims=True) + EPS) * w


def _silu(x):
    return x * jax.nn.sigmoid(x)


def _dot(a, b):
    return jnp.dot(a, b, preferred_element_type=F32)


def _wdot(a, w):
    if w.dtype == F32:
        return jnp.dot(a, w, precision=HIGHEST, preferred_element_type=F32)
    return jnp.dot(a.astype(BF16), w, preferred_element_type=F32)


def _bdot(a, b):
    return lax.dot_general(a, b, (((2,), (1,)), ((0,), (0,))), preferred_element_type=F32)


def _bdot_nt(a, b):
    return lax.dot_general(a, b, (((2,), (2,)), ((0,), (0,))), preferred_element_type=F32)


def _bdot_tn(a, b):
    return lax.dot_general(a, b, (((1,), (1,)), ((0,), (0,))), preferred_element_type=F32)


def _rms_matmul_body(x_ref, g_ref, w_ref, o_ref):
    xn = _rms(x_ref[...], g_ref[...])
    o_ref[...] = _wdot(xn, w_ref[...])


def _wspec(shape, index_map, steps):
    if steps == 1:
        return pl.BlockSpec(shape, index_map, pipeline_mode=pl.Buffered(1))
    return pl.BlockSpec(shape, index_map)


def rms_matmul(x, g, w, tm, name, tn=None):
    m, d = x.shape
    n = w.shape[1]
    tn = n if tn is None else tn
    nj = n // tn
    return pl.pallas_call(
        _rms_matmul_body,
        out_shape=jax.ShapeDtypeStruct((m, n), F32),
        grid=(m // tm, nj),
        in_specs=[pl.BlockSpec((tm, d), lambda i, j: (i, 0)), _resident((1, d)),
                  _wspec((d, tn), lambda i, j: (0, j), nj)],
        out_specs=pl.BlockSpec((tm, tn), lambda i, j: (i, j)),
        compiler_params=_params("parallel", "arbitrary"),
        name=name,
    )(x, g.reshape(1, d), w)


def _out_proj_body(x_ref, a1_ref, a2_ref, w1_ref, w2_ref, *rest, n_main):
    o_ref = rest[-1]

    @pl.when(pl.program_id(0) < n_main)
    def _():
        acc = _wdot(a1_ref[...], w1_ref[...]) + _wdot(a2_ref[...], w2_ref[...])
        o_ref[...] = x_ref[...] + acc

    @pl.when(pl.program_id(0) >= n_main)
    def _():
        o_ref[...] = jnp.zeros_like(o_ref)


def out_proj(x, a1, a2, w, tm, name, out_rows=None, into=None, row_offset=0):
    m, d = x.shape
    k1, k2 = a1.shape[1], a2.shape[1]
    assert w.shape[0] == k1 + k2 and k1 % k2 == 0 and row_offset % tm == 0
    out_rows = m if out_rows is None else out_rows
    off = row_offset // tm
    n_main = m // tm
    n_extra = 0 if (into is not None or out_rows == m) else 1
    assert out_rows - m <= n_extra * tm or into is not None
    last = n_main - 1
    in_specs = [
        pl.BlockSpec((tm, d), lambda i: (jnp.minimum(i, last), 0)),
        pl.BlockSpec((tm, k1), lambda i: (jnp.minimum(i, last), 0)),
        pl.BlockSpec((tm, k2), lambda i: (jnp.minimum(i, last), 0)),
        pl.BlockSpec((k1, d), lambda i: (0, 0), pipeline_mode=pl.Buffered(1)),
        pl.BlockSpec((k2, d), lambda i: (k1 // k2, 0), pipeline_mode=pl.Buffered(1)),
    ]
    args = [x, a1, a2, w, w]
    aliases = {}
    if into is not None:
        in_specs.append(pl.BlockSpec(memory_space=pl.ANY))
        args.append(into)
        aliases = {len(args) - 1: 0}
    return pl.pallas_call(
        functools.partial(_out_proj_body, n_main=n_main),
        out_shape=jax.ShapeDtypeStruct((out_rows, d), F32),
        grid=(n_main + n_extra,),
        in_specs=in_specs,
        out_specs=pl.BlockSpec((tm, d), lambda i: (i + off, 0)),
        input_output_aliases=aliases,
        compiler_params=_params("parallel"),
        name=name,
    )(*args)


def _ffn_body(h_ref, g_ref, wg_ref, wu_ref, wo_ref, o_ref, xn_ref):
    @pl.when(pl.program_id(1) == 0)
    def _():
        h = h_ref[...]
        xn_ref[...] = _rms(h, g_ref[...])
        o_ref[...] = h

    xn = xn_ref[...]
    a = _wdot(xn, wg_ref[...])
    b = _wdot(xn, wu_ref[...])
    o_ref[...] += _wdot(_silu(a) * b, wo_ref[...])


def ffn_dense(h, g, wi, wo, tm, name, tf=None):
    m, d = h.shape
    f = wo.shape[0]
    tf = f if tf is None else tf
    nj = f // tf
    return pl.pallas_call(
        _ffn_body,
        out_shape=jax.ShapeDtypeStruct((m, d), F32),
        grid=(m // tm, nj),
        in_specs=[
            pl.BlockSpec((tm, d), lambda i, j: (i, 0)),
            _resident((1, d)),
            _wspec((d, tf), lambda i, j: (0, j), nj),
            _wspec((d, tf), lambda i, j: (0, nj + j), nj),
            _wspec((tf, d), lambda i, j: (j, 0), nj),
        ],
        out_specs=pl.BlockSpec((tm, d), lambda i, j: (i, 0)),
        scratch_shapes=[pltpu.VMEM((tm, d), F32)],
        compiler_params=_params("parallel", "arbitrary"),
        name=name,
    )(h, g.reshape(1, d), wi, wi, wo)


def _pair_blockdiag(y, left):
    yb = y.astype(BF16)
    zero = jnp.zeros_like(yb)
    return jnp.concatenate([jnp.where(left, yb, zero), jnp.where(left, zero, yb)], axis=1)


def _pair_mm(x, y, left):
    return _bdot(x.astype(BF16), _pair_blockdiag(y, left))


def _unit_lower_inverse(mx, eye, blk, left):
    mm = functools.partial(_pair_mm, left=left)
    md = jnp.where(blk, mx, 0.0)
    c = mx - md
    p = eye - md
    m2 = mm(md, md)
    p = p + mm(p, m2)
    m4 = mm(m2, m2)
    p = p + mm(p, m4)
    m8 = mm(m4, m4)
    td = p + mm(p, m8)
    n = mm(td, c)
    n2 = mm(n, n)
    q = eye - n
    q = q + mm(q, n2)
    return mm(q, td)


def _dn_prompt_body(qkv_ref, z_ref, bl_ref, al_ref, cw_ref, alog_ref, dtb_ref, on_ref,
                    dn_ref, sfin_ref, s_ref, xbuf_ref, act_ref, *, rows, heads, dk, dv):
    c = pl.program_id(1)
    wq = heads * dk

    @pl.when(c == 0)
    def _():
        s_ref[...] = jnp.zeros_like(s_ref)
        xbuf_ref[0:SUBLANE, :] = jnp.zeros((SUBLANE, xbuf_ref.shape[1]), F32)

    xbuf_ref[SUBLANE:SUBLANE + rows, :] = qkv_ref[...]
    first = SUBLANE - (CONV_WIDTH - 1)
    y = xbuf_ref[pl.ds(first, rows), :] * cw_ref[0:1, :]
    for j in range(1, CONV_WIDTH):
        y = y + xbuf_ref[pl.ds(first + j, rows), :] * cw_ref[j:j + 1, :]
    act_ref[...] = _silu(y)
    xbuf_ref[0:SUBLANE, :] = xbuf_ref[rows:rows + SUBLANE, :]

    pairs = heads // 2
    ri = lax.broadcasted_iota(jnp.int32, (CHUNK, 2 * CHUNK), 0)
    lane = lax.broadcasted_iota(jnp.int32, (CHUNK, 2 * CHUNK), 1)
    cj = lane % CHUNK
    left = (lane < CHUNK)[None]
    incl = (ri >= cj)[None]
    strict = (ri > cj)[None]
    blk = ((ri // INV_BLOCK) == (cj // INV_BLOCK))[None]
    eye = (ri == cj).astype(F32)[None]
    tr = lax.broadcasted_iota(jnp.int32, (CHUNK, CHUNK), 0)
    tc = lax.broadcasted_iota(jnp.int32, (CHUNK, CHUNK), 1)
    tri = (tr >= tc).astype(F32)

    n_chunks = rows // CHUNK
    zero = jnp.zeros((CHUNK, dk), BF16)

    def chunk_rows(cc):
        return slice(cc * CHUNK, (cc + 1) * CHUNK)

    def per_head(fn):
        return jnp.stack([fn(chunk_rows(cc), h) for cc in range(n_chunks) for h in range(heads)], axis=0)

    def per_pair(fn):
        return jnp.stack([fn(cc, p) for cc in range(n_chunks) for p in range(pairs)], axis=0)

    q3 = per_head(lambda rs, h: act_ref[rs, h * dk:(h + 1) * dk])
    k3 = per_head(lambda rs, h: act_ref[rs, wq + h * dk:wq + (h + 1) * dk])
    v3 = per_head(lambda rs, h: act_ref[rs, 2 * wq + h * dv:2 * wq + (h + 1) * dv])
    qn = q3 * lax.rsqrt(jnp.sum(q3 * q3, axis=-1, keepdims=True) + EPS) * (dk ** -0.5)
    kn = k3 * lax.rsqrt(jnp.sum(k3 * k3, axis=-1, keepdims=True) + EPS)

    beta = jax.nn.sigmoid(bl_ref[...])
    g = -jnp.exp(alog_ref[...]) * jax.nn.softplus(al_ref[...] + dtb_ref[...])
    gcum = [jnp.dot(tri, g[chunk_rows(cc)], precision=HIGHEST, preferred_element_type=F32)
            for cc in range(n_chunks)]
    gcum_t = [jnp.concatenate([gc_, pltpu.roll(gc_, LANE - 1, 1)], axis=0).T for gc_ in gcum]

    def pair_cols(x_of):
        return per_pair(lambda cc, p: jnp.where(left[0], x_of(cc)[:, 2 * p:2 * p + 1],
                                                x_of(cc)[:, 2 * p + 1:2 * p + 2]))

    gc2 = pair_cols(lambda cc: gcum[cc])
    bc2 = pair_cols(lambda cc: beta[chunk_rows(cc)])
    gr2 = per_pair(lambda cc, p: gcum_t[cc][2 * p:2 * p + 1, :])
    decay = jnp.where(incl, jnp.exp(gc2 - gr2), 0.0)

    kb = kn.astype(BF16)
    qb = qn.astype(BF16)

    def pair_of(x, cc, p):
        return x[cc * heads + 2 * p], x[cc * heads + 2 * p + 1]

    k_lhs = per_pair(lambda cc, p: jnp.concatenate(pair_of(kb, cc, p), axis=1))
    q_lhs = per_pair(lambda cc, p: jnp.concatenate(pair_of(qb, cc, p), axis=1))
    k_bd = per_pair(lambda cc, p: jnp.concatenate(
        [jnp.concatenate([pair_of(kb, cc, p)[0], zero], axis=1),
         jnp.concatenate([zero, pair_of(kb, cc, p)[1]], axis=1)], axis=0))
    kk = _bdot_nt(k_lhs, k_bd)
    mx = jnp.where(strict, bc2 * kk * decay, 0.0)
    tinv = _unit_lower_inverse(mx, eye, blk, left)

    gcum_rows = jnp.concatenate(gcum, axis=0)
    gc = per_head(lambda rs, h: gcum_rows[rs, h:h + 1])
    bc = per_head(lambda rs, h: beta[rs, h:h + 1])
    eg = jnp.exp(gc)
    rhs = jnp.concatenate([v3 * bc, kn * (bc * eg)], axis=-1)
    sol = _bdot(_pair_blockdiag(tinv, left), rhs.astype(BF16).reshape(n_chunks * pairs, 2 * CHUNK, dv + dk))
    sol = sol.reshape(n_chunks * heads, CHUNK, dv + dk)
    w_val = sol[..., :dv]
    k_cd = sol[..., dv:].astype(BF16)
    qk_bd = _pair_blockdiag(_bdot_nt(q_lhs, k_bd) * decay, left)
    q_dec = (qn * eg).astype(BF16)
    g_last = gc[:, CHUNK - 1:CHUNK, :]
    k_tail = (kn * jnp.exp(g_last - gc)).astype(BF16)
    c_decay = jnp.exp(g_last)

    for cc in range(n_chunks):
        hs = slice(cc * heads, (cc + 1) * heads)
        ps = slice(cc * pairs, (cc + 1) * pairs)
        rs = chunk_rows(cc)
        s = s_ref[...]
        sb = s.astype(BF16)
        u = w_val[hs] - _bdot(k_cd[hs], sb)
        ub = u.astype(BF16)
        intra = _bdot(qk_bd[ps], ub.reshape(pairs, 2 * CHUNK, dv))
        o = _bdot(q_dec[hs], sb) + intra.reshape(heads, CHUNK, dv)
        s_ref[...] = s * c_decay[hs] + _bdot_tn(k_tail[hs], ub)

        o = o * lax.rsqrt(jnp.mean(o * o, axis=-1, keepdims=True) + EPS) * on_ref[...]
        for h in range(heads):
            z = z_ref[rs, h * dv:(h + 1) * dv]
            dn_ref[rs, h * dv:(h + 1) * dv] = o[h] * _silu(z)

    @pl.when(c == pl.num_programs(1) - 1)
    def _():
        sfin_ref[0] = s_ref[...]


def deltanet_prompt(proj, col, conv_w, a_log, dt_bias, o_norm, batch, seq, heads, dk, dv, rows):
    conv_ch = 2 * heads * dk + heads * dv
    zw = heads * dv
    steps = seq // rows

    def row_map(width_off):
        return lambda b, c: (b * steps + c, width_off)

    body = functools.partial(_dn_prompt_body, rows=rows, heads=heads, dk=dk, dv=dv)
    return pl.pallas_call(
        body,
        out_shape=(jax.ShapeDtypeStruct((batch * seq, zw), F32),
                   jax.ShapeDtypeStruct((batch, heads, dk, dv), F32)),
        grid=(batch, steps),
        in_specs=[
            pl.BlockSpec((rows, conv_ch), row_map(col["qkv"] // conv_ch)),
            pl.BlockSpec((rows, zw), row_map(col["z"] // zw)),
            pl.BlockSpec((rows, LANE), row_map(col["b"] // LANE)),
            pl.BlockSpec((rows, LANE), row_map(col["a"] // LANE)),
            _resident((CONV_WIDTH, conv_ch)),
            _resident((1, LANE)),
            _resident((1, LANE)),
            _resident((1, dv)),
        ],
        out_specs=(pl.BlockSpec((rows, zw), lambda b, c: (b * steps + c, 0)),
                   pl.BlockSpec((1, heads, dk, dv), lambda b, c: (b, 0, 0, 0))),
        scratch_shapes=[
            pltpu.VMEM((heads, dk, dv), F32),
            pltpu.VMEM((rows + SUBLANE, conv_ch), F32),
            pltpu.VMEM((rows, conv_ch), F32),
        ],
        compiler_params=_params("parallel", "arbitrary"),
        name="deltanet_prompt",
    )(proj, proj, proj, proj, conv_w, a_log, dt_bias, o_norm)


def _dn_sample_body(qkv_ref, z_ref, bl_ref, al_ref, cbuf_ref, st_ref, cw_ref, alog_ref, dtb_ref, on_ref,
                    dn_ref, sto_ref, cbo_ref, act_ref, *, nb, heads, dk, dv):
    wq = heads * dk
    new = qkv_ref[...]
    y = cbuf_ref[:, 0, :] * cw_ref[0:1, :]
    for j in range(1, CONV_WIDTH - 1):
        y = y + cbuf_ref[:, j, :] * cw_ref[j:j + 1, :]
    y = y + new * cw_ref[CONV_WIDTH - 1:CONV_WIDTH, :]
    act_ref[...] = _silu(y)
    for j in range(CONV_WIDTH - 2):
        cbo_ref[:, j, :] = cbuf_ref[:, j + 1, :]
    cbo_ref[:, CONV_WIDTH - 2, :] = new

    beta = jax.nn.sigmoid(bl_ref[...])
    g = -jnp.exp(alog_ref[...]) * jax.nn.softplus(al_ref[...] + dtb_ref[...])
    dec = jnp.exp(g)

    q_t, k_t = [], []
    for h in range(heads):
        q = act_ref[:, h * dk:(h + 1) * dk]
        k = act_ref[:, wq + h * dk:wq + (h + 1) * dk]
        qn = q * lax.rsqrt(jnp.sum(q * q, axis=-1, keepdims=True) + EPS) * (dk ** -0.5)
        kn = k * lax.rsqrt(jnp.sum(k * k, axis=-1, keepdims=True) + EPS)
        q_t.append(qn.T)
        k_t.append(kn.T)

    def per_head(fn):
        return jnp.stack([fn(h) for h in range(heads)], axis=0)

    for i in range(nb):
        row = slice(i, i + 1)
        kcol = per_head(lambda h: k_t[h][:, i:i + 1])
        qcol = per_head(lambda h: q_t[h][:, i:i + 1])
        v = per_head(lambda h: act_ref[row, 2 * wq + h * dv:2 * wq + (h + 1) * dv])
        z = per_head(lambda h: z_ref[row, h * dv:(h + 1) * dv])
        b_i = per_head(lambda h: beta[row, h:h + 1])
        d_i = per_head(lambda h: dec[row, h:h + 1])
        s = st_ref[i] * d_i
        u = b_i * (v - jnp.sum(kcol * s, axis=1, keepdims=True))
        s = s + kcol * u
        o = jnp.sum(qcol * s, axis=1, keepdims=True)
        sto_ref[i] = s
        o = o * lax.rsqrt(jnp.mean(o * o, axis=-1, keepdims=True) + EPS) * on_ref[...]
        o = o * _silu(z)
        for h in range(heads):
            dn_ref[row, h * dv:(h + 1) * dv] = o[h]


def deltanet_sample(proj, col, conv_buf, state, conv_w, a_log, dt_bias, o_norm, heads, dk, dv, nb):
    n_seq = proj.shape[0]
    conv_ch = 2 * heads * dk + heads * dv
    zw = heads * dv
    body = functools.partial(_dn_sample_body, nb=nb, heads=heads, dk=dk, dv=dv)
    return pl.pallas_call(
        body,
        out_shape=(jax.ShapeDtypeStruct((n_seq, zw), F32),
                   jax.ShapeDtypeStruct(state.shape, F32),
                   jax.ShapeDtypeStruct(conv_buf.shape, F32)),
        grid=(n_seq // nb,),
        in_specs=[
            pl.BlockSpec((nb, conv_ch), lambda i: (i, col["qkv"] // conv_ch)),
            pl.BlockSpec((nb, zw), lambda i: (i, col["z"] // zw)),
            pl.BlockSpec((nb, LANE), lambda i: (i, col["b"] // LANE)),
            pl.BlockSpec((nb, LANE), lambda i: (i, col["a"] // LANE)),
            pl.BlockSpec((nb, CONV_WIDTH - 1, conv_ch), lambda i: (i, 0, 0)),
            pl.BlockSpec((nb, heads, dk, dv), lambda i: (i, 0, 0, 0)),
            _resident((CONV_WIDTH, conv_ch)),
            _resident((1, LANE)),
            _resident((1, LANE)),
            _resident((1, dv)),
        ],
        out_specs=(pl.BlockSpec((nb, zw), lambda i: (i, 0)),
                   pl.BlockSpec((nb, heads, dk, dv), lambda i: (i, 0, 0, 0)),
                   pl.BlockSpec((nb, CONV_WIDTH - 1, conv_ch), lambda i: (i, 0, 0))),
        scratch_shapes=[
            pltpu.VMEM((nb, conv_ch), F32),
        ],
        compiler_params=_params("parallel"),
        name="deltanet_sample",
    )(proj, proj, proj, proj, conv_buf, state, conv_w, a_log, dt_bias, o_norm)


def _xattn_prompt_body(q_ref, k_ref, v_ref, o_ref, *, heads, dh):
    for h in range(heads):
        sl = slice(h * dh, (h + 1) * dh)
        q = q_ref[:, sl].astype(BF16)
        k = k_ref[:, sl].astype(BF16)
        v = v_ref[:, sl].astype(BF16)
        s = lax.dot_general(q, k, (((1,), (1,)), ((), ())), preferred_element_type=F32) * (dh ** -0.5)
        e = jnp.exp(s - jnp.max(s, axis=-1, keepdims=True))
        p = e / jnp.sum(e, axis=-1, keepdims=True)
        o_ref[:, sl] = _dot(p.astype(BF16), v)


def xattn_prompt(proj, cq_col, kv, batch, seq, n_mem, heads, dh, tq):
    xw = heads * dh
    steps = seq // tq
    body = functools.partial(_xattn_prompt_body, heads=heads, dh=dh)
    return pl.pallas_call(
        body,
        out_shape=jax.ShapeDtypeStruct((batch * seq, xw), F32),
        grid=(batch, steps),
        in_specs=[
            pl.BlockSpec((tq, xw), lambda b, i: (b * steps + i, cq_col // xw)),
            pl.BlockSpec((n_mem, xw), lambda b, i: (b, 0)),
            pl.BlockSpec((n_mem, xw), lambda b, i: (b, 1)),
        ],
        out_specs=pl.BlockSpec((tq, xw), lambda b, i: (b * steps + i, 0)),
        compiler_params=_params("parallel", "parallel"),
        name="xattn_prompt",
    )(proj, kv, kv)


def _xattn_sample_body(q_ref, k_ref, v_ref, o_ref, *, nb, heads, dh):
    rep = SUBLANE // heads
    n_t = k_ref.shape[2] // SUBLANE

    def over_groups(x, op):
        out = x
        for t in range(1, rep):
            out = op(out, pltpu.roll(x, t * heads, 0))
        return out

    for i in range(nb):
        row = slice(i, i + 1)
        q8 = jnp.concatenate([q_ref[row, h * dh:(h + 1) * dh] for h in range(heads)] * rep, axis=0)
        k3 = k_ref[0, i].reshape(n_t, SUBLANE, dh)
        v3 = v_ref[0, i].reshape(n_t, SUBLANE, dh)
        s = jnp.sum(k3 * q8[None], axis=-1, keepdims=True) * (dh ** -0.5)
        m = over_groups(jnp.max(s, axis=0), jnp.maximum)
        e = jnp.exp(s - m[None])
        den = over_groups(jnp.sum(e, axis=0), jnp.add)
        p = e / den[None]
        acc = over_groups(jnp.sum(p * v3, axis=0), jnp.add)
        for h in range(heads):
            o_ref[row, h * dh:(h + 1) * dh] = acc[h:h + 1, :]


def xattn_sample(proj, cq_col, cache_k, cache_v, layer, nb):
    n_layers, n_seq, n_mem, heads, dh = cache_k.shape
    assert SUBLANE % heads == 0 and (n_mem * heads) % SUBLANE == 0
    xw = heads * dh
    body = functools.partial(_xattn_sample_body, nb=nb, heads=heads, dh=dh)
    cache_spec = pl.BlockSpec((1, nb, n_mem * heads, dh), lambda i: (layer, i, 0, 0))
    flat = (n_layers, n_seq, n_mem * heads, dh)
    return pl.pallas_call(
        body,
        out_shape=jax.ShapeDtypeStruct((n_seq, xw), F32),
        grid=(n_seq // nb,),
        in_specs=[pl.BlockSpec((nb, xw), lambda i: (i, cq_col // xw)), cache_spec, cache_spec],
        out_specs=pl.BlockSpec((nb, xw), lambda i: (i, 0)),
        compiler_params=_params("parallel"),
        name="xattn_sample",
    )(proj, cache_k.reshape(flat), cache_v.reshape(flat))


def _pool_mix(sums, cnts, u_of, pw_ref, ps_ref, o_ref, gw):
    for gi in range(len(POOL_WINDOWS)):
        cols = slice(gi * gw, (gi + 1) * gw)
        d = sums[gi] / cnts[gi] - u_of(cols)
        o_ref[:, cols] = _wdot(d, pw_ref[gi]) * ps_ref[:, cols]


def _pool_prompt_body(u_ref, pw_ref, ps_ref, o_ref, xbuf, s2, s4, s8, *, tt, gw):
    t = pl.program_id(1)
    n = POOL_HALO + tt

    @pl.when(t == 0)
    def _():
        xbuf[0:POOL_HALO, :] = jnp.zeros((POOL_HALO, xbuf.shape[1]), F32)

    xbuf[POOL_HALO:n, :] = u_ref[...]
    s2[8:n, :] = xbuf[8:n, :] + xbuf[7:n - 1, :]
    s4[16:n, :] = s2[16:n, gw:] + s2[14:n - 2, gw:]
    s8[24:n, :] = s4[24:n, gw:] + s4[20:n - 4, gw:]
    s16 = s8[32:n, gw:] + s8[24:n - 8, gw:]
    sums = [s2[POOL_HALO:n, 0:gw], s4[POOL_HALO:n, 0:gw], s8[POOL_HALO:n, 0:gw], s16]
    pos = t * tt + lax.broadcasted_iota(jnp.int32, (tt, 1), 0)
    cnts = [jnp.minimum(pos + 1, w).astype(F32) for w in POOL_WINDOWS]
    _pool_mix(sums, cnts, lambda cols: u_ref[:, cols], pw_ref, ps_ref, o_ref, gw)
    xbuf[0:POOL_HALO, :] = xbuf[tt:n, :]


def pool_prompt(proj, pool_w, pool_scale, batch, seq, tt):
    pw_total = pool_scale.shape[-1]
    gw = pw_total // len(POOL_WINDOWS)
    steps = seq // tt
    n = POOL_HALO + tt
    body = functools.partial(_pool_prompt_body, tt=tt, gw=gw)
    return pl.pallas_call(
        body,
        out_shape=jax.ShapeDtypeStruct((batch * seq, pw_total), F32),
        grid=(batch, steps),
        in_specs=[
            pl.BlockSpec((tt, pw_total), lambda b, t: (b * steps + t, 0)),
            _resident(pool_w.shape),
            _resident((1, pw_total)),
        ],
        out_specs=pl.BlockSpec((tt, pw_total), lambda b, t: (b * steps + t, 0)),
        scratch_shapes=[
            pltpu.VMEM((n, pw_total), F32),
            pltpu.VMEM((n, pw_total), F32),
            pltpu.VMEM((n, pw_total - gw), F32),
            pltpu.VMEM((n, pw_total - 2 * gw), F32),
        ],
        compiler_params=_params("parallel", "arbitrary"),
        name="pool_prompt",
    )(proj, pool_w, pool_scale)


def _pool_sample_body(u_ref, st_ref, pw_ref, ps_ref, o_ref, sto_ref, *, gw, buf):
    pw_total = gw * len(POOL_WINDOWS)
    new = u_ref[:, 0:pw_total]
    sums, cnts = [], []
    for gi, w in enumerate(POOL_WINDOWS):
        cols = slice(gi * gw, (gi + 1) * gw)
        s = new[:, cols]
        for i in range(1, w):
            s = s + st_ref[:, buf - i, cols]
        sums.append(s)
        cnts.append(float(min(PAST_LEN + 1, w)))
    _pool_mix(sums, cnts, lambda cols: new[:, cols], pw_ref, ps_ref, o_ref, gw)
    sto_ref[:, 0:buf - 1, :] = st_ref[:, 1:buf, :]
    sto_ref[:, buf - 1, :] = new


def pool_sample(proj, state, pool_w, pool_scale):
    n_seq, buf, pw_total = state.shape
    gw = pw_total // len(POOL_WINDOWS)
    body = functools.partial(_pool_sample_body, gw=gw, buf=buf)
    return pl.pallas_call(
        body,
        out_shape=(jax.ShapeDtypeStruct((n_seq, pw_total), F32),
                   jax.ShapeDtypeStruct(state.shape, F32)),
        compiler_params=pltpu.CompilerParams(vmem_limit_bytes=VMEM_LIMIT_BYTES),
        name="pool_sample",
    )(proj, state, pool_w, pool_scale)


def _route_body(h_ref, g_ref, r_ref, idx_ref, gate_ref, *, n_experts):
    xn = _rms(h_ref[...], g_ref[...])
    logits = _wdot(xn, r_ref[...])
    lane = lax.broadcasted_iota(jnp.int32, logits.shape, 1)
    lm = jnp.where(lane < n_experts, logits, -jnp.inf)
    m1 = jnp.max(lm, axis=1, keepdims=True)
    i1 = jnp.min(jnp.where(lm == m1, lane, LANE), axis=1, keepdims=True)
    lm2 = jnp.where(lane == i1, -jnp.inf, lm)
    m2 = jnp.max(lm2, axis=1, keepdims=True)
    i2 = jnp.min(jnp.where(lm2 == m2, lane, LANE), axis=1, keepdims=True)
    e2 = jnp.exp(m2 - m1)
    den = 1.0 + e2
    idx_ref[...] = jnp.where(lane == 0, i1, jnp.where(lane == 1, i2, 0))
    gate_ref[...] = jnp.where(lane == 0, 1.0 / den, jnp.where(lane == 1, e2 / den, 0.0))


def moe_route(h_all, row_offset, rows, g, r_pad, tm, n_experts, name):
    d = h_all.shape[1]
    off = row_offset // tm
    body = functools.partial(_route_body, n_experts=n_experts)
    return pl.pallas_call(
        body,
        out_shape=(jax.ShapeDtypeStruct((rows, LANE), jnp.int32), jax.ShapeDtypeStruct((rows, LANE), F32)),
        grid=(rows // tm,),
        in_specs=[pl.BlockSpec((tm, d), lambda i: (i + off, 0)), _resident((1, d)), _resident((d, LANE))],
        out_specs=(pl.BlockSpec((tm, LANE), lambda i: (i, 0)), pl.BlockSpec((tm, LANE), lambda i: (i, 0))),
        compiler_params=_params("parallel"),
        name=name,
    )(h_all, g.reshape(1, d), r_pad)


def _start_row_copies(idx_ref, g0, n_groups, make_copy, priority_of):
    for t in range(n_groups):
        for u in range(SUBLANE):
            make_copy(idx_ref[0, 0, (g0 + t) * SUBLANE + u], g0 + t, u).start(priority=priority_of(u))


def _gather_queue(u):
    return 1


def _scatter_queue(u):
    return u % 2


def _experts_body(te_ref, nu_ref, g_first_ref, g_next_ref, s_prev_ref, s_cur_ref,
                  h_hbm, g_ref, wg_ref, wu_ref, wo_ref, y_hbm,
                  xbuf, acc, xn_ref, gsem, ssem, *, tm, gps):
    i = pl.program_id(0)
    j = pl.program_id(1)
    n_tiles = pl.num_programs(0)
    nj = pl.num_programs(1)
    n_used = nu_ref[0]
    slot = i % 2
    other = 1 - slot
    valid = i < n_used
    groups = tm // SUBLANE
    groups_pad = xbuf.shape[1]
    d = xn_ref.shape[1]

    def gather(s):
        return lambda row, grp, u: pltpu.make_async_copy(
            h_hbm.at[pl.ds(row, 1)], xbuf.at[s, grp, pl.ds(u, 1)], gsem.at[s])

    def scatter(s):
        return lambda row, grp, u: pltpu.make_async_copy(
            acc.at[s, grp, pl.ds(u, 1)], y_hbm.at[pl.ds(row, 1)], ssem.at[s])

    def start_all(idx_ref, make_copy, priority_of):
        def body(t, carry):
            _start_row_copies(idx_ref, t, 1, make_copy, priority_of)
            return carry
        lax.fori_loop(0, groups_pad, body, 0)

    def wait_gather(s):
        def body(t, carry):
            pltpu.make_async_copy(h_hbm.at[pl.ds(0, SUBLANE)], xbuf.at[s, t], gsem.at[s]).wait()
            return carry
        lax.fori_loop(0, groups_pad, body, 0)

    def wait_scatter(s):
        def body(t, carry):
            pltpu.make_async_copy(acc.at[s, t], y_hbm.at[pl.ds(0, SUBLANE)], ssem.at[s]).wait()
            return carry
        lax.fori_loop(0, groups_pad, body, 0)

    @pl.when(j == 0)
    def _():
        @pl.when(i == 0)
        def _():
            acc[...] = jnp.zeros_like(acc)
            start_all(g_first_ref, gather(0), _gather_queue)

        @pl.when(i <= n_used)
        def _():
            wait_gather(slot)

        @pl.when((i >= 1) & (i <= n_used))
        def _():
            wait_scatter(slot)

        @pl.when(valid)
        def _():
            xn_ref[...] = _rms(xbuf[slot, 0:groups].reshape(tm, d), g_ref[...]).astype(BF16)

        @pl.when(i == n_used)
        def _():
            start_all(s_prev_ref, scatter(other), _scatter_queue)
            wait_scatter(other)

    @pl.when(valid)
    def _():
        g0 = j * gps
        _start_row_copies(g_next_ref, g0, gps, gather(other), _gather_queue)
        _start_row_copies(s_prev_ref, g0, gps, scatter(other), _scatter_queue)

        xn = xn_ref[...]
        a = _dot(xn, wg_ref[0].astype(BF16))
        b = _dot(xn, wu_ref[0].astype(BF16))
        part = _dot((_silu(a) * b).astype(BF16), wo_ref[0].astype(BF16)).reshape(groups, SUBLANE, d)

        @pl.when(j == 0)
        def _():
            acc[slot, 0:groups] = part

        @pl.when(j > 0)
        def _():
            acc[slot, 0:groups] += part

    @pl.when(valid & (i == n_tiles - 1) & (j == nj - 1))
    def _():
        wait_gather(other)
        wait_scatter(other)
        start_all(s_cur_ref, scatter(slot), _scatter_queue)
        wait_scatter(slot)


def moe_experts(h_all, g, wi, wo, gather_rows, scatter_rows, tile_expert, n_used, y_rows, tm, tf):
    d = h_all.shape[1]
    f = wo.shape[1]
    nj = f // tf
    n_tiles = gather_rows.shape[0]
    rows_pad = gather_rows.shape[2]
    gps = rows_pad // SUBLANE // nj
    assert gps * nj * SUBLANE == rows_pad and rows_pad >= tm

    def wblock(col_off):
        def index_map(i, j, te, nu):
            jj = jnp.where(i < nu[0], j, nj - 1)
            return (te[i], 0, col_off + jj)
        return index_map

    def woblock(i, j, te, nu):
        return (te[i], jnp.where(i < nu[0], j, nj - 1), 0)

    def smem_tile(index_map):
        return pl.BlockSpec((1, 1, rows_pad), index_map, memory_space=pltpu.SMEM)

    grid_spec = pltpu.PrefetchScalarGridSpec(
        num_scalar_prefetch=2,
        grid=(n_tiles, nj),
        in_specs=[
            smem_tile(lambda i, j, te, nu: (0, 0, 0)),
            smem_tile(lambda i, j, te, nu: (jnp.minimum(i + 1, n_tiles - 1), 0, 0)),
            smem_tile(lambda i, j, te, nu: (i, 0, 0)),
            smem_tile(lambda i, j, te, nu: (i + 1, 0, 0)),
            pl.BlockSpec(memory_space=pl.ANY),
            pl.BlockSpec((1, d), lambda i, j, te, nu: (0, 0)),
            pl.BlockSpec((1, d, tf), wblock(0)),
            pl.BlockSpec((1, d, tf), wblock(nj)),
            pl.BlockSpec((1, tf, d), woblock),
        ],
        out_specs=pl.BlockSpec(memory_space=pl.ANY),
        scratch_shapes=[
            pltpu.VMEM((2, rows_pad // SUBLANE, SUBLANE, d), F32),
            pltpu.VMEM((2, rows_pad // SUBLANE, SUBLANE, d), F32),
            pltpu.VMEM((tm, d), BF16),
            pltpu.SemaphoreType.DMA((2,)),
            pltpu.SemaphoreType.DMA((2,)),
        ],
    )
    return pl.pallas_call(
        functools.partial(_experts_body, tm=tm, gps=gps),
        out_shape=jax.ShapeDtypeStruct((y_rows, d), F32),
        grid_spec=grid_spec,
        compiler_params=_params("arbitrary", "arbitrary"),
        name="moe_experts",
    )(tile_expert, n_used, gather_rows, gather_rows, scatter_rows, scatter_rows,
      h_all, g.reshape(1, d), wi, wi, wo)


def _combine_body(h_ref, gate_ref, y1_ref, y2_ref, fn_ref, o_ref, *, final_norm):
    gate = gate_ref[...]
    out = h_ref[...] + (gate[:, 0:1] * y1_ref[...] + gate[:, 1:2] * y2_ref[...])
    if final_norm:
        out = _rms(out, fn_ref[...])
    o_ref[...] = out


def moe_combine(h_all, row_offset, rows, gates, y, choice_stride, fn, tb, final_norm, name):
    d = h_all.shape[1]
    off = row_offset // tb
    off2 = (choice_stride + row_offset) // tb
    assert row_offset % tb == 0 and choice_stride % tb == 0
    body = functools.partial(_combine_body, final_norm=final_norm)
    return pl.pallas_call(
        body,
        out_shape=jax.ShapeDtypeStruct((rows, d), F32),
        grid=(rows // tb,),
        in_specs=[
            pl.BlockSpec((tb, d), lambda i: (i + off, 0)),
            pl.BlockSpec((tb, LANE), lambda i: (i, 0)),
            pl.BlockSpec((tb, d), lambda i: (i + off, 0)),
            pl.BlockSpec((tb, d), lambda i: (i + off2, 0)),
            _resident((1, d)),
        ],
        out_specs=pl.BlockSpec((tb, d), lambda i: (i, 0)),
        compiler_params=_params("parallel"),
        name=name,
    )(h_all, gates, y, y, fn.reshape(1, d))


def moe_plan(idx, n_experts, tm, rows_pad, choice_stride):
    n_tok, k = idx.shape
    n_slots = n_tok * k
    n_tiles = n_slots // tm + n_experts
    e_flat = idx.reshape(n_slots)
    onehot = (e_flat[:, None] == jnp.arange(n_experts, dtype=jnp.int32)[None, :]).astype(jnp.int32)
    csum = jnp.cumsum(onehot, axis=0)
    counts = csum[-1]
    rank = jnp.sum(csum * onehot, axis=1) - 1
    tiles_e = (counts + tm - 1) // tm
    tile_end = jnp.cumsum(tiles_e)
    pstart = (tile_end - tiles_e) * tm
    dest = jnp.sum(onehot * pstart[None, :], axis=1) + rank
    n_used = tile_end[-1]
    tile_ids = jnp.arange(n_tiles, dtype=jnp.int32)
    tile_expert = jnp.sum((tile_ids[:, None] >= tile_end[None, :]).astype(jnp.int32), axis=1)
    last_expert = jnp.sum((n_used - 1 >= tile_end).astype(jnp.int32))
    tile_expert = jnp.where(tile_ids < n_used, tile_expert, last_expert).astype(jnp.int32)
    slot_of = jnp.zeros((n_tiles * tm,), jnp.int32).at[dest].set(
        jnp.arange(1, n_slots + 1, dtype=jnp.int32), unique_indices=True).reshape(n_tiles, tm)
    filled = slot_of > 0
    tok = (slot_of - 1) // k
    choice = (slot_of - 1) % k
    gap = jnp.concatenate([jnp.arange(c * choice_stride + n_tok, (c + 1) * choice_stride, dtype=jnp.int32)
                           for c in range(k)])
    spare = jnp.concatenate([gap, k * choice_stride + jnp.arange(rows_pad, dtype=jnp.int32)])[:rows_pad]
    gather_rows = jnp.where(filled, tok, 0)
    scatter_rows = jnp.where(filled, choice * choice_stride + tok, spare[None, :tm])
    pad = rows_pad - tm
    gather_rows = jnp.concatenate([gather_rows, jnp.zeros((n_tiles, pad), jnp.int32)], axis=1)
    scatter_rows = jnp.concatenate(
        [scatter_rows, jnp.broadcast_to(spare[None, tm:], (n_tiles, pad))], axis=1)
    scatter_rows = jnp.concatenate([spare[None, :], scatter_rows], axis=0)
    return (gather_rows.reshape(n_tiles, 1, rows_pad).astype(jnp.int32),
            scatter_rows.reshape(n_tiles + 1, 1, rows_pad).astype(jnp.int32),
            tile_expert, n_used.reshape(1).astype(jnp.int32),
            k * choice_stride + max(0, rows_pad - k * (choice_stride - n_tok)))


def _pad_lanes(v):
    return jnp.zeros((1, LANE), F32).at[0, :v.shape[0]].set(v)


def kernel(x_prompt, x_sample, state_dn, state_dn_conv, state_pool, cache_mem_k, cache_mem_v, mem_prompt, a_norm1, a_w_in, a_conv_w, a_a_log, a_dt_bias, a_o_norm, a_w_out, a_norm2, a_ffn_wi, a_ffn_wo, b_norm1, b_w_in, b_pool_w, b_pool_scale, b_w_out, b_norm2, b_router, b_moe_wi, b_moe_wo, m_norm, m_w_k, m_w_v, final_norm):
    bp, seq, d = x_prompt.shape
    bs = x_sample.shape[0]
    assert x_sample.shape[1] == 1
    depth = m_norm.shape[0]
    assert depth % 2 == 0, "the output norm is fused into the expert block of the last (odd) layer"
    heads, dk, dv = state_dn.shape[2], state_dn.shape[3], state_dn.shape[4]
    conv_ch = state_dn_conv.shape[-1]
    qk_w, v_w = heads * dk, heads * dv
    assert conv_ch == 2 * qk_w + v_w and dk == LANE and dv == LANE and heads <= LANE
    n_mem, x_heads, x_dh = cache_mem_k.shape[2], cache_mem_k.shape[3], cache_mem_k.shape[4]
    xw = x_heads * x_dh
    pool_w_total = state_pool.shape[-1]
    pool_buf = state_pool.shape[2]
    d_ff = a_ffn_wo.shape[1]

    hp = x_prompt.reshape(bp * seq, d)
    hs = x_sample.reshape(bs, d)

    o1, o2, o3, o4 = conv_ch, conv_ch + v_w, conv_ch + v_w + heads, conv_ch + v_w + 2 * heads
    col_a = {"qkv": 0, "z": conv_ch, "cq": conv_ch + v_w, "b": conv_ch + v_w + xw, "a": conv_ch + v_w + xw + LANE}
    n_a = -(-(col_a["a"] + LANE) // 512) * 512

    dn_p, conv_p, pool_p, mk_all, mv_all = [], [], [], [], []
    dn_s, conv_s, pool_s = [], [], []
    for i in range(depth):
        j = i // 2
        w_kv = jnp.concatenate([m_w_k[i], m_w_v[i]], axis=1).astype(BF16)
        kv = rms_matmul(mem_prompt.reshape(bp * n_mem, d), m_norm[i], w_kv, 256, f"mem_kv_{i}")
        mk_all.append(kv[:, :xw].reshape(bp, n_mem, x_heads, x_dh))
        mv_all.append(kv[:, xw:].reshape(bp, n_mem, x_heads, x_dh))
        if i % 2 == 0:
            w = a_w_in[j]
            w_in = jnp.zeros((d, n_a), F32)
            w_in = w_in.at[:, :o2].set(w[:, :o2])
            w_in = w_in.at[:, col_a["cq"]:col_a["cq"] + xw].set(w[:, o4:])
            w_in = w_in.at[:, col_a["b"]:col_a["b"] + heads].set(w[:, o2:o3])
            w_in = w_in.at[:, col_a["a"]:col_a["a"] + heads].set(w[:, o3:o4])
            a_log, dt_b = _pad_lanes(a_a_log[j]), _pad_lanes(a_dt_bias[j])
            o_norm = a_o_norm[j].reshape(1, dv)

            proj_p = rms_matmul(hp, a_norm1[j], w_in.astype(BF16), 256, f"a_in_p_{i}")
            proj_s = rms_matmul(hs, a_norm1[j], w_in, bs, f"a_in_s_{i}", tn=512)
            mix_p, s_p = deltanet_prompt(proj_p, col_a, a_conv_w[j], a_log, dt_b, o_norm,
                                         bp, seq, heads, dk, dv, rows=DN_STEP_CHUNKS * CHUNK)
            mix_s, s_s, c_s = deltanet_sample(proj_s, col_a, state_dn_conv[j], state_dn[j], a_conv_w[j],
                                              a_log, dt_b, o_norm, heads, dk, dv, nb=SUBLANE)
            xa_p = xattn_prompt(proj_p, col_a["cq"], kv, bp, seq, n_mem, x_heads, x_dh, tq=512)
            xa_s = xattn_sample(proj_s, col_a["cq"], cache_mem_k, cache_mem_v, i, nb=SUBLANE)
            hp = out_proj(hp, mix_p, xa_p, a_w_out[j].astype(BF16), 512, f"a_out_p_{i}")
            hs = out_proj(hs, mix_s, xa_s, a_w_out[j], bs, f"a_out_s_{i}")
            hp = ffn_dense(hp, a_norm2[j], a_ffn_wi[j].astype(BF16), a_ffn_wo[j].astype(BF16), 256,
                           f"a_ffn_p_{i}")
            hs = ffn_dense(hs, a_norm2[j], a_ffn_wi[j], a_ffn_wo[j], bs, f"a_ffn_s_{i}", tf=256)
            dn_p.append(s_p)
            conv_p.append(proj_p.reshape(bp, seq, n_a)[:, seq - (CONV_WIDTH - 1):, :conv_ch])
            dn_s.append(s_s)
            conv_s.append(c_s)
        else:
            ps = b_pool_scale[j].reshape(1, pool_w_total)
            last = i == depth - 1

            proj_p = rms_matmul(hp, b_norm1[j], b_w_in[j].astype(BF16), 512, f"b_in_p_{i}")
            proj_s = rms_matmul(hs, b_norm1[j], b_w_in[j], bs, f"b_in_s_{i}")
            mix_p = pool_prompt(proj_p, b_pool_w[j].astype(BF16), ps, bp, seq, tt=512)
            mix_s, buf_s = pool_sample(proj_s, state_pool[j], b_pool_w[j], ps)
            xa_p = xattn_prompt(proj_p, pool_w_total, kv, bp, seq, n_mem, x_heads, x_dh, tq=512)
            xa_s = xattn_sample(proj_s, pool_w_total, cache_mem_k, cache_mem_v, i, nb=SUBLANE)
            n_p = bp * seq
            h_all = out_proj(hp, mix_p, xa_p, b_w_out[j].astype(BF16), 512, f"b_out_p_{i}", out_rows=n_p + bs)
            h_all = out_proj(hs, mix_s, xa_s, b_w_out[j], bs, f"b_out_s_{i}", out_rows=n_p + bs,
                             into=h_all, row_offset=n_p)
            n_experts = b_router.shape[-1]
            r_pad = jnp.zeros((d, LANE), F32).at[:, :n_experts].set(b_router[j])
            idx_p, gate_p = moe_route(h_all, 0, n_p, b_norm2[j], r_pad.astype(BF16), 512, n_experts,
                                      f"b_route_p_{i}")
            idx_s, gate_s = moe_route(h_all, n_p, bs, b_norm2[j], r_pad, bs, n_experts, f"b_route_s_{i}")
            idx = jnp.concatenate([idx_p[:, :TOP_K], idx_s[:, :TOP_K]], axis=0)
            nj = b_moe_wo.shape[2] // MOE_FF_TILE
            rows_pad = -(-(MOE_TILE // SUBLANE) // nj) * nj * SUBLANE
            choice_stride = -(-(n_p + bs) // 512) * 512
            gather_rows, scatter_rows, tile_expert, n_used, y_rows = moe_plan(
                idx, n_experts, MOE_TILE, rows_pad, choice_stride)
            y = moe_experts(h_all, b_norm2[j], b_moe_wi[j], b_moe_wo[j], gather_rows, scatter_rows,
                            tile_expert, n_used, y_rows, MOE_TILE, MOE_FF_TILE)
            hp = moe_combine(h_all, 0, n_p, gate_p, y, choice_stride, final_norm, 512, last,
                             f"b_combine_p_{i}")
            hs = moe_combine(h_all, n_p, bs, gate_s, y, choice_stride, final_norm, bs, last,
                             f"b_combine_s_{i}")
            pool_p.append(proj_p.reshape(bp, seq, -1)[:, seq - pool_buf:, :pool_w_total])
            pool_s.append(buf_s)

    y_prompt = hp.reshape(bp, seq, d)
    y_sample = hs.reshape(bs, 1, d)
    return (y_prompt, y_sample, jnp.stack(dn_p), jnp.stack(conv_p), jnp.stack(pool_p),
            jnp.stack(mk_all), jnp.stack(mv_all), jnp.stack(dn_s), jnp.stack(conv_s), jnp.stack(pool_s))
```

```python
import functools

import jax
import jax.numpy as jnp
from jax import lax
from jax.experimental import pallas as pl
from jax.experimental.pallas import tpu as pltpu

F32 = jnp.float32
BF16 = jnp.bfloat16
HIGHEST = lax.Precision.HIGHEST

EPS = 1e-6
CHUNK = 64
CONV_WIDTH = 4
POOL_WINDOWS = (2, 4, 8, 16)
PAST_LEN = 16384
TOP_K = 2

LANE = 128
SUBLANE = 8
VMEM_LIMIT_BYTES = 56 * 2**20

MOE_TILE = 1024
MOE_FF_TILE = 512
MOE_ROW_BLOCK = 512
DN_STEP_CHUNKS = 4
INV_BLOCK = 16
POOL_HALO = 32


def _params(*sem):
    return pltpu.CompilerParams(dimension_semantics=sem, vmem_limit_bytes=VMEM_LIMIT_BYTES)


def _resident(shape):
    nd = len(shape)
    return pl.BlockSpec(shape, lambda *_: (0,) * nd, pipeline_mode=pl.Buffered(1))


def _rms(x, w):
    return x * lax.rsqrt(jnp.mean(x * x, axis=-1, keepdims=True) + EPS) * w


def _silu(x):
    return x * jax.nn.sigmoid(x)


def _dot(a, b):
    return jnp.dot(a, b, preferred_element_type=F32)


def _wdot(a, w):
    if w.dtype == F32:
        return jnp.dot(a, w, precision=HIGHEST, preferred_element_type=F32)
    return jnp.dot(a.astype(BF16), w, preferred_element_type=F32)


def _bdot(a, b):
    return lax.dot_general(a, b, (((2,), (1,)), ((0,), (0,))), preferred_element_type=F32)


def _bdot_nt(a, b):
    return lax.dot_general(a, b, (((2,), (2,)), ((0,), (0,))), preferred_element_type=F32)


def _bdot_tn(a, b):
    return lax.dot_general(a, b, (((1,), (1,)), ((0,), (0,))), preferred_element_type=F32)


def _rms_matmul_body(x_ref, g_ref, w_ref, o_ref):
    xn = _rms(x_ref[...], g_ref[...])
    o_ref[...] = _wdot(xn, w_ref[...])


def _wspec(shape, index_map, steps):
    if steps == 1:
        return pl.BlockSpec(shape, index_map, pipeline_mode=pl.Buffered(1))
    return pl.BlockSpec(shape, index_map)


def rms_matmul(x, g, w, tm, name, tn=None):
    m, d = x.shape
    n = w.shape[1]
    tn = n if tn is None else tn
    nj = n // tn
    return pl.pallas_call(
        _rms_matmul_body,
        out_shape=jax.ShapeDtypeStruct((m, n), F32),
        grid=(m // tm, nj),
        in_specs=[pl.BlockSpec((tm, d), lambda i, j: (i, 0)), _resident((1, d)),
                  _wspec((d, tn), lambda i, j: (0, j), nj)],
        out_specs=pl.BlockSpec((tm, tn), lambda i, j: (i, j)),
        compiler_params=_params("parallel", "arbitrary"),
        name=name,
    )(x, g.reshape(1, d), w)


def _out_proj_body(x_ref, a1_ref, a2_ref, w1_ref, w2_ref, *rest, n_main):
    o_ref = rest[-1]

    @pl.when(pl.program_id(0) < n_main)
    def _():
        acc = _wdot(a1_ref[...], w1_ref[...]) + _wdot(a2_ref[...], w2_ref[...])
        o_ref[...] = x_ref[...] + acc

    @pl.when(pl.program_id(0) >= n_main)
    def _():
        o_ref[...] = jnp.zeros_like(o_ref)


def out_proj(x, a1, a2, w, tm, name, out_rows=None, into=None, row_offset=0):
    m, d = x.shape
    k1, k2 = a1.shape[1], a2.shape[1]
    assert w.shape[0] == k1 + k2 and k1 % k2 == 0 and row_offset % tm == 0
    out_rows = m if out_rows is None else out_rows
    off = row_offset // tm
    n_main = m // tm
    n_extra = 0 if (into is not None or out_rows == m) else 1
    assert out_rows - m <= n_extra * tm or into is not None
    last = n_main - 1
    in_specs = [
        pl.BlockSpec((tm, d), lambda i: (jnp.minimum(i, last), 0)),
        pl.BlockSpec((tm, k1), lambda i: (jnp.minimum(i, last), 0)),
        pl.BlockSpec((tm, k2), lambda i: (jnp.minimum(i, last), 0)),
        pl.BlockSpec((k1, d), lambda i: (0, 0), pipeline_mode=pl.Buffered(1)),
        pl.BlockSpec((k2, d), lambda i: (k1 // k2, 0), pipeline_mode=pl.Buffered(1)),
    ]
    args = [x, a1, a2, w, w]
    aliases = {}
    if into is not None:
        in_specs.append(pl.BlockSpec(memory_space=pl.ANY))
        args.append(into)
        aliases = {len(args) - 1: 0}
    return pl.pallas_call(
        functools.partial(_out_proj_body, n_main=n_main),
        out_shape=jax.ShapeDtypeStruct((out_rows, d), F32),
        grid=(n_main + n_extra,),
        in_specs=in_specs,
        out_specs=pl.BlockSpec((tm, d), lambda i: (i + off, 0)),
        input_output_aliases=aliases,
        compiler_params=_params("parallel"),
        name=name,
    )(*args)


def _ffn_body(h_ref, g_ref, wg_ref, wu_ref, wo_ref, o_ref, xn_ref):
    @pl.when(pl.program_id(1) == 0)
    def _():
        h = h_ref[...]
        xn_ref[...] = _rms(h, g_ref[...])
        o_ref[...] = h

    xn = xn_ref[...]
    a = _wdot(xn, wg_ref[...])
    b = _wdot(xn, wu_ref[...])
    o_ref[...] += _wdot(_silu(a) * b, wo_ref[...])


def ffn_dense(h, g, wi, wo, tm, name, tf=None):
    m, d = h.shape
    f = wo.shape[0]
    tf = f if tf is None else tf
    nj = f // tf
    return pl.pallas_call(
        _ffn_body,
        out_shape=jax.ShapeDtypeStruct((m, d), F32),
        grid=(m // tm, nj),
        in_specs=[
            pl.BlockSpec((tm, d), lambda i, j: (i, 0)),
            _resident((1, d)),
            _wspec((d, tf), lambda i, j: (0, j), nj),
            _wspec((d, tf), lambda i, j: (0, nj + j), nj),
            _wspec((tf, d), lambda i, j: (j, 0), nj),
        ],
        out_specs=pl.BlockSpec((tm, d), lambda i, j: (i, 0)),
        scratch_shapes=[pltpu.VMEM((tm, d), F32)],
        compiler_params=_params("parallel", "arbitrary"),
        name=name,
    )(h, g.reshape(1, d), wi, wi, wo)


def _pair_blockdiag(y, left):
    yb = y.astype(BF16)
    zero = jnp.zeros_like(yb)
    return jnp.concatenate([jnp.where(left, yb, zero), jnp.where(left, zero, yb)], axis=1)


def _pair_mm(x, y, left):
    return _bdot(x.astype(BF16), _pair_blockdiag(y, left))


def _unit_lower_inverse(mx, eye, blk, left):
    mm = functools.partial(_pair_mm, left=left)
    md = jnp.where(blk, mx, 0.0)
    c = mx - md
    p = eye - md
    m2 = mm(md, md)
    p = p + mm(p, m2)
    m4 = mm(m2, m2)
    p = p + mm(p, m4)
    m8 = mm(m4, m4)
    td = p + mm(p, m8)
    n = mm(td, c)
    n2 = mm(n, n)
    q = eye - n
    q = q + mm(q, n2)
    return mm(q, td)


def _dn_prompt_body(qkv_ref, z_ref, bl_ref, al_ref, cw_ref, alog_ref, dtb_ref, on_ref,
                    dn_ref, sfin_ref, s_ref, xbuf_ref, act_ref, *, rows, heads, dk, dv):
    c = pl.program_id(1)
    wq = heads * dk

    @pl.when(c == 0)
    def _():
        s_ref[...] = jnp.zeros_like(s_ref)
        xbuf_ref[0:SUBLANE, :] = jnp.zeros((SUBLANE, xbuf_ref.shape[1]), F32)

    xbuf_ref[SUBLANE:SUBLANE + rows, :] = qkv_ref[...]
    first = SUBLANE - (CONV_WIDTH - 1)
    y = xbuf_ref[pl.ds(first, rows), :] * cw_ref[0:1, :]
    for j in range(1, CONV_WIDTH):
        y = y + xbuf_ref[pl.ds(first + j, rows), :] * cw_ref[j:j + 1, :]
    act_ref[...] = _silu(y)
    xbuf_ref[0:SUBLANE, :] = xbuf_ref[rows:rows + SUBLANE, :]

    pairs = heads // 2
    ri = lax.broadcasted_iota(jnp.int32, (CHUNK, 2 * CHUNK), 0)
    lane = lax.broadcasted_iota(jnp.int32, (CHUNK, 2 * CHUNK), 1)
    cj = lane % CHUNK
    left = (lane < CHUNK)[None]
    incl = (ri >= cj)[None]
    strict = (ri > cj)[None]
    blk = ((ri // INV_BLOCK) == (cj // INV_BLOCK))[None]
    eye = (ri == cj).astype(F32)[None]
    tr = lax.broadcasted_iota(jnp.int32, (CHUNK, CHUNK), 0)
    tc = lax.broadcasted_iota(jnp.int32, (CHUNK, CHUNK), 1)
    tri = (tr >= tc).astype(F32)

    n_chunks = rows // CHUNK
    zero = jnp.zeros((CHUNK, dk), BF16)

    def chunk_rows(cc):
        return slice(cc * CHUNK, (cc + 1) * CHUNK)

    def per_head(fn):
        return jnp.stack([fn(chunk_rows(cc), h) for cc in range(n_chunks) for h in range(heads)], axis=0)

    def per_pair(fn):
        return jnp.stack([fn(cc, p) for cc in range(n_chunks) for p in range(pairs)], axis=0)

    q3 = per_head(lambda rs, h: act_ref[rs, h * dk:(h + 1) * dk])
    k3 = per_head(lambda rs, h: act_ref[rs, wq + h * dk:wq + (h + 1) * dk])
    v3 = per_head(lambda rs, h: act_ref[rs, 2 * wq + h * dv:2 * wq + (h + 1) * dv])
    qn = q3 * lax.rsqrt(jnp.sum(q3 * q3, axis=-1, keepdims=True) + EPS) * (dk ** -0.5)
    kn = k3 * lax.rsqrt(jnp.sum(k3 * k3, axis=-1, keepdims=True) + EPS)

    beta = jax.nn.sigmoid(bl_ref[...])
    g = -jnp.exp(alog_ref[...]) * jax.nn.softplus(al_ref[...] + dtb_ref[...])
    gcum = [jnp.dot(tri, g[chunk_rows(cc)], precision=HIGHEST, preferred_element_type=F32)
            for cc in range(n_chunks)]
    gcum_t = [jnp.concatenate([gc_, pltpu.roll(gc_, LANE - 1, 1)], axis=0).T for gc_ in gcum]

    def pair_cols(x_of):
        return per_pair(lambda cc, p: jnp.where(left[0], x_of(cc)[:, 2 * p:2 * p + 1],
                                                x_of(cc)[:, 2 * p + 1:2 * p + 2]))

    gc2 = pair_cols(lambda cc: gcum[cc])
    bc2 = pair_cols(lambda cc: beta[chunk_rows(cc)])
    gr2 = per_pair(lambda cc, p: gcum_t[cc][2 * p:2 * p + 1, :])
    decay = jnp.where(incl, jnp.exp(gc2 - gr2), 0.0)

    kb = kn.astype(BF16)
    qb = qn.astype(BF16)

    def pair_of(x, cc, p):
        return x[cc * heads + 2 * p], x[cc * heads + 2 * p + 1]

    k_lhs = per_pair(lambda cc, p: jnp.concatenate(pair_of(kb, cc, p), axis=1))
    q_lhs = per_pair(lambda cc, p: jnp.concatenate(pair_of(qb, cc, p), axis=1))
    k_bd = per_pair(lambda cc, p: jnp.concatenate(
        [jnp.concatenate([pair_of(kb, cc, p)[0], zero], axis=1),
         jnp.concatenate([zero, pair_of(kb, cc, p)[1]], axis=1)], axis=0))
    kk = _bdot_nt(k_lhs, k_bd)
    mx = jnp.where(strict, bc2 * kk * decay, 0.0)
    tinv = _unit_lower_inverse(mx, eye, blk, left)

    gcum_rows = jnp.concatenate(gcum, axis=0)
    gc = per_head(lambda rs, h: gcum_rows[rs, h:h + 1])
    bc = per_head(lambda rs, h: beta[rs, h:h + 1])
    eg = jnp.exp(gc)
    rhs = jnp.concatenate([v3 * bc, kn * (bc * eg)], axis=-1)
    sol = _bdot(_pair_blockdiag(tinv, left), rhs.astype(BF16).reshape(n_chunks * pairs, 2 * CHUNK, dv + dk))
    sol = sol.reshape(n_chunks * heads, CHUNK, dv + dk)
    w_val = sol[..., :dv]
    k_cd = sol[..., dv:].astype(BF16)
    qk_bd = _pair_blockdiag(_bdot_nt(q_lhs, k_bd) * decay, left)
    q_dec = (qn * eg).astype(BF16)
    g_last = gc[:, CHUNK - 1:CHUNK, :]
    k_tail = (kn * jnp.exp(g_last - gc)).astype(BF16)
    c_decay = jnp.exp(g_last)

    for cc in range(n_chunks):
        hs = slice(cc * heads, (cc + 1) * heads)
        ps = slice(cc * pairs, (cc + 1) * pairs)
        rs = chunk_rows(cc)
        s = s_ref[...]
        sb = s.astype(BF16)
        u = w_val[hs] - _bdot(k_cd[hs], sb)
        ub = u.astype(BF16)
        intra = _bdot(qk_bd[ps], ub.reshape(pairs, 2 * CHUNK, dv))
        o = _bdot(q_dec[hs], sb) + intra.reshape(heads, CHUNK, dv)
        s_ref[...] = s * c_decay[hs] + _bdot_tn(k_tail[hs], ub)

        o = o * lax.rsqrt(jnp.mean(o * o, axis=-1, keepdims=True) + EPS) * on_ref[...]
        for h in range(heads):
            z = z_ref[rs, h * dv:(h + 1) * dv]
            dn_ref[rs, h * dv:(h + 1) * dv] = o[h] * _silu(z)

    @pl.when(c == pl.num_programs(1) - 1)
    def _():
        sfin_ref[0] = s_ref[...]


def deltanet_prompt(proj, col, conv_w, a_log, dt_bias, o_norm, batch, seq, heads, dk, dv, rows):
    conv_ch = 2 * heads * dk + heads * dv
    zw = heads * dv
    steps = seq // rows

    def row_map(width_off):
        return lambda b, c: (b * steps + c, width_off)

    body = functools.partial(_dn_prompt_body, rows=rows, heads=heads, dk=dk, dv=dv)
    return pl.pallas_call(
        body,
        out_shape=(jax.ShapeDtypeStruct((batch * seq, zw), F32),
                   jax.ShapeDtypeStruct((batch, heads, dk, dv), F32)),
        grid=(batch, steps),
        in_specs=[
            pl.BlockSpec((rows, conv_ch), row_map(col["qkv"] // conv_ch)),
            pl.BlockSpec((rows, zw), row_map(col["z"] // zw)),
            pl.BlockSpec((rows, LANE), row_map(col["b"] // LANE)),
            pl.BlockSpec((rows, LANE), row_map(col["a"] // LANE)),
            _resident((CONV_WIDTH, conv_ch)),
            _resident((1, LANE)),
            _resident((1, LANE)),
            _resident((1, dv)),
        ],
        out_specs=(pl.BlockSpec((rows, zw), lambda b, c: (b * steps + c, 0)),
                   pl.BlockSpec((1, heads, dk, dv), lambda b, c: (b, 0, 0, 0))),
        scratch_shapes=[
            pltpu.VMEM((heads, dk, dv), F32),
            pltpu.VMEM((rows + SUBLANE, conv_ch), F32),
            pltpu.VMEM((rows, conv_ch), F32),
        ],
        compiler_params=_params("parallel", "arbitrary"),
        name="deltanet_prompt",
    )(proj, proj, proj, proj, conv_w, a_log, dt_bias, o_norm)


def _dn_sample_body(qkv_ref, z_ref, bl_ref, al_ref, cbuf_ref, st_ref, cw_ref, alog_ref, dtb_ref, on_ref,
                    dn_ref, sto_ref, cbo_ref, act_ref, *, nb, heads, dk, dv):
    wq = heads * dk
    new = qkv_ref[...]
    y = cbuf_ref[:, 0, :] * cw_ref[0:1, :]
    for j in range(1, CONV_WIDTH - 1):
        y = y + cbuf_ref[:, j, :] * cw_ref[j:j + 1, :]
    y = y + new * cw_ref[CONV_WIDTH - 1:CONV_WIDTH, :]
    act_ref[...] = _silu(y)
    for j in range(CONV_WIDTH - 2):
        cbo_ref[:, j, :] = cbuf_ref[:, j + 1, :]
    cbo_ref[:, CONV_WIDTH - 2, :] = new

    beta = jax.nn.sigmoid(bl_ref[...])
    g = -jnp.exp(alog_ref[...]) * jax.nn.softplus(al_ref[...] + dtb_ref[...])
    dec = jnp.exp(g)

    q_t, k_t = [], []
    for h in range(heads):
        q = act_ref[:, h * dk:(h + 1) * dk]
        k = act_ref[:, wq + h * dk:wq + (h + 1) * dk]
        qn = q * lax.rsqrt(jnp.sum(q * q, axis=-1, keepdims=True) + EPS) * (dk ** -0.5)
        kn = k * lax.rsqrt(jnp.sum(k * k, axis=-1, keepdims=True) + EPS)
        q_t.append(qn.T)
        k_t.append(kn.T)

    def per_head(fn):
        return jnp.stack([fn(h) for h in range(heads)], axis=0)

    for i in range(nb):
        row = slice(i, i + 1)
        kcol = per_head(lambda h: k_t[h][:, i:i + 1])
        qcol = per_head(lambda h: q_t[h][:, i:i + 1])
        v = per_head(lambda h: act_ref[row, 2 * wq + h * dv:2 * wq + (h + 1) * dv])
        z = per_head(lambda h: z_ref[row, h * dv:(h + 1) * dv])
        b_i = per_head(lambda h: beta[row, h:h + 1])
        d_i = per_head(lambda h: dec[row, h:h + 1])
        s = st_ref[i] * d_i
        u = b_i * (v - jnp.sum(kcol * s, axis=1, keepdims=True))
        s = s + kcol * u
        o = jnp.sum(qcol * s, axis=1, keepdims=True)
        sto_ref[i] = s
        o = o * lax.rsqrt(jnp.mean(o * o, axis=-1, keepdims=True) + EPS) * on_ref[...]
        o = o * _silu(z)
        for h in range(heads):
            dn_ref[row, h * dv:(h + 1) * dv] = o[h]


def deltanet_sample(proj, col, conv_buf, state, conv_w, a_log, dt_bias, o_norm, heads, dk, dv, nb):
    n_seq = proj.shape[0]
    conv_ch = 2 * heads * dk + heads * dv
    zw = heads * dv
    body = functools.partial(_dn_sample_body, nb=nb, heads=heads, dk=dk, dv=dv)
    return pl.pallas_call(
        body,
        out_shape=(jax.ShapeDtypeStruct((n_seq, zw), F32),
                   jax.ShapeDtypeStruct(state.shape, F32),
                   jax.ShapeDtypeStruct(conv_buf.shape, F32)),
        grid=(n_seq // nb,),
        in_specs=[
            pl.BlockSpec((nb, conv_ch), lambda i: (i, col["qkv"] // conv_ch)),
            pl.BlockSpec((nb, zw), lambda i: (i, col["z"] // zw)),
            pl.BlockSpec((nb, LANE), lambda i: (i, col["b"] // LANE)),
            pl.BlockSpec((nb, LANE), lambda i: (i, col["a"] // LANE)),
            pl.BlockSpec((nb, CONV_WIDTH - 1, conv_ch), lambda i: (i, 0, 0)),
            pl.BlockSpec((nb, heads, dk, dv), lambda i: (i, 0, 0, 0)),
            _resident((CONV_WIDTH, conv_ch)),
            _resident((1, LANE)),
            _resident((1, LANE)),
            _resident((1, dv)),
        ],
        out_specs=(pl.BlockSpec((nb, zw), lambda i: (i, 0)),
                   pl.BlockSpec((nb, heads, dk, dv), lambda i: (i, 0, 0, 0)),
                   pl.BlockSpec((nb, CONV_WIDTH - 1, conv_ch), lambda i: (i, 0, 0))),
        scratch_shapes=[
            pltpu.VMEM((nb, conv_ch), F32),
        ],
        compiler_params=_params("parallel"),
        name="deltanet_sample",
    )(proj, proj, proj, proj, conv_buf, state, conv_w, a_log, dt_bias, o_norm)


def _xattn_prompt_body(q_ref, k_ref, v_ref, o_ref, *, heads, dh):
    for h in range(heads):
        sl = slice(h * dh, (h + 1) * dh)
        q = q_ref[:, sl].astype(BF16)
        k = k_ref[:, sl].astype(BF16)
        v = v_ref[:, sl].astype(BF16)
        s = lax.dot_general(q, k, (((1,), (1,)), ((), ())), preferred_element_type=F32) * (dh ** -0.5)
        e = jnp.exp(s - jnp.max(s, axis=-1, keepdims=True))
        p = e / jnp.sum(e, axis=-1, keepdims=True)
        o_ref[:, sl] = _dot(p.astype(BF16), v)


def xattn_prompt(proj, cq_col, kv, batch, seq, n_mem, heads, dh, tq):
    xw = heads * dh
    steps = seq // tq
    body = functools.partial(_xattn_prompt_body, heads=heads, dh=dh)
    return pl.pallas_call(
        body,
        out_shape=jax.ShapeDtypeStruct((batch * seq, xw), F32),
        grid=(batch, steps),
        in_specs=[
            pl.BlockSpec((tq, xw), lambda b, i: (b * steps + i, cq_col // xw)),
            pl.BlockSpec((n_mem, xw), lambda b, i: (b, 0)),
            pl.BlockSpec((n_mem, xw), lambda b, i: (b, 1)),
        ],
        out_specs=pl.BlockSpec((tq, xw), lambda b, i: (b * steps + i, 0)),
        compiler_params=_params("parallel", "parallel"),
        name="xattn_prompt",
    )(proj, kv, kv)


def _xattn_sample_body(q_ref, k_ref, v_ref, o_ref, *, nb, heads, dh):
    rep = SUBLANE // heads
    n_t = k_ref.shape[2] // SUBLANE

    def over_groups(x, op):
        out = x
        for t in range(1, rep):
            out = op(out, pltpu.roll(x, t * heads, 0))
        return out

    for i in range(nb):
        row = slice(i, i + 1)
        q8 = jnp.concatenate([q_ref[row, h * dh:(h + 1) * dh] for h in range(heads)] * rep, axis=0)
        k3 = k_ref[0, i].reshape(n_t, SUBLANE, dh)
        v3 = v_ref[0, i].reshape(n_t, SUBLANE, dh)
        s = jnp.sum(k3 * q8[None], axis=-1, keepdims=True) * (dh ** -0.5)
        m = over_groups(jnp.max(s, axis=0), jnp.maximum)
        e = jnp.exp(s - m[None])
        den = over_groups(jnp.sum(e, axis=0), jnp.add)
        p = e / den[None]
        acc = over_groups(jnp.sum(p * v3, axis=0), jnp.add)
        for h in range(heads):
            o_ref[row, h * dh:(h + 1) * dh] = acc[h:h + 1, :]


def xattn_sample(proj, cq_col, cache_k, cache_v, layer, nb):
    n_layers, n_seq, n_mem, heads, dh = cache_k.shape
    assert SUBLANE % heads == 0 and (n_mem * heads) % SUBLANE == 0
    xw = heads * dh
    body = functools.partial(_xattn_sample_body, nb=nb, heads=heads, dh=dh)
    cache_spec = pl.BlockSpec((1, nb, n_mem * heads, dh), lambda i: (layer, i, 0, 0))
    flat = (n_layers, n_seq, n_mem * heads, dh)
    return pl.pallas_call(
        body,
        out_shape=jax.ShapeDtypeStruct((n_seq, xw), F32),
        grid=(n_seq // nb,),
        in_specs=[pl.BlockSpec((nb, xw), lambda i: (i, cq_col // xw)), cache_spec, cache_spec],
        out_specs=pl.BlockSpec((nb, xw), lambda i: (i, 0)),
        compiler_params=_params("parallel"),
        name="xattn_sample",
    )(proj, cache_k.reshape(flat), cache_v.reshape(flat))


def _pool_mix(sums, cnts, u_of, pw_ref, ps_ref, o_ref, gw):
    for gi in range(len(POOL_WINDOWS)):
        cols = slice(gi * gw, (gi + 1) * gw)
        d = sums[gi] / cnts[gi] - u_of(cols)
        o_ref[:, cols] = _wdot(d, pw_ref[gi]) * ps_ref[:, cols]


def _pool_prompt_body(u_ref, pw_ref, ps_ref, o_ref, xbuf, s2, s4, s8, *, tt, gw):
    t = pl.program_id(1)
    n = POOL_HALO + tt

    @pl.when(t == 0)
    def _():
        xbuf[0:POOL_HALO, :] = jnp.zeros((POOL_HALO, xbuf.shape[1]), F32)

    xbuf[POOL_HALO:n, :] = u_ref[...]
    s2[8:n, :] = xbuf[8:n, :] + xbuf[7:n - 1, :]
    s4[16:n, :] = s2[16:n, gw:] + s2[14:n - 2, gw:]
    s8[24:n, :] = s4[24:n, gw:] + s4[20:n - 4, gw:]
    s16 = s8[32:n, gw:] + s8[24:n - 8, gw:]
    sums = [s2[POOL_HALO:n, 0:gw], s4[POOL_HALO:n, 0:gw], s8[POOL_HALO:n, 0:gw], s16]
    pos = t * tt + lax.broadcasted_iota(jnp.int32, (tt, 1), 0)
    cnts = [jnp.minimum(pos + 1, w).astype(F32) for w in POOL_WINDOWS]
    _pool_mix(sums, cnts, lambda cols: u_ref[:, cols], pw_ref, ps_ref, o_ref, gw)
    xbuf[0:POOL_HALO, :] = xbuf[tt:n, :]


def pool_prompt(proj, pool_w, pool_scale, batch, seq, tt):
    pw_total = pool_scale.shape[-1]
    gw = pw_total // len(POOL_WINDOWS)
    steps = seq // tt
    n = POOL_HALO + tt
    body = functools.partial(_pool_prompt_body, tt=tt, gw=gw)
    return pl.pallas_call(
        body,
        out_shape=jax.ShapeDtypeStruct((batch * seq, pw_total), F32),
        grid=(batch, steps),
        in_specs=[
            pl.BlockSpec((tt, pw_total), lambda b, t: (b * steps + t, 0)),
            _resident(pool_w.shape),
            _resident((1, pw_total)),
        ],
        out_specs=pl.BlockSpec((tt, pw_total), lambda b, t: (b * steps + t, 0)),
        scratch_shapes=[
            pltpu.VMEM((n, pw_total), F32),
            pltpu.VMEM((n, pw_total), F32),
            pltpu.VMEM((n, pw_total - gw), F32),
            pltpu.VMEM((n, pw_total - 2 * gw), F32),
        ],
        compiler_params=_params("parallel", "arbitrary"),
        name="pool_prompt",
    )(proj, pool_w, pool_scale)


def _pool_sample_body(u_ref, st_ref, pw_ref, ps_ref, o_ref, sto_ref, *, gw, buf):
    pw_total = gw * len(POOL_WINDOWS)
    new = u_ref[:, 0:pw_total]
    sums, cnts = [], []
    for gi, w in enumerate(POOL_WINDOWS):
        cols = slice(gi * gw, (gi + 1) * gw)
        s = new[:, cols]
        for i in range(1, w):
            s = s + st_ref[:, buf - i, cols]
        sums.append(s)
        cnts.append(float(min(PAST_LEN + 1, w)))
    _pool_mix(sums, cnts, lambda cols: new[:, cols], pw_ref, ps_ref, o_ref, gw)
    sto_ref[:, 0:buf - 1, :] = st_ref[:, 1:buf, :]
    sto_ref[:, buf - 1, :] = new


def pool_sample(proj, state, pool_w, pool_scale):
    n_seq, buf, pw_total = state.shape
    gw = pw_total // len(POOL_WINDOWS)
    body = functools.partial(_pool_sample_body, gw=gw, buf=buf)
    return pl.pallas_call(
        body,
        out_shape=(jax.ShapeDtypeStruct((n_seq, pw_total), F32),
                   jax.ShapeDtypeStruct(state.shape, F32)),
        compiler_params=pltpu.CompilerParams(vmem_limit_bytes=VMEM_LIMIT_BYTES),
        name="pool_sample",
    )(proj, state, pool_w, pool_scale)


def _route_body(h_ref, g_ref, r_ref, idx_ref, gate_ref, *, n_experts):
    xn = _rms(h_ref[...], g_ref[...])
    logits = _wdot(xn, r_ref[...])
    lane = lax.broadcasted_iota(jnp.int32, logits.shape, 1)
    lm = jnp.where(lane < n_experts, logits, -jnp.inf)
    m1 = jnp.max(lm, axis=1, keepdims=True)
    i1 = jnp.min(jnp.where(lm == m1, lane, LANE), axis=1, keepdims=True)
    lm2 = jnp.where(lane == i1, -jnp.inf, lm)
    m2 = jnp.max(lm2, axis=1, keepdims=True)
    i2 = jnp.min(jnp.where(lm2 == m2, lane, LANE), axis=1, keepdims=True)
    e2 = jnp.exp(m2 - m1)
    den = 1.0 + e2
    idx_ref[...] = jnp.where(lane == 0, i1, jnp.where(lane == 1, i2, 0))
    gate_ref[...] = jnp.where(lane == 0, 1.0 / den, jnp.where(lane == 1, e2 / den, 0.0))


def moe_route(h_all, row_offset, rows, g, r_pad, tm, n_experts, name):
    d = h_all.shape[1]
    off = row_offset // tm
    body = functools.partial(_route_body, n_experts=n_experts)
    return pl.pallas_call(
        body,
        out_shape=(jax.ShapeDtypeStruct((rows, LANE), jnp.int32), jax.ShapeDtypeStruct((rows, LANE), F32)),
        grid=(rows // tm,),
        in_specs=[pl.BlockSpec((tm, d), lambda i: (i + off, 0)), _resident((1, d)), _resident((d, LANE))],
        out_specs=(pl.BlockSpec((tm, LANE), lambda i: (i, 0)), pl.BlockSpec((tm, LANE), lambda i: (i, 0))),
        compiler_params=_params("parallel"),
        name=name,
    )(h_all, g.reshape(1, d), r_pad)


def _dispatch_body(dest_ref, h_ref, gate_ref, zeros_hbm, o_hbm, ext, sem, *, tb, n):
    del zeros_hbm
    i = pl.program_id(0)
    slot = i % 2
    groups = tb // SUBLANE
    d = h_ref.shape[1]

    def row_copy(s, t, u, row):
        return pltpu.make_async_copy(ext.at[s, t, pl.ds(u, 1)], o_hbm.at[pl.ds(row, 1)], sem.at[s])

    def wait_slot(s):
        def body(t, carry):
            pltpu.make_async_copy(ext.at[s, t], o_hbm.at[pl.ds(0, SUBLANE)], sem.at[s]).wait()
            return carry
        lax.fori_loop(0, groups, body, 0)

    @pl.when(i >= 2)
    def _():
        wait_slot(slot)

    ext[slot, :, :, 0:d] = h_ref[...].reshape(groups, SUBLANE, d)
    ext[slot, :, :, d:d + LANE] = gate_ref[...].reshape(groups, SUBLANE, LANE)

    def body(t, carry):
        for u in range(SUBLANE):
            row_copy(slot, t, u, dest_ref[0, 0, t * SUBLANE + u]).start()
        return carry
    lax.fori_loop(0, groups, body, 0)

    @pl.when(i == n - 1)
    def _():
        wait_slot(slot)
        if n >= 2:
            wait_slot(1 - slot)


def moe_dispatch(h_all, row_offset, rows, gates, dest, into, tb, name):
    d = h_all.shape[1]
    off = row_offset // tb
    n = rows // tb
    body = functools.partial(_dispatch_body, tb=tb, n=n)
    return pl.pallas_call(
        body,
        out_shape=jax.ShapeDtypeStruct(into.shape, F32),
        grid=(n,),
        in_specs=[
            pl.BlockSpec((1, 1, tb), lambda i: (i, 0, 0), memory_space=pltpu.SMEM),
            pl.BlockSpec((tb, d), lambda i: (i + off, 0)),
            pl.BlockSpec((tb, LANE), lambda i: (i, 0)),
            pl.BlockSpec(memory_space=pl.ANY),
        ],
        out_specs=pl.BlockSpec(memory_space=pl.ANY),
        scratch_shapes=[pltpu.VMEM((2, tb // SUBLANE, SUBLANE, d + LANE), F32), pltpu.SemaphoreType.DMA((2,))],
        input_output_aliases={3: 0},
        compiler_params=_params("arbitrary"),
        name=name,
    )(dest.reshape(n, 1, tb), h_all, gates, into)


def _experts_body(te_ref, nu_ref, nb_ref, g_first_ref, g_next_ref, s_cur_ref,
                  h_hbm, g_ref, wg_ref, wu_ref, wo_ref, y_hbm,
                  xbuf, acc, xn_ref, gsem, ssem, *, tm, block, n_rows):
    i = pl.program_id(0)
    j = pl.program_id(1)
    n_tiles = pl.num_programs(0)
    nj = pl.num_programs(1)
    n_used = nu_ref[0]
    slot = i % 2
    valid = i < n_used
    chunks = tm // SUBLANE
    d = xn_ref.shape[1]

    def gather(idx_ref, s):
        def body(t, carry):
            row0 = pl.multiple_of(idx_ref[0, 0, t], SUBLANE)
            pltpu.make_async_copy(h_hbm.at[pl.ds(row0, SUBLANE), pl.ds(0, d)], xbuf.at[s, t], gsem.at[s]).start()
            return carry
        lax.fori_loop(0, chunks, body, 0)

    def wait_gather(s):
        def body(t, carry):
            pltpu.make_async_copy(h_hbm.at[pl.ds(0, SUBLANE), pl.ds(0, d)], xbuf.at[s, t], gsem.at[s]).wait()
            return carry
        lax.fori_loop(0, chunks, body, 0)

    def scatter():
        def body(t, carry):
            row0 = pl.multiple_of(s_cur_ref[0, 0, t], SUBLANE)
            pltpu.make_async_copy(acc.at[t], y_hbm.at[pl.ds(row0, SUBLANE)], ssem.at[0]).start()
            return carry
        lax.fori_loop(0, chunks, body, 0)

    def wait_scatter():
        def body(t, carry):
            pltpu.make_async_copy(acc.at[t], y_hbm.at[pl.ds(0, SUBLANE)], ssem.at[0]).wait()
            return carry
        lax.fori_loop(0, chunks, body, 0)

    @pl.when(j == 0)
    def _():
        @pl.when(i == 0)
        def _():
            gather(g_first_ref, 0)
            acc[...] = jnp.zeros_like(acc)
            first_free = nb_ref[0] * (block // SUBLANE)
            n_free = n_rows // SUBLANE - first_free

            def zero_copy(row0):
                return pltpu.make_async_copy(acc.at[0], y_hbm.at[pl.ds(pl.multiple_of(row0, SUBLANE), SUBLANE)],
                                             ssem.at[0])

            def fill(c, carry):
                zero_copy((first_free + c) * SUBLANE).start()
                zero_copy(n_rows + (first_free + c) * SUBLANE).start()
                return carry
            lax.fori_loop(0, n_free, fill, 0)

            def fill_spare(c, carry):
                zero_copy(2 * n_rows + c * SUBLANE).start()
                return carry
            lax.fori_loop(0, chunks, fill_spare, 0)

            def drain(c, carry):
                zero_copy(0).wait()
                return carry
            lax.fori_loop(0, 2 * n_free + chunks, drain, 0)

        @pl.when(i + 1 < n_used)
        def _():
            gather(g_next_ref, 1 - slot)

        @pl.when(valid)
        def _():
            wait_gather(slot)
            xn_ref[...] = _rms(xbuf[slot].reshape(tm, d), g_ref[...]).astype(BF16)

        @pl.when((i >= 1) & (i <= n_used))
        def _():
            wait_scatter()

    @pl.when(valid)
    def _():
        xn = xn_ref[...]
        a = _dot(xn, wg_ref[0].astype(BF16))
        b = _dot(xn, wu_ref[0].astype(BF16))
        part = _dot((_silu(a) * b).astype(BF16), wo_ref[0].astype(BF16)).reshape(chunks, SUBLANE, d)

        @pl.when(j == 0)
        def _():
            acc[...] = part

        @pl.when(j > 0)
        def _():
            acc[...] += part

        @pl.when(j == nj - 1)
        def _():
            scatter()

            @pl.when(i == n_tiles - 1)
            def _():
                wait_scatter()


def moe_experts(h_ps, g, wi, wo, gather_rows, scatter_rows, tile_expert, n_used, n_blocks_used, block, n_rows,
                tm, tf):
    d = wo.shape[2]
    f = wo.shape[1]
    nj = f // tf
    n_tiles = gather_rows.shape[0]
    chunks = tm // SUBLANE

    def wblock(col_off):
        def index_map(i, j, te, nu, nb):
            jj = jnp.where(i < nu[0], j, nj - 1)
            return (te[i], 0, col_off + jj)
        return index_map

    def woblock(i, j, te, nu, nb):
        return (te[i], jnp.where(i < nu[0], j, nj - 1), 0)

    def smem_tile(index_map):
        return pl.BlockSpec((1, 1, chunks), index_map, memory_space=pltpu.SMEM)

    grid_spec = pltpu.PrefetchScalarGridSpec(
        num_scalar_prefetch=3,
        grid=(n_tiles, nj),
        in_specs=[
            smem_tile(lambda i, j, te, nu, nb: (0, 0, 0)),
            smem_tile(lambda i, j, te, nu, nb: (jnp.minimum(i + 1, n_tiles - 1), 0, 0)),
            smem_tile(lambda i, j, te, nu, nb: (i, 0, 0)),
            pl.BlockSpec(memory_space=pl.ANY),
            pl.BlockSpec((1, d), lambda i, j, te, nu, nb: (0, 0)),
            pl.BlockSpec((1, d, tf), wblock(0)),
            pl.BlockSpec((1, d, tf), wblock(nj)),
            pl.BlockSpec((1, tf, d), woblock),
        ],
        out_specs=pl.BlockSpec(memory_space=pl.ANY),
        scratch_shapes=[
            pltpu.VMEM((2, chunks, SUBLANE, d), F32),
            pltpu.VMEM((chunks, SUBLANE, d), F32),
            pltpu.VMEM((tm, d), BF16),
            pltpu.SemaphoreType.DMA((2,)),
            pltpu.SemaphoreType.DMA((1,)),
        ],
    )
    return pl.pallas_call(
        functools.partial(_experts_body, tm=tm, block=block, n_rows=n_rows),
        out_shape=jax.ShapeDtypeStruct((2 * n_rows + tm, d), F32),
        grid_spec=grid_spec,
        compiler_params=_params("arbitrary", "arbitrary"),
        name="moe_experts",
    )(tile_expert, n_used, n_blocks_used, gather_rows, gather_rows, scatter_rows, h_ps, g.reshape(1, d),
      wi, wi, wo)


def _combine_body(nb_ref, hx_ref, y1_ref, y2_ref, fn_ref, o_ref, *, d, final_norm):
    @pl.when(pl.program_id(0) < nb_ref[0])
    def _():
        hx = hx_ref[...]
        out = hx[:, 0:d] + (hx[:, d:d + 1] * y1_ref[...] + hx[:, d + 1:d + 2] * y2_ref[...])
        if final_norm:
            out = _rms(out, fn_ref[...])
        o_ref[...] = out

    @pl.when(pl.program_id(0) >= nb_ref[0])
    def _():
        o_ref[...] = jnp.zeros_like(o_ref)


def moe_combine(h_ps, y, n_blocks_used, choice_stride, fn, tb, final_norm, name):
    rows, dx = h_ps.shape
    d = y.shape[1]
    off2 = choice_stride // tb

    def used(i, nb):
        return jnp.minimum(i, nb[0] - 1)

    grid_spec = pltpu.PrefetchScalarGridSpec(
        num_scalar_prefetch=1,
        grid=(rows // tb,),
        in_specs=[
            pl.BlockSpec((tb, dx), lambda i, nb: (used(i, nb), 0)),
            pl.BlockSpec((tb, d), lambda i, nb: (used(i, nb), 0)),
            pl.BlockSpec((tb, d), lambda i, nb: (used(i, nb) + off2, 0)),
            pl.BlockSpec((1, d), lambda i, nb: (0, 0)),
        ],
        out_specs=pl.BlockSpec((tb, d), lambda i, nb: (i, 0)),
    )
    return pl.pallas_call(
        functools.partial(_combine_body, d=d, final_norm=final_norm),
        out_shape=jax.ShapeDtypeStruct((rows, d), F32),
        grid_spec=grid_spec,
        compiler_params=_params("arbitrary"),
        name=name,
    )(n_blocks_used, h_ps, y, y, fn.reshape(1, d))


def _unpermute_body(cur_ref, nxt_ref, y_hbm, o_ref, buf, sem, *, tb):
    i = pl.program_id(0)
    n = pl.num_programs(0)
    slot = i % 2
    groups = tb // SUBLANE

    def start(idx_ref, s):
        def body(t, carry):
            for u in range(SUBLANE):
                row = idx_ref[0, 0, t * SUBLANE + u]
                pltpu.make_async_copy(y_hbm.at[pl.ds(row, 1)], buf.at[s, t, pl.ds(u, 1)], sem.at[s]).start()
            return carry
        lax.fori_loop(0, groups, body, 0)

    @pl.when(i == 0)
    def _():
        start(cur_ref, 0)

    @pl.when(i + 1 < n)
    def _():
        start(nxt_ref, 1 - slot)

    def wait_body(t, carry):
        pltpu.make_async_copy(y_hbm.at[pl.ds(0, SUBLANE)], buf.at[slot, t], sem.at[slot]).wait()
        return carry
    lax.fori_loop(0, groups, wait_body, 0)
    o_ref[...] = buf[slot].reshape(o_ref.shape)


def moe_unpermute(y_ps, dest, tb, name):
    rows = dest.shape[0]
    d = y_ps.shape[1]
    n = rows // tb
    idx = dest.reshape(n, 1, tb)

    def smem_tile(index_map):
        return pl.BlockSpec((1, 1, tb), index_map, memory_space=pltpu.SMEM)

    return pl.pallas_call(
        functools.partial(_unpermute_body, tb=tb),
        out_shape=jax.ShapeDtypeStruct((rows, d), F32),
        grid=(n,),
        in_specs=[smem_tile(lambda i: (i, 0, 0)), smem_tile(lambda i: (jnp.minimum(i + 1, n - 1), 0, 0)),
                  pl.BlockSpec(memory_space=pl.ANY)],
        out_specs=pl.BlockSpec((tb, d), lambda i: (i, 0)),
        scratch_shapes=[pltpu.VMEM((2, tb // SUBLANE, SUBLANE, d), F32), pltpu.SemaphoreType.DMA((2,))],
        compiler_params=_params("arbitrary"),
        name=name,
    )(idx, idx, y_ps)


def moe_plan(idx, n_experts, tm, block):
    n_tok, k = idx.shape
    assert k == 2
    n_pairs = n_experts * n_experts
    chunks_per_tile = tm // SUBLANE
    n_rows = -(-(n_tok + n_pairs * (SUBLANE - 1)) // block) * block
    n_chunks = n_rows // SUBLANE

    pair = idx[:, 0] * n_experts + idx[:, 1]
    onehot = (pair[:, None] == jnp.arange(n_pairs, dtype=jnp.int32)[None, :]).astype(jnp.int32)
    csum = jnp.cumsum(onehot, axis=0)
    counts = csum[-1]
    rank = jnp.sum(csum * onehot, axis=1) - 1
    padded = (counts + SUBLANE - 1) // SUBLANE * SUBLANE
    total = jnp.sum(padded)
    last_group = jnp.max(jnp.where(counts > 0, jnp.arange(n_pairs, dtype=jnp.int32), 0))
    padded = padded + jnp.where(jnp.arange(n_pairs) == last_group, (-total) % block, 0)
    group_end = jnp.cumsum(padded)
    group_start = group_end - padded
    dest = jnp.sum(onehot * group_start[None, :], axis=1) + rank
    rows_used = group_end[-1]

    chunk_row = jnp.arange(n_chunks, dtype=jnp.int32) * SUBLANE
    chunk_group = jnp.sum((chunk_row[:, None] >= group_end[None, :]).astype(jnp.int32), axis=1)
    chunk_used = chunk_row < rows_used
    chunk_group = jnp.minimum(chunk_group, n_pairs - 1)
    chunk_experts = jnp.stack([chunk_group // n_experts, chunk_group % n_experts], axis=1)
    experts = jnp.arange(n_experts, dtype=jnp.int32)
    member = ((chunk_experts[:, :, None] == experts[None, None, :]) & chunk_used[:, None, None])
    member = member.astype(jnp.int32)
    member_e = jnp.sum(member, axis=1)
    csum_e = jnp.cumsum(member_e, axis=0)
    counts_e = csum_e[-1]
    tiles_e = (counts_e + chunks_per_tile - 1) // chunks_per_tile
    tile_end = jnp.cumsum(tiles_e)
    start_e = (tile_end - tiles_e) * chunks_per_tile
    pos = jnp.sum(member * (start_e[None, None, :] + csum_e[:, None, :] - 1), axis=2)
    n_tiles = 2 * n_chunks // chunks_per_tile + n_experts
    n_pos = n_tiles * chunks_per_tile
    pos = jnp.where(chunk_used[:, None], pos, n_pos)
    entry = 2 * jnp.arange(n_chunks, dtype=jnp.int32)[:, None] + jnp.arange(2, dtype=jnp.int32)[None, :] + 1
    table = jnp.zeros((n_pos,), jnp.int32).at[pos.reshape(-1)].set(entry.reshape(-1), mode="drop")
    table = table.reshape(n_tiles, chunks_per_tile)
    filled = table > 0
    src_chunk = (table - 1) // 2
    choice = (table - 1) % 2
    spare = 2 * n_rows + jnp.arange(chunks_per_tile, dtype=jnp.int32) * SUBLANE
    gather_rows = jnp.where(filled, src_chunk * SUBLANE, 0)
    scatter_rows = jnp.where(filled, choice * n_rows + src_chunk * SUBLANE, spare[None, :])

    n_used = tile_end[-1]
    tile_ids = jnp.arange(n_tiles, dtype=jnp.int32)
    tile_expert = jnp.sum((tile_ids[:, None] >= tile_end[None, :]).astype(jnp.int32), axis=1)
    last_expert = jnp.sum((n_used - 1 >= tile_end).astype(jnp.int32))
    tile_expert = jnp.where(tile_ids < n_used, tile_expert, last_expert).astype(jnp.int32)
    return (dest.astype(jnp.int32),
            gather_rows.reshape(n_tiles, 1, chunks_per_tile).astype(jnp.int32),
            scatter_rows.reshape(n_tiles, 1, chunks_per_tile).astype(jnp.int32),
            tile_expert, n_used.reshape(1).astype(jnp.int32),
            (rows_used // block).reshape(1).astype(jnp.int32),
            n_rows)


def _pad_lanes(v):
    return jnp.zeros((1, LANE), F32).at[0, :v.shape[0]].set(v)


def kernel(x_prompt, x_sample, state_dn, state_dn_conv, state_pool, cache_mem_k, cache_mem_v, mem_prompt, a_norm1, a_w_in, a_conv_w, a_a_log, a_dt_bias, a_o_norm, a_w_out, a_norm2, a_ffn_wi, a_ffn_wo, b_norm1, b_w_in, b_pool_w, b_pool_scale, b_w_out, b_norm2, b_router, b_moe_wi, b_moe_wo, m_norm, m_w_k, m_w_v, final_norm):
    bp, seq, d = x_prompt.shape
    bs = x_sample.shape[0]
    assert x_sample.shape[1] == 1
    depth = m_norm.shape[0]
    assert depth % 2 == 0, "the output norm is fused into the expert block of the last (odd) layer"
    heads, dk, dv = state_dn.shape[2], state_dn.shape[3], state_dn.shape[4]
    conv_ch = state_dn_conv.shape[-1]
    qk_w, v_w = heads * dk, heads * dv
    assert conv_ch == 2 * qk_w + v_w and dk == LANE and dv == LANE and heads <= LANE
    n_mem, x_heads, x_dh = cache_mem_k.shape[2], cache_mem_k.shape[3], cache_mem_k.shape[4]
    xw = x_heads * x_dh
    pool_w_total = state_pool.shape[-1]
    pool_buf = state_pool.shape[2]
    d_ff = a_ffn_wo.shape[1]

    hp = x_prompt.reshape(bp * seq, d)
    hs = x_sample.reshape(bs, d)

    o1, o2, o3, o4 = conv_ch, conv_ch + v_w, conv_ch + v_w + heads, conv_ch + v_w + 2 * heads
    col_a = {"qkv": 0, "z": conv_ch, "cq": conv_ch + v_w, "b": conv_ch + v_w + xw, "a": conv_ch + v_w + xw + LANE}
    n_a = -(-(col_a["a"] + LANE) // 512) * 512

    dn_p, conv_p, pool_p, mk_all, mv_all = [], [], [], [], []
    dn_s, conv_s, pool_s = [], [], []
    for i in range(depth):
        j = i // 2
        w_kv = jnp.concatenate([m_w_k[i], m_w_v[i]], axis=1).astype(BF16)
        kv = rms_matmul(mem_prompt.reshape(bp * n_mem, d), m_norm[i], w_kv, 256, f"mem_kv_{i}")
        mk_all.append(kv[:, :xw].reshape(bp, n_mem, x_heads, x_dh))
        mv_all.append(kv[:, xw:].reshape(bp, n_mem, x_heads, x_dh))
        if i % 2 == 0:
            w = a_w_in[j]
            w_in = jnp.zeros((d, n_a), F32)
            w_in = w_in.at[:, :o2].set(w[:, :o2])
            w_in = w_in.at[:, col_a["cq"]:col_a["cq"] + xw].set(w[:, o4:])
            w_in = w_in.at[:, col_a["b"]:col_a["b"] + heads].set(w[:, o2:o3])
            w_in = w_in.at[:, col_a["a"]:col_a["a"] + heads].set(w[:, o3:o4])
            a_log, dt_b = _pad_lanes(a_a_log[j]), _pad_lanes(a_dt_bias[j])
            o_norm = a_o_norm[j].reshape(1, dv)

            proj_p = rms_matmul(hp, a_norm1[j], w_in.astype(BF16), 256, f"a_in_p_{i}")
            proj_s = rms_matmul(hs, a_norm1[j], w_in, bs, f"a_in_s_{i}", tn=512)
            mix_p, s_p = deltanet_prompt(proj_p, col_a, a_conv_w[j], a_log, dt_b, o_norm,
                                         bp, seq, heads, dk, dv, rows=DN_STEP_CHUNKS * CHUNK)
            mix_s, s_s, c_s = deltanet_sample(proj_s, col_a, state_dn_conv[j], state_dn[j], a_conv_w[j],
                                              a_log, dt_b, o_norm, heads, dk, dv, nb=SUBLANE)
            xa_p = xattn_prompt(proj_p, col_a["cq"], kv, bp, seq, n_mem, x_heads, x_dh, tq=512)
            xa_s = xattn_sample(proj_s, col_a["cq"], cache_mem_k, cache_mem_v, i, nb=SUBLANE)
            hp = out_proj(hp, mix_p, xa_p, a_w_out[j].astype(BF16), 512, f"a_out_p_{i}")
            hs = out_proj(hs, mix_s, xa_s, a_w_out[j], bs, f"a_out_s_{i}")
            hp = ffn_dense(hp, a_norm2[j], a_ffn_wi[j].astype(BF16), a_ffn_wo[j].astype(BF16), 256,
                           f"a_ffn_p_{i}")
            hs = ffn_dense(hs, a_norm2[j], a_ffn_wi[j], a_ffn_wo[j], bs, f"a_ffn_s_{i}", tf=256)
            dn_p.append(s_p)
            conv_p.append(proj_p.reshape(bp, seq, n_a)[:, seq - (CONV_WIDTH - 1):, :conv_ch])
            dn_s.append(s_s)
            conv_s.append(c_s)
        else:
            ps = b_pool_scale[j].reshape(1, pool_w_total)
            last = i == depth - 1

            proj_p = rms_matmul(hp, b_norm1[j], b_w_in[j].astype(BF16), 512, f"b_in_p_{i}")
            proj_s = rms_matmul(hs, b_norm1[j], b_w_in[j], bs, f"b_in_s_{i}")
            mix_p = pool_prompt(proj_p, b_pool_w[j].astype(BF16), ps, bp, seq, tt=512)
            mix_s, buf_s = pool_sample(proj_s, state_pool[j], b_pool_w[j], ps)
            xa_p = xattn_prompt(proj_p, pool_w_total, kv, bp, seq, n_mem, x_heads, x_dh, tq=512)
            xa_s = xattn_sample(proj_s, pool_w_total, cache_mem_k, cache_mem_v, i, nb=SUBLANE)
            n_p = bp * seq
            h_all = out_proj(hp, mix_p, xa_p, b_w_out[j].astype(BF16), 512, f"b_out_p_{i}", out_rows=n_p + bs)
            h_all = out_proj(hs, mix_s, xa_s, b_w_out[j], bs, f"b_out_s_{i}", out_rows=n_p + bs,
                             into=h_all, row_offset=n_p)
            n_experts = b_router.shape[-1]
            r_pad = jnp.zeros((d, LANE), F32).at[:, :n_experts].set(b_router[j])
            idx_p, gate_p = moe_route(h_all, 0, n_p, b_norm2[j], r_pad.astype(BF16), 512, n_experts,
                                      f"b_route_p_{i}")
            idx_s, gate_s = moe_route(h_all, n_p, bs, b_norm2[j], r_pad, bs, n_experts, f"b_route_s_{i}")
            idx = jnp.concatenate([idx_p[:, :TOP_K], idx_s[:, :TOP_K]], axis=0)
            (dest, gather_rows, scatter_rows, tile_expert, n_used, n_blocks_used, n_rows) = moe_plan(
                idx, n_experts, MOE_TILE, MOE_ROW_BLOCK)
            h_ps = jnp.zeros((n_rows, d + LANE), F32)
            h_ps = moe_dispatch(h_all, 0, n_p, gate_p, dest[:n_p], h_ps, 512, f"b_dispatch_p_{i}")
            h_ps = moe_dispatch(h_all, n_p, bs, gate_s, dest[n_p:], h_ps, bs, f"b_dispatch_s_{i}")
            y = moe_experts(h_ps, b_norm2[j], b_moe_wi[j], b_moe_wo[j], gather_rows, scatter_rows, tile_expert,
                            n_used, n_blocks_used, MOE_ROW_BLOCK, n_rows, MOE_TILE, MOE_FF_TILE)
            y_ps = moe_combine(h_ps, y, n_blocks_used, n_rows, final_norm, MOE_ROW_BLOCK, last,
                               f"b_combine_{i}")
            hp = moe_unpermute(y_ps, dest[:n_p], 512, f"b_unpermute_p_{i}")
            hs = moe_unpermute(y_ps, dest[n_p:], bs, f"b_unpermute_s_{i}")
            pool_p.append(proj_p.reshape(bp, seq, -1)[:, seq - pool_buf:, :pool_w_total])
            pool_s.append(buf_s)

    y_prompt = hp.reshape(bp, seq, d)
    y_sample = hs.reshape(bs, 1, d)
    return (y_prompt, y_sample, jnp.stack(dn_p), jnp.stack(conv_p), jnp.stack(pool_p),
            jnp.stack(mk_all), jnp.stack(mv_all), jnp.stack(dn_s), jnp.stack(conv_s), jnp.stack(pool_s))
```

```python
import functools

import jax
import jax.numpy as jnp
from jax import lax
from jax.experimental import pallas as pl
from jax.experimental.pallas import tpu as pltpu

F32 = jnp.float32
BF16 = jnp.bfloat16
HIGHEST = lax.Precision.HIGHEST

EPS = 1e-6
CHUNK = 64
CONV_WIDTH = 4
POOL_WINDOWS = (2, 4, 8, 16)
PAST_LEN = 16384
TOP_K = 2

LANE = 128
SUBLANE = 8
VMEM_LIMIT_BYTES = 56 * 2**20

MOE_TILE = 1024
MOE_FF_TILE = 512
MOE_ROW_BLOCK = 512
DN_STEP_CHUNKS = 4
INV_BLOCK = 16
POOL_HALO = 32


def _params(*sem):
    return pltpu.CompilerParams(dimension_semantics=sem, vmem_limit_bytes=VMEM_LIMIT_BYTES)


def _resident(shape):
    nd = len(shape)
    return pl.BlockSpec(shape, lambda *_: (0,) * nd, pipeline_mode=pl.Buffered(1))


def _rms(x, w):
    return x * lax.rsqrt(jnp.mean(x * x, axis=-1, keepdims=True) + EPS) * w


def _silu(x):
    return x * jax.nn.sigmoid(x)


def _dot(a, b):
    return jnp.dot(a, b, preferred_element_type=F32)


def _wdot(a, w):
    if w.dtype == F32:
        return jnp.dot(a, w, precision=HIGHEST, preferred_element_type=F32)
    return jnp.dot(a.astype(BF16), w, preferred_element_type=F32)


def _bdot(a, b):
    return lax.dot_general(a, b, (((2,), (1,)), ((0,), (0,))), preferred_element_type=F32)


def _bdot_nt(a, b):
    return lax.dot_general(a, b, (((2,), (2,)), ((0,), (0,))), preferred_element_type=F32)


def _bdot_tn(a, b):
    return lax.dot_general(a, b, (((1,), (1,)), ((0,), (0,))), preferred_element_type=F32)


def _rms_matmul_body(x_ref, g_ref, w_ref, o_ref):
    xn = _rms(x_ref[...], g_ref[...])
    o_ref[...] = _wdot(xn, w_ref[...])


def _wspec(shape, index_map, steps):
    if steps == 1:
        return pl.BlockSpec(shape, index_map, pipeline_mode=pl.Buffered(1))
    return pl.BlockSpec(shape, index_map)


def rms_matmul(x, g, w, tm, name, tn=None):
    m, d = x.shape
    n = w.shape[1]
    tn = n if tn is None else tn
    nj = n // tn
    return pl.pallas_call(
        _rms_matmul_body,
        out_shape=jax.ShapeDtypeStruct((m, n), F32),
        grid=(m // tm, nj),
        in_specs=[pl.BlockSpec((tm, d), lambda i, j: (i, 0)), _resident((1, d)),
                  _wspec((d, tn), lambda i, j: (0, j), nj)],
        out_specs=pl.BlockSpec((tm, tn), lambda i, j: (i, j)),
        compiler_params=_params("parallel", "arbitrary"),
        name=name,
    )(x, g.reshape(1, d), w)


def _out_proj_body(x_ref, a1_ref, a2_ref, w1_ref, w2_ref, *rest, n_main):
    o_ref = rest[-1]

    @pl.when(pl.program_id(0) < n_main)
    def _():
        acc = _wdot(a1_ref[...], w1_ref[...]) + _wdot(a2_ref[...], w2_ref[...])
        o_ref[...] = x_ref[...] + acc

    @pl.when(pl.program_id(0) >= n_main)
    def _():
        o_ref[...] = jnp.zeros_like(o_ref)


def out_proj(x, a1, a2, w, tm, name, out_rows=None, into=None, row_offset=0):
    m, d = x.shape
    k1, k2 = a1.shape[1], a2.shape[1]
    assert w.shape[0] == k1 + k2 and k1 % k2 == 0 and row_offset % tm == 0
    out_rows = m if out_rows is None else out_rows
    off = row_offset // tm
    n_main = m // tm
    n_extra = 0 if (into is not None or out_rows == m) else 1
    assert out_rows - m <= n_extra * tm or into is not None
    last = n_main - 1
    in_specs = [
        pl.BlockSpec((tm, d), lambda i: (jnp.minimum(i, last), 0)),
        pl.BlockSpec((tm, k1), lambda i: (jnp.minimum(i, last), 0)),
        pl.BlockSpec((tm, k2), lambda i: (jnp.minimum(i, last), 0)),
        pl.BlockSpec((k1, d), lambda i: (0, 0), pipeline_mode=pl.Buffered(1)),
        pl.BlockSpec((k2, d), lambda i: (k1 // k2, 0), pipeline_mode=pl.Buffered(1)),
    ]
    args = [x, a1, a2, w, w]
    aliases = {}
    if into is not None:
        in_specs.append(pl.BlockSpec(memory_space=pl.ANY))
        args.append(into)
        aliases = {len(args) - 1: 0}
    return pl.pallas_call(
        functools.partial(_out_proj_body, n_main=n_main),
        out_shape=jax.ShapeDtypeStruct((out_rows, d), F32),
        grid=(n_main + n_extra,),
        in_specs=in_specs,
        out_specs=pl.BlockSpec((tm, d), lambda i: (i + off, 0)),
        input_output_aliases=aliases,
        compiler_params=_params("parallel"),
        name=name,
    )(*args)


def _ffn_body(h_ref, g_ref, wg_ref, wu_ref, wo_ref, o_ref, xn_ref):
    @pl.when(pl.program_id(1) == 0)
    def _():
        h = h_ref[...]
        xn_ref[...] = _rms(h, g_ref[...])
        o_ref[...] = h

    xn = xn_ref[...]
    a = _wdot(xn, wg_ref[...])
    b = _wdot(xn, wu_ref[...])
    o_ref[...] += _wdot(_silu(a) * b, wo_ref[...])


def ffn_dense(h, g, wi, wo, tm, name, tf=None):
    m, d = h.shape
    f = wo.shape[0]
    tf = f if tf is None else tf
    nj = f // tf
    return pl.pallas_call(
        _ffn_body,
        out_shape=jax.ShapeDtypeStruct((m, d), F32),
        grid=(m // tm, nj),
        in_specs=[
            pl.BlockSpec((tm, d), lambda i, j: (i, 0)),
            _resident((1, d)),
            _wspec((d, tf), lambda i, j: (0, j), nj),
            _wspec((d, tf), lambda i, j: (0, nj + j), nj),
            _wspec((tf, d), lambda i, j: (j, 0), nj),
        ],
        out_specs=pl.BlockSpec((tm, d), lambda i, j: (i, 0)),
        scratch_shapes=[pltpu.VMEM((tm, d), F32)],
        compiler_params=_params("parallel", "arbitrary"),
        name=name,
    )(h, g.reshape(1, d), wi, wi, wo)


def _pair_blockdiag(y, left):
    yb = y.astype(BF16)
    zero = jnp.zeros_like(yb)
    return jnp.concatenate([jnp.where(left, yb, zero), jnp.where(left, zero, yb)], axis=1)


def _pair_mm(x, y, left):
    return _bdot(x.astype(BF16), _pair_blockdiag(y, left))


def _unit_lower_inverse(mx, eye, blk, left):
    mm = functools.partial(_pair_mm, left=left)
    md = jnp.where(blk, mx, 0.0)
    c = mx - md
    p = eye - md
    m2 = mm(md, md)
    p = p + mm(p, m2)
    m4 = mm(m2, m2)
    p = p + mm(p, m4)
    m8 = mm(m4, m4)
    td = p + mm(p, m8)
    n = mm(td, c)
    n2 = mm(n, n)
    q = eye - n
    q = q + mm(q, n2)
    return mm(q, td)


def _dn_prompt_body(qkv_ref, z_ref, bl_ref, al_ref, cw_ref, alog_ref, dtb_ref, on_ref,
                    dn_ref, sfin_ref, s_ref, xbuf_ref, act_ref, *, rows, heads, dk, dv):
    c = pl.program_id(1)
    wq = heads * dk

    @pl.when(c == 0)
    def _():
        s_ref[...] = jnp.zeros_like(s_ref)
        xbuf_ref[0:SUBLANE, :] = jnp.zeros((SUBLANE, xbuf_ref.shape[1]), F32)

    xbuf_ref[SUBLANE:SUBLANE + rows, :] = qkv_ref[...]
    first = SUBLANE - (CONV_WIDTH - 1)
    y = xbuf_ref[pl.ds(first, rows), :] * cw_ref[0:1, :]
    for j in range(1, CONV_WIDTH):
        y = y + xbuf_ref[pl.ds(first + j, rows), :] * cw_ref[j:j + 1, :]
    act_ref[...] = _silu(y)
    xbuf_ref[0:SUBLANE, :] = xbuf_ref[rows:rows + SUBLANE, :]

    pairs = heads // 2
    ri = lax.broadcasted_iota(jnp.int32, (CHUNK, 2 * CHUNK), 0)
    lane = lax.broadcasted_iota(jnp.int32, (CHUNK, 2 * CHUNK), 1)
    cj = lane % CHUNK
    left = (lane < CHUNK)[None]
    incl = (ri >= cj)[None]
    strict = (ri > cj)[None]
    blk = ((ri // INV_BLOCK) == (cj // INV_BLOCK))[None]
    eye = (ri == cj).astype(F32)[None]
    tr = lax.broadcasted_iota(jnp.int32, (CHUNK, CHUNK), 0)
    tc = lax.broadcasted_iota(jnp.int32, (CHUNK, CHUNK), 1)
    tri = (tr >= tc).astype(F32)

    n_chunks = rows // CHUNK
    zero = jnp.zeros((CHUNK, dk), BF16)

    def chunk_rows(cc):
        return slice(cc * CHUNK, (cc + 1) * CHUNK)

    def per_head(fn):
        return jnp.stack([fn(chunk_rows(cc), h) for cc in range(n_chunks) for h in range(heads)], axis=0)

    def per_pair(fn):
        return jnp.stack([fn(cc, p) for cc in range(n_chunks) for p in range(pairs)], axis=0)

    q3 = per_head(lambda rs, h: act_ref[rs, h * dk:(h + 1) * dk])
    k3 = per_head(lambda rs, h: act_ref[rs, wq + h * dk:wq + (h + 1) * dk])
    v3 = per_head(lambda rs, h: act_ref[rs, 2 * wq + h * dv:2 * wq + (h + 1) * dv])
    qn = q3 * lax.rsqrt(jnp.sum(q3 * q3, axis=-1, keepdims=True) + EPS) * (dk ** -0.5)
    kn = k3 * lax.rsqrt(jnp.sum(k3 * k3, axis=-1, keepdims=True) + EPS)

    beta = jax.nn.sigmoid(bl_ref[...])
    g = -jnp.exp(alog_ref[...]) * jax.nn.softplus(al_ref[...] + dtb_ref[...])
    gcum = [jnp.dot(tri, g[chunk_rows(cc)], precision=HIGHEST, preferred_element_type=F32)
            for cc in range(n_chunks)]
    gcum_t = [jnp.concatenate([gc_, pltpu.roll(gc_, LANE - 1, 1)], axis=0).T for gc_ in gcum]

    def pair_cols(x_of):
        return per_pair(lambda cc, p: jnp.where(left[0], x_of(cc)[:, 2 * p:2 * p + 1],
                                                x_of(cc)[:, 2 * p + 1:2 * p + 2]))

    gc2 = pair_cols(lambda cc: gcum[cc])
    bc2 = pair_cols(lambda cc: beta[chunk_rows(cc)])
    gr2 = per_pair(lambda cc, p: gcum_t[cc][2 * p:2 * p + 1, :])
    decay = jnp.where(incl, jnp.exp(gc2 - gr2), 0.0)

    kb = kn.astype(BF16)
    qb = qn.astype(BF16)

    def pair_of(x, cc, p):
        return x[cc * heads + 2 * p], x[cc * heads + 2 * p + 1]

    k_lhs = per_pair(lambda cc, p: jnp.concatenate(pair_of(kb, cc, p), axis=1))
    q_lhs = per_pair(lambda cc, p: jnp.concatenate(pair_of(qb, cc, p), axis=1))
    k_bd = per_pair(lambda cc, p: jnp.concatenate(
        [jnp.concatenate([pair_of(kb, cc, p)[0], zero], axis=1),
         jnp.concatenate([zero, pair_of(kb, cc, p)[1]], axis=1)], axis=0))
    kk = _bdot_nt(k_lhs, k_bd)
    mx = jnp.where(strict, bc2 * kk * decay, 0.0)
    tinv = _unit_lower_inverse(mx, eye, blk, left)

    gcum_rows = jnp.concatenate(gcum, axis=0)
    gc = per_head(lambda rs, h: gcum_rows[rs, h:h + 1])
    bc = per_head(lambda rs, h: beta[rs, h:h + 1])
    eg = jnp.exp(gc)
    rhs = jnp.concatenate([v3 * bc, kn * (bc * eg)], axis=-1)
    sol = _bdot(_pair_blockdiag(tinv, left), rhs.astype(BF16).reshape(n_chunks * pairs, 2 * CHUNK, dv + dk))
    sol = sol.reshape(n_chunks * heads, CHUNK, dv + dk)
    w_val = sol[..., :dv]
    k_cd = sol[..., dv:].astype(BF16)
    qk_bd = _pair_blockdiag(_bdot_nt(q_lhs, k_bd) * decay, left)
    q_dec = (qn * eg).astype(BF16)
    g_last = gc[:, CHUNK - 1:CHUNK, :]
    k_tail = (kn * jnp.exp(g_last - gc)).astype(BF16)
    c_decay = jnp.exp(g_last)

    for cc in range(n_chunks):
        hs = slice(cc * heads, (cc + 1) * heads)
        ps = slice(cc * pairs, (cc + 1) * pairs)
        rs = chunk_rows(cc)
        s = s_ref[...]
        sb = s.astype(BF16)
        u = w_val[hs] - _bdot(k_cd[hs], sb)
        ub = u.astype(BF16)
        intra = _bdot(qk_bd[ps], ub.reshape(pairs, 2 * CHUNK, dv))
        o = _bdot(q_dec[hs], sb) + intra.reshape(heads, CHUNK, dv)
        s_ref[...] = s * c_decay[hs] + _bdot_tn(k_tail[hs], ub)

        o = o * lax.rsqrt(jnp.mean(o * o, axis=-1, keepdims=True) + EPS) * on_ref[...]
        for h in range(heads):
            z = z_ref[rs, h * dv:(h + 1) * dv]
            dn_ref[rs, h * dv:(h + 1) * dv] = o[h] * _silu(z)

    @pl.when(c == pl.num_programs(1) - 1)
    def _():
        sfin_ref[0] = s_ref[...]


def deltanet_prompt(proj, col, conv_w, a_log, dt_bias, o_norm, batch, seq, heads, dk, dv, rows):
    conv_ch = 2 * heads * dk + heads * dv
    zw = heads * dv
    steps = seq // rows

    def row_map(width_off):
        return lambda b, c: (b * steps + c, width_off)

    body = functools.partial(_dn_prompt_body, rows=rows, heads=heads, dk=dk, dv=dv)
    return pl.pallas_call(
        body,
        out_shape=(jax.ShapeDtypeStruct((batch * seq, zw), F32),
                   jax.ShapeDtypeStruct((batch, heads, dk, dv), F32)),
        grid=(batch, steps),
        in_specs=[
            pl.BlockSpec((rows, conv_ch), row_map(col["qkv"] // conv_ch)),
            pl.BlockSpec((rows, zw), row_map(col["z"] // zw)),
            pl.BlockSpec((rows, LANE), row_map(col["b"] // LANE)),
            pl.BlockSpec((rows, LANE), row_map(col["a"] // LANE)),
            _resident((CONV_WIDTH, conv_ch)),
            _resident((1, LANE)),
            _resident((1, LANE)),
            _resident((1, dv)),
        ],
        out_specs=(pl.BlockSpec((rows, zw), lambda b, c: (b * steps + c, 0)),
                   pl.BlockSpec((1, heads, dk, dv), lambda b, c: (b, 0, 0, 0))),
        scratch_shapes=[
            pltpu.VMEM((heads, dk, dv), F32),
            pltpu.VMEM((rows + SUBLANE, conv_ch), F32),
            pltpu.VMEM((rows, conv_ch), F32),
        ],
        compiler_params=_params("parallel", "arbitrary"),
        name="deltanet_prompt",
    )(proj, proj, proj, proj, conv_w, a_log, dt_bias, o_norm)


def _dn_sample_body(qkv_ref, z_ref, bl_ref, al_ref, cbuf_ref, st_ref, cw_ref, alog_ref, dtb_ref, on_ref,
                    dn_ref, sto_ref, cbo_ref, act_ref, *, nb, heads, dk, dv):
    wq = heads * dk
    new = qkv_ref[...]
    y = cbuf_ref[:, 0, :] * cw_ref[0:1, :]
    for j in range(1, CONV_WIDTH - 1):
        y = y + cbuf_ref[:, j, :] * cw_ref[j:j + 1, :]
    y = y + new * cw_ref[CONV_WIDTH - 1:CONV_WIDTH, :]
    act_ref[...] = _silu(y)
    for j in range(CONV_WIDTH - 2):
        cbo_ref[:, j, :] = cbuf_ref[:, j + 1, :]
    cbo_ref[:, CONV_WIDTH - 2, :] = new

    beta = jax.nn.sigmoid(bl_ref[...])
    g = -jnp.exp(alog_ref[...]) * jax.nn.softplus(al_ref[...] + dtb_ref[...])
    dec = jnp.exp(g)

    q_t, k_t = [], []
    for h in range(heads):
        q = act_ref[:, h * dk:(h + 1) * dk]
        k = act_ref[:, wq + h * dk:wq + (h + 1) * dk]
        qn = q * lax.rsqrt(jnp.sum(q * q, axis=-1, keepdims=True) + EPS) * (dk ** -0.5)
        kn = k * lax.rsqrt(jnp.sum(k * k, axis=-1, keepdims=True) + EPS)
        q_t.append(qn.T)
        k_t.append(kn.T)

    def per_head(fn):
        return jnp.stack([fn(h) for h in range(heads)], axis=0)

    for i in range(nb):
        row = slice(i, i + 1)
        kcol = per_head(lambda h: k_t[h][:, i:i + 1])
        qcol = per_head(lambda h: q_t[h][:, i:i + 1])
        v = per_head(lambda h: act_ref[row, 2 * wq + h * dv:2 * wq + (h + 1) * dv])
        z = per_head(lambda h: z_ref[row, h * dv:(h + 1) * dv])
        b_i = per_head(lambda h: beta[row, h:h + 1])
        d_i = per_head(lambda h: dec[row, h:h + 1])
        s = st_ref[i] * d_i
        u = b_i * (v - jnp.sum(kcol * s, axis=1, keepdims=True))
        s = s + kcol * u
        o = jnp.sum(qcol * s, axis=1, keepdims=True)
        sto_ref[i] = s
        o = o * lax.rsqrt(jnp.mean(o * o, axis=-1, keepdims=True) + EPS) * on_ref[...]
        o = o * _silu(z)
        for h in range(heads):
            dn_ref[row, h * dv:(h + 1) * dv] = o[h]


def deltanet_sample(proj, col, conv_buf, state, conv_w, a_log, dt_bias, o_norm, heads, dk, dv, nb):
    n_seq = proj.shape[0]
    conv_ch = 2 * heads * dk + heads * dv
    zw = heads * dv
    body = functools.partial(_dn_sample_body, nb=nb, heads=heads, dk=dk, dv=dv)
    return pl.pallas_call(
        body,
        out_shape=(jax.ShapeDtypeStruct((n_seq, zw), F32),
                   jax.ShapeDtypeStruct(state.shape, F32),
                   jax.ShapeDtypeStruct(conv_buf.shape, F32)),
        grid=(n_seq // nb,),
        in_specs=[
            pl.BlockSpec((nb, conv_ch), lambda i: (i, col["qkv"] // conv_ch)),
            pl.BlockSpec((nb, zw), lambda i: (i, col["z"] // zw)),
            pl.BlockSpec((nb, LANE), lambda i: (i, col["b"] // LANE)),
            pl.BlockSpec((nb, LANE), lambda i: (i, col["a"] // LANE)),
            pl.BlockSpec((nb, CONV_WIDTH - 1, conv_ch), lambda i: (i, 0, 0)),
            pl.BlockSpec((nb, heads, dk, dv), lambda i: (i, 0, 0, 0)),
            _resident((CONV_WIDTH, conv_ch)),
            _resident((1, LANE)),
            _resident((1, LANE)),
            _resident((1, dv)),
        ],
        out_specs=(pl.BlockSpec((nb, zw), lambda i: (i, 0)),
                   pl.BlockSpec((nb, heads, dk, dv), lambda i: (i, 0, 0, 0)),
                   pl.BlockSpec((nb, CONV_WIDTH - 1, conv_ch), lambda i: (i, 0, 0))),
        scratch_shapes=[
            pltpu.VMEM((nb, conv_ch), F32),
        ],
        compiler_params=_params("parallel"),
        name="deltanet_sample",
    )(proj, proj, proj, proj, conv_buf, state, conv_w, a_log, dt_bias, o_norm)


def _xattn_prompt_body(q_ref, k_ref, v_ref, o_ref, *, heads, dh):
    for h in range(heads):
        sl = slice(h * dh, (h + 1) * dh)
        q = q_ref[:, sl].astype(BF16)
        k = k_ref[:, sl].astype(BF16)
        v = v_ref[:, sl].astype(BF16)
        s = lax.dot_general(q, k, (((1,), (1,)), ((), ())), preferred_element_type=F32) * (dh ** -0.5)
        e = jnp.exp(s - jnp.max(s, axis=-1, keepdims=True))
        p = e / jnp.sum(e, axis=-1, keepdims=True)
        o_ref[:, sl] = _dot(p.astype(BF16), v)


def xattn_prompt(proj, cq_col, kv, batch, seq, n_mem, heads, dh, tq):
    xw = heads * dh
    steps = seq // tq
    body = functools.partial(_xattn_prompt_body, heads=heads, dh=dh)
    return pl.pallas_call(
        body,
        out_shape=jax.ShapeDtypeStruct((batch * seq, xw), F32),
        grid=(batch, steps),
        in_specs=[
            pl.BlockSpec((tq, xw), lambda b, i: (b * steps + i, cq_col // xw)),
            pl.BlockSpec((n_mem, xw), lambda b, i: (b, 0)),
            pl.BlockSpec((n_mem, xw), lambda b, i: (b, 1)),
        ],
        out_specs=pl.BlockSpec((tq, xw), lambda b, i: (b * steps + i, 0)),
        compiler_params=_params("parallel", "parallel"),
        name="xattn_prompt",
    )(proj, kv, kv)


def _xattn_sample_body(q_ref, k_ref, v_ref, o_ref, *, nb, heads, dh):
    rep = SUBLANE // heads
    n_t = k_ref.shape[2] // SUBLANE

    def over_groups(x, op):
        out = x
        for t in range(1, rep):
            out = op(out, pltpu.roll(x, t * heads, 0))
        return out

    for i in range(nb):
        row = slice(i, i + 1)
        q8 = jnp.concatenate([q_ref[row, h * dh:(h + 1) * dh] for h in range(heads)] * rep, axis=0)
        k3 = k_ref[0, i].reshape(n_t, SUBLANE, dh)
        v3 = v_ref[0, i].reshape(n_t, SUBLANE, dh)
        s = jnp.sum(k3 * q8[None], axis=-1, keepdims=True) * (dh ** -0.5)
        m = over_groups(jnp.max(s, axis=0), jnp.maximum)
        e = jnp.exp(s - m[None])
        den = over_groups(jnp.sum(e, axis=0), jnp.add)
        p = e / den[None]
        acc = over_groups(jnp.sum(p * v3, axis=0), jnp.add)
        for h in range(heads):
            o_ref[row, h * dh:(h + 1) * dh] = acc[h:h + 1, :]


def xattn_sample(proj, cq_col, cache_k, cache_v, layer, nb):
    n_layers, n_seq, n_mem, heads, dh = cache_k.shape
    assert SUBLANE % heads == 0 and (n_mem * heads) % SUBLANE == 0
    xw = heads * dh
    body = functools.partial(_xattn_sample_body, nb=nb, heads=heads, dh=dh)
    cache_spec = pl.BlockSpec((1, nb, n_mem * heads, dh), lambda i: (layer, i, 0, 0))
    flat = (n_layers, n_seq, n_mem * heads, dh)
    return pl.pallas_call(
        body,
        out_shape=jax.ShapeDtypeStruct((n_seq, xw), F32),
        grid=(n_seq // nb,),
        in_specs=[pl.BlockSpec((nb, xw), lambda i: (i, cq_col // xw)), cache_spec, cache_spec],
        out_specs=pl.BlockSpec((nb, xw), lambda i: (i, 0)),
        compiler_params=_params("parallel"),
        name="xattn_sample",
    )(proj, cache_k.reshape(flat), cache_v.reshape(flat))


def _pool_mix(sums, cnts, u_of, pw_ref, ps_ref, o_ref, gw):
    for gi in range(len(POOL_WINDOWS)):
        cols = slice(gi * gw, (gi + 1) * gw)
        d = sums[gi] / cnts[gi] - u_of(cols)
        o_ref[:, cols] = _wdot(d, pw_ref[gi]) * ps_ref[:, cols]


def _pool_prompt_body(u_ref, pw_ref, ps_ref, o_ref, xbuf, s2, s4, s8, *, tt, gw):
    t = pl.program_id(1)
    n = POOL_HALO + tt

    @pl.when(t == 0)
    def _():
        xbuf[0:POOL_HALO, :] = jnp.zeros((POOL_HALO, xbuf.shape[1]), F32)

    xbuf[POOL_HALO:n, :] = u_ref[...]
    s2[8:n, :] = xbuf[8:n, :] + xbuf[7:n - 1, :]
    s4[16:n, :] = s2[16:n, gw:] + s2[14:n - 2, gw:]
    s8[24:n, :] = s4[24:n, gw:] + s4[20:n - 4, gw:]
    s16 = s8[32:n, gw:] + s8[24:n - 8, gw:]
    sums = [s2[POOL_HALO:n, 0:gw], s4[POOL_HALO:n, 0:gw], s8[POOL_HALO:n, 0:gw], s16]
    pos = t * tt + lax.broadcasted_iota(jnp.int32, (tt, 1), 0)
    cnts = [jnp.minimum(pos + 1, w).astype(F32) for w in POOL_WINDOWS]
    _pool_mix(sums, cnts, lambda cols: u_ref[:, cols], pw_ref, ps_ref, o_ref, gw)
    xbuf[0:POOL_HALO, :] = xbuf[tt:n, :]


def pool_prompt(proj, pool_w, pool_scale, batch, seq, tt):
    pw_total = pool_scale.shape[-1]
    gw = pw_total // len(POOL_WINDOWS)
    steps = seq // tt
    n = POOL_HALO + tt
    body = functools.partial(_pool_prompt_body, tt=tt, gw=gw)
    return pl.pallas_call(
        body,
        out_shape=jax.ShapeDtypeStruct((batch * seq, pw_total), F32),
        grid=(batch, steps),
        in_specs=[
            pl.BlockSpec((tt, pw_total), lambda b, t: (b * steps + t, 0)),
            _resident(pool_w.shape),
            _resident((1, pw_total)),
        ],
        out_specs=pl.BlockSpec((tt, pw_total), lambda b, t: (b * steps + t, 0)),
        scratch_shapes=[
            pltpu.VMEM((n, pw_total), F32),
            pltpu.VMEM((n, pw_total), F32),
            pltpu.VMEM((n, pw_total - gw), F32),
            pltpu.VMEM((n, pw_total - 2 * gw), F32),
        ],
        compiler_params=_params("parallel", "arbitrary"),
        name="pool_prompt",
    )(proj, pool_w, pool_scale)


def _pool_sample_body(u_ref, st_ref, pw_ref, ps_ref, o_ref, sto_ref, *, gw, buf):
    pw_total = gw * len(POOL_WINDOWS)
    new = u_ref[:, 0:pw_total]
    sums, cnts = [], []
    for gi, w in enumerate(POOL_WINDOWS):
        cols = slice(gi * gw, (gi + 1) * gw)
        s = new[:, cols]
        for i in range(1, w):
            s = s + st_ref[:, buf - i, cols]
        sums.append(s)
        cnts.append(float(min(PAST_LEN + 1, w)))
    _pool_mix(sums, cnts, lambda cols: new[:, cols], pw_ref, ps_ref, o_ref, gw)
    sto_ref[:, 0:buf - 1, :] = st_ref[:, 1:buf, :]
    sto_ref[:, buf - 1, :] = new


def pool_sample(proj, state, pool_w, pool_scale):
    n_seq, buf, pw_total = state.shape
    gw = pw_total // len(POOL_WINDOWS)
    body = functools.partial(_pool_sample_body, gw=gw, buf=buf)
    return pl.pallas_call(
        body,
        out_shape=(jax.ShapeDtypeStruct((n_seq, pw_total), F32),
                   jax.ShapeDtypeStruct(state.shape, F32)),
        compiler_params=pltpu.CompilerParams(vmem_limit_bytes=VMEM_LIMIT_BYTES),
        name="pool_sample",
    )(proj, state, pool_w, pool_scale)


def _route_body(h_ref, g_ref, r_ref, idx_ref, gate_ref, *, n_experts):
    xn = _rms(h_ref[...], g_ref[...])
    logits = _wdot(xn, r_ref[...])
    lane = lax.broadcasted_iota(jnp.int32, logits.shape, 1)
    lm = jnp.where(lane < n_experts, logits, -jnp.inf)
    m1 = jnp.max(lm, axis=1, keepdims=True)
    i1 = jnp.min(jnp.where(lm == m1, lane, LANE), axis=1, keepdims=True)
    lm2 = jnp.where(lane == i1, -jnp.inf, lm)
    m2 = jnp.max(lm2, axis=1, keepdims=True)
    i2 = jnp.min(jnp.where(lm2 == m2, lane, LANE), axis=1, keepdims=True)
    e2 = jnp.exp(m2 - m1)
    den = 1.0 + e2
    idx_ref[...] = jnp.where(lane == 0, i1, jnp.where(lane == 1, i2, 0))
    gate_ref[...] = jnp.where(lane == 0, 1.0 / den, jnp.where(lane == 1, e2 / den, 0.0))


def moe_route(h_all, row_offset, rows, g, r_pad, tm, n_experts, name):
    d = h_all.shape[1]
    off = row_offset // tm
    body = functools.partial(_route_body, n_experts=n_experts)
    return pl.pallas_call(
        body,
        out_shape=(jax.ShapeDtypeStruct((rows, LANE), jnp.int32), jax.ShapeDtypeStruct((rows, LANE), F32)),
        grid=(rows // tm,),
        in_specs=[pl.BlockSpec((tm, d), lambda i: (i + off, 0)), _resident((1, d)), _resident((d, LANE))],
        out_specs=(pl.BlockSpec((tm, LANE), lambda i: (i, 0)), pl.BlockSpec((tm, LANE), lambda i: (i, 0))),
        compiler_params=_params("parallel"),
        name=name,
    )(h_all, g.reshape(1, d), r_pad)


def _dispatch_body(dest_ref, h_ref, gate_ref, zeros_hbm, o_hbm, ext, sem, *, tb, n):
    del zeros_hbm
    i = pl.program_id(0)
    slot = i % 2
    groups = tb // SUBLANE
    d = h_ref.shape[1]

    def row_copy(s, t, u, row):
        return pltpu.make_async_copy(ext.at[s, t, pl.ds(u, 1)], o_hbm.at[pl.ds(row, 1)], sem.at[s])

    def wait_slot(s):
        def body(t, carry):
            pltpu.make_async_copy(ext.at[s, t], o_hbm.at[pl.ds(0, SUBLANE)], sem.at[s]).wait()
            return carry
        lax.fori_loop(0, groups, body, 0)

    @pl.when(i >= 2)
    def _():
        wait_slot(slot)

    ext[slot, :, :, 0:d] = h_ref[...].reshape(groups, SUBLANE, d)
    ext[slot, :, :, d:d + LANE] = gate_ref[...].reshape(groups, SUBLANE, LANE)

    def body(t, carry):
        for u in range(SUBLANE):
            row_copy(slot, t, u, dest_ref[0, 0, t * SUBLANE + u]).start()
        return carry
    lax.fori_loop(0, groups, body, 0)

    @pl.when(i == n - 1)
    def _():
        wait_slot(slot)
        if n >= 2:
            wait_slot(1 - slot)


def moe_dispatch(h_all, row_offset, rows, gates, dest, into, tb, name):
    d = h_all.shape[1]
    off = row_offset // tb
    n = rows // tb
    body = functools.partial(_dispatch_body, tb=tb, n=n)
    return pl.pallas_call(
        body,
        out_shape=jax.ShapeDtypeStruct(into.shape, F32),
        grid=(n,),
        in_specs=[
            pl.BlockSpec((1, 1, tb), lambda i: (i, 0, 0), memory_space=pltpu.SMEM),
            pl.BlockSpec((tb, d), lambda i: (i + off, 0)),
            pl.BlockSpec((tb, LANE), lambda i: (i, 0)),
            pl.BlockSpec(memory_space=pl.ANY),
        ],
        out_specs=pl.BlockSpec(memory_space=pl.ANY),
        scratch_shapes=[pltpu.VMEM((2, tb // SUBLANE, SUBLANE, d + LANE), F32), pltpu.SemaphoreType.DMA((2,))],
        input_output_aliases={3: 0},
        compiler_params=_params("arbitrary"),
        name=name,
    )(dest.reshape(n, 1, tb), h_all, gates, into)


def _experts_body(te_ref, nu_ref, nb_ref, g_first_ref, g_next_ref, s_cur_ref,
                  h_hbm, g_ref, wg_ref, wu_ref, wo_ref, y_hbm,
                  xbuf, acc, xn_ref, gsem, ssem, *, tm, block, n_rows):
    i = pl.program_id(0)
    j = pl.program_id(1)
    n_tiles = pl.num_programs(0)
    nj = pl.num_programs(1)
    n_used = nu_ref[0]
    slot = i % 2
    valid = i < n_used
    chunks = tm // SUBLANE
    d = xn_ref.shape[1]

    def gather(idx_ref, s):
        def body(t, carry):
            row0 = pl.multiple_of(idx_ref[0, 0, t], SUBLANE)
            pltpu.make_async_copy(h_hbm.at[pl.ds(row0, SUBLANE), pl.ds(0, d)], xbuf.at[s, t], gsem.at[s]).start()
            return carry
        lax.fori_loop(0, chunks, body, 0)

    def wait_gather(s):
        def body(t, carry):
            pltpu.make_async_copy(h_hbm.at[pl.ds(0, SUBLANE), pl.ds(0, d)], xbuf.at[s, t], gsem.at[s]).wait()
            return carry
        lax.fori_loop(0, chunks, body, 0)

    def scatter():
        def body(t, carry):
            row0 = pl.multiple_of(s_cur_ref[0, 0, t], SUBLANE)
            pltpu.make_async_copy(acc.at[t], y_hbm.at[pl.ds(row0, SUBLANE)], ssem.at[0]).start()
            return carry
        lax.fori_loop(0, chunks, body, 0)

    def wait_scatter():
        def body(t, carry):
            pltpu.make_async_copy(acc.at[t], y_hbm.at[pl.ds(0, SUBLANE)], ssem.at[0]).wait()
            return carry
        lax.fori_loop(0, chunks, body, 0)

    @pl.when(j == 0)
    def _():
        @pl.when(i == 0)
        def _():
            gather(g_first_ref, 0)
            acc[...] = jnp.zeros_like(acc)
            first_free = nb_ref[0] * (block // SUBLANE)
            n_free = n_rows // SUBLANE - first_free

            def zero_copy(row0):
                return pltpu.make_async_copy(acc.at[0], y_hbm.at[pl.ds(pl.multiple_of(row0, SUBLANE), SUBLANE)],
                                             ssem.at[0])

            def fill(c, carry):
                zero_copy((first_free + c) * SUBLANE).start()
                zero_copy(n_rows + (first_free + c) * SUBLANE).start()
                return carry
            lax.fori_loop(0, n_free, fill, 0)

            def fill_spare(c, carry):
                zero_copy(2 * n_rows + c * SUBLANE).start()
                return carry
            lax.fori_loop(0, chunks, fill_spare, 0)

            def drain(c, carry):
                zero_copy(0).wait()
                return carry
            lax.fori_loop(0, 2 * n_free + chunks, drain, 0)

        @pl.when(i + 1 < n_used)
        def _():
            gather(g_next_ref, 1 - slot)

        @pl.when(valid)
        def _():
            wait_gather(slot)
            xn_ref[...] = _rms(xbuf[slot].reshape(tm, d), g_ref[...]).astype(BF16)

        @pl.when((i >= 1) & (i <= n_used))
        def _():
            wait_scatter()

        @pl.when(valid)
        def _():
            acc[...] = jnp.zeros_like(acc)

    @pl.when(valid)
    def _():
        xn = xn_ref[...]
        a = _dot(xn, wg_ref[0].astype(BF16))
        b = _dot(xn, wu_ref[0].astype(BF16))
        part = _dot((_silu(a) * b).astype(BF16), wo_ref[0].astype(BF16)).reshape(chunks, SUBLANE, d)
        acc[...] += part

        @pl.when(j == nj - 1)
        def _():
            scatter()

            @pl.when(i == n_tiles - 1)
            def _():
                wait_scatter()


def moe_experts(h_ps, g, wi, wo, gather_rows, scatter_rows, tile_expert, n_used, n_blocks_used, block, n_rows,
                tm, tf):
    d = wo.shape[2]
    f = wo.shape[1]
    nj = f // tf
    n_tiles = gather_rows.shape[0]
    chunks = tm // SUBLANE

    def wblock(col_off):
        def index_map(i, j, te, nu, nb):
            jj = jnp.where(i < nu[0], j, nj - 1)
            return (te[i], 0, col_off + jj)
        return index_map

    def woblock(i, j, te, nu, nb):
        return (te[i], jnp.where(i < nu[0], j, nj - 1), 0)

    def smem_tile(index_map):
        return pl.BlockSpec((1, 1, chunks), index_map, memory_space=pltpu.SMEM)

    grid_spec = pltpu.PrefetchScalarGridSpec(
        num_scalar_prefetch=3,
        grid=(n_tiles, nj),
        in_specs=[
            smem_tile(lambda i, j, te, nu, nb: (0, 0, 0)),
            smem_tile(lambda i, j, te, nu, nb: (jnp.minimum(i + 1, n_tiles - 1), 0, 0)),
            smem_tile(lambda i, j, te, nu, nb: (i, 0, 0)),
            pl.BlockSpec(memory_space=pl.ANY),
            pl.BlockSpec((1, d), lambda i, j, te, nu, nb: (0, 0)),
            pl.BlockSpec((1, d, tf), wblock(0)),
            pl.BlockSpec((1, d, tf), wblock(nj)),
            pl.BlockSpec((1, tf, d), woblock),
        ],
        out_specs=pl.BlockSpec(memory_space=pl.ANY),
        scratch_shapes=[
            pltpu.VMEM((2, chunks, SUBLANE, d), F32),
            pltpu.VMEM((chunks, SUBLANE, d), F32),
            pltpu.VMEM((tm, d), BF16),
            pltpu.SemaphoreType.DMA((2,)),
            pltpu.SemaphoreType.DMA((1,)),
        ],
    )
    return pl.pallas_call(
        functools.partial(_experts_body, tm=tm, block=block, n_rows=n_rows),
        out_shape=jax.ShapeDtypeStruct((2 * n_rows + tm, d), F32),
        grid_spec=grid_spec,
        compiler_params=_params("arbitrary", "arbitrary"),
        name="moe_experts",
    )(tile_expert, n_used, n_blocks_used, gather_rows, gather_rows, scatter_rows, h_ps, g.reshape(1, d),
      wi, wi, wo)


def _combine_body(nb_ref, hx_ref, y1_ref, y2_ref, fn_ref, o_ref, *, d, final_norm):
    @pl.when(pl.program_id(0) < nb_ref[0])
    def _():
        hx = hx_ref[...]
        out = hx[:, 0:d] + (hx[:, d:d + 1] * y1_ref[...] + hx[:, d + 1:d + 2] * y2_ref[...])
        if final_norm:
            out = _rms(out, fn_ref[...])
        o_ref[...] = out

    @pl.when(pl.program_id(0) >= nb_ref[0])
    def _():
        o_ref[...] = jnp.zeros_like(o_ref)


def moe_combine(h_ps, y, n_blocks_used, choice_stride, fn, tb, final_norm, name):
    rows, dx = h_ps.shape
    d = y.shape[1]
    off2 = choice_stride // tb

    def used(i, nb):
        return jnp.minimum(i, nb[0] - 1)

    grid_spec = pltpu.PrefetchScalarGridSpec(
        num_scalar_prefetch=1,
        grid=(rows // tb,),
        in_specs=[
            pl.BlockSpec((tb, dx), lambda i, nb: (used(i, nb), 0)),
            pl.BlockSpec((tb, d), lambda i, nb: (used(i, nb), 0)),
            pl.BlockSpec((tb, d), lambda i, nb: (used(i, nb) + off2, 0)),
            pl.BlockSpec((1, d), lambda i, nb: (0, 0)),
        ],
        out_specs=pl.BlockSpec((tb, d), lambda i, nb: (i, 0)),
    )
    return pl.pallas_call(
        functools.partial(_combine_body, d=d, final_norm=final_norm),
        out_shape=jax.ShapeDtypeStruct((rows, d), F32),
        grid_spec=grid_spec,
        compiler_params=_params("arbitrary"),
        name=name,
    )(n_blocks_used, h_ps, y, y, fn.reshape(1, d))


def _unpermute_body(cur_ref, nxt_ref, y_hbm, o_ref, buf, sem, *, tb):
    i = pl.program_id(0)
    n = pl.num_programs(0)
    slot = i % 2
    groups = tb // SUBLANE

    def start(idx_ref, s):
        def body(t, carry):
            for u in range(SUBLANE):
                row = idx_ref[0, 0, t * SUBLANE + u]
                pltpu.make_async_copy(y_hbm.at[pl.ds(row, 1)], buf.at[s, t, pl.ds(u, 1)], sem.at[s]).start()
            return carry
        lax.fori_loop(0, groups, body, 0)

    @pl.when(i == 0)
    def _():
        start(cur_ref, 0)

    @pl.when(i + 1 < n)
    def _():
        start(nxt_ref, 1 - slot)

    def wait_body(t, carry):
        pltpu.make_async_copy(y_hbm.at[pl.ds(0, SUBLANE)], buf.at[slot, t], sem.at[slot]).wait()
        return carry
    lax.fori_loop(0, groups, wait_body, 0)
    o_ref[...] = buf[slot].reshape(o_ref.shape)


def moe_unpermute(y_ps, dest, tb, name):
    rows = dest.shape[0]
    d = y_ps.shape[1]
    n = rows // tb
    idx = dest.reshape(n, 1, tb)

    def smem_tile(index_map):
        return pl.BlockSpec((1, 1, tb), index_map, memory_space=pltpu.SMEM)

    return pl.pallas_call(
        functools.partial(_unpermute_body, tb=tb),
        out_shape=jax.ShapeDtypeStruct((rows, d), F32),
        grid=(n,),
        in_specs=[smem_tile(lambda i: (i, 0, 0)), smem_tile(lambda i: (jnp.minimum(i + 1, n - 1), 0, 0)),
                  pl.BlockSpec(memory_space=pl.ANY)],
        out_specs=pl.BlockSpec((tb, d), lambda i: (i, 0)),
        scratch_shapes=[pltpu.VMEM((2, tb // SUBLANE, SUBLANE, d), F32), pltpu.SemaphoreType.DMA((2,))],
        compiler_params=_params("arbitrary"),
        name=name,
    )(idx, idx, y_ps)


def moe_plan(idx, n_experts, tm, block):
    n_tok, k = idx.shape
    assert k == 2
    n_pairs = n_experts * n_experts
    chunks_per_tile = tm // SUBLANE
    n_rows = -(-(n_tok + n_pairs * (SUBLANE - 1)) // block) * block
    n_chunks = n_rows // SUBLANE

    pair = idx[:, 0] * n_experts + idx[:, 1]
    onehot = (pair[:, None] == jnp.arange(n_pairs, dtype=jnp.int32)[None, :]).astype(jnp.int32)
    csum = jnp.cumsum(onehot, axis=0)
    counts = csum[-1]
    rank = jnp.sum(csum * onehot, axis=1) - 1
    padded = (counts + SUBLANE - 1) // SUBLANE * SUBLANE
    total = jnp.sum(padded)
    last_group = jnp.max(jnp.where(counts > 0, jnp.arange(n_pairs, dtype=jnp.int32), 0))
    padded = padded + jnp.where(jnp.arange(n_pairs) == last_group, (-total) % block, 0)
    group_end = jnp.cumsum(padded)
    group_start = group_end - padded
    dest = jnp.sum(onehot * group_start[None, :], axis=1) + rank
    rows_used = group_end[-1]

    chunk_row = jnp.arange(n_chunks, dtype=jnp.int32) * SUBLANE
    chunk_group = jnp.sum((chunk_row[:, None] >= group_end[None, :]).astype(jnp.int32), axis=1)
    chunk_used = chunk_row < rows_used
    chunk_group = jnp.minimum(chunk_group, n_pairs - 1)
    chunk_experts = jnp.stack([chunk_group // n_experts, chunk_group % n_experts], axis=1)
    experts = jnp.arange(n_experts, dtype=jnp.int32)
    member = ((chunk_experts[:, :, None] == experts[None, None, :]) & chunk_used[:, None, None])
    member = member.astype(jnp.int32)
    member_e = jnp.sum(member, axis=1)
    csum_e = jnp.cumsum(member_e, axis=0)
    counts_e = csum_e[-1]
    tiles_e = (counts_e + chunks_per_tile - 1) // chunks_per_tile
    tile_end = jnp.cumsum(tiles_e)
    start_e = (tile_end - tiles_e) * chunks_per_tile
    pos = jnp.sum(member * (start_e[None, None, :] + csum_e[:, None, :] - 1), axis=2)
    n_tiles = 2 * n_chunks // chunks_per_tile + n_experts
    n_pos = n_tiles * chunks_per_tile
    pos = jnp.where(chunk_used[:, None], pos, n_pos)
    entry = 2 * jnp.arange(n_chunks, dtype=jnp.int32)[:, None] + jnp.arange(2, dtype=jnp.int32)[None, :] + 1
    table = jnp.zeros((n_pos,), jnp.int32).at[pos.reshape(-1)].set(entry.reshape(-1), mode="drop")
    table = table.reshape(n_tiles, chunks_per_tile)
    filled = table > 0
    src_chunk = (table - 1) // 2
    choice = (table - 1) % 2
    spare = 2 * n_rows + jnp.arange(chunks_per_tile, dtype=jnp.int32) * SUBLANE
    gather_rows = jnp.where(filled, src_chunk * SUBLANE, 0)
    scatter_rows = jnp.where(filled, choice * n_rows + src_chunk * SUBLANE, spare[None, :])

    n_used = tile_end[-1]
    tile_ids = jnp.arange(n_tiles, dtype=jnp.int32)
    tile_expert = jnp.sum((tile_ids[:, None] >= tile_end[None, :]).astype(jnp.int32), axis=1)
    last_expert = jnp.sum((n_used - 1 >= tile_end).astype(jnp.int32))
    tile_expert = jnp.where(tile_ids < n_used, tile_expert, last_expert).astype(jnp.int32)
    return (dest.astype(jnp.int32),
            gather_rows.reshape(n_tiles, 1, chunks_per_tile).astype(jnp.int32),
            scatter_rows.reshape(n_tiles, 1, chunks_per_tile).astype(jnp.int32),
            tile_expert, n_used.reshape(1).astype(jnp.int32),
            (rows_used // block).reshape(1).astype(jnp.int32),
            n_rows)


def _pad_lanes(v):
    return jnp.zeros((1, LANE), F32).at[0, :v.shape[0]].set(v)


def kernel(x_prompt, x_sample, state_dn, state_dn_conv, state_pool, cache_mem_k, cache_mem_v, mem_prompt, a_norm1, a_w_in, a_conv_w, a_a_log, a_dt_bias, a_o_norm, a_w_out, a_norm2, a_ffn_wi, a_ffn_wo, b_norm1, b_w_in, b_pool_w, b_pool_scale, b_w_out, b_norm2, b_router, b_moe_wi, b_moe_wo, m_norm, m_w_k, m_w_v, final_norm):
    bp, seq, d = x_prompt.shape
    bs = x_sample.shape[0]
    assert x_sample.shape[1] == 1
    depth = m_norm.shape[0]
    assert depth % 2 == 0, "the output norm is fused into the expert block of the last (odd) layer"
    heads, dk, dv = state_dn.shape[2], state_dn.shape[3], state_dn.shape[4]
    conv_ch = state_dn_conv.shape[-1]
    qk_w, v_w = heads * dk, heads * dv
    assert conv_ch == 2 * qk_w + v_w and dk == LANE and dv == LANE and heads <= LANE
    n_mem, x_heads, x_dh = cache_mem_k.shape[2], cache_mem_k.shape[3], cache_mem_k.shape[4]
    xw = x_heads * x_dh
    pool_w_total = state_pool.shape[-1]
    pool_buf = state_pool.shape[2]
    d_ff = a_ffn_wo.shape[1]

    hp = x_prompt.reshape(bp * seq, d)
    hs = x_sample.reshape(bs, d)

    o1, o2, o3, o4 = conv_ch, conv_ch + v_w, conv_ch + v_w + heads, conv_ch + v_w + 2 * heads
    col_a = {"qkv": 0, "z": conv_ch, "cq": conv_ch + v_w, "b": conv_ch + v_w + xw, "a": conv_ch + v_w + xw + LANE}
    n_a = -(-(col_a["a"] + LANE) // 512) * 512

    dn_p, conv_p, pool_p, mk_all, mv_all = [], [], [], [], []
    dn_s, conv_s, pool_s = [], [], []
    for i in range(depth):
        j = i // 2
        w_kv = jnp.concatenate([m_w_k[i], m_w_v[i]], axis=1).astype(BF16)
        kv = rms_matmul(mem_prompt.reshape(bp * n_mem, d), m_norm[i], w_kv, 256, f"mem_kv_{i}")
        mk_all.append(kv[:, :xw].reshape(bp, n_mem, x_heads, x_dh))
        mv_all.append(kv[:, xw:].reshape(bp, n_mem, x_heads, x_dh))
        if i % 2 == 0:
            w = a_w_in[j]
            w_in = jnp.zeros((d, n_a), F32)
            w_in = w_in.at[:, :o2].set(w[:, :o2])
            w_in = w_in.at[:, col_a["cq"]:col_a["cq"] + xw].set(w[:, o4:])
            w_in = w_in.at[:, col_a["b"]:col_a["b"] + heads].set(w[:, o2:o3])
            w_in = w_in.at[:, col_a["a"]:col_a["a"] + heads].set(w[:, o3:o4])
            a_log, dt_b = _pad_lanes(a_a_log[j]), _pad_lanes(a_dt_bias[j])
            o_norm = a_o_norm[j].reshape(1, dv)

            proj_p = rms_matmul(hp, a_norm1[j], w_in.astype(BF16), 256, f"a_in_p_{i}")
            proj_s = rms_matmul(hs, a_norm1[j], w_in, bs, f"a_in_s_{i}", tn=512)
            mix_p, s_p = deltanet_prompt(proj_p, col_a, a_conv_w[j], a_log, dt_b, o_norm,
                                         bp, seq, heads, dk, dv, rows=DN_STEP_CHUNKS * CHUNK)
            mix_s, s_s, c_s = deltanet_sample(proj_s, col_a, state_dn_conv[j], state_dn[j], a_conv_w[j],
                                              a_log, dt_b, o_norm, heads, dk, dv, nb=SUBLANE)
            xa_p = xattn_prompt(proj_p, col_a["cq"], kv, bp, seq, n_mem, x_heads, x_dh, tq=min(1024, seq))
            xa_s = xattn_sample(proj_s, col_a["cq"], cache_mem_k, cache_mem_v, i, nb=SUBLANE)
            hp = out_proj(hp, mix_p, xa_p, a_w_out[j].astype(BF16), 512, f"a_out_p_{i}")
            hs = out_proj(hs, mix_s, xa_s, a_w_out[j], bs, f"a_out_s_{i}")
            hp = ffn_dense(hp, a_norm2[j], a_ffn_wi[j].astype(BF16), a_ffn_wo[j].astype(BF16), 256,
                           f"a_ffn_p_{i}")
            hs = ffn_dense(hs, a_norm2[j], a_ffn_wi[j], a_ffn_wo[j], bs, f"a_ffn_s_{i}", tf=256)
            dn_p.append(s_p)
            conv_p.append(proj_p.reshape(bp, seq, n_a)[:, seq - (CONV_WIDTH - 1):, :conv_ch])
            dn_s.append(s_s)
            conv_s.append(c_s)
        else:
            ps = b_pool_scale[j].reshape(1, pool_w_total)
            last = i == depth - 1

            proj_p = rms_matmul(hp, b_norm1[j], b_w_in[j].astype(BF16), 512, f"b_in_p_{i}")
            proj_s = rms_matmul(hs, b_norm1[j], b_w_in[j], bs, f"b_in_s_{i}")
            mix_p = pool_prompt(proj_p, b_pool_w[j].astype(BF16), ps, bp, seq, tt=512)
            mix_s, buf_s = pool_sample(proj_s, state_pool[j], b_pool_w[j], ps)
            xa_p = xattn_prompt(proj_p, pool_w_total, kv, bp, seq, n_mem, x_heads, x_dh, tq=min(1024, seq))
            xa_s = xattn_sample(proj_s, pool_w_total, cache_mem_k, cache_mem_v, i, nb=SUBLANE)
            n_p = bp * seq
            h_all = out_proj(hp, mix_p, xa_p, b_w_out[j].astype(BF16), 512, f"b_out_p_{i}", out_rows=n_p + bs)
            h_all = out_proj(hs, mix_s, xa_s, b_w_out[j], bs, f"b_out_s_{i}", out_rows=n_p + bs,
                             into=h_all, row_offset=n_p)
            n_experts = b_router.shape[-1]
            r_pad = jnp.zeros((d, LANE), F32).at[:, :n_experts].set(b_router[j])
            idx_p, gate_p = moe_route(h_all, 0, n_p, b_norm2[j], r_pad.astype(BF16), 512, n_experts,
                                      f"b_route_p_{i}")
            idx_s, gate_s = moe_route(h_all, n_p, bs, b_norm2[j], r_pad, bs, n_experts, f"b_route_s_{i}")
            idx = jnp.concatenate([idx_p[:, :TOP_K], idx_s[:, :TOP_K]], axis=0)
            (dest, gather_rows, scatter_rows, tile_expert, n_used, n_blocks_used, n_rows) = moe_plan(
                idx, n_experts, MOE_TILE, MOE_ROW_BLOCK)
            h_ps = jnp.zeros((n_rows, d + LANE), F32)
            h_ps = moe_dispatch(h_all, 0, n_p, gate_p, dest[:n_p], h_ps, 512, f"b_dispatch_p_{i}")
            h_ps = moe_dispatch(h_all, n_p, bs, gate_s, dest[n_p:], h_ps, bs, f"b_dispatch_s_{i}")
            y = moe_experts(h_ps, b_norm2[j], b_moe_wi[j], b_moe_wo[j], gather_rows, scatter_rows, tile_expert,
                            n_used, n_blocks_used, MOE_ROW_BLOCK, n_rows, MOE_TILE, MOE_FF_TILE)
            y_ps = moe_combine(h_ps, y, n_blocks_used, n_rows, final_norm, MOE_ROW_BLOCK, last,
                               f"b_combine_{i}")
            hp = moe_unpermute(y_ps, dest[:n_p], 512, f"b_unpermute_p_{i}")
            hs = moe_unpermute(y_ps, dest[n_p:], bs, f"b_unpermute_s_{i}")
            pool_p.append(proj_p.reshape(bp, seq, -1)[:, seq - pool_buf:, :pool_w_total])
            pool_s.append(buf_s)

    y_prompt = hp.reshape(bp, seq, d)
    y_sample = hs.reshape(bs, 1, d)
    return (y_prompt, y_sample, jnp.stack(dn_p), jnp.stack(conv_p), jnp.stack(pool_p),
            jnp.stack(mk_all), jnp.stack(mv_all), jnp.stack(dn_s), jnp.stack(conv_s), jnp.stack(pool_s))
```

```python
import functools

import jax
import jax.numpy as jnp
from jax import lax
from jax.experimental import pallas as pl
from jax.experimental.pallas import tpu as pltpu

F32 = jnp.float32
BF16 = jnp.bfloat16
HIGHEST = lax.Precision.HIGHEST

EPS = 1e-6
CHUNK = 64
CONV_WIDTH = 4
POOL_WINDOWS = (2, 4, 8, 16)
PAST_LEN = 16384
TOP_K = 2

LANE = 128
SUBLANE = 8
VMEM_LIMIT_BYTES = 56 * 2**20

MOE_TILE = 1024
MOE_FF_TILE = 512
MOE_ROW_BLOCK = 512
DN_STEP_CHUNKS = 4
INV_BLOCK = 16
POOL_HALO = 32


def _params(*sem):
    return pltpu.CompilerParams(dimension_semantics=sem, vmem_limit_bytes=VMEM_LIMIT_BYTES)


def _resident(shape):
    nd = len(shape)
    return pl.BlockSpec(shape, lambda *_: (0,) * nd, pipeline_mode=pl.Buffered(1))


def _rms(x, w):
    return x * lax.rsqrt(jnp.mean(x * x, axis=-1, keepdims=True) + EPS) * w


def _silu(x):
    half = 0.5 * x
    return half + half * jnp.tanh(half)


def _dot(a, b):
    return jnp.dot(a, b, preferred_element_type=F32)


def _wdot(a, w):
    if w.dtype == F32:
        return jnp.dot(a, w, precision=HIGHEST, preferred_element_type=F32)
    return jnp.dot(a.astype(BF16), w, preferred_element_type=F32)


def _bdot(a, b):
    return lax.dot_general(a, b, (((2,), (1,)), ((0,), (0,))), preferred_element_type=F32)


def _bdot_nt(a, b):
    return lax.dot_general(a, b, (((2,), (2,)), ((0,), (0,))), preferred_element_type=F32)


def _bdot_tn(a, b):
    return lax.dot_general(a, b, (((1,), (1,)), ((0,), (0,))), preferred_element_type=F32)


def _rms_matmul_body(x_ref, g_ref, w_ref, o_ref):
    xn = _rms(x_ref[...], g_ref[...])
    o_ref[...] = _wdot(xn, w_ref[...])


def _wspec(shape, index_map, steps):
    if steps == 1:
        return pl.BlockSpec(shape, index_map, pipeline_mode=pl.Buffered(1))
    return pl.BlockSpec(shape, index_map)


def rms_matmul(x, g, w, tm, name, tn=None):
    m, d = x.shape
    n = w.shape[1]
    tn = n if tn is None else tn
    nj = n // tn
    return pl.pallas_call(
        _rms_matmul_body,
        out_shape=jax.ShapeDtypeStruct((m, n), F32),
        grid=(m // tm, nj),
        in_specs=[pl.BlockSpec((tm, d), lambda i, j: (i, 0)), _resident((1, d)),
                  _wspec((d, tn), lambda i, j: (0, j), nj)],
        out_specs=pl.BlockSpec((tm, tn), lambda i, j: (i, j)),
        compiler_params=_params("parallel", "arbitrary"),
        name=name,
    )(x, g.reshape(1, d), w)


def _out_proj_body(x_ref, a1_ref, a2_ref, w1_ref, w2_ref, *rest, n_main):
    o_ref = rest[-1]

    @pl.when(pl.program_id(0) < n_main)
    def _():
        acc = _wdot(a1_ref[...], w1_ref[...]) + _wdot(a2_ref[...], w2_ref[...])
        o_ref[...] = x_ref[...] + acc

    @pl.when(pl.program_id(0) >= n_main)
    def _():
        o_ref[...] = jnp.zeros_like(o_ref)


def out_proj(x, a1, a2, w, tm, name, out_rows=None, into=None, row_offset=0):
    m, d = x.shape
    k1, k2 = a1.shape[1], a2.shape[1]
    assert w.shape[0] == k1 + k2 and k1 % k2 == 0 and row_offset % tm == 0
    out_rows = m if out_rows is None else out_rows
    off = row_offset // tm
    n_main = m // tm
    n_extra = 0 if (into is not None or out_rows == m) else 1
    assert out_rows - m <= n_extra * tm or into is not None
    last = n_main - 1
    in_specs = [
        pl.BlockSpec((tm, d), lambda i: (jnp.minimum(i, last), 0)),
        pl.BlockSpec((tm, k1), lambda i: (jnp.minimum(i, last), 0)),
        pl.BlockSpec((tm, k2), lambda i: (jnp.minimum(i, last), 0)),
        pl.BlockSpec((k1, d), lambda i: (0, 0), pipeline_mode=pl.Buffered(1)),
        pl.BlockSpec((k2, d), lambda i: (k1 // k2, 0), pipeline_mode=pl.Buffered(1)),
    ]
    args = [x, a1, a2, w, w]
    aliases = {}
    if into is not None:
        in_specs.append(pl.BlockSpec(memory_space=pl.ANY))
        args.append(into)
        aliases = {len(args) - 1: 0}
    return pl.pallas_call(
        functools.partial(_out_proj_body, n_main=n_main),
        out_shape=jax.ShapeDtypeStruct((out_rows, d), F32),
        grid=(n_main + n_extra,),
        in_specs=in_specs,
        out_specs=pl.BlockSpec((tm, d), lambda i: (i + off, 0)),
        input_output_aliases=aliases,
        compiler_params=_params("parallel"),
        name=name,
    )(*args)


def _ffn_body(h_ref, g_ref, wg_ref, wu_ref, wo_ref, o_ref, xn_ref):
    @pl.when(pl.program_id(1) == 0)
    def _():
        h = h_ref[...]
        xn_ref[...] = _rms(h, g_ref[...])
        o_ref[...] = h

    xn = xn_ref[...]
    a = _wdot(xn, wg_ref[...])
    b = _wdot(xn, wu_ref[...])
    o_ref[...] += _wdot(_silu(a) * b, wo_ref[...])


def ffn_dense(h, g, wi, wo, tm, name, tf=None):
    m, d = h.shape
    f = wo.shape[0]
    tf = f if tf is None else tf
    nj = f // tf
    return pl.pallas_call(
        _ffn_body,
        out_shape=jax.ShapeDtypeStruct((m, d), F32),
        grid=(m // tm, nj),
        in_specs=[
            pl.BlockSpec((tm, d), lambda i, j: (i, 0)),
            _resident((1, d)),
            _wspec((d, tf), lambda i, j: (0, j), nj),
            _wspec((d, tf), lambda i, j: (0, nj + j), nj),
            _wspec((tf, d), lambda i, j: (j, 0), nj),
        ],
        out_specs=pl.BlockSpec((tm, d), lambda i, j: (i, 0)),
        scratch_shapes=[pltpu.VMEM((tm, d), F32)],
        compiler_params=_params("parallel", "arbitrary"),
        name=name,
    )(h, g.reshape(1, d), wi, wi, wo)


def _mem_kv_body(x_ref, g_ref, w_ref, kv_ref, k_ref, v_ref, *, heads, dh):
    xw = heads * dh
    res = _wdot(_rms(x_ref[...], g_ref[...]), w_ref[...])
    kv_ref[...] = res
    for h in range(heads):
        k_ref[:, h, :] = res[:, h * dh:(h + 1) * dh]
        v_ref[:, h, :] = res[:, xw + h * dh:xw + (h + 1) * dh]


def mem_kv(mem, g, w_kv, heads, dh, tm, name):
    m, d = mem.shape
    xw = heads * dh
    head_shape = jax.ShapeDtypeStruct((m, heads, dh), F32)
    head_spec = pl.BlockSpec((tm, heads, dh), lambda i: (i, 0, 0))
    return pl.pallas_call(
        functools.partial(_mem_kv_body, heads=heads, dh=dh),
        out_shape=(jax.ShapeDtypeStruct((m, 2 * xw), F32), head_shape, head_shape),
        grid=(m // tm,),
        in_specs=[pl.BlockSpec((tm, d), lambda i: (i, 0)), _resident((1, d)), _resident((d, 2 * xw))],
        out_specs=(pl.BlockSpec((tm, 2 * xw), lambda i: (i, 0)), head_spec, head_spec),
        compiler_params=_params("parallel"),
        name=name,
    )(mem, g.reshape(1, d), w_kv)


def _out_proj_ffn_body(x_ref, a1_ref, a2_ref, w1_ref, w2_ref, g_ref, wg_ref, wu_ref, wo_ref, o_ref):
    h = x_ref[...] + (_wdot(a1_ref[...], w1_ref[...]) + _wdot(a2_ref[...], w2_ref[...]))
    xn = _rms(h, g_ref[...])
    a = _wdot(xn, wg_ref[...])
    b = _wdot(xn, wu_ref[...])
    o_ref[...] = h + _wdot(_silu(a) * b, wo_ref[...])


def out_proj_ffn(x, a1, a2, w, g, wi, wo, tm, name):
    m, d = x.shape
    k1, k2 = a1.shape[1], a2.shape[1]
    f = wo.shape[0]
    assert w.shape[0] == k1 + k2 and k1 % k2 == 0
    return pl.pallas_call(
        _out_proj_ffn_body,
        out_shape=jax.ShapeDtypeStruct((m, d), F32),
        grid=(m // tm,),
        in_specs=[
            pl.BlockSpec((tm, d), lambda i: (i, 0)),
            pl.BlockSpec((tm, k1), lambda i: (i, 0)),
            pl.BlockSpec((tm, k2), lambda i: (i, 0)),
            pl.BlockSpec((k1, d), lambda i: (0, 0), pipeline_mode=pl.Buffered(1)),
            pl.BlockSpec((k2, d), lambda i: (k1 // k2, 0), pipeline_mode=pl.Buffered(1)),
            _resident((1, d)),
            pl.BlockSpec((d, f), lambda i: (0, 0), pipeline_mode=pl.Buffered(1)),
            pl.BlockSpec((d, f), lambda i: (0, 1), pipeline_mode=pl.Buffered(1)),
            _resident((f, d)),
        ],
        out_specs=pl.BlockSpec((tm, d), lambda i: (i, 0)),
        compiler_params=_params("parallel"),
        name=name,
    )(x, a1, a2, w, w, g.reshape(1, d), wi, wi, wo)


def _pair_blockdiag(y, left):
    yb = y.astype(BF16)
    zero = jnp.zeros_like(yb)
    return jnp.concatenate([jnp.where(left, yb, zero), jnp.where(left, zero, yb)], axis=1)


def _pair_mm(x, y, left):
    return _bdot(x.astype(BF16), _pair_blockdiag(y, left))


def _unit_lower_inverse(mx, eye, blk, left):
    mm = functools.partial(_pair_mm, left=left)
    md = jnp.where(blk, mx, 0.0)
    c = mx - md
    p = eye - md
    m2 = mm(md, md)
    p = p + mm(p, m2)
    m4 = mm(m2, m2)
    p = p + mm(p, m4)
    m8 = mm(m4, m4)
    td = p + mm(p, m8)
    n = mm(td, c)
    n2 = mm(n, n)
    q = eye - n
    q = q + mm(q, n2)
    return mm(q, td)


def _dn_prompt_body(qkv_ref, z_ref, bl_ref, al_ref, cw_ref, alog_ref, dtb_ref, on_ref,
                    dn_ref, sfin_ref, s_ref, xbuf_ref, act_ref, *, rows, heads, dk, dv):
    c = pl.program_id(1)
    wq = heads * dk

    @pl.when(c == 0)
    def _():
        s_ref[...] = jnp.zeros_like(s_ref)
        xbuf_ref[0:SUBLANE, :] = jnp.zeros((SUBLANE, xbuf_ref.shape[1]), F32)

    xbuf_ref[SUBLANE:SUBLANE + rows, :] = qkv_ref[...]
    first = SUBLANE - (CONV_WIDTH - 1)
    y = xbuf_ref[pl.ds(first, rows), :] * cw_ref[0:1, :]
    for j in range(1, CONV_WIDTH):
        y = y + xbuf_ref[pl.ds(first + j, rows), :] * cw_ref[j:j + 1, :]
    act_ref[...] = _silu(y)
    xbuf_ref[0:SUBLANE, :] = xbuf_ref[rows:rows + SUBLANE, :]

    pairs = heads // 2
    ri = lax.broadcasted_iota(jnp.int32, (CHUNK, 2 * CHUNK), 0)
    lane = lax.broadcasted_iota(jnp.int32, (CHUNK, 2 * CHUNK), 1)
    cj = lane % CHUNK
    left = (lane < CHUNK)[None]
    incl = (ri >= cj)[None]
    strict = (ri > cj)[None]
    blk = ((ri // INV_BLOCK) == (cj // INV_BLOCK))[None]
    eye = (ri == cj).astype(F32)[None]
    tr = lax.broadcasted_iota(jnp.int32, (CHUNK, CHUNK), 0)
    tc = lax.broadcasted_iota(jnp.int32, (CHUNK, CHUNK), 1)
    tri = (tr >= tc).astype(F32)

    n_chunks = rows // CHUNK
    zero = jnp.zeros((CHUNK, dk), BF16)

    def chunk_rows(cc):
        return slice(cc * CHUNK, (cc + 1) * CHUNK)

    def per_head(fn):
        return jnp.stack([fn(chunk_rows(cc), h) for cc in range(n_chunks) for h in range(heads)], axis=0)

    def per_pair(fn):
        return jnp.stack([fn(cc, p) for cc in range(n_chunks) for p in range(pairs)], axis=0)

    q3 = per_head(lambda rs, h: act_ref[rs, h * dk:(h + 1) * dk])
    k3 = per_head(lambda rs, h: act_ref[rs, wq + h * dk:wq + (h + 1) * dk])
    v3 = per_head(lambda rs, h: act_ref[rs, 2 * wq + h * dv:2 * wq + (h + 1) * dv])
    qn = q3 * lax.rsqrt(jnp.sum(q3 * q3, axis=-1, keepdims=True) + EPS) * (dk ** -0.5)
    kn = k3 * lax.rsqrt(jnp.sum(k3 * k3, axis=-1, keepdims=True) + EPS)

    beta = jax.nn.sigmoid(bl_ref[...])
    g = -jnp.exp(alog_ref[...]) * jax.nn.softplus(al_ref[...] + dtb_ref[...])
    gcum = [jnp.dot(tri, g[chunk_rows(cc)], precision=HIGHEST, preferred_element_type=F32)
            for cc in range(n_chunks)]
    gcum_t = [jnp.concatenate([gc_, pltpu.roll(gc_, LANE - 1, 1)], axis=0).T for gc_ in gcum]

    def pair_cols(x_of):
        return per_pair(lambda cc, p: jnp.where(left[0], x_of(cc)[:, 2 * p:2 * p + 1],
                                                x_of(cc)[:, 2 * p + 1:2 * p + 2]))

    gc2 = pair_cols(lambda cc: gcum[cc])
    bc2 = pair_cols(lambda cc: beta[chunk_rows(cc)])
    gr2 = per_pair(lambda cc, p: gcum_t[cc][2 * p:2 * p + 1, :])
    decay = jnp.where(incl, jnp.exp(gc2 - gr2), 0.0)

    kb = kn.astype(BF16)
    qb = qn.astype(BF16)

    def pair_of(x, cc, p):
        return x[cc * heads + 2 * p], x[cc * heads + 2 * p + 1]

    k_lhs = per_pair(lambda cc, p: jnp.concatenate(pair_of(kb, cc, p), axis=1))
    q_lhs = per_pair(lambda cc, p: jnp.concatenate(pair_of(qb, cc, p), axis=1))
    k_bd = per_pair(lambda cc, p: jnp.concatenate(
        [jnp.concatenate([pair_of(kb, cc, p)[0], zero], axis=1),
         jnp.concatenate([zero, pair_of(kb, cc, p)[1]], axis=1)], axis=0))
    kk = _bdot_nt(k_lhs, k_bd)
    mx = jnp.where(strict, bc2 * kk * decay, 0.0)
    tinv = _unit_lower_inverse(mx, eye, blk, left)

    gcum_rows = jnp.concatenate(gcum, axis=0)
    gc = per_head(lambda rs, h: gcum_rows[rs, h:h + 1])
    bc = per_head(lambda rs, h: beta[rs, h:h + 1])
    eg = jnp.exp(gc)
    rhs = jnp.concatenate([v3 * bc, kn * (bc * eg)], axis=-1)
    sol = _bdot(_pair_blockdiag(tinv, left), rhs.astype(BF16).reshape(n_chunks * pairs, 2 * CHUNK, dv + dk))
    sol = sol.reshape(n_chunks * heads, CHUNK, dv + dk)
    w_val = sol[..., :dv]
    k_cd = sol[..., dv:].astype(BF16)
    qk_bd = _pair_blockdiag(_bdot_nt(q_lhs, k_bd) * decay, left)
    q_dec = (qn * eg).astype(BF16)
    g_last = gc[:, CHUNK - 1:CHUNK, :]
    k_tail = (kn * jnp.exp(g_last - gc)).astype(BF16)
    c_decay = jnp.exp(g_last)

    for cc in range(n_chunks):
        hs = slice(cc * heads, (cc + 1) * heads)
        ps = slice(cc * pairs, (cc + 1) * pairs)
        rs = chunk_rows(cc)
        s = s_ref[...]
        sb = s.astype(BF16)
        u = w_val[hs] - _bdot(k_cd[hs], sb)
        ub = u.astype(BF16)
        intra = _bdot(qk_bd[ps], ub.reshape(pairs, 2 * CHUNK, dv))
        o = _bdot(q_dec[hs], sb) + intra.reshape(heads, CHUNK, dv)
        s_ref[...] = s * c_decay[hs] + _bdot_tn(k_tail[hs], ub)

        o = o * lax.rsqrt(jnp.mean(o * o, axis=-1, keepdims=True) + EPS) * on_ref[...]
        for h in range(heads):
            z = z_ref[rs, h * dv:(h + 1) * dv]
            dn_ref[rs, h * dv:(h + 1) * dv] = o[h] * _silu(z)

    @pl.when(c == pl.num_programs(1) - 1)
    def _():
        sfin_ref[0] = s_ref[...]


def deltanet_prompt(proj, col, conv_w, a_log, dt_bias, o_norm, batch, seq, heads, dk, dv, rows):
    conv_ch = 2 * heads * dk + heads * dv
    zw = heads * dv
    steps = seq // rows

    def row_map(width_off):
        return lambda b, c: (b * steps + c, width_off)

    body = functools.partial(_dn_prompt_body, rows=rows, heads=heads, dk=dk, dv=dv)
    return pl.pallas_call(
        body,
        out_shape=(jax.ShapeDtypeStruct((batch * seq, zw), F32),
                   jax.ShapeDtypeStruct((batch, heads, dk, dv), F32)),
        grid=(batch, steps),
        in_specs=[
            pl.BlockSpec((rows, conv_ch), row_map(col["qkv"] // conv_ch)),
            pl.BlockSpec((rows, zw), row_map(col["z"] // zw)),
            pl.BlockSpec((rows, LANE), row_map(col["b"] // LANE)),
            pl.BlockSpec((rows, LANE), row_map(col["a"] // LANE)),
            _resident((CONV_WIDTH, conv_ch)),
            _resident((1, LANE)),
            _resident((1, LANE)),
            _resident((1, dv)),
        ],
        out_specs=(pl.BlockSpec((rows, zw), lambda b, c: (b * steps + c, 0)),
                   pl.BlockSpec((1, heads, dk, dv), lambda b, c: (b, 0, 0, 0))),
        scratch_shapes=[
            pltpu.VMEM((heads, dk, dv), F32),
            pltpu.VMEM((rows + SUBLANE, conv_ch), F32),
            pltpu.VMEM((rows, conv_ch), F32),
        ],
        compiler_params=_params("parallel", "arbitrary"),
        name="deltanet_prompt",
    )(proj, proj, proj, proj, conv_w, a_log, dt_bias, o_norm)


def _dn_sample_body(qkv_ref, z_ref, bl_ref, al_ref, cbuf_ref, st_ref, cw_ref, alog_ref, dtb_ref, on_ref,
                    dn_ref, sto_ref, cbo_ref, act_ref, *, nb, heads, dk, dv):
    wq = heads * dk
    new = qkv_ref[...]
    y = cbuf_ref[:, 0, :] * cw_ref[0:1, :]
    for j in range(1, CONV_WIDTH - 1):
        y = y + cbuf_ref[:, j, :] * cw_ref[j:j + 1, :]
    y = y + new * cw_ref[CONV_WIDTH - 1:CONV_WIDTH, :]
    act_ref[...] = _silu(y)
    for j in range(CONV_WIDTH - 2):
        cbo_ref[:, j, :] = cbuf_ref[:, j + 1, :]
    cbo_ref[:, CONV_WIDTH - 2, :] = new

    beta = jax.nn.sigmoid(bl_ref[...])
    g = -jnp.exp(alog_ref[...]) * jax.nn.softplus(al_ref[...] + dtb_ref[...])
    dec = jnp.exp(g)

    q_t, k_t = [], []
    for h in range(heads):
        q = act_ref[:, h * dk:(h + 1) * dk]
        k = act_ref[:, wq + h * dk:wq + (h + 1) * dk]
        qn = q * lax.rsqrt(jnp.sum(q * q, axis=-1, keepdims=True) + EPS) * (dk ** -0.5)
        kn = k * lax.rsqrt(jnp.sum(k * k, axis=-1, keepdims=True) + EPS)
        q_t.append(qn.T)
        k_t.append(kn.T)

    def per_head(fn):
        return jnp.stack([fn(h) for h in range(heads)], axis=0)

    for i in range(nb):
        row = slice(i, i + 1)
        kcol = per_head(lambda h: k_t[h][:, i:i + 1])
        qcol = per_head(lambda h: q_t[h][:, i:i + 1])
        v = per_head(lambda h: act_ref[row, 2 * wq + h * dv:2 * wq + (h + 1) * dv])
        z = per_head(lambda h: z_ref[row, h * dv:(h + 1) * dv])
        b_i = per_head(lambda h: beta[row, h:h + 1])
        d_i = per_head(lambda h: dec[row, h:h + 1])
        s = st_ref[i] * d_i
        u = b_i * (v - jnp.sum(kcol * s, axis=1, keepdims=True))
        s = s + kcol * u
        o = jnp.sum(qcol * s, axis=1, keepdims=True)
        sto_ref[i] = s
        o = o * lax.rsqrt(jnp.mean(o * o, axis=-1, keepdims=True) + EPS) * on_ref[...]
        o = o * _silu(z)
        for h in range(heads):
            dn_ref[row, h * dv:(h + 1) * dv] = o[h]


def deltanet_sample(proj, col, conv_buf, state, conv_w, a_log, dt_bias, o_norm, heads, dk, dv, nb):
    n_seq = proj.shape[0]
    conv_ch = 2 * heads * dk + heads * dv
    zw = heads * dv
    body = functools.partial(_dn_sample_body, nb=nb, heads=heads, dk=dk, dv=dv)
    return pl.pallas_call(
        body,
        out_shape=(jax.ShapeDtypeStruct((n_seq, zw), F32),
                   jax.ShapeDtypeStruct(state.shape, F32),
                   jax.ShapeDtypeStruct(conv_buf.shape, F32)),
        grid=(n_seq // nb,),
        in_specs=[
            pl.BlockSpec((nb, conv_ch), lambda i: (i, col["qkv"] // conv_ch)),
            pl.BlockSpec((nb, zw), lambda i: (i, col["z"] // zw)),
            pl.BlockSpec((nb, LANE), lambda i: (i, col["b"] // LANE)),
            pl.BlockSpec((nb, LANE), lambda i: (i, col["a"] // LANE)),
            pl.BlockSpec((nb, CONV_WIDTH - 1, conv_ch), lambda i: (i, 0, 0)),
            pl.BlockSpec((nb, heads, dk, dv), lambda i: (i, 0, 0, 0)),
            _resident((CONV_WIDTH, conv_ch)),
            _resident((1, LANE)),
            _resident((1, LANE)),
            _resident((1, dv)),
        ],
        out_specs=(pl.BlockSpec((nb, zw), lambda i: (i, 0)),
                   pl.BlockSpec((nb, heads, dk, dv), lambda i: (i, 0, 0, 0)),
                   pl.BlockSpec((nb, CONV_WIDTH - 1, conv_ch), lambda i: (i, 0, 0))),
        scratch_shapes=[
            pltpu.VMEM((nb, conv_ch), F32),
        ],
        compiler_params=_params("parallel"),
        name="deltanet_sample",
    )(proj, proj, proj, proj, conv_buf, state, conv_w, a_log, dt_bias, o_norm)


def _xattn_prompt_body(q_ref, k_ref, v_ref, o_ref, *, heads, dh):
    for h in range(heads):
        sl = slice(h * dh, (h + 1) * dh)
        q = q_ref[:, sl].astype(BF16)
        k = k_ref[:, sl].astype(BF16)
        v = v_ref[:, sl].astype(BF16)
        s = lax.dot_general(q, k, (((1,), (1,)), ((), ())), preferred_element_type=F32) * (dh ** -0.5)
        e = jnp.exp(s - jnp.max(s, axis=-1, keepdims=True))
        p = e / jnp.sum(e, axis=-1, keepdims=True)
        o_ref[:, sl] = _dot(p.astype(BF16), v)


def xattn_prompt(proj, cq_col, kv, batch, seq, n_mem, heads, dh, tq):
    xw = heads * dh
    steps = seq // tq
    body = functools.partial(_xattn_prompt_body, heads=heads, dh=dh)
    return pl.pallas_call(
        body,
        out_shape=jax.ShapeDtypeStruct((batch * seq, xw), F32),
        grid=(batch, steps),
        in_specs=[
            pl.BlockSpec((tq, xw), lambda b, i: (b * steps + i, cq_col // xw)),
            pl.BlockSpec((n_mem, xw), lambda b, i: (b, 0)),
            pl.BlockSpec((n_mem, xw), lambda b, i: (b, 1)),
        ],
        out_specs=pl.BlockSpec((tq, xw), lambda b, i: (b * steps + i, 0)),
        compiler_params=_params("parallel", "parallel"),
        name="xattn_prompt",
    )(proj, kv, kv)


def _xattn_sample_body(q_ref, k_ref, v_ref, o_ref, *, nb, heads, dh):
    rep = SUBLANE // heads
    n_t = k_ref.shape[2] // SUBLANE

    def over_groups(x, op):
        out = x
        for t in range(1, rep):
            out = op(out, pltpu.roll(x, t * heads, 0))
        return out

    for i in range(nb):
        row = slice(i, i + 1)
        q8 = jnp.concatenate([q_ref[row, h * dh:(h + 1) * dh] for h in range(heads)] * rep, axis=0)
        k3 = k_ref[0, i].reshape(n_t, SUBLANE, dh)
        v3 = v_ref[0, i].reshape(n_t, SUBLANE, dh)
        s = jnp.sum(k3 * q8[None], axis=-1, keepdims=True) * (dh ** -0.5)
        m = over_groups(jnp.max(s, axis=0), jnp.maximum)
        e = jnp.exp(s - m[None])
        den = over_groups(jnp.sum(e, axis=0), jnp.add)
        p = e / den[None]
        acc = over_groups(jnp.sum(p * v3, axis=0), jnp.add)
        for h in range(heads):
            o_ref[row, h * dh:(h + 1) * dh] = acc[h:h + 1, :]


def xattn_sample(proj, cq_col, cache_k, cache_v, layer, nb):
    n_layers, n_seq, n_mem, heads, dh = cache_k.shape
    assert SUBLANE % heads == 0 and (n_mem * heads) % SUBLANE == 0
    xw = heads * dh
    body = functools.partial(_xattn_sample_body, nb=nb, heads=heads, dh=dh)
    cache_spec = pl.BlockSpec((1, nb, n_mem * heads, dh), lambda i: (layer, i, 0, 0))
    flat = (n_layers, n_seq, n_mem * heads, dh)
    return pl.pallas_call(
        body,
        out_shape=jax.ShapeDtypeStruct((n_seq, xw), F32),
        grid=(n_seq // nb,),
        in_specs=[pl.BlockSpec((nb, xw), lambda i: (i, cq_col // xw)), cache_spec, cache_spec],
        out_specs=pl.BlockSpec((nb, xw), lambda i: (i, 0)),
        compiler_params=_params("parallel"),
        name="xattn_sample",
    )(proj, cache_k.reshape(flat), cache_v.reshape(flat))


def _pool_mix(sums, cnts, u_of, pw_ref, ps_ref, o_ref, gw):
    for gi in range(len(POOL_WINDOWS)):
        cols = slice(gi * gw, (gi + 1) * gw)
        d = sums[gi] / cnts[gi] - u_of(cols)
        o_ref[:, cols] = _wdot(d, pw_ref[gi]) * ps_ref[:, cols]


def _pool_prompt_body(u_ref, pw_ref, ps_ref, o_ref, xbuf, s2, s4, s8, *, tt, gw):
    t = pl.program_id(1)
    n = POOL_HALO + tt

    @pl.when(t == 0)
    def _():
        xbuf[0:POOL_HALO, :] = jnp.zeros((POOL_HALO, xbuf.shape[1]), F32)

    xbuf[POOL_HALO:n, :] = u_ref[...]
    s2[8:n, :] = xbuf[8:n, :] + xbuf[7:n - 1, :]
    s4[16:n, :] = s2[16:n, gw:] + s2[14:n - 2, gw:]
    s8[24:n, :] = s4[24:n, gw:] + s4[20:n - 4, gw:]
    s16 = s8[32:n, gw:] + s8[24:n - 8, gw:]
    sums = [s2[POOL_HALO:n, 0:gw], s4[POOL_HALO:n, 0:gw], s8[POOL_HALO:n, 0:gw], s16]
    pos = t * tt + lax.broadcasted_iota(jnp.int32, (tt, 1), 0)
    cnts = [jnp.minimum(pos + 1, w).astype(F32) for w in POOL_WINDOWS]
    _pool_mix(sums, cnts, lambda cols: u_ref[:, cols], pw_ref, ps_ref, o_ref, gw)
    xbuf[0:POOL_HALO, :] = xbuf[tt:n, :]


def pool_prompt(proj, pool_w, pool_scale, batch, seq, tt):
    pw_total = pool_scale.shape[-1]
    gw = pw_total // len(POOL_WINDOWS)
    steps = seq // tt
    n = POOL_HALO + tt
    body = functools.partial(_pool_prompt_body, tt=tt, gw=gw)
    return pl.pallas_call(
        body,
        out_shape=jax.ShapeDtypeStruct((batch * seq, pw_total), F32),
        grid=(batch, steps),
        in_specs=[
            pl.BlockSpec((tt, pw_total), lambda b, t: (b * steps + t, 0)),
            _resident(pool_w.shape),
            _resident((1, pw_total)),
        ],
        out_specs=pl.BlockSpec((tt, pw_total), lambda b, t: (b * steps + t, 0)),
        scratch_shapes=[
            pltpu.VMEM((n, pw_total), F32),
            pltpu.VMEM((n, pw_total), F32),
            pltpu.VMEM((n, pw_total - gw), F32),
            pltpu.VMEM((n, pw_total - 2 * gw), F32),
        ],
        compiler_params=_params("parallel", "arbitrary"),
        name="pool_prompt",
    )(proj, pool_w, pool_scale)


def _pool_sample_body(u_ref, st_ref, pw_ref, ps_ref, o_ref, sto_ref, *, gw, buf):
    pw_total = gw * len(POOL_WINDOWS)
    new = u_ref[:, 0:pw_total]
    sums, cnts = [], []
    for gi, w in enumerate(POOL_WINDOWS):
        cols = slice(gi * gw, (gi + 1) * gw)
        s = new[:, cols]
        for i in range(1, w):
            s = s + st_ref[:, buf - i, cols]
        sums.append(s)
        cnts.append(float(min(PAST_LEN + 1, w)))
    _pool_mix(sums, cnts, lambda cols: new[:, cols], pw_ref, ps_ref, o_ref, gw)
    sto_ref[:, 0:buf - 1, :] = st_ref[:, 1:buf, :]
    sto_ref[:, buf - 1, :] = new


def pool_sample(proj, state, pool_w, pool_scale):
    n_seq, buf, pw_total = state.shape
    gw = pw_total // len(POOL_WINDOWS)
    body = functools.partial(_pool_sample_body, gw=gw, buf=buf)
    return pl.pallas_call(
        body,
        out_shape=(jax.ShapeDtypeStruct((n_seq, pw_total), F32),
                   jax.ShapeDtypeStruct(state.shape, F32)),
        compiler_params=pltpu.CompilerParams(vmem_limit_bytes=VMEM_LIMIT_BYTES),
        name="pool_sample",
    )(proj, state, pool_w, pool_scale)


def _route_body(h_ref, g_ref, r_ref, idx_ref, gate_ref, *, n_experts):
    xn = _rms(h_ref[...], g_ref[...])
    logits = _wdot(xn, r_ref[...])
    lane = lax.broadcasted_iota(jnp.int32, logits.shape, 1)
    lm = jnp.where(lane < n_experts, logits, -jnp.inf)
    m1 = jnp.max(lm, axis=1, keepdims=True)
    i1 = jnp.min(jnp.where(lm == m1, lane, LANE), axis=1, keepdims=True)
    lm2 = jnp.where(lane == i1, -jnp.inf, lm)
    m2 = jnp.max(lm2, axis=1, keepdims=True)
    i2 = jnp.min(jnp.where(lm2 == m2, lane, LANE), axis=1, keepdims=True)
    e2 = jnp.exp(m2 - m1)
    den = 1.0 + e2
    idx_ref[...] = jnp.where(lane == 0, i1, jnp.where(lane == 1, i2, 0))
    gate_ref[...] = jnp.where(lane == 0, 1.0 / den, jnp.where(lane == 1, e2 / den, 0.0))


def moe_route(h_all, row_offset, rows, g, r_pad, tm, n_experts, name):
    d = h_all.shape[1]
    off = row_offset // tm
    body = functools.partial(_route_body, n_experts=n_experts)
    return pl.pallas_call(
        body,
        out_shape=(jax.ShapeDtypeStruct((rows, LANE), jnp.int32), jax.ShapeDtypeStruct((rows, LANE), F32)),
        grid=(rows // tm,),
        in_specs=[pl.BlockSpec((tm, d), lambda i: (i + off, 0)), _resident((1, d)), _resident((d, LANE))],
        out_specs=(pl.BlockSpec((tm, LANE), lambda i: (i, 0)), pl.BlockSpec((tm, LANE), lambda i: (i, 0))),
        compiler_params=_params("parallel"),
        name=name,
    )(h_all, g.reshape(1, d), r_pad)


def _dispatch_body(dest_ref, h_ref, gate_ref, zeros_hbm, o_hbm, ext, sem, *, tb, n):
    del zeros_hbm
    i = pl.program_id(0)
    slot = i % 2
    groups = tb // SUBLANE
    d = h_ref.shape[1]

    def row_copy(s, t, u, row):
        return pltpu.make_async_copy(ext.at[s, t, pl.ds(u, 1)], o_hbm.at[pl.ds(row, 1)], sem.at[s])

    def wait_slot(s):
        def body(t, carry):
            pltpu.make_async_copy(ext.at[s, t], o_hbm.at[pl.ds(0, SUBLANE)], sem.at[s]).wait()
            return carry
        lax.fori_loop(0, groups, body, 0)

    @pl.when(i >= 2)
    def _():
        wait_slot(slot)

    ext[slot, :, :, 0:d] = h_ref[...].reshape(groups, SUBLANE, d)
    ext[slot, :, :, d:d + LANE] = gate_ref[...].reshape(groups, SUBLANE, LANE)

    def body(t, carry):
        for u in range(SUBLANE):
            row_copy(slot, t, u, dest_ref[0, 0, t * SUBLANE + u]).start()
        return carry
    lax.fori_loop(0, groups, body, 0)

    @pl.when(i == n - 1)
    def _():
        wait_slot(slot)
        if n >= 2:
            wait_slot(1 - slot)


def moe_dispatch(h_all, row_offset, rows, gates, dest, into, tb, name):
    d = h_all.shape[1]
    off = row_offset // tb
    n = rows // tb
    body = functools.partial(_dispatch_body, tb=tb, n=n)
    return pl.pallas_call(
        body,
        out_shape=jax.ShapeDtypeStruct(into.shape, F32),
        grid=(n,),
        in_specs=[
            pl.BlockSpec((1, 1, tb), lambda i: (i, 0, 0), memory_space=pltpu.SMEM),
            pl.BlockSpec((tb, d), lambda i: (i + off, 0)),
            pl.BlockSpec((tb, LANE), lambda i: (i, 0)),
            pl.BlockSpec(memory_space=pl.ANY),
        ],
        out_specs=pl.BlockSpec(memory_space=pl.ANY),
        scratch_shapes=[pltpu.VMEM((2, tb // SUBLANE, SUBLANE, d + LANE), F32), pltpu.SemaphoreType.DMA((2,))],
        input_output_aliases={3: 0},
        compiler_params=_params("arbitrary"),
        name=name,
    )(dest.reshape(n, 1, tb), h_all, gates, into)


def _experts_body(te_ref, nu_ref, nb_ref, g_first_ref, g_next_ref, s_cur_ref,
                  h_hbm, g_ref, wg_ref, wu_ref, wo_ref, y_hbm,
                  xbuf, acc, xn_ref, gsem, ssem, *, tm, block, n_rows):
    i = pl.program_id(0)
    j = pl.program_id(1)
    n_tiles = pl.num_programs(0)
    nj = pl.num_programs(1)
    n_used = nu_ref[0]
    slot = i % 2
    valid = i < n_used
    chunks = tm // SUBLANE
    d = xn_ref.shape[1]

    def gather(idx_ref, s):
        def body(t, carry):
            row0 = pl.multiple_of(idx_ref[0, 0, t], SUBLANE)
            pltpu.make_async_copy(h_hbm.at[pl.ds(row0, SUBLANE), pl.ds(0, d)], xbuf.at[s, t], gsem.at[s]).start()
            return carry
        lax.fori_loop(0, chunks, body, 0)

    def wait_gather(s):
        def body(t, carry):
            pltpu.make_async_copy(h_hbm.at[pl.ds(0, SUBLANE), pl.ds(0, d)], xbuf.at[s, t], gsem.at[s]).wait()
            return carry
        lax.fori_loop(0, chunks, body, 0)

    def scatter():
        def body(t, carry):
            row0 = pl.multiple_of(s_cur_ref[0, 0, t], SUBLANE)
            pltpu.make_async_copy(acc.at[t], y_hbm.at[pl.ds(row0, SUBLANE)], ssem.at[0]).start()
            return carry
        lax.fori_loop(0, chunks, body, 0)

    def wait_scatter():
        def body(t, carry):
            pltpu.make_async_copy(acc.at[t], y_hbm.at[pl.ds(0, SUBLANE)], ssem.at[0]).wait()
            return carry
        lax.fori_loop(0, chunks, body, 0)

    @pl.when(j == 0)
    def _():
        @pl.when(i == 0)
        def _():
            gather(g_first_ref, 0)
            acc[...] = jnp.zeros_like(acc)
            first_free = nb_ref[0] * (block // SUBLANE)
            n_free = n_rows // SUBLANE - first_free

            def zero_copy(row0):
                return pltpu.make_async_copy(acc.at[0], y_hbm.at[pl.ds(pl.multiple_of(row0, SUBLANE), SUBLANE)],
                                             ssem.at[0])

            def fill(c, carry):
                zero_copy((first_free + c) * SUBLANE).start()
                zero_copy(n_rows + (first_free + c) * SUBLANE).start()
                return carry
            lax.fori_loop(0, n_free, fill, 0)

            def fill_spare(c, carry):
                zero_copy(2 * n_rows + c * SUBLANE).start()
                return carry
            lax.fori_loop(0, chunks, fill_spare, 0)

            def drain(c, carry):
                zero_copy(0).wait()
                return carry
            lax.fori_loop(0, 2 * n_free + chunks, drain, 0)

        @pl.when(i + 1 < n_used)
        def _():
            gather(g_next_ref, 1 - slot)

        @pl.when(valid)
        def _():
            wait_gather(slot)
            xn_ref[...] = _rms(xbuf[slot].reshape(tm, d), g_ref[...]).astype(BF16)

        @pl.when((i >= 1) & (i <= n_used))
        def _():
            wait_scatter()

        @pl.when(valid)
        def _():
            acc[...] = jnp.zeros_like(acc)

    @pl.when(valid)
    def _():
        xn = xn_ref[...]
        a = _dot(xn, wg_ref[0].astype(BF16))
        b = _dot(xn, wu_ref[0].astype(BF16))
        part = _dot((_silu(a) * b).astype(BF16), wo_ref[0].astype(BF16)).reshape(chunks, SUBLANE, d)
        acc[...] += part

        @pl.when(j == nj - 1)
        def _():
            scatter()

            @pl.when(i == n_tiles - 1)
            def _():
                wait_scatter()


def moe_experts(h_ps, g, wi, wo, gather_rows, scatter_rows, tile_expert, n_used, n_blocks_used, block, n_rows,
                tm, tf):
    d = wo.shape[2]
    f = wo.shape[1]
    nj = f // tf
    n_tiles = gather_rows.shape[0]
    chunks = tm // SUBLANE

    def wblock(col_off):
        def index_map(i, j, te, nu, nb):
            jj = jnp.where(i < nu[0], j, nj - 1)
            return (te[i], 0, col_off + jj)
        return index_map

    def woblock(i, j, te, nu, nb):
        return (te[i], jnp.where(i < nu[0], j, nj - 1), 0)

    def smem_tile(index_map):
        return pl.BlockSpec((1, 1, chunks), index_map, memory_space=pltpu.SMEM)

    grid_spec = pltpu.PrefetchScalarGridSpec(
        num_scalar_prefetch=3,
        grid=(n_tiles, nj),
        in_specs=[
            smem_tile(lambda i, j, te, nu, nb: (0, 0, 0)),
            smem_tile(lambda i, j, te, nu, nb: (jnp.minimum(i + 1, n_tiles - 1), 0, 0)),
            smem_tile(lambda i, j, te, nu, nb: (i, 0, 0)),
            pl.BlockSpec(memory_space=pl.ANY),
            pl.BlockSpec((1, d), lambda i, j, te, nu, nb: (0, 0)),
            pl.BlockSpec((1, d, tf), wblock(0)),
            pl.BlockSpec((1, d, tf), wblock(nj)),
            pl.BlockSpec((1, tf, d), woblock),
        ],
        out_specs=pl.BlockSpec(memory_space=pl.ANY),
        scratch_shapes=[
            pltpu.VMEM((2, chunks, SUBLANE, d), F32),
            pltpu.VMEM((chunks, SUBLANE, d), F32),
            pltpu.VMEM((tm, d), BF16),
            pltpu.SemaphoreType.DMA((2,)),
            pltpu.SemaphoreType.DMA((1,)),
        ],
    )
    return pl.pallas_call(
        functools.partial(_experts_body, tm=tm, block=block, n_rows=n_rows),
        out_shape=jax.ShapeDtypeStruct((2 * n_rows + tm, d), F32),
        grid_spec=grid_spec,
        compiler_params=_params("arbitrary", "arbitrary"),
        name="moe_experts",
    )(tile_expert, n_used, n_blocks_used, gather_rows, gather_rows, scatter_rows, h_ps, g.reshape(1, d),
      wi, wi, wo)


def _combine_body(nb_ref, hx_ref, y1_ref, y2_ref, fn_ref, o_ref, *, d, final_norm):
    @pl.when(pl.program_id(0) < nb_ref[0])
    def _():
        hx = hx_ref[...]
        out = hx[:, 0:d] + (hx[:, d:d + 1] * y1_ref[...] + hx[:, d + 1:d + 2] * y2_ref[...])
        if final_norm:
            out = _rms(out, fn_ref[...])
        o_ref[...] = out

    @pl.when(pl.program_id(0) >= nb_ref[0])
    def _():
        o_ref[...] = jnp.zeros_like(o_ref)


def moe_combine(h_ps, y, n_blocks_used, choice_stride, fn, tb, final_norm, name):
    rows, dx = h_ps.shape
    d = y.shape[1]
    off2 = choice_stride // tb

    def used(i, nb):
        return jnp.minimum(i, nb[0] - 1)

    grid_spec = pltpu.PrefetchScalarGridSpec(
        num_scalar_prefetch=1,
        grid=(rows // tb,),
        in_specs=[
            pl.BlockSpec((tb, dx), lambda i, nb: (used(i, nb), 0)),
            pl.BlockSpec((tb, d), lambda i, nb: (used(i, nb), 0)),
            pl.BlockSpec((tb, d), lambda i, nb: (used(i, nb) + off2, 0)),
            pl.BlockSpec((1, d), lambda i, nb: (0, 0)),
        ],
        out_specs=pl.BlockSpec((tb, d), lambda i, nb: (i, 0)),
    )
    return pl.pallas_call(
        functools.partial(_combine_body, d=d, final_norm=final_norm),
        out_shape=jax.ShapeDtypeStruct((rows, d), F32),
        grid_spec=grid_spec,
        compiler_params=_params("arbitrary"),
        name=name,
    )(n_blocks_used, h_ps, y, y, fn.reshape(1, d))


def _unpermute_body(cur_ref, nxt_ref, y_hbm, o_ref, buf, sem, *, tb):
    i = pl.program_id(0)
    n = pl.num_programs(0)
    slot = i % 2
    groups = tb // SUBLANE

    def start(idx_ref, s):
        def body(t, carry):
            for u in range(SUBLANE):
                row = idx_ref[0, 0, t * SUBLANE + u]
                pltpu.make_async_copy(y_hbm.at[pl.ds(row, 1)], buf.at[s, t, pl.ds(u, 1)], sem.at[s]).start()
            return carry
        lax.fori_loop(0, groups, body, 0)

    @pl.when(i == 0)
    def _():
        start(cur_ref, 0)

    @pl.when(i + 1 < n)
    def _():
        start(nxt_ref, 1 - slot)

    def wait_body(t, carry):
        pltpu.make_async_copy(y_hbm.at[pl.ds(0, SUBLANE)], buf.at[slot, t], sem.at[slot]).wait()
        return carry
    lax.fori_loop(0, groups, wait_body, 0)
    o_ref[...] = buf[slot].reshape(o_ref.shape)


def moe_unpermute(y_ps, dest, tb, name):
    rows = dest.shape[0]
    d = y_ps.shape[1]
    n = rows // tb
    idx = dest.reshape(n, 1, tb)

    def smem_tile(index_map):
        return pl.BlockSpec((1, 1, tb), index_map, memory_space=pltpu.SMEM)

    return pl.pallas_call(
        functools.partial(_unpermute_body, tb=tb),
        out_shape=jax.ShapeDtypeStruct((rows, d), F32),
        grid=(n,),
        in_specs=[smem_tile(lambda i: (i, 0, 0)), smem_tile(lambda i: (jnp.minimum(i + 1, n - 1), 0, 0)),
                  pl.BlockSpec(memory_space=pl.ANY)],
        out_specs=pl.BlockSpec((tb, d), lambda i: (i, 0)),
        scratch_shapes=[pltpu.VMEM((2, tb // SUBLANE, SUBLANE, d), F32), pltpu.SemaphoreType.DMA((2,))],
        compiler_params=_params("arbitrary"),
        name=name,
    )(idx, idx, y_ps)


def moe_plan(idx, n_experts, tm, block):
    n_tok, k = idx.shape
    assert k == 2
    n_pairs = n_experts * n_experts
    chunks_per_tile = tm // SUBLANE
    n_rows = -(-(n_tok + n_pairs * (SUBLANE - 1)) // block) * block
    n_chunks = n_rows // SUBLANE

    pair = idx[:, 0] * n_experts + idx[:, 1]
    onehot = (pair[:, None] == jnp.arange(n_pairs, dtype=jnp.int32)[None, :]).astype(jnp.int32)
    csum = jnp.cumsum(onehot, axis=0)
    counts = csum[-1]
    rank = jnp.sum(csum * onehot, axis=1) - 1
    padded = (counts + SUBLANE - 1) // SUBLANE * SUBLANE
    total = jnp.sum(padded)
    last_group = jnp.max(jnp.where(counts > 0, jnp.arange(n_pairs, dtype=jnp.int32), 0))
    padded = padded + jnp.where(jnp.arange(n_pairs) == last_group, (-total) % block, 0)
    group_end = jnp.cumsum(padded)
    group_start = group_end - padded
    dest = jnp.sum(onehot * group_start[None, :], axis=1) + rank
    rows_used = group_end[-1]

    chunk_row = jnp.arange(n_chunks, dtype=jnp.int32) * SUBLANE
    chunk_group = jnp.sum((chunk_row[:, None] >= group_end[None, :]).astype(jnp.int32), axis=1)
    chunk_used = chunk_row < rows_used
    chunk_group = jnp.minimum(chunk_group, n_pairs - 1)
    chunk_experts = jnp.stack([chunk_group // n_experts, chunk_group % n_experts], axis=1)
    experts = jnp.arange(n_experts, dtype=jnp.int32)
    member = ((chunk_experts[:, :, None] == experts[None, None, :]) & chunk_used[:, None, None])
    member = member.astype(jnp.int32)
    member_e = jnp.sum(member, axis=1)
    csum_e = jnp.cumsum(member_e, axis=0)
    counts_e = csum_e[-1]
    tiles_e = (counts_e + chunks_per_tile - 1) // chunks_per_tile
    tile_end = jnp.cumsum(tiles_e)
    start_e = (tile_end - tiles_e) * chunks_per_tile
    pos = jnp.sum(member * (start_e[None, None, :] + csum_e[:, None, :] - 1), axis=2)
    n_tiles = 2 * n_chunks // chunks_per_tile + n_experts
    n_pos = n_tiles * chunks_per_tile
    pos = jnp.where(chunk_used[:, None], pos, n_pos)
    entry = 2 * jnp.arange(n_chunks, dtype=jnp.int32)[:, None] + jnp.arange(2, dtype=jnp.int32)[None, :] + 1
    table = jnp.zeros((n_pos,), jnp.int32).at[pos.reshape(-1)].set(entry.reshape(-1), mode="drop")
    table = table.reshape(n_tiles, chunks_per_tile)
    filled = table > 0
    src_chunk = (table - 1) // 2
    choice = (table - 1) % 2
    spare = 2 * n_rows + jnp.arange(chunks_per_tile, dtype=jnp.int32) * SUBLANE
    gather_rows = jnp.where(filled, src_chunk * SUBLANE, 0)
    scatter_rows = jnp.where(filled, choice * n_rows + src_chunk * SUBLANE, spare[None, :])

    n_used = tile_end[-1]
    tile_ids = jnp.arange(n_tiles, dtype=jnp.int32)
    tile_expert = jnp.sum((tile_ids[:, None] >= tile_end[None, :]).astype(jnp.int32), axis=1)
    last_expert = jnp.sum((n_used - 1 >= tile_end).astype(jnp.int32))
    tile_expert = jnp.where(tile_ids < n_used, tile_expert, last_expert).astype(jnp.int32)
    return (dest.astype(jnp.int32),
            gather_rows.reshape(n_tiles, 1, chunks_per_tile).astype(jnp.int32),
            scatter_rows.reshape(n_tiles, 1, chunks_per_tile).astype(jnp.int32),
            tile_expert, n_used.reshape(1).astype(jnp.int32),
            (rows_used // block).reshape(1).astype(jnp.int32),
            n_rows)


def _pad_lanes(v):
    return jnp.zeros((1, LANE), F32).at[0, :v.shape[0]].set(v)


def kernel(x_prompt, x_sample, state_dn, state_dn_conv, state_pool, cache_mem_k, cache_mem_v, mem_prompt, a_norm1, a_w_in, a_conv_w, a_a_log, a_dt_bias, a_o_norm, a_w_out, a_norm2, a_ffn_wi, a_ffn_wo, b_norm1, b_w_in, b_pool_w, b_pool_scale, b_w_out, b_norm2, b_router, b_moe_wi, b_moe_wo, m_norm, m_w_k, m_w_v, final_norm):
    bp, seq, d = x_prompt.shape
    bs = x_sample.shape[0]
    assert x_sample.shape[1] == 1
    depth = m_norm.shape[0]
    assert depth % 2 == 0, "the output norm is fused into the expert block of the last (odd) layer"
    heads, dk, dv = state_dn.shape[2], state_dn.shape[3], state_dn.shape[4]
    conv_ch = state_dn_conv.shape[-1]
    qk_w, v_w = heads * dk, heads * dv
    assert conv_ch == 2 * qk_w + v_w and dk == LANE and dv == LANE and heads <= LANE
    n_mem, x_heads, x_dh = cache_mem_k.shape[2], cache_mem_k.shape[3], cache_mem_k.shape[4]
    xw = x_heads * x_dh
    pool_w_total = state_pool.shape[-1]
    pool_buf = state_pool.shape[2]
    d_ff = a_ffn_wo.shape[1]

    hp = x_prompt.reshape(bp * seq, d)
    hs = x_sample.reshape(bs, d)

    o1, o2, o3, o4 = conv_ch, conv_ch + v_w, conv_ch + v_w + heads, conv_ch + v_w + 2 * heads
    col_a = {"qkv": 0, "z": conv_ch, "cq": conv_ch + v_w, "b": conv_ch + v_w + xw, "a": conv_ch + v_w + xw + LANE}
    n_a = -(-(col_a["a"] + LANE) // 512) * 512

    dn_p, conv_p, pool_p, mk_all, mv_all = [], [], [], [], []
    dn_s, conv_s, pool_s = [], [], []
    for i in range(depth):
        j = i // 2
        w_kv = jnp.concatenate([m_w_k[i], m_w_v[i]], axis=1).astype(BF16)
        kv, mk, mv = mem_kv(mem_prompt.reshape(bp * n_mem, d), m_norm[i], w_kv, x_heads, x_dh, 256,
                            f"mem_kv_{i}")
        mk_all.append(mk.reshape(bp, n_mem, x_heads, x_dh))
        mv_all.append(mv.reshape(bp, n_mem, x_heads, x_dh))
        if i % 2 == 0:
            w = a_w_in[j]
            w_in = jnp.zeros((d, n_a), F32)
            w_in = w_in.at[:, :o2].set(w[:, :o2])
            w_in = w_in.at[:, col_a["cq"]:col_a["cq"] + xw].set(w[:, o4:])
            w_in = w_in.at[:, col_a["b"]:col_a["b"] + heads].set(w[:, o2:o3])
            w_in = w_in.at[:, col_a["a"]:col_a["a"] + heads].set(w[:, o3:o4])
            a_log, dt_b = _pad_lanes(a_a_log[j]), _pad_lanes(a_dt_bias[j])
            o_norm = a_o_norm[j].reshape(1, dv)

            proj_p = rms_matmul(hp, a_norm1[j], w_in.astype(BF16), 256, f"a_in_p_{i}")
            proj_s = rms_matmul(hs, a_norm1[j], w_in, bs, f"a_in_s_{i}", tn=512)
            mix_p, s_p = deltanet_prompt(proj_p, col_a, a_conv_w[j], a_log, dt_b, o_norm,
                                         bp, seq, heads, dk, dv, rows=DN_STEP_CHUNKS * CHUNK)
            mix_s, s_s, c_s = deltanet_sample(proj_s, col_a, state_dn_conv[j], state_dn[j], a_conv_w[j],
                                              a_log, dt_b, o_norm, heads, dk, dv, nb=SUBLANE)
            xa_p = xattn_prompt(proj_p, col_a["cq"], kv, bp, seq, n_mem, x_heads, x_dh, tq=min(1024, seq))
            xa_s = xattn_sample(proj_s, col_a["cq"], cache_mem_k, cache_mem_v, i, nb=SUBLANE)
            hp = out_proj_ffn(hp, mix_p, xa_p, a_w_out[j].astype(BF16), a_norm2[j], a_ffn_wi[j].astype(BF16),
                              a_ffn_wo[j].astype(BF16), 256, f"a_out_ffn_p_{i}")
            hs = out_proj(hs, mix_s, xa_s, a_w_out[j], bs, f"a_out_s_{i}")
            hs = ffn_dense(hs, a_norm2[j], a_ffn_wi[j], a_ffn_wo[j], bs, f"a_ffn_s_{i}", tf=256)
            dn_p.append(s_p)
            conv_p.append(proj_p.reshape(bp, seq, n_a)[:, seq - (CONV_WIDTH - 1):, :conv_ch])
            dn_s.append(s_s)
            conv_s.append(c_s)
        else:
            ps = b_pool_scale[j].reshape(1, pool_w_total)
            last = i == depth - 1

            proj_p = rms_matmul(hp, b_norm1[j], b_w_in[j].astype(BF16), 512, f"b_in_p_{i}")
            proj_s = rms_matmul(hs, b_norm1[j], b_w_in[j], bs, f"b_in_s_{i}")
            mix_p = pool_prompt(proj_p, b_pool_w[j].astype(BF16), ps, bp, seq, tt=512)
            mix_s, buf_s = pool_sample(proj_s, state_pool[j], b_pool_w[j], ps)
            xa_p = xattn_prompt(proj_p, pool_w_total, kv, bp, seq, n_mem, x_heads, x_dh, tq=min(1024, seq))
            xa_s = xattn_sample(proj_s, pool_w_total, cache_mem_k, cache_mem_v, i, nb=SUBLANE)
            n_p = bp * seq
            h_all = out_proj(hp, mix_p, xa_p, b_w_out[j].astype(BF16), 512, f"b_out_p_{i}", out_rows=n_p + bs)
            h_all = out_proj(hs, mix_s, xa_s, b_w_out[j], bs, f"b_out_s_{i}", out_rows=n_p + bs,
                             into=h_all, row_offset=n_p)
            n_experts = b_router.shape[-1]
            r_pad = jnp.zeros((d, LANE), F32).at[:, :n_experts].set(b_router[j])
            idx_p, gate_p = moe_route(h_all, 0, n_p, b_norm2[j], r_pad.astype(BF16), 512, n_experts,
                                      f"b_route_p_{i}")
            idx_s, gate_s = moe_route(h_all, n_p, bs, b_norm2[j], r_pad, bs, n_experts, f"b_route_s_{i}")
            idx = jnp.concatenate([idx_p[:, :TOP_K], idx_s[:, :TOP_K]], axis=0)
            (dest, gather_rows, scatter_rows, tile_expert, n_used, n_blocks_used, n_rows) = moe_plan(
                idx, n_experts, MOE_TILE, MOE_ROW_BLOCK)
            h_ps = jnp.zeros((n_rows, d + LANE), F32)
            h_ps = moe_dispatch(h_all, 0, n_p, gate_p, dest[:n_p], h_ps, 512, f"b_dispatch_p_{i}")
            h_ps = moe_dispatch(h_all, n_p, bs, gate_s, dest[n_p:], h_ps, bs, f"b_dispatch_s_{i}")
            y = moe_experts(h_ps, b_norm2[j], b_moe_wi[j], b_moe_wo[j], gather_rows, scatter_rows, tile_expert,
                            n_used, n_blocks_used, MOE_ROW_BLOCK, n_rows, MOE_TILE, MOE_FF_TILE)
            y_ps = moe_combine(h_ps, y, n_blocks_used, n_rows, final_norm, MOE_ROW_BLOCK, last,
                               f"b_combine_{i}")
            hp = moe_unpermute(y_ps, dest[:n_p], 512, f"b_unpermute_p_{i}")
            hs = moe_unpermute(y_ps, dest[n_p:], bs, f"b_unpermute_s_{i}")
            pool_p.append(proj_p.reshape(bp, seq, -1)[:, seq - pool_buf:, :pool_w_total])
            pool_s.append(buf_s)

    y_prompt = hp.reshape(bp, seq, d)
    y_sample = hs.reshape(bs, 1, d)
    return (y_prompt, y_sample, jnp.stack(dn_p), jnp.stack(conv_p), jnp.stack(pool_p),
            jnp.stack(mk_all), jnp.stack(mv_all), jnp.stack(dn_s), jnp.stack(conv_s), jnp.stack(pool_s))
```

```python
import functools

import jax
import jax.numpy as jnp
from jax import lax
from jax.experimental import pallas as pl
from jax.experimental.pallas import tpu as pltpu

F32 = jnp.float32
BF16 = jnp.bfloat16
HIGHEST = lax.Precision.HIGHEST

EPS = 1e-6
CHUNK = 64
CONV_WIDTH = 4
POOL_WINDOWS = (2, 4, 8, 16)
PAST_LEN = 16384
TOP_K = 2

LANE = 128
SUBLANE = 8
VMEM_LIMIT_BYTES = 56 * 2**20

MOE_TILE = 1024
MOE_FF_TILE = 512
MOE_ROW_BLOCK = 512
DN_STEP_CHUNKS = 4
INV_BLOCK = 16
POOL_HALO = 32


def _params(*sem):
    return pltpu.CompilerParams(dimension_semantics=sem, vmem_limit_bytes=VMEM_LIMIT_BYTES)


def _resident(shape):
    nd = len(shape)
    return pl.BlockSpec(shape, lambda *_: (0,) * nd, pipeline_mode=pl.Buffered(1))


def _rms(x, w):
    return x * lax.rsqrt(jnp.mean(x * x, axis=-1, keepdims=True) + EPS) * w


def _silu(x):
    half = 0.5 * x
    return half + half * jnp.tanh(half)


def _dot(a, b):
    return jnp.dot(a, b, preferred_element_type=F32)


def _wdot(a, w):
    if w.dtype == F32:
        return jnp.dot(a, w, precision=HIGHEST, preferred_element_type=F32)
    return jnp.dot(a.astype(BF16), w, preferred_element_type=F32)


def _bdot(a, b):
    return lax.dot_general(a, b, (((2,), (1,)), ((0,), (0,))), preferred_element_type=F32)


def _bdot_nt(a, b):
    return lax.dot_general(a, b, (((2,), (2,)), ((0,), (0,))), preferred_element_type=F32)


def _bdot_tn(a, b):
    return lax.dot_general(a, b, (((1,), (1,)), ((0,), (0,))), preferred_element_type=F32)


def _rms_matmul_body(x_ref, g_ref, w_ref, o_ref):
    xn = _rms(x_ref[...], g_ref[...])
    o_ref[...] = _wdot(xn, w_ref[...])


def _wspec(shape, index_map, steps):
    if steps == 1:
        return pl.BlockSpec(shape, index_map, pipeline_mode=pl.Buffered(1))
    return pl.BlockSpec(shape, index_map)


def rms_matmul(x, g, w, tm, name, tn=None):
    m, d = x.shape
    n = w.shape[1]
    tn = n if tn is None else tn
    nj = n // tn
    return pl.pallas_call(
        _rms_matmul_body,
        out_shape=jax.ShapeDtypeStruct((m, n), F32),
        grid=(m // tm, nj),
        in_specs=[pl.BlockSpec((tm, d), lambda i, j: (i, 0)), _resident((1, d)),
                  _wspec((d, tn), lambda i, j: (0, j), nj)],
        out_specs=pl.BlockSpec((tm, tn), lambda i, j: (i, j)),
        compiler_params=_params("parallel", "arbitrary"),
        name=name,
    )(x, g.reshape(1, d), w)


def _out_proj_body(x_ref, a1_ref, a2_ref, w1_ref, w2_ref, *rest, n_main):
    o_ref = rest[-1]

    @pl.when(pl.program_id(0) < n_main)
    def _():
        acc = _wdot(a1_ref[...], w1_ref[...]) + _wdot(a2_ref[...], w2_ref[...])
        o_ref[...] = x_ref[...] + acc

    @pl.when(pl.program_id(0) >= n_main)
    def _():
        o_ref[...] = jnp.zeros_like(o_ref)


def out_proj(x, a1, a2, w, tm, name, out_rows=None, into=None, row_offset=0):
    m, d = x.shape
    k1, k2 = a1.shape[1], a2.shape[1]
    assert w.shape[0] == k1 + k2 and k1 % k2 == 0 and row_offset % tm == 0
    out_rows = m if out_rows is None else out_rows
    off = row_offset // tm
    n_main = m // tm
    n_extra = 0 if (into is not None or out_rows == m) else 1
    assert out_rows - m <= n_extra * tm or into is not None
    last = n_main - 1
    in_specs = [
        pl.BlockSpec((tm, d), lambda i: (jnp.minimum(i, last), 0)),
        pl.BlockSpec((tm, k1), lambda i: (jnp.minimum(i, last), 0)),
        pl.BlockSpec((tm, k2), lambda i: (jnp.minimum(i, last), 0)),
        pl.BlockSpec((k1, d), lambda i: (0, 0), pipeline_mode=pl.Buffered(1)),
        pl.BlockSpec((k2, d), lambda i: (k1 // k2, 0), pipeline_mode=pl.Buffered(1)),
    ]
    args = [x, a1, a2, w, w]
    aliases = {}
    if into is not None:
        in_specs.append(pl.BlockSpec(memory_space=pl.ANY))
        args.append(into)
        aliases = {len(args) - 1: 0}
    return pl.pallas_call(
        functools.partial(_out_proj_body, n_main=n_main),
        out_shape=jax.ShapeDtypeStruct((out_rows, d), F32),
        grid=(n_main + n_extra,),
        in_specs=in_specs,
        out_specs=pl.BlockSpec((tm, d), lambda i: (i + off, 0)),
        input_output_aliases=aliases,
        compiler_params=_params("parallel"),
        name=name,
    )(*args)


def _ffn_body(h_ref, g_ref, wg_ref, wu_ref, wo_ref, o_ref, xn_ref):
    @pl.when(pl.program_id(1) == 0)
    def _():
        h = h_ref[...]
        xn_ref[...] = _rms(h, g_ref[...])
        o_ref[...] = h

    xn = xn_ref[...]
    a = _wdot(xn, wg_ref[...])
    b = _wdot(xn, wu_ref[...])
    o_ref[...] += _wdot(_silu(a) * b, wo_ref[...])


def ffn_dense(h, g, wi, wo, tm, name, tf=None):
    m, d = h.shape
    f = wo.shape[0]
    tf = f if tf is None else tf
    nj = f // tf
    return pl.pallas_call(
        _ffn_body,
        out_shape=jax.ShapeDtypeStruct((m, d), F32),
        grid=(m // tm, nj),
        in_specs=[
            pl.BlockSpec((tm, d), lambda i, j: (i, 0)),
            _resident((1, d)),
            _wspec((d, tf), lambda i, j: (0, j), nj),
            _wspec((d, tf), lambda i, j: (0, nj + j), nj),
            _wspec((tf, d), lambda i, j: (j, 0), nj),
        ],
        out_specs=pl.BlockSpec((tm, d), lambda i, j: (i, 0)),
        scratch_shapes=[pltpu.VMEM((tm, d), F32)],
        compiler_params=_params("parallel", "arbitrary"),
        name=name,
    )(h, g.reshape(1, d), wi, wi, wo)


def _mem_kv_body(x_ref, g_ref, w_ref, kv_ref, k_ref, v_ref, *, heads, dh):
    xw = heads * dh
    res = _wdot(_rms(x_ref[...], g_ref[...]), w_ref[...])
    kv_ref[...] = res
    for h in range(heads):
        k_ref[:, h, :] = res[:, h * dh:(h + 1) * dh]
        v_ref[:, h, :] = res[:, xw + h * dh:xw + (h + 1) * dh]


def mem_kv(mem, g, w_kv, heads, dh, tm, name):
    m, d = mem.shape
    xw = heads * dh
    head_shape = jax.ShapeDtypeStruct((m, heads, dh), F32)
    head_spec = pl.BlockSpec((tm, heads, dh), lambda i: (i, 0, 0))
    return pl.pallas_call(
        functools.partial(_mem_kv_body, heads=heads, dh=dh),
        out_shape=(jax.ShapeDtypeStruct((m, 2 * xw), F32), head_shape, head_shape),
        grid=(m // tm,),
        in_specs=[pl.BlockSpec((tm, d), lambda i: (i, 0)), _resident((1, d)), _resident((d, 2 * xw))],
        out_specs=(pl.BlockSpec((tm, 2 * xw), lambda i: (i, 0)), head_spec, head_spec),
        compiler_params=_params("parallel"),
        name=name,
    )(mem, g.reshape(1, d), w_kv)


def _out_proj_ffn_body(x_ref, a1_ref, a2_ref, w1_ref, w2_ref, g_ref, wg_ref, wu_ref, wo_ref, o_ref):
    h = x_ref[...] + (_wdot(a1_ref[...], w1_ref[...]) + _wdot(a2_ref[...], w2_ref[...]))
    xn = _rms(h, g_ref[...])
    a = _wdot(xn, wg_ref[...])
    b = _wdot(xn, wu_ref[...])
    o_ref[...] = h + _wdot(_silu(a) * b, wo_ref[...])


def out_proj_ffn(x, a1, a2, w, g, wi, wo, tm, name):
    m, d = x.shape
    k1, k2 = a1.shape[1], a2.shape[1]
    f = wo.shape[0]
    assert w.shape[0] == k1 + k2 and k1 % k2 == 0
    return pl.pallas_call(
        _out_proj_ffn_body,
        out_shape=jax.ShapeDtypeStruct((m, d), F32),
        grid=(m // tm,),
        in_specs=[
            pl.BlockSpec((tm, d), lambda i: (i, 0)),
            pl.BlockSpec((tm, k1), lambda i: (i, 0)),
            pl.BlockSpec((tm, k2), lambda i: (i, 0)),
            pl.BlockSpec((k1, d), lambda i: (0, 0), pipeline_mode=pl.Buffered(1)),
            pl.BlockSpec((k2, d), lambda i: (k1 // k2, 0), pipeline_mode=pl.Buffered(1)),
            _resident((1, d)),
            pl.BlockSpec((d, f), lambda i: (0, 0), pipeline_mode=pl.Buffered(1)),
            pl.BlockSpec((d, f), lambda i: (0, 1), pipeline_mode=pl.Buffered(1)),
            _resident((f, d)),
        ],
        out_specs=pl.BlockSpec((tm, d), lambda i: (i, 0)),
        compiler_params=_params("parallel"),
        name=name,
    )(x, a1, a2, w, w, g.reshape(1, d), wi, wi, wo)


def _pair_blockdiag(y, left):
    yb = y.astype(BF16)
    zero = jnp.zeros_like(yb)
    return jnp.concatenate([jnp.where(left, yb, zero), jnp.where(left, zero, yb)], axis=1)


def _pair_mm(x, y, left):
    return _bdot(x.astype(BF16), _pair_blockdiag(y, left))


def _unit_lower_inverse(mx, eye, blk, left):
    mm = functools.partial(_pair_mm, left=left)
    md = jnp.where(blk, mx, 0.0)
    c = mx - md
    p = eye - md
    m2 = mm(md, md)
    p = p + mm(p, m2)
    m4 = mm(m2, m2)
    p = p + mm(p, m4)
    m8 = mm(m4, m4)
    td = p + mm(p, m8)
    n = mm(td, c)
    n2 = mm(n, n)
    q = eye - n
    q = q + mm(q, n2)
    return mm(q, td)


def _dn_prompt_body(qkv_ref, z_ref, bl_ref, al_ref, cw_ref, alog_ref, dtb_ref, on_ref,
                    dn_ref, sfin_ref, s_ref, xbuf_ref, act_ref, *, rows, heads, dk, dv):
    c = pl.program_id(1)
    wq = heads * dk

    @pl.when(c == 0)
    def _():
        s_ref[...] = jnp.zeros_like(s_ref)
        xbuf_ref[0:SUBLANE, :] = jnp.zeros((SUBLANE, xbuf_ref.shape[1]), F32)

    xbuf_ref[SUBLANE:SUBLANE + rows, :] = qkv_ref[...]
    first = SUBLANE - (CONV_WIDTH - 1)
    y = xbuf_ref[pl.ds(first, rows), :] * cw_ref[0:1, :]
    for j in range(1, CONV_WIDTH):
        y = y + xbuf_ref[pl.ds(first + j, rows), :] * cw_ref[j:j + 1, :]
    act_ref[...] = _silu(y)
    xbuf_ref[0:SUBLANE, :] = xbuf_ref[rows:rows + SUBLANE, :]

    pairs = heads // 2
    ri = lax.broadcasted_iota(jnp.int32, (CHUNK, 2 * CHUNK), 0)
    lane = lax.broadcasted_iota(jnp.int32, (CHUNK, 2 * CHUNK), 1)
    cj = lane % CHUNK
    left = (lane < CHUNK)[None]
    incl = (ri >= cj)[None]
    strict = (ri > cj)[None]
    blk = ((ri // INV_BLOCK) == (cj // INV_BLOCK))[None]
    eye = (ri == cj).astype(F32)[None]
    tr = lax.broadcasted_iota(jnp.int32, (CHUNK, CHUNK), 0)
    tc = lax.broadcasted_iota(jnp.int32, (CHUNK, CHUNK), 1)
    tri = (tr >= tc).astype(F32)

    n_chunks = rows // CHUNK
    zero = jnp.zeros((CHUNK, dk), BF16)

    def chunk_rows(cc):
        return slice(cc * CHUNK, (cc + 1) * CHUNK)

    def per_head(fn):
        return jnp.stack([fn(chunk_rows(cc), h) for cc in range(n_chunks) for h in range(heads)], axis=0)

    def per_pair(fn):
        return jnp.stack([fn(cc, p) for cc in range(n_chunks) for p in range(pairs)], axis=0)

    q3 = per_head(lambda rs, h: act_ref[rs, h * dk:(h + 1) * dk])
    k3 = per_head(lambda rs, h: act_ref[rs, wq + h * dk:wq + (h + 1) * dk])
    v3 = per_head(lambda rs, h: act_ref[rs, 2 * wq + h * dv:2 * wq + (h + 1) * dv])
    qn = q3 * lax.rsqrt(jnp.sum(q3 * q3, axis=-1, keepdims=True) + EPS) * (dk ** -0.5)
    kn = k3 * lax.rsqrt(jnp.sum(k3 * k3, axis=-1, keepdims=True) + EPS)

    beta = jax.nn.sigmoid(bl_ref[...])
    log_decay = -jnp.exp(alog_ref[...]) * jax.nn.softplus(al_ref[...] + dtb_ref[...])
    gcum = [jnp.dot(tri, log_decay[chunk_rows(cc)], precision=HIGHEST, preferred_element_type=F32)
            for cc in range(n_chunks)]
    gcum_t = [jnp.concatenate([gc_, pltpu.roll(gc_, LANE - 1, 1)], axis=0).T for gc_ in gcum]

    def pair_cols(x_of):
        return per_pair(lambda cc, p: jnp.where(left[0], x_of(cc)[:, 2 * p:2 * p + 1],
                                                x_of(cc)[:, 2 * p + 1:2 * p + 2]))

    gc2 = pair_cols(lambda cc: gcum[cc])
    bc2 = pair_cols(lambda cc: beta[chunk_rows(cc)])
    gr2 = per_pair(lambda cc, p: gcum_t[cc][2 * p:2 * p + 1, :])
    decay = jnp.where(incl, jnp.exp(gc2 - gr2), 0.0)

    kb = kn.astype(BF16)
    qb = qn.astype(BF16)

    def pair_of(x, cc, p):
        return x[cc * heads + 2 * p], x[cc * heads + 2 * p + 1]

    k_lhs = per_pair(lambda cc, p: jnp.concatenate(pair_of(kb, cc, p), axis=1))
    q_lhs = per_pair(lambda cc, p: jnp.concatenate(pair_of(qb, cc, p), axis=1))
    k_bd = per_pair(lambda cc, p: jnp.concatenate(
        [jnp.concatenate([pair_of(kb, cc, p)[0], zero], axis=1),
         jnp.concatenate([zero, pair_of(kb, cc, p)[1]], axis=1)], axis=0))
    kk = _bdot_nt(k_lhs, k_bd)
    mx = jnp.where(strict, bc2 * kk * decay, 0.0)
    tinv = _unit_lower_inverse(mx, eye, blk, left)

    gcum_rows = jnp.concatenate(gcum, axis=0)
    gc = per_head(lambda rs, h: gcum_rows[rs, h:h + 1])
    bc = per_head(lambda rs, h: beta[rs, h:h + 1])
    eg = jnp.exp(gc)
    rhs = jnp.concatenate([v3 * bc, kn * (bc * eg)], axis=-1)
    sol = _bdot(_pair_blockdiag(tinv, left), rhs.astype(BF16).reshape(n_chunks * pairs, 2 * CHUNK, dv + dk))
    sol = sol.reshape(n_chunks * heads, CHUNK, dv + dk)
    w_val = sol[..., :dv]
    k_cd = sol[..., dv:].astype(BF16)
    qk_bd = _pair_blockdiag(_bdot_nt(q_lhs, k_bd) * decay, left)
    q_dec = (qn * eg).astype(BF16)
    g_last = gc[:, CHUNK - 1:CHUNK, :]
    k_tail = (kn * jnp.exp(g_last - gc)).astype(BF16)
    c_decay = jnp.exp(g_last)

    for cc in range(n_chunks):
        hs = slice(cc * heads, (cc + 1) * heads)
        ps = slice(cc * pairs, (cc + 1) * pairs)
        rs = chunk_rows(cc)
        s = s_ref[...]
        sb = s.astype(BF16)
        u = w_val[hs] - _bdot(k_cd[hs], sb)
        ub = u.astype(BF16)
        intra = _bdot(qk_bd[ps], ub.reshape(pairs, 2 * CHUNK, dv))
        o = _bdot(q_dec[hs], sb) + intra.reshape(heads, CHUNK, dv)
        s_ref[...] = s * c_decay[hs] + _bdot_tn(k_tail[hs], ub)

        o = o * lax.rsqrt(jnp.mean(o * o, axis=-1, keepdims=True) + EPS) * on_ref[...]
        for h in range(heads):
            z = z_ref[rs, h * dv:(h + 1) * dv]
            dn_ref[rs, h * dv:(h + 1) * dv] = o[h] * _silu(z)

    @pl.when(c == pl.num_programs(1) - 1)
    def _():
        sfin_ref[0] = s_ref[...]


def deltanet_prompt(proj, col, conv_w, a_log, dt_bias, o_norm, batch, seq, heads, dk, dv, rows):
    conv_ch = 2 * heads * dk + heads * dv
    zw = heads * dv
    steps = seq // rows

    def row_map(width_off):
        return lambda b, c: (b * steps + c, width_off)

    body = functools.partial(_dn_prompt_body, rows=rows, heads=heads, dk=dk, dv=dv)
    return pl.pallas_call(
        body,
        out_shape=(jax.ShapeDtypeStruct((batch * seq, zw), F32),
                   jax.ShapeDtypeStruct((batch, heads, dk, dv), F32)),
        grid=(batch, steps),
        in_specs=[
            pl.BlockSpec((rows, conv_ch), row_map(col["qkv"] // conv_ch)),
            pl.BlockSpec((rows, zw), row_map(col["z"] // zw)),
            pl.BlockSpec((rows, LANE), row_map(col["b"] // LANE)),
            pl.BlockSpec((rows, LANE), row_map(col["a"] // LANE)),
            _resident((CONV_WIDTH, conv_ch)),
            _resident((1, LANE)),
            _resident((1, LANE)),
            _resident((1, dv)),
        ],
        out_specs=(pl.BlockSpec((rows, zw), lambda b, c: (b * steps + c, 0)),
                   pl.BlockSpec((1, heads, dk, dv), lambda b, c: (b, 0, 0, 0))),
        scratch_shapes=[
            pltpu.VMEM((heads, dk, dv), F32),
            pltpu.VMEM((rows + SUBLANE, conv_ch), F32),
            pltpu.VMEM((rows, conv_ch), F32),
        ],
        compiler_params=_params("parallel", "arbitrary"),
        name="deltanet_prompt",
    )(proj, proj, proj, proj, conv_w, a_log, dt_bias, o_norm)


def _dn_sample_body(qkv_ref, z_ref, bl_ref, al_ref, cbuf_ref, st_ref, cw_ref, alog_ref, dtb_ref, on_ref,
                    dn_ref, sto_ref, cbo_ref, act_ref, *, nb, heads, dk, dv):
    wq = heads * dk
    new = qkv_ref[...]
    y = cbuf_ref[:, 0, :] * cw_ref[0:1, :]
    for j in range(1, CONV_WIDTH - 1):
        y = y + cbuf_ref[:, j, :] * cw_ref[j:j + 1, :]
    y = y + new * cw_ref[CONV_WIDTH - 1:CONV_WIDTH, :]
    act_ref[...] = _silu(y)
    for j in range(CONV_WIDTH - 2):
        cbo_ref[:, j, :] = cbuf_ref[:, j + 1, :]
    cbo_ref[:, CONV_WIDTH - 2, :] = new

    beta = jax.nn.sigmoid(bl_ref[...])
    g = -jnp.exp(alog_ref[...]) * jax.nn.softplus(al_ref[...] + dtb_ref[...])
    dec = jnp.exp(g)

    q_t, k_t = [], []
    for h in range(heads):
        q = act_ref[:, h * dk:(h + 1) * dk]
        k = act_ref[:, wq + h * dk:wq + (h + 1) * dk]
        qn = q * lax.rsqrt(jnp.sum(q * q, axis=-1, keepdims=True) + EPS) * (dk ** -0.5)
        kn = k * lax.rsqrt(jnp.sum(k * k, axis=-1, keepdims=True) + EPS)
        q_t.append(qn.T)
        k_t.append(kn.T)

    def per_head(fn):
        return jnp.stack([fn(h) for h in range(heads)], axis=0)

    for i in range(nb):
        row = slice(i, i + 1)
        kcol = per_head(lambda h: k_t[h][:, i:i + 1])
        qcol = per_head(lambda h: q_t[h][:, i:i + 1])
        v = per_head(lambda h: act_ref[row, 2 * wq + h * dv:2 * wq + (h + 1) * dv])
        z = per_head(lambda h: z_ref[row, h * dv:(h + 1) * dv])
        b_i = per_head(lambda h: beta[row, h:h + 1])
        d_i = per_head(lambda h: dec[row, h:h + 1])
        s = st_ref[i] * d_i
        u = b_i * (v - jnp.sum(kcol * s, axis=1, keepdims=True))
        s = s + kcol * u
        o = jnp.sum(qcol * s, axis=1, keepdims=True)
        sto_ref[i] = s
        o = o * lax.rsqrt(jnp.mean(o * o, axis=-1, keepdims=True) + EPS) * on_ref[...]
        o = o * _silu(z)
        for h in range(heads):
            dn_ref[row, h * dv:(h + 1) * dv] = o[h]


def deltanet_sample(proj, col, conv_buf, state, conv_w, a_log, dt_bias, o_norm, heads, dk, dv, nb):
    n_seq = proj.shape[0]
    conv_ch = 2 * heads * dk + heads * dv
    zw = heads * dv
    body = functools.partial(_dn_sample_body, nb=nb, heads=heads, dk=dk, dv=dv)
    return pl.pallas_call(
        body,
        out_shape=(jax.ShapeDtypeStruct((n_seq, zw), F32),
                   jax.ShapeDtypeStruct(state.shape, F32),
                   jax.ShapeDtypeStruct(conv_buf.shape, F32)),
        grid=(n_seq // nb,),
        in_specs=[
            pl.BlockSpec((nb, conv_ch), lambda i: (i, col["qkv"] // conv_ch)),
            pl.BlockSpec((nb, zw), lambda i: (i, col["z"] // zw)),
            pl.BlockSpec((nb, LANE), lambda i: (i, col["b"] // LANE)),
            pl.BlockSpec((nb, LANE), lambda i: (i, col["a"] // LANE)),
            pl.BlockSpec((nb, CONV_WIDTH - 1, conv_ch), lambda i: (i, 0, 0)),
            pl.BlockSpec((nb, heads, dk, dv), lambda i: (i, 0, 0, 0)),
            _resident((CONV_WIDTH, conv_ch)),
            _resident((1, LANE)),
            _resident((1, LANE)),
            _resident((1, dv)),
        ],
        out_specs=(pl.BlockSpec((nb, zw), lambda i: (i, 0)),
                   pl.BlockSpec((nb, heads, dk, dv), lambda i: (i, 0, 0, 0)),
                   pl.BlockSpec((nb, CONV_WIDTH - 1, conv_ch), lambda i: (i, 0, 0))),
        scratch_shapes=[
            pltpu.VMEM((nb, conv_ch), F32),
        ],
        compiler_params=_params("parallel"),
        name="deltanet_sample",
    )(proj, proj, proj, proj, conv_buf, state, conv_w, a_log, dt_bias, o_norm)


def _xattn_prompt_body(q_ref, k_ref, v_ref, o_ref, *, heads, dh):
    for h in range(heads):
        sl = slice(h * dh, (h + 1) * dh)
        q = q_ref[:, sl].astype(BF16)
        k = k_ref[:, sl].astype(BF16)
        v = v_ref[:, sl].astype(BF16)
        s = lax.dot_general(q, k, (((1,), (1,)), ((), ())), preferred_element_type=F32) * (dh ** -0.5)
        e = jnp.exp(s - jnp.max(s, axis=-1, keepdims=True))
        p = e / jnp.sum(e, axis=-1, keepdims=True)
        o_ref[:, sl] = _dot(p.astype(BF16), v)


def xattn_prompt(proj, cq_col, kv, batch, seq, n_mem, heads, dh, tq):
    xw = heads * dh
    steps = seq // tq
    body = functools.partial(_xattn_prompt_body, heads=heads, dh=dh)
    return pl.pallas_call(
        body,
        out_shape=jax.ShapeDtypeStruct((batch * seq, xw), F32),
        grid=(batch, steps),
        in_specs=[
            pl.BlockSpec((tq, xw), lambda b, i: (b * steps + i, cq_col // xw)),
            pl.BlockSpec((n_mem, xw), lambda b, i: (b, 0)),
            pl.BlockSpec((n_mem, xw), lambda b, i: (b, 1)),
        ],
        out_specs=pl.BlockSpec((tq, xw), lambda b, i: (b * steps + i, 0)),
        compiler_params=_params("parallel", "parallel"),
        name="xattn_prompt",
    )(proj, kv, kv)


def _xattn_sample_body(q_ref, k_ref, v_ref, o_ref, *, nb, heads, dh):
    rep = SUBLANE // heads
    n_t = k_ref.shape[2] // SUBLANE

    def over_groups(x, op):
        out = x
        for t in range(1, rep):
            out = op(out, pltpu.roll(x, t * heads, 0))
        return out

    for i in range(nb):
        row = slice(i, i + 1)
        q8 = jnp.concatenate([q_ref[row, h * dh:(h + 1) * dh] for h in range(heads)] * rep, axis=0)
        k3 = k_ref[0, i].reshape(n_t, SUBLANE, dh)
        v3 = v_ref[0, i].reshape(n_t, SUBLANE, dh)
        s = jnp.sum(k3 * q8[None], axis=-1, keepdims=True) * (dh ** -0.5)
        m = over_groups(jnp.max(s, axis=0), jnp.maximum)
        e = jnp.exp(s - m[None])
        den = over_groups(jnp.sum(e, axis=0), jnp.add)
        p = e / den[None]
        acc = over_groups(jnp.sum(p * v3, axis=0), jnp.add)
        for h in range(heads):
            o_ref[row, h * dh:(h + 1) * dh] = acc[h:h + 1, :]


def xattn_sample(proj, cq_col, cache_k, cache_v, layer, nb):
    n_layers, n_seq, n_mem, heads, dh = cache_k.shape
    assert SUBLANE % heads == 0 and (n_mem * heads) % SUBLANE == 0
    xw = heads * dh
    body = functools.partial(_xattn_sample_body, nb=nb, heads=heads, dh=dh)
    cache_spec = pl.BlockSpec((1, nb, n_mem * heads, dh), lambda i: (layer, i, 0, 0))
    flat = (n_layers, n_seq, n_mem * heads, dh)
    return pl.pallas_call(
        body,
        out_shape=jax.ShapeDtypeStruct((n_seq, xw), F32),
        grid=(n_seq // nb,),
        in_specs=[pl.BlockSpec((nb, xw), lambda i: (i, cq_col // xw)), cache_spec, cache_spec],
        out_specs=pl.BlockSpec((nb, xw), lambda i: (i, 0)),
        compiler_params=_params("parallel"),
        name="xattn_sample",
    )(proj, cache_k.reshape(flat), cache_v.reshape(flat))


def _pool_mix(sums, cnts, u_of, pw_ref, ps_ref, o_ref, gw):
    for gi in range(len(POOL_WINDOWS)):
        cols = slice(gi * gw, (gi + 1) * gw)
        d = sums[gi] / cnts[gi] - u_of(cols)
        o_ref[:, cols] = _wdot(d, pw_ref[gi]) * ps_ref[:, cols]


def _pool_prompt_body(u_ref, pw_ref, ps_ref, o_ref, xbuf, s2, s4, s8, *, tt, gw):
    t = pl.program_id(1)
    n = POOL_HALO + tt

    @pl.when(t == 0)
    def _():
        xbuf[0:POOL_HALO, :] = jnp.zeros((POOL_HALO, xbuf.shape[1]), F32)

    xbuf[POOL_HALO:n, :] = u_ref[...]
    s2[8:n, :] = xbuf[8:n, :] + xbuf[7:n - 1, :]
    s4[16:n, :] = s2[16:n, gw:] + s2[14:n - 2, gw:]
    s8[24:n, :] = s4[24:n, gw:] + s4[20:n - 4, gw:]
    s16 = s8[32:n, gw:] + s8[24:n - 8, gw:]
    sums = [s2[POOL_HALO:n, 0:gw], s4[POOL_HALO:n, 0:gw], s8[POOL_HALO:n, 0:gw], s16]
    pos = t * tt + lax.broadcasted_iota(jnp.int32, (tt, 1), 0)
    cnts = [jnp.minimum(pos + 1, w).astype(F32) for w in POOL_WINDOWS]
    _pool_mix(sums, cnts, lambda cols: u_ref[:, cols], pw_ref, ps_ref, o_ref, gw)
    xbuf[0:POOL_HALO, :] = xbuf[tt:n, :]


def pool_prompt(proj, pool_w, pool_scale, batch, seq, tt):
    pw_total = pool_scale.shape[-1]
    gw = pw_total // len(POOL_WINDOWS)
    steps = seq // tt
    n = POOL_HALO + tt
    body = functools.partial(_pool_prompt_body, tt=tt, gw=gw)
    return pl.pallas_call(
        body,
        out_shape=jax.ShapeDtypeStruct((batch * seq, pw_total), F32),
        grid=(batch, steps),
        in_specs=[
            pl.BlockSpec((tt, pw_total), lambda b, t: (b * steps + t, 0)),
            _resident(pool_w.shape),
            _resident((1, pw_total)),
        ],
        out_specs=pl.BlockSpec((tt, pw_total), lambda b, t: (b * steps + t, 0)),
        scratch_shapes=[
            pltpu.VMEM((n, pw_total), F32),
            pltpu.VMEM((n, pw_total), F32),
            pltpu.VMEM((n, pw_total - gw), F32),
            pltpu.VMEM((n, pw_total - 2 * gw), F32),
        ],
        compiler_params=_params("parallel", "arbitrary"),
        name="pool_prompt",
    )(proj, pool_w, pool_scale)


def _pool_sample_body(u_ref, st_ref, pw_ref, ps_ref, o_ref, sto_ref, *, gw, buf):
    pw_total = gw * len(POOL_WINDOWS)
    new = u_ref[:, 0:pw_total]
    sums, cnts = [], []
    for gi, w in enumerate(POOL_WINDOWS):
        cols = slice(gi * gw, (gi + 1) * gw)
        s = new[:, cols]
        for i in range(1, w):
            s = s + st_ref[:, buf - i, cols]
        sums.append(s)
        cnts.append(float(min(PAST_LEN + 1, w)))
    _pool_mix(sums, cnts, lambda cols: new[:, cols], pw_ref, ps_ref, o_ref, gw)
    sto_ref[:, 0:buf - 1, :] = st_ref[:, 1:buf, :]
    sto_ref[:, buf - 1, :] = new


def pool_sample(proj, state, pool_w, pool_scale):
    n_seq, buf, pw_total = state.shape
    gw = pw_total // len(POOL_WINDOWS)
    body = functools.partial(_pool_sample_body, gw=gw, buf=buf)
    return pl.pallas_call(
        body,
        out_shape=(jax.ShapeDtypeStruct((n_seq, pw_total), F32),
                   jax.ShapeDtypeStruct(state.shape, F32)),
        compiler_params=pltpu.CompilerParams(vmem_limit_bytes=VMEM_LIMIT_BYTES),
        name="pool_sample",
    )(proj, state, pool_w, pool_scale)


def _route_body(h_ref, g_ref, r_ref, idx_ref, gate_ref, *, n_experts):
    xn = _rms(h_ref[...], g_ref[...])
    logits = _wdot(xn, r_ref[...])
    lane = lax.broadcasted_iota(jnp.int32, logits.shape, 1)
    lm = jnp.where(lane < n_experts, logits, -jnp.inf)
    m1 = jnp.max(lm, axis=1, keepdims=True)
    i1 = jnp.min(jnp.where(lm == m1, lane, LANE), axis=1, keepdims=True)
    lm2 = jnp.where(lane == i1, -jnp.inf, lm)
    m2 = jnp.max(lm2, axis=1, keepdims=True)
    i2 = jnp.min(jnp.where(lm2 == m2, lane, LANE), axis=1, keepdims=True)
    e2 = jnp.exp(m2 - m1)
    den = 1.0 + e2
    idx_ref[...] = jnp.where(lane == 0, i1, jnp.where(lane == 1, i2, 0))
    gate_ref[...] = jnp.where(lane == 0, 1.0 / den, jnp.where(lane == 1, e2 / den, 0.0))


def moe_route(h_all, row_offset, rows, g, r_pad, tm, n_experts, name):
    d = h_all.shape[1]
    off = row_offset // tm
    body = functools.partial(_route_body, n_experts=n_experts)
    return pl.pallas_call(
        body,
        out_shape=(jax.ShapeDtypeStruct((rows, LANE), jnp.int32), jax.ShapeDtypeStruct((rows, LANE), F32)),
        grid=(rows // tm,),
        in_specs=[pl.BlockSpec((tm, d), lambda i: (i + off, 0)), _resident((1, d)), _resident((d, LANE))],
        out_specs=(pl.BlockSpec((tm, LANE), lambda i: (i, 0)), pl.BlockSpec((tm, LANE), lambda i: (i, 0))),
        compiler_params=_params("parallel"),
        name=name,
    )(h_all, g.reshape(1, d), r_pad)


def _dispatch_body(dest_ref, h_ref, gate_ref, zeros_hbm, o_hbm, ext, sem, *, tb, n):
    del zeros_hbm
    i = pl.program_id(0)
    slot = i % 2
    groups = tb // SUBLANE
    d = h_ref.shape[1]

    def row_copy(s, t, u, row):
        return pltpu.make_async_copy(ext.at[s, t, pl.ds(u, 1)], o_hbm.at[pl.ds(row, 1)], sem.at[s])

    def wait_slot(s):
        def body(t, carry):
            pltpu.make_async_copy(ext.at[s, t], o_hbm.at[pl.ds(0, SUBLANE)], sem.at[s]).wait()
            return carry
        lax.fori_loop(0, groups, body, 0)

    @pl.when(i >= 2)
    def _():
        wait_slot(slot)

    ext[slot, :, :, 0:d] = h_ref[...].reshape(groups, SUBLANE, d)
    ext[slot, :, :, d:d + LANE] = gate_ref[...].reshape(groups, SUBLANE, LANE)

    def body(t, carry):
        for u in range(SUBLANE):
            row_copy(slot, t, u, dest_ref[0, 0, t * SUBLANE + u]).start(priority=u % 2)
        return carry
    lax.fori_loop(0, groups, body, 0)

    @pl.when(i == n - 1)
    def _():
        wait_slot(slot)
        if n >= 2:
            wait_slot(1 - slot)


def moe_dispatch(h_all, row_offset, rows, gates, dest, into, tb, name):
    d = h_all.shape[1]
    off = row_offset // tb
    n = rows // tb
    body = functools.partial(_dispatch_body, tb=tb, n=n)
    return pl.pallas_call(
        body,
        out_shape=jax.ShapeDtypeStruct(into.shape, F32),
        grid=(n,),
        in_specs=[
            pl.BlockSpec((1, 1, tb), lambda i: (i, 0, 0), memory_space=pltpu.SMEM),
            pl.BlockSpec((tb, d), lambda i: (i + off, 0)),
            pl.BlockSpec((tb, LANE), lambda i: (i, 0)),
            pl.BlockSpec(memory_space=pl.ANY),
        ],
        out_specs=pl.BlockSpec(memory_space=pl.ANY),
        scratch_shapes=[pltpu.VMEM((2, tb // SUBLANE, SUBLANE, d + LANE), F32), pltpu.SemaphoreType.DMA((2,))],
        input_output_aliases={3: 0},
        compiler_params=_params("arbitrary"),
        name=name,
    )(dest.reshape(n, 1, tb), h_all, gates, into)


def _experts_body(te_ref, nu_ref, nb_ref, fill_ref, g_first_ref, g_next_ref, s_cur_ref,
                  h_hbm, g_ref, wg_ref, wu_ref, wo_ref, y_hbm,
                  xbuf, acc, xn_ref, gsem, ssem, *, tm, block, n_rows):
    i = pl.program_id(0)
    j = pl.program_id(1)
    n_tiles = pl.num_programs(0)
    nj = pl.num_programs(1)
    n_used = nu_ref[0]
    slot = i % 2
    valid = i < n_used
    chunks = tm // SUBLANE
    d = xn_ref.shape[1]

    def gather(idx_ref, s):
        def body(t, carry):
            row0 = pl.multiple_of(idx_ref[0, 0, t], SUBLANE)
            pltpu.make_async_copy(h_hbm.at[pl.ds(row0, SUBLANE), pl.ds(0, d)], xbuf.at[s, t], gsem.at[s]).start()
            return carry
        lax.fori_loop(0, chunks, body, 0)

    def wait_gather(s):
        def body(t, carry):
            pltpu.make_async_copy(h_hbm.at[pl.ds(0, SUBLANE), pl.ds(0, d)], xbuf.at[s, t], gsem.at[s]).wait()
            return carry
        lax.fori_loop(0, chunks, body, 0)

    def scatter():
        def body(t, carry):
            row0 = pl.multiple_of(s_cur_ref[0, 0, t], SUBLANE)
            pltpu.make_async_copy(acc.at[t], y_hbm.at[pl.ds(row0, SUBLANE)], ssem.at[0]).start()
            return carry
        lax.fori_loop(0, chunks, body, 0)

    def wait_scatter():
        def body(t, carry):
            pltpu.make_async_copy(acc.at[t], y_hbm.at[pl.ds(0, SUBLANE)], ssem.at[0]).wait()
            return carry
        lax.fori_loop(0, chunks, body, 0)

    @pl.when(j == 0)
    def _():
        @pl.when(i == 0)
        def _():
            gather(g_first_ref, 0)
            acc[...] = jnp.zeros_like(acc)
            first_free = nb_ref[0] * (block // SUBLANE)
            n_free = n_rows // SUBLANE - first_free

            def zero_copy(row0):
                return pltpu.make_async_copy(acc.at[0], y_hbm.at[pl.ds(pl.multiple_of(row0, SUBLANE), SUBLANE)],
                                             ssem.at[0])

            def fill(c, carry):
                zero_copy((first_free + c) * SUBLANE).start()
                zero_copy(n_rows + (first_free + c) * SUBLANE).start()
                return carry
            lax.fori_loop(0, n_free, fill, 0)

            def fill_spare(c, carry):
                zero_copy(2 * n_rows + c * SUBLANE).start()
                return carry
            lax.fori_loop(0, chunks, fill_spare, 0)

            def drain(c, carry):
                zero_copy(0).wait()
                return carry
            lax.fori_loop(0, 2 * n_free + chunks, drain, 0)

        @pl.when(i + 1 < n_used)
        def _():
            gather(g_next_ref, 1 - slot)

        @pl.when(valid)
        def _():
            wait_gather(slot)
            xn_ref[...] = _rms(xbuf[slot].reshape(tm, d), g_ref[...]).astype(BF16)

        @pl.when((i >= 1) & (i <= n_used))
        def _():
            wait_scatter()

        @pl.when(valid)
        def _():
            acc[...] = jnp.zeros_like(acc)

    def ff_step(n_chunks):
        rows = n_chunks * SUBLANE
        xn = xn_ref[0:rows, :]
        a = _dot(xn, wg_ref[0].astype(BF16))
        b = _dot(xn, wu_ref[0].astype(BF16))
        part = _dot((_silu(a) * b).astype(BF16), wo_ref[0].astype(BF16)).reshape(n_chunks, SUBLANE, d)
        acc[0:n_chunks] += part

    half_full = fill_ref[i] <= chunks // 2

    @pl.when(valid & jnp.logical_not(half_full))
    def _():
        ff_step(chunks)

    @pl.when(valid & half_full)
    def _():
        ff_step(chunks // 2)

    @pl.when(valid & (j == nj - 1))
    def _():
        scatter()

        @pl.when(i == n_tiles - 1)
        def _():
            wait_scatter()


def moe_experts(h_ps, g, wi, wo, gather_rows, scatter_rows, tile_expert, n_used, n_blocks_used, tile_fill,
                block, n_rows, tm, tf):
    d = wo.shape[2]
    f = wo.shape[1]
    nj = f // tf
    n_tiles = gather_rows.shape[0]
    chunks = tm // SUBLANE

    def wblock(col_off):
        def index_map(i, j, te, nu, nb, tf_):
            jj = jnp.where(i < nu[0], j, nj - 1)
            return (te[i], 0, col_off + jj)
        return index_map

    def woblock(i, j, te, nu, nb, tf_):
        return (te[i], jnp.where(i < nu[0], j, nj - 1), 0)

    def smem_tile(index_map):
        return pl.BlockSpec((1, 1, chunks), index_map, memory_space=pltpu.SMEM)

    grid_spec = pltpu.PrefetchScalarGridSpec(
        num_scalar_prefetch=4,
        grid=(n_tiles, nj),
        in_specs=[
            smem_tile(lambda i, j, te, nu, nb, tf_: (0, 0, 0)),
            smem_tile(lambda i, j, te, nu, nb, tf_: (jnp.minimum(i + 1, n_tiles - 1), 0, 0)),
            smem_tile(lambda i, j, te, nu, nb, tf_: (i, 0, 0)),
            pl.BlockSpec(memory_space=pl.ANY),
            pl.BlockSpec((1, d), lambda i, j, te, nu, nb, tf_: (0, 0)),
            pl.BlockSpec((1, d, tf), wblock(0)),
            pl.BlockSpec((1, d, tf), wblock(nj)),
            pl.BlockSpec((1, tf, d), woblock),
        ],
        out_specs=pl.BlockSpec(memory_space=pl.ANY),
        scratch_shapes=[
            pltpu.VMEM((2, chunks, SUBLANE, d), F32),
            pltpu.VMEM((chunks, SUBLANE, d), F32),
            pltpu.VMEM((tm, d), BF16),
            pltpu.SemaphoreType.DMA((2,)),
            pltpu.SemaphoreType.DMA((1,)),
        ],
    )
    return pl.pallas_call(
        functools.partial(_experts_body, tm=tm, block=block, n_rows=n_rows),
        out_shape=jax.ShapeDtypeStruct((2 * n_rows + tm, d), F32),
        grid_spec=grid_spec,
        compiler_params=_params("arbitrary", "arbitrary"),
        name="moe_experts",
    )(tile_expert, n_used, n_blocks_used, tile_fill, gather_rows, gather_rows, scatter_rows, h_ps,
      g.reshape(1, d), wi, wi, wo)


def _combine_body(nb_ref, hx_ref, y1_ref, y2_ref, fn_ref, o_ref, *, d, final_norm):
    @pl.when(pl.program_id(0) < nb_ref[0])
    def _():
        hx = hx_ref[...]
        out = hx[:, 0:d] + (hx[:, d:d + 1] * y1_ref[...] + hx[:, d + 1:d + 2] * y2_ref[...])
        if final_norm:
            out = _rms(out, fn_ref[...])
        o_ref[...] = out

    @pl.when(pl.program_id(0) >= nb_ref[0])
    def _():
        o_ref[...] = jnp.zeros_like(o_ref)


def moe_combine(h_ps, y, n_blocks_used, choice_stride, fn, tb, final_norm, name):
    rows, dx = h_ps.shape
    d = y.shape[1]
    off2 = choice_stride // tb

    def used(i, nb):
        return jnp.minimum(i, nb[0] - 1)

    grid_spec = pltpu.PrefetchScalarGridSpec(
        num_scalar_prefetch=1,
        grid=(rows // tb,),
        in_specs=[
            pl.BlockSpec((tb, dx), lambda i, nb: (used(i, nb), 0)),
            pl.BlockSpec((tb, d), lambda i, nb: (used(i, nb), 0)),
            pl.BlockSpec((tb, d), lambda i, nb: (used(i, nb) + off2, 0)),
            pl.BlockSpec((1, d), lambda i, nb: (0, 0)),
        ],
        out_specs=pl.BlockSpec((tb, d), lambda i, nb: (i, 0)),
    )
    return pl.pallas_call(
        functools.partial(_combine_body, d=d, final_norm=final_norm),
        out_shape=jax.ShapeDtypeStruct((rows, d), F32),
        grid_spec=grid_spec,
        compiler_params=_params("arbitrary"),
        name=name,
    )(n_blocks_used, h_ps, y, y, fn.reshape(1, d))


def _unpermute_body(cur_ref, nxt_ref, y_hbm, o_ref, buf, sem, *, tb):
    i = pl.program_id(0)
    n = pl.num_programs(0)
    slot = i % 2
    groups = tb // SUBLANE

    def start(idx_ref, s):
        def body(t, carry):
            for u in range(SUBLANE):
                row = idx_ref[0, 0, t * SUBLANE + u]
                pltpu.make_async_copy(y_hbm.at[pl.ds(row, 1)], buf.at[s, t, pl.ds(u, 1)], sem.at[s]).start(
                    priority=u % 2)
            return carry
        lax.fori_loop(0, groups, body, 0)

    @pl.when(i == 0)
    def _():
        start(cur_ref, 0)

    @pl.when(i + 1 < n)
    def _():
        start(nxt_ref, 1 - slot)

    def wait_body(t, carry):
        pltpu.make_async_copy(y_hbm.at[pl.ds(0, SUBLANE)], buf.at[slot, t], sem.at[slot]).wait()
        return carry
    lax.fori_loop(0, groups, wait_body, 0)
    o_ref[...] = buf[slot].reshape(o_ref.shape)


def moe_unpermute(y_ps, dest, tb, name):
    rows = dest.shape[0]
    d = y_ps.shape[1]
    n = rows // tb
    idx = dest.reshape(n, 1, tb)

    def smem_tile(index_map):
        return pl.BlockSpec((1, 1, tb), index_map, memory_space=pltpu.SMEM)

    return pl.pallas_call(
        functools.partial(_unpermute_body, tb=tb),
        out_shape=jax.ShapeDtypeStruct((rows, d), F32),
        grid=(n,),
        in_specs=[smem_tile(lambda i: (i, 0, 0)), smem_tile(lambda i: (jnp.minimum(i + 1, n - 1), 0, 0)),
                  pl.BlockSpec(memory_space=pl.ANY)],
        out_specs=pl.BlockSpec((tb, d), lambda i: (i, 0)),
        scratch_shapes=[pltpu.VMEM((2, tb // SUBLANE, SUBLANE, d), F32), pltpu.SemaphoreType.DMA((2,))],
        compiler_params=_params("arbitrary"),
        name=name,
    )(idx, idx, y_ps)


def moe_plan(idx, n_experts, tm, block):
    n_tok, k = idx.shape
    assert k == 2
    n_pairs = n_experts * n_experts
    chunks_per_tile = tm // SUBLANE
    n_rows = -(-(n_tok + n_pairs * (SUBLANE - 1)) // block) * block
    n_chunks = n_rows // SUBLANE

    pair = idx[:, 0] * n_experts + idx[:, 1]
    onehot = (pair[:, None] == jnp.arange(n_pairs, dtype=jnp.int32)[None, :]).astype(jnp.int32)
    csum = jnp.cumsum(onehot, axis=0)
    counts = csum[-1]
    rank = jnp.sum(csum * onehot, axis=1) - 1
    padded = (counts + SUBLANE - 1) // SUBLANE * SUBLANE
    total = jnp.sum(padded)
    last_group = jnp.max(jnp.where(counts > 0, jnp.arange(n_pairs, dtype=jnp.int32), 0))
    padded = padded + jnp.where(jnp.arange(n_pairs) == last_group, (-total) % block, 0)
    group_end = jnp.cumsum(padded)
    group_start = group_end - padded
    dest = jnp.sum(onehot * group_start[None, :], axis=1) + rank
    rows_used = group_end[-1]

    chunk_row = jnp.arange(n_chunks, dtype=jnp.int32) * SUBLANE
    chunk_group = jnp.sum((chunk_row[:, None] >= group_end[None, :]).astype(jnp.int32), axis=1)
    chunk_used = chunk_row < rows_used
    chunk_group = jnp.minimum(chunk_group, n_pairs - 1)
    chunk_experts = jnp.stack([chunk_group // n_experts, chunk_group % n_experts], axis=1)
    experts = jnp.arange(n_experts, dtype=jnp.int32)
    member = ((chunk_experts[:, :, None] == experts[None, None, :]) & chunk_used[:, None, None])
    member = member.astype(jnp.int32)
    member_e = jnp.sum(member, axis=1)
    csum_e = jnp.cumsum(member_e, axis=0)
    counts_e = csum_e[-1]
    tiles_e = (counts_e + chunks_per_tile - 1) // chunks_per_tile
    tile_end = jnp.cumsum(tiles_e)
    start_e = (tile_end - tiles_e) * chunks_per_tile
    pos = jnp.sum(member * (start_e[None, None, :] + csum_e[:, None, :] - 1), axis=2)
    n_tiles = 2 * n_chunks // chunks_per_tile + n_experts
    n_pos = n_tiles * chunks_per_tile
    pos = jnp.where(chunk_used[:, None], pos, n_pos)
    entry = 2 * jnp.arange(n_chunks, dtype=jnp.int32)[:, None] + jnp.arange(2, dtype=jnp.int32)[None, :] + 1
    table = jnp.zeros((n_pos,), jnp.int32).at[pos.reshape(-1)].set(entry.reshape(-1), mode="drop")
    table = table.reshape(n_tiles, chunks_per_tile)
    filled = table > 0
    src_chunk = (table - 1) // 2
    choice = (table - 1) % 2
    spare = 2 * n_rows + jnp.arange(chunks_per_tile, dtype=jnp.int32) * SUBLANE
    gather_rows = jnp.where(filled, src_chunk * SUBLANE, 0)
    scatter_rows = jnp.where(filled, choice * n_rows + src_chunk * SUBLANE, spare[None, :])

    n_used = tile_end[-1]
    tile_ids = jnp.arange(n_tiles, dtype=jnp.int32)
    tile_expert = jnp.sum((tile_ids[:, None] >= tile_end[None, :]).astype(jnp.int32), axis=1)
    last_expert = jnp.sum((n_used - 1 >= tile_end).astype(jnp.int32))
    tile_expert = jnp.where(tile_ids < n_used, tile_expert, last_expert).astype(jnp.int32)
    return (dest.astype(jnp.int32),
            gather_rows.reshape(n_tiles, 1, chunks_per_tile).astype(jnp.int32),
            scatter_rows.reshape(n_tiles, 1, chunks_per_tile).astype(jnp.int32),
            tile_expert, n_used.reshape(1).astype(jnp.int32),
            (rows_used // block).reshape(1).astype(jnp.int32),
            jnp.sum(filled.astype(jnp.int32), axis=1), n_rows)


def _pad_lanes(v):
    return jnp.zeros((1, LANE), F32).at[0, :v.shape[0]].set(v)


def kernel(x_prompt, x_sample, state_dn, state_dn_conv, state_pool, cache_mem_k, cache_mem_v, mem_prompt, a_norm1, a_w_in, a_conv_w, a_a_log, a_dt_bias, a_o_norm, a_w_out, a_norm2, a_ffn_wi, a_ffn_wo, b_norm1, b_w_in, b_pool_w, b_pool_scale, b_w_out, b_norm2, b_router, b_moe_wi, b_moe_wo, m_norm, m_w_k, m_w_v, final_norm):
    bp, seq, d = x_prompt.shape
    bs = x_sample.shape[0]
    assert x_sample.shape[1] == 1
    depth = m_norm.shape[0]
    assert depth % 2 == 0, "the output norm is fused into the expert block of the last (odd) layer"
    heads, dk, dv = state_dn.shape[2], state_dn.shape[3], state_dn.shape[4]
    conv_ch = state_dn_conv.shape[-1]
    qk_w, v_w = heads * dk, heads * dv
    assert conv_ch == 2 * qk_w + v_w and dk == LANE and dv == LANE and heads <= LANE
    n_mem, x_heads, x_dh = cache_mem_k.shape[2], cache_mem_k.shape[3], cache_mem_k.shape[4]
    xw = x_heads * x_dh
    pool_w_total = state_pool.shape[-1]
    pool_buf = state_pool.shape[2]
    d_ff = a_ffn_wo.shape[1]

    hp = x_prompt.reshape(bp * seq, d)
    hs = x_sample.reshape(bs, d)

    o1, o2, o3, o4 = conv_ch, conv_ch + v_w, conv_ch + v_w + heads, conv_ch + v_w + 2 * heads
    col_a = {"qkv": 0, "z": conv_ch, "cq": conv_ch + v_w, "b": conv_ch + v_w + xw, "a": conv_ch + v_w + xw + LANE}
    n_a = -(-(col_a["a"] + LANE) // 512) * 512

    dn_p, conv_p, pool_p, mk_all, mv_all = [], [], [], [], []
    dn_s, conv_s, pool_s = [], [], []
    for i in range(depth):
        j = i // 2
        w_kv = jnp.concatenate([m_w_k[i], m_w_v[i]], axis=1).astype(BF16)
        kv, mk, mv = mem_kv(mem_prompt.reshape(bp * n_mem, d), m_norm[i], w_kv, x_heads, x_dh, 256,
                            f"mem_kv_{i}")
        mk_all.append(mk.reshape(bp, n_mem, x_heads, x_dh))
        mv_all.append(mv.reshape(bp, n_mem, x_heads, x_dh))
        if i % 2 == 0:
            w = a_w_in[j]
            w_in = jnp.zeros((d, n_a), F32)
            w_in = w_in.at[:, :o2].set(w[:, :o2])
            w_in = w_in.at[:, col_a["cq"]:col_a["cq"] + xw].set(w[:, o4:])
            w_in = w_in.at[:, col_a["b"]:col_a["b"] + heads].set(w[:, o2:o3])
            w_in = w_in.at[:, col_a["a"]:col_a["a"] + heads].set(w[:, o3:o4])
            a_log, dt_b = _pad_lanes(a_a_log[j]), _pad_lanes(a_dt_bias[j])
            o_norm = a_o_norm[j].reshape(1, dv)

            proj_p = rms_matmul(hp, a_norm1[j], w_in.astype(BF16), 256, f"a_in_p_{i}")
            proj_s = rms_matmul(hs, a_norm1[j], w_in, bs, f"a_in_s_{i}", tn=512)
            mix_p, s_p = deltanet_prompt(proj_p, col_a, a_conv_w[j], a_log, dt_b, o_norm,
                                         bp, seq, heads, dk, dv, rows=DN_STEP_CHUNKS * CHUNK)
            mix_s, s_s, c_s = deltanet_sample(proj_s, col_a, state_dn_conv[j], state_dn[j], a_conv_w[j],
                                              a_log, dt_b, o_norm, heads, dk, dv, nb=SUBLANE)
            xa_p = xattn_prompt(proj_p, col_a["cq"], kv, bp, seq, n_mem, x_heads, x_dh, tq=min(1024, seq))
            xa_s = xattn_sample(proj_s, col_a["cq"], cache_mem_k, cache_mem_v, i, nb=SUBLANE)
            hp = out_proj_ffn(hp, mix_p, xa_p, a_w_out[j].astype(BF16), a_norm2[j], a_ffn_wi[j].astype(BF16),
                              a_ffn_wo[j].astype(BF16), 256, f"a_out_ffn_p_{i}")
            hs = out_proj(hs, mix_s, xa_s, a_w_out[j], bs, f"a_out_s_{i}")
            hs = ffn_dense(hs, a_norm2[j], a_ffn_wi[j], a_ffn_wo[j], bs, f"a_ffn_s_{i}", tf=256)
            dn_p.append(s_p)
            conv_p.append(proj_p.reshape(bp, seq, n_a)[:, seq - (CONV_WIDTH - 1):, :conv_ch])
            dn_s.append(s_s)
            conv_s.append(c_s)
        else:
            ps = b_pool_scale[j].reshape(1, pool_w_total)
            last = i == depth - 1

            proj_p = rms_matmul(hp, b_norm1[j], b_w_in[j].astype(BF16), 512, f"b_in_p_{i}")
            proj_s = rms_matmul(hs, b_norm1[j], b_w_in[j], bs, f"b_in_s_{i}")
            mix_p = pool_prompt(proj_p, b_pool_w[j].astype(BF16), ps, bp, seq, tt=512)
            mix_s, buf_s = pool_sample(proj_s, state_pool[j], b_pool_w[j], ps)
            xa_p = xattn_prompt(proj_p, pool_w_total, kv, bp, seq, n_mem, x_heads, x_dh, tq=min(1024, seq))
            xa_s = xattn_sample(proj_s, pool_w_total, cache_mem_k, cache_mem_v, i, nb=SUBLANE)
            n_p = bp * seq
            h_all = out_proj(hp, mix_p, xa_p, b_w_out[j].astype(BF16), 512, f"b_out_p_{i}", out_rows=n_p + bs)
            h_all = out_proj(hs, mix_s, xa_s, b_w_out[j], bs, f"b_out_s_{i}", out_rows=n_p + bs,
                             into=h_all, row_offset=n_p)
            n_experts = b_router.shape[-1]
            r_pad = jnp.zeros((d, LANE), F32).at[:, :n_experts].set(b_router[j])
            idx_p, gate_p = moe_route(h_all, 0, n_p, b_norm2[j], r_pad.astype(BF16), 512, n_experts,
                                      f"b_route_p_{i}")
            idx_s, gate_s = moe_route(h_all, n_p, bs, b_norm2[j], r_pad, bs, n_experts, f"b_route_s_{i}")
            idx = jnp.concatenate([idx_p[:, :TOP_K], idx_s[:, :TOP_K]], axis=0)
            (dest, gather_rows, scatter_rows, tile_expert, n_used, n_blocks_used, tile_fill, n_rows) = moe_plan(
                idx, n_experts, MOE_TILE, MOE_ROW_BLOCK)
            h_ps = jnp.zeros((n_rows, d + LANE), F32)
            h_ps = moe_dispatch(h_all, 0, n_p, gate_p, dest[:n_p], h_ps, 512, f"b_dispatch_p_{i}")
            h_ps = moe_dispatch(h_all, n_p, bs, gate_s, dest[n_p:], h_ps, bs, f"b_dispatch_s_{i}")
            y = moe_experts(h_ps, b_norm2[j], b_moe_wi[j], b_moe_wo[j], gather_rows, scatter_rows, tile_expert,
                            n_used, n_blocks_used, tile_fill, MOE_ROW_BLOCK, n_rows, MOE_TILE, MOE_FF_TILE)
            y_ps = moe_combine(h_ps, y, n_blocks_used, n_rows, final_norm, MOE_ROW_BLOCK, last,
                               f"b_combine_{i}")
            hp = moe_unpermute(y_ps, dest[:n_p], 512, f"b_unpermute_p_{i}")
            hs = moe_unpermute(y_ps, dest[n_p:], bs, f"b_unpermute_s_{i}")
            pool_p.append(proj_p.reshape(bp, seq, -1)[:, seq - pool_buf:, :pool_w_total])
            pool_s.append(buf_s)

    y_prompt = hp.reshape(bp, seq, d)
    y_sample = hs.reshape(bs, 1, d)
    return (y_prompt, y_sample, jnp.stack(dn_p), jnp.stack(conv_p), jnp.stack(pool_p),
            jnp.stack(mk_all), jnp.stack(mv_all), jnp.stack(dn_s), jnp.stack(conv_s), jnp.stack(pool_s))
```

```python
import functools

import jax
import jax.numpy as jnp
from jax import lax
from jax.experimental import pallas as pl
from jax.experimental.pallas import tpu as pltpu

F32 = jnp.float32
BF16 = jnp.bfloat16
HIGHEST = lax.Precision.HIGHEST

EPS = 1e-6
CHUNK = 64
CONV_WIDTH = 4
POOL_WINDOWS = (2, 4, 8, 16)
PAST_LEN = 16384
TOP_K = 2

LANE = 128
SUBLANE = 8
VMEM_LIMIT_BYTES = 56 * 2**20

MOE_TILE = 1024
MOE_FF_TILE = 512
MOE_ROW_BLOCK = 512
DN_STEP_CHUNKS = 8
INV_BLOCK = 16
POOL_HALO = 32


def _params(*sem):
    return pltpu.CompilerParams(dimension_semantics=sem, vmem_limit_bytes=VMEM_LIMIT_BYTES)


def _resident(shape):
    nd = len(shape)
    return pl.BlockSpec(shape, lambda *_: (0,) * nd, pipeline_mode=pl.Buffered(1))


def _rms(x, w):
    return x * lax.rsqrt(jnp.mean(x * x, axis=-1, keepdims=True) + EPS) * w


def _silu(x):
    half = 0.5 * x
    return half + half * jnp.tanh(half)


def _dot(a, b):
    return jnp.dot(a, b, preferred_element_type=F32)


def _wdot(a, w):
    if w.dtype == F32:
        return jnp.dot(a, w, precision=HIGHEST, preferred_element_type=F32)
    return jnp.dot(a.astype(BF16), w, preferred_element_type=F32)


def _bdot(a, b):
    return lax.dot_general(a, b, (((2,), (1,)), ((0,), (0,))), preferred_element_type=F32)


def _bdot_nt(a, b):
    return lax.dot_general(a, b, (((2,), (2,)), ((0,), (0,))), preferred_element_type=F32)


def _bdot_tn(a, b):
    return lax.dot_general(a, b, (((1,), (1,)), ((0,), (0,))), preferred_element_type=F32)


def _rms_matmul_body(x_ref, g_ref, w_ref, o_ref):
    xn = _rms(x_ref[...], g_ref[...])
    o_ref[...] = _wdot(xn, w_ref[...])


def _wspec(shape, index_map, steps):
    if steps == 1:
        return pl.BlockSpec(shape, index_map, pipeline_mode=pl.Buffered(1))
    return pl.BlockSpec(shape, index_map)


def rms_matmul(x, g, w, tm, name, tn=None):
    m, d = x.shape
    n = w.shape[1]
    tn = n if tn is None else tn
    nj = n // tn
    return pl.pallas_call(
        _rms_matmul_body,
        out_shape=jax.ShapeDtypeStruct((m, n), F32),
        grid=(m // tm, nj),
        in_specs=[pl.BlockSpec((tm, d), lambda i, j: (i, 0)), _resident((1, d)),
                  _wspec((d, tn), lambda i, j: (0, j), nj)],
        out_specs=pl.BlockSpec((tm, tn), lambda i, j: (i, j)),
        compiler_params=_params("parallel", "arbitrary"),
        name=name,
    )(x, g.reshape(1, d), w)


def _out_proj_body(x_ref, a1_ref, a2_ref, w1_ref, w2_ref, *rest, n_main):
    o_ref = rest[-1]

    @pl.when(pl.program_id(0) < n_main)
    def _():
        acc = _wdot(a1_ref[...], w1_ref[...]) + _wdot(a2_ref[...], w2_ref[...])
        o_ref[...] = x_ref[...] + acc

    @pl.when(pl.program_id(0) >= n_main)
    def _():
        o_ref[...] = jnp.zeros_like(o_ref)


def out_proj(x, a1, a2, w, tm, name, out_rows=None, into=None, row_offset=0):
    m, d = x.shape
    k1, k2 = a1.shape[1], a2.shape[1]
    assert w.shape[0] == k1 + k2 and k1 % k2 == 0 and row_offset % tm == 0
    out_rows = m if out_rows is None else out_rows
    off = row_offset // tm
    n_main = m // tm
    n_extra = 0 if (into is not None or out_rows == m) else 1
    assert out_rows - m <= n_extra * tm or into is not None
    last = n_main - 1
    in_specs = [
        pl.BlockSpec((tm, d), lambda i: (jnp.minimum(i, last), 0)),
        pl.BlockSpec((tm, k1), lambda i: (jnp.minimum(i, last), 0)),
        pl.BlockSpec((tm, k2), lambda i: (jnp.minimum(i, last), 0)),
        pl.BlockSpec((k1, d), lambda i: (0, 0), pipeline_mode=pl.Buffered(1)),
        pl.BlockSpec((k2, d), lambda i: (k1 // k2, 0), pipeline_mode=pl.Buffered(1)),
    ]
    args = [x, a1, a2, w, w]
    aliases = {}
    if into is not None:
        in_specs.append(pl.BlockSpec(memory_space=pl.ANY))
        args.append(into)
        aliases = {len(args) - 1: 0}
    return pl.pallas_call(
        functools.partial(_out_proj_body, n_main=n_main),
        out_shape=jax.ShapeDtypeStruct((out_rows, d), F32),
        grid=(n_main + n_extra,),
        in_specs=in_specs,
        out_specs=pl.BlockSpec((tm, d), lambda i: (i + off, 0)),
        input_output_aliases=aliases,
        compiler_params=_params("parallel"),
        name=name,
    )(*args)


def _ffn_body(h_ref, g_ref, wg_ref, wu_ref, wo_ref, o_ref, xn_ref):
    @pl.when(pl.program_id(1) == 0)
    def _():
        h = h_ref[...]
        xn_ref[...] = _rms(h, g_ref[...])
        o_ref[...] = h

    xn = xn_ref[...]
    a = _wdot(xn, wg_ref[...])
    b = _wdot(xn, wu_ref[...])
    o_ref[...] += _wdot(_silu(a) * b, wo_ref[...])


def ffn_dense(h, g, wi, wo, tm, name, tf=None):
    m, d = h.shape
    f = wo.shape[0]
    tf = f if tf is None else tf
    nj = f // tf
    return pl.pallas_call(
        _ffn_body,
        out_shape=jax.ShapeDtypeStruct((m, d), F32),
        grid=(m // tm, nj),
        in_specs=[
            pl.BlockSpec((tm, d), lambda i, j: (i, 0)),
            _resident((1, d)),
            _wspec((d, tf), lambda i, j: (0, j), nj),
            _wspec((d, tf), lambda i, j: (0, nj + j), nj),
            _wspec((tf, d), lambda i, j: (j, 0), nj),
        ],
        out_specs=pl.BlockSpec((tm, d), lambda i, j: (i, 0)),
        scratch_shapes=[pltpu.VMEM((tm, d), F32)],
        compiler_params=_params("parallel", "arbitrary"),
        name=name,
    )(h, g.reshape(1, d), wi, wi, wo)


def _mem_kv_body(x_ref, g_ref, w_ref, kv_ref, k_ref, v_ref, *, heads, dh):
    xw = heads * dh
    res = _wdot(_rms(x_ref[...], g_ref[...]), w_ref[...])
    kv_ref[...] = res
    for h in range(heads):
        k_ref[:, h, :] = res[:, h * dh:(h + 1) * dh]
        v_ref[:, h, :] = res[:, xw + h * dh:xw + (h + 1) * dh]


def mem_kv(mem, g, w_kv, heads, dh, tm, name):
    m, d = mem.shape
    xw = heads * dh
    head_shape = jax.ShapeDtypeStruct((m, heads, dh), F32)
    head_spec = pl.BlockSpec((tm, heads, dh), lambda i: (i, 0, 0))
    return pl.pallas_call(
        functools.partial(_mem_kv_body, heads=heads, dh=dh),
        out_shape=(jax.ShapeDtypeStruct((m, 2 * xw), F32), head_shape, head_shape),
        grid=(m // tm,),
        in_specs=[pl.BlockSpec((tm, d), lambda i: (i, 0)), _resident((1, d)), _resident((d, 2 * xw))],
        out_specs=(pl.BlockSpec((tm, 2 * xw), lambda i: (i, 0)), head_spec, head_spec),
        compiler_params=_params("parallel"),
        name=name,
    )(mem, g.reshape(1, d), w_kv)


def _out_proj_ffn_body(x_ref, a1_ref, a2_ref, w1_ref, w2_ref, g_ref, wg_ref, wu_ref, wo_ref, o_ref):
    h = x_ref[...] + (_wdot(a1_ref[...], w1_ref[...]) + _wdot(a2_ref[...], w2_ref[...]))
    xn = _rms(h, g_ref[...])
    a = _wdot(xn, wg_ref[...])
    b = _wdot(xn, wu_ref[...])
    o_ref[...] = h + _wdot(_silu(a) * b, wo_ref[...])


def out_proj_ffn(x, a1, a2, w, g, wi, wo, tm, name):
    m, d = x.shape
    k1, k2 = a1.shape[1], a2.shape[1]
    f = wo.shape[0]
    assert w.shape[0] == k1 + k2 and k1 % k2 == 0
    return pl.pallas_call(
        _out_proj_ffn_body,
        out_shape=jax.ShapeDtypeStruct((m, d), F32),
        grid=(m // tm,),
        in_specs=[
            pl.BlockSpec((tm, d), lambda i: (i, 0)),
            pl.BlockSpec((tm, k1), lambda i: (i, 0)),
            pl.BlockSpec((tm, k2), lambda i: (i, 0)),
            pl.BlockSpec((k1, d), lambda i: (0, 0), pipeline_mode=pl.Buffered(1)),
            pl.BlockSpec((k2, d), lambda i: (k1 // k2, 0), pipeline_mode=pl.Buffered(1)),
            _resident((1, d)),
            pl.BlockSpec((d, f), lambda i: (0, 0), pipeline_mode=pl.Buffered(1)),
            pl.BlockSpec((d, f), lambda i: (0, 1), pipeline_mode=pl.Buffered(1)),
            _resident((f, d)),
        ],
        out_specs=pl.BlockSpec((tm, d), lambda i: (i, 0)),
        compiler_params=_params("parallel"),
        name=name,
    )(x, a1, a2, w, w, g.reshape(1, d), wi, wi, wo)


def _pair_blockdiag(y, left):
    yb = y.astype(BF16)
    zero = jnp.zeros_like(yb)
    return jnp.concatenate([jnp.where(left, yb, zero), jnp.where(left, zero, yb)], axis=1)


def _pair_mm(x, y, left):
    return _bdot(x.astype(BF16), _pair_blockdiag(y, left))


def _unit_lower_inverse(mx, eye, blk, left):
    mm = functools.partial(_pair_mm, left=left)
    md = jnp.where(blk, mx, 0.0)
    c = mx - md
    p = eye - md
    m2 = mm(md, md)
    p = p + mm(p, m2)
    m4 = mm(m2, m2)
    p = p + mm(p, m4)
    m8 = mm(m4, m4)
    td = p + mm(p, m8)
    n = mm(td, c)
    n2 = mm(n, n)
    q = eye - n
    q = q + mm(q, n2)
    return mm(q, td)


def _dn_prompt_body(qkv_ref, z_ref, bl_ref, al_ref, cw_ref, alog_ref, dtb_ref, on_ref,
                    dn_ref, sfin_ref, s_ref, xbuf_ref, act_ref, *, rows, heads, dk, dv):
    c = pl.program_id(1)
    wq = heads * dk

    @pl.when(c == 0)
    def _():
        s_ref[...] = jnp.zeros_like(s_ref)
        xbuf_ref[0:SUBLANE, :] = jnp.zeros((SUBLANE, xbuf_ref.shape[1]), F32)

    xbuf_ref[SUBLANE:SUBLANE + rows, :] = qkv_ref[...]
    first = SUBLANE - (CONV_WIDTH - 1)
    y = xbuf_ref[pl.ds(first, rows), :] * cw_ref[0:1, :]
    for j in range(1, CONV_WIDTH):
        y = y + xbuf_ref[pl.ds(first + j, rows), :] * cw_ref[j:j + 1, :]
    act_ref[...] = _silu(y)
    xbuf_ref[0:SUBLANE, :] = xbuf_ref[rows:rows + SUBLANE, :]

    pairs = heads // 2
    ri = lax.broadcasted_iota(jnp.int32, (CHUNK, 2 * CHUNK), 0)
    lane = lax.broadcasted_iota(jnp.int32, (CHUNK, 2 * CHUNK), 1)
    cj = lane % CHUNK
    left = (lane < CHUNK)[None]
    incl = (ri >= cj)[None]
    strict = (ri > cj)[None]
    blk = ((ri // INV_BLOCK) == (cj // INV_BLOCK))[None]
    eye = (ri == cj).astype(F32)[None]
    tr = lax.broadcasted_iota(jnp.int32, (CHUNK, CHUNK), 0)
    tc = lax.broadcasted_iota(jnp.int32, (CHUNK, CHUNK), 1)
    tri = (tr >= tc).astype(F32)

    n_chunks = rows // CHUNK
    zero = jnp.zeros((CHUNK, dk), BF16)

    def chunk_rows(cc):
        return slice(cc * CHUNK, (cc + 1) * CHUNK)

    def per_head(fn):
        return jnp.stack([fn(chunk_rows(cc), h) for cc in range(n_chunks) for h in range(heads)], axis=0)

    def per_pair(fn):
        return jnp.stack([fn(cc, p) for cc in range(n_chunks) for p in range(pairs)], axis=0)

    q3 = per_head(lambda rs, h: act_ref[rs, h * dk:(h + 1) * dk])
    k3 = per_head(lambda rs, h: act_ref[rs, wq + h * dk:wq + (h + 1) * dk])
    v3 = per_head(lambda rs, h: act_ref[rs, 2 * wq + h * dv:2 * wq + (h + 1) * dv])
    qn = q3 * lax.rsqrt(jnp.sum(q3 * q3, axis=-1, keepdims=True) + EPS) * (dk ** -0.5)
    kn = k3 * lax.rsqrt(jnp.sum(k3 * k3, axis=-1, keepdims=True) + EPS)

    beta = jax.nn.sigmoid(bl_ref[...])
    log_decay = -jnp.exp(alog_ref[...]) * jax.nn.softplus(al_ref[...] + dtb_ref[...])
    gcum = [jnp.dot(tri, log_decay[chunk_rows(cc)], precision=HIGHEST, preferred_element_type=F32)
            for cc in range(n_chunks)]
    gcum_t = [jnp.concatenate([gc_, pltpu.roll(gc_, LANE - 1, 1)], axis=0).T for gc_ in gcum]

    def pair_cols(x_of):
        return per_pair(lambda cc, p: jnp.where(left[0], x_of(cc)[:, 2 * p:2 * p + 1],
                                                x_of(cc)[:, 2 * p + 1:2 * p + 2]))

    gc2 = pair_cols(lambda cc: gcum[cc])
    bc2 = pair_cols(lambda cc: beta[chunk_rows(cc)])
    gr2 = per_pair(lambda cc, p: gcum_t[cc][2 * p:2 * p + 1, :])
    decay = jnp.where(incl, jnp.exp(gc2 - gr2), 0.0)

    kb = kn.astype(BF16)
    qb = qn.astype(BF16)

    def pair_of(x, cc, p):
        return x[cc * heads + 2 * p], x[cc * heads + 2 * p + 1]

    k_lhs = per_pair(lambda cc, p: jnp.concatenate(pair_of(kb, cc, p), axis=1))
    q_lhs = per_pair(lambda cc, p: jnp.concatenate(pair_of(qb, cc, p), axis=1))
    k_bd = per_pair(lambda cc, p: jnp.concatenate(
        [jnp.concatenate([pair_of(kb, cc, p)[0], zero], axis=1),
         jnp.concatenate([zero, pair_of(kb, cc, p)[1]], axis=1)], axis=0))
    kk = _bdot_nt(k_lhs, k_bd)
    mx = jnp.where(strict, bc2 * kk * decay, 0.0)
    tinv = _unit_lower_inverse(mx, eye, blk, left)

    gcum_rows = jnp.concatenate(gcum, axis=0)
    gc = per_head(lambda rs, h: gcum_rows[rs, h:h + 1])
    bc = per_head(lambda rs, h: beta[rs, h:h + 1])
    eg = jnp.exp(gc)
    rhs = jnp.concatenate([v3 * bc, kn * (bc * eg)], axis=-1)
    sol = _bdot(_pair_blockdiag(tinv, left), rhs.astype(BF16).reshape(n_chunks * pairs, 2 * CHUNK, dv + dk))
    sol = sol.reshape(n_chunks * heads, CHUNK, dv + dk)
    w_val = sol[..., :dv]
    k_cd = sol[..., dv:].astype(BF16)
    qk_bd = _pair_blockdiag(_bdot_nt(q_lhs, k_bd) * decay, left)
    q_dec = (qn * eg).astype(BF16)
    g_last = gc[:, CHUNK - 1:CHUNK, :]
    k_tail = (kn * jnp.exp(g_last - gc)).astype(BF16)
    c_decay = jnp.exp(g_last)

    for cc in range(n_chunks):
        hs = slice(cc * heads, (cc + 1) * heads)
        ps = slice(cc * pairs, (cc + 1) * pairs)
        rs = chunk_rows(cc)
        s = s_ref[...]
        sb = s.astype(BF16)
        u = w_val[hs] - _bdot(k_cd[hs], sb)
        ub = u.astype(BF16)
        intra = _bdot(qk_bd[ps], ub.reshape(pairs, 2 * CHUNK, dv))
        o = _bdot(q_dec[hs], sb) + intra.reshape(heads, CHUNK, dv)
        s_ref[...] = s * c_decay[hs] + _bdot_tn(k_tail[hs], ub)

        o = o * lax.rsqrt(jnp.mean(o * o, axis=-1, keepdims=True) + EPS) * on_ref[...]
        for h in range(heads):
            z = z_ref[rs, h * dv:(h + 1) * dv]
            dn_ref[rs, h * dv:(h + 1) * dv] = o[h] * _silu(z)

    @pl.when(c == pl.num_programs(1) - 1)
    def _():
        sfin_ref[0] = s_ref[...]


def deltanet_prompt(proj, col, conv_w, a_log, dt_bias, o_norm, batch, seq, heads, dk, dv, rows):
    conv_ch = 2 * heads * dk + heads * dv
    zw = heads * dv
    steps = seq // rows

    def row_map(width_off):
        return lambda b, c: (b * steps + c, width_off)

    body = functools.partial(_dn_prompt_body, rows=rows, heads=heads, dk=dk, dv=dv)
    return pl.pallas_call(
        body,
        out_shape=(jax.ShapeDtypeStruct((batch * seq, zw), F32),
                   jax.ShapeDtypeStruct((batch, heads, dk, dv), F32)),
        grid=(batch, steps),
        in_specs=[
            pl.BlockSpec((rows, conv_ch), row_map(col["qkv"] // conv_ch)),
            pl.BlockSpec((rows, zw), row_map(col["z"] // zw)),
            pl.BlockSpec((rows, LANE), row_map(col["b"] // LANE)),
            pl.BlockSpec((rows, LANE), row_map(col["a"] // LANE)),
            _resident((CONV_WIDTH, conv_ch)),
            _resident((1, LANE)),
            _resident((1, LANE)),
            _resident((1, dv)),
        ],
        out_specs=(pl.BlockSpec((rows, zw), lambda b, c: (b * steps + c, 0)),
                   pl.BlockSpec((1, heads, dk, dv), lambda b, c: (b, 0, 0, 0))),
        scratch_shapes=[
            pltpu.VMEM((heads, dk, dv), F32),
            pltpu.VMEM((rows + SUBLANE, conv_ch), F32),
            pltpu.VMEM((rows, conv_ch), F32),
        ],
        compiler_params=_params("parallel", "arbitrary"),
        name="deltanet_prompt",
    )(proj, proj, proj, proj, conv_w, a_log, dt_bias, o_norm)


def _dn_sample_body(qkv_ref, z_ref, bl_ref, al_ref, cbuf_ref, st_ref, cw_ref, alog_ref, dtb_ref, on_ref,
                    dn_ref, sto_ref, cbo_ref, act_ref, *, nb, heads, dk, dv):
    wq = heads * dk
    new = qkv_ref[...]
    y = cbuf_ref[:, 0, :] * cw_ref[0:1, :]
    for j in range(1, CONV_WIDTH - 1):
        y = y + cbuf_ref[:, j, :] * cw_ref[j:j + 1, :]
    y = y + new * cw_ref[CONV_WIDTH - 1:CONV_WIDTH, :]
    act_ref[...] = _silu(y)
    for j in range(CONV_WIDTH - 2):
        cbo_ref[:, j, :] = cbuf_ref[:, j + 1, :]
    cbo_ref[:, CONV_WIDTH - 2, :] = new

    beta = jax.nn.sigmoid(bl_ref[...])
    g = -jnp.exp(alog_ref[...]) * jax.nn.softplus(al_ref[...] + dtb_ref[...])
    dec = jnp.exp(g)

    q_t, k_t = [], []
    for h in range(heads):
        q = act_ref[:, h * dk:(h + 1) * dk]
        k = act_ref[:, wq + h * dk:wq + (h + 1) * dk]
        qn = q * lax.rsqrt(jnp.sum(q * q, axis=-1, keepdims=True) + EPS) * (dk ** -0.5)
        kn = k * lax.rsqrt(jnp.sum(k * k, axis=-1, keepdims=True) + EPS)
        q_t.append(qn.T)
        k_t.append(kn.T)

    def per_head(fn):
        return jnp.stack([fn(h) for h in range(heads)], axis=0)

    for i in range(nb):
        row = slice(i, i + 1)
        kcol = per_head(lambda h: k_t[h][:, i:i + 1])
        qcol = per_head(lambda h: q_t[h][:, i:i + 1])
        v = per_head(lambda h: act_ref[row, 2 * wq + h * dv:2 * wq + (h + 1) * dv])
        z = per_head(lambda h: z_ref[row, h * dv:(h + 1) * dv])
        b_i = per_head(lambda h: beta[row, h:h + 1])
        d_i = per_head(lambda h: dec[row, h:h + 1])
        s = st_ref[i] * d_i
        u = b_i * (v - jnp.sum(kcol * s, axis=1, keepdims=True))
        s = s + kcol * u
        o = jnp.sum(qcol * s, axis=1, keepdims=True)
        sto_ref[i] = s
        o = o * lax.rsqrt(jnp.mean(o * o, axis=-1, keepdims=True) + EPS) * on_ref[...]
        o = o * _silu(z)
        for h in range(heads):
            dn_ref[row, h * dv:(h + 1) * dv] = o[h]


def deltanet_sample(proj, col, conv_buf, state, conv_w, a_log, dt_bias, o_norm, heads, dk, dv, nb):
    n_seq = proj.shape[0]
    conv_ch = 2 * heads * dk + heads * dv
    zw = heads * dv
    body = functools.partial(_dn_sample_body, nb=nb, heads=heads, dk=dk, dv=dv)
    return pl.pallas_call(
        body,
        out_shape=(jax.ShapeDtypeStruct((n_seq, zw), F32),
                   jax.ShapeDtypeStruct(state.shape, F32),
                   jax.ShapeDtypeStruct(conv_buf.shape, F32)),
        grid=(n_seq // nb,),
        in_specs=[
            pl.BlockSpec((nb, conv_ch), lambda i: (i, col["qkv"] // conv_ch)),
            pl.BlockSpec((nb, zw), lambda i: (i, col["z"] // zw)),
            pl.BlockSpec((nb, LANE), lambda i: (i, col["b"] // LANE)),
            pl.BlockSpec((nb, LANE), lambda i: (i, col["a"] // LANE)),
            pl.BlockSpec((nb, CONV_WIDTH - 1, conv_ch), lambda i: (i, 0, 0)),
            pl.BlockSpec((nb, heads, dk, dv), lambda i: (i, 0, 0, 0)),
            _resident((CONV_WIDTH, conv_ch)),
            _resident((1, LANE)),
            _resident((1, LANE)),
            _resident((1, dv)),
        ],
        out_specs=(pl.BlockSpec((nb, zw), lambda i: (i, 0)),
                   pl.BlockSpec((nb, heads, dk, dv), lambda i: (i, 0, 0, 0)),
                   pl.BlockSpec((nb, CONV_WIDTH - 1, conv_ch), lambda i: (i, 0, 0))),
        scratch_shapes=[
            pltpu.VMEM((nb, conv_ch), F32),
        ],
        compiler_params=_params("parallel"),
        name="deltanet_sample",
    )(proj, proj, proj, proj, conv_buf, state, conv_w, a_log, dt_bias, o_norm)


def _xattn_prompt_body(q_ref, k_ref, v_ref, o_ref, *, heads, dh):
    for h in range(heads):
        sl = slice(h * dh, (h + 1) * dh)
        q = q_ref[:, sl].astype(BF16)
        k = k_ref[:, sl].astype(BF16)
        v = v_ref[:, sl].astype(BF16)
        s = lax.dot_general(q, k, (((1,), (1,)), ((), ())), preferred_element_type=F32) * (dh ** -0.5)
        e = jnp.exp(s - jnp.max(s, axis=-1, keepdims=True))
        p = e / jnp.sum(e, axis=-1, keepdims=True)
        o_ref[:, sl] = _dot(p.astype(BF16), v)


def xattn_prompt(proj, cq_col, kv, batch, seq, n_mem, heads, dh, tq):
    xw = heads * dh
    steps = seq // tq
    body = functools.partial(_xattn_prompt_body, heads=heads, dh=dh)
    return pl.pallas_call(
        body,
        out_shape=jax.ShapeDtypeStruct((batch * seq, xw), F32),
        grid=(batch, steps),
        in_specs=[
            pl.BlockSpec((tq, xw), lambda b, i: (b * steps + i, cq_col // xw)),
            pl.BlockSpec((n_mem, xw), lambda b, i: (b, 0)),
            pl.BlockSpec((n_mem, xw), lambda b, i: (b, 1)),
        ],
        out_specs=pl.BlockSpec((tq, xw), lambda b, i: (b * steps + i, 0)),
        compiler_params=_params("parallel", "parallel"),
        name="xattn_prompt",
    )(proj, kv, kv)


def _xattn_sample_body(q_ref, k_ref, v_ref, o_ref, *, nb, heads, dh):
    rep = SUBLANE // heads
    n_t = k_ref.shape[2] // SUBLANE

    def over_groups(x, op):
        out = x
        for t in range(1, rep):
            out = op(out, pltpu.roll(x, t * heads, 0))
        return out

    for i in range(nb):
        row = slice(i, i + 1)
        q8 = jnp.concatenate([q_ref[row, h * dh:(h + 1) * dh] for h in range(heads)] * rep, axis=0)
        k3 = k_ref[0, i].reshape(n_t, SUBLANE, dh)
        v3 = v_ref[0, i].reshape(n_t, SUBLANE, dh)
        s = jnp.sum(k3 * q8[None], axis=-1, keepdims=True) * (dh ** -0.5)
        m = over_groups(jnp.max(s, axis=0), jnp.maximum)
        e = jnp.exp(s - m[None])
        den = over_groups(jnp.sum(e, axis=0), jnp.add)
        p = e / den[None]
        acc = over_groups(jnp.sum(p * v3, axis=0), jnp.add)
        for h in range(heads):
            o_ref[row, h * dh:(h + 1) * dh] = acc[h:h + 1, :]


def xattn_sample(proj, cq_col, cache_k, cache_v, layer, nb):
    n_layers, n_seq, n_mem, heads, dh = cache_k.shape
    assert SUBLANE % heads == 0 and (n_mem * heads) % SUBLANE == 0
    xw = heads * dh
    body = functools.partial(_xattn_sample_body, nb=nb, heads=heads, dh=dh)
    cache_spec = pl.BlockSpec((1, nb, n_mem * heads, dh), lambda i: (layer, i, 0, 0))
    flat = (n_layers, n_seq, n_mem * heads, dh)
    return pl.pallas_call(
        body,
        out_shape=jax.ShapeDtypeStruct((n_seq, xw), F32),
        grid=(n_seq // nb,),
        in_specs=[pl.BlockSpec((nb, xw), lambda i: (i, cq_col // xw)), cache_spec, cache_spec],
        out_specs=pl.BlockSpec((nb, xw), lambda i: (i, 0)),
        compiler_params=_params("parallel"),
        name="xattn_sample",
    )(proj, cache_k.reshape(flat), cache_v.reshape(flat))


def _pool_mix(sums, cnts, u_of, pw_ref, ps_ref, o_ref, gw):
    for gi in range(len(POOL_WINDOWS)):
        cols = slice(gi * gw, (gi + 1) * gw)
        d = sums[gi] / cnts[gi] - u_of(cols)
        o_ref[:, cols] = _wdot(d, pw_ref[gi]) * ps_ref[:, cols]


def _pool_prompt_body(x_ref, g_ref, w_ref, pw_ref, ps_ref, o_ref, cq_ref, tail_ref, xbuf, s2, s4, s8, *, tt, gw):
    t = pl.program_id(1)
    n = POOL_HALO + tt
    pw_total = xbuf.shape[1]

    @pl.when(t == 0)
    def _():
        xbuf[0:POOL_HALO, :] = jnp.zeros((POOL_HALO, pw_total), F32)

    proj = _wdot(_rms(x_ref[...], g_ref[...]), w_ref[...])
    cq_ref[...] = proj[:, pw_total:]
    xbuf[POOL_HALO:n, :] = proj[:, 0:pw_total]
    s2[8:n, :] = xbuf[8:n, :] + xbuf[7:n - 1, :]
    s4[16:n, :] = s2[16:n, gw:] + s2[14:n - 2, gw:]
    s8[24:n, :] = s4[24:n, gw:] + s4[20:n - 4, gw:]
    s16 = s8[32:n, gw:] + s8[24:n - 8, gw:]
    sums = [s2[POOL_HALO:n, 0:gw], s4[POOL_HALO:n, 0:gw], s8[POOL_HALO:n, 0:gw], s16]
    pos = t * tt + lax.broadcasted_iota(jnp.int32, (tt, 1), 0)
    cnts = [jnp.minimum(pos + 1, w).astype(F32) for w in POOL_WINDOWS]
    _pool_mix(sums, cnts, lambda cols: xbuf[POOL_HALO:n, cols], pw_ref, ps_ref, o_ref, gw)
    tail_ref[0] = xbuf[tt:n, :]
    xbuf[0:POOL_HALO, :] = xbuf[tt:n, :]


def proj_pool_prompt(x, g, w, pool_w, pool_scale, batch, seq, tt):
    m, d = x.shape
    n_out = w.shape[1]
    pw_total = pool_scale.shape[-1]
    gw = pw_total // len(POOL_WINDOWS)
    steps = seq // tt
    n = POOL_HALO + tt
    body = functools.partial(_pool_prompt_body, tt=tt, gw=gw)
    return pl.pallas_call(
        body,
        out_shape=(jax.ShapeDtypeStruct((m, pw_total), F32),
                   jax.ShapeDtypeStruct((m, n_out - pw_total), F32),
                   jax.ShapeDtypeStruct((batch, POOL_HALO, pw_total), F32)),
        grid=(batch, steps),
        in_specs=[
            pl.BlockSpec((tt, d), lambda b, t: (b * steps + t, 0)),
            _resident((1, d)),
            _resident((d, n_out)),
            _resident(pool_w.shape),
            _resident((1, pw_total)),
        ],
        out_specs=(pl.BlockSpec((tt, pw_total), lambda b, t: (b * steps + t, 0)),
                   pl.BlockSpec((tt, n_out - pw_total), lambda b, t: (b * steps + t, 0)),
                   pl.BlockSpec((1, POOL_HALO, pw_total), lambda b, t: (b, 0, 0))),
        scratch_shapes=[
            pltpu.VMEM((n, pw_total), F32),
            pltpu.VMEM((n, pw_total), F32),
            pltpu.VMEM((n, pw_total - gw), F32),
            pltpu.VMEM((n, pw_total - 2 * gw), F32),
        ],
        compiler_params=_params("parallel", "arbitrary"),
        name="proj_pool_prompt",
    )(x, g.reshape(1, d), w, pool_w, pool_scale)


def _pool_sample_body(u_ref, st_ref, pw_ref, ps_ref, o_ref, sto_ref, *, gw, buf):
    pw_total = gw * len(POOL_WINDOWS)
    new = u_ref[:, 0:pw_total]
    sums, cnts = [], []
    for gi, w in enumerate(POOL_WINDOWS):
        cols = slice(gi * gw, (gi + 1) * gw)
        s = new[:, cols]
        for i in range(1, w):
            s = s + st_ref[:, buf - i, cols]
        sums.append(s)
        cnts.append(float(min(PAST_LEN + 1, w)))
    _pool_mix(sums, cnts, lambda cols: new[:, cols], pw_ref, ps_ref, o_ref, gw)
    sto_ref[:, 0:buf - 1, :] = st_ref[:, 1:buf, :]
    sto_ref[:, buf - 1, :] = new


def pool_sample(proj, state, pool_w, pool_scale):
    n_seq, buf, pw_total = state.shape
    gw = pw_total // len(POOL_WINDOWS)
    body = functools.partial(_pool_sample_body, gw=gw, buf=buf)
    return pl.pallas_call(
        body,
        out_shape=(jax.ShapeDtypeStruct((n_seq, pw_total), F32),
                   jax.ShapeDtypeStruct(state.shape, F32)),
        compiler_params=pltpu.CompilerParams(vmem_limit_bytes=VMEM_LIMIT_BYTES),
        name="pool_sample",
    )(proj, state, pool_w, pool_scale)


def _route_body(h_ref, g_ref, r_ref, idx_ref, gate_ref, *, n_experts):
    xn = _rms(h_ref[...], g_ref[...])
    logits = _wdot(xn, r_ref[...])
    lane = lax.broadcasted_iota(jnp.int32, logits.shape, 1)
    lm = jnp.where(lane < n_experts, logits, -jnp.inf)
    m1 = jnp.max(lm, axis=1, keepdims=True)
    i1 = jnp.min(jnp.where(lm == m1, lane, LANE), axis=1, keepdims=True)
    lm2 = jnp.where(lane == i1, -jnp.inf, lm)
    m2 = jnp.max(lm2, axis=1, keepdims=True)
    i2 = jnp.min(jnp.where(lm2 == m2, lane, LANE), axis=1, keepdims=True)
    e2 = jnp.exp(m2 - m1)
    den = 1.0 + e2
    idx_ref[...] = jnp.where(lane == 0, i1, jnp.where(lane == 1, i2, 0))
    gate_ref[...] = jnp.where(lane == 0, 1.0 / den, jnp.where(lane == 1, e2 / den, 0.0))


def moe_route(h_all, row_offset, rows, g, r_pad, tm, n_experts, name):
    d = h_all.shape[1]
    off = row_offset // tm
    body = functools.partial(_route_body, n_experts=n_experts)
    return pl.pallas_call(
        body,
        out_shape=(jax.ShapeDtypeStruct((rows, LANE), jnp.int32), jax.ShapeDtypeStruct((rows, LANE), F32)),
        grid=(rows // tm,),
        in_specs=[pl.BlockSpec((tm, d), lambda i: (i + off, 0)), _resident((1, d)), _resident((d, LANE))],
        out_specs=(pl.BlockSpec((tm, LANE), lambda i: (i, 0)), pl.BlockSpec((tm, LANE), lambda i: (i, 0))),
        compiler_params=_params("parallel"),
        name=name,
    )(h_all, g.reshape(1, d), r_pad)


def _dispatch_body(dest_ref, h_ref, gate_ref, zeros_hbm, o_hbm, ext, sem, *, tb, n):
    del zeros_hbm
    i = pl.program_id(0)
    slot = i % 2
    groups = tb // SUBLANE
    d = h_ref.shape[1]

    def row_copy(s, t, u, row):
        return pltpu.make_async_copy(ext.at[s, t, pl.ds(u, 1)], o_hbm.at[pl.ds(row, 1)], sem.at[s])

    def wait_slot(s):
        def body(t, carry):
            pltpu.make_async_copy(ext.at[s, t], o_hbm.at[pl.ds(0, SUBLANE)], sem.at[s]).wait()
            return carry
        lax.fori_loop(0, groups, body, 0)

    @pl.when(i >= 2)
    def _():
        wait_slot(slot)

    ext[slot, :, :, 0:d] = h_ref[...].reshape(groups, SUBLANE, d)
    ext[slot, :, :, d:d + LANE] = gate_ref[...].reshape(groups, SUBLANE, LANE)

    def body(t, carry):
        for u in range(SUBLANE):
            row_copy(slot, t, u, dest_ref[0, 0, t * SUBLANE + u]).start(priority=u % 2)
        return carry
    lax.fori_loop(0, groups, body, 0)

    @pl.when(i == n - 1)
    def _():
        wait_slot(slot)
        if n >= 2:
            wait_slot(1 - slot)


def moe_dispatch(h_all, row_offset, rows, gates, dest, into, tb, name):
    d = h_all.shape[1]
    off = row_offset // tb
    n = rows // tb
    body = functools.partial(_dispatch_body, tb=tb, n=n)
    return pl.pallas_call(
        body,
        out_shape=jax.ShapeDtypeStruct(into.shape, F32),
        grid=(n,),
        in_specs=[
            pl.BlockSpec((1, 1, tb), lambda i: (i, 0, 0), memory_space=pltpu.SMEM),
            pl.BlockSpec((tb, d), lambda i: (i + off, 0)),
            pl.BlockSpec((tb, LANE), lambda i: (i, 0)),
            pl.BlockSpec(memory_space=pl.ANY),
        ],
        out_specs=pl.BlockSpec(memory_space=pl.ANY),
        scratch_shapes=[pltpu.VMEM((2, tb // SUBLANE, SUBLANE, d + LANE), F32), pltpu.SemaphoreType.DMA((2,))],
        input_output_aliases={3: 0},
        compiler_params=_params("arbitrary"),
        name=name,
    )(dest.reshape(n, 1, tb), h_all, gates, into)


def _experts_body(te_ref, nu_ref, nb_ref, fill_ref, g_first_ref, g_next_ref, s_cur_ref,
                  h_hbm, g_ref, wg_ref, wu_ref, wo_ref, y_hbm,
                  xbuf, acc, xn_ref, gsem, ssem, *, tm, block, n_rows):
    i = pl.program_id(0)
    j = pl.program_id(1)
    n_tiles = pl.num_programs(0)
    nj = pl.num_programs(1)
    n_used = nu_ref[0]
    slot = i % 2
    valid = i < n_used
    chunks = tm // SUBLANE
    d = xn_ref.shape[1]

    def gather(idx_ref, s):
        def body(t, carry):
            row0 = pl.multiple_of(idx_ref[0, 0, t], SUBLANE)
            pltpu.make_async_copy(h_hbm.at[pl.ds(row0, SUBLANE), pl.ds(0, d)], xbuf.at[s, t], gsem.at[s]).start()
            return carry
        lax.fori_loop(0, chunks, body, 0)

    def wait_gather(s):
        def body(t, carry):
            pltpu.make_async_copy(h_hbm.at[pl.ds(0, SUBLANE), pl.ds(0, d)], xbuf.at[s, t], gsem.at[s]).wait()
            return carry
        lax.fori_loop(0, chunks, body, 0)

    def scatter():
        def body(t, carry):
            row0 = pl.multiple_of(s_cur_ref[0, 0, t], SUBLANE)
            pltpu.make_async_copy(acc.at[t], y_hbm.at[pl.ds(row0, SUBLANE)], ssem.at[0]).start()
            return carry
        lax.fori_loop(0, chunks, body, 0)

    def wait_scatter():
        def body(t, carry):
            pltpu.make_async_copy(acc.at[t], y_hbm.at[pl.ds(0, SUBLANE)], ssem.at[0]).wait()
            return carry
        lax.fori_loop(0, chunks, body, 0)

    @pl.when(j == 0)
    def _():
        @pl.when(i == 0)
        def _():
            gather(g_first_ref, 0)
            acc[...] = jnp.zeros_like(acc)
            first_free = nb_ref[0] * (block // SUBLANE)
            n_free = n_rows // SUBLANE - first_free

            def zero_copy(row0):
                return pltpu.make_async_copy(acc.at[0], y_hbm.at[pl.ds(pl.multiple_of(row0, SUBLANE), SUBLANE)],
                                             ssem.at[0])

            def fill(c, carry):
                zero_copy((first_free + c) * SUBLANE).start()
                zero_copy(n_rows + (first_free + c) * SUBLANE).start()
                return carry
            lax.fori_loop(0, n_free, fill, 0)

            def fill_spare(c, carry):
                zero_copy(2 * n_rows + c * SUBLANE).start()
                return carry
            lax.fori_loop(0, chunks, fill_spare, 0)

            def drain(c, carry):
                zero_copy(0).wait()
                return carry
            lax.fori_loop(0, 2 * n_free + chunks, drain, 0)

        @pl.when(i + 1 < n_used)
        def _():
            gather(g_next_ref, 1 - slot)

        @pl.when(valid)
        def _():
            wait_gather(slot)
            xn_ref[...] = _rms(xbuf[slot].reshape(tm, d), g_ref[...]).astype(BF16)

        @pl.when((i >= 1) & (i <= n_used))
        def _():
            wait_scatter()

        @pl.when(valid)
        def _():
            acc[...] = jnp.zeros_like(acc)

    def ff_step(n_chunks):
        rows = n_chunks * SUBLANE
        xn = xn_ref[0:rows, :]
        a = _dot(xn, wg_ref[0].astype(BF16))
        b = _dot(xn, wu_ref[0].astype(BF16))
        part = _dot((_silu(a) * b).astype(BF16), wo_ref[0].astype(BF16)).reshape(n_chunks, SUBLANE, d)
        acc[0:n_chunks] += part

    half_full = fill_ref[i] <= chunks // 2

    @pl.when(valid & jnp.logical_not(half_full))
    def _():
        ff_step(chunks)

    @pl.when(valid & half_full)
    def _():
        ff_step(chunks // 2)

    @pl.when(valid & (j == nj - 1))
    def _():
        scatter()

        @pl.when(i == n_tiles - 1)
        def _():
            wait_scatter()


def moe_experts(h_ps, g, wi, wo, gather_rows, scatter_rows, tile_expert, n_used, n_blocks_used, tile_fill,
                block, n_rows, tm, tf):
    d = wo.shape[2]
    f = wo.shape[1]
    nj = f // tf
    n_tiles = gather_rows.shape[0]
    chunks = tm // SUBLANE

    def wblock(col_off):
        def index_map(i, j, te, nu, nb, tf_):
            jj = jnp.where(i < nu[0], j, nj - 1)
            return (te[i], 0, col_off + jj)
        return index_map

    def woblock(i, j, te, nu, nb, tf_):
        return (te[i], jnp.where(i < nu[0], j, nj - 1), 0)

    def smem_tile(index_map):
        return pl.BlockSpec((1, 1, chunks), index_map, memory_space=pltpu.SMEM)

    grid_spec = pltpu.PrefetchScalarGridSpec(
        num_scalar_prefetch=4,
        grid=(n_tiles, nj),
        in_specs=[
            smem_tile(lambda i, j, te, nu, nb, tf_: (0, 0, 0)),
            smem_tile(lambda i, j, te, nu, nb, tf_: (jnp.minimum(i + 1, n_tiles - 1), 0, 0)),
            smem_tile(lambda i, j, te, nu, nb, tf_: (i, 0, 0)),
            pl.BlockSpec(memory_space=pl.ANY),
            pl.BlockSpec((1, d), lambda i, j, te, nu, nb, tf_: (0, 0)),
            pl.BlockSpec((1, d, tf), wblock(0)),
            pl.BlockSpec((1, d, tf), wblock(nj)),
            pl.BlockSpec((1, tf, d), woblock),
        ],
        out_specs=pl.BlockSpec(memory_space=pl.ANY),
        scratch_shapes=[
            pltpu.VMEM((2, chunks, SUBLANE, d), F32),
            pltpu.VMEM((chunks, SUBLANE, d), F32),
            pltpu.VMEM((tm, d), BF16),
            pltpu.SemaphoreType.DMA((2,)),
            pltpu.SemaphoreType.DMA((1,)),
        ],
    )
    return pl.pallas_call(
        functools.partial(_experts_body, tm=tm, block=block, n_rows=n_rows),
        out_shape=jax.ShapeDtypeStruct((2 * n_rows + tm, d), F32),
        grid_spec=grid_spec,
        compiler_params=_params("arbitrary", "arbitrary"),
        name="moe_experts",
    )(tile_expert, n_used, n_blocks_used, tile_fill, gather_rows, gather_rows, scatter_rows, h_ps,
      g.reshape(1, d), wi, wi, wo)


def _combine_body(nb_ref, hx_ref, y1_ref, y2_ref, fn_ref, o_ref, *, d, final_norm):
    @pl.when(pl.program_id(0) < nb_ref[0])
    def _():
        hx = hx_ref[...]
        out = hx[:, 0:d] + (hx[:, d:d + 1] * y1_ref[...] + hx[:, d + 1:d + 2] * y2_ref[...])
        if final_norm:
            out = _rms(out, fn_ref[...])
        o_ref[...] = out

    @pl.when(pl.program_id(0) >= nb_ref[0])
    def _():
        o_ref[...] = jnp.zeros_like(o_ref)


def moe_combine(h_ps, y, n_blocks_used, choice_stride, fn, tb, final_norm, name):
    rows, dx = h_ps.shape
    d = y.shape[1]
    off2 = choice_stride // tb

    def used(i, nb):
        return jnp.minimum(i, nb[0] - 1)

    grid_spec = pltpu.PrefetchScalarGridSpec(
        num_scalar_prefetch=1,
        grid=(rows // tb,),
        in_specs=[
            pl.BlockSpec((tb, dx), lambda i, nb: (used(i, nb), 0)),
            pl.BlockSpec((tb, d), lambda i, nb: (used(i, nb), 0)),
            pl.BlockSpec((tb, d), lambda i, nb: (used(i, nb) + off2, 0)),
            pl.BlockSpec((1, d), lambda i, nb: (0, 0)),
        ],
        out_specs=pl.BlockSpec((tb, d), lambda i, nb: (i, 0)),
    )
    return pl.pallas_call(
        functools.partial(_combine_body, d=d, final_norm=final_norm),
        out_shape=jax.ShapeDtypeStruct((rows, d), F32),
        grid_spec=grid_spec,
        compiler_params=_params("arbitrary"),
        name=name,
    )(n_blocks_used, h_ps, y, y, fn.reshape(1, d))


def _unpermute_body(cur_ref, nxt_ref, y_hbm, o_ref, buf, sem, *, tb):
    i = pl.program_id(0)
    n = pl.num_programs(0)
    slot = i % 2
    groups = tb // SUBLANE

    def start(idx_ref, s):
        def body(t, carry):
            for u in range(SUBLANE):
                row = idx_ref[0, 0, t * SUBLANE + u]
                pltpu.make_async_copy(y_hbm.at[pl.ds(row, 1)], buf.at[s, t, pl.ds(u, 1)], sem.at[s]).start(
                    priority=u % 2)
            return carry
        lax.fori_loop(0, groups, body, 0)

    @pl.when(i == 0)
    def _():
        start(cur_ref, 0)

    @pl.when(i + 1 < n)
    def _():
        start(nxt_ref, 1 - slot)

    def wait_body(t, carry):
        pltpu.make_async_copy(y_hbm.at[pl.ds(0, SUBLANE)], buf.at[slot, t], sem.at[slot]).wait()
        return carry
    lax.fori_loop(0, groups, wait_body, 0)
    o_ref[...] = buf[slot].reshape(o_ref.shape)


def moe_unpermute(y_ps, dest, tb, name):
    rows = dest.shape[0]
    d = y_ps.shape[1]
    n = rows // tb
    idx = dest.reshape(n, 1, tb)

    def smem_tile(index_map):
        return pl.BlockSpec((1, 1, tb), index_map, memory_space=pltpu.SMEM)

    return pl.pallas_call(
        functools.partial(_unpermute_body, tb=tb),
        out_shape=jax.ShapeDtypeStruct((rows, d), F32),
        grid=(n,),
        in_specs=[smem_tile(lambda i: (i, 0, 0)), smem_tile(lambda i: (jnp.minimum(i + 1, n - 1), 0, 0)),
                  pl.BlockSpec(memory_space=pl.ANY)],
        out_specs=pl.BlockSpec((tb, d), lambda i: (i, 0)),
        scratch_shapes=[pltpu.VMEM((2, tb // SUBLANE, SUBLANE, d), F32), pltpu.SemaphoreType.DMA((2,))],
        compiler_params=_params("arbitrary"),
        name=name,
    )(idx, idx, y_ps)


def moe_plan(idx, n_experts, tm, block):
    n_tok, k = idx.shape
    assert k == 2
    n_pairs = n_experts * n_experts
    chunks_per_tile = tm // SUBLANE
    n_rows = -(-(n_tok + n_pairs * (SUBLANE - 1)) // block) * block
    n_chunks = n_rows // SUBLANE

    pair = idx[:, 0] * n_experts + idx[:, 1]
    onehot = (pair[:, None] == jnp.arange(n_pairs, dtype=jnp.int32)[None, :]).astype(jnp.int32)
    csum = jnp.cumsum(onehot, axis=0)
    counts = csum[-1]
    rank = jnp.sum(csum * onehot, axis=1) - 1
    padded = (counts + SUBLANE - 1) // SUBLANE * SUBLANE
    total = jnp.sum(padded)
    last_group = jnp.max(jnp.where(counts > 0, jnp.arange(n_pairs, dtype=jnp.int32), 0))
    padded = padded + jnp.where(jnp.arange(n_pairs) == last_group, (-total) % block, 0)
    group_end = jnp.cumsum(padded)
    group_start = group_end - padded
    dest = jnp.sum(onehot * group_start[None, :], axis=1) + rank
    rows_used = group_end[-1]

    chunk_row = jnp.arange(n_chunks, dtype=jnp.int32) * SUBLANE
    chunk_group = jnp.sum((chunk_row[:, None] >= group_end[None, :]).astype(jnp.int32), axis=1)
    chunk_used = chunk_row < rows_used
    chunk_group = jnp.minimum(chunk_group, n_pairs - 1)
    chunk_experts = jnp.stack([chunk_group // n_experts, chunk_group % n_experts], axis=1)
    experts = jnp.arange(n_experts, dtype=jnp.int32)
    member = ((chunk_experts[:, :, None] == experts[None, None, :]) & chunk_used[:, None, None])
    member = member.astype(jnp.int32)
    member_e = jnp.sum(member, axis=1)
    csum_e = jnp.cumsum(member_e, axis=0)
    counts_e = csum_e[-1]
    tiles_e = (counts_e + chunks_per_tile - 1) // chunks_per_tile
    tile_end = jnp.cumsum(tiles_e)
    start_e = (tile_end - tiles_e) * chunks_per_tile
    pos = jnp.sum(member * (start_e[None, None, :] + csum_e[:, None, :] - 1), axis=2)
    n_tiles = 2 * n_chunks // chunks_per_tile + n_experts
    n_pos = n_tiles * chunks_per_tile
    pos = jnp.where(chunk_used[:, None], pos, n_pos)
    entry = 2 * jnp.arange(n_chunks, dtype=jnp.int32)[:, None] + jnp.arange(2, dtype=jnp.int32)[None, :] + 1
    table = jnp.zeros((n_pos,), jnp.int32).at[pos.reshape(-1)].set(entry.reshape(-1), mode="drop")
    table = table.reshape(n_tiles, chunks_per_tile)
    filled = table > 0
    src_chunk = (table - 1) // 2
    choice = (table - 1) % 2
    spare = 2 * n_rows + jnp.arange(chunks_per_tile, dtype=jnp.int32) * SUBLANE
    gather_rows = jnp.where(filled, src_chunk * SUBLANE, 0)
    scatter_rows = jnp.where(filled, choice * n_rows + src_chunk * SUBLANE, spare[None, :])

    n_used = tile_end[-1]
    tile_ids = jnp.arange(n_tiles, dtype=jnp.int32)
    tile_expert = jnp.sum((tile_ids[:, None] >= tile_end[None, :]).astype(jnp.int32), axis=1)
    last_expert = jnp.sum((n_used - 1 >= tile_end).astype(jnp.int32))
    tile_expert = jnp.where(tile_ids < n_used, tile_expert, last_expert).astype(jnp.int32)
    return (dest.astype(jnp.int32),
            gather_rows.reshape(n_tiles, 1, chunks_per_tile).astype(jnp.int32),
            scatter_rows.reshape(n_tiles, 1, chunks_per_tile).astype(jnp.int32),
            tile_expert, n_used.reshape(1).astype(jnp.int32),
            (rows_used // block).reshape(1).astype(jnp.int32),
            jnp.sum(filled.astype(jnp.int32), axis=1), n_rows)


def _pad_lanes(v):
    return jnp.zeros((1, LANE), F32).at[0, :v.shape[0]].set(v)


def kernel(x_prompt, x_sample, state_dn, state_dn_conv, state_pool, cache_mem_k, cache_mem_v, mem_prompt, a_norm1, a_w_in, a_conv_w, a_a_log, a_dt_bias, a_o_norm, a_w_out, a_norm2, a_ffn_wi, a_ffn_wo, b_norm1, b_w_in, b_pool_w, b_pool_scale, b_w_out, b_norm2, b_router, b_moe_wi, b_moe_wo, m_norm, m_w_k, m_w_v, final_norm):
    bp, seq, d = x_prompt.shape
    bs = x_sample.shape[0]
    assert x_sample.shape[1] == 1
    depth = m_norm.shape[0]
    assert depth % 2 == 0, "the output norm is fused into the expert block of the last (odd) layer"
    heads, dk, dv = state_dn.shape[2], state_dn.shape[3], state_dn.shape[4]
    conv_ch = state_dn_conv.shape[-1]
    qk_w, v_w = heads * dk, heads * dv
    assert conv_ch == 2 * qk_w + v_w and dk == LANE and dv == LANE and heads <= LANE
    n_mem, x_heads, x_dh = cache_mem_k.shape[2], cache_mem_k.shape[3], cache_mem_k.shape[4]
    xw = x_heads * x_dh
    pool_w_total = state_pool.shape[-1]
    pool_buf = state_pool.shape[2]
    d_ff = a_ffn_wo.shape[1]

    hp = x_prompt.reshape(bp * seq, d)
    hs = x_sample.reshape(bs, d)

    o1, o2, o3, o4 = conv_ch, conv_ch + v_w, conv_ch + v_w + heads, conv_ch + v_w + 2 * heads
    col_a = {"qkv": 0, "z": conv_ch, "cq": conv_ch + v_w, "b": conv_ch + v_w + xw, "a": conv_ch + v_w + xw + LANE}
    n_a = -(-(col_a["a"] + LANE) // 512) * 512

    dn_p, conv_p, pool_p, mk_all, mv_all = [], [], [], [], []
    dn_s, conv_s, pool_s = [], [], []
    for i in range(depth):
        j = i // 2
        w_kv = jnp.concatenate([m_w_k[i], m_w_v[i]], axis=1).astype(BF16)
        kv, mk, mv = mem_kv(mem_prompt.reshape(bp * n_mem, d), m_norm[i], w_kv, x_heads, x_dh, 256,
                            f"mem_kv_{i}")
        mk_all.append(mk.reshape(bp, n_mem, x_heads, x_dh))
        mv_all.append(mv.reshape(bp, n_mem, x_heads, x_dh))
        if i % 2 == 0:
            w = a_w_in[j]
            w_in = jnp.zeros((d, n_a), F32)
            w_in = w_in.at[:, :o2].set(w[:, :o2])
            w_in = w_in.at[:, col_a["cq"]:col_a["cq"] + xw].set(w[:, o4:])
            w_in = w_in.at[:, col_a["b"]:col_a["b"] + heads].set(w[:, o2:o3])
            w_in = w_in.at[:, col_a["a"]:col_a["a"] + heads].set(w[:, o3:o4])
            a_log, dt_b = _pad_lanes(a_a_log[j]), _pad_lanes(a_dt_bias[j])
            o_norm = a_o_norm[j].reshape(1, dv)

            proj_p = rms_matmul(hp, a_norm1[j], w_in.astype(BF16), 512, f"a_in_p_{i}")
            proj_s = rms_matmul(hs, a_norm1[j], w_in, bs, f"a_in_s_{i}", tn=512)
            mix_p, s_p = deltanet_prompt(proj_p, col_a, a_conv_w[j], a_log, dt_b, o_norm,
                                         bp, seq, heads, dk, dv, rows=DN_STEP_CHUNKS * CHUNK)
            mix_s, s_s, c_s = deltanet_sample(proj_s, col_a, state_dn_conv[j], state_dn[j], a_conv_w[j],
                                              a_log, dt_b, o_norm, heads, dk, dv, nb=SUBLANE)
            xa_p = xattn_prompt(proj_p, col_a["cq"], kv, bp, seq, n_mem, x_heads, x_dh, tq=min(1024, seq))
            xa_s = xattn_sample(proj_s, col_a["cq"], cache_mem_k, cache_mem_v, i, nb=SUBLANE)
            hp = out_proj_ffn(hp, mix_p, xa_p, a_w_out[j].astype(BF16), a_norm2[j], a_ffn_wi[j].astype(BF16),
                              a_ffn_wo[j].astype(BF16), 256, f"a_out_ffn_p_{i}")
            hs = out_proj(hs, mix_s, xa_s, a_w_out[j], bs, f"a_out_s_{i}")
            hs = ffn_dense(hs, a_norm2[j], a_ffn_wi[j], a_ffn_wo[j], bs, f"a_ffn_s_{i}", tf=256)
            dn_p.append(s_p)
            conv_p.append(proj_p.reshape(bp, seq, n_a)[:, seq - (CONV_WIDTH - 1):, :conv_ch])
            dn_s.append(s_s)
            conv_s.append(c_s)
        else:
            ps = b_pool_scale[j].reshape(1, pool_w_total)
            last = i == depth - 1

            mix_p, cq_p, pool_tail = proj_pool_prompt(hp, b_norm1[j], b_w_in[j].astype(BF16),
                                                      b_pool_w[j].astype(BF16), ps, bp, seq, tt=512)
            proj_s = rms_matmul(hs, b_norm1[j], b_w_in[j], bs, f"b_in_s_{i}")
            mix_s, buf_s = pool_sample(proj_s, state_pool[j], b_pool_w[j], ps)
            xa_p = xattn_prompt(cq_p, 0, kv, bp, seq, n_mem, x_heads, x_dh, tq=min(1024, seq))
            xa_s = xattn_sample(proj_s, pool_w_total, cache_mem_k, cache_mem_v, i, nb=SUBLANE)
            n_p = bp * seq
            h_all = out_proj(hp, mix_p, xa_p, b_w_out[j].astype(BF16), 512, f"b_out_p_{i}", out_rows=n_p + bs)
            h_all = out_proj(hs, mix_s, xa_s, b_w_out[j], bs, f"b_out_s_{i}", out_rows=n_p + bs,
                             into=h_all, row_offset=n_p)
            n_experts = b_router.shape[-1]
            r_pad = jnp.zeros((d, LANE), F32).at[:, :n_experts].set(b_router[j])
            idx_p, gate_p = moe_route(h_all, 0, n_p, b_norm2[j], r_pad.astype(BF16), 512, n_experts,
                                      f"b_route_p_{i}")
            idx_s, gate_s = moe_route(h_all, n_p, bs, b_norm2[j], r_pad, bs, n_experts, f"b_route_s_{i}")
            idx = jnp.concatenate([idx_p[:, :TOP_K], idx_s[:, :TOP_K]], axis=0)
            (dest, gather_rows, scatter_rows, tile_expert, n_used, n_blocks_used, tile_fill, n_rows) = moe_plan(
                idx, n_experts, MOE_TILE, MOE_ROW_BLOCK)
            h_ps = jnp.zeros((n_rows, d + LANE), F32)
            h_ps = moe_dispatch(h_all, 0, n_p, gate_p, dest[:n_p], h_ps, 512, f"b_dispatch_p_{i}")
            h_ps = moe_dispatch(h_all, n_p, bs, gate_s, dest[n_p:], h_ps, bs, f"b_dispatch_s_{i}")
            y = moe_experts(h_ps, b_norm2[j], b_moe_wi[j], b_moe_wo[j], gather_rows, scatter_rows, tile_expert,
                            n_used, n_blocks_used, tile_fill, MOE_ROW_BLOCK, n_rows, MOE_TILE, MOE_FF_TILE)
            y_ps = moe_combine(h_ps, y, n_blocks_used, n_rows, final_norm, MOE_ROW_BLOCK, last,
                               f"b_combine_{i}")
            hp = moe_unpermute(y_ps, dest[:n_p], 512, f"b_unpermute_p_{i}")
            hs = moe_unpermute(y_ps, dest[n_p:], bs, f"b_unpermute_s_{i}")
            pool_p.append(pool_tail[:, POOL_HALO - pool_buf:, :])
            pool_s.append(buf_s)

    y_prompt = hp.reshape(bp, seq, d)
    y_sample = hs.reshape(bs, 1, d)
    return (y_prompt, y_sample, jnp.stack(dn_p), jnp.stack(conv_p), jnp.stack(pool_p),
            jnp.stack(mk_all), jnp.stack(mv_all), jnp.stack(dn_s), jnp.stack(conv_s), jnp.stack(pool_s))
```

```python
import functools

import jax
import jax.numpy as jnp
from jax import lax
from jax.experimental import pallas as pl
from jax.experimental.pallas import tpu as pltpu

F32 = jnp.float32
BF16 = jnp.bfloat16
HIGHEST = lax.Precision.HIGHEST

EPS = 1e-6
CHUNK = 64
CONV_WIDTH = 4
POOL_WINDOWS = (2, 4, 8, 16)
PAST_LEN = 16384
TOP_K = 2

LANE = 128
SUBLANE = 8
VMEM_LIMIT_BYTES = 56 * 2**20

MOE_TILE = 1024
MOE_FF_TILE = 512
MOE_ROW_BLOCK = 512
DN_STEP_CHUNKS = 8
INV_BLOCK = 16
POOL_HALO = 32


def _params(*sem):
    return pltpu.CompilerParams(dimension_semantics=sem, vmem_limit_bytes=VMEM_LIMIT_BYTES)


def _resident(shape):
    nd = len(shape)
    return pl.BlockSpec(shape, lambda *_: (0,) * nd, pipeline_mode=pl.Buffered(1))


def _rms(x, w):
    return x * lax.rsqrt(jnp.mean(x * x, axis=-1, keepdims=True) + EPS) * w


def _silu(x):
    half = 0.5 * x
    return half + half * jnp.tanh(half)


def _dot(a, b):
    return jnp.dot(a, b, preferred_element_type=F32)


def _wdot(a, w):
    if w.dtype == F32:
        return jnp.dot(a, w, precision=HIGHEST, preferred_element_type=F32)
    return jnp.dot(a.astype(BF16), w, preferred_element_type=F32)


def _bdot(a, b):
    return lax.dot_general(a, b, (((2,), (1,)), ((0,), (0,))), preferred_element_type=F32)


def _bdot_nt(a, b):
    return lax.dot_general(a, b, (((2,), (2,)), ((0,), (0,))), preferred_element_type=F32)


def _bdot_tn(a, b):
    return lax.dot_general(a, b, (((1,), (1,)), ((0,), (0,))), preferred_element_type=F32)


def _rms_matmul_body(x_ref, g_ref, w_ref, o_ref):
    xn = _rms(x_ref[...], g_ref[...])
    o_ref[...] = _wdot(xn, w_ref[...])


def _wspec(shape, index_map, steps):
    if steps == 1:
        return pl.BlockSpec(shape, index_map, pipeline_mode=pl.Buffered(1))
    return pl.BlockSpec(shape, index_map)


def rms_matmul(x, g, w, tm, name, tn=None):
    m, d = x.shape
    n = w.shape[1]
    tn = n if tn is None else tn
    nj = n // tn
    return pl.pallas_call(
        _rms_matmul_body,
        out_shape=jax.ShapeDtypeStruct((m, n), F32),
        grid=(m // tm, nj),
        in_specs=[pl.BlockSpec((tm, d), lambda i, j: (i, 0)), _resident((1, d)),
                  _wspec((d, tn), lambda i, j: (0, j), nj)],
        out_specs=pl.BlockSpec((tm, tn), lambda i, j: (i, j)),
        compiler_params=_params("parallel", "arbitrary"),
        name=name,
    )(x, g.reshape(1, d), w)


def _out_proj_body(x_ref, a1_ref, a2_ref, w1_ref, w2_ref, *rest, n_main, n_experts, has_into):
    route = n_experts is not None
    if route:
        g_ref, r_ref = rest[0], rest[1]
        o_ref, idx_ref, gate_ref = rest[-3:]
    else:
        o_ref = rest[-1]
    del has_into

    @pl.when(pl.program_id(0) < n_main)
    def _():
        acc = _wdot(a1_ref[...], w1_ref[...]) + _wdot(a2_ref[...], w2_ref[...])
        h = x_ref[...] + acc
        o_ref[...] = h
        if route:
            idx_ref[...], gate_ref[...] = _route_top2(h, g_ref[...], r_ref[...], n_experts)

    @pl.when(pl.program_id(0) >= n_main)
    def _():
        o_ref[...] = jnp.zeros_like(o_ref)
        if route:
            idx_ref[...] = jnp.zeros_like(idx_ref)
            gate_ref[...] = jnp.zeros_like(gate_ref)


def out_proj(x, a1, a2, w, tm, name, out_rows=None, into=None, row_offset=0, route=None):
    m, d = x.shape
    k1, k2 = a1.shape[1], a2.shape[1]
    assert w.shape[0] == k1 + k2 and k1 % k2 == 0 and row_offset % tm == 0
    out_rows = m if out_rows is None else out_rows
    off = row_offset // tm
    n_main = m // tm
    n_extra = 0 if (into is not None or out_rows == m) else 1
    assert out_rows - m <= n_extra * tm or into is not None
    last = n_main - 1
    in_specs = [
        pl.BlockSpec((tm, d), lambda i: (jnp.minimum(i, last), 0)),
        pl.BlockSpec((tm, k1), lambda i: (jnp.minimum(i, last), 0)),
        pl.BlockSpec((tm, k2), lambda i: (jnp.minimum(i, last), 0)),
        pl.BlockSpec((k1, d), lambda i: (0, 0), pipeline_mode=pl.Buffered(1)),
        pl.BlockSpec((k2, d), lambda i: (k1 // k2, 0), pipeline_mode=pl.Buffered(1)),
    ]
    args = [x, a1, a2, w, w]
    out_shape = [jax.ShapeDtypeStruct((out_rows, d), F32)]
    out_specs = [pl.BlockSpec((tm, d), lambda i: (i + off, 0))]
    n_experts = None
    if route is not None:
        g, r_pad, n_experts = route
        in_specs += [_resident((1, d)), _resident((d, LANE))]
        args += [g.reshape(1, d), r_pad]
        n_steps = n_main + n_extra
        out_shape += [jax.ShapeDtypeStruct((n_steps * tm, LANE), jnp.int32),
                      jax.ShapeDtypeStruct((n_steps * tm, LANE), F32)]
        out_specs += [pl.BlockSpec((tm, LANE), lambda i: (i, 0))] * 2
    aliases = {}
    if into is not None:
        in_specs.append(pl.BlockSpec(memory_space=pl.ANY))
        args.append(into)
        aliases = {len(args) - 1: 0}
    outs = pl.pallas_call(
        functools.partial(_out_proj_body, n_main=n_main, n_experts=n_experts, has_into=into is not None),
        out_shape=tuple(out_shape),
        grid=(n_main + n_extra,),
        in_specs=in_specs,
        out_specs=tuple(out_specs),
        input_output_aliases=aliases,
        compiler_params=_params("parallel"),
        name=name,
    )(*args)
    return outs if route is not None else outs[0]


def _ffn_body(h_ref, g_ref, wg_ref, wu_ref, wo_ref, o_ref, xn_ref):
    @pl.when(pl.program_id(1) == 0)
    def _():
        h = h_ref[...]
        xn_ref[...] = _rms(h, g_ref[...])
        o_ref[...] = h

    xn = xn_ref[...]
    a = _wdot(xn, wg_ref[...])
    b = _wdot(xn, wu_ref[...])
    o_ref[...] += _wdot(_silu(a) * b, wo_ref[...])


def ffn_dense(h, g, wi, wo, tm, name, tf=None):
    m, d = h.shape
    f = wo.shape[0]
    tf = f if tf is None else tf
    nj = f // tf
    return pl.pallas_call(
        _ffn_body,
        out_shape=jax.ShapeDtypeStruct((m, d), F32),
        grid=(m // tm, nj),
        in_specs=[
            pl.BlockSpec((tm, d), lambda i, j: (i, 0)),
            _resident((1, d)),
            _wspec((d, tf), lambda i, j: (0, j), nj),
            _wspec((d, tf), lambda i, j: (0, nj + j), nj),
            _wspec((tf, d), lambda i, j: (j, 0), nj),
        ],
        out_specs=pl.BlockSpec((tm, d), lambda i, j: (i, 0)),
        scratch_shapes=[pltpu.VMEM((tm, d), F32)],
        compiler_params=_params("parallel", "arbitrary"),
        name=name,
    )(h, g.reshape(1, d), wi, wi, wo)


def _mem_kv_body(x_ref, g_ref, w_ref, kv_ref, k_ref, v_ref, *, heads, dh):
    xw = heads * dh
    res = _wdot(_rms(x_ref[...], g_ref[...]), w_ref[...])
    kv_ref[...] = res
    for h in range(heads):
        k_ref[:, h, :] = res[:, h * dh:(h + 1) * dh]
        v_ref[:, h, :] = res[:, xw + h * dh:xw + (h + 1) * dh]


def mem_kv(mem, g, w_kv, heads, dh, tm, name):
    m, d = mem.shape
    xw = heads * dh
    head_shape = jax.ShapeDtypeStruct((m, heads, dh), F32)
    head_spec = pl.BlockSpec((tm, heads, dh), lambda i: (i, 0, 0))
    return pl.pallas_call(
        functools.partial(_mem_kv_body, heads=heads, dh=dh),
        out_shape=(jax.ShapeDtypeStruct((m, 2 * xw), F32), head_shape, head_shape),
        grid=(m // tm,),
        in_specs=[pl.BlockSpec((tm, d), lambda i: (i, 0)), _resident((1, d)), _resident((d, 2 * xw))],
        out_specs=(pl.BlockSpec((tm, 2 * xw), lambda i: (i, 0)), head_spec, head_spec),
        compiler_params=_params("parallel"),
        name=name,
    )(mem, g.reshape(1, d), w_kv)


def _out_proj_ffn_body(x_ref, a1_ref, a2_ref, w1_ref, w2_ref, g_ref, wg_ref, wu_ref, wo_ref, o_ref):
    h = x_ref[...] + (_wdot(a1_ref[...], w1_ref[...]) + _wdot(a2_ref[...], w2_ref[...]))
    xn = _rms(h, g_ref[...])
    a = _wdot(xn, wg_ref[...])
    b = _wdot(xn, wu_ref[...])
    o_ref[...] = h + _wdot(_silu(a) * b, wo_ref[...])


def out_proj_ffn(x, a1, a2, w, g, wi, wo, tm, name):
    m, d = x.shape
    k1, k2 = a1.shape[1], a2.shape[1]
    f = wo.shape[0]
    assert w.shape[0] == k1 + k2 and k1 % k2 == 0
    return pl.pallas_call(
        _out_proj_ffn_body,
        out_shape=jax.ShapeDtypeStruct((m, d), F32),
        grid=(m // tm,),
        in_specs=[
            pl.BlockSpec((tm, d), lambda i: (i, 0)),
            pl.BlockSpec((tm, k1), lambda i: (i, 0)),
            pl.BlockSpec((tm, k2), lambda i: (i, 0)),
            pl.BlockSpec((k1, d), lambda i: (0, 0), pipeline_mode=pl.Buffered(1)),
            pl.BlockSpec((k2, d), lambda i: (k1 // k2, 0), pipeline_mode=pl.Buffered(1)),
            _resident((1, d)),
            pl.BlockSpec((d, f), lambda i: (0, 0), pipeline_mode=pl.Buffered(1)),
            pl.BlockSpec((d, f), lambda i: (0, 1), pipeline_mode=pl.Buffered(1)),
            _resident((f, d)),
        ],
        out_specs=pl.BlockSpec((tm, d), lambda i: (i, 0)),
        compiler_params=_params("parallel"),
        name=name,
    )(x, a1, a2, w, w, g.reshape(1, d), wi, wi, wo)


def _pair_blockdiag(y, left):
    yb = y.astype(BF16)
    zero = jnp.zeros_like(yb)
    return jnp.concatenate([jnp.where(left, yb, zero), jnp.where(left, zero, yb)], axis=1)


def _pair_mm(x, y, left):
    return _bdot(x.astype(BF16), _pair_blockdiag(y, left))


def _unit_lower_inverse(mx, eye, blk, left):
    mm = functools.partial(_pair_mm, left=left)
    md = jnp.where(blk, mx, 0.0)
    c = mx - md
    p = eye - md
    m2 = mm(md, md)
    p = p + mm(p, m2)
    m4 = mm(m2, m2)
    p = p + mm(p, m4)
    m8 = mm(m4, m4)
    td = p + mm(p, m8)
    n = mm(td, c)
    n2 = mm(n, n)
    q = eye - n
    q = q + mm(q, n2)
    return mm(q, td)


def _dn_prompt_body(qkv_ref, z_ref, bl_ref, al_ref, cw_ref, alog_ref, dtb_ref, on_ref,
                    dn_ref, sfin_ref, s_ref, xbuf_ref, act_ref, *, rows, heads, dk, dv):
    c = pl.program_id(1)
    wq = heads * dk

    @pl.when(c == 0)
    def _():
        s_ref[...] = jnp.zeros_like(s_ref)
        xbuf_ref[0:SUBLANE, :] = jnp.zeros((SUBLANE, xbuf_ref.shape[1]), F32)

    xbuf_ref[SUBLANE:SUBLANE + rows, :] = qkv_ref[...]
    first = SUBLANE - (CONV_WIDTH - 1)
    y = xbuf_ref[pl.ds(first, rows), :] * cw_ref[0:1, :]
    for j in range(1, CONV_WIDTH):
        y = y + xbuf_ref[pl.ds(first + j, rows), :] * cw_ref[j:j + 1, :]
    act_ref[...] = _silu(y)
    xbuf_ref[0:SUBLANE, :] = xbuf_ref[rows:rows + SUBLANE, :]

    pairs = heads // 2
    ri = lax.broadcasted_iota(jnp.int32, (CHUNK, 2 * CHUNK), 0)
    lane = lax.broadcasted_iota(jnp.int32, (CHUNK, 2 * CHUNK), 1)
    cj = lane % CHUNK
    left = (lane < CHUNK)[None]
    incl = (ri >= cj)[None]
    strict = (ri > cj)[None]
    blk = ((ri // INV_BLOCK) == (cj // INV_BLOCK))[None]
    eye = (ri == cj).astype(F32)[None]
    tr = lax.broadcasted_iota(jnp.int32, (CHUNK, CHUNK), 0)
    tc = lax.broadcasted_iota(jnp.int32, (CHUNK, CHUNK), 1)
    tri = (tr >= tc).astype(F32)

    n_chunks = rows // CHUNK
    zero = jnp.zeros((CHUNK, dk), BF16)

    def chunk_rows(cc):
        return slice(cc * CHUNK, (cc + 1) * CHUNK)

    def per_head(fn):
        return jnp.stack([fn(chunk_rows(cc), h) for cc in range(n_chunks) for h in range(heads)], axis=0)

    def per_pair(fn):
        return jnp.stack([fn(cc, p) for cc in range(n_chunks) for p in range(pairs)], axis=0)

    q3 = per_head(lambda rs, h: act_ref[rs, h * dk:(h + 1) * dk])
    k3 = per_head(lambda rs, h: act_ref[rs, wq + h * dk:wq + (h + 1) * dk])
    v3 = per_head(lambda rs, h: act_ref[rs, 2 * wq + h * dv:2 * wq + (h + 1) * dv])
    qn = q3 * lax.rsqrt(jnp.sum(q3 * q3, axis=-1, keepdims=True) + EPS) * (dk ** -0.5)
    kn = k3 * lax.rsqrt(jnp.sum(k3 * k3, axis=-1, keepdims=True) + EPS)

    beta = jax.nn.sigmoid(bl_ref[...])
    log_decay = -jnp.exp(alog_ref[...]) * jax.nn.softplus(al_ref[...] + dtb_ref[...])
    gcum = [jnp.dot(tri, log_decay[chunk_rows(cc)], precision=HIGHEST, preferred_element_type=F32)
            for cc in range(n_chunks)]
    gcum_t = [jnp.concatenate([gc_, pltpu.roll(gc_, LANE - 1, 1)], axis=0).T for gc_ in gcum]

    def pair_cols(x_of):
        return per_pair(lambda cc, p: jnp.where(left[0], x_of(cc)[:, 2 * p:2 * p + 1],
                                                x_of(cc)[:, 2 * p + 1:2 * p + 2]))

    gc2 = pair_cols(lambda cc: gcum[cc])
    bc2 = pair_cols(lambda cc: beta[chunk_rows(cc)])
    gr2 = per_pair(lambda cc, p: gcum_t[cc][2 * p:2 * p + 1, :])
    decay = jnp.where(incl, jnp.exp(gc2 - gr2), 0.0)

    kb = kn.astype(BF16)
    qb = qn.astype(BF16)

    def pair_of(x, cc, p):
        return x[cc * heads + 2 * p], x[cc * heads + 2 * p + 1]

    k_lhs = per_pair(lambda cc, p: jnp.concatenate(pair_of(kb, cc, p), axis=1))
    q_lhs = per_pair(lambda cc, p: jnp.concatenate(pair_of(qb, cc, p), axis=1))
    k_bd = per_pair(lambda cc, p: jnp.concatenate(
        [jnp.concatenate([pair_of(kb, cc, p)[0], zero], axis=1),
         jnp.concatenate([zero, pair_of(kb, cc, p)[1]], axis=1)], axis=0))
    kk = _bdot_nt(k_lhs, k_bd)
    mx = jnp.where(strict, bc2 * kk * decay, 0.0)
    tinv = _unit_lower_inverse(mx, eye, blk, left)

    gcum_rows = jnp.concatenate(gcum, axis=0)
    gc = per_head(lambda rs, h: gcum_rows[rs, h:h + 1])
    bc = per_head(lambda rs, h: beta[rs, h:h + 1])
    eg = jnp.exp(gc)
    rhs = jnp.concatenate([v3 * bc, kn * (bc * eg)], axis=-1)
    sol = _bdot(_pair_blockdiag(tinv, left), rhs.astype(BF16).reshape(n_chunks * pairs, 2 * CHUNK, dv + dk))
    sol = sol.reshape(n_chunks * heads, CHUNK, dv + dk)
    w_val = sol[..., :dv]
    k_cd = sol[..., dv:].astype(BF16)
    qk_bd = _pair_blockdiag(_bdot_nt(q_lhs, k_bd) * decay, left)
    q_dec = (qn * eg).astype(BF16)
    g_last = gc[:, CHUNK - 1:CHUNK, :]
    k_tail = (kn * jnp.exp(g_last - gc)).astype(BF16)
    c_decay = jnp.exp(g_last)

    for cc in range(n_chunks):
        hs = slice(cc * heads, (cc + 1) * heads)
        ps = slice(cc * pairs, (cc + 1) * pairs)
        rs = chunk_rows(cc)
        s = s_ref[...]
        sb = s.astype(BF16)
        u = w_val[hs] - _bdot(k_cd[hs], sb)
        ub = u.astype(BF16)
        intra = _bdot(qk_bd[ps], ub.reshape(pairs, 2 * CHUNK, dv))
        o = _bdot(q_dec[hs], sb) + intra.reshape(heads, CHUNK, dv)
        s_ref[...] = s * c_decay[hs] + _bdot_tn(k_tail[hs], ub)

        o = o * lax.rsqrt(jnp.mean(o * o, axis=-1, keepdims=True) + EPS) * on_ref[...]
        for h in range(heads):
            z = z_ref[rs, h * dv:(h + 1) * dv]
            dn_ref[rs, h * dv:(h + 1) * dv] = o[h] * _silu(z)

    @pl.when(c == pl.num_programs(1) - 1)
    def _():
        sfin_ref[0] = s_ref[...]


def deltanet_prompt(proj, col, conv_w, a_log, dt_bias, o_norm, batch, seq, heads, dk, dv, rows):
    conv_ch = 2 * heads * dk + heads * dv
    zw = heads * dv
    steps = seq // rows

    def row_map(width_off):
        return lambda b, c: (b * steps + c, width_off)

    body = functools.partial(_dn_prompt_body, rows=rows, heads=heads, dk=dk, dv=dv)
    return pl.pallas_call(
        body,
        out_shape=(jax.ShapeDtypeStruct((batch * seq, zw), F32),
                   jax.ShapeDtypeStruct((batch, heads, dk, dv), F32)),
        grid=(batch, steps),
        in_specs=[
            pl.BlockSpec((rows, conv_ch), row_map(col["qkv"] // conv_ch)),
            pl.BlockSpec((rows, zw), row_map(col["z"] // zw)),
            pl.BlockSpec((rows, LANE), row_map(col["b"] // LANE)),
            pl.BlockSpec((rows, LANE), row_map(col["a"] // LANE)),
            _resident((CONV_WIDTH, conv_ch)),
            _resident((1, LANE)),
            _resident((1, LANE)),
            _resident((1, dv)),
        ],
        out_specs=(pl.BlockSpec((rows, zw), lambda b, c: (b * steps + c, 0)),
                   pl.BlockSpec((1, heads, dk, dv), lambda b, c: (b, 0, 0, 0))),
        scratch_shapes=[
            pltpu.VMEM((heads, dk, dv), F32),
            pltpu.VMEM((rows + SUBLANE, conv_ch), F32),
            pltpu.VMEM((rows, conv_ch), F32),
        ],
        compiler_params=_params("parallel", "arbitrary"),
        name="deltanet_prompt",
    )(proj, proj, proj, proj, conv_w, a_log, dt_bias, o_norm)


def _dn_sample_body(qkv_ref, z_ref, bl_ref, al_ref, cbuf_ref, st_ref, cw_ref, alog_ref, dtb_ref, on_ref,
                    dn_ref, sto_ref, cbo_ref, act_ref, *, nb, heads, dk, dv):
    wq = heads * dk
    new = qkv_ref[...]
    y = cbuf_ref[:, 0, :] * cw_ref[0:1, :]
    for j in range(1, CONV_WIDTH - 1):
        y = y + cbuf_ref[:, j, :] * cw_ref[j:j + 1, :]
    y = y + new * cw_ref[CONV_WIDTH - 1:CONV_WIDTH, :]
    act_ref[...] = _silu(y)
    for j in range(CONV_WIDTH - 2):
        cbo_ref[:, j, :] = cbuf_ref[:, j + 1, :]
    cbo_ref[:, CONV_WIDTH - 2, :] = new

    beta = jax.nn.sigmoid(bl_ref[...])
    g = -jnp.exp(alog_ref[...]) * jax.nn.softplus(al_ref[...] + dtb_ref[...])
    dec = jnp.exp(g)

    q_t, k_t = [], []
    for h in range(heads):
        q = act_ref[:, h * dk:(h + 1) * dk]
        k = act_ref[:, wq + h * dk:wq + (h + 1) * dk]
        qn = q * lax.rsqrt(jnp.sum(q * q, axis=-1, keepdims=True) + EPS) * (dk ** -0.5)
        kn = k * lax.rsqrt(jnp.sum(k * k, axis=-1, keepdims=True) + EPS)
        q_t.append(qn.T)
        k_t.append(kn.T)

    def per_head(fn):
        return jnp.stack([fn(h) for h in range(heads)], axis=0)

    for i in range(nb):
        row = slice(i, i + 1)
        kcol = per_head(lambda h: k_t[h][:, i:i + 1])
        qcol = per_head(lambda h: q_t[h][:, i:i + 1])
        v = per_head(lambda h: act_ref[row, 2 * wq + h * dv:2 * wq + (h + 1) * dv])
        z = per_head(lambda h: z_ref[row, h * dv:(h + 1) * dv])
        b_i = per_head(lambda h: beta[row, h:h + 1])
        d_i = per_head(lambda h: dec[row, h:h + 1])
        s = st_ref[i] * d_i
        u = b_i * (v - jnp.sum(kcol * s, axis=1, keepdims=True))
        s = s + kcol * u
        o = jnp.sum(qcol * s, axis=1, keepdims=True)
        sto_ref[i] = s
        o = o * lax.rsqrt(jnp.mean(o * o, axis=-1, keepdims=True) + EPS) * on_ref[...]
        o = o * _silu(z)
        for h in range(heads):
            dn_ref[row, h * dv:(h + 1) * dv] = o[h]


def deltanet_sample(proj, col, conv_buf, state, conv_w, a_log, dt_bias, o_norm, heads, dk, dv, nb):
    n_seq = proj.shape[0]
    conv_ch = 2 * heads * dk + heads * dv
    zw = heads * dv
    body = functools.partial(_dn_sample_body, nb=nb, heads=heads, dk=dk, dv=dv)
    return pl.pallas_call(
        body,
        out_shape=(jax.ShapeDtypeStruct((n_seq, zw), F32),
                   jax.ShapeDtypeStruct(state.shape, F32),
                   jax.ShapeDtypeStruct(conv_buf.shape, F32)),
        grid=(n_seq // nb,),
        in_specs=[
            pl.BlockSpec((nb, conv_ch), lambda i: (i, col["qkv"] // conv_ch)),
            pl.BlockSpec((nb, zw), lambda i: (i, col["z"] // zw)),
            pl.BlockSpec((nb, LANE), lambda i: (i, col["b"] // LANE)),
            pl.BlockSpec((nb, LANE), lambda i: (i, col["a"] // LANE)),
            pl.BlockSpec((nb, CONV_WIDTH - 1, conv_ch), lambda i: (i, 0, 0)),
            pl.BlockSpec((nb, heads, dk, dv), lambda i: (i, 0, 0, 0)),
            _resident((CONV_WIDTH, conv_ch)),
            _resident((1, LANE)),
            _resident((1, LANE)),
            _resident((1, dv)),
        ],
        out_specs=(pl.BlockSpec((nb, zw), lambda i: (i, 0)),
                   pl.BlockSpec((nb, heads, dk, dv), lambda i: (i, 0, 0, 0)),
                   pl.BlockSpec((nb, CONV_WIDTH - 1, conv_ch), lambda i: (i, 0, 0))),
        scratch_shapes=[
            pltpu.VMEM((nb, conv_ch), F32),
        ],
        compiler_params=_params("parallel"),
        name="deltanet_sample",
    )(proj, proj, proj, proj, conv_buf, state, conv_w, a_log, dt_bias, o_norm)


def _xattn_prompt_body(q_ref, k_ref, v_ref, o_ref, *, heads, dh):
    for h in range(heads):
        sl = slice(h * dh, (h + 1) * dh)
        q = q_ref[:, sl].astype(BF16)
        k = k_ref[:, sl].astype(BF16)
        v = v_ref[:, sl].astype(BF16)
        s = lax.dot_general(q, k, (((1,), (1,)), ((), ())), preferred_element_type=F32) * (dh ** -0.5)
        e = jnp.exp(s - jnp.max(s, axis=-1, keepdims=True))
        p = e / jnp.sum(e, axis=-1, keepdims=True)
        o_ref[:, sl] = _dot(p.astype(BF16), v)


def xattn_prompt(proj, cq_col, kv, batch, seq, n_mem, heads, dh, tq):
    xw = heads * dh
    steps = seq // tq
    body = functools.partial(_xattn_prompt_body, heads=heads, dh=dh)
    return pl.pallas_call(
        body,
        out_shape=jax.ShapeDtypeStruct((batch * seq, xw), F32),
        grid=(batch, steps),
        in_specs=[
            pl.BlockSpec((tq, xw), lambda b, i: (b * steps + i, cq_col // xw)),
            pl.BlockSpec((n_mem, xw), lambda b, i: (b, 0)),
            pl.BlockSpec((n_mem, xw), lambda b, i: (b, 1)),
        ],
        out_specs=pl.BlockSpec((tq, xw), lambda b, i: (b * steps + i, 0)),
        compiler_params=_params("parallel", "parallel"),
        name="xattn_prompt",
    )(proj, kv, kv)


def _xattn_sample_body(q_ref, k_ref, v_ref, o_ref, *, nb, heads, dh):
    rep = SUBLANE // heads
    n_t = k_ref.shape[2] // SUBLANE

    def over_groups(x, op):
        out = x
        for t in range(1, rep):
            out = op(out, pltpu.roll(x, t * heads, 0))
        return out

    for i in range(nb):
        row = slice(i, i + 1)
        q8 = jnp.concatenate([q_ref[row, h * dh:(h + 1) * dh] for h in range(heads)] * rep, axis=0)
        k3 = k_ref[0, i].reshape(n_t, SUBLANE, dh)
        v3 = v_ref[0, i].reshape(n_t, SUBLANE, dh)
        s = jnp.sum(k3 * q8[None], axis=-1, keepdims=True) * (dh ** -0.5)
        m = over_groups(jnp.max(s, axis=0), jnp.maximum)
        e = jnp.exp(s - m[None])
        den = over_groups(jnp.sum(e, axis=0), jnp.add)
        p = e / den[None]
        acc = over_groups(jnp.sum(p * v3, axis=0), jnp.add)
        for h in range(heads):
            o_ref[row, h * dh:(h + 1) * dh] = acc[h:h + 1, :]


def xattn_sample(proj, cq_col, cache_k, cache_v, layer, nb):
    n_layers, n_seq, n_mem, heads, dh = cache_k.shape
    assert SUBLANE % heads == 0 and (n_mem * heads) % SUBLANE == 0
    xw = heads * dh
    body = functools.partial(_xattn_sample_body, nb=nb, heads=heads, dh=dh)
    cache_spec = pl.BlockSpec((1, nb, n_mem * heads, dh), lambda i: (layer, i, 0, 0))
    flat = (n_layers, n_seq, n_mem * heads, dh)
    return pl.pallas_call(
        body,
        out_shape=jax.ShapeDtypeStruct((n_seq, xw), F32),
        grid=(n_seq // nb,),
        in_specs=[pl.BlockSpec((nb, xw), lambda i: (i, cq_col // xw)), cache_spec, cache_spec],
        out_specs=pl.BlockSpec((nb, xw), lambda i: (i, 0)),
        compiler_params=_params("parallel"),
        name="xattn_sample",
    )(proj, cache_k.reshape(flat), cache_v.reshape(flat))


def _pool_mix(sums, cnts, u_of, pw_ref, ps_ref, o_ref, gw):
    for gi in range(len(POOL_WINDOWS)):
        cols = slice(gi * gw, (gi + 1) * gw)
        d = sums[gi] / cnts[gi] - u_of(cols)
        o_ref[:, cols] = _wdot(d, pw_ref[gi]) * ps_ref[:, cols]


def _pool_prompt_body(x_ref, g_ref, w_ref, pw_ref, ps_ref, o_ref, cq_ref, tail_ref, xbuf, s2, s4, s8, *, tt, gw):
    t = pl.program_id(1)
    n = POOL_HALO + tt
    pw_total = xbuf.shape[1]

    @pl.when(t == 0)
    def _():
        xbuf[0:POOL_HALO, :] = jnp.zeros((POOL_HALO, pw_total), F32)

    proj = _wdot(_rms(x_ref[...], g_ref[...]), w_ref[...])
    cq_ref[...] = proj[:, pw_total:]
    xbuf[POOL_HALO:n, :] = proj[:, 0:pw_total]
    s2[8:n, :] = xbuf[8:n, :] + xbuf[7:n - 1, :]
    s4[16:n, :] = s2[16:n, gw:] + s2[14:n - 2, gw:]
    s8[24:n, :] = s4[24:n, gw:] + s4[20:n - 4, gw:]
    s16 = s8[32:n, gw:] + s8[24:n - 8, gw:]
    sums = [s2[POOL_HALO:n, 0:gw], s4[POOL_HALO:n, 0:gw], s8[POOL_HALO:n, 0:gw], s16]
    pos = t * tt + lax.broadcasted_iota(jnp.int32, (tt, 1), 0)
    cnts = [jnp.minimum(pos + 1, w).astype(F32) for w in POOL_WINDOWS]
    _pool_mix(sums, cnts, lambda cols: xbuf[POOL_HALO:n, cols], pw_ref, ps_ref, o_ref, gw)
    tail_ref[0] = xbuf[tt:n, :]
    xbuf[0:POOL_HALO, :] = xbuf[tt:n, :]


def proj_pool_prompt(x, g, w, pool_w, pool_scale, batch, seq, tt):
    m, d = x.shape
    n_out = w.shape[1]
    pw_total = pool_scale.shape[-1]
    gw = pw_total // len(POOL_WINDOWS)
    steps = seq // tt
    n = POOL_HALO + tt
    body = functools.partial(_pool_prompt_body, tt=tt, gw=gw)
    return pl.pallas_call(
        body,
        out_shape=(jax.ShapeDtypeStruct((m, pw_total), F32),
                   jax.ShapeDtypeStruct((m, n_out - pw_total), F32),
                   jax.ShapeDtypeStruct((batch, POOL_HALO, pw_total), F32)),
        grid=(batch, steps),
        in_specs=[
            pl.BlockSpec((tt, d), lambda b, t: (b * steps + t, 0)),
            _resident((1, d)),
            _resident((d, n_out)),
            _resident(pool_w.shape),
            _resident((1, pw_total)),
        ],
        out_specs=(pl.BlockSpec((tt, pw_total), lambda b, t: (b * steps + t, 0)),
                   pl.BlockSpec((tt, n_out - pw_total), lambda b, t: (b * steps + t, 0)),
                   pl.BlockSpec((1, POOL_HALO, pw_total), lambda b, t: (b, 0, 0))),
        scratch_shapes=[
            pltpu.VMEM((n, pw_total), F32),
            pltpu.VMEM((n, pw_total), F32),
            pltpu.VMEM((n, pw_total - gw), F32),
            pltpu.VMEM((n, pw_total - 2 * gw), F32),
        ],
        compiler_params=_params("parallel", "arbitrary"),
        name="proj_pool_prompt",
    )(x, g.reshape(1, d), w, pool_w, pool_scale)


def _pool_sample_body(u_ref, st_ref, pw_ref, ps_ref, o_ref, sto_ref, *, gw, buf):
    pw_total = gw * len(POOL_WINDOWS)
    new = u_ref[:, 0:pw_total]
    sums, cnts = [], []
    for gi, w in enumerate(POOL_WINDOWS):
        cols = slice(gi * gw, (gi + 1) * gw)
        s = new[:, cols]
        for i in range(1, w):
            s = s + st_ref[:, buf - i, cols]
        sums.append(s)
        cnts.append(float(min(PAST_LEN + 1, w)))
    _pool_mix(sums, cnts, lambda cols: new[:, cols], pw_ref, ps_ref, o_ref, gw)
    sto_ref[:, 0:buf - 1, :] = st_ref[:, 1:buf, :]
    sto_ref[:, buf - 1, :] = new


def pool_sample(proj, state, pool_w, pool_scale):
    n_seq, buf, pw_total = state.shape
    gw = pw_total // len(POOL_WINDOWS)
    body = functools.partial(_pool_sample_body, gw=gw, buf=buf)
    return pl.pallas_call(
        body,
        out_shape=(jax.ShapeDtypeStruct((n_seq, pw_total), F32),
                   jax.ShapeDtypeStruct(state.shape, F32)),
        compiler_params=pltpu.CompilerParams(vmem_limit_bytes=VMEM_LIMIT_BYTES),
        name="pool_sample",
    )(proj, state, pool_w, pool_scale)


def _route_top2(h, g, r, n_experts):
    logits = _wdot(_rms(h, g), r)
    lane = lax.broadcasted_iota(jnp.int32, logits.shape, 1)
    lm = jnp.where(lane < n_experts, logits, -jnp.inf)
    m1 = jnp.max(lm, axis=1, keepdims=True)
    i1 = jnp.min(jnp.where(lm == m1, lane, LANE), axis=1, keepdims=True)
    lm2 = jnp.where(lane == i1, -jnp.inf, lm)
    m2 = jnp.max(lm2, axis=1, keepdims=True)
    i2 = jnp.min(jnp.where(lm2 == m2, lane, LANE), axis=1, keepdims=True)
    e2 = jnp.exp(m2 - m1)
    den = 1.0 + e2
    idx = jnp.where(lane == 0, i1, jnp.where(lane == 1, i2, 0))
    gate = jnp.where(lane == 0, 1.0 / den, jnp.where(lane == 1, e2 / den, 0.0))
    return idx, gate


def _dispatch_body(dest_ref, h_ref, gate_ref, zeros_hbm, o_hbm, ext, sem, *, tb, n):
    del zeros_hbm
    i = pl.program_id(0)
    slot = i % 2
    groups = tb // SUBLANE
    d = h_ref.shape[1]

    def row_copy(s, t, u, row):
        return pltpu.make_async_copy(ext.at[s, t, pl.ds(u, 1)], o_hbm.at[pl.ds(row, 1)], sem.at[s])

    def wait_slot(s):
        def body(t, carry):
            pltpu.make_async_copy(ext.at[s, t], o_hbm.at[pl.ds(0, SUBLANE)], sem.at[s]).wait()
            return carry
        lax.fori_loop(0, groups, body, 0)

    @pl.when(i >= 2)
    def _():
        wait_slot(slot)

    ext[slot, :, :, 0:d] = h_ref[...].reshape(groups, SUBLANE, d)
    ext[slot, :, :, d:d + LANE] = gate_ref[...].reshape(groups, SUBLANE, LANE)

    def body(t, carry):
        for u in range(SUBLANE):
            row_copy(slot, t, u, dest_ref[0, 0, t * SUBLANE + u]).start(priority=u % 2)
        return carry
    lax.fori_loop(0, groups, body, 0)

    @pl.when(i == n - 1)
    def _():
        wait_slot(slot)
        if n >= 2:
            wait_slot(1 - slot)


def moe_dispatch(h_all, row_offset, rows, gates, dest, into, tb, name):
    d = h_all.shape[1]
    off = row_offset // tb
    n = rows // tb
    body = functools.partial(_dispatch_body, tb=tb, n=n)
    return pl.pallas_call(
        body,
        out_shape=jax.ShapeDtypeStruct(into.shape, F32),
        grid=(n,),
        in_specs=[
            pl.BlockSpec((1, 1, tb), lambda i: (i, 0, 0), memory_space=pltpu.SMEM),
            pl.BlockSpec((tb, d), lambda i: (i + off, 0)),
            pl.BlockSpec((tb, LANE), lambda i: (i, 0)),
            pl.BlockSpec(memory_space=pl.ANY),
        ],
        out_specs=pl.BlockSpec(memory_space=pl.ANY),
        scratch_shapes=[pltpu.VMEM((2, tb // SUBLANE, SUBLANE, d + LANE), F32), pltpu.SemaphoreType.DMA((2,))],
        input_output_aliases={3: 0},
        compiler_params=_params("arbitrary"),
        name=name,
    )(dest.reshape(n, 1, tb), h_all, gates, into)


def _experts_body(te_ref, nu_ref, nb_ref, fill_ref, g_first_ref, g_next_ref, s_cur_ref,
                  h_hbm, g_ref, wg_ref, wu_ref, wo_ref, y_hbm,
                  xbuf, acc, xn_ref, gsem, ssem, *, tm, block, n_rows):
    i = pl.program_id(0)
    j = pl.program_id(1)
    n_tiles = pl.num_programs(0)
    nj = pl.num_programs(1)
    n_used = nu_ref[0]
    slot = i % 2
    valid = i < n_used
    chunks = tm // SUBLANE
    d = xn_ref.shape[1]

    def gather(idx_ref, s):
        def body(t, carry):
            row0 = pl.multiple_of(idx_ref[0, 0, t], SUBLANE)
            pltpu.make_async_copy(h_hbm.at[pl.ds(row0, SUBLANE), pl.ds(0, d)], xbuf.at[s, t], gsem.at[s]).start()
            return carry
        lax.fori_loop(0, chunks, body, 0)

    def wait_gather(s):
        def body(t, carry):
            pltpu.make_async_copy(h_hbm.at[pl.ds(0, SUBLANE), pl.ds(0, d)], xbuf.at[s, t], gsem.at[s]).wait()
            return carry
        lax.fori_loop(0, chunks, body, 0)

    def scatter():
        def body(t, carry):
            row0 = pl.multiple_of(s_cur_ref[0, 0, t], SUBLANE)
            pltpu.make_async_copy(acc.at[t], y_hbm.at[pl.ds(row0, SUBLANE)], ssem.at[0]).start()
            return carry
        lax.fori_loop(0, chunks, body, 0)

    def wait_scatter():
        def body(t, carry):
            pltpu.make_async_copy(acc.at[t], y_hbm.at[pl.ds(0, SUBLANE)], ssem.at[0]).wait()
            return carry
        lax.fori_loop(0, chunks, body, 0)

    @pl.when(j == 0)
    def _():
        @pl.when(i == 0)
        def _():
            gather(g_first_ref, 0)
            acc[...] = jnp.zeros_like(acc)
            first_free = nb_ref[0] * (block // SUBLANE)
            n_free = n_rows // SUBLANE - first_free

            def zero_copy(row0):
                return pltpu.make_async_copy(acc.at[0], y_hbm.at[pl.ds(pl.multiple_of(row0, SUBLANE), SUBLANE)],
                                             ssem.at[0])

            def fill(c, carry):
                zero_copy((first_free + c) * SUBLANE).start()
                zero_copy(n_rows + (first_free + c) * SUBLANE).start()
                return carry
            lax.fori_loop(0, n_free, fill, 0)

            def fill_spare(c, carry):
                zero_copy(2 * n_rows + c * SUBLANE).start()
                return carry
            lax.fori_loop(0, chunks, fill_spare, 0)

            def drain(c, carry):
                zero_copy(0).wait()
                return carry
            lax.fori_loop(0, 2 * n_free + chunks, drain, 0)

        @pl.when(i + 1 < n_used)
        def _():
            gather(g_next_ref, 1 - slot)

        @pl.when(valid)
        def _():
            wait_gather(slot)
            xn_ref[...] = _rms(xbuf[slot].reshape(tm, d), g_ref[...]).astype(BF16)

        @pl.when((i >= 1) & (i <= n_used))
        def _():
            wait_scatter()

        @pl.when(valid)
        def _():
            acc[...] = jnp.zeros_like(acc)

    def ff_step(n_chunks):
        rows = n_chunks * SUBLANE
        xn = xn_ref[0:rows, :]
        a = _dot(xn, wg_ref[0].astype(BF16))
        b = _dot(xn, wu_ref[0].astype(BF16))
        part = _dot((_silu(a) * b).astype(BF16), wo_ref[0].astype(BF16)).reshape(n_chunks, SUBLANE, d)
        acc[0:n_chunks] += part

    half_full = fill_ref[i] <= chunks // 2

    @pl.when(valid & jnp.logical_not(half_full))
    def _():
        ff_step(chunks)

    @pl.when(valid & half_full)
    def _():
        ff_step(chunks // 2)

    @pl.when(valid & (j == nj - 1))
    def _():
        scatter()

        @pl.when(i == n_tiles - 1)
        def _():
            wait_scatter()


def moe_experts(h_ps, g, wi, wo, gather_rows, scatter_rows, tile_expert, n_used, n_blocks_used, tile_fill,
                block, n_rows, tm, tf):
    d = wo.shape[2]
    f = wo.shape[1]
    nj = f // tf
    n_tiles = gather_rows.shape[0]
    chunks = tm // SUBLANE

    def wblock(col_off):
        def index_map(i, j, te, nu, nb, tf_):
            jj = jnp.where(i < nu[0], j, nj - 1)
            return (te[i], 0, col_off + jj)
        return index_map

    def woblock(i, j, te, nu, nb, tf_):
        return (te[i], jnp.where(i < nu[0], j, nj - 1), 0)

    def smem_tile(index_map):
        return pl.BlockSpec((1, 1, chunks), index_map, memory_space=pltpu.SMEM)

    grid_spec = pltpu.PrefetchScalarGridSpec(
        num_scalar_prefetch=4,
        grid=(n_tiles, nj),
        in_specs=[
            smem_tile(lambda i, j, te, nu, nb, tf_: (0, 0, 0)),
            smem_tile(lambda i, j, te, nu, nb, tf_: (jnp.minimum(i + 1, n_tiles - 1), 0, 0)),
            smem_tile(lambda i, j, te, nu, nb, tf_: (i, 0, 0)),
            pl.BlockSpec(memory_space=pl.ANY),
            pl.BlockSpec((1, d), lambda i, j, te, nu, nb, tf_: (0, 0)),
            pl.BlockSpec((1, d, tf), wblock(0)),
            pl.BlockSpec((1, d, tf), wblock(nj)),
            pl.BlockSpec((1, tf, d), woblock),
        ],
        out_specs=pl.BlockSpec(memory_space=pl.ANY),
        scratch_shapes=[
            pltpu.VMEM((2, chunks, SUBLANE, d), F32),
            pltpu.VMEM((chunks, SUBLANE, d), F32),
            pltpu.VMEM((tm, d), BF16),
            pltpu.SemaphoreType.DMA((2,)),
            pltpu.SemaphoreType.DMA((1,)),
        ],
    )
    return pl.pallas_call(
        functools.partial(_experts_body, tm=tm, block=block, n_rows=n_rows),
        out_shape=jax.ShapeDtypeStruct((2 * n_rows + tm, d), F32),
        grid_spec=grid_spec,
        compiler_params=_params("arbitrary", "arbitrary"),
        name="moe_experts",
    )(tile_expert, n_used, n_blocks_used, tile_fill, gather_rows, gather_rows, scatter_rows, h_ps,
      g.reshape(1, d), wi, wi, wo)


def _combine_body(nb_ref, hx_ref, y1_ref, y2_ref, fn_ref, o_ref, *, d, final_norm):
    @pl.when(pl.program_id(0) < nb_ref[0])
    def _():
        hx = hx_ref[...]
        out = hx[:, 0:d] + (hx[:, d:d + 1] * y1_ref[...] + hx[:, d + 1:d + 2] * y2_ref[...])
        if final_norm:
            out = _rms(out, fn_ref[...])
        o_ref[...] = out

    @pl.when(pl.program_id(0) >= nb_ref[0])
    def _():
        o_ref[...] = jnp.zeros_like(o_ref)


def moe_combine(h_ps, y, n_blocks_used, choice_stride, fn, tb, final_norm, name):
    rows, dx = h_ps.shape
    d = y.shape[1]
    off2 = choice_stride // tb

    def used(i, nb):
        return jnp.minimum(i, nb[0] - 1)

    grid_spec = pltpu.PrefetchScalarGridSpec(
        num_scalar_prefetch=1,
        grid=(rows // tb,),
        in_specs=[
            pl.BlockSpec((tb, dx), lambda i, nb: (used(i, nb), 0)),
            pl.BlockSpec((tb, d), lambda i, nb: (used(i, nb), 0)),
            pl.BlockSpec((tb, d), lambda i, nb: (used(i, nb) + off2, 0)),
            pl.BlockSpec((1, d), lambda i, nb: (0, 0)),
        ],
        out_specs=pl.BlockSpec((tb, d), lambda i, nb: (i, 0)),
    )
    return pl.pallas_call(
        functools.partial(_combine_body, d=d, final_norm=final_norm),
        out_shape=jax.ShapeDtypeStruct((rows, d), F32),
        grid_spec=grid_spec,
        compiler_params=_params("arbitrary"),
        name=name,
    )(n_blocks_used, h_ps, y, y, fn.reshape(1, d))


def _unpermute_body(cur_ref, nxt_ref, y_hbm, o_ref, buf, sem, *, tb):
    i = pl.program_id(0)
    n = pl.num_programs(0)
    slot = i % 2
    groups = tb // SUBLANE

    def start(idx_ref, s):
        def body(t, carry):
            for u in range(SUBLANE):
                row = idx_ref[0, 0, t * SUBLANE + u]
                pltpu.make_async_copy(y_hbm.at[pl.ds(row, 1)], buf.at[s, t, pl.ds(u, 1)], sem.at[s]).start(
                    priority=u % 2)
            return carry
        lax.fori_loop(0, groups, body, 0)

    @pl.when(i == 0)
    def _():
        start(cur_ref, 0)

    @pl.when(i + 1 < n)
    def _():
        start(nxt_ref, 1 - slot)

    def wait_body(t, carry):
        pltpu.make_async_copy(y_hbm.at[pl.ds(0, SUBLANE)], buf.at[slot, t], sem.at[slot]).wait()
        return carry
    lax.fori_loop(0, groups, wait_body, 0)
    o_ref[...] = buf[slot].reshape(o_ref.shape)


def moe_unpermute(y_ps, dest, tb, name):
    rows = dest.shape[0]
    d = y_ps.shape[1]
    n = rows // tb
    idx = dest.reshape(n, 1, tb)

    def smem_tile(index_map):
        return pl.BlockSpec((1, 1, tb), index_map, memory_space=pltpu.SMEM)

    return pl.pallas_call(
        functools.partial(_unpermute_body, tb=tb),
        out_shape=jax.ShapeDtypeStruct((rows, d), F32),
        grid=(n,),
        in_specs=[smem_tile(lambda i: (i, 0, 0)), smem_tile(lambda i: (jnp.minimum(i + 1, n - 1), 0, 0)),
                  pl.BlockSpec(memory_space=pl.ANY)],
        out_specs=pl.BlockSpec((tb, d), lambda i: (i, 0)),
        scratch_shapes=[pltpu.VMEM((2, tb // SUBLANE, SUBLANE, d), F32), pltpu.SemaphoreType.DMA((2,))],
        compiler_params=_params("arbitrary"),
        name=name,
    )(idx, idx, y_ps)


def moe_plan(idx, n_experts, tm, block):
    n_tok, k = idx.shape
    assert k == 2
    n_pairs = n_experts * n_experts
    chunks_per_tile = tm // SUBLANE
    n_rows = -(-(n_tok + n_pairs * (SUBLANE - 1)) // block) * block
    n_chunks = n_rows // SUBLANE

    pair = idx[:, 0] * n_experts + idx[:, 1]
    onehot = (pair[:, None] == jnp.arange(n_pairs, dtype=jnp.int32)[None, :]).astype(jnp.int32)
    csum = jnp.cumsum(onehot, axis=0)
    counts = csum[-1]
    rank = jnp.sum(csum * onehot, axis=1) - 1
    padded = (counts + SUBLANE - 1) // SUBLANE * SUBLANE
    total = jnp.sum(padded)
    last_group = jnp.max(jnp.where(counts > 0, jnp.arange(n_pairs, dtype=jnp.int32), 0))
    padded = padded + jnp.where(jnp.arange(n_pairs) == last_group, (-total) % block, 0)
    group_end = jnp.cumsum(padded)
    group_start = group_end - padded
    dest = jnp.sum(onehot * group_start[None, :], axis=1) + rank
    rows_used = group_end[-1]

    chunk_row = jnp.arange(n_chunks, dtype=jnp.int32) * SUBLANE
    chunk_group = jnp.sum((chunk_row[:, None] >= group_end[None, :]).astype(jnp.int32), axis=1)
    chunk_used = chunk_row < rows_used
    chunk_group = jnp.minimum(chunk_group, n_pairs - 1)
    chunk_experts = jnp.stack([chunk_group // n_experts, chunk_group % n_experts], axis=1)
    experts = jnp.arange(n_experts, dtype=jnp.int32)
    member = ((chunk_experts[:, :, None] == experts[None, None, :]) & chunk_used[:, None, None])
    member = member.astype(jnp.int32)
    member_e = jnp.sum(member, axis=1)
    csum_e = jnp.cumsum(member_e, axis=0)
    counts_e = csum_e[-1]
    tiles_e = (counts_e + chunks_per_tile - 1) // chunks_per_tile
    tile_end = jnp.cumsum(tiles_e)
    start_e = (tile_end - tiles_e) * chunks_per_tile
    pos = jnp.sum(member * (start_e[None, None, :] + csum_e[:, None, :] - 1), axis=2)
    n_tiles = 2 * n_chunks // chunks_per_tile + n_experts
    n_pos = n_tiles * chunks_per_tile
    pos = jnp.where(chunk_used[:, None], pos, n_pos)
    entry = 2 * jnp.arange(n_chunks, dtype=jnp.int32)[:, None] + jnp.arange(2, dtype=jnp.int32)[None, :] + 1
    table = jnp.zeros((n_pos,), jnp.int32).at[pos.reshape(-1)].set(entry.reshape(-1), mode="drop")
    table = table.reshape(n_tiles, chunks_per_tile)
    filled = table > 0
    src_chunk = (table - 1) // 2
    choice = (table - 1) % 2
    spare = 2 * n_rows + jnp.arange(chunks_per_tile, dtype=jnp.int32) * SUBLANE
    gather_rows = jnp.where(filled, src_chunk * SUBLANE, 0)
    scatter_rows = jnp.where(filled, choice * n_rows + src_chunk * SUBLANE, spare[None, :])

    n_used = tile_end[-1]
    tile_ids = jnp.arange(n_tiles, dtype=jnp.int32)
    tile_expert = jnp.sum((tile_ids[:, None] >= tile_end[None, :]).astype(jnp.int32), axis=1)
    last_expert = jnp.sum((n_used - 1 >= tile_end).astype(jnp.int32))
    tile_expert = jnp.where(tile_ids < n_used, tile_expert, last_expert).astype(jnp.int32)
    return (dest.astype(jnp.int32),
            gather_rows.reshape(n_tiles, 1, chunks_per_tile).astype(jnp.int32),
            scatter_rows.reshape(n_tiles, 1, chunks_per_tile).astype(jnp.int32),
            tile_expert, n_used.reshape(1).astype(jnp.int32),
            (rows_used // block).reshape(1).astype(jnp.int32),
            jnp.sum(filled.astype(jnp.int32), axis=1), n_rows)


def _pad_lanes(v):
    return jnp.zeros((1, LANE), F32).at[0, :v.shape[0]].set(v)


def kernel(x_prompt, x_sample, state_dn, state_dn_conv, state_pool, cache_mem_k, cache_mem_v, mem_prompt, a_norm1, a_w_in, a_conv_w, a_a_log, a_dt_bias, a_o_norm, a_w_out, a_norm2, a_ffn_wi, a_ffn_wo, b_norm1, b_w_in, b_pool_w, b_pool_scale, b_w_out, b_norm2, b_router, b_moe_wi, b_moe_wo, m_norm, m_w_k, m_w_v, final_norm):
    bp, seq, d = x_prompt.shape
    bs = x_sample.shape[0]
    assert x_sample.shape[1] == 1
    depth = m_norm.shape[0]
    assert depth % 2 == 0, "the output norm is fused into the expert block of the last (odd) layer"
    heads, dk, dv = state_dn.shape[2], state_dn.shape[3], state_dn.shape[4]
    conv_ch = state_dn_conv.shape[-1]
    qk_w, v_w = heads * dk, heads * dv
    assert conv_ch == 2 * qk_w + v_w and dk == LANE and dv == LANE and heads <= LANE
    n_mem, x_heads, x_dh = cache_mem_k.shape[2], cache_mem_k.shape[3], cache_mem_k.shape[4]
    xw = x_heads * x_dh
    pool_w_total = state_pool.shape[-1]
    pool_buf = state_pool.shape[2]
    d_ff = a_ffn_wo.shape[1]

    hp = x_prompt.reshape(bp * seq, d)
    hs = x_sample.reshape(bs, d)

    o1, o2, o3, o4 = conv_ch, conv_ch + v_w, conv_ch + v_w + heads, conv_ch + v_w + 2 * heads
    col_a = {"qkv": 0, "z": conv_ch, "cq": conv_ch + v_w, "b": conv_ch + v_w + xw, "a": conv_ch + v_w + xw + LANE}
    n_a = -(-(col_a["a"] + LANE) // 512) * 512

    dn_p, conv_p, pool_p, mk_all, mv_all = [], [], [], [], []
    dn_s, conv_s, pool_s = [], [], []
    for i in range(depth):
        j = i // 2
        w_kv = jnp.concatenate([m_w_k[i], m_w_v[i]], axis=1).astype(BF16)
        kv, mk, mv = mem_kv(mem_prompt.reshape(bp * n_mem, d), m_norm[i], w_kv, x_heads, x_dh, 256,
                            f"mem_kv_{i}")
        mk_all.append(mk.reshape(bp, n_mem, x_heads, x_dh))
        mv_all.append(mv.reshape(bp, n_mem, x_heads, x_dh))
        if i % 2 == 0:
            w = a_w_in[j]
            w_in = jnp.concatenate(
                [w[:, :o2], w[:, o4:], w[:, o2:o3], jnp.zeros((d, LANE - heads), F32),
                 w[:, o3:o4], jnp.zeros((d, n_a - col_a["a"] - heads), F32)], axis=1)
            assert w_in.shape[1] == n_a and col_a["b"] == o2 + xw
            a_log, dt_b = _pad_lanes(a_a_log[j]), _pad_lanes(a_dt_bias[j])
            o_norm = a_o_norm[j].reshape(1, dv)

            proj_p = rms_matmul(hp, a_norm1[j], w_in.astype(BF16), 512, f"a_in_p_{i}")
            proj_s = rms_matmul(hs, a_norm1[j], w_in, bs, f"a_in_s_{i}", tn=512)
            mix_p, s_p = deltanet_prompt(proj_p, col_a, a_conv_w[j], a_log, dt_b, o_norm,
                                         bp, seq, heads, dk, dv, rows=DN_STEP_CHUNKS * CHUNK)
            mix_s, s_s, c_s = deltanet_sample(proj_s, col_a, state_dn_conv[j], state_dn[j], a_conv_w[j],
                                              a_log, dt_b, o_norm, heads, dk, dv, nb=SUBLANE)
            xa_p = xattn_prompt(proj_p, col_a["cq"], kv, bp, seq, n_mem, x_heads, x_dh, tq=min(1024, seq))
            xa_s = xattn_sample(proj_s, col_a["cq"], cache_mem_k, cache_mem_v, i, nb=SUBLANE)
            hp = out_proj_ffn(hp, mix_p, xa_p, a_w_out[j].astype(BF16), a_norm2[j], a_ffn_wi[j].astype(BF16),
                              a_ffn_wo[j].astype(BF16), 256, f"a_out_ffn_p_{i}")
            hs = out_proj(hs, mix_s, xa_s, a_w_out[j], bs, f"a_out_s_{i}")
            hs = ffn_dense(hs, a_norm2[j], a_ffn_wi[j], a_ffn_wo[j], bs, f"a_ffn_s_{i}", tf=256)
            dn_p.append(s_p)
            conv_p.append(proj_p.reshape(bp, seq, n_a)[:, seq - (CONV_WIDTH - 1):, :conv_ch])
            dn_s.append(s_s)
            conv_s.append(c_s)
        else:
            ps = b_pool_scale[j].reshape(1, pool_w_total)
            last = i == depth - 1

            mix_p, cq_p, pool_tail = proj_pool_prompt(hp, b_norm1[j], b_w_in[j].astype(BF16),
                                                      b_pool_w[j].astype(BF16), ps, bp, seq, tt=512)
            proj_s = rms_matmul(hs, b_norm1[j], b_w_in[j], bs, f"b_in_s_{i}")
            mix_s, buf_s = pool_sample(proj_s, state_pool[j], b_pool_w[j], ps)
            xa_p = xattn_prompt(cq_p, 0, kv, bp, seq, n_mem, x_heads, x_dh, tq=min(1024, seq))
            xa_s = xattn_sample(proj_s, pool_w_total, cache_mem_k, cache_mem_v, i, nb=SUBLANE)
            n_p = bp * seq
            n_experts = b_router.shape[-1]
            r_pad = jnp.concatenate([b_router[j], jnp.zeros((d, LANE - n_experts), F32)], axis=1)
            h_all, idx_p, gate_p = out_proj(hp, mix_p, xa_p, b_w_out[j].astype(BF16), 512, f"b_out_p_{i}",
                                            out_rows=n_p + bs, route=(b_norm2[j], r_pad.astype(BF16), n_experts))
            h_all, idx_s, gate_s = out_proj(hs, mix_s, xa_s, b_w_out[j], bs, f"b_out_s_{i}", out_rows=n_p + bs,
                                            into=h_all, row_offset=n_p, route=(b_norm2[j], r_pad, n_experts))
            idx = jnp.concatenate([idx_p[:n_p, :TOP_K], idx_s[:, :TOP_K]], axis=0)
            (dest, gather_rows, scatter_rows, tile_expert, n_used, n_blocks_used, tile_fill, n_rows) = moe_plan(
                idx, n_experts, MOE_TILE, MOE_ROW_BLOCK)
            h_ps = jnp.zeros((n_rows, d + LANE), F32)
            h_ps = moe_dispatch(h_all, 0, n_p, gate_p, dest[:n_p], h_ps, 512, f"b_dispatch_p_{i}")
            h_ps = moe_dispatch(h_all, n_p, bs, gate_s, dest[n_p:], h_ps, bs, f"b_dispatch_s_{i}")
            y = moe_experts(h_ps, b_norm2[j], b_moe_wi[j], b_moe_wo[j], gather_rows, scatter_rows, tile_expert,
                            n_used, n_blocks_used, tile_fill, MOE_ROW_BLOCK, n_rows, MOE_TILE, MOE_FF_TILE)
            y_ps = moe_combine(h_ps, y, n_blocks_used, n_rows, final_norm, MOE_ROW_BLOCK, last,
                               f"b_combine_{i}")
            hp = moe_unpermute(y_ps, dest[:n_p], 512, f"b_unpermute_p_{i}")
            hs = moe_unpermute(y_ps, dest[n_p:], bs, f"b_unpermute_s_{i}")
            pool_p.append(pool_tail[:, POOL_HALO - pool_buf:, :])
            pool_s.append(buf_s)

    y_prompt = hp.reshape(bp, seq, d)
    y_sample = hs.reshape(bs, 1, d)
    return (y_prompt, y_sample, jnp.stack(dn_p), jnp.stack(conv_p), jnp.stack(pool_p),
            jnp.stack(mk_all), jnp.stack(mv_all), jnp.stack(dn_s), jnp.stack(conv_s), jnp.stack(pool_s))
```

```python
import functools

import jax
import jax.numpy as jnp
from jax import lax
from jax.experimental import pallas as pl
from jax.experimental.pallas import tpu as pltpu

F32 = jnp.float32
BF16 = jnp.bfloat16
HIGHEST = lax.Precision.HIGHEST

EPS = 1e-6
CHUNK = 64
CONV_WIDTH = 4
POOL_WINDOWS = (2, 4, 8, 16)
PAST_LEN = 16384
TOP_K = 2

LANE = 128
SUBLANE = 8
VMEM_LIMIT_BYTES = 56 * 2**20

MOE_TILE = 1024
MOE_FF_TILE = 512
MOE_ROW_BLOCK = 512
DN_STEP_CHUNKS = 8
INV_BLOCK = 16
POOL_HALO = 32


def _params(*sem):
    return pltpu.CompilerParams(dimension_semantics=sem, vmem_limit_bytes=VMEM_LIMIT_BYTES)


def _resident(shape):
    nd = len(shape)
    return pl.BlockSpec(shape, lambda *_: (0,) * nd, pipeline_mode=pl.Buffered(1))


def _rms(x, w):
    return x * lax.rsqrt(jnp.mean(x * x, axis=-1, keepdims=True) + EPS) * w


def _silu(x):
    half = 0.5 * x
    return half + half * jnp.tanh(half)


def _dot(a, b):
    return jnp.dot(a, b, preferred_element_type=F32)


def _wdot(a, w):
    if w.dtype == F32:
        return jnp.dot(a, w, precision=HIGHEST, preferred_element_type=F32)
    return jnp.dot(a.astype(BF16), w, preferred_element_type=F32)


def _bdot(a, b):
    return lax.dot_general(a, b, (((2,), (1,)), ((0,), (0,))), preferred_element_type=F32)


def _bdot_nt(a, b):
    return lax.dot_general(a, b, (((2,), (2,)), ((0,), (0,))), preferred_element_type=F32)


def _bdot_tn(a, b):
    return lax.dot_general(a, b, (((1,), (1,)), ((0,), (0,))), preferred_element_type=F32)


def _rms_matmul_body(x_ref, g_ref, w_ref, o_ref):
    xn = _rms(x_ref[...], g_ref[...])
    o_ref[...] = _wdot(xn, w_ref[...])


def _wspec(shape, index_map, steps):
    if steps == 1:
        return pl.BlockSpec(shape, index_map, pipeline_mode=pl.Buffered(1))
    return pl.BlockSpec(shape, index_map)


def rms_matmul(x, g, w, tm, name, tn=None):
    m, d = x.shape
    n = w.shape[1]
    tn = n if tn is None else tn
    nj = n // tn
    return pl.pallas_call(
        _rms_matmul_body,
        out_shape=jax.ShapeDtypeStruct((m, n), F32),
        grid=(m // tm, nj),
        in_specs=[pl.BlockSpec((tm, d), lambda i, j: (i, 0)), _resident((1, d)),
                  _wspec((d, tn), lambda i, j: (0, j), nj)],
        out_specs=pl.BlockSpec((tm, tn), lambda i, j: (i, j)),
        compiler_params=_params("parallel", "arbitrary"),
        name=name,
    )(x, g.reshape(1, d), w)


def _out_proj_body(x_ref, a1_ref, a2_ref, w1_ref, w2_ref, *rest, n_main, n_experts, has_into):
    route = n_experts is not None
    if route:
        g_ref, r_ref = rest[0], rest[1]
        o_ref, idx_ref, gate_ref = rest[-3:]
    else:
        o_ref = rest[-1]
    del has_into

    @pl.when(pl.program_id(0) < n_main)
    def _():
        acc = _wdot(a1_ref[...], w1_ref[...]) + _wdot(a2_ref[...], w2_ref[...])
        h = x_ref[...] + acc
        o_ref[...] = h
        if route:
            idx_ref[...], gate_ref[...] = _route_top2(h, g_ref[...], r_ref[...], n_experts)

    @pl.when(pl.program_id(0) >= n_main)
    def _():
        o_ref[...] = jnp.zeros_like(o_ref)
        if route:
            idx_ref[...] = jnp.zeros_like(idx_ref)
            gate_ref[...] = jnp.zeros_like(gate_ref)


def out_proj(x, a1, a2, w, tm, name, out_rows=None, into=None, row_offset=0, route=None):
    m, d = x.shape
    k1, k2 = a1.shape[1], a2.shape[1]
    assert w.shape[0] == k1 + k2 and k1 % k2 == 0 and row_offset % tm == 0
    out_rows = m if out_rows is None else out_rows
    off = row_offset // tm
    n_main = m // tm
    n_extra = 0 if (into is not None or out_rows == m) else 1
    assert out_rows - m <= n_extra * tm or into is not None
    last = n_main - 1
    in_specs = [
        pl.BlockSpec((tm, d), lambda i: (jnp.minimum(i, last), 0)),
        pl.BlockSpec((tm, k1), lambda i: (jnp.minimum(i, last), 0)),
        pl.BlockSpec((tm, k2), lambda i: (jnp.minimum(i, last), 0)),
        pl.BlockSpec((k1, d), lambda i: (0, 0), pipeline_mode=pl.Buffered(1)),
        pl.BlockSpec((k2, d), lambda i: (k1 // k2, 0), pipeline_mode=pl.Buffered(1)),
    ]
    args = [x, a1, a2, w, w]
    out_shape = [jax.ShapeDtypeStruct((out_rows, d), F32)]
    out_specs = [pl.BlockSpec((tm, d), lambda i: (i + off, 0))]
    n_experts = None
    if route is not None:
        g, r_pad, n_experts = route
        in_specs += [_resident((1, d)), _resident((d, LANE))]
        args += [g.reshape(1, d), r_pad]
        n_steps = n_main + n_extra
        out_shape += [jax.ShapeDtypeStruct((n_steps * tm, LANE), jnp.int32),
                      jax.ShapeDtypeStruct((n_steps * tm, LANE), F32)]
        out_specs += [pl.BlockSpec((tm, LANE), lambda i: (i, 0))] * 2
    aliases = {}
    if into is not None:
        in_specs.append(pl.BlockSpec(memory_space=pl.ANY))
        args.append(into)
        aliases = {len(args) - 1: 0}
    outs = pl.pallas_call(
        functools.partial(_out_proj_body, n_main=n_main, n_experts=n_experts, has_into=into is not None),
        out_shape=tuple(out_shape),
        grid=(n_main + n_extra,),
        in_specs=in_specs,
        out_specs=tuple(out_specs),
        input_output_aliases=aliases,
        compiler_params=_params("parallel"),
        name=name,
    )(*args)
    return outs if route is not None else outs[0]


def _ffn_body(h_ref, g_ref, wg_ref, wu_ref, wo_ref, o_ref, xn_ref):
    @pl.when(pl.program_id(1) == 0)
    def _():
        h = h_ref[...]
        xn_ref[...] = _rms(h, g_ref[...])
        o_ref[...] = h

    xn = xn_ref[...]
    a = _wdot(xn, wg_ref[...])
    b = _wdot(xn, wu_ref[...])
    o_ref[...] += _wdot(_silu(a) * b, wo_ref[...])


def ffn_dense(h, g, wi, wo, tm, name, tf=None):
    m, d = h.shape
    f = wo.shape[0]
    tf = f if tf is None else tf
    nj = f // tf
    return pl.pallas_call(
        _ffn_body,
        out_shape=jax.ShapeDtypeStruct((m, d), F32),
        grid=(m // tm, nj),
        in_specs=[
            pl.BlockSpec((tm, d), lambda i, j: (i, 0)),
            _resident((1, d)),
            _wspec((d, tf), lambda i, j: (0, j), nj),
            _wspec((d, tf), lambda i, j: (0, nj + j), nj),
            _wspec((tf, d), lambda i, j: (j, 0), nj),
        ],
        out_specs=pl.BlockSpec((tm, d), lambda i, j: (i, 0)),
        scratch_shapes=[pltpu.VMEM((tm, d), F32)],
        compiler_params=_params("parallel", "arbitrary"),
        name=name,
    )(h, g.reshape(1, d), wi, wi, wo)


def _mem_kv_body(x_ref, g_ref, w_ref, *rest, heads, dh, n_own):
    kv_ref, k_ref, v_ref = rest[-3:]
    xw = heads * dh

    @pl.when(pl.program_id(0) < n_own)
    def _():
        res = _wdot(_rms(x_ref[...], g_ref[...]), w_ref[...])
        kv_ref[...] = res
        for h in range(heads):
            k_ref[:, h, :] = res[:, h * dh:(h + 1) * dh]
            v_ref[:, h, :] = res[:, xw + h * dh:xw + (h + 1) * dh]

    @pl.when(pl.program_id(0) >= n_own)
    def _():
        k_ref[...] = jnp.zeros_like(k_ref)
        v_ref[...] = jnp.zeros_like(v_ref)


def mem_kv(mem, g, w_kv, heads, dh, tm, name, layer, n_layers, into=None):
    m, d = mem.shape
    xw = heads * dh
    n_own = m // tm
    assert (into is None) == (layer == 0)
    n_steps = n_own * n_layers if into is None else n_own
    head_shape = jax.ShapeDtypeStruct((n_layers * m, heads, dh), F32)
    head_spec = pl.BlockSpec((tm, heads, dh), lambda i: (i + layer * n_own, 0, 0))
    own = lambda i: (jnp.minimum(i, n_own - 1), 0)
    in_specs = [pl.BlockSpec((tm, d), own), _resident((1, d)), _resident((d, 2 * xw))]
    args = [mem, g.reshape(1, d), w_kv]
    aliases = {}
    if into is not None:
        in_specs += [pl.BlockSpec(memory_space=pl.ANY)] * 2
        args += list(into)
        aliases = {3: 1, 4: 2}
    return pl.pallas_call(
        functools.partial(_mem_kv_body, heads=heads, dh=dh, n_own=n_own),
        out_shape=(jax.ShapeDtypeStruct((m, 2 * xw), F32), head_shape, head_shape),
        grid=(n_steps,),
        in_specs=in_specs,
        out_specs=(pl.BlockSpec((tm, 2 * xw), own), head_spec, head_spec),
        input_output_aliases=aliases,
        compiler_params=_params("arbitrary"),
        name=name,
    )(*args)


def _out_proj_ffn_body(x_ref, a1_ref, a2_ref, w1_ref, w2_ref, g_ref, wg_ref, wu_ref, wo_ref, o_ref):
    h = x_ref[...] + (_wdot(a1_ref[...], w1_ref[...]) + _wdot(a2_ref[...], w2_ref[...]))
    xn = _rms(h, g_ref[...])
    a = _wdot(xn, wg_ref[...])
    b = _wdot(xn, wu_ref[...])
    o_ref[...] = h + _wdot(_silu(a) * b, wo_ref[...])


def out_proj_ffn(x, a1, a2, w, g, wi, wo, tm, name):
    m, d = x.shape
    k1, k2 = a1.shape[1], a2.shape[1]
    f = wo.shape[0]
    assert w.shape[0] == k1 + k2 and k1 % k2 == 0
    return pl.pallas_call(
        _out_proj_ffn_body,
        out_shape=jax.ShapeDtypeStruct((m, d), F32),
        grid=(m // tm,),
        in_specs=[
            pl.BlockSpec((tm, d), lambda i: (i, 0)),
            pl.BlockSpec((tm, k1), lambda i: (i, 0)),
            pl.BlockSpec((tm, k2), lambda i: (i, 0)),
            pl.BlockSpec((k1, d), lambda i: (0, 0), pipeline_mode=pl.Buffered(1)),
            pl.BlockSpec((k2, d), lambda i: (k1 // k2, 0), pipeline_mode=pl.Buffered(1)),
            _resident((1, d)),
            pl.BlockSpec((d, f), lambda i: (0, 0), pipeline_mode=pl.Buffered(1)),
            pl.BlockSpec((d, f), lambda i: (0, 1), pipeline_mode=pl.Buffered(1)),
            _resident((f, d)),
        ],
        out_specs=pl.BlockSpec((tm, d), lambda i: (i, 0)),
        compiler_params=_params("parallel"),
        name=name,
    )(x, a1, a2, w, w, g.reshape(1, d), wi, wi, wo)


def _pair_blockdiag(y, left):
    yb = y.astype(BF16)
    zero = jnp.zeros_like(yb)
    return jnp.concatenate([jnp.where(left, yb, zero), jnp.where(left, zero, yb)], axis=1)


def _pair_mm(x, y, left):
    return _bdot(x.astype(BF16), _pair_blockdiag(y, left))


def _unit_lower_inverse(mx, eye, blk, left):
    mm = functools.partial(_pair_mm, left=left)
    md = jnp.where(blk, mx, 0.0)
    c = mx - md
    p = eye - md
    m2 = mm(md, md)
    p = p + mm(p, m2)
    m4 = mm(m2, m2)
    p = p + mm(p, m4)
    m8 = mm(m4, m4)
    td = p + mm(p, m8)
    n = mm(td, c)
    n2 = mm(n, n)
    q = eye - n
    q = q + mm(q, n2)
    return mm(q, td)


def _dn_prompt_body(qkv_ref, z_ref, bl_ref, al_ref, cw_ref, alog_ref, dtb_ref, on_ref,
                    dn_ref, sfin_ref, s_ref, xbuf_ref, act_ref, *, rows, heads, dk, dv):
    c = pl.program_id(1)
    wq = heads * dk

    @pl.when(c == 0)
    def _():
        s_ref[...] = jnp.zeros_like(s_ref)
        xbuf_ref[0:SUBLANE, :] = jnp.zeros((SUBLANE, xbuf_ref.shape[1]), F32)

    xbuf_ref[SUBLANE:SUBLANE + rows, :] = qkv_ref[...]
    first = SUBLANE - (CONV_WIDTH - 1)
    y = xbuf_ref[pl.ds(first, rows), :] * cw_ref[0:1, :]
    for j in range(1, CONV_WIDTH):
        y = y + xbuf_ref[pl.ds(first + j, rows), :] * cw_ref[j:j + 1, :]
    act_ref[...] = _silu(y)
    xbuf_ref[0:SUBLANE, :] = xbuf_ref[rows:rows + SUBLANE, :]

    pairs = heads // 2
    ri = lax.broadcasted_iota(jnp.int32, (CHUNK, 2 * CHUNK), 0)
    lane = lax.broadcasted_iota(jnp.int32, (CHUNK, 2 * CHUNK), 1)
    cj = lane % CHUNK
    left = (lane < CHUNK)[None]
    incl = (ri >= cj)[None]
    strict = (ri > cj)[None]
    blk = ((ri // INV_BLOCK) == (cj // INV_BLOCK))[None]
    eye = (ri == cj).astype(F32)[None]
    tr = lax.broadcasted_iota(jnp.int32, (CHUNK, CHUNK), 0)
    tc = lax.broadcasted_iota(jnp.int32, (CHUNK, CHUNK), 1)
    tri = (tr >= tc).astype(F32)

    n_chunks = rows // CHUNK
    zero = jnp.zeros((CHUNK, dk), BF16)

    def chunk_rows(cc):
        return slice(cc * CHUNK, (cc + 1) * CHUNK)

    def per_head(fn):
        return jnp.stack([fn(chunk_rows(cc), h) for cc in range(n_chunks) for h in range(heads)], axis=0)

    def per_pair(fn):
        return jnp.stack([fn(cc, p) for cc in range(n_chunks) for p in range(pairs)], axis=0)

    q3 = per_head(lambda rs, h: act_ref[rs, h * dk:(h + 1) * dk])
    k3 = per_head(lambda rs, h: act_ref[rs, wq + h * dk:wq + (h + 1) * dk])
    v3 = per_head(lambda rs, h: act_ref[rs, 2 * wq + h * dv:2 * wq + (h + 1) * dv])
    qn = q3 * lax.rsqrt(jnp.sum(q3 * q3, axis=-1, keepdims=True) + EPS) * (dk ** -0.5)
    kn = k3 * lax.rsqrt(jnp.sum(k3 * k3, axis=-1, keepdims=True) + EPS)

    beta = jax.nn.sigmoid(bl_ref[...])
    log_decay = -jnp.exp(alog_ref[...]) * jax.nn.softplus(al_ref[...] + dtb_ref[...])
    gcum = [jnp.dot(tri, log_decay[chunk_rows(cc)], precision=HIGHEST, preferred_element_type=F32)
            for cc in range(n_chunks)]
    gcum_t = [jnp.concatenate([gc_, pltpu.roll(gc_, LANE - 1, 1)], axis=0).T for gc_ in gcum]

    def pair_cols(x_of):
        return per_pair(lambda cc, p: jnp.where(left[0], x_of(cc)[:, 2 * p:2 * p + 1],
                                                x_of(cc)[:, 2 * p + 1:2 * p + 2]))

    gc2 = pair_cols(lambda cc: gcum[cc])
    bc2 = pair_cols(lambda cc: beta[chunk_rows(cc)])
    gr2 = per_pair(lambda cc, p: gcum_t[cc][2 * p:2 * p + 1, :])
    decay = jnp.where(incl, jnp.exp(gc2 - gr2), 0.0)

    kb = kn.astype(BF16)
    qb = qn.astype(BF16)

    def pair_of(x, cc, p):
        return x[cc * heads + 2 * p], x[cc * heads + 2 * p + 1]

    k_lhs = per_pair(lambda cc, p: jnp.concatenate(pair_of(kb, cc, p), axis=1))
    q_lhs = per_pair(lambda cc, p: jnp.concatenate(pair_of(qb, cc, p), axis=1))
    k_bd = per_pair(lambda cc, p: jnp.concatenate(
        [jnp.concatenate([pair_of(kb, cc, p)[0], zero], axis=1),
         jnp.concatenate([zero, pair_of(kb, cc, p)[1]], axis=1)], axis=0))
    kk = _bdot_nt(k_lhs, k_bd)
    mx = jnp.where(strict, bc2 * kk * decay, 0.0)
    tinv = _unit_lower_inverse(mx, eye, blk, left)

    gcum_rows = jnp.concatenate(gcum, axis=0)
    gc = per_head(lambda rs, h: gcum_rows[rs, h:h + 1])
    bc = per_head(lambda rs, h: beta[rs, h:h + 1])
    eg = jnp.exp(gc)
    rhs = jnp.concatenate([v3 * bc, kn * (bc * eg)], axis=-1)
    sol = _bdot(_pair_blockdiag(tinv, left), rhs.astype(BF16).reshape(n_chunks * pairs, 2 * CHUNK, dv + dk))
    sol = sol.reshape(n_chunks * heads, CHUNK, dv + dk)
    w_val = sol[..., :dv]
    k_cd = sol[..., dv:].astype(BF16)
    qk_bd = _pair_blockdiag(_bdot_nt(q_lhs, k_bd) * decay, left)
    q_dec = (qn * eg).astype(BF16)
    g_last = gc[:, CHUNK - 1:CHUNK, :]
    k_tail = (kn * jnp.exp(g_last - gc)).astype(BF16)
    c_decay = jnp.exp(g_last)

    for cc in range(n_chunks):
        hs = slice(cc * heads, (cc + 1) * heads)
        ps = slice(cc * pairs, (cc + 1) * pairs)
        rs = chunk_rows(cc)
        s = s_ref[...]
        sb = s.astype(BF16)
        u = w_val[hs] - _bdot(k_cd[hs], sb)
        ub = u.astype(BF16)
        intra = _bdot(qk_bd[ps], ub.reshape(pairs, 2 * CHUNK, dv))
        o = _bdot(q_dec[hs], sb) + intra.reshape(heads, CHUNK, dv)
        s_ref[...] = s * c_decay[hs] + _bdot_tn(k_tail[hs], ub)

        o = o * lax.rsqrt(jnp.mean(o * o, axis=-1, keepdims=True) + EPS) * on_ref[...]
        for h in range(heads):
            z = z_ref[rs, h * dv:(h + 1) * dv]
            dn_ref[rs, h * dv:(h + 1) * dv] = o[h] * _silu(z)

    @pl.when(c == pl.num_programs(1) - 1)
    def _():
        sfin_ref[0] = s_ref[...]


def deltanet_prompt(proj, col, conv_w, a_log, dt_bias, o_norm, batch, seq, heads, dk, dv, rows):
    conv_ch = 2 * heads * dk + heads * dv
    zw = heads * dv
    steps = seq // rows

    def row_map(width_off):
        return lambda b, c: (b * steps + c, width_off)

    body = functools.partial(_dn_prompt_body, rows=rows, heads=heads, dk=dk, dv=dv)
    return pl.pallas_call(
        body,
        out_shape=(jax.ShapeDtypeStruct((batch * seq, zw), F32),
                   jax.ShapeDtypeStruct((batch, heads, dk, dv), F32)),
        grid=(batch, steps),
        in_specs=[
            pl.BlockSpec((rows, conv_ch), row_map(col["qkv"] // conv_ch)),
            pl.BlockSpec((rows, zw), row_map(col["z"] // zw)),
            pl.BlockSpec((rows, LANE), row_map(col["b"] // LANE)),
            pl.BlockSpec((rows, LANE), row_map(col["a"] // LANE)),
            _resident((CONV_WIDTH, conv_ch)),
            _resident((1, LANE)),
            _resident((1, LANE)),
            _resident((1, dv)),
        ],
        out_specs=(pl.BlockSpec((rows, zw), lambda b, c: (b * steps + c, 0)),
                   pl.BlockSpec((1, heads, dk, dv), lambda b, c: (b, 0, 0, 0))),
        scratch_shapes=[
            pltpu.VMEM((heads, dk, dv), F32),
            pltpu.VMEM((rows + SUBLANE, conv_ch), F32),
            pltpu.VMEM((rows, conv_ch), F32),
        ],
        compiler_params=_params("parallel", "arbitrary"),
        name="deltanet_prompt",
    )(proj, proj, proj, proj, conv_w, a_log, dt_bias, o_norm)


def _dn_sample_body(qkv_ref, z_ref, bl_ref, al_ref, cbuf_ref, st_ref, cw_ref, alog_ref, dtb_ref, on_ref,
                    dn_ref, sto_ref, cbo_ref, act_ref, *, nb, heads, dk, dv):
    wq = heads * dk
    new = qkv_ref[...]
    y = cbuf_ref[:, 0, :] * cw_ref[0:1, :]
    for j in range(1, CONV_WIDTH - 1):
        y = y + cbuf_ref[:, j, :] * cw_ref[j:j + 1, :]
    y = y + new * cw_ref[CONV_WIDTH - 1:CONV_WIDTH, :]
    act_ref[...] = _silu(y)
    for j in range(CONV_WIDTH - 2):
        cbo_ref[:, j, :] = cbuf_ref[:, j + 1, :]
    cbo_ref[:, CONV_WIDTH - 2, :] = new

    beta = jax.nn.sigmoid(bl_ref[...])
    g = -jnp.exp(alog_ref[...]) * jax.nn.softplus(al_ref[...] + dtb_ref[...])
    dec = jnp.exp(g)

    q_t, k_t = [], []
    for h in range(heads):
        q = act_ref[:, h * dk:(h + 1) * dk]
        k = act_ref[:, wq + h * dk:wq + (h + 1) * dk]
        qn = q * lax.rsqrt(jnp.sum(q * q, axis=-1, keepdims=True) + EPS) * (dk ** -0.5)
        kn = k * lax.rsqrt(jnp.sum(k * k, axis=-1, keepdims=True) + EPS)
        q_t.append(qn.T)
        k_t.append(kn.T)

    def per_head(fn):
        return jnp.stack([fn(h) for h in range(heads)], axis=0)

    for i in range(nb):
        row = slice(i, i + 1)
        kcol = per_head(lambda h: k_t[h][:, i:i + 1])
        qcol = per_head(lambda h: q_t[h][:, i:i + 1])
        v = per_head(lambda h: act_ref[row, 2 * wq + h * dv:2 * wq + (h + 1) * dv])
        z = per_head(lambda h: z_ref[row, h * dv:(h + 1) * dv])
        b_i = per_head(lambda h: beta[row, h:h + 1])
        d_i = per_head(lambda h: dec[row, h:h + 1])
        s = st_ref[i] * d_i
        u = b_i * (v - jnp.sum(kcol * s, axis=1, keepdims=True))
        s = s + kcol * u
        o = jnp.sum(qcol * s, axis=1, keepdims=True)
        sto_ref[i] = s
        o = o * lax.rsqrt(jnp.mean(o * o, axis=-1, keepdims=True) + EPS) * on_ref[...]
        o = o * _silu(z)
        for h in range(heads):
            dn_ref[row, h * dv:(h + 1) * dv] = o[h]


def deltanet_sample(proj, col, conv_buf, state, conv_w, a_log, dt_bias, o_norm, heads, dk, dv, nb):
    n_seq = proj.shape[0]
    conv_ch = 2 * heads * dk + heads * dv
    zw = heads * dv
    body = functools.partial(_dn_sample_body, nb=nb, heads=heads, dk=dk, dv=dv)
    return pl.pallas_call(
        body,
        out_shape=(jax.ShapeDtypeStruct((n_seq, zw), F32),
                   jax.ShapeDtypeStruct(state.shape, F32),
                   jax.ShapeDtypeStruct(conv_buf.shape, F32)),
        grid=(n_seq // nb,),
        in_specs=[
            pl.BlockSpec((nb, conv_ch), lambda i: (i, col["qkv"] // conv_ch)),
            pl.BlockSpec((nb, zw), lambda i: (i, col["z"] // zw)),
            pl.BlockSpec((nb, LANE), lambda i: (i, col["b"] // LANE)),
            pl.BlockSpec((nb, LANE), lambda i: (i, col["a"] // LANE)),
            pl.BlockSpec((nb, CONV_WIDTH - 1, conv_ch), lambda i: (i, 0, 0)),
            pl.BlockSpec((nb, heads, dk, dv), lambda i: (i, 0, 0, 0)),
            _resident((CONV_WIDTH, conv_ch)),
            _resident((1, LANE)),
            _resident((1, LANE)),
            _resident((1, dv)),
        ],
        out_specs=(pl.BlockSpec((nb, zw), lambda i: (i, 0)),
                   pl.BlockSpec((nb, heads, dk, dv), lambda i: (i, 0, 0, 0)),
                   pl.BlockSpec((nb, CONV_WIDTH - 1, conv_ch), lambda i: (i, 0, 0))),
        scratch_shapes=[
            pltpu.VMEM((nb, conv_ch), F32),
        ],
        compiler_params=_params("parallel"),
        name="deltanet_sample",
    )(proj, proj, proj, proj, conv_buf, state, conv_w, a_log, dt_bias, o_norm)


def _xattn_prompt_body(q_ref, k_ref, v_ref, o_ref, *, heads, dh):
    for h in range(heads):
        sl = slice(h * dh, (h + 1) * dh)
        q = q_ref[:, sl].astype(BF16)
        k = k_ref[:, sl].astype(BF16)
        v = v_ref[:, sl].astype(BF16)
        s = lax.dot_general(q, k, (((1,), (1,)), ((), ())), preferred_element_type=F32) * (dh ** -0.5)
        e = jnp.exp(s - jnp.max(s, axis=-1, keepdims=True))
        p = e / jnp.sum(e, axis=-1, keepdims=True)
        o_ref[:, sl] = _dot(p.astype(BF16), v)


def xattn_prompt(proj, cq_col, kv, batch, seq, n_mem, heads, dh, tq):
    xw = heads * dh
    steps = seq // tq
    body = functools.partial(_xattn_prompt_body, heads=heads, dh=dh)
    return pl.pallas_call(
        body,
        out_shape=jax.ShapeDtypeStruct((batch * seq, xw), F32),
        grid=(batch, steps),
        in_specs=[
            pl.BlockSpec((tq, xw), lambda b, i: (b * steps + i, cq_col // xw)),
            pl.BlockSpec((n_mem, xw), lambda b, i: (b, 0)),
            pl.BlockSpec((n_mem, xw), lambda b, i: (b, 1)),
        ],
        out_specs=pl.BlockSpec((tq, xw), lambda b, i: (b * steps + i, 0)),
        compiler_params=_params("parallel", "parallel"),
        name="xattn_prompt",
    )(proj, kv, kv)


def _xattn_sample_body(q_ref, k_ref, v_ref, o_ref, *, nb, heads, dh):
    rep = SUBLANE // heads
    n_t = k_ref.shape[2] // SUBLANE

    def over_groups(x, op):
        out = x
        for t in range(1, rep):
            out = op(out, pltpu.roll(x, t * heads, 0))
        return out

    for i in range(nb):
        row = slice(i, i + 1)
        q8 = jnp.concatenate([q_ref[row, h * dh:(h + 1) * dh] for h in range(heads)] * rep, axis=0)
        k3 = k_ref[0, i].reshape(n_t, SUBLANE, dh)
        v3 = v_ref[0, i].reshape(n_t, SUBLANE, dh)
        s = jnp.sum(k3 * q8[None], axis=-1, keepdims=True) * (dh ** -0.5)
        m = over_groups(jnp.max(s, axis=0), jnp.maximum)
        e = jnp.exp(s - m[None])
        den = over_groups(jnp.sum(e, axis=0), jnp.add)
        p = e / den[None]
        acc = over_groups(jnp.sum(p * v3, axis=0), jnp.add)
        for h in range(heads):
            o_ref[row, h * dh:(h + 1) * dh] = acc[h:h + 1, :]


def xattn_sample(proj, cq_col, cache_k, cache_v, layer, nb):
    n_layers, n_seq, n_mem, heads, dh = cache_k.shape
    assert SUBLANE % heads == 0 and (n_mem * heads) % SUBLANE == 0
    xw = heads * dh
    body = functools.partial(_xattn_sample_body, nb=nb, heads=heads, dh=dh)
    cache_spec = pl.BlockSpec((1, nb, n_mem * heads, dh), lambda i: (layer, i, 0, 0))
    flat = (n_layers, n_seq, n_mem * heads, dh)
    return pl.pallas_call(
        body,
        out_shape=jax.ShapeDtypeStruct((n_seq, xw), F32),
        grid=(n_seq // nb,),
        in_specs=[pl.BlockSpec((nb, xw), lambda i: (i, cq_col // xw)), cache_spec, cache_spec],
        out_specs=pl.BlockSpec((nb, xw), lambda i: (i, 0)),
        compiler_params=_params("parallel"),
        name="xattn_sample",
    )(proj, cache_k.reshape(flat), cache_v.reshape(flat))


def _pool_mix(sums, cnts, u_of, pw_ref, ps_ref, o_ref, gw):
    for gi in range(len(POOL_WINDOWS)):
        cols = slice(gi * gw, (gi + 1) * gw)
        d = sums[gi] / cnts[gi] - u_of(cols)
        o_ref[:, cols] = _wdot(d, pw_ref[gi]) * ps_ref[:, cols]


def _pool_prompt_body(x_ref, g_ref, w_ref, pw_ref, ps_ref, o_ref, cq_ref, tail_ref, xbuf, s2, s4, s8, *, tt, gw):
    t = pl.program_id(1)
    n = POOL_HALO + tt
    pw_total = xbuf.shape[1]

    @pl.when(t == 0)
    def _():
        xbuf[0:POOL_HALO, :] = jnp.zeros((POOL_HALO, pw_total), F32)

    proj = _wdot(_rms(x_ref[...], g_ref[...]), w_ref[...])
    cq_ref[...] = proj[:, pw_total:]
    xbuf[POOL_HALO:n, :] = proj[:, 0:pw_total]
    s2[8:n, :] = xbuf[8:n, :] + xbuf[7:n - 1, :]
    s4[16:n, :] = s2[16:n, gw:] + s2[14:n - 2, gw:]
    s8[24:n, :] = s4[24:n, gw:] + s4[20:n - 4, gw:]
    s16 = s8[32:n, gw:] + s8[24:n - 8, gw:]
    sums = [s2[POOL_HALO:n, 0:gw], s4[POOL_HALO:n, 0:gw], s8[POOL_HALO:n, 0:gw], s16]
    pos = t * tt + lax.broadcasted_iota(jnp.int32, (tt, 1), 0)
    cnts = [jnp.minimum(pos + 1, w).astype(F32) for w in POOL_WINDOWS]
    _pool_mix(sums, cnts, lambda cols: xbuf[POOL_HALO:n, cols], pw_ref, ps_ref, o_ref, gw)
    tail_ref[0] = xbuf[tt:n, :]
    xbuf[0:POOL_HALO, :] = xbuf[tt:n, :]


def proj_pool_prompt(x, g, w, pool_w, pool_scale, batch, seq, tt):
    m, d = x.shape
    n_out = w.shape[1]
    pw_total = pool_scale.shape[-1]
    gw = pw_total // len(POOL_WINDOWS)
    steps = seq // tt
    n = POOL_HALO + tt
    body = functools.partial(_pool_prompt_body, tt=tt, gw=gw)
    return pl.pallas_call(
        body,
        out_shape=(jax.ShapeDtypeStruct((m, pw_total), F32),
                   jax.ShapeDtypeStruct((m, n_out - pw_total), F32),
                   jax.ShapeDtypeStruct((batch, POOL_HALO, pw_total), F32)),
        grid=(batch, steps),
        in_specs=[
            pl.BlockSpec((tt, d), lambda b, t: (b * steps + t, 0)),
            _resident((1, d)),
            _resident((d, n_out)),
            _resident(pool_w.shape),
            _resident((1, pw_total)),
        ],
        out_specs=(pl.BlockSpec((tt, pw_total), lambda b, t: (b * steps + t, 0)),
                   pl.BlockSpec((tt, n_out - pw_total), lambda b, t: (b * steps + t, 0)),
                   pl.BlockSpec((1, POOL_HALO, pw_total), lambda b, t: (b, 0, 0))),
        scratch_shapes=[
            pltpu.VMEM((n, pw_total), F32),
            pltpu.VMEM((n, pw_total), F32),
            pltpu.VMEM((n, pw_total - gw), F32),
            pltpu.VMEM((n, pw_total - 2 * gw), F32),
        ],
        compiler_params=_params("parallel", "arbitrary"),
        name="proj_pool_prompt",
    )(x, g.reshape(1, d), w, pool_w, pool_scale)


def _pool_sample_body(u_ref, st_ref, pw_ref, ps_ref, o_ref, sto_ref, *, gw, buf):
    pw_total = gw * len(POOL_WINDOWS)
    new = u_ref[:, 0:pw_total]
    sums, cnts = [], []
    for gi, w in enumerate(POOL_WINDOWS):
        cols = slice(gi * gw, (gi + 1) * gw)
        s = new[:, cols]
        for i in range(1, w):
            s = s + st_ref[:, buf - i, cols]
        sums.append(s)
        cnts.append(float(min(PAST_LEN + 1, w)))
    _pool_mix(sums, cnts, lambda cols: new[:, cols], pw_ref, ps_ref, o_ref, gw)
    sto_ref[:, 0:buf - 1, :] = st_ref[:, 1:buf, :]
    sto_ref[:, buf - 1, :] = new


def pool_sample(proj, state, pool_w, pool_scale):
    n_seq, buf, pw_total = state.shape
    gw = pw_total // len(POOL_WINDOWS)
    body = functools.partial(_pool_sample_body, gw=gw, buf=buf)
    return pl.pallas_call(
        body,
        out_shape=(jax.ShapeDtypeStruct((n_seq, pw_total), F32),
                   jax.ShapeDtypeStruct(state.shape, F32)),
        compiler_params=pltpu.CompilerParams(vmem_limit_bytes=VMEM_LIMIT_BYTES),
        name="pool_sample",
    )(proj, state, pool_w, pool_scale)


def _route_top2(h, g, r, n_experts):
    logits = _wdot(_rms(h, g), r)
    lane = lax.broadcasted_iota(jnp.int32, logits.shape, 1)
    lm = jnp.where(lane < n_experts, logits, -jnp.inf)
    m1 = jnp.max(lm, axis=1, keepdims=True)
    i1 = jnp.min(jnp.where(lm == m1, lane, LANE), axis=1, keepdims=True)
    lm2 = jnp.where(lane == i1, -jnp.inf, lm)
    m2 = jnp.max(lm2, axis=1, keepdims=True)
    i2 = jnp.min(jnp.where(lm2 == m2, lane, LANE), axis=1, keepdims=True)
    e2 = jnp.exp(m2 - m1)
    den = 1.0 + e2
    idx = jnp.where(lane == 0, i1, jnp.where(lane == 1, i2, 0))
    gate = jnp.where(lane == 0, 1.0 / den, jnp.where(lane == 1, e2 / den, 0.0))
    return idx, gate


def _dispatch_body(dest_ref, h_ref, gate_ref, zeros_hbm, o_hbm, ext, sem, *, tb, n):
    del zeros_hbm
    i = pl.program_id(0)
    slot = i % 2
    groups = tb // SUBLANE
    d = h_ref.shape[1]

    def row_copy(s, t, u, row):
        return pltpu.make_async_copy(ext.at[s, t, pl.ds(u, 1)], o_hbm.at[pl.ds(row, 1)], sem.at[s])

    def wait_slot(s):
        def body(t, carry):
            pltpu.make_async_copy(ext.at[s, t], o_hbm.at[pl.ds(0, SUBLANE)], sem.at[s]).wait()
            return carry
        lax.fori_loop(0, groups, body, 0)

    @pl.when(i >= 2)
    def _():
        wait_slot(slot)

    ext[slot, :, :, 0:d] = h_ref[...].reshape(groups, SUBLANE, d)
    ext[slot, :, :, d:d + LANE] = gate_ref[...].reshape(groups, SUBLANE, LANE)

    def body(t, carry):
        for u in range(SUBLANE):
            row_copy(slot, t, u, dest_ref[0, 0, t * SUBLANE + u]).start(priority=u % 2)
        return carry
    lax.fori_loop(0, groups, body, 0)

    @pl.when(i == n - 1)
    def _():
        wait_slot(slot)
        if n >= 2:
            wait_slot(1 - slot)


def moe_dispatch(h_all, row_offset, rows, gates, dest, into, tb, name):
    d = h_all.shape[1]
    off = row_offset // tb
    n = rows // tb
    body = functools.partial(_dispatch_body, tb=tb, n=n)
    return pl.pallas_call(
        body,
        out_shape=jax.ShapeDtypeStruct(into.shape, F32),
        grid=(n,),
        in_specs=[
            pl.BlockSpec((1, 1, tb), lambda i: (i, 0, 0), memory_space=pltpu.SMEM),
            pl.BlockSpec((tb, d), lambda i: (i + off, 0)),
            pl.BlockSpec((tb, LANE), lambda i: (i, 0)),
            pl.BlockSpec(memory_space=pl.ANY),
        ],
        out_specs=pl.BlockSpec(memory_space=pl.ANY),
        scratch_shapes=[pltpu.VMEM((2, tb // SUBLANE, SUBLANE, d + LANE), F32), pltpu.SemaphoreType.DMA((2,))],
        input_output_aliases={3: 0},
        compiler_params=_params("arbitrary"),
        name=name,
    )(dest.reshape(n, 1, tb), h_all, gates, into)


def _experts_body(te_ref, nu_ref, nb_ref, fill_ref, g_first_ref, g_next_ref, s_cur_ref,
                  h_hbm, g_ref, wg_ref, wu_ref, wo_ref, y_hbm,
                  xbuf, acc, xn_ref, gsem, ssem, *, tm, block, n_rows):
    i = pl.program_id(0)
    j = pl.program_id(1)
    n_tiles = pl.num_programs(0)
    nj = pl.num_programs(1)
    n_used = nu_ref[0]
    slot = i % 2
    valid = i < n_used
    chunks = tm // SUBLANE
    d = xn_ref.shape[1]

    def gather(idx_ref, s):
        def body(t, carry):
            row0 = pl.multiple_of(idx_ref[0, 0, t], SUBLANE)
            pltpu.make_async_copy(h_hbm.at[pl.ds(row0, SUBLANE), pl.ds(0, d)], xbuf.at[s, t], gsem.at[s]).start()
            return carry
        lax.fori_loop(0, chunks, body, 0)

    def wait_gather(s):
        def body(t, carry):
            pltpu.make_async_copy(h_hbm.at[pl.ds(0, SUBLANE), pl.ds(0, d)], xbuf.at[s, t], gsem.at[s]).wait()
            return carry
        lax.fori_loop(0, chunks, body, 0)

    def scatter():
        def body(t, carry):
            row0 = pl.multiple_of(s_cur_ref[0, 0, t], SUBLANE)
            pltpu.make_async_copy(acc.at[t], y_hbm.at[pl.ds(row0, SUBLANE)], ssem.at[0]).start()
            return carry
        lax.fori_loop(0, chunks, body, 0)

    def wait_scatter():
        def body(t, carry):
            pltpu.make_async_copy(acc.at[t], y_hbm.at[pl.ds(0, SUBLANE)], ssem.at[0]).wait()
            return carry
        lax.fori_loop(0, chunks, body, 0)

    @pl.when(j == 0)
    def _():
        @pl.when(i == 0)
        def _():
            gather(g_first_ref, 0)
            acc[...] = jnp.zeros_like(acc)
            first_free = nb_ref[0] * (block // SUBLANE)
            n_free = n_rows // SUBLANE - first_free

            def zero_copy(row0):
                return pltpu.make_async_copy(acc.at[0], y_hbm.at[pl.ds(pl.multiple_of(row0, SUBLANE), SUBLANE)],
                                             ssem.at[0])

            def fill(c, carry):
                zero_copy((first_free + c) * SUBLANE).start()
                zero_copy(n_rows + (first_free + c) * SUBLANE).start()
                return carry
            lax.fori_loop(0, n_free, fill, 0)

            def fill_spare(c, carry):
                zero_copy(2 * n_rows + c * SUBLANE).start()
                return carry
            lax.fori_loop(0, chunks, fill_spare, 0)

            def drain(c, carry):
                zero_copy(0).wait()
                return carry
            lax.fori_loop(0, 2 * n_free + chunks, drain, 0)

        @pl.when(i + 1 < n_used)
        def _():
            gather(g_next_ref, 1 - slot)

        @pl.when(valid)
        def _():
            wait_gather(slot)
            xn_ref[...] = _rms(xbuf[slot].reshape(tm, d), g_ref[...]).astype(BF16)

        @pl.when((i >= 1) & (i <= n_used))
        def _():
            wait_scatter()

        @pl.when(valid)
        def _():
            acc[...] = jnp.zeros_like(acc)

    def ff_step(n_chunks):
        rows = n_chunks * SUBLANE
        xn = xn_ref[0:rows, :]
        a = _dot(xn, wg_ref[0].astype(BF16))
        b = _dot(xn, wu_ref[0].astype(BF16))
        part = _dot((_silu(a) * b).astype(BF16), wo_ref[0].astype(BF16)).reshape(n_chunks, SUBLANE, d)
        acc[0:n_chunks] += part

    half_full = fill_ref[i] <= chunks // 2

    @pl.when(valid & jnp.logical_not(half_full))
    def _():
        ff_step(chunks)

    @pl.when(valid & half_full)
    def _():
        ff_step(chunks // 2)

    @pl.when(valid & (j == nj - 1))
    def _():
        scatter()

        @pl.when(i == n_tiles - 1)
        def _():
            wait_scatter()


def moe_experts(h_ps, g, wi, wo, gather_rows, scatter_rows, tile_expert, n_used, n_blocks_used, tile_fill,
                block, n_rows, tm, tf):
    d = wo.shape[2]
    f = wo.shape[1]
    nj = f // tf
    n_tiles = gather_rows.shape[0]
    chunks = tm // SUBLANE

    def wblock(col_off):
        def index_map(i, j, te, nu, nb, tf_):
            jj = jnp.where(i < nu[0], j, nj - 1)
            return (te[i], 0, col_off + jj)
        return index_map

    def woblock(i, j, te, nu, nb, tf_):
        return (te[i], jnp.where(i < nu[0], j, nj - 1), 0)

    def smem_tile(index_map):
        return pl.BlockSpec((1, 1, chunks), index_map, memory_space=pltpu.SMEM)

    grid_spec = pltpu.PrefetchScalarGridSpec(
        num_scalar_prefetch=4,
        grid=(n_tiles, nj),
        in_specs=[
            smem_tile(lambda i, j, te, nu, nb, tf_: (0, 0, 0)),
            smem_tile(lambda i, j, te, nu, nb, tf_: (jnp.minimum(i + 1, n_tiles - 1), 0, 0)),
            smem_tile(lambda i, j, te, nu, nb, tf_: (i, 0, 0)),
            pl.BlockSpec(memory_space=pl.ANY),
            pl.BlockSpec((1, d), lambda i, j, te, nu, nb, tf_: (0, 0)),
            pl.BlockSpec((1, d, tf), wblock(0)),
            pl.BlockSpec((1, d, tf), wblock(nj)),
            pl.BlockSpec((1, tf, d), woblock),
        ],
        out_specs=pl.BlockSpec(memory_space=pl.ANY),
        scratch_shapes=[
            pltpu.VMEM((2, chunks, SUBLANE, d), F32),
            pltpu.VMEM((chunks, SUBLANE, d), F32),
            pltpu.VMEM((tm, d), BF16),
            pltpu.SemaphoreType.DMA((2,)),
            pltpu.SemaphoreType.DMA((1,)),
        ],
    )
    return pl.pallas_call(
        functools.partial(_experts_body, tm=tm, block=block, n_rows=n_rows),
        out_shape=jax.ShapeDtypeStruct((2 * n_rows + tm, d), F32),
        grid_spec=grid_spec,
        compiler_params=_params("arbitrary", "arbitrary"),
        name="moe_experts",
    )(tile_expert, n_used, n_blocks_used, tile_fill, gather_rows, gather_rows, scatter_rows, h_ps,
      g.reshape(1, d), wi, wi, wo)


def _combine_body(nb_ref, hx_ref, y1_ref, y2_ref, fn_ref, o_ref, *, d, final_norm):
    @pl.when(pl.program_id(0) < nb_ref[0])
    def _():
        hx = hx_ref[...]
        out = hx[:, 0:d] + (hx[:, d:d + 1] * y1_ref[...] + hx[:, d + 1:d + 2] * y2_ref[...])
        if final_norm:
            out = _rms(out, fn_ref[...])
        o_ref[...] = out

    @pl.when(pl.program_id(0) >= nb_ref[0])
    def _():
        o_ref[...] = jnp.zeros_like(o_ref)


def moe_combine(h_ps, y, n_blocks_used, choice_stride, fn, tb, final_norm, name):
    rows, dx = h_ps.shape
    d = y.shape[1]
    off2 = choice_stride // tb

    def used(i, nb):
        return jnp.minimum(i, nb[0] - 1)

    grid_spec = pltpu.PrefetchScalarGridSpec(
        num_scalar_prefetch=1,
        grid=(rows // tb,),
        in_specs=[
            pl.BlockSpec((tb, dx), lambda i, nb: (used(i, nb), 0)),
            pl.BlockSpec((tb, d), lambda i, nb: (used(i, nb), 0)),
            pl.BlockSpec((tb, d), lambda i, nb: (used(i, nb) + off2, 0)),
            pl.BlockSpec((1, d), lambda i, nb: (0, 0)),
        ],
        out_specs=pl.BlockSpec((tb, d), lambda i, nb: (i, 0)),
    )
    return pl.pallas_call(
        functools.partial(_combine_body, d=d, final_norm=final_norm),
        out_shape=jax.ShapeDtypeStruct((rows, d), F32),
        grid_spec=grid_spec,
        compiler_params=_params("arbitrary"),
        name=name,
    )(n_blocks_used, h_ps, y, y, fn.reshape(1, d))


def _unpermute_body(cur_ref, nxt_ref, y_hbm, o_ref, buf, sem, *, tb):
    i = pl.program_id(0)
    n = pl.num_programs(0)
    slot = i % 2
    groups = tb // SUBLANE

    def start(idx_ref, s):
        def body(t, carry):
            for u in range(SUBLANE):
                row = idx_ref[0, 0, t * SUBLANE + u]
                pltpu.make_async_copy(y_hbm.at[pl.ds(row, 1)], buf.at[s, t, pl.ds(u, 1)], sem.at[s]).start(
                    priority=u % 2)
            return carry
        lax.fori_loop(0, groups, body, 0)

    @pl.when(i == 0)
    def _():
        start(cur_ref, 0)

    @pl.when(i + 1 < n)
    def _():
        start(nxt_ref, 1 - slot)

    def wait_body(t, carry):
        pltpu.make_async_copy(y_hbm.at[pl.ds(0, SUBLANE)], buf.at[slot, t], sem.at[slot]).wait()
        return carry
    lax.fori_loop(0, groups, wait_body, 0)
    o_ref[...] = buf[slot].reshape(o_ref.shape)


def moe_unpermute(y_ps, dest, tb, name):
    rows = dest.shape[0]
    d = y_ps.shape[1]
    n = rows // tb
    idx = dest.reshape(n, 1, tb)

    def smem_tile(index_map):
        return pl.BlockSpec((1, 1, tb), index_map, memory_space=pltpu.SMEM)

    return pl.pallas_call(
        functools.partial(_unpermute_body, tb=tb),
        out_shape=jax.ShapeDtypeStruct((rows, d), F32),
        grid=(n,),
        in_specs=[smem_tile(lambda i: (i, 0, 0)), smem_tile(lambda i: (jnp.minimum(i + 1, n - 1), 0, 0)),
                  pl.BlockSpec(memory_space=pl.ANY)],
        out_specs=pl.BlockSpec((tb, d), lambda i: (i, 0)),
        scratch_shapes=[pltpu.VMEM((2, tb // SUBLANE, SUBLANE, d), F32), pltpu.SemaphoreType.DMA((2,))],
        compiler_params=_params("arbitrary"),
        name=name,
    )(idx, idx, y_ps)


def moe_plan(idx, n_experts, tm, block):
    n_tok, k = idx.shape
    assert k == 2
    n_pairs = n_experts * n_experts
    chunks_per_tile = tm // SUBLANE
    n_rows = -(-(n_tok + n_pairs * (SUBLANE - 1)) // block) * block
    n_chunks = n_rows // SUBLANE

    pair = idx[:, 0] * n_experts + idx[:, 1]
    onehot = (pair[:, None] == jnp.arange(n_pairs, dtype=jnp.int32)[None, :]).astype(jnp.int32)
    csum = jnp.cumsum(onehot, axis=0)
    counts = csum[-1]
    rank = jnp.sum(csum * onehot, axis=1) - 1
    padded = (counts + SUBLANE - 1) // SUBLANE * SUBLANE
    total = jnp.sum(padded)
    last_group = jnp.max(jnp.where(counts > 0, jnp.arange(n_pairs, dtype=jnp.int32), 0))
    padded = padded + jnp.where(jnp.arange(n_pairs) == last_group, (-total) % block, 0)
    group_end = jnp.cumsum(padded)
    group_start = group_end - padded
    dest = jnp.sum(onehot * group_start[None, :], axis=1) + rank
    rows_used = group_end[-1]

    chunk_row = jnp.arange(n_chunks, dtype=jnp.int32) * SUBLANE
    chunk_group = jnp.sum((chunk_row[:, None] >= group_end[None, :]).astype(jnp.int32), axis=1)
    chunk_used = chunk_row < rows_used
    chunk_group = jnp.minimum(chunk_group, n_pairs - 1)
    chunk_experts = jnp.stack([chunk_group // n_experts, chunk_group % n_experts], axis=1)
    experts = jnp.arange(n_experts, dtype=jnp.int32)
    member = ((chunk_experts[:, :, None] == experts[None, None, :]) & chunk_used[:, None, None])
    member = member.astype(jnp.int32)
    member_e = jnp.sum(member, axis=1)
    csum_e = jnp.cumsum(member_e, axis=0)
    counts_e = csum_e[-1]
    tiles_e = (counts_e + chunks_per_tile - 1) // chunks_per_tile
    tile_end = jnp.cumsum(tiles_e)
    start_e = (tile_end - tiles_e) * chunks_per_tile
    pos = jnp.sum(member * (start_e[None, None, :] + csum_e[:, None, :] - 1), axis=2)
    n_tiles = 2 * n_chunks // chunks_per_tile + n_experts
    n_pos = n_tiles * chunks_per_tile
    pos = jnp.where(chunk_used[:, None], pos, n_pos)
    entry = 2 * jnp.arange(n_chunks, dtype=jnp.int32)[:, None] + jnp.arange(2, dtype=jnp.int32)[None, :] + 1
    table = jnp.zeros((n_pos,), jnp.int32).at[pos.reshape(-1)].set(entry.reshape(-1), mode="drop")
    table = table.reshape(n_tiles, chunks_per_tile)
    filled = table > 0
    src_chunk = (table - 1) // 2
    choice = (table - 1) % 2
    spare = 2 * n_rows + jnp.arange(chunks_per_tile, dtype=jnp.int32) * SUBLANE
    gather_rows = jnp.where(filled, src_chunk * SUBLANE, 0)
    scatter_rows = jnp.where(filled, choice * n_rows + src_chunk * SUBLANE, spare[None, :])

    n_used = tile_end[-1]
    tile_ids = jnp.arange(n_tiles, dtype=jnp.int32)
    tile_expert = jnp.sum((tile_ids[:, None] >= tile_end[None, :]).astype(jnp.int32), axis=1)
    last_expert = jnp.sum((n_used - 1 >= tile_end).astype(jnp.int32))
    tile_expert = jnp.where(tile_ids < n_used, tile_expert, last_expert).astype(jnp.int32)
    return (dest.astype(jnp.int32),
            gather_rows.reshape(n_tiles, 1, chunks_per_tile).astype(jnp.int32),
            scatter_rows.reshape(n_tiles, 1, chunks_per_tile).astype(jnp.int32),
            tile_expert, n_used.reshape(1).astype(jnp.int32),
            (rows_used // block).reshape(1).astype(jnp.int32),
            jnp.sum(filled.astype(jnp.int32), axis=1), n_rows)


def _pad_lanes(v):
    return jnp.zeros((1, LANE), F32).at[0, :v.shape[0]].set(v)


def kernel(x_prompt, x_sample, state_dn, state_dn_conv, state_pool, cache_mem_k, cache_mem_v, mem_prompt, a_norm1, a_w_in, a_conv_w, a_a_log, a_dt_bias, a_o_norm, a_w_out, a_norm2, a_ffn_wi, a_ffn_wo, b_norm1, b_w_in, b_pool_w, b_pool_scale, b_w_out, b_norm2, b_router, b_moe_wi, b_moe_wo, m_norm, m_w_k, m_w_v, final_norm):
    bp, seq, d = x_prompt.shape
    bs = x_sample.shape[0]
    assert x_sample.shape[1] == 1
    depth = m_norm.shape[0]
    assert depth % 2 == 0, "the output norm is fused into the expert block of the last (odd) layer"
    heads, dk, dv = state_dn.shape[2], state_dn.shape[3], state_dn.shape[4]
    conv_ch = state_dn_conv.shape[-1]
    qk_w, v_w = heads * dk, heads * dv
    assert conv_ch == 2 * qk_w + v_w and dk == LANE and dv == LANE and heads <= LANE
    n_mem, x_heads, x_dh = cache_mem_k.shape[2], cache_mem_k.shape[3], cache_mem_k.shape[4]
    xw = x_heads * x_dh
    pool_w_total = state_pool.shape[-1]
    pool_buf = state_pool.shape[2]
    d_ff = a_ffn_wo.shape[1]

    hp = x_prompt.reshape(bp * seq, d)
    hs = x_sample.reshape(bs, d)

    o1, o2, o3, o4 = conv_ch, conv_ch + v_w, conv_ch + v_w + heads, conv_ch + v_w + 2 * heads
    col_a = {"qkv": 0, "z": conv_ch, "cq": conv_ch + v_w, "b": conv_ch + v_w + xw, "a": conv_ch + v_w + xw + LANE}
    n_a = -(-(col_a["a"] + LANE) // 512) * 512

    dn_p, conv_p, pool_p = [], [], []
    mk_all = mv_all = None
    dn_s, conv_s, pool_s = [], [], []
    for i in range(depth):
        j = i // 2
        w_kv = jnp.concatenate([m_w_k[i], m_w_v[i]], axis=1).astype(BF16)
        kv, mk_all, mv_all = mem_kv(mem_prompt.reshape(bp * n_mem, d), m_norm[i], w_kv, x_heads, x_dh, 256,
                                    f"mem_kv_{i}", i, depth, into=None if i == 0 else (mk_all, mv_all))
        if i % 2 == 0:
            w = a_w_in[j]
            def regroup(t):
                rows = t.shape[0]
                return jnp.concatenate(
                    [t[:, :o2], t[:, o4:], t[:, o2:o3], jnp.zeros((rows, LANE - heads), t.dtype),
                     t[:, o3:o4], jnp.zeros((rows, n_a - col_a["a"] - heads), t.dtype)], axis=1)

            assert col_a["b"] == o2 + xw
            a_log, dt_b = _pad_lanes(a_a_log[j]), _pad_lanes(a_dt_bias[j])
            o_norm = a_o_norm[j].reshape(1, dv)

            proj_p = rms_matmul(hp, a_norm1[j], regroup(w.astype(BF16)), 512, f"a_in_p_{i}")
            proj_s = regroup(rms_matmul(hs, a_norm1[j], w, bs, f"a_in_s_{i}"))
            mix_p, s_p = deltanet_prompt(proj_p, col_a, a_conv_w[j], a_log, dt_b, o_norm,
                                         bp, seq, heads, dk, dv, rows=DN_STEP_CHUNKS * CHUNK)
            mix_s, s_s, c_s = deltanet_sample(proj_s, col_a, state_dn_conv[j], state_dn[j], a_conv_w[j],
                                              a_log, dt_b, o_norm, heads, dk, dv, nb=SUBLANE)
            xa_p = xattn_prompt(proj_p, col_a["cq"], kv, bp, seq, n_mem, x_heads, x_dh, tq=min(1024, seq))
            xa_s = xattn_sample(proj_s, col_a["cq"], cache_mem_k, cache_mem_v, i, nb=SUBLANE)
            hp = out_proj_ffn(hp, mix_p, xa_p, a_w_out[j].astype(BF16), a_norm2[j], a_ffn_wi[j].astype(BF16),
                              a_ffn_wo[j].astype(BF16), 256, f"a_out_ffn_p_{i}")
            hs = out_proj(hs, mix_s, xa_s, a_w_out[j], bs, f"a_out_s_{i}")
            hs = ffn_dense(hs, a_norm2[j], a_ffn_wi[j], a_ffn_wo[j], bs, f"a_ffn_s_{i}", tf=256)
            dn_p.append(s_p)
            conv_p.append(proj_p.reshape(bp, seq, n_a)[:, seq - (CONV_WIDTH - 1):, :conv_ch])
            dn_s.append(s_s)
            conv_s.append(c_s)
        else:
            ps = b_pool_scale[j].reshape(1, pool_w_total)
            last = i == depth - 1

            mix_p, cq_p, pool_tail = proj_pool_prompt(hp, b_norm1[j], b_w_in[j].astype(BF16),
                                                      b_pool_w[j].astype(BF16), ps, bp, seq, tt=512)
            proj_s = rms_matmul(hs, b_norm1[j], b_w_in[j], bs, f"b_in_s_{i}")
            mix_s, buf_s = pool_sample(proj_s, state_pool[j], b_pool_w[j], ps)
            xa_p = xattn_prompt(cq_p, 0, kv, bp, seq, n_mem, x_heads, x_dh, tq=min(1024, seq))
            xa_s = xattn_sample(proj_s, pool_w_total, cache_mem_k, cache_mem_v, i, nb=SUBLANE)
            n_p = bp * seq
            n_experts = b_router.shape[-1]
            r_pad = jnp.concatenate([b_router[j], jnp.zeros((d, LANE - n_experts), F32)], axis=1)
            h_all, idx_p, gate_p = out_proj(hp, mix_p, xa_p, b_w_out[j].astype(BF16), 512, f"b_out_p_{i}",
                                            out_rows=n_p + bs, route=(b_norm2[j], r_pad.astype(BF16), n_experts))
            h_all, idx_s, gate_s = out_proj(hs, mix_s, xa_s, b_w_out[j], bs, f"b_out_s_{i}", out_rows=n_p + bs,
                                            into=h_all, row_offset=n_p, route=(b_norm2[j], r_pad, n_experts))
            idx = jnp.concatenate([idx_p[:n_p, :TOP_K], idx_s[:, :TOP_K]], axis=0)
            (dest, gather_rows, scatter_rows, tile_expert, n_used, n_blocks_used, tile_fill, n_rows) = moe_plan(
                idx, n_experts, MOE_TILE, MOE_ROW_BLOCK)
            h_ps = jnp.zeros((n_rows, d + LANE), F32)
            h_ps = moe_dispatch(h_all, 0, n_p, gate_p, dest[:n_p], h_ps, 512, f"b_dispatch_p_{i}")
            h_ps = moe_dispatch(h_all, n_p, bs, gate_s, dest[n_p:], h_ps, bs, f"b_dispatch_s_{i}")
            y = moe_experts(h_ps, b_norm2[j], b_moe_wi[j], b_moe_wo[j], gather_rows, scatter_rows, tile_expert,
                            n_used, n_blocks_used, tile_fill, MOE_ROW_BLOCK, n_rows, MOE_TILE, MOE_FF_TILE)
            y_ps = moe_combine(h_ps, y, n_blocks_used, n_rows, final_norm, MOE_ROW_BLOCK, last,
                               f"b_combine_{i}")
            hp = moe_unpermute(y_ps, dest[:n_p], 512, f"b_unpermute_p_{i}")
            hs = moe_unpermute(y_ps, dest[n_p:], bs, f"b_unpermute_s_{i}")
            pool_p.append(pool_tail[:, POOL_HALO - pool_buf:, :])
            pool_s.append(buf_s)

    y_prompt = hp.reshape(bp, seq, d)
    y_sample = hs.reshape(bs, 1, d)
    return (y_prompt, y_sample, jnp.stack(dn_p), jnp.stack(conv_p), jnp.stack(pool_p),
            mk_all.reshape(depth, bp, n_mem, x_heads, x_dh), mv_all.reshape(depth, bp, n_mem, x_heads, x_dh),
            jnp.stack(dn_s), jnp.stack(conv_s), jnp.stack(pool_s))
```

```python
import functools

import jax
import jax.numpy as jnp
from jax import lax
from jax.experimental import pallas as pl
from jax.experimental.pallas import tpu as pltpu

F32 = jnp.float32
BF16 = jnp.bfloat16
HIGHEST = lax.Precision.HIGHEST

EPS = 1e-6
CHUNK = 64
CONV_WIDTH = 4
POOL_WINDOWS = (2, 4, 8, 16)
PAST_LEN = 16384
TOP_K = 2

LANE = 128
SUBLANE = 8
VMEM_LIMIT_BYTES = 56 * 2**20

ROWS_IN_A = 512
ROWS_OUT_FFN = 256
ROWS_STREAM = 512
ROWS_MEM = 256
ROWS_XATTN = 1024
FF_TILE_SAMPLE = 256
MOE_TILE = 1024
MOE_FF_TILE = 512
MOE_ROW_BLOCK = 512
DN_STEP_CHUNKS = 8
INV_BLOCK = 16
POOL_HALO = 32


def _params(*sem):
    return pltpu.CompilerParams(dimension_semantics=sem, vmem_limit_bytes=VMEM_LIMIT_BYTES)


def _resident(shape):
    nd = len(shape)
    return pl.BlockSpec(shape, lambda *_: (0,) * nd, pipeline_mode=pl.Buffered(1))


def _rms(x, w):
    return x * lax.rsqrt(jnp.mean(x * x, axis=-1, keepdims=True) + EPS) * w


def _silu(x):
    half = 0.5 * x
    return half + half * jnp.tanh(half)


def _dot(a, b):
    return jnp.dot(a, b, preferred_element_type=F32)


def _wdot(a, w):
    if w.dtype == F32:
        return jnp.dot(a, w, precision=HIGHEST, preferred_element_type=F32)
    return jnp.dot(a.astype(BF16), w, preferred_element_type=F32)


def _bdot(a, b):
    return lax.dot_general(a, b, (((2,), (1,)), ((0,), (0,))), preferred_element_type=F32)


def _bdot_nt(a, b):
    return lax.dot_general(a, b, (((2,), (2,)), ((0,), (0,))), preferred_element_type=F32)


def _bdot_tn(a, b):
    return lax.dot_general(a, b, (((1,), (1,)), ((0,), (0,))), preferred_element_type=F32)


def _rms_matmul_body(x_ref, g_ref, w_ref, o_ref):
    xn = _rms(x_ref[...], g_ref[...])
    o_ref[...] = _wdot(xn, w_ref[...])


def _wspec(shape, index_map, steps):
    if steps == 1:
        return pl.BlockSpec(shape, index_map, pipeline_mode=pl.Buffered(1))
    return pl.BlockSpec(shape, index_map)


def rms_matmul(x, g, w, tm, name, tn=None):
    m, d = x.shape
    n = w.shape[1]
    tn = n if tn is None else tn
    nj = n // tn
    return pl.pallas_call(
        _rms_matmul_body,
        out_shape=jax.ShapeDtypeStruct((m, n), F32),
        grid=(m // tm, nj),
        in_specs=[pl.BlockSpec((tm, d), lambda i, j: (i, 0)), _resident((1, d)),
                  _wspec((d, tn), lambda i, j: (0, j), nj)],
        out_specs=pl.BlockSpec((tm, tn), lambda i, j: (i, j)),
        compiler_params=_params("parallel", "arbitrary"),
        name=name,
    )(x, g.reshape(1, d), w)


def _out_proj_body(x_ref, a1_ref, a2_ref, w1_ref, w2_ref, *rest, n_main, n_experts, has_into):
    route = n_experts is not None
    if route:
        g_ref, r_ref = rest[0], rest[1]
        o_ref, idx_ref, gate_ref = rest[-3:]
    else:
        o_ref = rest[-1]
    del has_into

    @pl.when(pl.program_id(0) < n_main)
    def _():
        acc = _wdot(a1_ref[...], w1_ref[...]) + _wdot(a2_ref[...], w2_ref[...])
        h = x_ref[...] + acc
        o_ref[...] = h
        if route:
            idx_ref[...], gate_ref[...] = _route_top2(h, g_ref[...], r_ref[...], n_experts)

    @pl.when(pl.program_id(0) >= n_main)
    def _():
        o_ref[...] = jnp.zeros_like(o_ref)
        if route:
            idx_ref[...] = jnp.zeros_like(idx_ref)
            gate_ref[...] = jnp.zeros_like(gate_ref)


def out_proj(x, a1, a2, w, tm, name, out_rows=None, into=None, row_offset=0, route=None):
    m, d = x.shape
    k1, k2 = a1.shape[1], a2.shape[1]
    assert w.shape[0] == k1 + k2 and k1 % k2 == 0 and row_offset % tm == 0
    out_rows = m if out_rows is None else out_rows
    off = row_offset // tm
    n_main = m // tm
    n_extra = 0 if (into is not None or out_rows == m) else 1
    assert out_rows - m <= n_extra * tm or into is not None
    last = n_main - 1
    in_specs = [
        pl.BlockSpec((tm, d), lambda i: (jnp.minimum(i, last), 0)),
        pl.BlockSpec((tm, k1), lambda i: (jnp.minimum(i, last), 0)),
        pl.BlockSpec((tm, k2), lambda i: (jnp.minimum(i, last), 0)),
        pl.BlockSpec((k1, d), lambda i: (0, 0), pipeline_mode=pl.Buffered(1)),
        pl.BlockSpec((k2, d), lambda i: (k1 // k2, 0), pipeline_mode=pl.Buffered(1)),
    ]
    args = [x, a1, a2, w, w]
    out_shape = [jax.ShapeDtypeStruct((out_rows, d), F32)]
    out_specs = [pl.BlockSpec((tm, d), lambda i: (i + off, 0))]
    n_experts = None
    if route is not None:
        g, r_pad, n_experts = route
        in_specs += [_resident((1, d)), _resident((d, LANE))]
        args += [g.reshape(1, d), r_pad]
        n_steps = n_main + n_extra
        out_shape += [jax.ShapeDtypeStruct((n_steps * tm, LANE), jnp.int32),
                      jax.ShapeDtypeStruct((n_steps * tm, LANE), F32)]
        out_specs += [pl.BlockSpec((tm, LANE), lambda i: (i, 0))] * 2
    aliases = {}
    if into is not None:
        in_specs.append(pl.BlockSpec(memory_space=pl.ANY))
        args.append(into)
        aliases = {len(args) - 1: 0}
    outs = pl.pallas_call(
        functools.partial(_out_proj_body, n_main=n_main, n_experts=n_experts, has_into=into is not None),
        out_shape=tuple(out_shape),
        grid=(n_main + n_extra,),
        in_specs=in_specs,
        out_specs=tuple(out_specs),
        input_output_aliases=aliases,
        compiler_params=_params("parallel"),
        name=name,
    )(*args)
    return outs if route is not None else outs[0]


def _ffn_body(h_ref, g_ref, wg_ref, wu_ref, wo_ref, o_ref, xn_ref):
    @pl.when(pl.program_id(1) == 0)
    def _():
        h = h_ref[...]
        xn_ref[...] = _rms(h, g_ref[...])
        o_ref[...] = h

    xn = xn_ref[...]
    a = _wdot(xn, wg_ref[...])
    b = _wdot(xn, wu_ref[...])
    o_ref[...] += _wdot(_silu(a) * b, wo_ref[...])


def ffn_dense(h, g, wi, wo, tm, name, tf=None):
    m, d = h.shape
    f = wo.shape[0]
    tf = f if tf is None else tf
    nj = f // tf
    return pl.pallas_call(
        _ffn_body,
        out_shape=jax.ShapeDtypeStruct((m, d), F32),
        grid=(m // tm, nj),
        in_specs=[
            pl.BlockSpec((tm, d), lambda i, j: (i, 0)),
            _resident((1, d)),
            _wspec((d, tf), lambda i, j: (0, j), nj),
            _wspec((d, tf), lambda i, j: (0, nj + j), nj),
            _wspec((tf, d), lambda i, j: (j, 0), nj),
        ],
        out_specs=pl.BlockSpec((tm, d), lambda i, j: (i, 0)),
        scratch_shapes=[pltpu.VMEM((tm, d), F32)],
        compiler_params=_params("parallel", "arbitrary"),
        name=name,
    )(h, g.reshape(1, d), wi, wi, wo)


def _mem_kv_body(x_ref, g_ref, w_ref, *rest, heads, dh, n_own):
    kv_ref, k_ref, v_ref = rest[-3:]
    xw = heads * dh

    @pl.when(pl.program_id(0) < n_own)
    def _():
        res = _wdot(_rms(x_ref[...], g_ref[...]), w_ref[...])
        kv_ref[...] = res
        for h in range(heads):
            k_ref[:, h, :] = res[:, h * dh:(h + 1) * dh]
            v_ref[:, h, :] = res[:, xw + h * dh:xw + (h + 1) * dh]

    @pl.when(pl.program_id(0) >= n_own)
    def _():
        k_ref[...] = jnp.zeros_like(k_ref)
        v_ref[...] = jnp.zeros_like(v_ref)


def mem_kv(mem, g, w_kv, heads, dh, tm, name, layer, n_layers, into=None):
    m, d = mem.shape
    xw = heads * dh
    n_own = m // tm
    assert (into is None) == (layer == 0)
    n_steps = n_own * n_layers if into is None else n_own
    head_shape = jax.ShapeDtypeStruct((n_layers * m, heads, dh), F32)
    head_spec = pl.BlockSpec((tm, heads, dh), lambda i: (i + layer * n_own, 0, 0))
    own = lambda i: (jnp.minimum(i, n_own - 1), 0)
    in_specs = [pl.BlockSpec((tm, d), own), _resident((1, d)), _resident((d, 2 * xw))]
    args = [mem, g.reshape(1, d), w_kv]
    aliases = {}
    if into is not None:
        in_specs += [pl.BlockSpec(memory_space=pl.ANY)] * 2
        args += list(into)
        aliases = {3: 1, 4: 2}
    return pl.pallas_call(
        functools.partial(_mem_kv_body, heads=heads, dh=dh, n_own=n_own),
        out_shape=(jax.ShapeDtypeStruct((m, 2 * xw), F32), head_shape, head_shape),
        grid=(n_steps,),
        in_specs=in_specs,
        out_specs=(pl.BlockSpec((tm, 2 * xw), own), head_spec, head_spec),
        input_output_aliases=aliases,
        compiler_params=_params("arbitrary"),
        name=name,
    )(*args)


def _out_proj_ffn_body(x_ref, a1_ref, a2_ref, w1_ref, w2_ref, g_ref, wg_ref, wu_ref, wo_ref, o_ref):
    h = x_ref[...] + (_wdot(a1_ref[...], w1_ref[...]) + _wdot(a2_ref[...], w2_ref[...]))
    xn = _rms(h, g_ref[...])
    a = _wdot(xn, wg_ref[...])
    b = _wdot(xn, wu_ref[...])
    o_ref[...] = h + _wdot(_silu(a) * b, wo_ref[...])


def out_proj_ffn(x, a1, a2, w, g, wi, wo, tm, name):
    m, d = x.shape
    k1, k2 = a1.shape[1], a2.shape[1]
    f = wo.shape[0]
    assert w.shape[0] == k1 + k2 and k1 % k2 == 0
    return pl.pallas_call(
        _out_proj_ffn_body,
        out_shape=jax.ShapeDtypeStruct((m, d), F32),
        grid=(m // tm,),
        in_specs=[
            pl.BlockSpec((tm, d), lambda i: (i, 0)),
            pl.BlockSpec((tm, k1), lambda i: (i, 0)),
            pl.BlockSpec((tm, k2), lambda i: (i, 0)),
            pl.BlockSpec((k1, d), lambda i: (0, 0), pipeline_mode=pl.Buffered(1)),
            pl.BlockSpec((k2, d), lambda i: (k1 // k2, 0), pipeline_mode=pl.Buffered(1)),
            _resident((1, d)),
            pl.BlockSpec((d, f), lambda i: (0, 0), pipeline_mode=pl.Buffered(1)),
            pl.BlockSpec((d, f), lambda i: (0, 1), pipeline_mode=pl.Buffered(1)),
            _resident((f, d)),
        ],
        out_specs=pl.BlockSpec((tm, d), lambda i: (i, 0)),
        compiler_params=_params("parallel"),
        name=name,
    )(x, a1, a2, w, w, g.reshape(1, d), wi, wi, wo)


def _pair_blockdiag(y, left):
    yb = y.astype(BF16)
    zero = jnp.zeros_like(yb)
    return jnp.concatenate([jnp.where(left, yb, zero), jnp.where(left, zero, yb)], axis=1)


def _pair_mm(x, y, left):
    return _bdot(x.astype(BF16), _pair_blockdiag(y, left))


def _unit_lower_inverse(mx, eye, blk, left):
    mm = functools.partial(_pair_mm, left=left)
    md = jnp.where(blk, mx, 0.0)
    c = mx - md
    p = eye - md
    m2 = mm(md, md)
    p = p + mm(p, m2)
    m4 = mm(m2, m2)
    p = p + mm(p, m4)
    m8 = mm(m4, m4)
    td = p + mm(p, m8)
    n = mm(td, c)
    n2 = mm(n, n)
    q = eye - n
    q = q + mm(q, n2)
    return mm(q, td)


def _dn_prompt_body(qkv_ref, z_ref, bl_ref, al_ref, cw_ref, alog_ref, dtb_ref, on_ref,
                    dn_ref, sfin_ref, s_ref, xbuf_ref, act_ref, *, rows, heads, dk, dv):
    c = pl.program_id(1)
    wq = heads * dk

    @pl.when(c == 0)
    def _():
        s_ref[...] = jnp.zeros_like(s_ref)
        xbuf_ref[0:SUBLANE, :] = jnp.zeros((SUBLANE, xbuf_ref.shape[1]), F32)

    xbuf_ref[SUBLANE:SUBLANE + rows, :] = qkv_ref[...]
    first = SUBLANE - (CONV_WIDTH - 1)
    y = xbuf_ref[pl.ds(first, rows), :] * cw_ref[0:1, :]
    for j in range(1, CONV_WIDTH):
        y = y + xbuf_ref[pl.ds(first + j, rows), :] * cw_ref[j:j + 1, :]
    act_ref[...] = _silu(y)
    xbuf_ref[0:SUBLANE, :] = xbuf_ref[rows:rows + SUBLANE, :]

    pairs = heads // 2
    ri = lax.broadcasted_iota(jnp.int32, (CHUNK, 2 * CHUNK), 0)
    lane = lax.broadcasted_iota(jnp.int32, (CHUNK, 2 * CHUNK), 1)
    cj = lane % CHUNK
    left = (lane < CHUNK)[None]
    incl = (ri >= cj)[None]
    strict = (ri > cj)[None]
    blk = ((ri // INV_BLOCK) == (cj // INV_BLOCK))[None]
    eye = (ri == cj).astype(F32)[None]
    tr = lax.broadcasted_iota(jnp.int32, (CHUNK, CHUNK), 0)
    tc = lax.broadcasted_iota(jnp.int32, (CHUNK, CHUNK), 1)
    tri = (tr >= tc).astype(F32)

    n_chunks = rows // CHUNK
    zero = jnp.zeros((CHUNK, dk), BF16)

    def chunk_rows(cc):
        return slice(cc * CHUNK, (cc + 1) * CHUNK)

    def per_head(fn):
        return jnp.stack([fn(chunk_rows(cc), h) for cc in range(n_chunks) for h in range(heads)], axis=0)

    def per_pair(fn):
        return jnp.stack([fn(cc, p) for cc in range(n_chunks) for p in range(pairs)], axis=0)

    q3 = per_head(lambda rs, h: act_ref[rs, h * dk:(h + 1) * dk])
    k3 = per_head(lambda rs, h: act_ref[rs, wq + h * dk:wq + (h + 1) * dk])
    v3 = per_head(lambda rs, h: act_ref[rs, 2 * wq + h * dv:2 * wq + (h + 1) * dv])
    qn = q3 * lax.rsqrt(jnp.sum(q3 * q3, axis=-1, keepdims=True) + EPS) * (dk ** -0.5)
    kn = k3 * lax.rsqrt(jnp.sum(k3 * k3, axis=-1, keepdims=True) + EPS)

    beta = jax.nn.sigmoid(bl_ref[...])
    log_decay = -jnp.exp(alog_ref[...]) * jax.nn.softplus(al_ref[...] + dtb_ref[...])
    gcum = [jnp.dot(tri, log_decay[chunk_rows(cc)], precision=HIGHEST, preferred_element_type=F32)
            for cc in range(n_chunks)]
    gcum_t = [jnp.concatenate([gc_, pltpu.roll(gc_, LANE - 1, 1)], axis=0).T for gc_ in gcum]

    def pair_cols(x_of):
        return per_pair(lambda cc, p: jnp.where(left[0], x_of(cc)[:, 2 * p:2 * p + 1],
                                                x_of(cc)[:, 2 * p + 1:2 * p + 2]))

    gc2 = pair_cols(lambda cc: gcum[cc])
    bc2 = pair_cols(lambda cc: beta[chunk_rows(cc)])
    gr2 = per_pair(lambda cc, p: gcum_t[cc][2 * p:2 * p + 1, :])
    decay = jnp.where(incl, jnp.exp(gc2 - gr2), 0.0)

    kb = kn.astype(BF16)
    qb = qn.astype(BF16)

    def pair_of(x, cc, p):
        return x[cc * heads + 2 * p], x[cc * heads + 2 * p + 1]

    k_lhs = per_pair(lambda cc, p: jnp.concatenate(pair_of(kb, cc, p), axis=1))
    q_lhs = per_pair(lambda cc, p: jnp.concatenate(pair_of(qb, cc, p), axis=1))
    k_bd = per_pair(lambda cc, p: jnp.concatenate(
        [jnp.concatenate([pair_of(kb, cc, p)[0], zero], axis=1),
         jnp.concatenate([zero, pair_of(kb, cc, p)[1]], axis=1)], axis=0))
    kk = _bdot_nt(k_lhs, k_bd)
    mx = jnp.where(strict, bc2 * kk * decay, 0.0)
    tinv = _unit_lower_inverse(mx, eye, blk, left)

    gcum_rows = jnp.concatenate(gcum, axis=0)
    gc = per_head(lambda rs, h: gcum_rows[rs, h:h + 1])
    bc = per_head(lambda rs, h: beta[rs, h:h + 1])
    eg = jnp.exp(gc)
    rhs = jnp.concatenate([v3 * bc, kn * (bc * eg)], axis=-1)
    sol = _bdot(_pair_blockdiag(tinv, left), rhs.astype(BF16).reshape(n_chunks * pairs, 2 * CHUNK, dv + dk))
    sol = sol.reshape(n_chunks * heads, CHUNK, dv + dk)
    w_val = sol[..., :dv]
    k_cd = sol[..., dv:].astype(BF16)
    qk_bd = _pair_blockdiag(_bdot_nt(q_lhs, k_bd) * decay, left)
    q_dec = (qn * eg).astype(BF16)
    g_last = gc[:, CHUNK - 1:CHUNK, :]
    k_tail = (kn * jnp.exp(g_last - gc)).astype(BF16)
    c_decay = jnp.exp(g_last)

    for cc in range(n_chunks):
        hs = slice(cc * heads, (cc + 1) * heads)
        ps = slice(cc * pairs, (cc + 1) * pairs)
        rs = chunk_rows(cc)
        s = s_ref[...]
        sb = s.astype(BF16)
        u = w_val[hs] - _bdot(k_cd[hs], sb)
        ub = u.astype(BF16)
        intra = _bdot(qk_bd[ps], ub.reshape(pairs, 2 * CHUNK, dv))
        o = _bdot(q_dec[hs], sb) + intra.reshape(heads, CHUNK, dv)
        s_ref[...] = s * c_decay[hs] + _bdot_tn(k_tail[hs], ub)

        o = o * lax.rsqrt(jnp.mean(o * o, axis=-1, keepdims=True) + EPS) * on_ref[...]
        for h in range(heads):
            z = z_ref[rs, h * dv:(h + 1) * dv]
            dn_ref[rs, h * dv:(h + 1) * dv] = o[h] * _silu(z)

    @pl.when(c == pl.num_programs(1) - 1)
    def _():
        sfin_ref[0] = s_ref[...]


def deltanet_prompt(proj, col, conv_w, a_log, dt_bias, o_norm, batch, seq, heads, dk, dv, rows):
    conv_ch = 2 * heads * dk + heads * dv
    zw = heads * dv
    steps = seq // rows

    def row_map(width_off):
        return lambda b, c: (b * steps + c, width_off)

    body = functools.partial(_dn_prompt_body, rows=rows, heads=heads, dk=dk, dv=dv)
    return pl.pallas_call(
        body,
        out_shape=(jax.ShapeDtypeStruct((batch * seq, zw), F32),
                   jax.ShapeDtypeStruct((batch, heads, dk, dv), F32)),
        grid=(batch, steps),
        in_specs=[
            pl.BlockSpec((rows, conv_ch), row_map(col["qkv"] // conv_ch)),
            pl.BlockSpec((rows, zw), row_map(col["z"] // zw)),
            pl.BlockSpec((rows, LANE), row_map(col["b"] // LANE)),
            pl.BlockSpec((rows, LANE), row_map(col["a"] // LANE)),
            _resident((CONV_WIDTH, conv_ch)),
            _resident((1, LANE)),
            _resident((1, LANE)),
            _resident((1, dv)),
        ],
        out_specs=(pl.BlockSpec((rows, zw), lambda b, c: (b * steps + c, 0)),
                   pl.BlockSpec((1, heads, dk, dv), lambda b, c: (b, 0, 0, 0))),
        scratch_shapes=[
            pltpu.VMEM((heads, dk, dv), F32),
            pltpu.VMEM((rows + SUBLANE, conv_ch), F32),
            pltpu.VMEM((rows, conv_ch), F32),
        ],
        compiler_params=_params("parallel", "arbitrary"),
        name="deltanet_prompt",
    )(proj, proj, proj, proj, conv_w, a_log, dt_bias, o_norm)


def _dn_sample_body(qkv_ref, z_ref, bl_ref, al_ref, cbuf_ref, st_ref, cw_ref, alog_ref, dtb_ref, on_ref,
                    dn_ref, sto_ref, cbo_ref, act_ref, *, nb, heads, dk, dv):
    wq = heads * dk
    new = qkv_ref[...]
    y = cbuf_ref[:, 0, :] * cw_ref[0:1, :]
    for j in range(1, CONV_WIDTH - 1):
        y = y + cbuf_ref[:, j, :] * cw_ref[j:j + 1, :]
    y = y + new * cw_ref[CONV_WIDTH - 1:CONV_WIDTH, :]
    act_ref[...] = _silu(y)
    for j in range(CONV_WIDTH - 2):
        cbo_ref[:, j, :] = cbuf_ref[:, j + 1, :]
    cbo_ref[:, CONV_WIDTH - 2, :] = new

    beta = jax.nn.sigmoid(bl_ref[...])
    g = -jnp.exp(alog_ref[...]) * jax.nn.softplus(al_ref[...] + dtb_ref[...])
    dec = jnp.exp(g)

    q_t, k_t = [], []
    for h in range(heads):
        q = act_ref[:, h * dk:(h + 1) * dk]
        k = act_ref[:, wq + h * dk:wq + (h + 1) * dk]
        qn = q * lax.rsqrt(jnp.sum(q * q, axis=-1, keepdims=True) + EPS) * (dk ** -0.5)
        kn = k * lax.rsqrt(jnp.sum(k * k, axis=-1, keepdims=True) + EPS)
        q_t.append(qn.T)
        k_t.append(kn.T)

    def per_head(fn):
        return jnp.stack([fn(h) for h in range(heads)], axis=0)

    for i in range(nb):
        row = slice(i, i + 1)
        kcol = per_head(lambda h: k_t[h][:, i:i + 1])
        qcol = per_head(lambda h: q_t[h][:, i:i + 1])
        v = per_head(lambda h: act_ref[row, 2 * wq + h * dv:2 * wq + (h + 1) * dv])
        z = per_head(lambda h: z_ref[row, h * dv:(h + 1) * dv])
        b_i = per_head(lambda h: beta[row, h:h + 1])
        d_i = per_head(lambda h: dec[row, h:h + 1])
        s = st_ref[i] * d_i
        u = b_i * (v - jnp.sum(kcol * s, axis=1, keepdims=True))
        s = s + kcol * u
        o = jnp.sum(qcol * s, axis=1, keepdims=True)
        sto_ref[i] = s
        o = o * lax.rsqrt(jnp.mean(o * o, axis=-1, keepdims=True) + EPS) * on_ref[...]
        o = o * _silu(z)
        for h in range(heads):
            dn_ref[row, h * dv:(h + 1) * dv] = o[h]


def deltanet_sample(proj, col, conv_buf, state, conv_w, a_log, dt_bias, o_norm, heads, dk, dv, nb):
    n_seq = proj.shape[0]
    conv_ch = 2 * heads * dk + heads * dv
    zw = heads * dv
    body = functools.partial(_dn_sample_body, nb=nb, heads=heads, dk=dk, dv=dv)
    return pl.pallas_call(
        body,
        out_shape=(jax.ShapeDtypeStruct((n_seq, zw), F32),
                   jax.ShapeDtypeStruct(state.shape, F32),
                   jax.ShapeDtypeStruct(conv_buf.shape, F32)),
        grid=(n_seq // nb,),
        in_specs=[
            pl.BlockSpec((nb, conv_ch), lambda i: (i, col["qkv"] // conv_ch)),
            pl.BlockSpec((nb, zw), lambda i: (i, col["z"] // zw)),
            pl.BlockSpec((nb, LANE), lambda i: (i, col["b"] // LANE)),
            pl.BlockSpec((nb, LANE), lambda i: (i, col["a"] // LANE)),
            pl.BlockSpec((nb, CONV_WIDTH - 1, conv_ch), lambda i: (i, 0, 0)),
            pl.BlockSpec((nb, heads, dk, dv), lambda i: (i, 0, 0, 0)),
            _resident((CONV_WIDTH, conv_ch)),
            _resident((1, LANE)),
            _resident((1, LANE)),
            _resident((1, dv)),
        ],
        out_specs=(pl.BlockSpec((nb, zw), lambda i: (i, 0)),
                   pl.BlockSpec((nb, heads, dk, dv), lambda i: (i, 0, 0, 0)),
                   pl.BlockSpec((nb, CONV_WIDTH - 1, conv_ch), lambda i: (i, 0, 0))),
        scratch_shapes=[
            pltpu.VMEM((nb, conv_ch), F32),
        ],
        compiler_params=_params("parallel"),
        name="deltanet_sample",
    )(proj, proj, proj, proj, conv_buf, state, conv_w, a_log, dt_bias, o_norm)


def _xattn_prompt_body(q_ref, k_ref, v_ref, o_ref, *, heads, dh):
    for h in range(heads):
        sl = slice(h * dh, (h + 1) * dh)
        q = q_ref[:, sl].astype(BF16)
        k = k_ref[:, sl].astype(BF16)
        v = v_ref[:, sl].astype(BF16)
        s = lax.dot_general(q, k, (((1,), (1,)), ((), ())), preferred_element_type=F32) * (dh ** -0.5)
        e = jnp.exp(s - jnp.max(s, axis=-1, keepdims=True))
        p = e / jnp.sum(e, axis=-1, keepdims=True)
        o_ref[:, sl] = _dot(p.astype(BF16), v)


def xattn_prompt(proj, cq_col, kv, batch, seq, n_mem, heads, dh, tq):
    xw = heads * dh
    steps = seq // tq
    body = functools.partial(_xattn_prompt_body, heads=heads, dh=dh)
    return pl.pallas_call(
        body,
        out_shape=jax.ShapeDtypeStruct((batch * seq, xw), F32),
        grid=(batch, steps),
        in_specs=[
            pl.BlockSpec((tq, xw), lambda b, i: (b * steps + i, cq_col // xw)),
            pl.BlockSpec((n_mem, xw), lambda b, i: (b, 0)),
            pl.BlockSpec((n_mem, xw), lambda b, i: (b, 1)),
        ],
        out_specs=pl.BlockSpec((tq, xw), lambda b, i: (b * steps + i, 0)),
        compiler_params=_params("parallel", "parallel"),
        name="xattn_prompt",
    )(proj, kv, kv)


def _xattn_sample_body(q_ref, k_ref, v_ref, o_ref, *, nb, heads, dh):
    rep = SUBLANE // heads
    n_t = k_ref.shape[2] // SUBLANE

    def over_groups(x, op):
        out = x
        for t in range(1, rep):
            out = op(out, pltpu.roll(x, t * heads, 0))
        return out

    for i in range(nb):
        row = slice(i, i + 1)
        q8 = jnp.concatenate([q_ref[row, h * dh:(h + 1) * dh] for h in range(heads)] * rep, axis=0)
        k3 = k_ref[0, i].reshape(n_t, SUBLANE, dh)
        v3 = v_ref[0, i].reshape(n_t, SUBLANE, dh)
        s = jnp.sum(k3 * q8[None], axis=-1, keepdims=True) * (dh ** -0.5)
        m = over_groups(jnp.max(s, axis=0), jnp.maximum)
        e = jnp.exp(s - m[None])
        den = over_groups(jnp.sum(e, axis=0), jnp.add)
        p = e / den[None]
        acc = over_groups(jnp.sum(p * v3, axis=0), jnp.add)
        for h in range(heads):
            o_ref[row, h * dh:(h + 1) * dh] = acc[h:h + 1, :]


def xattn_sample(proj, cq_col, cache_k, cache_v, layer, nb):
    n_layers, n_seq, n_mem, heads, dh = cache_k.shape
    assert SUBLANE % heads == 0 and (n_mem * heads) % SUBLANE == 0
    xw = heads * dh
    body = functools.partial(_xattn_sample_body, nb=nb, heads=heads, dh=dh)
    cache_spec = pl.BlockSpec((1, nb, n_mem * heads, dh), lambda i: (layer, i, 0, 0))
    flat = (n_layers, n_seq, n_mem * heads, dh)
    return pl.pallas_call(
        body,
        out_shape=jax.ShapeDtypeStruct((n_seq, xw), F32),
        grid=(n_seq // nb,),
        in_specs=[pl.BlockSpec((nb, xw), lambda i: (i, cq_col // xw)), cache_spec, cache_spec],
        out_specs=pl.BlockSpec((nb, xw), lambda i: (i, 0)),
        compiler_params=_params("parallel"),
        name="xattn_sample",
    )(proj, cache_k.reshape(flat), cache_v.reshape(flat))


def _pool_mix(sums, cnts, u_of, pw_ref, ps_ref, o_ref, gw):
    for gi in range(len(POOL_WINDOWS)):
        cols = slice(gi * gw, (gi + 1) * gw)
        d = sums[gi] / cnts[gi] - u_of(cols)
        o_ref[:, cols] = _wdot(d, pw_ref[gi]) * ps_ref[:, cols]


def _pool_prompt_body(x_ref, g_ref, w_ref, pw_ref, ps_ref, o_ref, cq_ref, tail_ref, xbuf, s2, s4, s8, *, tt, gw):
    t = pl.program_id(1)
    n = POOL_HALO + tt
    pw_total = xbuf.shape[1]

    @pl.when(t == 0)
    def _():
        xbuf[0:POOL_HALO, :] = jnp.zeros((POOL_HALO, pw_total), F32)

    proj = _wdot(_rms(x_ref[...], g_ref[...]), w_ref[...])
    cq_ref[...] = proj[:, pw_total:]
    xbuf[POOL_HALO:n, :] = proj[:, 0:pw_total]
    s2[8:n, :] = xbuf[8:n, :] + xbuf[7:n - 1, :]
    s4[16:n, :] = s2[16:n, gw:] + s2[14:n - 2, gw:]
    s8[24:n, :] = s4[24:n, gw:] + s4[20:n - 4, gw:]
    s16 = s8[32:n, gw:] + s8[24:n - 8, gw:]
    sums = [s2[POOL_HALO:n, 0:gw], s4[POOL_HALO:n, 0:gw], s8[POOL_HALO:n, 0:gw], s16]
    pos = t * tt + lax.broadcasted_iota(jnp.int32, (tt, 1), 0)
    cnts = [jnp.minimum(pos + 1, w).astype(F32) for w in POOL_WINDOWS]
    _pool_mix(sums, cnts, lambda cols: xbuf[POOL_HALO:n, cols], pw_ref, ps_ref, o_ref, gw)
    tail_ref[0] = xbuf[tt:n, :]
    xbuf[0:POOL_HALO, :] = xbuf[tt:n, :]


def proj_pool_prompt(x, g, w, pool_w, pool_scale, batch, seq, tt):
    m, d = x.shape
    n_out = w.shape[1]
    pw_total = pool_scale.shape[-1]
    gw = pw_total // len(POOL_WINDOWS)
    steps = seq // tt
    n = POOL_HALO + tt
    body = functools.partial(_pool_prompt_body, tt=tt, gw=gw)
    return pl.pallas_call(
        body,
        out_shape=(jax.ShapeDtypeStruct((m, pw_total), F32),
                   jax.ShapeDtypeStruct((m, n_out - pw_total), F32),
                   jax.ShapeDtypeStruct((batch, POOL_HALO, pw_total), F32)),
        grid=(batch, steps),
        in_specs=[
            pl.BlockSpec((tt, d), lambda b, t: (b * steps + t, 0)),
            _resident((1, d)),
            _resident((d, n_out)),
            _resident(pool_w.shape),
            _resident((1, pw_total)),
        ],
        out_specs=(pl.BlockSpec((tt, pw_total), lambda b, t: (b * steps + t, 0)),
                   pl.BlockSpec((tt, n_out - pw_total), lambda b, t: (b * steps + t, 0)),
                   pl.BlockSpec((1, POOL_HALO, pw_total), lambda b, t: (b, 0, 0))),
        scratch_shapes=[
            pltpu.VMEM((n, pw_total), F32),
            pltpu.VMEM((n, pw_total), F32),
            pltpu.VMEM((n, pw_total - gw), F32),
            pltpu.VMEM((n, pw_total - 2 * gw), F32),
        ],
        compiler_params=_params("parallel", "arbitrary"),
        name="proj_pool_prompt",
    )(x, g.reshape(1, d), w, pool_w, pool_scale)


def _pool_sample_body(u_ref, st_ref, pw_ref, ps_ref, o_ref, sto_ref, *, gw, buf):
    pw_total = gw * len(POOL_WINDOWS)
    new = u_ref[:, 0:pw_total]
    sums, cnts = [], []
    for gi, w in enumerate(POOL_WINDOWS):
        cols = slice(gi * gw, (gi + 1) * gw)
        s = new[:, cols]
        for i in range(1, w):
            s = s + st_ref[:, buf - i, cols]
        sums.append(s)
        cnts.append(float(min(PAST_LEN + 1, w)))
    _pool_mix(sums, cnts, lambda cols: new[:, cols], pw_ref, ps_ref, o_ref, gw)
    sto_ref[:, 0:buf - 1, :] = st_ref[:, 1:buf, :]
    sto_ref[:, buf - 1, :] = new


def pool_sample(proj, state, pool_w, pool_scale):
    n_seq, buf, pw_total = state.shape
    gw = pw_total // len(POOL_WINDOWS)
    body = functools.partial(_pool_sample_body, gw=gw, buf=buf)
    return pl.pallas_call(
        body,
        out_shape=(jax.ShapeDtypeStruct((n_seq, pw_total), F32),
                   jax.ShapeDtypeStruct(state.shape, F32)),
        compiler_params=pltpu.CompilerParams(vmem_limit_bytes=VMEM_LIMIT_BYTES),
        name="pool_sample",
    )(proj, state, pool_w, pool_scale)


def _route_top2(h, g, r, n_experts):
    logits = _wdot(_rms(h, g), r)
    lane = lax.broadcasted_iota(jnp.int32, logits.shape, 1)
    lm = jnp.where(lane < n_experts, logits, -jnp.inf)
    m1 = jnp.max(lm, axis=1, keepdims=True)
    i1 = jnp.min(jnp.where(lm == m1, lane, LANE), axis=1, keepdims=True)
    lm2 = jnp.where(lane == i1, -jnp.inf, lm)
    m2 = jnp.max(lm2, axis=1, keepdims=True)
    i2 = jnp.min(jnp.where(lm2 == m2, lane, LANE), axis=1, keepdims=True)
    e2 = jnp.exp(m2 - m1)
    den = 1.0 + e2
    idx = jnp.where(lane == 0, i1, jnp.where(lane == 1, i2, 0))
    gate = jnp.where(lane == 0, 1.0 / den, jnp.where(lane == 1, e2 / den, 0.0))
    return idx, gate


def _dispatch_body(dest_ref, h_ref, gate_ref, zeros_hbm, o_hbm, ext, sem, *, tb, n):
    del zeros_hbm
    i = pl.program_id(0)
    slot = i % 2
    groups = tb // SUBLANE
    d = h_ref.shape[1]

    def row_copy(s, t, u, row):
        return pltpu.make_async_copy(ext.at[s, t, pl.ds(u, 1)], o_hbm.at[pl.ds(row, 1)], sem.at[s])

    def wait_slot(s):
        def body(t, carry):
            pltpu.make_async_copy(ext.at[s, t], o_hbm.at[pl.ds(0, SUBLANE)], sem.at[s]).wait()
            return carry
        lax.fori_loop(0, groups, body, 0)

    @pl.when(i >= 2)
    def _():
        wait_slot(slot)

    ext[slot, :, :, 0:d] = h_ref[...].reshape(groups, SUBLANE, d)
    ext[slot, :, :, d:d + LANE] = gate_ref[...].reshape(groups, SUBLANE, LANE)

    def body(t, carry):
        for u in range(SUBLANE):
            row_copy(slot, t, u, dest_ref[0, 0, t * SUBLANE + u]).start(priority=u % 2)
        return carry
    lax.fori_loop(0, groups, body, 0)

    @pl.when(i == n - 1)
    def _():
        wait_slot(slot)
        if n >= 2:
            wait_slot(1 - slot)


def moe_dispatch(h_all, row_offset, rows, gates, dest, into, tb, name):
    d = h_all.shape[1]
    off = row_offset // tb
    n = rows // tb
    body = functools.partial(_dispatch_body, tb=tb, n=n)
    return pl.pallas_call(
        body,
        out_shape=jax.ShapeDtypeStruct(into.shape, F32),
        grid=(n,),
        in_specs=[
            pl.BlockSpec((1, 1, tb), lambda i: (i, 0, 0), memory_space=pltpu.SMEM),
            pl.BlockSpec((tb, d), lambda i: (i + off, 0)),
            pl.BlockSpec((tb, LANE), lambda i: (i, 0)),
            pl.BlockSpec(memory_space=pl.ANY),
        ],
        out_specs=pl.BlockSpec(memory_space=pl.ANY),
        scratch_shapes=[pltpu.VMEM((2, tb // SUBLANE, SUBLANE, d + LANE), F32), pltpu.SemaphoreType.DMA((2,))],
        input_output_aliases={3: 0},
        compiler_params=_params("arbitrary"),
        name=name,
    )(dest.reshape(n, 1, tb), h_all, gates, into)


def _experts_body(te_ref, nu_ref, nb_ref, fill_ref, g_first_ref, g_next_ref, s_cur_ref,
                  h_hbm, g_ref, wg_ref, wu_ref, wo_ref, y_hbm,
                  xbuf, acc, xn_ref, gsem, ssem, *, tm, block, n_rows):
    i = pl.program_id(0)
    j = pl.program_id(1)
    n_tiles = pl.num_programs(0)
    nj = pl.num_programs(1)
    n_used = nu_ref[0]
    slot = i % 2
    valid = i < n_used
    chunks = tm // SUBLANE
    d = xn_ref.shape[1]

    def gather(idx_ref, s):
        def body(t, carry):
            row0 = pl.multiple_of(idx_ref[0, 0, t], SUBLANE)
            pltpu.make_async_copy(h_hbm.at[pl.ds(row0, SUBLANE), pl.ds(0, d)], xbuf.at[s, t], gsem.at[s]).start()
            return carry
        lax.fori_loop(0, chunks, body, 0)

    def wait_gather(s):
        def body(t, carry):
            pltpu.make_async_copy(h_hbm.at[pl.ds(0, SUBLANE), pl.ds(0, d)], xbuf.at[s, t], gsem.at[s]).wait()
            return carry
        lax.fori_loop(0, chunks, body, 0)

    def scatter():
        def body(t, carry):
            row0 = pl.multiple_of(s_cur_ref[0, 0, t], SUBLANE)
            pltpu.make_async_copy(acc.at[t], y_hbm.at[pl.ds(row0, SUBLANE)], ssem.at[0]).start()
            return carry
        lax.fori_loop(0, chunks, body, 0)

    def wait_scatter():
        def body(t, carry):
            pltpu.make_async_copy(acc.at[t], y_hbm.at[pl.ds(0, SUBLANE)], ssem.at[0]).wait()
            return carry
        lax.fori_loop(0, chunks, body, 0)

    @pl.when(j == 0)
    def _():
        @pl.when(i == 0)
        def _():
            gather(g_first_ref, 0)
            acc[...] = jnp.zeros_like(acc)
            first_free = nb_ref[0] * (block // SUBLANE)
            n_free = n_rows // SUBLANE - first_free

            def zero_copy(row0):
                return pltpu.make_async_copy(acc.at[0], y_hbm.at[pl.ds(pl.multiple_of(row0, SUBLANE), SUBLANE)],
                                             ssem.at[0])

            def fill(c, carry):
                zero_copy((first_free + c) * SUBLANE).start()
                zero_copy(n_rows + (first_free + c) * SUBLANE).start()
                return carry
            lax.fori_loop(0, n_free, fill, 0)

            def fill_spare(c, carry):
                zero_copy(2 * n_rows + c * SUBLANE).start()
                return carry
            lax.fori_loop(0, chunks, fill_spare, 0)

            def drain(c, carry):
                zero_copy(0).wait()
                return carry
            lax.fori_loop(0, 2 * n_free + chunks, drain, 0)

        @pl.when(i + 1 < n_used)
        def _():
            gather(g_next_ref, 1 - slot)

        @pl.when(valid)
        def _():
            wait_gather(slot)
            xn_ref[...] = _rms(xbuf[slot].reshape(tm, d), g_ref[...]).astype(BF16)

        @pl.when((i >= 1) & (i <= n_used))
        def _():
            wait_scatter()

        @pl.when(valid)
        def _():
            acc[...] = jnp.zeros_like(acc)

    def ff_step(n_chunks):
        rows = n_chunks * SUBLANE
        xn = xn_ref[0:rows, :]
        a = _dot(xn, wg_ref[0].astype(BF16))
        b = _dot(xn, wu_ref[0].astype(BF16))
        part = _dot((_silu(a) * b).astype(BF16), wo_ref[0].astype(BF16)).reshape(n_chunks, SUBLANE, d)
        acc[0:n_chunks] += part

    fill = fill_ref[i]

    @pl.when(valid & (fill > chunks // 2))
    def _():
        ff_step(chunks)

    @pl.when(valid & (fill > chunks // 4) & (fill <= chunks // 2))
    def _():
        ff_step(chunks // 2)

    @pl.when(valid & (fill <= chunks // 4))
    def _():
        ff_step(chunks // 4)

    @pl.when(valid & (j == nj - 1))
    def _():
        scatter()

        @pl.when(i == n_tiles - 1)
        def _():
            wait_scatter()


def moe_experts(h_ps, g, wi, wo, gather_rows, scatter_rows, tile_expert, n_used, n_blocks_used, tile_fill,
                block, n_rows, tm, tf):
    d = wo.shape[2]
    f = wo.shape[1]
    nj = f // tf
    n_tiles = gather_rows.shape[0]
    chunks = tm // SUBLANE

    def wblock(col_off):
        def index_map(i, j, te, nu, nb, tf_):
            jj = jnp.where(i < nu[0], j, nj - 1)
            return (te[i], 0, col_off + jj)
        return index_map

    def woblock(i, j, te, nu, nb, tf_):
        return (te[i], jnp.where(i < nu[0], j, nj - 1), 0)

    def smem_tile(index_map):
        return pl.BlockSpec((1, 1, chunks), index_map, memory_space=pltpu.SMEM)

    grid_spec = pltpu.PrefetchScalarGridSpec(
        num_scalar_prefetch=4,
        grid=(n_tiles, nj),
        in_specs=[
            smem_tile(lambda i, j, te, nu, nb, tf_: (0, 0, 0)),
            smem_tile(lambda i, j, te, nu, nb, tf_: (jnp.minimum(i + 1, n_tiles - 1), 0, 0)),
            smem_tile(lambda i, j, te, nu, nb, tf_: (i, 0, 0)),
            pl.BlockSpec(memory_space=pl.ANY),
            pl.BlockSpec((1, d), lambda i, j, te, nu, nb, tf_: (0, 0)),
            pl.BlockSpec((1, d, tf), wblock(0)),
            pl.BlockSpec((1, d, tf), wblock(nj)),
            pl.BlockSpec((1, tf, d), woblock),
        ],
        out_specs=pl.BlockSpec(memory_space=pl.ANY),
        scratch_shapes=[
            pltpu.VMEM((2, chunks, SUBLANE, d), F32),
            pltpu.VMEM((chunks, SUBLANE, d), F32),
            pltpu.VMEM((tm, d), BF16),
            pltpu.SemaphoreType.DMA((2,)),
            pltpu.SemaphoreType.DMA((1,)),
        ],
    )
    return pl.pallas_call(
        functools.partial(_experts_body, tm=tm, block=block, n_rows=n_rows),
        out_shape=jax.ShapeDtypeStruct((2 * n_rows + tm, d), F32),
        grid_spec=grid_spec,
        compiler_params=_params("arbitrary", "arbitrary"),
        name="moe_experts",
    )(tile_expert, n_used, n_blocks_used, tile_fill, gather_rows, gather_rows, scatter_rows, h_ps,
      g.reshape(1, d), wi, wi, wo)


def _combine_body(nb_ref, hx_ref, y1_ref, y2_ref, fn_ref, o_ref, *, d, final_norm):
    @pl.when(pl.program_id(0) < nb_ref[0])
    def _():
        hx = hx_ref[...]
        out = hx[:, 0:d] + (hx[:, d:d + 1] * y1_ref[...] + hx[:, d + 1:d + 2] * y2_ref[...])
        if final_norm:
            out = _rms(out, fn_ref[...])
        o_ref[...] = out

    @pl.when(pl.program_id(0) >= nb_ref[0])
    def _():
        o_ref[...] = jnp.zeros_like(o_ref)


def moe_combine(h_ps, y, n_blocks_used, choice_stride, fn, tb, final_norm, name):
    rows, dx = h_ps.shape
    d = y.shape[1]
    off2 = choice_stride // tb

    def used(i, nb):
        return jnp.minimum(i, nb[0] - 1)

    grid_spec = pltpu.PrefetchScalarGridSpec(
        num_scalar_prefetch=1,
        grid=(rows // tb,),
        in_specs=[
            pl.BlockSpec((tb, dx), lambda i, nb: (used(i, nb), 0)),
            pl.BlockSpec((tb, d), lambda i, nb: (used(i, nb), 0)),
            pl.BlockSpec((tb, d), lambda i, nb: (used(i, nb) + off2, 0)),
            pl.BlockSpec((1, d), lambda i, nb: (0, 0)),
        ],
        out_specs=pl.BlockSpec((tb, d), lambda i, nb: (i, 0)),
    )
    return pl.pallas_call(
        functools.partial(_combine_body, d=d, final_norm=final_norm),
        out_shape=jax.ShapeDtypeStruct((rows, d), F32),
        grid_spec=grid_spec,
        compiler_params=_params("arbitrary"),
        name=name,
    )(n_blocks_used, h_ps, y, y, fn.reshape(1, d))


def _unpermute_body(cur_ref, nxt_ref, y_hbm, o_ref, buf, sem, *, tb):
    i = pl.program_id(0)
    n = pl.num_programs(0)
    slot = i % 2
    groups = tb // SUBLANE

    def start(idx_ref, s):
        def body(t, carry):
            for u in range(SUBLANE):
                row = idx_ref[0, 0, t * SUBLANE + u]
                pltpu.make_async_copy(y_hbm.at[pl.ds(row, 1)], buf.at[s, t, pl.ds(u, 1)], sem.at[s]).start(
                    priority=u % 2)
            return carry
        lax.fori_loop(0, groups, body, 0)

    @pl.when(i == 0)
    def _():
        start(cur_ref, 0)

    @pl.when(i + 1 < n)
    def _():
        start(nxt_ref, 1 - slot)

    def wait_body(t, carry):
        pltpu.make_async_copy(y_hbm.at[pl.ds(0, SUBLANE)], buf.at[slot, t], sem.at[slot]).wait()
        return carry
    lax.fori_loop(0, groups, wait_body, 0)
    o_ref[...] = buf[slot].reshape(o_ref.shape)


def moe_unpermute(y_ps, dest, tb, name):
    rows = dest.shape[0]
    d = y_ps.shape[1]
    n = rows // tb
    idx = dest.reshape(n, 1, tb)

    def smem_tile(index_map):
        return pl.BlockSpec((1, 1, tb), index_map, memory_space=pltpu.SMEM)

    return pl.pallas_call(
        functools.partial(_unpermute_body, tb=tb),
        out_shape=jax.ShapeDtypeStruct((rows, d), F32),
        grid=(n,),
        in_specs=[smem_tile(lambda i: (i, 0, 0)), smem_tile(lambda i: (jnp.minimum(i + 1, n - 1), 0, 0)),
                  pl.BlockSpec(memory_space=pl.ANY)],
        out_specs=pl.BlockSpec((tb, d), lambda i: (i, 0)),
        scratch_shapes=[pltpu.VMEM((2, tb // SUBLANE, SUBLANE, d), F32), pltpu.SemaphoreType.DMA((2,))],
        compiler_params=_params("arbitrary"),
        name=name,
    )(idx, idx, y_ps)


def moe_plan(idx, n_experts, tm, block):
    n_tok, k = idx.shape
    assert k == 2
    n_pairs = n_experts * n_experts
    chunks_per_tile = tm // SUBLANE
    n_rows = -(-(n_tok + n_pairs * (SUBLANE - 1)) // block) * block
    n_chunks = n_rows // SUBLANE

    pair = idx[:, 0] * n_experts + idx[:, 1]
    onehot = (pair[:, None] == jnp.arange(n_pairs, dtype=jnp.int32)[None, :]).astype(jnp.int32)
    csum = jnp.cumsum(onehot, axis=0)
    counts = csum[-1]
    rank = jnp.sum(csum * onehot, axis=1) - 1
    padded = (counts + SUBLANE - 1) // SUBLANE * SUBLANE
    total = jnp.sum(padded)
    last_group = jnp.max(jnp.where(counts > 0, jnp.arange(n_pairs, dtype=jnp.int32), 0))
    padded = padded + jnp.where(jnp.arange(n_pairs) == last_group, (-total) % block, 0)
    group_end = jnp.cumsum(padded)
    group_start = group_end - padded
    dest = jnp.sum(onehot * group_start[None, :], axis=1) + rank
    rows_used = group_end[-1]

    chunk_row = jnp.arange(n_chunks, dtype=jnp.int32) * SUBLANE
    chunk_group = jnp.sum((chunk_row[:, None] >= group_end[None, :]).astype(jnp.int32), axis=1)
    chunk_used = chunk_row < rows_used
    chunk_group = jnp.minimum(chunk_group, n_pairs - 1)
    chunk_experts = jnp.stack([chunk_group // n_experts, chunk_group % n_experts], axis=1)
    experts = jnp.arange(n_experts, dtype=jnp.int32)
    member = ((chunk_experts[:, :, None] == experts[None, None, :]) & chunk_used[:, None, None])
    member = member.astype(jnp.int32)
    member_e = jnp.sum(member, axis=1)
    csum_e = jnp.cumsum(member_e, axis=0)
    counts_e = csum_e[-1]
    tiles_e = (counts_e + chunks_per_tile - 1) // chunks_per_tile
    tile_end = jnp.cumsum(tiles_e)
    start_e = (tile_end - tiles_e) * chunks_per_tile
    pos = jnp.sum(member * (start_e[None, None, :] + csum_e[:, None, :] - 1), axis=2)
    n_tiles = 2 * n_chunks // chunks_per_tile + n_experts
    n_pos = n_tiles * chunks_per_tile
    pos = jnp.where(chunk_used[:, None], pos, n_pos)
    entry = 2 * jnp.arange(n_chunks, dtype=jnp.int32)[:, None] + jnp.arange(2, dtype=jnp.int32)[None, :] + 1
    table = jnp.zeros((n_pos,), jnp.int32).at[pos.reshape(-1)].set(entry.reshape(-1), mode="drop")
    table = table.reshape(n_tiles, chunks_per_tile)
    filled = table > 0
    src_chunk = (table - 1) // 2
    choice = (table - 1) % 2
    spare = 2 * n_rows + jnp.arange(chunks_per_tile, dtype=jnp.int32) * SUBLANE
    gather_rows = jnp.where(filled, src_chunk * SUBLANE, 0)
    scatter_rows = jnp.where(filled, choice * n_rows + src_chunk * SUBLANE, spare[None, :])

    n_used = tile_end[-1]
    tile_ids = jnp.arange(n_tiles, dtype=jnp.int32)
    tile_expert = jnp.sum((tile_ids[:, None] >= tile_end[None, :]).astype(jnp.int32), axis=1)
    last_expert = jnp.sum((n_used - 1 >= tile_end).astype(jnp.int32))
    tile_expert = jnp.where(tile_ids < n_used, tile_expert, last_expert).astype(jnp.int32)
    return (dest.astype(jnp.int32),
            gather_rows.reshape(n_tiles, 1, chunks_per_tile).astype(jnp.int32),
            scatter_rows.reshape(n_tiles, 1, chunks_per_tile).astype(jnp.int32),
            tile_expert, n_used.reshape(1).astype(jnp.int32),
            (rows_used // block).reshape(1).astype(jnp.int32),
            jnp.sum(filled.astype(jnp.int32), axis=1), n_rows)


def _pad_lanes(v):
    return jnp.zeros((1, LANE), F32).at[0, :v.shape[0]].set(v)


def kernel(x_prompt, x_sample, state_dn, state_dn_conv, state_pool, cache_mem_k, cache_mem_v, mem_prompt, a_norm1, a_w_in, a_conv_w, a_a_log, a_dt_bias, a_o_norm, a_w_out, a_norm2, a_ffn_wi, a_ffn_wo, b_norm1, b_w_in, b_pool_w, b_pool_scale, b_w_out, b_norm2, b_router, b_moe_wi, b_moe_wo, m_norm, m_w_k, m_w_v, final_norm):
    bp, seq, d = x_prompt.shape
    bs = x_sample.shape[0]
    assert x_sample.shape[1] == 1
    depth = m_norm.shape[0]
    assert depth % 2 == 0, "the output norm is fused into the expert block of the last (odd) layer"
    heads, dk, dv = state_dn.shape[2], state_dn.shape[3], state_dn.shape[4]
    conv_ch = state_dn_conv.shape[-1]
    qk_w, v_w = heads * dk, heads * dv
    assert conv_ch == 2 * qk_w + v_w and dk == LANE and dv == LANE and heads <= LANE
    n_mem, x_heads, x_dh = cache_mem_k.shape[2], cache_mem_k.shape[3], cache_mem_k.shape[4]
    xw = x_heads * x_dh
    pool_w_total = state_pool.shape[-1]
    pool_buf = state_pool.shape[2]

    hp = x_prompt.reshape(bp * seq, d)
    hs = x_sample.reshape(bs, d)

    o1, o2, o3, o4 = conv_ch, conv_ch + v_w, conv_ch + v_w + heads, conv_ch + v_w + 2 * heads
    col_a = {"qkv": 0, "z": conv_ch, "cq": conv_ch + v_w, "b": conv_ch + v_w + xw, "a": conv_ch + v_w + xw + LANE}
    n_a = col_a["a"] + LANE

    dn_p, conv_p, pool_p = [], [], []
    mk_all = mv_all = None
    dn_s, conv_s, pool_s = [], [], []
    for i in range(depth):
        j = i // 2
        w_kv = jnp.concatenate([m_w_k[i], m_w_v[i]], axis=1).astype(BF16)
        kv, mk_all, mv_all = mem_kv(mem_prompt.reshape(bp * n_mem, d), m_norm[i], w_kv, x_heads, x_dh, ROWS_MEM,
                                    f"mem_kv_{i}", i, depth, into=None if i == 0 else (mk_all, mv_all))
        if i % 2 == 0:
            w = a_w_in[j]
            def regroup(t):
                rows = t.shape[0]
                return jnp.concatenate(
                    [t[:, :o2], t[:, o4:], t[:, o2:o3], jnp.zeros((rows, LANE - heads), t.dtype),
                     t[:, o3:o4], jnp.zeros((rows, n_a - col_a["a"] - heads), t.dtype)], axis=1)

            assert col_a["b"] == o2 + xw
            a_log, dt_b = _pad_lanes(a_a_log[j]), _pad_lanes(a_dt_bias[j])
            o_norm = a_o_norm[j].reshape(1, dv)

            proj_p = rms_matmul(hp, a_norm1[j], regroup(w.astype(BF16)), ROWS_IN_A, f"a_in_p_{i}")
            proj_s = regroup(rms_matmul(hs, a_norm1[j], w, bs, f"a_in_s_{i}"))
            mix_p, s_p = deltanet_prompt(proj_p, col_a, a_conv_w[j], a_log, dt_b, o_norm,
                                         bp, seq, heads, dk, dv, rows=DN_STEP_CHUNKS * CHUNK)
            mix_s, s_s, c_s = deltanet_sample(proj_s, col_a, state_dn_conv[j], state_dn[j], a_conv_w[j],
                                              a_log, dt_b, o_norm, heads, dk, dv, nb=SUBLANE)
            xa_p = xattn_prompt(proj_p, col_a["cq"], kv, bp, seq, n_mem, x_heads, x_dh, tq=min(ROWS_XATTN, seq))
            xa_s = xattn_sample(proj_s, col_a["cq"], cache_mem_k, cache_mem_v, i, nb=SUBLANE)
            hp = out_proj_ffn(hp, mix_p, xa_p, a_w_out[j].astype(BF16), a_norm2[j], a_ffn_wi[j].astype(BF16),
                              a_ffn_wo[j].astype(BF16), ROWS_OUT_FFN, f"a_out_ffn_p_{i}")
            hs = out_proj(hs, mix_s, xa_s, a_w_out[j], bs, f"a_out_s_{i}")
            hs = ffn_dense(hs, a_norm2[j], a_ffn_wi[j], a_ffn_wo[j], bs, f"a_ffn_s_{i}", tf=FF_TILE_SAMPLE)
            dn_p.append(s_p)
            conv_p.append(proj_p.reshape(bp, seq, n_a)[:, seq - (CONV_WIDTH - 1):, :conv_ch])
            dn_s.append(s_s)
            conv_s.append(c_s)
        else:
            ps = b_pool_scale[j].reshape(1, pool_w_total)
            last = i == depth - 1

            mix_p, cq_p, pool_tail = proj_pool_prompt(hp, b_norm1[j], b_w_in[j].astype(BF16),
                                                      b_pool_w[j].astype(BF16), ps, bp, seq, tt=ROWS_STREAM)
            proj_s = rms_matmul(hs, b_norm1[j], b_w_in[j], bs, f"b_in_s_{i}")
            mix_s, buf_s = pool_sample(proj_s, state_pool[j], b_pool_w[j], ps)
            xa_p = xattn_prompt(cq_p, 0, kv, bp, seq, n_mem, x_heads, x_dh, tq=min(ROWS_XATTN, seq))
            xa_s = xattn_sample(proj_s, pool_w_total, cache_mem_k, cache_mem_v, i, nb=SUBLANE)
            n_p = bp * seq
            n_experts = b_router.shape[-1]
            r_pad = jnp.concatenate([b_router[j], jnp.zeros((d, LANE - n_experts), F32)], axis=1)
            h_all, idx_p, gate_p = out_proj(hp, mix_p, xa_p, b_w_out[j].astype(BF16), ROWS_STREAM, f"b_out_p_{i}",
                                            out_rows=n_p + bs, route=(b_norm2[j], r_pad.astype(BF16), n_experts))
            h_all, idx_s, gate_s = out_proj(hs, mix_s, xa_s, b_w_out[j], bs, f"b_out_s_{i}", out_rows=n_p + bs,
                                            into=h_all, row_offset=n_p, route=(b_norm2[j], r_pad, n_experts))
            idx = jnp.concatenate([idx_p[:n_p, :TOP_K], idx_s[:, :TOP_K]], axis=0)
            (dest, gather_rows, scatter_rows, tile_expert, n_used, n_blocks_used, tile_fill, n_rows) = moe_plan(
                idx, n_experts, MOE_TILE, MOE_ROW_BLOCK)
            h_ps = jnp.zeros((n_rows, d + LANE), F32)
            h_ps = moe_dispatch(h_all, 0, n_p, gate_p, dest[:n_p], h_ps, ROWS_STREAM, f"b_dispatch_p_{i}")
            h_ps = moe_dispatch(h_all, n_p, bs, gate_s, dest[n_p:], h_ps, bs, f"b_dispatch_s_{i}")
            y = moe_experts(h_ps, b_norm2[j], b_moe_wi[j], b_moe_wo[j], gather_rows, scatter_rows, tile_expert,
                            n_used, n_blocks_used, tile_fill, MOE_ROW_BLOCK, n_rows, MOE_TILE, MOE_FF_TILE)
            y_ps = moe_combine(h_ps, y, n_blocks_used, n_rows, final_norm, MOE_ROW_BLOCK, last,
                               f"b_combine_{i}")
            hp = moe_unpermute(y_ps, dest[:n_p], ROWS_STREAM, f"b_unpermute_p_{i}")
            hs = moe_unpermute(y_ps, dest[n_p:], bs, f"b_unpermute_s_{i}")
            pool_p.append(pool_tail[:, POOL_HALO - pool_buf:, :])
            pool_s.append(buf_s)

    y_prompt = hp.reshape(bp, seq, d)
    y_sample = hs.reshape(bs, 1, d)
    return (y_prompt, y_sample, jnp.stack(dn_p), jnp.stack(conv_p), jnp.stack(pool_p),
            mk_all.reshape(depth, bp, n_mem, x_heads, x_dh), mv_all.reshape(depth, bp, n_mem, x_heads, x_dh),
            jnp.stack(dn_s), jnp.stack(conv_s), jnp.stack(pool_s))
```

```python
import functools

import jax
import jax.numpy as jnp
from jax import lax
from jax.experimental import pallas as pl
from jax.experimental.pallas import tpu as pltpu

F32 = jnp.float32
BF16 = jnp.bfloat16
HIGHEST = lax.Precision.HIGHEST

EPS = 1e-6
CHUNK = 64
CONV_WIDTH = 4
POOL_WINDOWS = (2, 4, 8, 16)
PAST_LEN = 16384
TOP_K = 2

LANE = 128
SUBLANE = 8
VMEM_LIMIT_BYTES = 56 * 2**20

ROWS_IN_A = 512
ROWS_OUT_FFN = 512
ROWS_STREAM = 512
ROWS_MEM = 256
ROWS_XATTN = 1024
FF_TILE_SAMPLE = 256
MOE_TILE = 1024
MOE_FF_TILE = 512
MOE_ROW_BLOCK = 512
DN_STEP_CHUNKS = 8
INV_BLOCK = 16
POOL_HALO = 32


def _params(*sem):
    return pltpu.CompilerParams(dimension_semantics=sem, vmem_limit_bytes=VMEM_LIMIT_BYTES)


def _resident(shape):
    nd = len(shape)
    return pl.BlockSpec(shape, lambda *_: (0,) * nd, pipeline_mode=pl.Buffered(1))


def _rms(x, w):
    return x * lax.rsqrt(jnp.mean(x * x, axis=-1, keepdims=True) + EPS) * w


def _silu(x):
    half = 0.5 * x
    return half + half * jnp.tanh(half)


def _dot(a, b):
    return jnp.dot(a, b, preferred_element_type=F32)


def _wdot(a, w):
    if w.dtype == F32:
        return jnp.dot(a, w, precision=HIGHEST, preferred_element_type=F32)
    return jnp.dot(a.astype(BF16), w, preferred_element_type=F32)


def _bdot(a, b):
    return lax.dot_general(a, b, (((2,), (1,)), ((0,), (0,))), preferred_element_type=F32)


def _bdot_nt(a, b):
    return lax.dot_general(a, b, (((2,), (2,)), ((0,), (0,))), preferred_element_type=F32)


def _bdot_tn(a, b):
    return lax.dot_general(a, b, (((1,), (1,)), ((0,), (0,))), preferred_element_type=F32)


def _rms_matmul_body(x_ref, g_ref, w_ref, o_ref):
    xn = _rms(x_ref[...], g_ref[...])
    o_ref[...] = _wdot(xn, w_ref[...])


def _wspec(shape, index_map, steps):
    if steps == 1:
        return pl.BlockSpec(shape, index_map, pipeline_mode=pl.Buffered(1))
    return pl.BlockSpec(shape, index_map)


def rms_matmul(x, g, w, tm, name, tn=None):
    m, d = x.shape
    n = w.shape[1]
    tn = n if tn is None else tn
    nj = n // tn
    return pl.pallas_call(
        _rms_matmul_body,
        out_shape=jax.ShapeDtypeStruct((m, n), F32),
        grid=(m // tm, nj),
        in_specs=[pl.BlockSpec((tm, d), lambda i, j: (i, 0)), _resident((1, d)),
                  _wspec((d, tn), lambda i, j: (0, j), nj)],
        out_specs=pl.BlockSpec((tm, tn), lambda i, j: (i, j)),
        compiler_params=_params("parallel", "arbitrary"),
        name=name,
    )(x, g.reshape(1, d), w)


def _out_proj_body(x_ref, a1_ref, a2_ref, w1_ref, w2_ref, *rest, n_main, n_experts, has_into):
    route = n_experts is not None
    if route:
        g_ref, r_ref = rest[0], rest[1]
        o_ref, idx_ref, gate_ref = rest[-3:]
    else:
        o_ref = rest[-1]
    del has_into

    @pl.when(pl.program_id(0) < n_main)
    def _():
        acc = _wdot(a1_ref[...], w1_ref[...]) + _wdot(a2_ref[...], w2_ref[...])
        h = x_ref[...] + acc
        o_ref[...] = h
        if route:
            idx_ref[...], gate_ref[...] = _route_top2(h, g_ref[...], r_ref[...], n_experts)

    @pl.when(pl.program_id(0) >= n_main)
    def _():
        o_ref[...] = jnp.zeros_like(o_ref)
        if route:
            idx_ref[...] = jnp.zeros_like(idx_ref)
            gate_ref[...] = jnp.zeros_like(gate_ref)


def out_proj(x, a1, a2, w, tm, name, out_rows=None, into=None, row_offset=0, route=None):
    m, d = x.shape
    k1, k2 = a1.shape[1], a2.shape[1]
    assert w.shape[0] == k1 + k2 and k1 % k2 == 0 and row_offset % tm == 0
    out_rows = m if out_rows is None else out_rows
    off = row_offset // tm
    n_main = m // tm
    n_extra = 0 if (into is not None or out_rows == m) else 1
    assert out_rows - m <= n_extra * tm or into is not None
    last = n_main - 1
    in_specs = [
        pl.BlockSpec((tm, d), lambda i: (jnp.minimum(i, last), 0)),
        pl.BlockSpec((tm, k1), lambda i: (jnp.minimum(i, last), 0)),
        pl.BlockSpec((tm, k2), lambda i: (jnp.minimum(i, last), 0)),
        pl.BlockSpec((k1, d), lambda i: (0, 0), pipeline_mode=pl.Buffered(1)),
        pl.BlockSpec((k2, d), lambda i: (k1 // k2, 0), pipeline_mode=pl.Buffered(1)),
    ]
    args = [x, a1, a2, w, w]
    out_shape = [jax.ShapeDtypeStruct((out_rows, d), F32)]
    out_specs = [pl.BlockSpec((tm, d), lambda i: (i + off, 0))]
    n_experts = None
    if route is not None:
        g, r_pad, n_experts = route
        in_specs += [_resident((1, d)), _resident((d, LANE))]
        args += [g.reshape(1, d), r_pad]
        n_steps = n_main + n_extra
        out_shape += [jax.ShapeDtypeStruct((n_steps * tm, LANE), jnp.int32),
                      jax.ShapeDtypeStruct((n_steps * tm, LANE), F32)]
        out_specs += [pl.BlockSpec((tm, LANE), lambda i: (i, 0))] * 2
    aliases = {}
    if into is not None:
        in_specs.append(pl.BlockSpec(memory_space=pl.ANY))
        args.append(into)
        aliases = {len(args) - 1: 0}
    outs = pl.pallas_call(
        functools.partial(_out_proj_body, n_main=n_main, n_experts=n_experts, has_into=into is not None),
        out_shape=tuple(out_shape),
        grid=(n_main + n_extra,),
        in_specs=in_specs,
        out_specs=tuple(out_specs),
        input_output_aliases=aliases,
        compiler_params=_params("parallel"),
        name=name,
    )(*args)
    return outs if route is not None else outs[0]


def _ffn_body(h_ref, g_ref, wg_ref, wu_ref, wo_ref, o_ref, xn_ref):
    @pl.when(pl.program_id(1) == 0)
    def _():
        h = h_ref[...]
        xn_ref[...] = _rms(h, g_ref[...])
        o_ref[...] = h

    xn = xn_ref[...]
    a = _wdot(xn, wg_ref[...])
    b = _wdot(xn, wu_ref[...])
    o_ref[...] += _wdot(_silu(a) * b, wo_ref[...])


def ffn_dense(h, g, wi, wo, tm, name, tf=None):
    m, d = h.shape
    f = wo.shape[0]
    tf = f if tf is None else tf
    nj = f // tf
    return pl.pallas_call(
        _ffn_body,
        out_shape=jax.ShapeDtypeStruct((m, d), F32),
        grid=(m // tm, nj),
        in_specs=[
            pl.BlockSpec((tm, d), lambda i, j: (i, 0)),
            _resident((1, d)),
            _wspec((d, tf), lambda i, j: (0, j), nj),
            _wspec((d, tf), lambda i, j: (0, nj + j), nj),
            _wspec((tf, d), lambda i, j: (j, 0), nj),
        ],
        out_specs=pl.BlockSpec((tm, d), lambda i, j: (i, 0)),
        scratch_shapes=[pltpu.VMEM((tm, d), F32)],
        compiler_params=_params("parallel", "arbitrary"),
        name=name,
    )(h, g.reshape(1, d), wi, wi, wo)


def _mem_kv_body(x_ref, g_ref, w_ref, *rest, heads, dh, n_own):
    kv_ref, k_ref, v_ref = rest[-3:]
    xw = heads * dh

    @pl.when(pl.program_id(0) < n_own)
    def _():
        res = _wdot(_rms(x_ref[...], g_ref[...]), w_ref[...])
        kv_ref[...] = res
        for h in range(heads):
            k_ref[:, h, :] = res[:, h * dh:(h + 1) * dh]
            v_ref[:, h, :] = res[:, xw + h * dh:xw + (h + 1) * dh]

    @pl.when(pl.program_id(0) >= n_own)
    def _():
        k_ref[...] = jnp.zeros_like(k_ref)
        v_ref[...] = jnp.zeros_like(v_ref)


def mem_kv(mem, g, w_kv, heads, dh, tm, name, layer, n_layers, into=None):
    m, d = mem.shape
    xw = heads * dh
    n_own = m // tm
    assert (into is None) == (layer == 0)
    n_steps = n_own * n_layers if into is None else n_own
    head_shape = jax.ShapeDtypeStruct((n_layers * m, heads, dh), F32)
    head_spec = pl.BlockSpec((tm, heads, dh), lambda i: (i + layer * n_own, 0, 0))
    own = lambda i: (jnp.minimum(i, n_own - 1), 0)
    in_specs = [pl.BlockSpec((tm, d), own), _resident((1, d)), _resident((d, 2 * xw))]
    args = [mem, g.reshape(1, d), w_kv]
    aliases = {}
    if into is not None:
        in_specs += [pl.BlockSpec(memory_space=pl.ANY)] * 2
        args += list(into)
        aliases = {3: 1, 4: 2}
    return pl.pallas_call(
        functools.partial(_mem_kv_body, heads=heads, dh=dh, n_own=n_own),
        out_shape=(jax.ShapeDtypeStruct((m, 2 * xw), F32), head_shape, head_shape),
        grid=(n_steps,),
        in_specs=in_specs,
        out_specs=(pl.BlockSpec((tm, 2 * xw), own), head_spec, head_spec),
        input_output_aliases=aliases,
        compiler_params=_params("arbitrary"),
        name=name,
    )(*args)


def _out_proj_ffn_body(x_ref, a1_ref, a2_ref, w1_ref, w2_ref, g_ref, wg_ref, wu_ref, wo_ref, o_ref):
    h = x_ref[...] + (_wdot(a1_ref[...], w1_ref[...]) + _wdot(a2_ref[...], w2_ref[...]))
    xn = _rms(h, g_ref[...])
    a = _wdot(xn, wg_ref[...])
    b = _wdot(xn, wu_ref[...])
    o_ref[...] = h + _wdot(_silu(a) * b, wo_ref[...])


def out_proj_ffn(x, a1, a2, w, g, wi, wo, tm, name):
    m, d = x.shape
    k1, k2 = a1.shape[1], a2.shape[1]
    f = wo.shape[0]
    assert w.shape[0] == k1 + k2 and k1 % k2 == 0
    return pl.pallas_call(
        _out_proj_ffn_body,
        out_shape=jax.ShapeDtypeStruct((m, d), F32),
        grid=(m // tm,),
        in_specs=[
            pl.BlockSpec((tm, d), lambda i: (i, 0)),
            pl.BlockSpec((tm, k1), lambda i: (i, 0)),
            pl.BlockSpec((tm, k2), lambda i: (i, 0)),
            pl.BlockSpec((k1, d), lambda i: (0, 0), pipeline_mode=pl.Buffered(1)),
            pl.BlockSpec((k2, d), lambda i: (k1 // k2, 0), pipeline_mode=pl.Buffered(1)),
            _resident((1, d)),
            pl.BlockSpec((d, f), lambda i: (0, 0), pipeline_mode=pl.Buffered(1)),
            pl.BlockSpec((d, f), lambda i: (0, 1), pipeline_mode=pl.Buffered(1)),
            _resident((f, d)),
        ],
        out_specs=pl.BlockSpec((tm, d), lambda i: (i, 0)),
        compiler_params=_params("parallel"),
        name=name,
    )(x, a1, a2, w, w, g.reshape(1, d), wi, wi, wo)


def _pair_blockdiag(y, left):
    yb = y.astype(BF16)
    zero = jnp.zeros_like(yb)
    return jnp.concatenate([jnp.where(left, yb, zero), jnp.where(left, zero, yb)], axis=1)


def _pair_mm(x, y, left):
    return _bdot(x.astype(BF16), _pair_blockdiag(y, left))


def _unit_lower_inverse(mx, eye, blk, left):
    mm = functools.partial(_pair_mm, left=left)
    md = jnp.where(blk, mx, 0.0)
    c = mx - md
    p = eye - md
    m2 = mm(md, md)
    p = p + mm(p, m2)
    m4 = mm(m2, m2)
    p = p + mm(p, m4)
    m8 = mm(m4, m4)
    td = p + mm(p, m8)
    n = mm(td, c)
    n2 = mm(n, n)
    q = eye - n
    q = q + mm(q, n2)
    return mm(q, td)


def _dn_prompt_body(qkv_ref, z_ref, bl_ref, al_ref, cw_ref, alog_ref, dtb_ref, on_ref,
                    dn_ref, sfin_ref, s_ref, xbuf_ref, act_ref, *, rows, heads, dk, dv):
    c = pl.program_id(1)
    wq = heads * dk

    @pl.when(c == 0)
    def _():
        s_ref[...] = jnp.zeros_like(s_ref)
        xbuf_ref[0:SUBLANE, :] = jnp.zeros((SUBLANE, xbuf_ref.shape[1]), F32)

    xbuf_ref[SUBLANE:SUBLANE + rows, :] = qkv_ref[...]
    first = SUBLANE - (CONV_WIDTH - 1)
    y = xbuf_ref[pl.ds(first, rows), :] * cw_ref[0:1, :]
    for j in range(1, CONV_WIDTH):
        y = y + xbuf_ref[pl.ds(first + j, rows), :] * cw_ref[j:j + 1, :]
    act_ref[...] = _silu(y)
    xbuf_ref[0:SUBLANE, :] = xbuf_ref[rows:rows + SUBLANE, :]

    pairs = heads // 2
    ri = lax.broadcasted_iota(jnp.int32, (CHUNK, 2 * CHUNK), 0)
    lane = lax.broadcasted_iota(jnp.int32, (CHUNK, 2 * CHUNK), 1)
    cj = lane % CHUNK
    left = (lane < CHUNK)[None]
    incl = (ri >= cj)[None]
    strict = (ri > cj)[None]
    blk = ((ri // INV_BLOCK) == (cj // INV_BLOCK))[None]
    eye = (ri == cj).astype(F32)[None]
    tr = lax.broadcasted_iota(jnp.int32, (CHUNK, CHUNK), 0)
    tc = lax.broadcasted_iota(jnp.int32, (CHUNK, CHUNK), 1)
    tri = (tr >= tc).astype(F32)

    n_chunks = rows // CHUNK
    zero = jnp.zeros((CHUNK, dk), BF16)

    def chunk_rows(cc):
        return slice(cc * CHUNK, (cc + 1) * CHUNK)

    def per_head(fn):
        return jnp.stack([fn(chunk_rows(cc), h) for cc in range(n_chunks) for h in range(heads)], axis=0)

    def per_pair(fn):
        return jnp.stack([fn(cc, p) for cc in range(n_chunks) for p in range(pairs)], axis=0)

    q3 = per_head(lambda rs, h: act_ref[rs, h * dk:(h + 1) * dk])
    k3 = per_head(lambda rs, h: act_ref[rs, wq + h * dk:wq + (h + 1) * dk])
    v3 = per_head(lambda rs, h: act_ref[rs, 2 * wq + h * dv:2 * wq + (h + 1) * dv])
    qn = q3 * lax.rsqrt(jnp.sum(q3 * q3, axis=-1, keepdims=True) + EPS) * (dk ** -0.5)
    kn = k3 * lax.rsqrt(jnp.sum(k3 * k3, axis=-1, keepdims=True) + EPS)

    beta = jax.nn.sigmoid(bl_ref[...])
    log_decay = -jnp.exp(alog_ref[...]) * jax.nn.softplus(al_ref[...] + dtb_ref[...])
    gcum = [jnp.dot(tri, log_decay[chunk_rows(cc)], precision=HIGHEST, preferred_element_type=F32)
            for cc in range(n_chunks)]
    gcum_t = [jnp.concatenate([gc_, pltpu.roll(gc_, LANE - 1, 1)], axis=0).T for gc_ in gcum]

    def pair_cols(x_of):
        return per_pair(lambda cc, p: jnp.where(left[0], x_of(cc)[:, 2 * p:2 * p + 1],
                                                x_of(cc)[:, 2 * p + 1:2 * p + 2]))

    gc2 = pair_cols(lambda cc: gcum[cc])
    bc2 = pair_cols(lambda cc: beta[chunk_rows(cc)])
    gr2 = per_pair(lambda cc, p: gcum_t[cc][2 * p:2 * p + 1, :])
    decay = jnp.where(incl, jnp.exp(gc2 - gr2), 0.0)

    kb = kn.astype(BF16)
    qb = qn.astype(BF16)

    def pair_of(x, cc, p):
        return x[cc * heads + 2 * p], x[cc * heads + 2 * p + 1]

    k_lhs = per_pair(lambda cc, p: jnp.concatenate(pair_of(kb, cc, p), axis=1))
    q_lhs = per_pair(lambda cc, p: jnp.concatenate(pair_of(qb, cc, p), axis=1))
    k_bd = per_pair(lambda cc, p: jnp.concatenate(
        [jnp.concatenate([pair_of(kb, cc, p)[0], zero], axis=1),
         jnp.concatenate([zero, pair_of(kb, cc, p)[1]], axis=1)], axis=0))
    kk = _bdot_nt(k_lhs, k_bd)
    mx = jnp.where(strict, bc2 * kk * decay, 0.0)
    tinv = _unit_lower_inverse(mx, eye, blk, left)

    gcum_rows = jnp.concatenate(gcum, axis=0)
    gc = per_head(lambda rs, h: gcum_rows[rs, h:h + 1])
    bc = per_head(lambda rs, h: beta[rs, h:h + 1])
    eg = jnp.exp(gc)
    rhs = jnp.concatenate([v3 * bc, kn * (bc * eg)], axis=-1)
    sol = _bdot(_pair_blockdiag(tinv, left), rhs.astype(BF16).reshape(n_chunks * pairs, 2 * CHUNK, dv + dk))
    sol = sol.reshape(n_chunks * heads, CHUNK, dv + dk)
    w_val = sol[..., :dv]
    k_cd = sol[..., dv:].astype(BF16)
    qk_bd = _pair_blockdiag(_bdot_nt(q_lhs, k_bd) * decay, left)
    q_dec = (qn * eg).astype(BF16)
    g_last = gc[:, CHUNK - 1:CHUNK, :]
    k_tail = (kn * jnp.exp(g_last - gc)).astype(BF16)
    c_decay = jnp.exp(g_last)

    for cc in range(n_chunks):
        hs = slice(cc * heads, (cc + 1) * heads)
        ps = slice(cc * pairs, (cc + 1) * pairs)
        rs = chunk_rows(cc)
        s = s_ref[...]
        sb = s.astype(BF16)
        u = w_val[hs] - _bdot(k_cd[hs], sb)
        ub = u.astype(BF16)
        intra = _bdot(qk_bd[ps], ub.reshape(pairs, 2 * CHUNK, dv))
        o = _bdot(q_dec[hs], sb) + intra.reshape(heads, CHUNK, dv)
        s_ref[...] = s * c_decay[hs] + _bdot_tn(k_tail[hs], ub)

        o = o * lax.rsqrt(jnp.mean(o * o, axis=-1, keepdims=True) + EPS) * on_ref[...]
        for h in range(heads):
            z = z_ref[rs, h * dv:(h + 1) * dv]
            dn_ref[rs, h * dv:(h + 1) * dv] = o[h] * _silu(z)

    @pl.when(c == pl.num_programs(1) - 1)
    def _():
        sfin_ref[0] = s_ref[...]


def deltanet_prompt(proj, col, conv_w, a_log, dt_bias, o_norm, batch, seq, heads, dk, dv, rows):
    conv_ch = 2 * heads * dk + heads * dv
    zw = heads * dv
    steps = seq // rows

    def row_map(width_off):
        return lambda b, c: (b * steps + c, width_off)

    body = functools.partial(_dn_prompt_body, rows=rows, heads=heads, dk=dk, dv=dv)
    return pl.pallas_call(
        body,
        out_shape=(jax.ShapeDtypeStruct((batch * seq, zw), F32),
                   jax.ShapeDtypeStruct((batch, heads, dk, dv), F32)),
        grid=(batch, steps),
        in_specs=[
            pl.BlockSpec((rows, conv_ch), row_map(col["qkv"] // conv_ch)),
            pl.BlockSpec((rows, zw), row_map(col["z"] // zw)),
            pl.BlockSpec((rows, LANE), row_map(col["b"] // LANE)),
            pl.BlockSpec((rows, LANE), row_map(col["a"] // LANE)),
            _resident((CONV_WIDTH, conv_ch)),
            _resident((1, LANE)),
            _resident((1, LANE)),
            _resident((1, dv)),
        ],
        out_specs=(pl.BlockSpec((rows, zw), lambda b, c: (b * steps + c, 0)),
                   pl.BlockSpec((1, heads, dk, dv), lambda b, c: (b, 0, 0, 0))),
        scratch_shapes=[
            pltpu.VMEM((heads, dk, dv), F32),
            pltpu.VMEM((rows + SUBLANE, conv_ch), F32),
            pltpu.VMEM((rows, conv_ch), F32),
        ],
        compiler_params=_params("parallel", "arbitrary"),
        name="deltanet_prompt",
    )(proj, proj, proj, proj, conv_w, a_log, dt_bias, o_norm)


def _dn_sample_body(qkv_ref, z_ref, bl_ref, al_ref, cbuf_ref, st_ref, cw_ref, alog_ref, dtb_ref, on_ref,
                    dn_ref, sto_ref, cbo_ref, act_ref, *, nb, heads, dk, dv):
    wq = heads * dk
    new = qkv_ref[...]
    y = cbuf_ref[:, 0, :] * cw_ref[0:1, :]
    for j in range(1, CONV_WIDTH - 1):
        y = y + cbuf_ref[:, j, :] * cw_ref[j:j + 1, :]
    y = y + new * cw_ref[CONV_WIDTH - 1:CONV_WIDTH, :]
    act_ref[...] = _silu(y)
    for j in range(CONV_WIDTH - 2):
        cbo_ref[:, j, :] = cbuf_ref[:, j + 1, :]
    cbo_ref[:, CONV_WIDTH - 2, :] = new

    beta = jax.nn.sigmoid(bl_ref[...])
    g = -jnp.exp(alog_ref[...]) * jax.nn.softplus(al_ref[...] + dtb_ref[...])
    dec = jnp.exp(g)

    q_t, k_t = [], []
    for h in range(heads):
        q = act_ref[:, h * dk:(h + 1) * dk]
        k = act_ref[:, wq + h * dk:wq + (h + 1) * dk]
        qn = q * lax.rsqrt(jnp.sum(q * q, axis=-1, keepdims=True) + EPS) * (dk ** -0.5)
        kn = k * lax.rsqrt(jnp.sum(k * k, axis=-1, keepdims=True) + EPS)
        q_t.append(qn.T)
        k_t.append(kn.T)

    def per_head(fn):
        return jnp.stack([fn(h) for h in range(heads)], axis=0)

    for i in range(nb):
        row = slice(i, i + 1)
        kcol = per_head(lambda h: k_t[h][:, i:i + 1])
        qcol = per_head(lambda h: q_t[h][:, i:i + 1])
        v = per_head(lambda h: act_ref[row, 2 * wq + h * dv:2 * wq + (h + 1) * dv])
        z = per_head(lambda h: z_ref[row, h * dv:(h + 1) * dv])
        b_i = per_head(lambda h: beta[row, h:h + 1])
        d_i = per_head(lambda h: dec[row, h:h + 1])
        s = st_ref[i] * d_i
        u = b_i * (v - jnp.sum(kcol * s, axis=1, keepdims=True))
        s = s + kcol * u
        o = jnp.sum(qcol * s, axis=1, keepdims=True)
        sto_ref[i] = s
        o = o * lax.rsqrt(jnp.mean(o * o, axis=-1, keepdims=True) + EPS) * on_ref[...]
        o = o * _silu(z)
        for h in range(heads):
            dn_ref[row, h * dv:(h + 1) * dv] = o[h]


def deltanet_sample(proj, col, conv_buf, state, conv_w, a_log, dt_bias, o_norm, heads, dk, dv, nb):
    n_seq = proj.shape[0]
    conv_ch = 2 * heads * dk + heads * dv
    zw = heads * dv
    body = functools.partial(_dn_sample_body, nb=nb, heads=heads, dk=dk, dv=dv)
    return pl.pallas_call(
        body,
        out_shape=(jax.ShapeDtypeStruct((n_seq, zw), F32),
                   jax.ShapeDtypeStruct(state.shape, F32),
                   jax.ShapeDtypeStruct(conv_buf.shape, F32)),
        grid=(n_seq // nb,),
        in_specs=[
            pl.BlockSpec((nb, conv_ch), lambda i: (i, col["qkv"] // conv_ch)),
            pl.BlockSpec((nb, zw), lambda i: (i, col["z"] // zw)),
            pl.BlockSpec((nb, LANE), lambda i: (i, col["b"] // LANE)),
            pl.BlockSpec((nb, LANE), lambda i: (i, col["a"] // LANE)),
            pl.BlockSpec((nb, CONV_WIDTH - 1, conv_ch), lambda i: (i, 0, 0)),
            pl.BlockSpec((nb, heads, dk, dv), lambda i: (i, 0, 0, 0)),
            _resident((CONV_WIDTH, conv_ch)),
            _resident((1, LANE)),
            _resident((1, LANE)),
            _resident((1, dv)),
        ],
        out_specs=(pl.BlockSpec((nb, zw), lambda i: (i, 0)),
                   pl.BlockSpec((nb, heads, dk, dv), lambda i: (i, 0, 0, 0)),
                   pl.BlockSpec((nb, CONV_WIDTH - 1, conv_ch), lambda i: (i, 0, 0))),
        scratch_shapes=[
            pltpu.VMEM((nb, conv_ch), F32),
        ],
        compiler_params=_params("parallel"),
        name="deltanet_sample",
    )(proj, proj, proj, proj, conv_buf, state, conv_w, a_log, dt_bias, o_norm)


def _xattn_prompt_body(q_ref, k_ref, v_ref, o_ref, *, heads, dh):
    for h in range(heads):
        sl = slice(h * dh, (h + 1) * dh)
        q = q_ref[:, sl].astype(BF16)
        k = k_ref[:, sl].astype(BF16)
        v = v_ref[:, sl].astype(BF16)
        s = lax.dot_general(q, k, (((1,), (1,)), ((), ())), preferred_element_type=F32) * (dh ** -0.5)
        e = jnp.exp(s - jnp.max(s, axis=-1, keepdims=True))
        p = e / jnp.sum(e, axis=-1, keepdims=True)
        o_ref[:, sl] = _dot(p.astype(BF16), v)


def xattn_prompt(proj, cq_col, kv, batch, seq, n_mem, heads, dh, tq):
    xw = heads * dh
    steps = seq // tq
    body = functools.partial(_xattn_prompt_body, heads=heads, dh=dh)
    return pl.pallas_call(
        body,
        out_shape=jax.ShapeDtypeStruct((batch * seq, xw), F32),
        grid=(batch, steps),
        in_specs=[
            pl.BlockSpec((tq, xw), lambda b, i: (b * steps + i, cq_col // xw)),
            pl.BlockSpec((n_mem, xw), lambda b, i: (b, 0)),
            pl.BlockSpec((n_mem, xw), lambda b, i: (b, 1)),
        ],
        out_specs=pl.BlockSpec((tq, xw), lambda b, i: (b * steps + i, 0)),
        compiler_params=_params("parallel", "parallel"),
        name="xattn_prompt",
    )(proj, kv, kv)


def _xattn_sample_body(q_ref, k_ref, v_ref, o_ref, *, nb, heads, dh):
    rep = SUBLANE // heads
    n_t = k_ref.shape[2] // SUBLANE

    def over_groups(x, op):
        out = x
        for t in range(1, rep):
            out = op(out, pltpu.roll(x, t * heads, 0))
        return out

    for i in range(nb):
        row = slice(i, i + 1)
        q8 = jnp.concatenate([q_ref[row, h * dh:(h + 1) * dh] for h in range(heads)] * rep, axis=0)
        k3 = k_ref[0, i].reshape(n_t, SUBLANE, dh)
        v3 = v_ref[0, i].reshape(n_t, SUBLANE, dh)
        s = jnp.sum(k3 * q8[None], axis=-1, keepdims=True) * (dh ** -0.5)
        m = over_groups(jnp.max(s, axis=0), jnp.maximum)
        e = jnp.exp(s - m[None])
        den = over_groups(jnp.sum(e, axis=0), jnp.add)
        p = e / den[None]
        acc = over_groups(jnp.sum(p * v3, axis=0), jnp.add)
        for h in range(heads):
            o_ref[row, h * dh:(h + 1) * dh] = acc[h:h + 1, :]


def xattn_sample(proj, cq_col, cache_k, cache_v, layer, nb):
    n_layers, n_seq, n_mem, heads, dh = cache_k.shape
    assert SUBLANE % heads == 0 and (n_mem * heads) % SUBLANE == 0
    xw = heads * dh
    body = functools.partial(_xattn_sample_body, nb=nb, heads=heads, dh=dh)
    cache_spec = pl.BlockSpec((1, nb, n_mem * heads, dh), lambda i: (layer, i, 0, 0))
    flat = (n_layers, n_seq, n_mem * heads, dh)
    return pl.pallas_call(
        body,
        out_shape=jax.ShapeDtypeStruct((n_seq, xw), F32),
        grid=(n_seq // nb,),
        in_specs=[pl.BlockSpec((nb, xw), lambda i: (i, cq_col // xw)), cache_spec, cache_spec],
        out_specs=pl.BlockSpec((nb, xw), lambda i: (i, 0)),
        compiler_params=_params("parallel"),
        name="xattn_sample",
    )(proj, cache_k.reshape(flat), cache_v.reshape(flat))


def _pool_mix(sums, cnts, u_of, pw_ref, ps_ref, o_ref, gw):
    for gi in range(len(POOL_WINDOWS)):
        cols = slice(gi * gw, (gi + 1) * gw)
        d = sums[gi] / cnts[gi] - u_of(cols)
        o_ref[:, cols] = _wdot(d, pw_ref[gi]) * ps_ref[:, cols]


def _pool_prompt_body(x_ref, g_ref, w_ref, pw_ref, ps_ref, o_ref, cq_ref, tail_ref, xbuf, s2, s4, s8, *, tt, gw):
    t = pl.program_id(1)
    n = POOL_HALO + tt
    pw_total = xbuf.shape[1]

    @pl.when(t == 0)
    def _():
        xbuf[0:POOL_HALO, :] = jnp.zeros((POOL_HALO, pw_total), F32)

    proj = _wdot(_rms(x_ref[...], g_ref[...]), w_ref[...])
    cq_ref[...] = proj[:, pw_total:]
    xbuf[POOL_HALO:n, :] = proj[:, 0:pw_total]
    s2[8:n, :] = xbuf[8:n, :] + xbuf[7:n - 1, :]
    s4[16:n, :] = s2[16:n, gw:] + s2[14:n - 2, gw:]
    s8[24:n, :] = s4[24:n, gw:] + s4[20:n - 4, gw:]
    s16 = s8[32:n, gw:] + s8[24:n - 8, gw:]
    sums = [s2[POOL_HALO:n, 0:gw], s4[POOL_HALO:n, 0:gw], s8[POOL_HALO:n, 0:gw], s16]
    pos = t * tt + lax.broadcasted_iota(jnp.int32, (tt, 1), 0)
    cnts = [jnp.minimum(pos + 1, w).astype(F32) for w in POOL_WINDOWS]
    _pool_mix(sums, cnts, lambda cols: xbuf[POOL_HALO:n, cols], pw_ref, ps_ref, o_ref, gw)
    tail_ref[0] = xbuf[tt:n, :]
    xbuf[0:POOL_HALO, :] = xbuf[tt:n, :]


def proj_pool_prompt(x, g, w, pool_w, pool_scale, batch, seq, tt):
    m, d = x.shape
    n_out = w.shape[1]
    pw_total = pool_scale.shape[-1]
    gw = pw_total // len(POOL_WINDOWS)
    steps = seq // tt
    n = POOL_HALO + tt
    body = functools.partial(_pool_prompt_body, tt=tt, gw=gw)
    return pl.pallas_call(
        body,
        out_shape=(jax.ShapeDtypeStruct((m, pw_total), F32),
                   jax.ShapeDtypeStruct((m, n_out - pw_total), F32),
                   jax.ShapeDtypeStruct((batch, POOL_HALO, pw_total), F32)),
        grid=(batch, steps),
        in_specs=[
            pl.BlockSpec((tt, d), lambda b, t: (b * steps + t, 0)),
            _resident((1, d)),
            _resident((d, n_out)),
            _resident(pool_w.shape),
            _resident((1, pw_total)),
        ],
        out_specs=(pl.BlockSpec((tt, pw_total), lambda b, t: (b * steps + t, 0)),
                   pl.BlockSpec((tt, n_out - pw_total), lambda b, t: (b * steps + t, 0)),
                   pl.BlockSpec((1, POOL_HALO, pw_total), lambda b, t: (b, 0, 0))),
        scratch_shapes=[
            pltpu.VMEM((n, pw_total), F32),
            pltpu.VMEM((n, pw_total), F32),
            pltpu.VMEM((n, pw_total - gw), F32),
            pltpu.VMEM((n, pw_total - 2 * gw), F32),
        ],
        compiler_params=_params("parallel", "arbitrary"),
        name="proj_pool_prompt",
    )(x, g.reshape(1, d), w, pool_w, pool_scale)


def _pool_sample_body(u_ref, st_ref, pw_ref, ps_ref, o_ref, sto_ref, *, gw, buf):
    pw_total = gw * len(POOL_WINDOWS)
    new = u_ref[:, 0:pw_total]
    sums, cnts = [], []
    for gi, w in enumerate(POOL_WINDOWS):
        cols = slice(gi * gw, (gi + 1) * gw)
        s = new[:, cols]
        for i in range(1, w):
            s = s + st_ref[:, buf - i, cols]
        sums.append(s)
        cnts.append(float(min(PAST_LEN + 1, w)))
    _pool_mix(sums, cnts, lambda cols: new[:, cols], pw_ref, ps_ref, o_ref, gw)
    sto_ref[:, 0:buf - 1, :] = st_ref[:, 1:buf, :]
    sto_ref[:, buf - 1, :] = new


def pool_sample(proj, state, pool_w, pool_scale):
    n_seq, buf, pw_total = state.shape
    gw = pw_total // len(POOL_WINDOWS)
    body = functools.partial(_pool_sample_body, gw=gw, buf=buf)
    return pl.pallas_call(
        body,
        out_shape=(jax.ShapeDtypeStruct((n_seq, pw_total), F32),
                   jax.ShapeDtypeStruct(state.shape, F32)),
        compiler_params=pltpu.CompilerParams(vmem_limit_bytes=VMEM_LIMIT_BYTES),
        name="pool_sample",
    )(proj, state, pool_w, pool_scale)


def _route_top2(h, g, r, n_experts):
    logits = _wdot(_rms(h, g), r)
    lane = lax.broadcasted_iota(jnp.int32, logits.shape, 1)
    lm = jnp.where(lane < n_experts, logits, -jnp.inf)
    m1 = jnp.max(lm, axis=1, keepdims=True)
    i1 = jnp.min(jnp.where(lm == m1, lane, LANE), axis=1, keepdims=True)
    lm2 = jnp.where(lane == i1, -jnp.inf, lm)
    m2 = jnp.max(lm2, axis=1, keepdims=True)
    i2 = jnp.min(jnp.where(lm2 == m2, lane, LANE), axis=1, keepdims=True)
    e2 = jnp.exp(m2 - m1)
    den = 1.0 + e2
    idx = jnp.where(lane == 0, i1, jnp.where(lane == 1, i2, 0))
    gate = jnp.where(lane == 0, 1.0 / den, jnp.where(lane == 1, e2 / den, 0.0))
    return idx, gate


def _dispatch_body(dest_ref, h_ref, gate_ref, zeros_hbm, o_hbm, ext, sem, *, tb, n):
    del zeros_hbm
    i = pl.program_id(0)
    slot = i % 2
    groups = tb // SUBLANE
    d = h_ref.shape[1]

    def row_copy(s, t, u, row):
        return pltpu.make_async_copy(ext.at[s, t, pl.ds(u, 1)], o_hbm.at[pl.ds(row, 1)], sem.at[s])

    def wait_slot(s):
        def body(t, carry):
            pltpu.make_async_copy(ext.at[s, t], o_hbm.at[pl.ds(0, SUBLANE)], sem.at[s]).wait()
            return carry
        lax.fori_loop(0, groups, body, 0)

    @pl.when(i >= 2)
    def _():
        wait_slot(slot)

    ext[slot, :, :, 0:d] = h_ref[...].reshape(groups, SUBLANE, d)
    ext[slot, :, :, d:d + LANE] = gate_ref[...].reshape(groups, SUBLANE, LANE)

    def body(t, carry):
        for u in range(SUBLANE):
            row_copy(slot, t, u, dest_ref[0, 0, t * SUBLANE + u]).start(priority=u % 2)
        return carry
    lax.fori_loop(0, groups, body, 0)

    @pl.when(i == n - 1)
    def _():
        wait_slot(slot)
        if n >= 2:
            wait_slot(1 - slot)


def moe_dispatch(h_all, row_offset, rows, gates, dest, into, tb, name):
    d = h_all.shape[1]
    off = row_offset // tb
    n = rows // tb
    body = functools.partial(_dispatch_body, tb=tb, n=n)
    return pl.pallas_call(
        body,
        out_shape=jax.ShapeDtypeStruct(into.shape, F32),
        grid=(n,),
        in_specs=[
            pl.BlockSpec((1, 1, tb), lambda i: (i, 0, 0), memory_space=pltpu.SMEM),
            pl.BlockSpec((tb, d), lambda i: (i + off, 0)),
            pl.BlockSpec((tb, LANE), lambda i: (i, 0)),
            pl.BlockSpec(memory_space=pl.ANY),
        ],
        out_specs=pl.BlockSpec(memory_space=pl.ANY),
        scratch_shapes=[pltpu.VMEM((2, tb // SUBLANE, SUBLANE, d + LANE), F32), pltpu.SemaphoreType.DMA((2,))],
        input_output_aliases={3: 0},
        compiler_params=_params("arbitrary"),
        name=name,
    )(dest.reshape(n, 1, tb), h_all, gates, into)


def _experts_body(te_ref, nu_ref, nb_ref, fill_ref, g_first_ref, g_next_ref, s_cur_ref,
                  h_hbm, g_ref, wg_ref, wu_ref, wo_ref, y_hbm,
                  xbuf, acc, xn_ref, gsem, ssem, *, tm, block, n_rows):
    i = pl.program_id(0)
    j = pl.program_id(1)
    n_tiles = pl.num_programs(0)
    nj = pl.num_programs(1)
    n_used = nu_ref[0]
    slot = i % 2
    valid = i < n_used
    chunks = tm // SUBLANE
    d = xn_ref.shape[1]

    def gather(idx_ref, s):
        def body(t, carry):
            row0 = pl.multiple_of(idx_ref[0, 0, t], SUBLANE)
            pltpu.make_async_copy(h_hbm.at[pl.ds(row0, SUBLANE), pl.ds(0, d)], xbuf.at[s, t], gsem.at[s]).start()
            return carry
        lax.fori_loop(0, chunks, body, 0)

    def wait_gather(s):
        def body(t, carry):
            pltpu.make_async_copy(h_hbm.at[pl.ds(0, SUBLANE), pl.ds(0, d)], xbuf.at[s, t], gsem.at[s]).wait()
            return carry
        lax.fori_loop(0, chunks, body, 0)

    def scatter():
        def body(t, carry):
            row0 = pl.multiple_of(s_cur_ref[0, 0, t], SUBLANE)
            pltpu.make_async_copy(acc.at[t], y_hbm.at[pl.ds(row0, SUBLANE)], ssem.at[0]).start()
            return carry
        lax.fori_loop(0, chunks, body, 0)

    def wait_scatter():
        def body(t, carry):
            pltpu.make_async_copy(acc.at[t], y_hbm.at[pl.ds(0, SUBLANE)], ssem.at[0]).wait()
            return carry
        lax.fori_loop(0, chunks, body, 0)

    @pl.when(j == 0)
    def _():
        @pl.when(i == 0)
        def _():
            gather(g_first_ref, 0)
            acc[...] = jnp.zeros_like(acc)
            first_free = nb_ref[0] * (block // SUBLANE)
            n_free = n_rows // SUBLANE - first_free

            def zero_copy(row0):
                return pltpu.make_async_copy(acc.at[0], y_hbm.at[pl.ds(pl.multiple_of(row0, SUBLANE), SUBLANE)],
                                             ssem.at[0])

            def fill(c, carry):
                zero_copy((first_free + c) * SUBLANE).start()
                zero_copy(n_rows + (first_free + c) * SUBLANE).start()
                return carry
            lax.fori_loop(0, n_free, fill, 0)

            def fill_spare(c, carry):
                zero_copy(2 * n_rows + c * SUBLANE).start()
                return carry
            lax.fori_loop(0, chunks, fill_spare, 0)

            def drain(c, carry):
                zero_copy(0).wait()
                return carry
            lax.fori_loop(0, 2 * n_free + chunks, drain, 0)

        @pl.when(i + 1 < n_used)
        def _():
            gather(g_next_ref, 1 - slot)

        @pl.when(valid)
        def _():
            wait_gather(slot)
            xn_ref[...] = _rms(xbuf[slot].reshape(tm, d), g_ref[...]).astype(BF16)

        @pl.when((i >= 1) & (i <= n_used))
        def _():
            wait_scatter()

        @pl.when(valid)
        def _():
            acc[...] = jnp.zeros_like(acc)

    def ff_step(n_chunks):
        rows = n_chunks * SUBLANE
        xn = xn_ref[0:rows, :]
        a = _dot(xn, wg_ref[0].astype(BF16))
        b = _dot(xn, wu_ref[0].astype(BF16))
        part = _dot((_silu(a) * b).astype(BF16), wo_ref[0].astype(BF16)).reshape(n_chunks, SUBLANE, d)
        acc[0:n_chunks] += part

    fill = fill_ref[i]

    @pl.when(valid & (fill > chunks // 2))
    def _():
        ff_step(chunks)

    @pl.when(valid & (fill > chunks // 4) & (fill <= chunks // 2))
    def _():
        ff_step(chunks // 2)

    @pl.when(valid & (fill <= chunks // 4))
    def _():
        ff_step(chunks // 4)

    @pl.when(valid & (j == nj - 1))
    def _():
        scatter()

        @pl.when(i == n_tiles - 1)
        def _():
            wait_scatter()


def moe_experts(h_ps, g, wi, wo, gather_rows, scatter_rows, tile_expert, n_used, n_blocks_used, tile_fill,
                block, n_rows, tm, tf):
    d = wo.shape[2]
    f = wo.shape[1]
    nj = f // tf
    n_tiles = gather_rows.shape[0]
    chunks = tm // SUBLANE

    def wblock(col_off):
        def index_map(i, j, te, nu, nb, tf_):
            jj = jnp.where(i < nu[0], j, nj - 1)
            return (te[i], 0, col_off + jj)
        return index_map

    def woblock(i, j, te, nu, nb, tf_):
        return (te[i], jnp.where(i < nu[0], j, nj - 1), 0)

    def smem_tile(index_map):
        return pl.BlockSpec((1, 1, chunks), index_map, memory_space=pltpu.SMEM)

    grid_spec = pltpu.PrefetchScalarGridSpec(
        num_scalar_prefetch=4,
        grid=(n_tiles, nj),
        in_specs=[
            smem_tile(lambda i, j, te, nu, nb, tf_: (0, 0, 0)),
            smem_tile(lambda i, j, te, nu, nb, tf_: (jnp.minimum(i + 1, n_tiles - 1), 0, 0)),
            smem_tile(lambda i, j, te, nu, nb, tf_: (i, 0, 0)),
            pl.BlockSpec(memory_space=pl.ANY),
            pl.BlockSpec((1, d), lambda i, j, te, nu, nb, tf_: (0, 0)),
            pl.BlockSpec((1, d, tf), wblock(0)),
            pl.BlockSpec((1, d, tf), wblock(nj)),
            pl.BlockSpec((1, tf, d), woblock),
        ],
        out_specs=pl.BlockSpec(memory_space=pl.ANY),
        scratch_shapes=[
            pltpu.VMEM((2, chunks, SUBLANE, d), F32),
            pltpu.VMEM((chunks, SUBLANE, d), F32),
            pltpu.VMEM((tm, d), BF16),
            pltpu.SemaphoreType.DMA((2,)),
            pltpu.SemaphoreType.DMA((1,)),
        ],
    )
    return pl.pallas_call(
        functools.partial(_experts_body, tm=tm, block=block, n_rows=n_rows),
        out_shape=jax.ShapeDtypeStruct((2 * n_rows + tm, d), F32),
        grid_spec=grid_spec,
        compiler_params=_params("arbitrary", "arbitrary"),
        name="moe_experts",
    )(tile_expert, n_used, n_blocks_used, tile_fill, gather_rows, gather_rows, scatter_rows, h_ps,
      g.reshape(1, d), wi, wi, wo)


def _combine_body(nb_ref, hx_ref, y1_ref, y2_ref, fn_ref, o_ref, *, d, final_norm):
    @pl.when(pl.program_id(0) < nb_ref[0])
    def _():
        hx = hx_ref[...]
        out = hx[:, 0:d] + (hx[:, d:d + 1] * y1_ref[...] + hx[:, d + 1:d + 2] * y2_ref[...])
        if final_norm:
            out = _rms(out, fn_ref[...])
        o_ref[...] = out

    @pl.when(pl.program_id(0) >= nb_ref[0])
    def _():
        o_ref[...] = jnp.zeros_like(o_ref)


def moe_combine(h_ps, y, n_blocks_used, choice_stride, fn, tb, final_norm, name):
    rows, dx = h_ps.shape
    d = y.shape[1]
    off2 = choice_stride // tb

    def used(i, nb):
        return jnp.minimum(i, nb[0] - 1)

    grid_spec = pltpu.PrefetchScalarGridSpec(
        num_scalar_prefetch=1,
        grid=(rows // tb,),
        in_specs=[
            pl.BlockSpec((tb, dx), lambda i, nb: (used(i, nb), 0)),
            pl.BlockSpec((tb, d), lambda i, nb: (used(i, nb), 0)),
            pl.BlockSpec((tb, d), lambda i, nb: (used(i, nb) + off2, 0)),
            pl.BlockSpec((1, d), lambda i, nb: (0, 0)),
        ],
        out_specs=pl.BlockSpec((tb, d), lambda i, nb: (i, 0)),
    )
    return pl.pallas_call(
        functools.partial(_combine_body, d=d, final_norm=final_norm),
        out_shape=jax.ShapeDtypeStruct((rows, d), F32),
        grid_spec=grid_spec,
        compiler_params=_params("arbitrary"),
        name=name,
    )(n_blocks_used, h_ps, y, y, fn.reshape(1, d))


def _unpermute_body(cur_ref, nxt_ref, y_hbm, o_ref, buf, sem, *, tb):
    i = pl.program_id(0)
    n = pl.num_programs(0)
    slot = i % 2
    groups = tb // SUBLANE

    def start(idx_ref, s):
        def body(t, carry):
            for u in range(SUBLANE):
                row = idx_ref[0, 0, t * SUBLANE + u]
                pltpu.make_async_copy(y_hbm.at[pl.ds(row, 1)], buf.at[s, t, pl.ds(u, 1)], sem.at[s]).start(
                    priority=u % 2)
            return carry
        lax.fori_loop(0, groups, body, 0)

    @pl.when(i == 0)
    def _():
        start(cur_ref, 0)

    @pl.when(i + 1 < n)
    def _():
        start(nxt_ref, 1 - slot)

    def wait_body(t, carry):
        pltpu.make_async_copy(y_hbm.at[pl.ds(0, SUBLANE)], buf.at[slot, t], sem.at[slot]).wait()
        return carry
    lax.fori_loop(0, groups, wait_body, 0)
    o_ref[...] = buf[slot].reshape(o_ref.shape)


def moe_unpermute(y_ps, dest, tb, name):
    rows = dest.shape[0]
    d = y_ps.shape[1]
    n = rows // tb
    idx = dest.reshape(n, 1, tb)

    def smem_tile(index_map):
        return pl.BlockSpec((1, 1, tb), index_map, memory_space=pltpu.SMEM)

    return pl.pallas_call(
        functools.partial(_unpermute_body, tb=tb),
        out_shape=jax.ShapeDtypeStruct((rows, d), F32),
        grid=(n,),
        in_specs=[smem_tile(lambda i: (i, 0, 0)), smem_tile(lambda i: (jnp.minimum(i + 1, n - 1), 0, 0)),
                  pl.BlockSpec(memory_space=pl.ANY)],
        out_specs=pl.BlockSpec((tb, d), lambda i: (i, 0)),
        scratch_shapes=[pltpu.VMEM((2, tb // SUBLANE, SUBLANE, d), F32), pltpu.SemaphoreType.DMA((2,))],
        compiler_params=_params("arbitrary"),
        name=name,
    )(idx, idx, y_ps)


def moe_plan(idx, n_experts, tm, block):
    n_tok, k = idx.shape
    assert k == 2
    n_pairs = n_experts * n_experts
    chunks_per_tile = tm // SUBLANE
    n_rows = -(-(n_tok + n_pairs * (SUBLANE - 1)) // block) * block
    n_chunks = n_rows // SUBLANE

    pair = idx[:, 0] * n_experts + idx[:, 1]
    onehot = (pair[:, None] == jnp.arange(n_pairs, dtype=jnp.int32)[None, :]).astype(jnp.int32)
    csum = jnp.cumsum(onehot, axis=0)
    counts = csum[-1]
    rank = jnp.sum(csum * onehot, axis=1) - 1
    padded = (counts + SUBLANE - 1) // SUBLANE * SUBLANE
    total = jnp.sum(padded)
    last_group = jnp.max(jnp.where(counts > 0, jnp.arange(n_pairs, dtype=jnp.int32), 0))
    padded = padded + jnp.where(jnp.arange(n_pairs) == last_group, (-total) % block, 0)
    group_end = jnp.cumsum(padded)
    group_start = group_end - padded
    dest = jnp.sum(onehot * group_start[None, :], axis=1) + rank
    rows_used = group_end[-1]

    chunk_row = jnp.arange(n_chunks, dtype=jnp.int32) * SUBLANE
    chunk_group = jnp.sum((chunk_row[:, None] >= group_end[None, :]).astype(jnp.int32), axis=1)
    chunk_used = chunk_row < rows_used
    chunk_group = jnp.minimum(chunk_group, n_pairs - 1)
    chunk_experts = jnp.stack([chunk_group // n_experts, chunk_group % n_experts], axis=1)
    experts = jnp.arange(n_experts, dtype=jnp.int32)
    member = ((chunk_experts[:, :, None] == experts[None, None, :]) & chunk_used[:, None, None])
    member = member.astype(jnp.int32)
    member_e = jnp.sum(member, axis=1)
    csum_e = jnp.cumsum(member_e, axis=0)
    counts_e = csum_e[-1]
    tiles_e = (counts_e + chunks_per_tile - 1) // chunks_per_tile
    tile_end = jnp.cumsum(tiles_e)
    start_e = (tile_end - tiles_e) * chunks_per_tile
    pos = jnp.sum(member * (start_e[None, None, :] + csum_e[:, None, :] - 1), axis=2)
    n_tiles = 2 * n_chunks // chunks_per_tile + n_experts
    n_pos = n_tiles * chunks_per_tile
    pos = jnp.where(chunk_used[:, None], pos, n_pos)
    entry = 2 * jnp.arange(n_chunks, dtype=jnp.int32)[:, None] + jnp.arange(2, dtype=jnp.int32)[None, :] + 1
    table = jnp.zeros((n_pos,), jnp.int32).at[pos.reshape(-1)].set(entry.reshape(-1), mode="drop")
    table = table.reshape(n_tiles, chunks_per_tile)
    filled = table > 0
    src_chunk = (table - 1) // 2
    choice = (table - 1) % 2
    spare = 2 * n_rows + jnp.arange(chunks_per_tile, dtype=jnp.int32) * SUBLANE
    gather_rows = jnp.where(filled, src_chunk * SUBLANE, 0)
    scatter_rows = jnp.where(filled, choice * n_rows + src_chunk * SUBLANE, spare[None, :])

    n_used = tile_end[-1]
    tile_ids = jnp.arange(n_tiles, dtype=jnp.int32)
    tile_expert = jnp.sum((tile_ids[:, None] >= tile_end[None, :]).astype(jnp.int32), axis=1)
    last_expert = jnp.sum((n_used - 1 >= tile_end).astype(jnp.int32))
    tile_expert = jnp.where(tile_ids < n_used, tile_expert, last_expert).astype(jnp.int32)
    return (dest.astype(jnp.int32),
            gather_rows.reshape(n_tiles, 1, chunks_per_tile).astype(jnp.int32),
            scatter_rows.reshape(n_tiles, 1, chunks_per_tile).astype(jnp.int32),
            tile_expert, n_used.reshape(1).astype(jnp.int32),
            (rows_used // block).reshape(1).astype(jnp.int32),
            jnp.sum(filled.astype(jnp.int32), axis=1), n_rows)


def _pad_lanes(v):
    return jnp.zeros((1, LANE), F32).at[0, :v.shape[0]].set(v)


def kernel(x_prompt, x_sample, state_dn, state_dn_conv, state_pool, cache_mem_k, cache_mem_v, mem_prompt, a_norm1, a_w_in, a_conv_w, a_a_log, a_dt_bias, a_o_norm, a_w_out, a_norm2, a_ffn_wi, a_ffn_wo, b_norm1, b_w_in, b_pool_w, b_pool_scale, b_w_out, b_norm2, b_router, b_moe_wi, b_moe_wo, m_norm, m_w_k, m_w_v, final_norm):
    bp, seq, d = x_prompt.shape
    bs = x_sample.shape[0]
    assert x_sample.shape[1] == 1
    depth = m_norm.shape[0]
    assert depth % 2 == 0, "the output norm is fused into the expert block of the last (odd) layer"
    heads, dk, dv = state_dn.shape[2], state_dn.shape[3], state_dn.shape[4]
    conv_ch = state_dn_conv.shape[-1]
    qk_w, v_w = heads * dk, heads * dv
    assert conv_ch == 2 * qk_w + v_w and dk == LANE and dv == LANE and heads <= LANE
    n_mem, x_heads, x_dh = cache_mem_k.shape[2], cache_mem_k.shape[3], cache_mem_k.shape[4]
    xw = x_heads * x_dh
    pool_w_total = state_pool.shape[-1]
    pool_buf = state_pool.shape[2]

    hp = x_prompt.reshape(bp * seq, d)
    hs = x_sample.reshape(bs, d)

    o1, o2, o3, o4 = conv_ch, conv_ch + v_w, conv_ch + v_w + heads, conv_ch + v_w + 2 * heads
    col_a = {"qkv": 0, "z": conv_ch, "cq": conv_ch + v_w, "b": conv_ch + v_w + xw, "a": conv_ch + v_w + xw + LANE}
    n_a = col_a["a"] + LANE

    dn_p, conv_p, pool_p = [], [], []
    mk_all = mv_all = None
    dn_s, conv_s, pool_s = [], [], []
    for i in range(depth):
        j = i // 2
        w_kv = jnp.concatenate([m_w_k[i], m_w_v[i]], axis=1).astype(BF16)
        kv, mk_all, mv_all = mem_kv(mem_prompt.reshape(bp * n_mem, d), m_norm[i], w_kv, x_heads, x_dh, ROWS_MEM,
                                    f"mem_kv_{i}", i, depth, into=None if i == 0 else (mk_all, mv_all))
        if i % 2 == 0:
            w = a_w_in[j]
            def regroup(t):
                rows = t.shape[0]
                return jnp.concatenate(
                    [t[:, :o2], t[:, o4:], t[:, o2:o3], jnp.zeros((rows, LANE - heads), t.dtype),
                     t[:, o3:o4], jnp.zeros((rows, n_a - col_a["a"] - heads), t.dtype)], axis=1)

            assert col_a["b"] == o2 + xw
            a_log, dt_b = _pad_lanes(a_a_log[j]), _pad_lanes(a_dt_bias[j])
            o_norm = a_o_norm[j].reshape(1, dv)

            proj_p = rms_matmul(hp, a_norm1[j], regroup(w.astype(BF16)), ROWS_IN_A, f"a_in_p_{i}")
            proj_s = regroup(rms_matmul(hs, a_norm1[j], w, bs, f"a_in_s_{i}"))
            mix_p, s_p = deltanet_prompt(proj_p, col_a, a_conv_w[j], a_log, dt_b, o_norm,
                                         bp, seq, heads, dk, dv, rows=DN_STEP_CHUNKS * CHUNK)
            mix_s, s_s, c_s = deltanet_sample(proj_s, col_a, state_dn_conv[j], state_dn[j], a_conv_w[j],
                                              a_log, dt_b, o_norm, heads, dk, dv, nb=SUBLANE)
            xa_p = xattn_prompt(proj_p, col_a["cq"], kv, bp, seq, n_mem, x_heads, x_dh, tq=min(ROWS_XATTN, seq))
            xa_s = xattn_sample(proj_s, col_a["cq"], cache_mem_k, cache_mem_v, i, nb=SUBLANE)
            hp = out_proj_ffn(hp, mix_p, xa_p, a_w_out[j].astype(BF16), a_norm2[j], a_ffn_wi[j].astype(BF16),
                              a_ffn_wo[j].astype(BF16), ROWS_OUT_FFN, f"a_out_ffn_p_{i}")
            hs = out_proj(hs, mix_s, xa_s, a_w_out[j], bs, f"a_out_s_{i}")
            hs = ffn_dense(hs, a_norm2[j], a_ffn_wi[j], a_ffn_wo[j], bs, f"a_ffn_s_{i}", tf=FF_TILE_SAMPLE)
            dn_p.append(s_p)
            conv_p.append(proj_p.reshape(bp, seq, n_a)[:, seq - (CONV_WIDTH - 1):, :conv_ch])
            dn_s.append(s_s)
            conv_s.append(c_s)
        else:
            ps = b_pool_scale[j].reshape(1, pool_w_total)
            last = i == depth - 1

            mix_p, cq_p, pool_tail = proj_pool_prompt(hp, b_norm1[j], b_w_in[j].astype(BF16),
                                                      b_pool_w[j].astype(BF16), ps, bp, seq, tt=ROWS_STREAM)
            proj_s = rms_matmul(hs, b_norm1[j], b_w_in[j], bs, f"b_in_s_{i}")
            mix_s, buf_s = pool_sample(proj_s, state_pool[j], b_pool_w[j], ps)
            xa_p = xattn_prompt(cq_p, 0, kv, bp, seq, n_mem, x_heads, x_dh, tq=min(ROWS_XATTN, seq))
            xa_s = xattn_sample(proj_s, pool_w_total, cache_mem_k, cache_mem_v, i, nb=SUBLANE)
            n_p = bp * seq
            n_experts = b_router.shape[-1]
            r_pad = jnp.concatenate([b_router[j], jnp.zeros((d, LANE - n_experts), F32)], axis=1)
            h_all, idx_p, gate_p = out_proj(hp, mix_p, xa_p, b_w_out[j].astype(BF16), ROWS_STREAM, f"b_out_p_{i}",
                                            out_rows=n_p + bs, route=(b_norm2[j], r_pad.astype(BF16), n_experts))
            h_all, idx_s, gate_s = out_proj(hs, mix_s, xa_s, b_w_out[j], bs, f"b_out_s_{i}", out_rows=n_p + bs,
                                            into=h_all, row_offset=n_p, route=(b_norm2[j], r_pad, n_experts))
            idx = jnp.concatenate([idx_p[:n_p, :TOP_K], idx_s[:, :TOP_K]], axis=0)
            (dest, gather_rows, scatter_rows, tile_expert, n_used, n_blocks_used, tile_fill, n_rows) = moe_plan(
                idx, n_experts, MOE_TILE, MOE_ROW_BLOCK)
            h_ps = jnp.zeros((n_rows, d + LANE), F32)
            h_ps = moe_dispatch(h_all, 0, n_p, gate_p, dest[:n_p], h_ps, ROWS_STREAM, f"b_dispatch_p_{i}")
            h_ps = moe_dispatch(h_all, n_p, bs, gate_s, dest[n_p:], h_ps, bs, f"b_dispatch_s_{i}")
            y = moe_experts(h_ps, b_norm2[j], b_moe_wi[j], b_moe_wo[j], gather_rows, scatter_rows, tile_expert,
                            n_used, n_blocks_used, tile_fill, MOE_ROW_BLOCK, n_rows, MOE_TILE, MOE_FF_TILE)
            y_ps = moe_combine(h_ps, y, n_blocks_used, n_rows, final_norm, MOE_ROW_BLOCK, last,
                               f"b_combine_{i}")
            hp = moe_unpermute(y_ps, dest[:n_p], ROWS_STREAM, f"b_unpermute_p_{i}")
            hs = moe_unpermute(y_ps, dest[n_p:], bs, f"b_unpermute_s_{i}")
            pool_p.append(pool_tail[:, POOL_HALO - pool_buf:, :])
            pool_s.append(buf_s)

    y_prompt = hp.reshape(bp, seq, d)
    y_sample = hs.reshape(bs, 1, d)
    return (y_prompt, y_sample, jnp.stack(dn_p), jnp.stack(conv_p), jnp.stack(pool_p),
            mk_all.reshape(depth, bp, n_mem, x_heads, x_dh), mv_all.reshape(depth, bp, n_mem, x_heads, x_dh),
            jnp.stack(dn_s), jnp.stack(conv_s), jnp.stack(pool_s))
```

```python
import functools

import jax
import jax.numpy as jnp
from jax import lax
from jax.experimental import pallas as pl
from jax.experimental.pallas import tpu as pltpu

F32 = jnp.float32
BF16 = jnp.bfloat16
HIGHEST = lax.Precision.HIGHEST

EPS = 1e-6
CHUNK = 64
CONV_WIDTH = 4
POOL_WINDOWS = (2, 4, 8, 16)
PAST_LEN = 16384
TOP_K = 2

LANE = 128
SUBLANE = 8
VMEM_LIMIT_BYTES = 56 * 2**20

ROWS_IN_A = 512
ROWS_OUT_FFN = 512
ROWS_STREAM = 512
ROWS_MEM = 256
ROWS_XATTN = 2048
FF_TILE_SAMPLE = 256
MOE_TILE = 1024
MOE_FF_TILE = 512
MOE_ROW_BLOCK = 512
DN_STEP_CHUNKS = 8
INV_BLOCK = 16
POOL_HALO = 32


def _params(*sem):
    return pltpu.CompilerParams(dimension_semantics=sem, vmem_limit_bytes=VMEM_LIMIT_BYTES)


def _resident(shape):
    nd = len(shape)
    return pl.BlockSpec(shape, lambda *_: (0,) * nd, pipeline_mode=pl.Buffered(1))


def _rms(x, w):
    return x * lax.rsqrt(jnp.mean(x * x, axis=-1, keepdims=True) + EPS) * w


def _silu(x):
    half = 0.5 * x
    return half + half * jnp.tanh(half)


def _dot(a, b):
    return jnp.dot(a, b, preferred_element_type=F32)


def _wdot(a, w):
    if w.dtype == F32:
        return jnp.dot(a, w, precision=HIGHEST, preferred_element_type=F32)
    return jnp.dot(a.astype(BF16), w, preferred_element_type=F32)


def _bdot(a, b):
    return lax.dot_general(a, b, (((2,), (1,)), ((0,), (0,))), preferred_element_type=F32)


def _bdot_nt(a, b):
    return lax.dot_general(a, b, (((2,), (2,)), ((0,), (0,))), preferred_element_type=F32)


def _bdot_tn(a, b):
    return lax.dot_general(a, b, (((1,), (1,)), ((0,), (0,))), preferred_element_type=F32)


def _rms_matmul_body(x_ref, g_ref, w_ref, o_ref):
    xn = _rms(x_ref[...], g_ref[...])
    o_ref[...] = _wdot(xn, w_ref[...])


def _wspec(shape, index_map, steps):
    if steps == 1:
        return pl.BlockSpec(shape, index_map, pipeline_mode=pl.Buffered(1))
    return pl.BlockSpec(shape, index_map)


def rms_matmul(x, g, w, tm, name, tn=None):
    m, d = x.shape
    n = w.shape[1]
    tn = n if tn is None else tn
    nj = n // tn
    return pl.pallas_call(
        _rms_matmul_body,
        out_shape=jax.ShapeDtypeStruct((m, n), F32),
        grid=(m // tm, nj),
        in_specs=[pl.BlockSpec((tm, d), lambda i, j: (i, 0)), _resident((1, d)),
                  _wspec((d, tn), lambda i, j: (0, j), nj)],
        out_specs=pl.BlockSpec((tm, tn), lambda i, j: (i, j)),
        compiler_params=_params("parallel", "arbitrary"),
        name=name,
    )(x, g.reshape(1, d), w)


def _out_proj_body(x_ref, a1_ref, a2_ref, w1_ref, w2_ref, *rest, n_main, n_experts, has_into):
    route = n_experts is not None
    if route:
        g_ref, r_ref = rest[0], rest[1]
        o_ref, idx_ref, gate_ref = rest[-3:]
    else:
        o_ref = rest[-1]
    del has_into

    @pl.when(pl.program_id(0) < n_main)
    def _():
        acc = _wdot(a1_ref[...], w1_ref[...]) + _wdot(a2_ref[...], w2_ref[...])
        h = x_ref[...] + acc
        o_ref[...] = h
        if route:
            idx_ref[...], gate_ref[...] = _route_top2(h, g_ref[...], r_ref[...], n_experts)

    @pl.when(pl.program_id(0) >= n_main)
    def _():
        o_ref[...] = jnp.zeros_like(o_ref)
        if route:
            idx_ref[...] = jnp.zeros_like(idx_ref)
            gate_ref[...] = jnp.zeros_like(gate_ref)


def out_proj(x, a1, a2, w, tm, name, out_rows=None, into=None, row_offset=0, route=None):
    m, d = x.shape
    k1, k2 = a1.shape[1], a2.shape[1]
    assert w.shape[0] == k1 + k2 and k1 % k2 == 0 and row_offset % tm == 0
    out_rows = m if out_rows is None else out_rows
    off = row_offset // tm
    n_main = m // tm
    n_extra = 0 if (into is not None or out_rows == m) else 1
    assert out_rows - m <= n_extra * tm or into is not None
    last = n_main - 1
    in_specs = [
        pl.BlockSpec((tm, d), lambda i: (jnp.minimum(i, last), 0)),
        pl.BlockSpec((tm, k1), lambda i: (jnp.minimum(i, last), 0)),
        pl.BlockSpec((tm, k2), lambda i: (jnp.minimum(i, last), 0)),
        pl.BlockSpec((k1, d), lambda i: (0, 0), pipeline_mode=pl.Buffered(1)),
        pl.BlockSpec((k2, d), lambda i: (k1 // k2, 0), pipeline_mode=pl.Buffered(1)),
    ]
    args = [x, a1, a2, w, w]
    out_shape = [jax.ShapeDtypeStruct((out_rows, d), F32)]
    out_specs = [pl.BlockSpec((tm, d), lambda i: (i + off, 0))]
    n_experts = None
    if route is not None:
        g, r_pad, n_experts = route
        in_specs += [_resident((1, d)), _resident((d, LANE))]
        args += [g.reshape(1, d), r_pad]
        n_steps = n_main + n_extra
        out_shape += [jax.ShapeDtypeStruct((n_steps * tm, LANE), jnp.int32),
                      jax.ShapeDtypeStruct((n_steps * tm, LANE), F32)]
        out_specs += [pl.BlockSpec((tm, LANE), lambda i: (i, 0))] * 2
    aliases = {}
    if into is not None:
        in_specs.append(pl.BlockSpec(memory_space=pl.ANY))
        args.append(into)
        aliases = {len(args) - 1: 0}
    outs = pl.pallas_call(
        functools.partial(_out_proj_body, n_main=n_main, n_experts=n_experts, has_into=into is not None),
        out_shape=tuple(out_shape),
        grid=(n_main + n_extra,),
        in_specs=in_specs,
        out_specs=tuple(out_specs),
        input_output_aliases=aliases,
        compiler_params=_params("parallel"),
        name=name,
    )(*args)
    return outs if route is not None else outs[0]


def _ffn_body(h_ref, g_ref, wg_ref, wu_ref, wo_ref, o_ref, xn_ref):
    @pl.when(pl.program_id(1) == 0)
    def _():
        h = h_ref[...]
        xn_ref[...] = _rms(h, g_ref[...])
        o_ref[...] = h

    xn = xn_ref[...]
    a = _wdot(xn, wg_ref[...])
    b = _wdot(xn, wu_ref[...])
    o_ref[...] += _wdot(_silu(a) * b, wo_ref[...])


def ffn_dense(h, g, wi, wo, tm, name, tf=None):
    m, d = h.shape
    f = wo.shape[0]
    tf = f if tf is None else tf
    nj = f // tf
    return pl.pallas_call(
        _ffn_body,
        out_shape=jax.ShapeDtypeStruct((m, d), F32),
        grid=(m // tm, nj),
        in_specs=[
            pl.BlockSpec((tm, d), lambda i, j: (i, 0)),
            _resident((1, d)),
            _wspec((d, tf), lambda i, j: (0, j), nj),
            _wspec((d, tf), lambda i, j: (0, nj + j), nj),
            _wspec((tf, d), lambda i, j: (j, 0), nj),
        ],
        out_specs=pl.BlockSpec((tm, d), lambda i, j: (i, 0)),
        scratch_shapes=[pltpu.VMEM((tm, d), F32)],
        compiler_params=_params("parallel", "arbitrary"),
        name=name,
    )(h, g.reshape(1, d), wi, wi, wo)


def _mem_kv_body(x_ref, g_ref, w_ref, *rest, heads, dh, n_own):
    kv_ref, k_ref, v_ref = rest[-3:]
    xw = heads * dh

    @pl.when(pl.program_id(0) < n_own)
    def _():
        res = _wdot(_rms(x_ref[...], g_ref[...]), w_ref[...])
        kv_ref[...] = res
        for h in range(heads):
            k_ref[:, h, :] = res[:, h * dh:(h + 1) * dh]
            v_ref[:, h, :] = res[:, xw + h * dh:xw + (h + 1) * dh]

    @pl.when(pl.program_id(0) >= n_own)
    def _():
        k_ref[...] = jnp.zeros_like(k_ref)
        v_ref[...] = jnp.zeros_like(v_ref)


def mem_kv(mem, g, w_kv, heads, dh, tm, name, layer, n_layers, into=None):
    m, d = mem.shape
    xw = heads * dh
    n_own = m // tm
    assert (into is None) == (layer == 0)
    n_steps = n_own * n_layers if into is None else n_own
    head_shape = jax.ShapeDtypeStruct((n_layers * m, heads, dh), F32)
    head_spec = pl.BlockSpec((tm, heads, dh), lambda i: (i + layer * n_own, 0, 0))
    own = lambda i: (jnp.minimum(i, n_own - 1), 0)
    in_specs = [pl.BlockSpec((tm, d), own), _resident((1, d)), _resident((d, 2 * xw))]
    args = [mem, g.reshape(1, d), w_kv]
    aliases = {}
    if into is not None:
        in_specs += [pl.BlockSpec(memory_space=pl.ANY)] * 2
        args += list(into)
        aliases = {3: 1, 4: 2}
    return pl.pallas_call(
        functools.partial(_mem_kv_body, heads=heads, dh=dh, n_own=n_own),
        out_shape=(jax.ShapeDtypeStruct((m, 2 * xw), F32), head_shape, head_shape),
        grid=(n_steps,),
        in_specs=in_specs,
        out_specs=(pl.BlockSpec((tm, 2 * xw), own), head_spec, head_spec),
        input_output_aliases=aliases,
        compiler_params=_params("arbitrary"),
        name=name,
    )(*args)


def _out_proj_ffn_body(x_ref, a1_ref, a2_ref, w1_ref, w2_ref, g_ref, wg_ref, wu_ref, wo_ref, o_ref):
    h = x_ref[...] + (_wdot(a1_ref[...], w1_ref[...]) + _wdot(a2_ref[...], w2_ref[...]))
    xn = _rms(h, g_ref[...])
    a = _wdot(xn, wg_ref[...])
    b = _wdot(xn, wu_ref[...])
    o_ref[...] = h + _wdot(_silu(a) * b, wo_ref[...])


def out_proj_ffn(x, a1, a2, w, g, wi, wo, tm, name):
    m, d = x.shape
    k1, k2 = a1.shape[1], a2.shape[1]
    f = wo.shape[0]
    assert w.shape[0] == k1 + k2 and k1 % k2 == 0
    return pl.pallas_call(
        _out_proj_ffn_body,
        out_shape=jax.ShapeDtypeStruct((m, d), F32),
        grid=(m // tm,),
        in_specs=[
            pl.BlockSpec((tm, d), lambda i: (i, 0)),
            pl.BlockSpec((tm, k1), lambda i: (i, 0)),
            pl.BlockSpec((tm, k2), lambda i: (i, 0)),
            pl.BlockSpec((k1, d), lambda i: (0, 0), pipeline_mode=pl.Buffered(1)),
            pl.BlockSpec((k2, d), lambda i: (k1 // k2, 0), pipeline_mode=pl.Buffered(1)),
            _resident((1, d)),
            pl.BlockSpec((d, f), lambda i: (0, 0), pipeline_mode=pl.Buffered(1)),
            pl.BlockSpec((d, f), lambda i: (0, 1), pipeline_mode=pl.Buffered(1)),
            _resident((f, d)),
        ],
        out_specs=pl.BlockSpec((tm, d), lambda i: (i, 0)),
        compiler_params=_params("parallel"),
        name=name,
    )(x, a1, a2, w, w, g.reshape(1, d), wi, wi, wo)


def _pair_blockdiag(y, left):
    yb = y.astype(BF16)
    zero = jnp.zeros_like(yb)
    return jnp.concatenate([jnp.where(left, yb, zero), jnp.where(left, zero, yb)], axis=1)


def _pair_mm(x, y, left):
    return _bdot(x.astype(BF16), _pair_blockdiag(y, left))


def _unit_lower_inverse(mx, eye, blk, left):
    mm = functools.partial(_pair_mm, left=left)
    md = jnp.where(blk, mx, 0.0)
    c = mx - md
    p = eye - md
    m2 = mm(md, md)
    p = p + mm(p, m2)
    m4 = mm(m2, m2)
    p = p + mm(p, m4)
    m8 = mm(m4, m4)
    td = p + mm(p, m8)
    n = mm(td, c)
    n2 = mm(n, n)
    q = eye - n
    q = q + mm(q, n2)
    return mm(q, td)


def _dn_prompt_body(qkv_ref, z_ref, bl_ref, al_ref, cw_ref, alog_ref, dtb_ref, on_ref,
                    dn_ref, sfin_ref, s_ref, xbuf_ref, act_ref, *, rows, heads, dk, dv):
    c = pl.program_id(1)
    wq = heads * dk

    @pl.when(c == 0)
    def _():
        s_ref[...] = jnp.zeros_like(s_ref)
        xbuf_ref[0:SUBLANE, :] = jnp.zeros((SUBLANE, xbuf_ref.shape[1]), F32)

    xbuf_ref[SUBLANE:SUBLANE + rows, :] = qkv_ref[...]
    first = SUBLANE - (CONV_WIDTH - 1)
    y = xbuf_ref[pl.ds(first, rows), :] * cw_ref[0:1, :]
    for j in range(1, CONV_WIDTH):
        y = y + xbuf_ref[pl.ds(first + j, rows), :] * cw_ref[j:j + 1, :]
    act_ref[...] = _silu(y)
    xbuf_ref[0:SUBLANE, :] = xbuf_ref[rows:rows + SUBLANE, :]

    pairs = heads // 2
    ri = lax.broadcasted_iota(jnp.int32, (CHUNK, 2 * CHUNK), 0)
    lane = lax.broadcasted_iota(jnp.int32, (CHUNK, 2 * CHUNK), 1)
    cj = lane % CHUNK
    left = (lane < CHUNK)[None]
    incl = (ri >= cj)[None]
    strict = (ri > cj)[None]
    blk = ((ri // INV_BLOCK) == (cj // INV_BLOCK))[None]
    eye = (ri == cj).astype(F32)[None]
    tr = lax.broadcasted_iota(jnp.int32, (CHUNK, CHUNK), 0)
    tc = lax.broadcasted_iota(jnp.int32, (CHUNK, CHUNK), 1)
    tri = (tr >= tc).astype(F32)

    n_chunks = rows // CHUNK
    zero = jnp.zeros((CHUNK, dk), BF16)

    def chunk_rows(cc):
        return slice(cc * CHUNK, (cc + 1) * CHUNK)

    def per_head(fn):
        return jnp.stack([fn(chunk_rows(cc), h) for cc in range(n_chunks) for h in range(heads)], axis=0)

    def per_pair(fn):
        return jnp.stack([fn(cc, p) for cc in range(n_chunks) for p in range(pairs)], axis=0)

    q3 = per_head(lambda rs, h: act_ref[rs, h * dk:(h + 1) * dk])
    k3 = per_head(lambda rs, h: act_ref[rs, wq + h * dk:wq + (h + 1) * dk])
    v3 = per_head(lambda rs, h: act_ref[rs, 2 * wq + h * dv:2 * wq + (h + 1) * dv])
    qn = q3 * lax.rsqrt(jnp.sum(q3 * q3, axis=-1, keepdims=True) + EPS) * (dk ** -0.5)
    kn = k3 * lax.rsqrt(jnp.sum(k3 * k3, axis=-1, keepdims=True) + EPS)

    beta = jax.nn.sigmoid(bl_ref[...])
    log_decay = -jnp.exp(alog_ref[...]) * jax.nn.softplus(al_ref[...] + dtb_ref[...])
    gcum = [jnp.dot(tri, log_decay[chunk_rows(cc)], precision=HIGHEST, preferred_element_type=F32)
            for cc in range(n_chunks)]
    gcum_t = [jnp.concatenate([gc_, pltpu.roll(gc_, LANE - 1, 1)], axis=0).T for gc_ in gcum]

    def pair_cols(x_of):
        return per_pair(lambda cc, p: jnp.where(left[0], x_of(cc)[:, 2 * p:2 * p + 1],
                                                x_of(cc)[:, 2 * p + 1:2 * p + 2]))

    gc2 = pair_cols(lambda cc: gcum[cc])
    bc2 = pair_cols(lambda cc: beta[chunk_rows(cc)])
    gr2 = per_pair(lambda cc, p: gcum_t[cc][2 * p:2 * p + 1, :])
    decay = jnp.where(incl, jnp.exp(gc2 - gr2), 0.0)

    kb = kn.astype(BF16)
    qb = qn.astype(BF16)

    def pair_of(x, cc, p):
        return x[cc * heads + 2 * p], x[cc * heads + 2 * p + 1]

    k_lhs = per_pair(lambda cc, p: jnp.concatenate(pair_of(kb, cc, p), axis=1))
    q_lhs = per_pair(lambda cc, p: jnp.concatenate(pair_of(qb, cc, p), axis=1))
    k_bd = per_pair(lambda cc, p: jnp.concatenate(
        [jnp.concatenate([pair_of(kb, cc, p)[0], zero], axis=1),
         jnp.concatenate([zero, pair_of(kb, cc, p)[1]], axis=1)], axis=0))
    kk = _bdot_nt(k_lhs, k_bd)
    mx = jnp.where(strict, bc2 * kk * decay, 0.0)
    tinv = _unit_lower_inverse(mx, eye, blk, left)

    gcum_rows = jnp.concatenate(gcum, axis=0)
    gc = per_head(lambda rs, h: gcum_rows[rs, h:h + 1])
    bc = per_head(lambda rs, h: beta[rs, h:h + 1])
    eg = jnp.exp(gc)
    rhs = jnp.concatenate([v3 * bc, kn * (bc * eg)], axis=-1)
    sol = _bdot(_pair_blockdiag(tinv, left), rhs.astype(BF16).reshape(n_chunks * pairs, 2 * CHUNK, dv + dk))
    sol = sol.reshape(n_chunks * heads, CHUNK, dv + dk)
    w_val = sol[..., :dv]
    k_cd = sol[..., dv:].astype(BF16)
    qk_bd = _pair_blockdiag(_bdot_nt(q_lhs, k_bd) * decay, left)
    q_dec = (qn * eg).astype(BF16)
    g_last = gc[:, CHUNK - 1:CHUNK, :]
    k_tail = (kn * jnp.exp(g_last - gc)).astype(BF16)
    c_decay = jnp.exp(g_last)

    for cc in range(n_chunks):
        hs = slice(cc * heads, (cc + 1) * heads)
        ps = slice(cc * pairs, (cc + 1) * pairs)
        rs = chunk_rows(cc)
        s = s_ref[...]
        sb = s.astype(BF16)
        u = w_val[hs] - _bdot(k_cd[hs], sb)
        ub = u.astype(BF16)
        intra = _bdot(qk_bd[ps], ub.reshape(pairs, 2 * CHUNK, dv))
        o = _bdot(q_dec[hs], sb) + intra.reshape(heads, CHUNK, dv)
        s_ref[...] = s * c_decay[hs] + _bdot_tn(k_tail[hs], ub)

        o = o * lax.rsqrt(jnp.mean(o * o, axis=-1, keepdims=True) + EPS) * on_ref[...]
        for h in range(heads):
            z = z_ref[rs, h * dv:(h + 1) * dv]
            dn_ref[rs, h * dv:(h + 1) * dv] = o[h] * _silu(z)

    @pl.when(c == pl.num_programs(1) - 1)
    def _():
        sfin_ref[0] = s_ref[...]


def deltanet_prompt(proj, col, conv_w, a_log, dt_bias, o_norm, batch, seq, heads, dk, dv, rows):
    conv_ch = 2 * heads * dk + heads * dv
    zw = heads * dv
    steps = seq // rows

    def row_map(width_off):
        return lambda b, c: (b * steps + c, width_off)

    body = functools.partial(_dn_prompt_body, rows=rows, heads=heads, dk=dk, dv=dv)
    return pl.pallas_call(
        body,
        out_shape=(jax.ShapeDtypeStruct((batch * seq, zw), F32),
                   jax.ShapeDtypeStruct((batch, heads, dk, dv), F32)),
        grid=(batch, steps),
        in_specs=[
            pl.BlockSpec((rows, conv_ch), row_map(col["qkv"] // conv_ch)),
            pl.BlockSpec((rows, zw), row_map(col["z"] // zw)),
            pl.BlockSpec((rows, LANE), row_map(col["b"] // LANE)),
            pl.BlockSpec((rows, LANE), row_map(col["a"] // LANE)),
            _resident((CONV_WIDTH, conv_ch)),
            _resident((1, LANE)),
            _resident((1, LANE)),
            _resident((1, dv)),
        ],
        out_specs=(pl.BlockSpec((rows, zw), lambda b, c: (b * steps + c, 0)),
                   pl.BlockSpec((1, heads, dk, dv), lambda b, c: (b, 0, 0, 0))),
        scratch_shapes=[
            pltpu.VMEM((heads, dk, dv), F32),
            pltpu.VMEM((rows + SUBLANE, conv_ch), F32),
            pltpu.VMEM((rows, conv_ch), F32),
        ],
        compiler_params=_params("parallel", "arbitrary"),
        name="deltanet_prompt",
    )(proj, proj, proj, proj, conv_w, a_log, dt_bias, o_norm)


def _dn_sample_body(qkv_ref, z_ref, bl_ref, al_ref, cbuf_ref, st_ref, cw_ref, alog_ref, dtb_ref, on_ref,
                    dn_ref, sto_ref, cbo_ref, act_ref, *, nb, heads, dk, dv):
    wq = heads * dk
    new = qkv_ref[...]
    y = cbuf_ref[:, 0, :] * cw_ref[0:1, :]
    for j in range(1, CONV_WIDTH - 1):
        y = y + cbuf_ref[:, j, :] * cw_ref[j:j + 1, :]
    y = y + new * cw_ref[CONV_WIDTH - 1:CONV_WIDTH, :]
    act_ref[...] = _silu(y)
    for j in range(CONV_WIDTH - 2):
        cbo_ref[:, j, :] = cbuf_ref[:, j + 1, :]
    cbo_ref[:, CONV_WIDTH - 2, :] = new

    beta = jax.nn.sigmoid(bl_ref[...])
    g = -jnp.exp(alog_ref[...]) * jax.nn.softplus(al_ref[...] + dtb_ref[...])
    dec = jnp.exp(g)

    q_t, k_t = [], []
    for h in range(heads):
        q = act_ref[:, h * dk:(h + 1) * dk]
        k = act_ref[:, wq + h * dk:wq + (h + 1) * dk]
        qn = q * lax.rsqrt(jnp.sum(q * q, axis=-1, keepdims=True) + EPS) * (dk ** -0.5)
        kn = k * lax.rsqrt(jnp.sum(k * k, axis=-1, keepdims=True) + EPS)
        q_t.append(qn.T)
        k_t.append(kn.T)

    def per_head(fn):
        return jnp.stack([fn(h) for h in range(heads)], axis=0)

    for i in range(nb):
        row = slice(i, i + 1)
        kcol = per_head(lambda h: k_t[h][:, i:i + 1])
        qcol = per_head(lambda h: q_t[h][:, i:i + 1])
        v = per_head(lambda h: act_ref[row, 2 * wq + h * dv:2 * wq + (h + 1) * dv])
        z = per_head(lambda h: z_ref[row, h * dv:(h + 1) * dv])
        b_i = per_head(lambda h: beta[row, h:h + 1])
        d_i = per_head(lambda h: dec[row, h:h + 1])
        s = st_ref[i] * d_i
        u = b_i * (v - jnp.sum(kcol * s, axis=1, keepdims=True))
        s = s + kcol * u
        o = jnp.sum(qcol * s, axis=1, keepdims=True)
        sto_ref[i] = s
        o = o * lax.rsqrt(jnp.mean(o * o, axis=-1, keepdims=True) + EPS) * on_ref[...]
        o = o * _silu(z)
        for h in range(heads):
            dn_ref[row, h * dv:(h + 1) * dv] = o[h]


def deltanet_sample(proj, col, conv_buf, state, conv_w, a_log, dt_bias, o_norm, heads, dk, dv, nb):
    n_seq = proj.shape[0]
    conv_ch = 2 * heads * dk + heads * dv
    zw = heads * dv
    body = functools.partial(_dn_sample_body, nb=nb, heads=heads, dk=dk, dv=dv)
    return pl.pallas_call(
        body,
        out_shape=(jax.ShapeDtypeStruct((n_seq, zw), F32),
                   jax.ShapeDtypeStruct(state.shape, F32),
                   jax.ShapeDtypeStruct(conv_buf.shape, F32)),
        grid=(n_seq // nb,),
        in_specs=[
            pl.BlockSpec((nb, conv_ch), lambda i: (i, col["qkv"] // conv_ch)),
            pl.BlockSpec((nb, zw), lambda i: (i, col["z"] // zw)),
            pl.BlockSpec((nb, LANE), lambda i: (i, col["b"] // LANE)),
            pl.BlockSpec((nb, LANE), lambda i: (i, col["a"] // LANE)),
            pl.BlockSpec((nb, CONV_WIDTH - 1, conv_ch), lambda i: (i, 0, 0)),
            pl.BlockSpec((nb, heads, dk, dv), lambda i: (i, 0, 0, 0)),
            _resident((CONV_WIDTH, conv_ch)),
            _resident((1, LANE)),
            _resident((1, LANE)),
            _resident((1, dv)),
        ],
        out_specs=(pl.BlockSpec((nb, zw), lambda i: (i, 0)),
                   pl.BlockSpec((nb, heads, dk, dv), lambda i: (i, 0, 0, 0)),
                   pl.BlockSpec((nb, CONV_WIDTH - 1, conv_ch), lambda i: (i, 0, 0))),
        scratch_shapes=[
            pltpu.VMEM((nb, conv_ch), F32),
        ],
        compiler_params=_params("parallel"),
        name="deltanet_sample",
    )(proj, proj, proj, proj, conv_buf, state, conv_w, a_log, dt_bias, o_norm)


def _xattn_prompt_body(q_ref, k_ref, v_ref, o_ref, *, heads, dh):
    for h in range(heads):
        sl = slice(h * dh, (h + 1) * dh)
        q = q_ref[:, sl].astype(BF16)
        k = k_ref[:, sl].astype(BF16)
        v = v_ref[:, sl].astype(BF16)
        s = lax.dot_general(q, k, (((1,), (1,)), ((), ())), preferred_element_type=F32) * (dh ** -0.5)
        e = jnp.exp(s - jnp.max(s, axis=-1, keepdims=True))
        p = e / jnp.sum(e, axis=-1, keepdims=True)
        o_ref[:, sl] = _dot(p.astype(BF16), v)


def xattn_prompt(proj, cq_col, kv, batch, seq, n_mem, heads, dh, tq):
    xw = heads * dh
    steps = seq // tq
    body = functools.partial(_xattn_prompt_body, heads=heads, dh=dh)
    return pl.pallas_call(
        body,
        out_shape=jax.ShapeDtypeStruct((batch * seq, xw), F32),
        grid=(batch, steps),
        in_specs=[
            pl.BlockSpec((tq, xw), lambda b, i: (b * steps + i, cq_col // xw)),
            pl.BlockSpec((n_mem, xw), lambda b, i: (b, 0)),
            pl.BlockSpec((n_mem, xw), lambda b, i: (b, 1)),
        ],
        out_specs=pl.BlockSpec((tq, xw), lambda b, i: (b * steps + i, 0)),
        compiler_params=_params("parallel", "parallel"),
        name="xattn_prompt",
    )(proj, kv, kv)


def _xattn_sample_body(q_ref, k_ref, v_ref, o_ref, *, nb, heads, dh):
    rep = SUBLANE // heads
    n_t = k_ref.shape[2] // SUBLANE

    def over_groups(x, op):
        out = x
        for t in range(1, rep):
            out = op(out, pltpu.roll(x, t * heads, 0))
        return out

    for i in range(nb):
        row = slice(i, i + 1)
        q8 = jnp.concatenate([q_ref[row, h * dh:(h + 1) * dh] for h in range(heads)] * rep, axis=0)
        k3 = k_ref[0, i].reshape(n_t, SUBLANE, dh)
        v3 = v_ref[0, i].reshape(n_t, SUBLANE, dh)
        s = jnp.sum(k3 * q8[None], axis=-1, keepdims=True) * (dh ** -0.5)
        m = over_groups(jnp.max(s, axis=0), jnp.maximum)
        e = jnp.exp(s - m[None])
        den = over_groups(jnp.sum(e, axis=0), jnp.add)
        p = e / den[None]
        acc = over_groups(jnp.sum(p * v3, axis=0), jnp.add)
        for h in range(heads):
            o_ref[row, h * dh:(h + 1) * dh] = acc[h:h + 1, :]


def xattn_sample(proj, cq_col, cache_k, cache_v, layer, nb):
    n_layers, n_seq, n_mem, heads, dh = cache_k.shape
    assert SUBLANE % heads == 0 and (n_mem * heads) % SUBLANE == 0
    xw = heads * dh
    body = functools.partial(_xattn_sample_body, nb=nb, heads=heads, dh=dh)
    cache_spec = pl.BlockSpec((1, nb, n_mem * heads, dh), lambda i: (layer, i, 0, 0))
    flat = (n_layers, n_seq, n_mem * heads, dh)
    return pl.pallas_call(
        body,
        out_shape=jax.ShapeDtypeStruct((n_seq, xw), F32),
        grid=(n_seq // nb,),
        in_specs=[pl.BlockSpec((nb, xw), lambda i: (i, cq_col // xw)), cache_spec, cache_spec],
        out_specs=pl.BlockSpec((nb, xw), lambda i: (i, 0)),
        compiler_params=_params("parallel"),
        name="xattn_sample",
    )(proj, cache_k.reshape(flat), cache_v.reshape(flat))


def _pool_mix(sums, cnts, u_of, pw_ref, ps_ref, o_ref, gw):
    for gi in range(len(POOL_WINDOWS)):
        cols = slice(gi * gw, (gi + 1) * gw)
        d = sums[gi] / cnts[gi] - u_of(cols)
        o_ref[:, cols] = _wdot(d, pw_ref[gi]) * ps_ref[:, cols]


def _pool_prompt_body(x_ref, g_ref, w_ref, pw_ref, ps_ref, o_ref, cq_ref, tail_ref, xbuf, s2, s4, s8, *, tt, gw):
    t = pl.program_id(1)
    n = POOL_HALO + tt
    pw_total = xbuf.shape[1]

    @pl.when(t == 0)
    def _():
        xbuf[0:POOL_HALO, :] = jnp.zeros((POOL_HALO, pw_total), F32)

    proj = _wdot(_rms(x_ref[...], g_ref[...]), w_ref[...])
    cq_ref[...] = proj[:, pw_total:]
    xbuf[POOL_HALO:n, :] = proj[:, 0:pw_total]
    s2[8:n, :] = xbuf[8:n, :] + xbuf[7:n - 1, :]
    s4[16:n, :] = s2[16:n, gw:] + s2[14:n - 2, gw:]
    s8[24:n, :] = s4[24:n, gw:] + s4[20:n - 4, gw:]
    s16 = s8[32:n, gw:] + s8[24:n - 8, gw:]
    sums = [s2[POOL_HALO:n, 0:gw], s4[POOL_HALO:n, 0:gw], s8[POOL_HALO:n, 0:gw], s16]
    pos = t * tt + lax.broadcasted_iota(jnp.int32, (tt, 1), 0)
    cnts = [jnp.minimum(pos + 1, w).astype(F32) for w in POOL_WINDOWS]
    _pool_mix(sums, cnts, lambda cols: xbuf[POOL_HALO:n, cols], pw_ref, ps_ref, o_ref, gw)
    tail_ref[0] = xbuf[tt:n, :]
    xbuf[0:POOL_HALO, :] = xbuf[tt:n, :]


def proj_pool_prompt(x, g, w, pool_w, pool_scale, batch, seq, tt):
    m, d = x.shape
    n_out = w.shape[1]
    pw_total = pool_scale.shape[-1]
    gw = pw_total // len(POOL_WINDOWS)
    steps = seq // tt
    n = POOL_HALO + tt
    body = functools.partial(_pool_prompt_body, tt=tt, gw=gw)
    return pl.pallas_call(
        body,
        out_shape=(jax.ShapeDtypeStruct((m, pw_total), F32),
                   jax.ShapeDtypeStruct((m, n_out - pw_total), F32),
                   jax.ShapeDtypeStruct((batch, POOL_HALO, pw_total), F32)),
        grid=(batch, steps),
        in_specs=[
            pl.BlockSpec((tt, d), lambda b, t: (b * steps + t, 0)),
            _resident((1, d)),
            _resident((d, n_out)),
            _resident(pool_w.shape),
            _resident((1, pw_total)),
        ],
        out_specs=(pl.BlockSpec((tt, pw_total), lambda b, t: (b * steps + t, 0)),
                   pl.BlockSpec((tt, n_out - pw_total), lambda b, t: (b * steps + t, 0)),
                   pl.BlockSpec((1, POOL_HALO, pw_total), lambda b, t: (b, 0, 0))),
        scratch_shapes=[
            pltpu.VMEM((n, pw_total), F32),
            pltpu.VMEM((n, pw_total), F32),
            pltpu.VMEM((n, pw_total - gw), F32),
            pltpu.VMEM((n, pw_total - 2 * gw), F32),
        ],
        compiler_params=_params("parallel", "arbitrary"),
        name="proj_pool_prompt",
    )(x, g.reshape(1, d), w, pool_w, pool_scale)


def _pool_sample_body(u_ref, st_ref, pw_ref, ps_ref, o_ref, sto_ref, *, gw, buf):
    pw_total = gw * len(POOL_WINDOWS)
    new = u_ref[:, 0:pw_total]
    sums, cnts = [], []
    for gi, w in enumerate(POOL_WINDOWS):
        cols = slice(gi * gw, (gi + 1) * gw)
        s = new[:, cols]
        for i in range(1, w):
            s = s + st_ref[:, buf - i, cols]
        sums.append(s)
        cnts.append(float(min(PAST_LEN + 1, w)))
    _pool_mix(sums, cnts, lambda cols: new[:, cols], pw_ref, ps_ref, o_ref, gw)
    sto_ref[:, 0:buf - 1, :] = st_ref[:, 1:buf, :]
    sto_ref[:, buf - 1, :] = new


def pool_sample(proj, state, pool_w, pool_scale):
    n_seq, buf, pw_total = state.shape
    gw = pw_total // len(POOL_WINDOWS)
    body = functools.partial(_pool_sample_body, gw=gw, buf=buf)
    return pl.pallas_call(
        body,
        out_shape=(jax.ShapeDtypeStruct((n_seq, pw_total), F32),
                   jax.ShapeDtypeStruct(state.shape, F32)),
        compiler_params=pltpu.CompilerParams(vmem_limit_bytes=VMEM_LIMIT_BYTES),
        name="pool_sample",
    )(proj, state, pool_w, pool_scale)


def _route_top2(h, g, r, n_experts):
    logits = _wdot(_rms(h, g), r)
    lane = lax.broadcasted_iota(jnp.int32, logits.shape, 1)
    lm = jnp.where(lane < n_experts, logits, -jnp.inf)
    m1 = jnp.max(lm, axis=1, keepdims=True)
    i1 = jnp.min(jnp.where(lm == m1, lane, LANE), axis=1, keepdims=True)
    lm2 = jnp.where(lane == i1, -jnp.inf, lm)
    m2 = jnp.max(lm2, axis=1, keepdims=True)
    i2 = jnp.min(jnp.where(lm2 == m2, lane, LANE), axis=1, keepdims=True)
    e2 = jnp.exp(m2 - m1)
    den = 1.0 + e2
    idx = jnp.where(lane == 0, i1, jnp.where(lane == 1, i2, 0))
    gate = jnp.where(lane == 0, 1.0 / den, jnp.where(lane == 1, e2 / den, 0.0))
    return idx, gate


def _dispatch_body(dest_ref, h_ref, gate_ref, zeros_hbm, o_hbm, ext, sem, *, tb, n):
    del zeros_hbm
    i = pl.program_id(0)
    slot = i % 2
    groups = tb // SUBLANE
    d = h_ref.shape[1]

    def row_copy(s, t, u, row):
        return pltpu.make_async_copy(ext.at[s, t, pl.ds(u, 1)], o_hbm.at[pl.ds(row, 1)], sem.at[s])

    def wait_slot(s):
        def body(t, carry):
            pltpu.make_async_copy(ext.at[s, t], o_hbm.at[pl.ds(0, SUBLANE)], sem.at[s]).wait()
            return carry
        lax.fori_loop(0, groups, body, 0)

    @pl.when(i >= 2)
    def _():
        wait_slot(slot)

    ext[slot, :, :, 0:d] = h_ref[...].reshape(groups, SUBLANE, d)
    ext[slot, :, :, d:d + LANE] = gate_ref[...].reshape(groups, SUBLANE, LANE)

    def body(t, carry):
        for u in range(SUBLANE):
            row_copy(slot, t, u, dest_ref[0, 0, t * SUBLANE + u]).start(priority=u % 2)
        return carry
    lax.fori_loop(0, groups, body, 0)

    @pl.when(i == n - 1)
    def _():
        wait_slot(slot)
        if n >= 2:
            wait_slot(1 - slot)


def moe_dispatch(h_all, row_offset, rows, gates, dest, into, tb, name):
    d = h_all.shape[1]
    off = row_offset // tb
    n = rows // tb
    body = functools.partial(_dispatch_body, tb=tb, n=n)
    return pl.pallas_call(
        body,
        out_shape=jax.ShapeDtypeStruct(into.shape, F32),
        grid=(n,),
        in_specs=[
            pl.BlockSpec((1, 1, tb), lambda i: (i, 0, 0), memory_space=pltpu.SMEM),
            pl.BlockSpec((tb, d), lambda i: (i + off, 0)),
            pl.BlockSpec((tb, LANE), lambda i: (i, 0)),
            pl.BlockSpec(memory_space=pl.ANY),
        ],
        out_specs=pl.BlockSpec(memory_space=pl.ANY),
        scratch_shapes=[pltpu.VMEM((2, tb // SUBLANE, SUBLANE, d + LANE), F32), pltpu.SemaphoreType.DMA((2,))],
        input_output_aliases={3: 0},
        compiler_params=_params("arbitrary"),
        name=name,
    )(dest.reshape(n, 1, tb), h_all, gates, into)


def _experts_body(te_ref, nu_ref, nb_ref, fill_ref, g_first_ref, g_next_ref, s_cur_ref,
                  h_hbm, g_ref, wg_ref, wu_ref, wo_ref, y_hbm,
                  xbuf, acc, xn_ref, gsem, ssem, *, tm, block, n_rows):
    i = pl.program_id(0)
    j = pl.program_id(1)
    n_tiles = pl.num_programs(0)
    nj = pl.num_programs(1)
    n_used = nu_ref[0]
    slot = i % 2
    valid = i < n_used
    chunks = tm // SUBLANE
    d = xn_ref.shape[1]

    def gather(idx_ref, s):
        def body(t, carry):
            row0 = pl.multiple_of(idx_ref[0, 0, t], SUBLANE)
            pltpu.make_async_copy(h_hbm.at[pl.ds(row0, SUBLANE), pl.ds(0, d)], xbuf.at[s, t], gsem.at[s]).start()
            return carry
        lax.fori_loop(0, chunks, body, 0)

    def wait_gather(s):
        def body(t, carry):
            pltpu.make_async_copy(h_hbm.at[pl.ds(0, SUBLANE), pl.ds(0, d)], xbuf.at[s, t], gsem.at[s]).wait()
            return carry
        lax.fori_loop(0, chunks, body, 0)

    def scatter():
        def body(t, carry):
            row0 = pl.multiple_of(s_cur_ref[0, 0, t], SUBLANE)
            pltpu.make_async_copy(acc.at[t], y_hbm.at[pl.ds(row0, SUBLANE)], ssem.at[0]).start()
            return carry
        lax.fori_loop(0, chunks, body, 0)

    def wait_scatter():
        def body(t, carry):
            pltpu.make_async_copy(acc.at[t], y_hbm.at[pl.ds(0, SUBLANE)], ssem.at[0]).wait()
            return carry
        lax.fori_loop(0, chunks, body, 0)

    @pl.when(j == 0)
    def _():
        @pl.when(i == 0)
        def _():
            gather(g_first_ref, 0)
            acc[...] = jnp.zeros_like(acc)
            first_free = nb_ref[0] * (block // SUBLANE)
            n_free = n_rows // SUBLANE - first_free

            def zero_copy(row0):
                return pltpu.make_async_copy(acc.at[0], y_hbm.at[pl.ds(pl.multiple_of(row0, SUBLANE), SUBLANE)],
                                             ssem.at[0])

            def fill(c, carry):
                zero_copy((first_free + c) * SUBLANE).start()
                zero_copy(n_rows + (first_free + c) * SUBLANE).start()
                return carry
            lax.fori_loop(0, n_free, fill, 0)

            def fill_spare(c, carry):
                zero_copy(2 * n_rows + c * SUBLANE).start()
                return carry
            lax.fori_loop(0, chunks, fill_spare, 0)

            def drain(c, carry):
                zero_copy(0).wait()
                return carry
            lax.fori_loop(0, 2 * n_free + chunks, drain, 0)

        @pl.when(i + 1 < n_used)
        def _():
            gather(g_next_ref, 1 - slot)

        @pl.when(valid)
        def _():
            wait_gather(slot)
            xn_ref[...] = _rms(xbuf[slot].reshape(tm, d), g_ref[...]).astype(BF16)

        @pl.when((i >= 1) & (i <= n_used))
        def _():
            wait_scatter()

        @pl.when(valid)
        def _():
            acc[...] = jnp.zeros_like(acc)

    def ff_step(n_chunks):
        rows = n_chunks * SUBLANE
        xn = xn_ref[0:rows, :]
        a = _dot(xn, wg_ref[0].astype(BF16))
        b = _dot(xn, wu_ref[0].astype(BF16))
        part = _dot((_silu(a) * b).astype(BF16), wo_ref[0].astype(BF16)).reshape(n_chunks, SUBLANE, d)
        acc[0:n_chunks] += part

    fill = fill_ref[i]

    @pl.when(valid & (fill > chunks // 2))
    def _():
        ff_step(chunks)

    @pl.when(valid & (fill > chunks // 4) & (fill <= chunks // 2))
    def _():
        ff_step(chunks // 2)

    @pl.when(valid & (fill <= chunks // 4))
    def _():
        ff_step(chunks // 4)

    @pl.when(valid & (j == nj - 1))
    def _():
        scatter()

        @pl.when(i == n_tiles - 1)
        def _():
            wait_scatter()


def moe_experts(h_ps, g, wi, wo, gather_rows, scatter_rows, tile_expert, n_used, n_blocks_used, tile_fill,
                block, n_rows, tm, tf):
    d = wo.shape[2]
    f = wo.shape[1]
    nj = f // tf
    n_tiles = gather_rows.shape[0]
    chunks = tm // SUBLANE

    def wblock(col_off):
        def index_map(i, j, te, nu, nb, tf_):
            jj = jnp.where(i < nu[0], j, nj - 1)
            return (te[i], 0, col_off + jj)
        return index_map

    def woblock(i, j, te, nu, nb, tf_):
        return (te[i], jnp.where(i < nu[0], j, nj - 1), 0)

    def smem_tile(index_map):
        return pl.BlockSpec((1, 1, chunks), index_map, memory_space=pltpu.SMEM)

    grid_spec = pltpu.PrefetchScalarGridSpec(
        num_scalar_prefetch=4,
        grid=(n_tiles, nj),
        in_specs=[
            smem_tile(lambda i, j, te, nu, nb, tf_: (0, 0, 0)),
            smem_tile(lambda i, j, te, nu, nb, tf_: (jnp.minimum(i + 1, n_tiles - 1), 0, 0)),
            smem_tile(lambda i, j, te, nu, nb, tf_: (i, 0, 0)),
            pl.BlockSpec(memory_space=pl.ANY),
            pl.BlockSpec((1, d), lambda i, j, te, nu, nb, tf_: (0, 0)),
            pl.BlockSpec((1, d, tf), wblock(0)),
            pl.BlockSpec((1, d, tf), wblock(nj)),
            pl.BlockSpec((1, tf, d), woblock),
        ],
        out_specs=pl.BlockSpec(memory_space=pl.ANY),
        scratch_shapes=[
            pltpu.VMEM((2, chunks, SUBLANE, d), F32),
            pltpu.VMEM((chunks, SUBLANE, d), F32),
            pltpu.VMEM((tm, d), BF16),
            pltpu.SemaphoreType.DMA((2,)),
            pltpu.SemaphoreType.DMA((1,)),
        ],
    )
    return pl.pallas_call(
        functools.partial(_experts_body, tm=tm, block=block, n_rows=n_rows),
        out_shape=jax.ShapeDtypeStruct((2 * n_rows + tm, d), F32),
        grid_spec=grid_spec,
        compiler_params=_params("arbitrary", "arbitrary"),
        name="moe_experts",
    )(tile_expert, n_used, n_blocks_used, tile_fill, gather_rows, gather_rows, scatter_rows, h_ps,
      g.reshape(1, d), wi, wi, wo)


def _combine_body(nb_ref, hx_ref, y1_ref, y2_ref, fn_ref, o_ref, *, d, final_norm):
    @pl.when(pl.program_id(0) < nb_ref[0])
    def _():
        hx = hx_ref[...]
        out = hx[:, 0:d] + (hx[:, d:d + 1] * y1_ref[...] + hx[:, d + 1:d + 2] * y2_ref[...])
        if final_norm:
            out = _rms(out, fn_ref[...])
        o_ref[...] = out

    @pl.when(pl.program_id(0) >= nb_ref[0])
    def _():
        o_ref[...] = jnp.zeros_like(o_ref)


def moe_combine(h_ps, y, n_blocks_used, choice_stride, fn, tb, final_norm, name):
    rows, dx = h_ps.shape
    d = y.shape[1]
    off2 = choice_stride // tb

    def used(i, nb):
        return jnp.minimum(i, nb[0] - 1)

    grid_spec = pltpu.PrefetchScalarGridSpec(
        num_scalar_prefetch=1,
        grid=(rows // tb,),
        in_specs=[
            pl.BlockSpec((tb, dx), lambda i, nb: (used(i, nb), 0)),
            pl.BlockSpec((tb, d), lambda i, nb: (used(i, nb), 0)),
            pl.BlockSpec((tb, d), lambda i, nb: (used(i, nb) + off2, 0)),
            pl.BlockSpec((1, d), lambda i, nb: (0, 0)),
        ],
        out_specs=pl.BlockSpec((tb, d), lambda i, nb: (i, 0)),
    )
    return pl.pallas_call(
        functools.partial(_combine_body, d=d, final_norm=final_norm),
        out_shape=jax.ShapeDtypeStruct((rows, d), F32),
        grid_spec=grid_spec,
        compiler_params=_params("arbitrary"),
        name=name,
    )(n_blocks_used, h_ps, y, y, fn.reshape(1, d))


def _unpermute_body(cur_ref, nxt_ref, y_hbm, o_ref, buf, sem, *, tb):
    i = pl.program_id(0)
    n = pl.num_programs(0)
    slot = i % 2
    groups = tb // SUBLANE

    def start(idx_ref, s):
        def body(t, carry):
            for u in range(SUBLANE):
                row = idx_ref[0, 0, t * SUBLANE + u]
                pltpu.make_async_copy(y_hbm.at[pl.ds(row, 1)], buf.at[s, t, pl.ds(u, 1)], sem.at[s]).start(
                    priority=u % 2)
            return carry
        lax.fori_loop(0, groups, body, 0)

    @pl.when(i == 0)
    def _():
        start(cur_ref, 0)

    @pl.when(i + 1 < n)
    def _():
        start(nxt_ref, 1 - slot)

    def wait_body(t, carry):
        pltpu.make_async_copy(y_hbm.at[pl.ds(0, SUBLANE)], buf.at[slot, t], sem.at[slot]).wait()
        return carry
    lax.fori_loop(0, groups, wait_body, 0)
    o_ref[...] = buf[slot].reshape(o_ref.shape)


def moe_unpermute(y_ps, dest, tb, name):
    rows = dest.shape[0]
    d = y_ps.shape[1]
    n = rows // tb
    idx = dest.reshape(n, 1, tb)

    def smem_tile(index_map):
        return pl.BlockSpec((1, 1, tb), index_map, memory_space=pltpu.SMEM)

    return pl.pallas_call(
        functools.partial(_unpermute_body, tb=tb),
        out_shape=jax.ShapeDtypeStruct((rows, d), F32),
        grid=(n,),
        in_specs=[smem_tile(lambda i: (i, 0, 0)), smem_tile(lambda i: (jnp.minimum(i + 1, n - 1), 0, 0)),
                  pl.BlockSpec(memory_space=pl.ANY)],
        out_specs=pl.BlockSpec((tb, d), lambda i: (i, 0)),
        scratch_shapes=[pltpu.VMEM((2, tb // SUBLANE, SUBLANE, d), F32), pltpu.SemaphoreType.DMA((2,))],
        compiler_params=_params("arbitrary"),
        name=name,
    )(idx, idx, y_ps)


def moe_plan(idx, n_experts, tm, block):
    n_tok, k = idx.shape
    assert k == 2
    n_pairs = n_experts * n_experts
    chunks_per_tile = tm // SUBLANE
    n_rows = -(-(n_tok + n_pairs * (SUBLANE - 1)) // block) * block
    n_chunks = n_rows // SUBLANE

    pair = idx[:, 0] * n_experts + idx[:, 1]
    onehot = (pair[:, None] == jnp.arange(n_pairs, dtype=jnp.int32)[None, :]).astype(jnp.int32)
    csum = jnp.cumsum(onehot, axis=0)
    counts = csum[-1]
    rank = jnp.sum(csum * onehot, axis=1) - 1
    padded = (counts + SUBLANE - 1) // SUBLANE * SUBLANE
    total = jnp.sum(padded)
    last_group = jnp.max(jnp.where(counts > 0, jnp.arange(n_pairs, dtype=jnp.int32), 0))
    padded = padded + jnp.where(jnp.arange(n_pairs) == last_group, (-total) % block, 0)
    group_end = jnp.cumsum(padded)
    group_start = group_end - padded
    dest = jnp.sum(onehot * group_start[None, :], axis=1) + rank
    rows_used = group_end[-1]

    chunk_row = jnp.arange(n_chunks, dtype=jnp.int32) * SUBLANE
    chunk_group = jnp.sum((chunk_row[:, None] >= group_end[None, :]).astype(jnp.int32), axis=1)
    chunk_used = chunk_row < rows_used
    chunk_group = jnp.minimum(chunk_group, n_pairs - 1)
    chunk_experts = jnp.stack([chunk_group // n_experts, chunk_group % n_experts], axis=1)
    experts = jnp.arange(n_experts, dtype=jnp.int32)
    member = ((chunk_experts[:, :, None] == experts[None, None, :]) & chunk_used[:, None, None])
    member = member.astype(jnp.int32)
    member_e = jnp.sum(member, axis=1)
    csum_e = jnp.cumsum(member_e, axis=0)
    counts_e = csum_e[-1]
    tiles_e = (counts_e + chunks_per_tile - 1) // chunks_per_tile
    tile_end = jnp.cumsum(tiles_e)
    start_e = (tile_end - tiles_e) * chunks_per_tile
    pos = jnp.sum(member * (start_e[None, None, :] + csum_e[:, None, :] - 1), axis=2)
    n_tiles = 2 * n_chunks // chunks_per_tile + n_experts
    n_pos = n_tiles * chunks_per_tile
    pos = jnp.where(chunk_used[:, None], pos, n_pos)
    entry = 2 * jnp.arange(n_chunks, dtype=jnp.int32)[:, None] + jnp.arange(2, dtype=jnp.int32)[None, :] + 1
    table = jnp.zeros((n_pos,), jnp.int32).at[pos.reshape(-1)].set(entry.reshape(-1), mode="drop")
    table = table.reshape(n_tiles, chunks_per_tile)
    filled = table > 0
    src_chunk = (table - 1) // 2
    choice = (table - 1) % 2
    spare = 2 * n_rows + jnp.arange(chunks_per_tile, dtype=jnp.int32) * SUBLANE
    gather_rows = jnp.where(filled, src_chunk * SUBLANE, 0)
    scatter_rows = jnp.where(filled, choice * n_rows + src_chunk * SUBLANE, spare[None, :])

    n_used = tile_end[-1]
    tile_ids = jnp.arange(n_tiles, dtype=jnp.int32)
    tile_expert = jnp.sum((tile_ids[:, None] >= tile_end[None, :]).astype(jnp.int32), axis=1)
    last_expert = jnp.sum((n_used - 1 >= tile_end).astype(jnp.int32))
    tile_expert = jnp.where(tile_ids < n_used, tile_expert, last_expert).astype(jnp.int32)
    return (dest.astype(jnp.int32),
            gather_rows.reshape(n_tiles, 1, chunks_per_tile).astype(jnp.int32),
            scatter_rows.reshape(n_tiles, 1, chunks_per_tile).astype(jnp.int32),
            tile_expert, n_used.reshape(1).astype(jnp.int32),
            (rows_used // block).reshape(1).astype(jnp.int32),
            jnp.sum(filled.astype(jnp.int32), axis=1), n_rows)


def _pad_lanes(v):
    return jnp.zeros((1, LANE), F32).at[0, :v.shape[0]].set(v)


def kernel(x_prompt, x_sample, state_dn, state_dn_conv, state_pool, cache_mem_k, cache_mem_v, mem_prompt, a_norm1, a_w_in, a_conv_w, a_a_log, a_dt_bias, a_o_norm, a_w_out, a_norm2, a_ffn_wi, a_ffn_wo, b_norm1, b_w_in, b_pool_w, b_pool_scale, b_w_out, b_norm2, b_router, b_moe_wi, b_moe_wo, m_norm, m_w_k, m_w_v, final_norm):
    bp, seq, d = x_prompt.shape
    bs = x_sample.shape[0]
    assert x_sample.shape[1] == 1
    depth = m_norm.shape[0]
    assert depth % 2 == 0, "the output norm is fused into the expert block of the last (odd) layer"
    heads, dk, dv = state_dn.shape[2], state_dn.shape[3], state_dn.shape[4]
    conv_ch = state_dn_conv.shape[-1]
    qk_w, v_w = heads * dk, heads * dv
    assert conv_ch == 2 * qk_w + v_w and dk == LANE and dv == LANE and heads <= LANE
    n_mem, x_heads, x_dh = cache_mem_k.shape[2], cache_mem_k.shape[3], cache_mem_k.shape[4]
    xw = x_heads * x_dh
    pool_w_total = state_pool.shape[-1]
    pool_buf = state_pool.shape[2]

    hp = x_prompt.reshape(bp * seq, d)
    hs = x_sample.reshape(bs, d)

    o1, o2, o3, o4 = conv_ch, conv_ch + v_w, conv_ch + v_w + heads, conv_ch + v_w + 2 * heads
    col_a = {"qkv": 0, "z": conv_ch, "cq": conv_ch + v_w, "b": conv_ch + v_w + xw, "a": conv_ch + v_w + xw + LANE}
    n_a = col_a["a"] + LANE

    dn_p, conv_p, pool_p = [], [], []
    mk_all = mv_all = None
    dn_s, conv_s, pool_s = [], [], []
    for i in range(depth):
        j = i // 2
        w_kv = jnp.concatenate([m_w_k[i], m_w_v[i]], axis=1).astype(BF16)
        kv, mk_all, mv_all = mem_kv(mem_prompt.reshape(bp * n_mem, d), m_norm[i], w_kv, x_heads, x_dh, ROWS_MEM,
                                    f"mem_kv_{i}", i, depth, into=None if i == 0 else (mk_all, mv_all))
        if i % 2 == 0:
            w = a_w_in[j]
            def regroup(t):
                rows = t.shape[0]
                return jnp.concatenate(
                    [t[:, :o2], t[:, o4:], t[:, o2:o3], jnp.zeros((rows, LANE - heads), t.dtype),
                     t[:, o3:o4], jnp.zeros((rows, n_a - col_a["a"] - heads), t.dtype)], axis=1)

            assert col_a["b"] == o2 + xw
            a_log, dt_b = _pad_lanes(a_a_log[j]), _pad_lanes(a_dt_bias[j])
            o_norm = a_o_norm[j].reshape(1, dv)

            proj_p = rms_matmul(hp, a_norm1[j], regroup(w.astype(BF16)), ROWS_IN_A, f"a_in_p_{i}")
            proj_s = regroup(rms_matmul(hs, a_norm1[j], w, bs, f"a_in_s_{i}"))
            mix_p, s_p = deltanet_prompt(proj_p, col_a, a_conv_w[j], a_log, dt_b, o_norm,
                                         bp, seq, heads, dk, dv, rows=DN_STEP_CHUNKS * CHUNK)
            mix_s, s_s, c_s = deltanet_sample(proj_s, col_a, state_dn_conv[j], state_dn[j], a_conv_w[j],
                                              a_log, dt_b, o_norm, heads, dk, dv, nb=SUBLANE)
            xa_p = xattn_prompt(proj_p, col_a["cq"], kv, bp, seq, n_mem, x_heads, x_dh, tq=min(ROWS_XATTN, seq))
            xa_s = xattn_sample(proj_s, col_a["cq"], cache_mem_k, cache_mem_v, i, nb=SUBLANE)
            hp = out_proj_ffn(hp, mix_p, xa_p, a_w_out[j].astype(BF16), a_norm2[j], a_ffn_wi[j].astype(BF16),
                              a_ffn_wo[j].astype(BF16), ROWS_OUT_FFN, f"a_out_ffn_p_{i}")
            hs = out_proj(hs, mix_s, xa_s, a_w_out[j], bs, f"a_out_s_{i}")
            hs = ffn_dense(hs, a_norm2[j], a_ffn_wi[j], a_ffn_wo[j], bs, f"a_ffn_s_{i}", tf=FF_TILE_SAMPLE)
            dn_p.append(s_p)
            conv_p.append(proj_p.reshape(bp, seq, n_a)[:, seq - (CONV_WIDTH - 1):, :conv_ch])
            dn_s.append(s_s)
            conv_s.append(c_s)
        else:
            ps = b_pool_scale[j].reshape(1, pool_w_total)
            last = i == depth - 1

            mix_p, cq_p, pool_tail = proj_pool_prompt(hp, b_norm1[j], b_w_in[j].astype(BF16),
                                                      b_pool_w[j].astype(BF16), ps, bp, seq, tt=ROWS_STREAM)
            proj_s = rms_matmul(hs, b_norm1[j], b_w_in[j], bs, f"b_in_s_{i}")
            mix_s, buf_s = pool_sample(proj_s, state_pool[j], b_pool_w[j], ps)
            xa_p = xattn_prompt(cq_p, 0, kv, bp, seq, n_mem, x_heads, x_dh, tq=min(ROWS_XATTN, seq))
            xa_s = xattn_sample(proj_s, pool_w_total, cache_mem_k, cache_mem_v, i, nb=SUBLANE)
            n_p = bp * seq
            n_experts = b_router.shape[-1]
            r_pad = jnp.concatenate([b_router[j], jnp.zeros((d, LANE - n_experts), F32)], axis=1)
            h_all, idx_p, gate_p = out_proj(hp, mix_p, xa_p, b_w_out[j].astype(BF16), ROWS_STREAM, f"b_out_p_{i}",
                                            out_rows=n_p + bs, route=(b_norm2[j], r_pad.astype(BF16), n_experts))
            h_all, idx_s, gate_s = out_proj(hs, mix_s, xa_s, b_w_out[j], bs, f"b_out_s_{i}", out_rows=n_p + bs,
                                            into=h_all, row_offset=n_p, route=(b_norm2[j], r_pad, n_experts))
            idx = jnp.concatenate([idx_p[:n_p, :TOP_K], idx_s[:, :TOP_K]], axis=0)
            (dest, gather_rows, scatter_rows, tile_expert, n_used, n_blocks_used, tile_fill, n_rows) = moe_plan(
                idx, n_experts, MOE_TILE, MOE_ROW_BLOCK)
            h_ps = jnp.zeros((n_rows, d + LANE), F32)
            h_ps = moe_dispatch(h_all, 0, n_p, gate_p, dest[:n_p], h_ps, ROWS_STREAM, f"b_dispatch_p_{i}")
            h_ps = moe_dispatch(h_all, n_p, bs, gate_s, dest[n_p:], h_ps, bs, f"b_dispatch_s_{i}")
            y = moe_experts(h_ps, b_norm2[j], b_moe_wi[j], b_moe_wo[j], gather_rows, scatter_rows, tile_expert,
                            n_used, n_blocks_used, tile_fill, MOE_ROW_BLOCK, n_rows, MOE_TILE, MOE_FF_TILE)
            y_ps = moe_combine(h_ps, y, n_blocks_used, n_rows, final_norm, MOE_ROW_BLOCK, last,
                               f"b_combine_{i}")
            hp = moe_unpermute(y_ps, dest[:n_p], ROWS_STREAM, f"b_unpermute_p_{i}")
            hs = moe_unpermute(y_ps, dest[n_p:], bs, f"b_unpermute_s_{i}")
            pool_p.append(pool_tail[:, POOL_HALO - pool_buf:, :])
            pool_s.append(buf_s)

    y_prompt = hp.reshape(bp, seq, d)
    y_sample = hs.reshape(bs, 1, d)
    return (y_prompt, y_sample, jnp.stack(dn_p), jnp.stack(conv_p), jnp.stack(pool_p),
            mk_all.reshape(depth, bp, n_mem, x_heads, x_dh), mv_all.reshape(depth, bp, n_mem, x_heads, x_dh),
            jnp.stack(dn_s), jnp.stack(conv_s), jnp.stack(pool_s))
```

```python
import functools

import jax
import jax.numpy as jnp
from jax import lax
from jax.experimental import pallas as pl
from jax.experimental.pallas import tpu as pltpu

F32 = jnp.float32
BF16 = jnp.bfloat16
HIGHEST = lax.Precision.HIGHEST

EPS = 1e-6
CHUNK = 64
CONV_WIDTH = 4
POOL_WINDOWS = (2, 4, 8, 16)
PAST_LEN = 16384
TOP_K = 2

LANE = 128
SUBLANE = 8
VMEM_LIMIT_BYTES = 56 * 2**20

ROWS_IN_A = 512
ROWS_OUT_FFN = 512
ROWS_STREAM = 1024
ROWS_MEM = 256
ROWS_XATTN = 2048
FF_TILE_SAMPLE = 256
MOE_TILE = 1024
MOE_FF_TILE = 512
MOE_ROW_BLOCK = 512
DN_STEP_CHUNKS = 8
INV_BLOCK = 16
POOL_HALO = 32


def _params(*sem):
    return pltpu.CompilerParams(dimension_semantics=sem, vmem_limit_bytes=VMEM_LIMIT_BYTES)


def _resident(shape):
    nd = len(shape)
    return pl.BlockSpec(shape, lambda *_: (0,) * nd, pipeline_mode=pl.Buffered(1))


def _rms(x, w):
    return x * lax.rsqrt(jnp.mean(x * x, axis=-1, keepdims=True) + EPS) * w


def _silu(x):
    half = 0.5 * x
    return half + half * jnp.tanh(half)


def _dot(a, b):
    return jnp.dot(a, b, preferred_element_type=F32)


def _wdot(a, w):
    if w.dtype == F32:
        return jnp.dot(a, w, precision=HIGHEST, preferred_element_type=F32)
    return jnp.dot(a.astype(BF16), w, preferred_element_type=F32)


def _bdot(a, b):
    return lax.dot_general(a, b, (((2,), (1,)), ((0,), (0,))), preferred_element_type=F32)


def _bdot_nt(a, b):
    return lax.dot_general(a, b, (((2,), (2,)), ((0,), (0,))), preferred_element_type=F32)


def _bdot_tn(a, b):
    return lax.dot_general(a, b, (((1,), (1,)), ((0,), (0,))), preferred_element_type=F32)


def _rms_matmul_body(x_ref, g_ref, w_ref, o_ref):
    xn = _rms(x_ref[...], g_ref[...])
    o_ref[...] = _wdot(xn, w_ref[...])


def _wspec(shape, index_map, steps):
    if steps == 1:
        return pl.BlockSpec(shape, index_map, pipeline_mode=pl.Buffered(1))
    return pl.BlockSpec(shape, index_map)


def rms_matmul(x, g, w, tm, name, tn=None):
    m, d = x.shape
    n = w.shape[1]
    tn = n if tn is None else tn
    nj = n // tn
    return pl.pallas_call(
        _rms_matmul_body,
        out_shape=jax.ShapeDtypeStruct((m, n), F32),
        grid=(m // tm, nj),
        in_specs=[pl.BlockSpec((tm, d), lambda i, j: (i, 0)), _resident((1, d)),
                  _wspec((d, tn), lambda i, j: (0, j), nj)],
        out_specs=pl.BlockSpec((tm, tn), lambda i, j: (i, j)),
        compiler_params=_params("parallel", "arbitrary"),
        name=name,
    )(x, g.reshape(1, d), w)


def _out_proj_body(x_ref, a1_ref, a2_ref, w1_ref, w2_ref, *rest, n_main, n_experts, has_into):
    route = n_experts is not None
    if route:
        g_ref, r_ref = rest[0], rest[1]
        o_ref, idx_ref, gate_ref = rest[-3:]
    else:
        o_ref = rest[-1]
    del has_into

    @pl.when(pl.program_id(0) < n_main)
    def _():
        acc = _wdot(a1_ref[...], w1_ref[...]) + _wdot(a2_ref[...], w2_ref[...])
        h = x_ref[...] + acc
        o_ref[...] = h
        if route:
            idx_ref[...], gate_ref[...] = _route_top2(h, g_ref[...], r_ref[...], n_experts)

    @pl.when(pl.program_id(0) >= n_main)
    def _():
        o_ref[...] = jnp.zeros_like(o_ref)
        if route:
            idx_ref[...] = jnp.zeros_like(idx_ref)
            gate_ref[...] = jnp.zeros_like(gate_ref)


def out_proj(x, a1, a2, w, tm, name, out_rows=None, into=None, row_offset=0, route=None):
    m, d = x.shape
    k1, k2 = a1.shape[1], a2.shape[1]
    assert w.shape[0] == k1 + k2 and k1 % k2 == 0 and row_offset % tm == 0
    out_rows = m if out_rows is None else out_rows
    off = row_offset // tm
    n_main = m // tm
    n_extra = 0 if (into is not None or out_rows == m) else 1
    assert out_rows - m <= n_extra * tm or into is not None
    last = n_main - 1
    in_specs = [
        pl.BlockSpec((tm, d), lambda i: (jnp.minimum(i, last), 0)),
        pl.BlockSpec((tm, k1), lambda i: (jnp.minimum(i, last), 0)),
        pl.BlockSpec((tm, k2), lambda i: (jnp.minimum(i, last), 0)),
        pl.BlockSpec((k1, d), lambda i: (0, 0), pipeline_mode=pl.Buffered(1)),
        pl.BlockSpec((k2, d), lambda i: (k1 // k2, 0), pipeline_mode=pl.Buffered(1)),
    ]
    args = [x, a1, a2, w, w]
    out_shape = [jax.ShapeDtypeStruct((out_rows, d), F32)]
    out_specs = [pl.BlockSpec((tm, d), lambda i: (i + off, 0))]
    n_experts = None
    if route is not None:
        g, r_pad, n_experts = route
        in_specs += [_resident((1, d)), _resident((d, LANE))]
        args += [g.reshape(1, d), r_pad]
        n_steps = n_main + n_extra
        out_shape += [jax.ShapeDtypeStruct((n_steps * tm, LANE), jnp.int32),
                      jax.ShapeDtypeStruct((n_steps * tm, LANE), F32)]
        out_specs += [pl.BlockSpec((tm, LANE), lambda i: (i, 0))] * 2
    aliases = {}
    if into is not None:
        in_specs.append(pl.BlockSpec(memory_space=pl.ANY))
        args.append(into)
        aliases = {len(args) - 1: 0}
    outs = pl.pallas_call(
        functools.partial(_out_proj_body, n_main=n_main, n_experts=n_experts, has_into=into is not None),
        out_shape=tuple(out_shape),
        grid=(n_main + n_extra,),
        in_specs=in_specs,
        out_specs=tuple(out_specs),
        input_output_aliases=aliases,
        compiler_params=_params("parallel"),
        name=name,
    )(*args)
    return outs if route is not None else outs[0]


def _ffn_body(h_ref, g_ref, wg_ref, wu_ref, wo_ref, o_ref, xn_ref):
    @pl.when(pl.program_id(1) == 0)
    def _():
        h = h_ref[...]
        xn_ref[...] = _rms(h, g_ref[...])
        o_ref[...] = h

    xn = xn_ref[...]
    a = _wdot(xn, wg_ref[...])
    b = _wdot(xn, wu_ref[...])
    o_ref[...] += _wdot(_silu(a) * b, wo_ref[...])


def ffn_dense(h, g, wi, wo, tm, name, tf=None):
    m, d = h.shape
    f = wo.shape[0]
    tf = f if tf is None else tf
    nj = f // tf
    return pl.pallas_call(
        _ffn_body,
        out_shape=jax.ShapeDtypeStruct((m, d), F32),
        grid=(m // tm, nj),
        in_specs=[
            pl.BlockSpec((tm, d), lambda i, j: (i, 0)),
            _resident((1, d)),
            _wspec((d, tf), lambda i, j: (0, j), nj),
            _wspec((d, tf), lambda i, j: (0, nj + j), nj),
            _wspec((tf, d), lambda i, j: (j, 0), nj),
        ],
        out_specs=pl.BlockSpec((tm, d), lambda i, j: (i, 0)),
        scratch_shapes=[pltpu.VMEM((tm, d), F32)],
        compiler_params=_params("parallel", "arbitrary"),
        name=name,
    )(h, g.reshape(1, d), wi, wi, wo)


def _mem_kv_body(x_ref, g_ref, w_ref, *rest, heads, dh, n_own):
    kv_ref, k_ref, v_ref = rest[-3:]
    xw = heads * dh

    @pl.when(pl.program_id(0) < n_own)
    def _():
        res = _wdot(_rms(x_ref[...], g_ref[...]), w_ref[...])
        kv_ref[...] = res
        for h in range(heads):
            k_ref[:, h, :] = res[:, h * dh:(h + 1) * dh]
            v_ref[:, h, :] = res[:, xw + h * dh:xw + (h + 1) * dh]

    @pl.when(pl.program_id(0) >= n_own)
    def _():
        k_ref[...] = jnp.zeros_like(k_ref)
        v_ref[...] = jnp.zeros_like(v_ref)


def mem_kv(mem, g, w_kv, heads, dh, tm, name, layer, n_layers, into=None):
    m, d = mem.shape
    xw = heads * dh
    n_own = m // tm
    assert (into is None) == (layer == 0)
    n_steps = n_own * n_layers if into is None else n_own
    head_shape = jax.ShapeDtypeStruct((n_layers * m, heads, dh), F32)
    head_spec = pl.BlockSpec((tm, heads, dh), lambda i: (i + layer * n_own, 0, 0))
    own = lambda i: (jnp.minimum(i, n_own - 1), 0)
    in_specs = [pl.BlockSpec((tm, d), own), _resident((1, d)), _resident((d, 2 * xw))]
    args = [mem, g.reshape(1, d), w_kv]
    aliases = {}
    if into is not None:
        in_specs += [pl.BlockSpec(memory_space=pl.ANY)] * 2
        args += list(into)
        aliases = {3: 1, 4: 2}
    return pl.pallas_call(
        functools.partial(_mem_kv_body, heads=heads, dh=dh, n_own=n_own),
        out_shape=(jax.ShapeDtypeStruct((m, 2 * xw), F32), head_shape, head_shape),
        grid=(n_steps,),
        in_specs=in_specs,
        out_specs=(pl.BlockSpec((tm, 2 * xw), own), head_spec, head_spec),
        input_output_aliases=aliases,
        compiler_params=_params("arbitrary"),
        name=name,
    )(*args)


def _out_proj_ffn_body(x_ref, a1_ref, a2_ref, w1_ref, w2_ref, g_ref, wg_ref, wu_ref, wo_ref, o_ref):
    h = x_ref[...] + (_wdot(a1_ref[...], w1_ref[...]) + _wdot(a2_ref[...], w2_ref[...]))
    xn = _rms(h, g_ref[...])
    a = _wdot(xn, wg_ref[...])
    b = _wdot(xn, wu_ref[...])
    o_ref[...] = h + _wdot(_silu(a) * b, wo_ref[...])


def out_proj_ffn(x, a1, a2, w, g, wi, wo, tm, name):
    m, d = x.shape
    k1, k2 = a1.shape[1], a2.shape[1]
    f = wo.shape[0]
    assert w.shape[0] == k1 + k2 and k1 % k2 == 0
    return pl.pallas_call(
        _out_proj_ffn_body,
        out_shape=jax.ShapeDtypeStruct((m, d), F32),
        grid=(m // tm,),
        in_specs=[
            pl.BlockSpec((tm, d), lambda i: (i, 0)),
            pl.BlockSpec((tm, k1), lambda i: (i, 0)),
            pl.BlockSpec((tm, k2), lambda i: (i, 0)),
            pl.BlockSpec((k1, d), lambda i: (0, 0), pipeline_mode=pl.Buffered(1)),
            pl.BlockSpec((k2, d), lambda i: (k1 // k2, 0), pipeline_mode=pl.Buffered(1)),
            _resident((1, d)),
            pl.BlockSpec((d, f), lambda i: (0, 0), pipeline_mode=pl.Buffered(1)),
            pl.BlockSpec((d, f), lambda i: (0, 1), pipeline_mode=pl.Buffered(1)),
            _resident((f, d)),
        ],
        out_specs=pl.BlockSpec((tm, d), lambda i: (i, 0)),
        compiler_params=_params("parallel"),
        name=name,
    )(x, a1, a2, w, w, g.reshape(1, d), wi, wi, wo)


def _pair_blockdiag(y, left):
    yb = y.astype(BF16)
    zero = jnp.zeros_like(yb)
    return jnp.concatenate([jnp.where(left, yb, zero), jnp.where(left, zero, yb)], axis=1)


def _pair_mm(x, y, left):
    return _bdot(x.astype(BF16), _pair_blockdiag(y, left))


def _unit_lower_inverse(mx, eye, blk, left):
    mm = functools.partial(_pair_mm, left=left)
    md = jnp.where(blk, mx, 0.0)
    c = mx - md
    p = eye - md
    m2 = mm(md, md)
    p = p + mm(p, m2)
    m4 = mm(m2, m2)
    p = p + mm(p, m4)
    m8 = mm(m4, m4)
    td = p + mm(p, m8)
    n = mm(td, c)
    n2 = mm(n, n)
    q = eye - n
    q = q + mm(q, n2)
    return mm(q, td)


def _dn_prompt_body(qkv_ref, z_ref, bl_ref, al_ref, cw_ref, alog_ref, dtb_ref, on_ref,
                    dn_ref, sfin_ref, s_ref, xbuf_ref, act_ref, *, rows, heads, dk, dv):
    c = pl.program_id(1)
    wq = heads * dk

    @pl.when(c == 0)
    def _():
        s_ref[...] = jnp.zeros_like(s_ref)
        xbuf_ref[0:SUBLANE, :] = jnp.zeros((SUBLANE, xbuf_ref.shape[1]), F32)

    xbuf_ref[SUBLANE:SUBLANE + rows, :] = qkv_ref[...]
    first = SUBLANE - (CONV_WIDTH - 1)
    y = xbuf_ref[pl.ds(first, rows), :] * cw_ref[0:1, :]
    for j in range(1, CONV_WIDTH):
        y = y + xbuf_ref[pl.ds(first + j, rows), :] * cw_ref[j:j + 1, :]
    act_ref[...] = _silu(y)
    xbuf_ref[0:SUBLANE, :] = xbuf_ref[rows:rows + SUBLANE, :]

    pairs = heads // 2
    ri = lax.broadcasted_iota(jnp.int32, (CHUNK, 2 * CHUNK), 0)
    lane = lax.broadcasted_iota(jnp.int32, (CHUNK, 2 * CHUNK), 1)
    cj = lane % CHUNK
    left = (lane < CHUNK)[None]
    incl = (ri >= cj)[None]
    strict = (ri > cj)[None]
    blk = ((ri // INV_BLOCK) == (cj // INV_BLOCK))[None]
    eye = (ri == cj).astype(F32)[None]
    tr = lax.broadcasted_iota(jnp.int32, (CHUNK, CHUNK), 0)
    tc = lax.broadcasted_iota(jnp.int32, (CHUNK, CHUNK), 1)
    tri = (tr >= tc).astype(F32)

    n_chunks = rows // CHUNK
    zero = jnp.zeros((CHUNK, dk), BF16)

    def chunk_rows(cc):
        return slice(cc * CHUNK, (cc + 1) * CHUNK)

    def per_head(fn):
        return jnp.stack([fn(chunk_rows(cc), h) for cc in range(n_chunks) for h in range(heads)], axis=0)

    def per_pair(fn):
        return jnp.stack([fn(cc, p) for cc in range(n_chunks) for p in range(pairs)], axis=0)

    q3 = per_head(lambda rs, h: act_ref[rs, h * dk:(h + 1) * dk])
    k3 = per_head(lambda rs, h: act_ref[rs, wq + h * dk:wq + (h + 1) * dk])
    v3 = per_head(lambda rs, h: act_ref[rs, 2 * wq + h * dv:2 * wq + (h + 1) * dv])
    qn = q3 * lax.rsqrt(jnp.sum(q3 * q3, axis=-1, keepdims=True) + EPS) * (dk ** -0.5)
    kn = k3 * lax.rsqrt(jnp.sum(k3 * k3, axis=-1, keepdims=True) + EPS)

    beta = jax.nn.sigmoid(bl_ref[...])
    log_decay = -jnp.exp(alog_ref[...]) * jax.nn.softplus(al_ref[...] + dtb_ref[...])
    gcum = [jnp.dot(tri, log_decay[chunk_rows(cc)], precision=HIGHEST, preferred_element_type=F32)
            for cc in range(n_chunks)]
    gcum_t = [jnp.concatenate([gc_, pltpu.roll(gc_, LANE - 1, 1)], axis=0).T for gc_ in gcum]

    def pair_cols(x_of):
        return per_pair(lambda cc, p: jnp.where(left[0], x_of(cc)[:, 2 * p:2 * p + 1],
                                                x_of(cc)[:, 2 * p + 1:2 * p + 2]))

    gc2 = pair_cols(lambda cc: gcum[cc])
    bc2 = pair_cols(lambda cc: beta[chunk_rows(cc)])
    gr2 = per_pair(lambda cc, p: gcum_t[cc][2 * p:2 * p + 1, :])
    decay = jnp.where(incl, jnp.exp(gc2 - gr2), 0.0)

    kb = kn.astype(BF16)
    qb = qn.astype(BF16)

    def pair_of(x, cc, p):
        return x[cc * heads + 2 * p], x[cc * heads + 2 * p + 1]

    k_lhs = per_pair(lambda cc, p: jnp.concatenate(pair_of(kb, cc, p), axis=1))
    q_lhs = per_pair(lambda cc, p: jnp.concatenate(pair_of(qb, cc, p), axis=1))
    k_bd = per_pair(lambda cc, p: jnp.concatenate(
        [jnp.concatenate([pair_of(kb, cc, p)[0], zero], axis=1),
         jnp.concatenate([zero, pair_of(kb, cc, p)[1]], axis=1)], axis=0))
    kk = _bdot_nt(k_lhs, k_bd)
    mx = jnp.where(strict, bc2 * kk * decay, 0.0)
    tinv = _unit_lower_inverse(mx, eye, blk, left)

    gcum_rows = jnp.concatenate(gcum, axis=0)
    gc = per_head(lambda rs, h: gcum_rows[rs, h:h + 1])
    bc = per_head(lambda rs, h: beta[rs, h:h + 1])
    eg = jnp.exp(gc)
    rhs = jnp.concatenate([v3 * bc, kn * (bc * eg)], axis=-1)
    sol = _bdot(_pair_blockdiag(tinv, left), rhs.astype(BF16).reshape(n_chunks * pairs, 2 * CHUNK, dv + dk))
    sol = sol.reshape(n_chunks * heads, CHUNK, dv + dk)
    w_val = sol[..., :dv]
    k_cd = sol[..., dv:].astype(BF16)
    qk_bd = _pair_blockdiag(_bdot_nt(q_lhs, k_bd) * decay, left)
    q_dec = (qn * eg).astype(BF16)
    g_last = gc[:, CHUNK - 1:CHUNK, :]
    k_tail = (kn * jnp.exp(g_last - gc)).astype(BF16)
    c_decay = jnp.exp(g_last)

    for cc in range(n_chunks):
        hs = slice(cc * heads, (cc + 1) * heads)
        ps = slice(cc * pairs, (cc + 1) * pairs)
        rs = chunk_rows(cc)
        s = s_ref[...]
        sb = s.astype(BF16)
        u = w_val[hs] - _bdot(k_cd[hs], sb)
        ub = u.astype(BF16)
        intra = _bdot(qk_bd[ps], ub.reshape(pairs, 2 * CHUNK, dv))
        o = _bdot(q_dec[hs], sb) + intra.reshape(heads, CHUNK, dv)
        s_ref[...] = s * c_decay[hs] + _bdot_tn(k_tail[hs], ub)

        o = o * lax.rsqrt(jnp.mean(o * o, axis=-1, keepdims=True) + EPS) * on_ref[...]
        for h in range(heads):
            z = z_ref[rs, h * dv:(h + 1) * dv]
            dn_ref[rs, h * dv:(h + 1) * dv] = o[h] * _silu(z)

    @pl.when(c == pl.num_programs(1) - 1)
    def _():
        sfin_ref[0] = s_ref[...]


def deltanet_prompt(proj, col, conv_w, a_log, dt_bias, o_norm, batch, seq, heads, dk, dv, rows):
    conv_ch = 2 * heads * dk + heads * dv
    zw = heads * dv
    steps = seq // rows

    def row_map(width_off):
        return lambda b, c: (b * steps + c, width_off)

    body = functools.partial(_dn_prompt_body, rows=rows, heads=heads, dk=dk, dv=dv)
    return pl.pallas_call(
        body,
        out_shape=(jax.ShapeDtypeStruct((batch * seq, zw), F32),
                   jax.ShapeDtypeStruct((batch, heads, dk, dv), F32)),
        grid=(batch, steps),
        in_specs=[
            pl.BlockSpec((rows, conv_ch), row_map(col["qkv"] // conv_ch)),
            pl.BlockSpec((rows, zw), row_map(col["z"] // zw)),
            pl.BlockSpec((rows, LANE), row_map(col["b"] // LANE)),
            pl.BlockSpec((rows, LANE), row_map(col["a"] // LANE)),
            _resident((CONV_WIDTH, conv_ch)),
            _resident((1, LANE)),
            _resident((1, LANE)),
            _resident((1, dv)),
        ],
        out_specs=(pl.BlockSpec((rows, zw), lambda b, c: (b * steps + c, 0)),
                   pl.BlockSpec((1, heads, dk, dv), lambda b, c: (b, 0, 0, 0))),
        scratch_shapes=[
            pltpu.VMEM((heads, dk, dv), F32),
            pltpu.VMEM((rows + SUBLANE, conv_ch), F32),
            pltpu.VMEM((rows, conv_ch), F32),
        ],
        compiler_params=_params("parallel", "arbitrary"),
        name="deltanet_prompt",
    )(proj, proj, proj, proj, conv_w, a_log, dt_bias, o_norm)


def _dn_sample_body(qkv_ref, z_ref, bl_ref, al_ref, cbuf_ref, st_ref, cw_ref, alog_ref, dtb_ref, on_ref,
                    dn_ref, sto_ref, cbo_ref, act_ref, *, nb, heads, dk, dv):
    wq = heads * dk
    new = qkv_ref[...]
    y = cbuf_ref[:, 0, :] * cw_ref[0:1, :]
    for j in range(1, CONV_WIDTH - 1):
        y = y + cbuf_ref[:, j, :] * cw_ref[j:j + 1, :]
    y = y + new * cw_ref[CONV_WIDTH - 1:CONV_WIDTH, :]
    act_ref[...] = _silu(y)
    for j in range(CONV_WIDTH - 2):
        cbo_ref[:, j, :] = cbuf_ref[:, j + 1, :]
    cbo_ref[:, CONV_WIDTH - 2, :] = new

    beta = jax.nn.sigmoid(bl_ref[...])
    g = -jnp.exp(alog_ref[...]) * jax.nn.softplus(al_ref[...] + dtb_ref[...])
    dec = jnp.exp(g)

    q_t, k_t = [], []
    for h in range(heads):
        q = act_ref[:, h * dk:(h + 1) * dk]
        k = act_ref[:, wq + h * dk:wq + (h + 1) * dk]
        qn = q * lax.rsqrt(jnp.sum(q * q, axis=-1, keepdims=True) + EPS) * (dk ** -0.5)
        kn = k * lax.rsqrt(jnp.sum(k * k, axis=-1, keepdims=True) + EPS)
        q_t.append(qn.T)
        k_t.append(kn.T)

    def per_head(fn):
        return jnp.stack([fn(h) for h in range(heads)], axis=0)

    for i in range(nb):
        row = slice(i, i + 1)
        kcol = per_head(lambda h: k_t[h][:, i:i + 1])
        qcol = per_head(lambda h: q_t[h][:, i:i + 1])
        v = per_head(lambda h: act_ref[row, 2 * wq + h * dv:2 * wq + (h + 1) * dv])
        z = per_head(lambda h: z_ref[row, h * dv:(h + 1) * dv])
        b_i = per_head(lambda h: beta[row, h:h + 1])
        d_i = per_head(lambda h: dec[row, h:h + 1])
        s = st_ref[i] * d_i
        u = b_i * (v - jnp.sum(kcol * s, axis=1, keepdims=True))
        s = s + kcol * u
        o = jnp.sum(qcol * s, axis=1, keepdims=True)
        sto_ref[i] = s
        o = o * lax.rsqrt(jnp.mean(o * o, axis=-1, keepdims=True) + EPS) * on_ref[...]
        o = o * _silu(z)
        for h in range(heads):
            dn_ref[row, h * dv:(h + 1) * dv] = o[h]


def deltanet_sample(proj, col, conv_buf, state, conv_w, a_log, dt_bias, o_norm, heads, dk, dv, nb):
    n_seq = proj.shape[0]
    conv_ch = 2 * heads * dk + heads * dv
    zw = heads * dv
    body = functools.partial(_dn_sample_body, nb=nb, heads=heads, dk=dk, dv=dv)
    return pl.pallas_call(
        body,
        out_shape=(jax.ShapeDtypeStruct((n_seq, zw), F32),
                   jax.ShapeDtypeStruct(state.shape, F32),
                   jax.ShapeDtypeStruct(conv_buf.shape, F32)),
        grid=(n_seq // nb,),
        in_specs=[
            pl.BlockSpec((nb, conv_ch), lambda i: (i, col["qkv"] // conv_ch)),
            pl.BlockSpec((nb, zw), lambda i: (i, col["z"] // zw)),
            pl.BlockSpec((nb, LANE), lambda i: (i, col["b"] // LANE)),
            pl.BlockSpec((nb, LANE), lambda i: (i, col["a"] // LANE)),
            pl.BlockSpec((nb, CONV_WIDTH - 1, conv_ch), lambda i: (i, 0, 0)),
            pl.BlockSpec((nb, heads, dk, dv), lambda i: (i, 0, 0, 0)),
            _resident((CONV_WIDTH, conv_ch)),
            _resident((1, LANE)),
            _resident((1, LANE)),
            _resident((1, dv)),
        ],
        out_specs=(pl.BlockSpec((nb, zw), lambda i: (i, 0)),
                   pl.BlockSpec((nb, heads, dk, dv), lambda i: (i, 0, 0, 0)),
                   pl.BlockSpec((nb, CONV_WIDTH - 1, conv_ch), lambda i: (i, 0, 0))),
        scratch_shapes=[
            pltpu.VMEM((nb, conv_ch), F32),
        ],
        compiler_params=_params("parallel"),
        name="deltanet_sample",
    )(proj, proj, proj, proj, conv_buf, state, conv_w, a_log, dt_bias, o_norm)


def _xattn_prompt_body(q_ref, k_ref, v_ref, o_ref, *, heads, dh):
    for h in range(heads):
        sl = slice(h * dh, (h + 1) * dh)
        q = q_ref[:, sl].astype(BF16)
        k = k_ref[:, sl].astype(BF16)
        v = v_ref[:, sl].astype(BF16)
        s = lax.dot_general(q, k, (((1,), (1,)), ((), ())), preferred_element_type=F32) * (dh ** -0.5)
        e = jnp.exp(s - jnp.max(s, axis=-1, keepdims=True))
        p = e / jnp.sum(e, axis=-1, keepdims=True)
        o_ref[:, sl] = _dot(p.astype(BF16), v)


def xattn_prompt(proj, cq_col, kv, batch, seq, n_mem, heads, dh, tq):
    xw = heads * dh
    steps = seq // tq
    body = functools.partial(_xattn_prompt_body, heads=heads, dh=dh)
    return pl.pallas_call(
        body,
        out_shape=jax.ShapeDtypeStruct((batch * seq, xw), F32),
        grid=(batch, steps),
        in_specs=[
            pl.BlockSpec((tq, xw), lambda b, i: (b * steps + i, cq_col // xw)),
            pl.BlockSpec((n_mem, xw), lambda b, i: (b, 0)),
            pl.BlockSpec((n_mem, xw), lambda b, i: (b, 1)),
        ],
        out_specs=pl.BlockSpec((tq, xw), lambda b, i: (b * steps + i, 0)),
        compiler_params=_params("parallel", "parallel"),
        name="xattn_prompt",
    )(proj, kv, kv)


def _xattn_sample_body(q_ref, k_ref, v_ref, o_ref, *, nb, heads, dh):
    rep = SUBLANE // heads
    n_t = k_ref.shape[2] // SUBLANE

    def over_groups(x, op):
        out = x
        for t in range(1, rep):
            out = op(out, pltpu.roll(x, t * heads, 0))
        return out

    for i in range(nb):
        row = slice(i, i + 1)
        q8 = jnp.concatenate([q_ref[row, h * dh:(h + 1) * dh] for h in range(heads)] * rep, axis=0)
        k3 = k_ref[0, i].reshape(n_t, SUBLANE, dh)
        v3 = v_ref[0, i].reshape(n_t, SUBLANE, dh)
        s = jnp.sum(k3 * q8[None], axis=-1, keepdims=True) * (dh ** -0.5)
        m = over_groups(jnp.max(s, axis=0), jnp.maximum)
        e = jnp.exp(s - m[None])
        den = over_groups(jnp.sum(e, axis=0), jnp.add)
        p = e / den[None]
        acc = over_groups(jnp.sum(p * v3, axis=0), jnp.add)
        for h in range(heads):
            o_ref[row, h * dh:(h + 1) * dh] = acc[h:h + 1, :]


def xattn_sample(proj, cq_col, cache_k, cache_v, layer, nb):
    n_layers, n_seq, n_mem, heads, dh = cache_k.shape
    assert SUBLANE % heads == 0 and (n_mem * heads) % SUBLANE == 0
    xw = heads * dh
    body = functools.partial(_xattn_sample_body, nb=nb, heads=heads, dh=dh)
    cache_spec = pl.BlockSpec((1, nb, n_mem * heads, dh), lambda i: (layer, i, 0, 0))
    flat = (n_layers, n_seq, n_mem * heads, dh)
    return pl.pallas_call(
        body,
        out_shape=jax.ShapeDtypeStruct((n_seq, xw), F32),
        grid=(n_seq // nb,),
        in_specs=[pl.BlockSpec((nb, xw), lambda i: (i, cq_col // xw)), cache_spec, cache_spec],
        out_specs=pl.BlockSpec((nb, xw), lambda i: (i, 0)),
        compiler_params=_params("parallel"),
        name="xattn_sample",
    )(proj, cache_k.reshape(flat), cache_v.reshape(flat))


def _pool_mix(sums, cnts, u_of, pw_ref, ps_ref, o_ref, gw):
    for gi in range(len(POOL_WINDOWS)):
        cols = slice(gi * gw, (gi + 1) * gw)
        d = sums[gi] / cnts[gi] - u_of(cols)
        o_ref[:, cols] = _wdot(d, pw_ref[gi]) * ps_ref[:, cols]


def _pool_prompt_body(x_ref, g_ref, w_ref, pw_ref, ps_ref, o_ref, cq_ref, tail_ref, xbuf, s2, s4, s8, *, tt, gw):
    t = pl.program_id(1)
    n = POOL_HALO + tt
    pw_total = xbuf.shape[1]

    @pl.when(t == 0)
    def _():
        xbuf[0:POOL_HALO, :] = jnp.zeros((POOL_HALO, pw_total), F32)

    proj = _wdot(_rms(x_ref[...], g_ref[...]), w_ref[...])
    cq_ref[...] = proj[:, pw_total:]
    xbuf[POOL_HALO:n, :] = proj[:, 0:pw_total]
    s2[8:n, :] = xbuf[8:n, :] + xbuf[7:n - 1, :]
    s4[16:n, :] = s2[16:n, gw:] + s2[14:n - 2, gw:]
    s8[24:n, :] = s4[24:n, gw:] + s4[20:n - 4, gw:]
    s16 = s8[32:n, gw:] + s8[24:n - 8, gw:]
    sums = [s2[POOL_HALO:n, 0:gw], s4[POOL_HALO:n, 0:gw], s8[POOL_HALO:n, 0:gw], s16]
    pos = t * tt + lax.broadcasted_iota(jnp.int32, (tt, 1), 0)
    cnts = [jnp.minimum(pos + 1, w).astype(F32) for w in POOL_WINDOWS]
    _pool_mix(sums, cnts, lambda cols: xbuf[POOL_HALO:n, cols], pw_ref, ps_ref, o_ref, gw)
    tail_ref[0] = xbuf[tt:n, :]
    xbuf[0:POOL_HALO, :] = xbuf[tt:n, :]


def proj_pool_prompt(x, g, w, pool_w, pool_scale, batch, seq, tt):
    m, d = x.shape
    n_out = w.shape[1]
    pw_total = pool_scale.shape[-1]
    gw = pw_total // len(POOL_WINDOWS)
    steps = seq // tt
    n = POOL_HALO + tt
    body = functools.partial(_pool_prompt_body, tt=tt, gw=gw)
    return pl.pallas_call(
        body,
        out_shape=(jax.ShapeDtypeStruct((m, pw_total), F32),
                   jax.ShapeDtypeStruct((m, n_out - pw_total), F32),
                   jax.ShapeDtypeStruct((batch, POOL_HALO, pw_total), F32)),
        grid=(batch, steps),
        in_specs=[
            pl.BlockSpec((tt, d), lambda b, t: (b * steps + t, 0)),
            _resident((1, d)),
            _resident((d, n_out)),
            _resident(pool_w.shape),
            _resident((1, pw_total)),
        ],
        out_specs=(pl.BlockSpec((tt, pw_total), lambda b, t: (b * steps + t, 0)),
                   pl.BlockSpec((tt, n_out - pw_total), lambda b, t: (b * steps + t, 0)),
                   pl.BlockSpec((1, POOL_HALO, pw_total), lambda b, t: (b, 0, 0))),
        scratch_shapes=[
            pltpu.VMEM((n, pw_total), F32),
            pltpu.VMEM((n, pw_total), F32),
            pltpu.VMEM((n, pw_total - gw), F32),
            pltpu.VMEM((n, pw_total - 2 * gw), F32),
        ],
        compiler_params=_params("parallel", "arbitrary"),
        name="proj_pool_prompt",
    )(x, g.reshape(1, d), w, pool_w, pool_scale)


def _pool_sample_body(u_ref, st_ref, pw_ref, ps_ref, o_ref, sto_ref, *, gw, buf):
    pw_total = gw * len(POOL_WINDOWS)
    new = u_ref[:, 0:pw_total]
    sums, cnts = [], []
    for gi, w in enumerate(POOL_WINDOWS):
        cols = slice(gi * gw, (gi + 1) * gw)
        s = new[:, cols]
        for i in range(1, w):
            s = s + st_ref[:, buf - i, cols]
        sums.append(s)
        cnts.append(float(min(PAST_LEN + 1, w)))
    _pool_mix(sums, cnts, lambda cols: new[:, cols], pw_ref, ps_ref, o_ref, gw)
    sto_ref[:, 0:buf - 1, :] = st_ref[:, 1:buf, :]
    sto_ref[:, buf - 1, :] = new


def pool_sample(proj, state, pool_w, pool_scale):
    n_seq, buf, pw_total = state.shape
    gw = pw_total // len(POOL_WINDOWS)
    body = functools.partial(_pool_sample_body, gw=gw, buf=buf)
    return pl.pallas_call(
        body,
        out_shape=(jax.ShapeDtypeStruct((n_seq, pw_total), F32),
                   jax.ShapeDtypeStruct(state.shape, F32)),
        compiler_params=pltpu.CompilerParams(vmem_limit_bytes=VMEM_LIMIT_BYTES),
        name="pool_sample",
    )(proj, state, pool_w, pool_scale)


def _route_top2(h, g, r, n_experts):
    logits = _wdot(_rms(h, g), r)
    lane = lax.broadcasted_iota(jnp.int32, logits.shape, 1)
    lm = jnp.where(lane < n_experts, logits, -jnp.inf)
    m1 = jnp.max(lm, axis=1, keepdims=True)
    i1 = jnp.min(jnp.where(lm == m1, lane, LANE), axis=1, keepdims=True)
    lm2 = jnp.where(lane == i1, -jnp.inf, lm)
    m2 = jnp.max(lm2, axis=1, keepdims=True)
    i2 = jnp.min(jnp.where(lm2 == m2, lane, LANE), axis=1, keepdims=True)
    e2 = jnp.exp(m2 - m1)
    den = 1.0 + e2
    idx = jnp.where(lane == 0, i1, jnp.where(lane == 1, i2, 0))
    gate = jnp.where(lane == 0, 1.0 / den, jnp.where(lane == 1, e2 / den, 0.0))
    return idx, gate


def _dispatch_body(dest_ref, h_ref, gate_ref, zeros_hbm, o_hbm, ext, sem, *, tb, n):
    del zeros_hbm
    i = pl.program_id(0)
    slot = i % 2
    groups = tb // SUBLANE
    d = h_ref.shape[1]

    def row_copy(s, t, u, row):
        return pltpu.make_async_copy(ext.at[s, t, pl.ds(u, 1)], o_hbm.at[pl.ds(row, 1)], sem.at[s])

    def wait_slot(s):
        def body(t, carry):
            pltpu.make_async_copy(ext.at[s, t], o_hbm.at[pl.ds(0, SUBLANE)], sem.at[s]).wait()
            return carry
        lax.fori_loop(0, groups, body, 0)

    @pl.when(i >= 2)
    def _():
        wait_slot(slot)

    ext[slot, :, :, 0:d] = h_ref[...].reshape(groups, SUBLANE, d)
    ext[slot, :, :, d:d + LANE] = gate_ref[...].reshape(groups, SUBLANE, LANE)

    def body(t, carry):
        for u in range(SUBLANE):
            row_copy(slot, t, u, dest_ref[0, 0, t * SUBLANE + u]).start(priority=u % 2)
        return carry
    lax.fori_loop(0, groups, body, 0)

    @pl.when(i == n - 1)
    def _():
        wait_slot(slot)
        if n >= 2:
            wait_slot(1 - slot)


def moe_dispatch(h_all, row_offset, rows, gates, dest, into, tb, name):
    d = h_all.shape[1]
    off = row_offset // tb
    n = rows // tb
    body = functools.partial(_dispatch_body, tb=tb, n=n)
    return pl.pallas_call(
        body,
        out_shape=jax.ShapeDtypeStruct(into.shape, F32),
        grid=(n,),
        in_specs=[
            pl.BlockSpec((1, 1, tb), lambda i: (i, 0, 0), memory_space=pltpu.SMEM),
            pl.BlockSpec((tb, d), lambda i: (i + off, 0)),
            pl.BlockSpec((tb, LANE), lambda i: (i, 0)),
            pl.BlockSpec(memory_space=pl.ANY),
        ],
        out_specs=pl.BlockSpec(memory_space=pl.ANY),
        scratch_shapes=[pltpu.VMEM((2, tb // SUBLANE, SUBLANE, d + LANE), F32), pltpu.SemaphoreType.DMA((2,))],
        input_output_aliases={3: 0},
        compiler_params=_params("arbitrary"),
        name=name,
    )(dest.reshape(n, 1, tb), h_all, gates, into)


def _experts_body(te_ref, nu_ref, nb_ref, fill_ref, g_first_ref, g_next_ref, s_cur_ref,
                  h_hbm, g_ref, wg_ref, wu_ref, wo_ref, y_hbm,
                  xbuf, acc, xn_ref, gsem, ssem, *, tm, block, n_rows):
    i = pl.program_id(0)
    j = pl.program_id(1)
    n_tiles = pl.num_programs(0)
    nj = pl.num_programs(1)
    n_used = nu_ref[0]
    slot = i % 2
    valid = i < n_used
    chunks = tm // SUBLANE
    d = xn_ref.shape[1]

    def gather(idx_ref, s):
        def body(t, carry):
            row0 = pl.multiple_of(idx_ref[0, 0, t], SUBLANE)
            pltpu.make_async_copy(h_hbm.at[pl.ds(row0, SUBLANE), pl.ds(0, d)], xbuf.at[s, t], gsem.at[s]).start()
            return carry
        lax.fori_loop(0, chunks, body, 0)

    def wait_gather(s):
        def body(t, carry):
            pltpu.make_async_copy(h_hbm.at[pl.ds(0, SUBLANE), pl.ds(0, d)], xbuf.at[s, t], gsem.at[s]).wait()
            return carry
        lax.fori_loop(0, chunks, body, 0)

    def scatter():
        def body(t, carry):
            row0 = pl.multiple_of(s_cur_ref[0, 0, t], SUBLANE)
            pltpu.make_async_copy(acc.at[t], y_hbm.at[pl.ds(row0, SUBLANE)], ssem.at[0]).start()
            return carry
        lax.fori_loop(0, chunks, body, 0)

    def wait_scatter():
        def body(t, carry):
            pltpu.make_async_copy(acc.at[t], y_hbm.at[pl.ds(0, SUBLANE)], ssem.at[0]).wait()
            return carry
        lax.fori_loop(0, chunks, body, 0)

    @pl.when(j == 0)
    def _():
        @pl.when(i == 0)
        def _():
            gather(g_first_ref, 0)
            acc[...] = jnp.zeros_like(acc)
            first_free = nb_ref[0] * (block // SUBLANE)
            n_free = n_rows // SUBLANE - first_free

            def zero_copy(row0):
                return pltpu.make_async_copy(acc.at[0], y_hbm.at[pl.ds(pl.multiple_of(row0, SUBLANE), SUBLANE)],
                                             ssem.at[0])

            def fill(c, carry):
                zero_copy((first_free + c) * SUBLANE).start()
                zero_copy(n_rows + (first_free + c) * SUBLANE).start()
                return carry
            lax.fori_loop(0, n_free, fill, 0)

            def fill_spare(c, carry):
                zero_copy(2 * n_rows + c * SUBLANE).start()
                return carry
            lax.fori_loop(0, chunks, fill_spare, 0)

            def drain(c, carry):
                zero_copy(0).wait()
                return carry
            lax.fori_loop(0, 2 * n_free + chunks, drain, 0)

        @pl.when(i + 1 < n_used)
        def _():
            gather(g_next_ref, 1 - slot)

        @pl.when(valid)
        def _():
            wait_gather(slot)
            xn_ref[...] = _rms(xbuf[slot].reshape(tm, d), g_ref[...]).astype(BF16)

        @pl.when((i >= 1) & (i <= n_used))
        def _():
            wait_scatter()

        @pl.when(valid)
        def _():
            acc[...] = jnp.zeros_like(acc)

    def ff_step(n_chunks):
        rows = n_chunks * SUBLANE
        xn = xn_ref[0:rows, :]
        a = _dot(xn, wg_ref[0].astype(BF16))
        b = _dot(xn, wu_ref[0].astype(BF16))
        part = _dot((_silu(a) * b).astype(BF16), wo_ref[0].astype(BF16)).reshape(n_chunks, SUBLANE, d)
        acc[0:n_chunks] += part

    fill = fill_ref[i]

    @pl.when(valid & (fill > chunks // 2))
    def _():
        ff_step(chunks)

    @pl.when(valid & (fill > chunks // 4) & (fill <= chunks // 2))
    def _():
        ff_step(chunks // 2)

    @pl.when(valid & (fill <= chunks // 4))
    def _():
        ff_step(chunks // 4)

    @pl.when(valid & (j == nj - 1))
    def _():
        scatter()

        @pl.when(i == n_tiles - 1)
        def _():
            wait_scatter()


def moe_experts(h_ps, g, wi, wo, gather_rows, scatter_rows, tile_expert, n_used, n_blocks_used, tile_fill,
                block, n_rows, tm, tf):
    d = wo.shape[2]
    f = wo.shape[1]
    nj = f // tf
    n_tiles = gather_rows.shape[0]
    chunks = tm // SUBLANE

    def wblock(col_off):
        def index_map(i, j, te, nu, nb, tf_):
            jj = jnp.where(i < nu[0], j, nj - 1)
            return (te[i], 0, col_off + jj)
        return index_map

    def woblock(i, j, te, nu, nb, tf_):
        return (te[i], jnp.where(i < nu[0], j, nj - 1), 0)

    def smem_tile(index_map):
        return pl.BlockSpec((1, 1, chunks), index_map, memory_space=pltpu.SMEM)

    grid_spec = pltpu.PrefetchScalarGridSpec(
        num_scalar_prefetch=4,
        grid=(n_tiles, nj),
        in_specs=[
            smem_tile(lambda i, j, te, nu, nb, tf_: (0, 0, 0)),
            smem_tile(lambda i, j, te, nu, nb, tf_: (jnp.minimum(i + 1, n_tiles - 1), 0, 0)),
            smem_tile(lambda i, j, te, nu, nb, tf_: (i, 0, 0)),
            pl.BlockSpec(memory_space=pl.ANY),
            pl.BlockSpec((1, d), lambda i, j, te, nu, nb, tf_: (0, 0)),
            pl.BlockSpec((1, d, tf), wblock(0)),
            pl.BlockSpec((1, d, tf), wblock(nj)),
            pl.BlockSpec((1, tf, d), woblock),
        ],
        out_specs=pl.BlockSpec(memory_space=pl.ANY),
        scratch_shapes=[
            pltpu.VMEM((2, chunks, SUBLANE, d), F32),
            pltpu.VMEM((chunks, SUBLANE, d), F32),
            pltpu.VMEM((tm, d), BF16),
            pltpu.SemaphoreType.DMA((2,)),
            pltpu.SemaphoreType.DMA((1,)),
        ],
    )
    return pl.pallas_call(
        functools.partial(_experts_body, tm=tm, block=block, n_rows=n_rows),
        out_shape=jax.ShapeDtypeStruct((2 * n_rows + tm, d), F32),
        grid_spec=grid_spec,
        compiler_params=_params("arbitrary", "arbitrary"),
        name="moe_experts",
    )(tile_expert, n_used, n_blocks_used, tile_fill, gather_rows, gather_rows, scatter_rows, h_ps,
      g.reshape(1, d), wi, wi, wo)


def _combine_body(nb_ref, hx_ref, y1_ref, y2_ref, fn_ref, o_ref, *, d, final_norm):
    @pl.when(pl.program_id(0) < nb_ref[0])
    def _():
        hx = hx_ref[...]
        out = hx[:, 0:d] + (hx[:, d:d + 1] * y1_ref[...] + hx[:, d + 1:d + 2] * y2_ref[...])
        if final_norm:
            out = _rms(out, fn_ref[...])
        o_ref[...] = out

    @pl.when(pl.program_id(0) >= nb_ref[0])
    def _():
        o_ref[...] = jnp.zeros_like(o_ref)


def moe_combine(h_ps, y, n_blocks_used, choice_stride, fn, tb, final_norm, name):
    rows, dx = h_ps.shape
    d = y.shape[1]
    off2 = choice_stride // tb

    def used(i, nb):
        return jnp.minimum(i, nb[0] - 1)

    grid_spec = pltpu.PrefetchScalarGridSpec(
        num_scalar_prefetch=1,
        grid=(rows // tb,),
        in_specs=[
            pl.BlockSpec((tb, dx), lambda i, nb: (used(i, nb), 0)),
            pl.BlockSpec((tb, d), lambda i, nb: (used(i, nb), 0)),
            pl.BlockSpec((tb, d), lambda i, nb: (used(i, nb) + off2, 0)),
            pl.BlockSpec((1, d), lambda i, nb: (0, 0)),
        ],
        out_specs=pl.BlockSpec((tb, d), lambda i, nb: (i, 0)),
    )
    return pl.pallas_call(
        functools.partial(_combine_body, d=d, final_norm=final_norm),
        out_shape=jax.ShapeDtypeStruct((rows, d), F32),
        grid_spec=grid_spec,
        compiler_params=_params("arbitrary"),
        name=name,
    )(n_blocks_used, h_ps, y, y, fn.reshape(1, d))


def _unpermute_body(cur_ref, nxt_ref, y_hbm, o_ref, buf, sem, *, tb):
    i = pl.program_id(0)
    n = pl.num_programs(0)
    slot = i % 2
    groups = tb // SUBLANE

    def start(idx_ref, s):
        def body(t, carry):
            for u in range(SUBLANE):
                row = idx_ref[0, 0, t * SUBLANE + u]
                pltpu.make_async_copy(y_hbm.at[pl.ds(row, 1)], buf.at[s, t, pl.ds(u, 1)], sem.at[s]).start(
                    priority=u % 2)
            return carry
        lax.fori_loop(0, groups, body, 0)

    @pl.when(i == 0)
    def _():
        start(cur_ref, 0)

    @pl.when(i + 1 < n)
    def _():
        start(nxt_ref, 1 - slot)

    def wait_body(t, carry):
        pltpu.make_async_copy(y_hbm.at[pl.ds(0, SUBLANE)], buf.at[slot, t], sem.at[slot]).wait()
        return carry
    lax.fori_loop(0, groups, wait_body, 0)
    o_ref[...] = buf[slot].reshape(o_ref.shape)


def moe_unpermute(y_ps, dest, tb, name):
    rows = dest.shape[0]
    d = y_ps.shape[1]
    n = rows // tb
    idx = dest.reshape(n, 1, tb)

    def smem_tile(index_map):
        return pl.BlockSpec((1, 1, tb), index_map, memory_space=pltpu.SMEM)

    return pl.pallas_call(
        functools.partial(_unpermute_body, tb=tb),
        out_shape=jax.ShapeDtypeStruct((rows, d), F32),
        grid=(n,),
        in_specs=[smem_tile(lambda i: (i, 0, 0)), smem_tile(lambda i: (jnp.minimum(i + 1, n - 1), 0, 0)),
                  pl.BlockSpec(memory_space=pl.ANY)],
        out_specs=pl.BlockSpec((tb, d), lambda i: (i, 0)),
        scratch_shapes=[pltpu.VMEM((2, tb // SUBLANE, SUBLANE, d), F32), pltpu.SemaphoreType.DMA((2,))],
        compiler_params=_params("arbitrary"),
        name=name,
    )(idx, idx, y_ps)


def moe_plan(idx, n_experts, tm, block):
    n_tok, k = idx.shape
    assert k == 2
    n_pairs = n_experts * n_experts
    chunks_per_tile = tm // SUBLANE
    n_rows = -(-(n_tok + n_pairs * (SUBLANE - 1)) // block) * block
    n_chunks = n_rows // SUBLANE

    pair = idx[:, 0] * n_experts + idx[:, 1]
    onehot = (pair[:, None] == jnp.arange(n_pairs, dtype=jnp.int32)[None, :]).astype(jnp.int32)
    csum = jnp.cumsum(onehot, axis=0)
    counts = csum[-1]
    rank = jnp.sum(csum * onehot, axis=1) - 1
    padded = (counts + SUBLANE - 1) // SUBLANE * SUBLANE
    total = jnp.sum(padded)
    last_group = jnp.max(jnp.where(counts > 0, jnp.arange(n_pairs, dtype=jnp.int32), 0))
    padded = padded + jnp.where(jnp.arange(n_pairs) == last_group, (-total) % block, 0)
    group_end = jnp.cumsum(padded)
    group_start = group_end - padded
    dest = jnp.sum(onehot * group_start[None, :], axis=1) + rank
    rows_used = group_end[-1]

    chunk_row = jnp.arange(n_chunks, dtype=jnp.int32) * SUBLANE
    chunk_group = jnp.sum((chunk_row[:, None] >= group_end[None, :]).astype(jnp.int32), axis=1)
    chunk_used = chunk_row < rows_used
    chunk_group = jnp.minimum(chunk_group, n_pairs - 1)
    chunk_experts = jnp.stack([chunk_group // n_experts, chunk_group % n_experts], axis=1)
    experts = jnp.arange(n_experts, dtype=jnp.int32)
    member = ((chunk_experts[:, :, None] == experts[None, None, :]) & chunk_used[:, None, None])
    member = member.astype(jnp.int32)
    member_e = jnp.sum(member, axis=1)
    csum_e = jnp.cumsum(member_e, axis=0)
    counts_e = csum_e[-1]
    tiles_e = (counts_e + chunks_per_tile - 1) // chunks_per_tile
    tile_end = jnp.cumsum(tiles_e)
    start_e = (tile_end - tiles_e) * chunks_per_tile
    pos = jnp.sum(member * (start_e[None, None, :] + csum_e[:, None, :] - 1), axis=2)
    n_tiles = 2 * n_chunks // chunks_per_tile + n_experts
    n_pos = n_tiles * chunks_per_tile
    pos = jnp.where(chunk_used[:, None], pos, n_pos)
    entry = 2 * jnp.arange(n_chunks, dtype=jnp.int32)[:, None] + jnp.arange(2, dtype=jnp.int32)[None, :] + 1
    table = jnp.zeros((n_pos,), jnp.int32).at[pos.reshape(-1)].set(entry.reshape(-1), mode="drop")
    table = table.reshape(n_tiles, chunks_per_tile)
    filled = table > 0
    src_chunk = (table - 1) // 2
    choice = (table - 1) % 2
    spare = 2 * n_rows + jnp.arange(chunks_per_tile, dtype=jnp.int32) * SUBLANE
    gather_rows = jnp.where(filled, src_chunk * SUBLANE, 0)
    scatter_rows = jnp.where(filled, choice * n_rows + src_chunk * SUBLANE, spare[None, :])

    n_used = tile_end[-1]
    tile_ids = jnp.arange(n_tiles, dtype=jnp.int32)
    tile_expert = jnp.sum((tile_ids[:, None] >= tile_end[None, :]).astype(jnp.int32), axis=1)
    last_expert = jnp.sum((n_used - 1 >= tile_end).astype(jnp.int32))
    tile_expert = jnp.where(tile_ids < n_used, tile_expert, last_expert).astype(jnp.int32)
    return (dest.astype(jnp.int32),
            gather_rows.reshape(n_tiles, 1, chunks_per_tile).astype(jnp.int32),
            scatter_rows.reshape(n_tiles, 1, chunks_per_tile).astype(jnp.int32),
            tile_expert, n_used.reshape(1).astype(jnp.int32),
            (rows_used // block).reshape(1).astype(jnp.int32),
            jnp.sum(filled.astype(jnp.int32), axis=1), n_rows)


def _pad_lanes(v):
    return jnp.zeros((1, LANE), F32).at[0, :v.shape[0]].set(v)


def kernel(x_prompt, x_sample, state_dn, state_dn_conv, state_pool, cache_mem_k, cache_mem_v, mem_prompt, a_norm1, a_w_in, a_conv_w, a_a_log, a_dt_bias, a_o_norm, a_w_out, a_norm2, a_ffn_wi, a_ffn_wo, b_norm1, b_w_in, b_pool_w, b_pool_scale, b_w_out, b_norm2, b_router, b_moe_wi, b_moe_wo, m_norm, m_w_k, m_w_v, final_norm):
    bp, seq, d = x_prompt.shape
    bs = x_sample.shape[0]
    assert x_sample.shape[1] == 1
    depth = m_norm.shape[0]
    assert depth % 2 == 0, "the output norm is fused into the expert block of the last (odd) layer"
    heads, dk, dv = state_dn.shape[2], state_dn.shape[3], state_dn.shape[4]
    conv_ch = state_dn_conv.shape[-1]
    qk_w, v_w = heads * dk, heads * dv
    assert conv_ch == 2 * qk_w + v_w and dk == LANE and dv == LANE and heads <= LANE
    n_mem, x_heads, x_dh = cache_mem_k.shape[2], cache_mem_k.shape[3], cache_mem_k.shape[4]
    xw = x_heads * x_dh
    pool_w_total = state_pool.shape[-1]
    pool_buf = state_pool.shape[2]

    hp = x_prompt.reshape(bp * seq, d)
    hs = x_sample.reshape(bs, d)

    o1, o2, o3, o4 = conv_ch, conv_ch + v_w, conv_ch + v_w + heads, conv_ch + v_w + 2 * heads
    col_a = {"qkv": 0, "z": conv_ch, "cq": conv_ch + v_w, "b": conv_ch + v_w + xw, "a": conv_ch + v_w + xw + LANE}
    n_a = col_a["a"] + LANE

    dn_p, conv_p, pool_p = [], [], []
    mk_all = mv_all = None
    dn_s, conv_s, pool_s = [], [], []
    for i in range(depth):
        j = i // 2
        w_kv = jnp.concatenate([m_w_k[i], m_w_v[i]], axis=1).astype(BF16)
        kv, mk_all, mv_all = mem_kv(mem_prompt.reshape(bp * n_mem, d), m_norm[i], w_kv, x_heads, x_dh, ROWS_MEM,
                                    f"mem_kv_{i}", i, depth, into=None if i == 0 else (mk_all, mv_all))
        if i % 2 == 0:
            w = a_w_in[j]
            def regroup(t):
                rows = t.shape[0]
                return jnp.concatenate(
                    [t[:, :o2], t[:, o4:], t[:, o2:o3], jnp.zeros((rows, LANE - heads), t.dtype),
                     t[:, o3:o4], jnp.zeros((rows, n_a - col_a["a"] - heads), t.dtype)], axis=1)

            assert col_a["b"] == o2 + xw
            a_log, dt_b = _pad_lanes(a_a_log[j]), _pad_lanes(a_dt_bias[j])
            o_norm = a_o_norm[j].reshape(1, dv)

            proj_p = rms_matmul(hp, a_norm1[j], regroup(w.astype(BF16)), ROWS_IN_A, f"a_in_p_{i}")
            proj_s = regroup(rms_matmul(hs, a_norm1[j], w, bs, f"a_in_s_{i}"))
            mix_p, s_p = deltanet_prompt(proj_p, col_a, a_conv_w[j], a_log, dt_b, o_norm,
                                         bp, seq, heads, dk, dv, rows=DN_STEP_CHUNKS * CHUNK)
            mix_s, s_s, c_s = deltanet_sample(proj_s, col_a, state_dn_conv[j], state_dn[j], a_conv_w[j],
                                              a_log, dt_b, o_norm, heads, dk, dv, nb=SUBLANE)
            xa_p = xattn_prompt(proj_p, col_a["cq"], kv, bp, seq, n_mem, x_heads, x_dh, tq=min(ROWS_XATTN, seq))
            xa_s = xattn_sample(proj_s, col_a["cq"], cache_mem_k, cache_mem_v, i, nb=SUBLANE)
            hp = out_proj_ffn(hp, mix_p, xa_p, a_w_out[j].astype(BF16), a_norm2[j], a_ffn_wi[j].astype(BF16),
                              a_ffn_wo[j].astype(BF16), ROWS_OUT_FFN, f"a_out_ffn_p_{i}")
            hs = out_proj(hs, mix_s, xa_s, a_w_out[j], bs, f"a_out_s_{i}")
            hs = ffn_dense(hs, a_norm2[j], a_ffn_wi[j], a_ffn_wo[j], bs, f"a_ffn_s_{i}", tf=FF_TILE_SAMPLE)
            dn_p.append(s_p)
            conv_p.append(proj_p.reshape(bp, seq, n_a)[:, seq - (CONV_WIDTH - 1):, :conv_ch])
            dn_s.append(s_s)
            conv_s.append(c_s)
        else:
            ps = b_pool_scale[j].reshape(1, pool_w_total)
            last = i == depth - 1

            mix_p, cq_p, pool_tail = proj_pool_prompt(hp, b_norm1[j], b_w_in[j].astype(BF16),
                                                      b_pool_w[j].astype(BF16), ps, bp, seq, tt=min(ROWS_STREAM, seq))
            proj_s = rms_matmul(hs, b_norm1[j], b_w_in[j], bs, f"b_in_s_{i}")
            mix_s, buf_s = pool_sample(proj_s, state_pool[j], b_pool_w[j], ps)
            xa_p = xattn_prompt(cq_p, 0, kv, bp, seq, n_mem, x_heads, x_dh, tq=min(ROWS_XATTN, seq))
            xa_s = xattn_sample(proj_s, pool_w_total, cache_mem_k, cache_mem_v, i, nb=SUBLANE)
            n_p = bp * seq
            n_experts = b_router.shape[-1]
            r_pad = jnp.concatenate([b_router[j], jnp.zeros((d, LANE - n_experts), F32)], axis=1)
            h_all, idx_p, gate_p = out_proj(hp, mix_p, xa_p, b_w_out[j].astype(BF16), ROWS_STREAM, f"b_out_p_{i}",
                                            out_rows=n_p + bs, route=(b_norm2[j], r_pad.astype(BF16), n_experts))
            h_all, idx_s, gate_s = out_proj(hs, mix_s, xa_s, b_w_out[j], bs, f"b_out_s_{i}", out_rows=n_p + bs,
                                            into=h_all, row_offset=n_p, route=(b_norm2[j], r_pad, n_experts))
            idx = jnp.concatenate([idx_p[:n_p, :TOP_K], idx_s[:, :TOP_K]], axis=0)
            (dest, gather_rows, scatter_rows, tile_expert, n_used, n_blocks_used, tile_fill, n_rows) = moe_plan(
                idx, n_experts, MOE_TILE, MOE_ROW_BLOCK)
            h_ps = jnp.zeros((n_rows, d + LANE), F32)
            h_ps = moe_dispatch(h_all, 0, n_p, gate_p, dest[:n_p], h_ps, ROWS_STREAM, f"b_dispatch_p_{i}")
            h_ps = moe_dispatch(h_all, n_p, bs, gate_s, dest[n_p:], h_ps, bs, f"b_dispatch_s_{i}")
            y = moe_experts(h_ps, b_norm2[j], b_moe_wi[j], b_moe_wo[j], gather_rows, scatter_rows, tile_expert,
                            n_used, n_blocks_used, tile_fill, MOE_ROW_BLOCK, n_rows, MOE_TILE, MOE_FF_TILE)
            y_ps = moe_combine(h_ps, y, n_blocks_used, n_rows, final_norm, MOE_ROW_BLOCK, last,
                               f"b_combine_{i}")
            hp = moe_unpermute(y_ps, dest[:n_p], ROWS_STREAM, f"b_unpermute_p_{i}")
            hs = moe_unpermute(y_ps, dest[n_p:], bs, f"b_unpermute_s_{i}")
            pool_p.append(pool_tail[:, POOL_HALO - pool_buf:, :])
            pool_s.append(buf_s)

    y_prompt = hp.reshape(bp, seq, d)
    y_sample = hs.reshape(bs, 1, d)
    return (y_prompt, y_sample, jnp.stack(dn_p), jnp.stack(conv_p), jnp.stack(pool_p),
            mk_all.reshape(depth, bp, n_mem, x_heads, x_dh), mv_all.reshape(depth, bp, n_mem, x_heads, x_dh),
            jnp.stack(dn_s), jnp.stack(conv_s), jnp.stack(pool_s))
```

```python
import functools

import jax
import jax.numpy as jnp
from jax import lax
from jax.experimental import pallas as pl
from jax.experimental.pallas import tpu as pltpu

F32 = jnp.float32
BF16 = jnp.bfloat16
HIGHEST = lax.Precision.HIGHEST

EPS = 1e-6
CHUNK = 64
CONV_WIDTH = 4
POOL_WINDOWS = (2, 4, 8, 16)
PAST_LEN = 16384
TOP_K = 2

LANE = 128
SUBLANE = 8
VMEM_LIMIT_BYTES = 56 * 2**20

ROWS_IN_A = 512
ROWS_OUT_FFN = 512
ROWS_STREAM = 1024
ROWS_MEM = 256
ROWS_XATTN = 2048
FF_TILE_SAMPLE = 256
MOE_TILE = 1024
MOE_FF_TILE = 512
MOE_ROW_BLOCK = 512
DN_STEP_CHUNKS = 8
INV_BLOCK = 16
POOL_HALO = 32


def _params(*sem):
    return pltpu.CompilerParams(dimension_semantics=sem, vmem_limit_bytes=VMEM_LIMIT_BYTES)


def _resident(shape):
    nd = len(shape)
    return pl.BlockSpec(shape, lambda *_: (0,) * nd, pipeline_mode=pl.Buffered(1))


def _rms(x, w):
    return x * lax.rsqrt(jnp.mean(x * x, axis=-1, keepdims=True) + EPS) * w


def _silu(x):
    half = 0.5 * x
    return half + half * jnp.tanh(half)


def _dot(a, b):
    return jnp.dot(a, b, preferred_element_type=F32)


def _wdot(a, w):
    if w.dtype == F32:
        return jnp.dot(a, w, precision=HIGHEST, preferred_element_type=F32)
    return jnp.dot(a.astype(BF16), w, preferred_element_type=F32)


def _bdot(a, b):
    return lax.dot_general(a, b, (((2,), (1,)), ((0,), (0,))), preferred_element_type=F32)


def _bdot_nt(a, b):
    return lax.dot_general(a, b, (((2,), (2,)), ((0,), (0,))), preferred_element_type=F32)


def _bdot_tn(a, b):
    return lax.dot_general(a, b, (((1,), (1,)), ((0,), (0,))), preferred_element_type=F32)


def _rms_matmul_body(x_ref, g_ref, w_ref, o_ref):
    xn = _rms(x_ref[...], g_ref[...])
    o_ref[...] = _wdot(xn, w_ref[...])


def _wspec(shape, index_map, steps):
    if steps == 1:
        return pl.BlockSpec(shape, index_map, pipeline_mode=pl.Buffered(1))
    return pl.BlockSpec(shape, index_map)


def rms_matmul(x, g, w, tm, name, tn=None):
    m, d = x.shape
    n = w.shape[1]
    tn = n if tn is None else tn
    nj = n // tn
    return pl.pallas_call(
        _rms_matmul_body,
        out_shape=jax.ShapeDtypeStruct((m, n), F32),
        grid=(m // tm, nj),
        in_specs=[pl.BlockSpec((tm, d), lambda i, j: (i, 0)), _resident((1, d)),
                  _wspec((d, tn), lambda i, j: (0, j), nj)],
        out_specs=pl.BlockSpec((tm, tn), lambda i, j: (i, j)),
        compiler_params=_params("parallel", "arbitrary"),
        name=name,
    )(x, g.reshape(1, d), w)


def _out_proj_body(x_ref, a1_ref, a2_ref, w1_ref, w2_ref, *rest, n_main, n_experts, has_into):
    route = n_experts is not None
    if route:
        g_ref, r_ref = rest[0], rest[1]
        o_ref, idx_ref, gate_ref = rest[-3:]
    else:
        o_ref = rest[-1]
    del has_into

    @pl.when(pl.program_id(0) < n_main)
    def _():
        acc = _wdot(a1_ref[...], w1_ref[...]) + _wdot(a2_ref[...], w2_ref[...])
        h = x_ref[...] + acc
        o_ref[...] = h
        if route:
            idx_ref[...], gate_ref[...] = _route_top2(h, g_ref[...], r_ref[...], n_experts)

    @pl.when(pl.program_id(0) >= n_main)
    def _():
        o_ref[...] = jnp.zeros_like(o_ref)
        if route:
            idx_ref[...] = jnp.zeros_like(idx_ref)
            gate_ref[...] = jnp.zeros_like(gate_ref)


def out_proj(x, a1, a2, w, tm, name, out_rows=None, into=None, row_offset=0, route=None):
    m, d = x.shape
    k1, k2 = a1.shape[1], a2.shape[1]
    assert w.shape[0] == k1 + k2 and k1 % k2 == 0 and row_offset % tm == 0
    out_rows = m if out_rows is None else out_rows
    off = row_offset // tm
    n_main = m // tm
    n_extra = 0 if (into is not None or out_rows == m) else 1
    assert out_rows - m <= n_extra * tm or into is not None
    last = n_main - 1
    in_specs = [
        pl.BlockSpec((tm, d), lambda i: (jnp.minimum(i, last), 0)),
        pl.BlockSpec((tm, k1), lambda i: (jnp.minimum(i, last), 0)),
        pl.BlockSpec((tm, k2), lambda i: (jnp.minimum(i, last), 0)),
        pl.BlockSpec((k1, d), lambda i: (0, 0), pipeline_mode=pl.Buffered(1)),
        pl.BlockSpec((k2, d), lambda i: (k1 // k2, 0), pipeline_mode=pl.Buffered(1)),
    ]
    args = [x, a1, a2, w, w]
    out_shape = [jax.ShapeDtypeStruct((out_rows, d), F32)]
    out_specs = [pl.BlockSpec((tm, d), lambda i: (i + off, 0))]
    n_experts = None
    if route is not None:
        g, r_pad, n_experts = route
        in_specs += [_resident((1, d)), _resident((d, LANE))]
        args += [g.reshape(1, d), r_pad]
        n_steps = n_main + n_extra
        out_shape += [jax.ShapeDtypeStruct((n_steps * tm, LANE), jnp.int32),
                      jax.ShapeDtypeStruct((n_steps * tm, LANE), F32)]
        out_specs += [pl.BlockSpec((tm, LANE), lambda i: (i, 0))] * 2
    aliases = {}
    if into is not None:
        in_specs.append(pl.BlockSpec(memory_space=pl.ANY))
        args.append(into)
        aliases = {len(args) - 1: 0}
    outs = pl.pallas_call(
        functools.partial(_out_proj_body, n_main=n_main, n_experts=n_experts, has_into=into is not None),
        out_shape=tuple(out_shape),
        grid=(n_main + n_extra,),
        in_specs=in_specs,
        out_specs=tuple(out_specs),
        input_output_aliases=aliases,
        compiler_params=_params("parallel"),
        name=name,
    )(*args)
    return outs if route is not None else outs[0]


def _ffn_body(h_ref, g_ref, wg_ref, wu_ref, wo_ref, o_ref, xn_ref):
    @pl.when(pl.program_id(1) == 0)
    def _():
        h = h_ref[...]
        xn_ref[...] = _rms(h, g_ref[...])
        o_ref[...] = h

    xn = xn_ref[...]
    a = _wdot(xn, wg_ref[...])
    b = _wdot(xn, wu_ref[...])
    o_ref[...] += _wdot(_silu(a) * b, wo_ref[...])


def ffn_dense(h, g, wi, wo, tm, name, tf=None):
    m, d = h.shape
    f = wo.shape[0]
    tf = f if tf is None else tf
    nj = f // tf
    return pl.pallas_call(
        _ffn_body,
        out_shape=jax.ShapeDtypeStruct((m, d), F32),
        grid=(m // tm, nj),
        in_specs=[
            pl.BlockSpec((tm, d), lambda i, j: (i, 0)),
            _resident((1, d)),
            _wspec((d, tf), lambda i, j: (0, j), nj),
            _wspec((d, tf), lambda i, j: (0, nj + j), nj),
            _wspec((tf, d), lambda i, j: (j, 0), nj),
        ],
        out_specs=pl.BlockSpec((tm, d), lambda i, j: (i, 0)),
        scratch_shapes=[pltpu.VMEM((tm, d), F32)],
        compiler_params=_params("parallel", "arbitrary"),
        name=name,
    )(h, g.reshape(1, d), wi, wi, wo)


def _mem_kv_body(x_ref, g_ref, w_ref, *rest, heads, dh, n_own):
    kv_ref, k_ref, v_ref = rest[-3:]
    xw = heads * dh

    @pl.when(pl.program_id(0) < n_own)
    def _():
        res = _wdot(_rms(x_ref[...], g_ref[...]), w_ref[...])
        kv_ref[...] = res
        for h in range(heads):
            k_ref[:, h, :] = res[:, h * dh:(h + 1) * dh]
            v_ref[:, h, :] = res[:, xw + h * dh:xw + (h + 1) * dh]

    @pl.when(pl.program_id(0) >= n_own)
    def _():
        k_ref[...] = jnp.zeros_like(k_ref)
        v_ref[...] = jnp.zeros_like(v_ref)


def mem_kv(mem, g, w_kv, heads, dh, tm, name, layer, n_layers, into=None):
    m, d = mem.shape
    xw = heads * dh
    n_own = m // tm
    assert (into is None) == (layer == 0)
    n_steps = n_own * n_layers if into is None else n_own
    head_shape = jax.ShapeDtypeStruct((n_layers * m, heads, dh), F32)
    head_spec = pl.BlockSpec((tm, heads, dh), lambda i: (i + layer * n_own, 0, 0))
    own = lambda i: (jnp.minimum(i, n_own - 1), 0)
    in_specs = [pl.BlockSpec((tm, d), own), _resident((1, d)), _resident((d, 2 * xw))]
    args = [mem, g.reshape(1, d), w_kv]
    aliases = {}
    if into is not None:
        in_specs += [pl.BlockSpec(memory_space=pl.ANY)] * 2
        args += list(into)
        aliases = {3: 1, 4: 2}
    return pl.pallas_call(
        functools.partial(_mem_kv_body, heads=heads, dh=dh, n_own=n_own),
        out_shape=(jax.ShapeDtypeStruct((m, 2 * xw), F32), head_shape, head_shape),
        grid=(n_steps,),
        in_specs=in_specs,
        out_specs=(pl.BlockSpec((tm, 2 * xw), own), head_spec, head_spec),
        input_output_aliases=aliases,
        compiler_params=_params("arbitrary"),
        name=name,
    )(*args)


def _out_proj_ffn_body(x_ref, a1_ref, a2_ref, w1_ref, w2_ref, g_ref, wg_ref, wu_ref, wo_ref, o_ref):
    h = x_ref[...] + (_wdot(a1_ref[...], w1_ref[...]) + _wdot(a2_ref[...], w2_ref[...]))
    xn = _rms(h, g_ref[...])
    a = _wdot(xn, wg_ref[...])
    b = _wdot(xn, wu_ref[...])
    o_ref[...] = h + _wdot(_silu(a) * b, wo_ref[...])


def out_proj_ffn(x, a1, a2, w, g, wi, wo, tm, name):
    m, d = x.shape
    k1, k2 = a1.shape[1], a2.shape[1]
    f = wo.shape[0]
    assert w.shape[0] == k1 + k2 and k1 % k2 == 0
    return pl.pallas_call(
        _out_proj_ffn_body,
        out_shape=jax.ShapeDtypeStruct((m, d), F32),
        grid=(m // tm,),
        in_specs=[
            pl.BlockSpec((tm, d), lambda i: (i, 0)),
            pl.BlockSpec((tm, k1), lambda i: (i, 0)),
            pl.BlockSpec((tm, k2), lambda i: (i, 0)),
            pl.BlockSpec((k1, d), lambda i: (0, 0), pipeline_mode=pl.Buffered(1)),
            pl.BlockSpec((k2, d), lambda i: (k1 // k2, 0), pipeline_mode=pl.Buffered(1)),
            _resident((1, d)),
            pl.BlockSpec((d, f), lambda i: (0, 0), pipeline_mode=pl.Buffered(1)),
            pl.BlockSpec((d, f), lambda i: (0, 1), pipeline_mode=pl.Buffered(1)),
            _resident((f, d)),
        ],
        out_specs=pl.BlockSpec((tm, d), lambda i: (i, 0)),
        compiler_params=_params("parallel"),
        name=name,
    )(x, a1, a2, w, w, g.reshape(1, d), wi, wi, wo)


def _pair_blockdiag(y, left):
    yb = y.astype(BF16)
    zero = jnp.zeros_like(yb)
    return jnp.concatenate([jnp.where(left, yb, zero), jnp.where(left, zero, yb)], axis=1)


def _pair_mm(x, y, left):
    return _bdot(x.astype(BF16), _pair_blockdiag(y, left))


def _unit_lower_inverse(mx, eye, blk, left):
    mm = functools.partial(_pair_mm, left=left)
    md = jnp.where(blk, mx, 0.0)
    c = mx - md
    p = eye - md
    m2 = mm(md, md)
    p = p + mm(p, m2)
    m4 = mm(m2, m2)
    p = p + mm(p, m4)
    m8 = mm(m4, m4)
    td = p + mm(p, m8)
    n = mm(td, c)
    n2 = mm(n, n)
    q = eye - n
    q = q + mm(q, n2)
    return mm(q, td)


def _dn_prompt_body(qkv_ref, z_ref, bl_ref, al_ref, cw_ref, alog_ref, dtb_ref, on_ref,
                    dn_ref, sfin_ref, s_ref, xbuf_ref, act_ref, *, rows, heads, dk, dv):
    c = pl.program_id(1)
    wq = heads * dk

    @pl.when(c == 0)
    def _():
        s_ref[...] = jnp.zeros_like(s_ref)
        xbuf_ref[0:SUBLANE, :] = jnp.zeros((SUBLANE, xbuf_ref.shape[1]), F32)

    xbuf_ref[SUBLANE:SUBLANE + rows, :] = qkv_ref[...]
    first = SUBLANE - (CONV_WIDTH - 1)
    y = xbuf_ref[pl.ds(first, rows), :] * cw_ref[0:1, :]
    for j in range(1, CONV_WIDTH):
        y = y + xbuf_ref[pl.ds(first + j, rows), :] * cw_ref[j:j + 1, :]
    act_ref[...] = _silu(y)
    xbuf_ref[0:SUBLANE, :] = xbuf_ref[rows:rows + SUBLANE, :]

    pairs = heads // 2
    ri = lax.broadcasted_iota(jnp.int32, (CHUNK, 2 * CHUNK), 0)
    lane = lax.broadcasted_iota(jnp.int32, (CHUNK, 2 * CHUNK), 1)
    cj = lane % CHUNK
    left = (lane < CHUNK)[None]
    incl = (ri >= cj)[None]
    strict = (ri > cj)[None]
    blk = ((ri // INV_BLOCK) == (cj // INV_BLOCK))[None]
    eye = (ri == cj).astype(F32)[None]
    tr = lax.broadcasted_iota(jnp.int32, (CHUNK, CHUNK), 0)
    tc = lax.broadcasted_iota(jnp.int32, (CHUNK, CHUNK), 1)
    tri = (tr >= tc).astype(F32)

    n_chunks = rows // CHUNK
    zero = jnp.zeros((CHUNK, dk), BF16)

    def chunk_rows(cc):
        return slice(cc * CHUNK, (cc + 1) * CHUNK)

    def per_head(fn):
        return jnp.stack([fn(chunk_rows(cc), h) for cc in range(n_chunks) for h in range(heads)], axis=0)

    def per_pair(fn):
        return jnp.stack([fn(cc, p) for cc in range(n_chunks) for p in range(pairs)], axis=0)

    q3 = per_head(lambda rs, h: act_ref[rs, h * dk:(h + 1) * dk])
    k3 = per_head(lambda rs, h: act_ref[rs, wq + h * dk:wq + (h + 1) * dk])
    v3 = per_head(lambda rs, h: act_ref[rs, 2 * wq + h * dv:2 * wq + (h + 1) * dv])
    qn = q3 * lax.rsqrt(jnp.sum(q3 * q3, axis=-1, keepdims=True) + EPS) * (dk ** -0.5)
    kn = k3 * lax.rsqrt(jnp.sum(k3 * k3, axis=-1, keepdims=True) + EPS)

    beta = jax.nn.sigmoid(bl_ref[...])
    log_decay = -jnp.exp(alog_ref[...]) * jax.nn.softplus(al_ref[...] + dtb_ref[...])
    gcum = [jnp.dot(tri, log_decay[chunk_rows(cc)], precision=HIGHEST, preferred_element_type=F32)
            for cc in range(n_chunks)]
    gcum_t = [jnp.concatenate([gc_, pltpu.roll(gc_, LANE - 1, 1)], axis=0).T for gc_ in gcum]

    def pair_cols(x_of):
        return per_pair(lambda cc, p: jnp.where(left[0], x_of(cc)[:, 2 * p:2 * p + 1],
                                                x_of(cc)[:, 2 * p + 1:2 * p + 2]))

    gc2 = pair_cols(lambda cc: gcum[cc])
    bc2 = pair_cols(lambda cc: beta[chunk_rows(cc)])
    gr2 = per_pair(lambda cc, p: gcum_t[cc][2 * p:2 * p + 1, :])
    decay = jnp.where(incl, jnp.exp(gc2 - gr2), 0.0)

    kb = kn.astype(BF16)
    qb = qn.astype(BF16)

    def pair_of(x, cc, p):
        return x[cc * heads + 2 * p], x[cc * heads + 2 * p + 1]

    k_lhs = per_pair(lambda cc, p: jnp.concatenate(pair_of(kb, cc, p), axis=1))
    q_lhs = per_pair(lambda cc, p: jnp.concatenate(pair_of(qb, cc, p), axis=1))
    k_bd = per_pair(lambda cc, p: jnp.concatenate(
        [jnp.concatenate([pair_of(kb, cc, p)[0], zero], axis=1),
         jnp.concatenate([zero, pair_of(kb, cc, p)[1]], axis=1)], axis=0))
    kk = _bdot_nt(k_lhs, k_bd)
    mx = jnp.where(strict, bc2 * kk * decay, 0.0)
    tinv = _unit_lower_inverse(mx, eye, blk, left)

    gcum_rows = jnp.concatenate(gcum, axis=0)
    gc = per_head(lambda rs, h: gcum_rows[rs, h:h + 1])
    bc = per_head(lambda rs, h: beta[rs, h:h + 1])
    eg = jnp.exp(gc)
    rhs = jnp.concatenate([v3 * bc, kn * (bc * eg)], axis=-1)
    sol = _bdot(_pair_blockdiag(tinv, left), rhs.astype(BF16).reshape(n_chunks * pairs, 2 * CHUNK, dv + dk))
    sol = sol.reshape(n_chunks * heads, CHUNK, dv + dk)
    w_val = sol[..., :dv]
    k_cd = sol[..., dv:].astype(BF16)
    qk_bd = _pair_blockdiag(_bdot_nt(q_lhs, k_bd) * decay, left)
    q_dec = (qn * eg).astype(BF16)
    g_last = gc[:, CHUNK - 1:CHUNK, :]
    k_tail = (kn * jnp.exp(g_last - gc)).astype(BF16)
    c_decay = jnp.exp(g_last)

    for cc in range(n_chunks):
        hs = slice(cc * heads, (cc + 1) * heads)
        ps = slice(cc * pairs, (cc + 1) * pairs)
        rs = chunk_rows(cc)
        s = s_ref[...]
        sb = s.astype(BF16)
        u = w_val[hs] - _bdot(k_cd[hs], sb)
        ub = u.astype(BF16)
        intra = _bdot(qk_bd[ps], ub.reshape(pairs, 2 * CHUNK, dv))
        o = _bdot(q_dec[hs], sb) + intra.reshape(heads, CHUNK, dv)
        s_ref[...] = s * c_decay[hs] + _bdot_tn(k_tail[hs], ub)

        o = o * lax.rsqrt(jnp.mean(o * o, axis=-1, keepdims=True) + EPS) * on_ref[...]
        for h in range(heads):
            z = z_ref[rs, h * dv:(h + 1) * dv]
            dn_ref[rs, h * dv:(h + 1) * dv] = o[h] * _silu(z)

    @pl.when(c == pl.num_programs(1) - 1)
    def _():
        sfin_ref[0] = s_ref[...]


def deltanet_prompt(proj, col, conv_w, a_log, dt_bias, o_norm, batch, seq, heads, dk, dv, rows):
    conv_ch = 2 * heads * dk + heads * dv
    zw = heads * dv
    steps = seq // rows

    def row_map(width_off):
        return lambda b, c: (b * steps + c, width_off)

    body = functools.partial(_dn_prompt_body, rows=rows, heads=heads, dk=dk, dv=dv)
    return pl.pallas_call(
        body,
        out_shape=(jax.ShapeDtypeStruct((batch * seq, zw), F32),
                   jax.ShapeDtypeStruct((batch, heads, dk, dv), F32)),
        grid=(batch, steps),
        in_specs=[
            pl.BlockSpec((rows, conv_ch), row_map(col["qkv"] // conv_ch)),
            pl.BlockSpec((rows, zw), row_map(col["z"] // zw)),
            pl.BlockSpec((rows, LANE), row_map(col["b"] // LANE)),
            pl.BlockSpec((rows, LANE), row_map(col["a"] // LANE)),
            _resident((CONV_WIDTH, conv_ch)),
            _resident((1, LANE)),
            _resident((1, LANE)),
            _resident((1, dv)),
        ],
        out_specs=(pl.BlockSpec((rows, zw), lambda b, c: (b * steps + c, 0)),
                   pl.BlockSpec((1, heads, dk, dv), lambda b, c: (b, 0, 0, 0))),
        scratch_shapes=[
            pltpu.VMEM((heads, dk, dv), F32),
            pltpu.VMEM((rows + SUBLANE, conv_ch), F32),
            pltpu.VMEM((rows, conv_ch), F32),
        ],
        compiler_params=_params("parallel", "arbitrary"),
        name="deltanet_prompt",
    )(proj, proj, proj, proj, conv_w, a_log, dt_bias, o_norm)


def _dn_sample_body(qkv_ref, z_ref, bl_ref, al_ref, cbuf_ref, st_ref, cw_ref, alog_ref, dtb_ref, on_ref,
                    dn_ref, sto_ref, cbo_ref, act_ref, *, nb, heads, dk, dv):
    wq = heads * dk
    new = qkv_ref[...]
    y = cbuf_ref[:, 0, :] * cw_ref[0:1, :]
    for j in range(1, CONV_WIDTH - 1):
        y = y + cbuf_ref[:, j, :] * cw_ref[j:j + 1, :]
    y = y + new * cw_ref[CONV_WIDTH - 1:CONV_WIDTH, :]
    act_ref[...] = _silu(y)
    for j in range(CONV_WIDTH - 2):
        cbo_ref[:, j, :] = cbuf_ref[:, j + 1, :]
    cbo_ref[:, CONV_WIDTH - 2, :] = new

    beta = jax.nn.sigmoid(bl_ref[...])
    g = -jnp.exp(alog_ref[...]) * jax.nn.softplus(al_ref[...] + dtb_ref[...])
    dec = jnp.exp(g)

    q_t, k_t = [], []
    for h in range(heads):
        q = act_ref[:, h * dk:(h + 1) * dk]
        k = act_ref[:, wq + h * dk:wq + (h + 1) * dk]
        qn = q * lax.rsqrt(jnp.sum(q * q, axis=-1, keepdims=True) + EPS) * (dk ** -0.5)
        kn = k * lax.rsqrt(jnp.sum(k * k, axis=-1, keepdims=True) + EPS)
        q_t.append(qn.T)
        k_t.append(kn.T)

    def per_head(fn):
        return jnp.stack([fn(h) for h in range(heads)], axis=0)

    for i in range(nb):
        row = slice(i, i + 1)
        kcol = per_head(lambda h: k_t[h][:, i:i + 1])
        qcol = per_head(lambda h: q_t[h][:, i:i + 1])
        v = per_head(lambda h: act_ref[row, 2 * wq + h * dv:2 * wq + (h + 1) * dv])
        z = per_head(lambda h: z_ref[row, h * dv:(h + 1) * dv])
        b_i = per_head(lambda h: beta[row, h:h + 1])
        d_i = per_head(lambda h: dec[row, h:h + 1])
        s = st_ref[i] * d_i
        u = b_i * (v - jnp.sum(kcol * s, axis=1, keepdims=True))
        s = s + kcol * u
        o = jnp.sum(qcol * s, axis=1, keepdims=True)
        sto_ref[i] = s
        o = o * lax.rsqrt(jnp.mean(o * o, axis=-1, keepdims=True) + EPS) * on_ref[...]
        o = o * _silu(z)
        for h in range(heads):
            dn_ref[row, h * dv:(h + 1) * dv] = o[h]


def deltanet_sample(proj, col, conv_buf, state, conv_w, a_log, dt_bias, o_norm, heads, dk, dv, nb):
    n_seq = proj.shape[0]
    conv_ch = 2 * heads * dk + heads * dv
    zw = heads * dv
    body = functools.partial(_dn_sample_body, nb=nb, heads=heads, dk=dk, dv=dv)
    return pl.pallas_call(
        body,
        out_shape=(jax.ShapeDtypeStruct((n_seq, zw), F32),
                   jax.ShapeDtypeStruct(state.shape, F32),
                   jax.ShapeDtypeStruct(conv_buf.shape, F32)),
        grid=(n_seq // nb,),
        in_specs=[
            pl.BlockSpec((nb, conv_ch), lambda i: (i, col["qkv"] // conv_ch)),
            pl.BlockSpec((nb, zw), lambda i: (i, col["z"] // zw)),
            pl.BlockSpec((nb, LANE), lambda i: (i, col["b"] // LANE)),
            pl.BlockSpec((nb, LANE), lambda i: (i, col["a"] // LANE)),
            pl.BlockSpec((nb, CONV_WIDTH - 1, conv_ch), lambda i: (i, 0, 0)),
            pl.BlockSpec((nb, heads, dk, dv), lambda i: (i, 0, 0, 0)),
            _resident((CONV_WIDTH, conv_ch)),
            _resident((1, LANE)),
            _resident((1, LANE)),
            _resident((1, dv)),
        ],
        out_specs=(pl.BlockSpec((nb, zw), lambda i: (i, 0)),
                   pl.BlockSpec((nb, heads, dk, dv), lambda i: (i, 0, 0, 0)),
                   pl.BlockSpec((nb, CONV_WIDTH - 1, conv_ch), lambda i: (i, 0, 0))),
        scratch_shapes=[
            pltpu.VMEM((nb, conv_ch), F32),
        ],
        compiler_params=_params("parallel"),
        name="deltanet_sample",
    )(proj, proj, proj, proj, conv_buf, state, conv_w, a_log, dt_bias, o_norm)


def _xattn_prompt_body(q_ref, k_ref, v_ref, o_ref, *, heads, dh):
    for h in range(heads):
        sl = slice(h * dh, (h + 1) * dh)
        q = q_ref[:, sl].astype(BF16)
        k = k_ref[:, sl].astype(BF16)
        v = v_ref[:, sl].astype(BF16)
        s = lax.dot_general(q, k, (((1,), (1,)), ((), ())), preferred_element_type=F32) * (dh ** -0.5)
        e = jnp.exp(s - jnp.max(s, axis=-1, keepdims=True))
        p = e / jnp.sum(e, axis=-1, keepdims=True)
        o_ref[:, sl] = _dot(p.astype(BF16), v)


def xattn_prompt(proj, cq_col, kv, batch, seq, n_mem, heads, dh, tq):
    xw = heads * dh
    steps = seq // tq
    body = functools.partial(_xattn_prompt_body, heads=heads, dh=dh)
    return pl.pallas_call(
        body,
        out_shape=jax.ShapeDtypeStruct((batch * seq, xw), F32),
        grid=(batch, steps),
        in_specs=[
            pl.BlockSpec((tq, xw), lambda b, i: (b * steps + i, cq_col // xw)),
            pl.BlockSpec((n_mem, xw), lambda b, i: (b, 0)),
            pl.BlockSpec((n_mem, xw), lambda b, i: (b, 1)),
        ],
        out_specs=pl.BlockSpec((tq, xw), lambda b, i: (b * steps + i, 0)),
        compiler_params=_params("parallel", "parallel"),
        name="xattn_prompt",
    )(proj, kv, kv)


def _xattn_sample_body(q_ref, k_ref, v_ref, o_ref, *, nb, heads, dh):
    rep = SUBLANE // heads
    n_t = k_ref.shape[2] // SUBLANE

    def over_groups(x, op):
        out = x
        for t in range(1, rep):
            out = op(out, pltpu.roll(x, t * heads, 0))
        return out

    for i in range(nb):
        row = slice(i, i + 1)
        q8 = jnp.concatenate([q_ref[row, h * dh:(h + 1) * dh] for h in range(heads)] * rep, axis=0)
        k3 = k_ref[0, i].reshape(n_t, SUBLANE, dh)
        v3 = v_ref[0, i].reshape(n_t, SUBLANE, dh)
        s = jnp.sum(k3 * q8[None], axis=-1, keepdims=True) * (dh ** -0.5)
        m = over_groups(jnp.max(s, axis=0), jnp.maximum)
        e = jnp.exp(s - m[None])
        den = over_groups(jnp.sum(e, axis=0), jnp.add)
        p = e / den[None]
        acc = over_groups(jnp.sum(p * v3, axis=0), jnp.add)
        for h in range(heads):
            o_ref[row, h * dh:(h + 1) * dh] = acc[h:h + 1, :]


def xattn_sample(proj, cq_col, cache_k, cache_v, layer, nb):
    n_layers, n_seq, n_mem, heads, dh = cache_k.shape
    assert SUBLANE % heads == 0 and (n_mem * heads) % SUBLANE == 0
    xw = heads * dh
    body = functools.partial(_xattn_sample_body, nb=nb, heads=heads, dh=dh)
    cache_spec = pl.BlockSpec((1, nb, n_mem * heads, dh), lambda i: (layer, i, 0, 0))
    flat = (n_layers, n_seq, n_mem * heads, dh)
    return pl.pallas_call(
        body,
        out_shape=jax.ShapeDtypeStruct((n_seq, xw), F32),
        grid=(n_seq // nb,),
        in_specs=[pl.BlockSpec((nb, xw), lambda i: (i, cq_col // xw)), cache_spec, cache_spec],
        out_specs=pl.BlockSpec((nb, xw), lambda i: (i, 0)),
        compiler_params=_params("parallel"),
        name="xattn_sample",
    )(proj, cache_k.reshape(flat), cache_v.reshape(flat))


def _pool_mix(sums, cnts, u_of, pw_ref, ps_ref, o_ref, gw):
    for gi in range(len(POOL_WINDOWS)):
        cols = slice(gi * gw, (gi + 1) * gw)
        d = sums[gi] / cnts[gi] - u_of(cols)
        o_ref[:, cols] = _wdot(d, pw_ref[gi]) * ps_ref[:, cols]


def _pool_prompt_body(x_ref, g_ref, w_ref, pw_ref, ps_ref, o_ref, cq_ref, tail_ref, xbuf, s2, s4, s8, *, tt, gw):
    t = pl.program_id(1)
    n = POOL_HALO + tt
    pw_total = xbuf.shape[1]

    @pl.when(t == 0)
    def _():
        xbuf[0:POOL_HALO, :] = jnp.zeros((POOL_HALO, pw_total), F32)

    proj = _wdot(_rms(x_ref[...], g_ref[...]), w_ref[...])
    cq_ref[...] = proj[:, pw_total:]
    xbuf[POOL_HALO:n, :] = proj[:, 0:pw_total]
    s2[8:n, :] = xbuf[8:n, :] + xbuf[7:n - 1, :]
    s4[16:n, :] = s2[16:n, gw:] + s2[14:n - 2, gw:]
    s8[24:n, :] = s4[24:n, gw:] + s4[20:n - 4, gw:]
    s16 = s8[32:n, gw:] + s8[24:n - 8, gw:]
    sums = [s2[POOL_HALO:n, 0:gw], s4[POOL_HALO:n, 0:gw], s8[POOL_HALO:n, 0:gw], s16]
    pos = t * tt + lax.broadcasted_iota(jnp.int32, (tt, 1), 0)
    cnts = [jnp.minimum(pos + 1, w).astype(F32) for w in POOL_WINDOWS]
    _pool_mix(sums, cnts, lambda cols: xbuf[POOL_HALO:n, cols], pw_ref, ps_ref, o_ref, gw)
    tail_ref[0] = xbuf[tt:n, :]
    xbuf[0:POOL_HALO, :] = xbuf[tt:n, :]


def proj_pool_prompt(x, g, w, pool_w, pool_scale, batch, seq, tt):
    m, d = x.shape
    n_out = w.shape[1]
    pw_total = pool_scale.shape[-1]
    gw = pw_total // len(POOL_WINDOWS)
    steps = seq // tt
    n = POOL_HALO + tt
    body = functools.partial(_pool_prompt_body, tt=tt, gw=gw)
    return pl.pallas_call(
        body,
        out_shape=(jax.ShapeDtypeStruct((m, pw_total), F32),
                   jax.ShapeDtypeStruct((m, n_out - pw_total), F32),
                   jax.ShapeDtypeStruct((batch, POOL_HALO, pw_total), F32)),
        grid=(batch, steps),
        in_specs=[
            pl.BlockSpec((tt, d), lambda b, t: (b * steps + t, 0)),
            _resident((1, d)),
            _resident((d, n_out)),
            _resident(pool_w.shape),
            _resident((1, pw_total)),
        ],
        out_specs=(pl.BlockSpec((tt, pw_total), lambda b, t: (b * steps + t, 0)),
                   pl.BlockSpec((tt, n_out - pw_total), lambda b, t: (b * steps + t, 0)),
                   pl.BlockSpec((1, POOL_HALO, pw_total), lambda b, t: (b, 0, 0))),
        scratch_shapes=[
            pltpu.VMEM((n, pw_total), F32),
            pltpu.VMEM((n, pw_total), F32),
            pltpu.VMEM((n, pw_total - gw), F32),
            pltpu.VMEM((n, pw_total - 2 * gw), F32),
        ],
        compiler_params=_params("parallel", "arbitrary"),
        name="proj_pool_prompt",
    )(x, g.reshape(1, d), w, pool_w, pool_scale)


def _pool_sample_body(u_ref, st_ref, pw_ref, ps_ref, o_ref, sto_ref, *, gw, buf):
    pw_total = gw * len(POOL_WINDOWS)
    new = u_ref[:, 0:pw_total]
    sums, cnts = [], []
    for gi, w in enumerate(POOL_WINDOWS):
        cols = slice(gi * gw, (gi + 1) * gw)
        s = new[:, cols]
        for i in range(1, w):
            s = s + st_ref[:, buf - i, cols]
        sums.append(s)
        cnts.append(float(min(PAST_LEN + 1, w)))
    _pool_mix(sums, cnts, lambda cols: new[:, cols], pw_ref, ps_ref, o_ref, gw)
    sto_ref[:, 0:buf - 1, :] = st_ref[:, 1:buf, :]
    sto_ref[:, buf - 1, :] = new


def pool_sample(proj, state, pool_w, pool_scale):
    n_seq, buf, pw_total = state.shape
    gw = pw_total // len(POOL_WINDOWS)
    body = functools.partial(_pool_sample_body, gw=gw, buf=buf)
    return pl.pallas_call(
        body,
        out_shape=(jax.ShapeDtypeStruct((n_seq, pw_total), F32),
                   jax.ShapeDtypeStruct(state.shape, F32)),
        compiler_params=pltpu.CompilerParams(vmem_limit_bytes=VMEM_LIMIT_BYTES),
        name="pool_sample",
    )(proj, state, pool_w, pool_scale)


def _route_top2(h, g, r, n_experts):
    logits = _wdot(_rms(h, g), r)
    lane = lax.broadcasted_iota(jnp.int32, logits.shape, 1)
    lm = jnp.where(lane < n_experts, logits, -jnp.inf)
    m1 = jnp.max(lm, axis=1, keepdims=True)
    i1 = jnp.min(jnp.where(lm == m1, lane, LANE), axis=1, keepdims=True)
    lm2 = jnp.where(lane == i1, -jnp.inf, lm)
    m2 = jnp.max(lm2, axis=1, keepdims=True)
    i2 = jnp.min(jnp.where(lm2 == m2, lane, LANE), axis=1, keepdims=True)
    e2 = jnp.exp(m2 - m1)
    den = 1.0 + e2
    idx = jnp.where(lane == 0, i1, jnp.where(lane == 1, i2, 0))
    gate = jnp.where(lane == 0, 1.0 / den, jnp.where(lane == 1, e2 / den, 0.0))
    return idx, gate


def _dispatch_body(dest_ref, h_ref, gate_ref, zeros_hbm, o_hbm, ext, sem, *, tb, n):
    del zeros_hbm
    i = pl.program_id(0)
    slot = i % 2
    groups = tb // SUBLANE
    d = h_ref.shape[1]

    def row_copy(s, t, u, row):
        return pltpu.make_async_copy(ext.at[s, t, pl.ds(u, 1)], o_hbm.at[pl.ds(row, 1)], sem.at[s])

    def wait_slot(s):
        def body(t, carry):
            pltpu.make_async_copy(ext.at[s, t], o_hbm.at[pl.ds(0, SUBLANE)], sem.at[s]).wait()
            return carry
        lax.fori_loop(0, groups, body, 0)

    @pl.when(i >= 2)
    def _():
        wait_slot(slot)

    ext[slot, :, :, 0:d] = h_ref[...].reshape(groups, SUBLANE, d)
    ext[slot, :, :, d:d + LANE] = gate_ref[...].reshape(groups, SUBLANE, LANE)

    def body(t, carry):
        for u in range(SUBLANE):
            row_copy(slot, t, u, dest_ref[0, 0, t * SUBLANE + u]).start()
        return carry
    lax.fori_loop(0, groups, body, 0)

    @pl.when(i == n - 1)
    def _():
        wait_slot(slot)
        if n >= 2:
            wait_slot(1 - slot)


def moe_dispatch(h_all, row_offset, rows, gates, dest, into, tb, name):
    d = h_all.shape[1]
    off = row_offset // tb
    n = rows // tb
    body = functools.partial(_dispatch_body, tb=tb, n=n)
    return pl.pallas_call(
        body,
        out_shape=jax.ShapeDtypeStruct(into.shape, F32),
        grid=(n,),
        in_specs=[
            pl.BlockSpec((1, 1, tb), lambda i: (i, 0, 0), memory_space=pltpu.SMEM),
            pl.BlockSpec((tb, d), lambda i: (i + off, 0)),
            pl.BlockSpec((tb, LANE), lambda i: (i, 0)),
            pl.BlockSpec(memory_space=pl.ANY),
        ],
        out_specs=pl.BlockSpec(memory_space=pl.ANY),
        scratch_shapes=[pltpu.VMEM((2, tb // SUBLANE, SUBLANE, d + LANE), F32), pltpu.SemaphoreType.DMA((2,))],
        input_output_aliases={3: 0},
        compiler_params=_params("arbitrary"),
        name=name,
    )(dest.reshape(n, 1, tb), h_all, gates, into)


def _experts_body(te_ref, nu_ref, nb_ref, fill_ref, g_first_ref, g_next_ref, s_cur_ref,
                  h_hbm, g_ref, wg_ref, wu_ref, wo_ref, y_hbm,
                  xbuf, acc, xn_ref, gsem, ssem, *, tm, block, n_rows):
    i = pl.program_id(0)
    j = pl.program_id(1)
    n_tiles = pl.num_programs(0)
    nj = pl.num_programs(1)
    n_used = nu_ref[0]
    slot = i % 2
    valid = i < n_used
    chunks = tm // SUBLANE
    d = xn_ref.shape[1]

    def gather(idx_ref, s):
        def body(t, carry):
            row0 = pl.multiple_of(idx_ref[0, 0, t], SUBLANE)
            pltpu.make_async_copy(h_hbm.at[pl.ds(row0, SUBLANE), pl.ds(0, d)], xbuf.at[s, t], gsem.at[s]).start()
            return carry
        lax.fori_loop(0, chunks, body, 0)

    def wait_gather(s):
        def body(t, carry):
            pltpu.make_async_copy(h_hbm.at[pl.ds(0, SUBLANE), pl.ds(0, d)], xbuf.at[s, t], gsem.at[s]).wait()
            return carry
        lax.fori_loop(0, chunks, body, 0)

    def scatter():
        def body(t, carry):
            row0 = pl.multiple_of(s_cur_ref[0, 0, t], SUBLANE)
            pltpu.make_async_copy(acc.at[t], y_hbm.at[pl.ds(row0, SUBLANE)], ssem.at[0]).start()
            return carry
        lax.fori_loop(0, chunks, body, 0)

    def wait_scatter():
        def body(t, carry):
            pltpu.make_async_copy(acc.at[t], y_hbm.at[pl.ds(0, SUBLANE)], ssem.at[0]).wait()
            return carry
        lax.fori_loop(0, chunks, body, 0)

    @pl.when(j == 0)
    def _():
        @pl.when(i == 0)
        def _():
            gather(g_first_ref, 0)
            acc[...] = jnp.zeros_like(acc)
            first_free = nb_ref[0] * (block // SUBLANE)
            n_free = n_rows // SUBLANE - first_free

            def zero_copy(row0):
                return pltpu.make_async_copy(acc.at[0], y_hbm.at[pl.ds(pl.multiple_of(row0, SUBLANE), SUBLANE)],
                                             ssem.at[0])

            def fill(c, carry):
                zero_copy((first_free + c) * SUBLANE).start()
                zero_copy(n_rows + (first_free + c) * SUBLANE).start()
                return carry
            lax.fori_loop(0, n_free, fill, 0)

            def fill_spare(c, carry):
                zero_copy(2 * n_rows + c * SUBLANE).start()
                return carry
            lax.fori_loop(0, chunks, fill_spare, 0)

            def drain(c, carry):
                zero_copy(0).wait()
                return carry
            lax.fori_loop(0, 2 * n_free + chunks, drain, 0)

        @pl.when(i + 1 < n_used)
        def _():
            gather(g_next_ref, 1 - slot)

        @pl.when(valid)
        def _():
            wait_gather(slot)
            xn_ref[...] = _rms(xbuf[slot].reshape(tm, d), g_ref[...]).astype(BF16)

        @pl.when((i >= 1) & (i <= n_used))
        def _():
            wait_scatter()

        @pl.when(valid)
        def _():
            acc[...] = jnp.zeros_like(acc)

    def ff_step(n_chunks):
        rows = n_chunks * SUBLANE
        xn = xn_ref[0:rows, :]
        a = _dot(xn, wg_ref[0].astype(BF16))
        b = _dot(xn, wu_ref[0].astype(BF16))
        part = _dot((_silu(a) * b).astype(BF16), wo_ref[0].astype(BF16)).reshape(n_chunks, SUBLANE, d)
        acc[0:n_chunks] += part

    fill = fill_ref[i]

    @pl.when(valid & (fill > chunks // 2))
    def _():
        ff_step(chunks)

    @pl.when(valid & (fill > chunks // 4) & (fill <= chunks // 2))
    def _():
        ff_step(chunks // 2)

    @pl.when(valid & (fill <= chunks // 4))
    def _():
        ff_step(chunks // 4)

    @pl.when(valid & (j == nj - 1))
    def _():
        scatter()

        @pl.when(i == n_tiles - 1)
        def _():
            wait_scatter()


def moe_experts(h_ps, g, wi, wo, gather_rows, scatter_rows, tile_expert, n_used, n_blocks_used, tile_fill,
                block, n_rows, tm, tf):
    d = wo.shape[2]
    f = wo.shape[1]
    nj = f // tf
    n_tiles = gather_rows.shape[0]
    chunks = tm // SUBLANE

    def wblock(col_off):
        def index_map(i, j, te, nu, nb, tf_):
            jj = jnp.where(i < nu[0], j, nj - 1)
            return (te[i], 0, col_off + jj)
        return index_map

    def woblock(i, j, te, nu, nb, tf_):
        return (te[i], jnp.where(i < nu[0], j, nj - 1), 0)

    def smem_tile(index_map):
        return pl.BlockSpec((1, 1, chunks), index_map, memory_space=pltpu.SMEM)

    grid_spec = pltpu.PrefetchScalarGridSpec(
        num_scalar_prefetch=4,
        grid=(n_tiles, nj),
        in_specs=[
            smem_tile(lambda i, j, te, nu, nb, tf_: (0, 0, 0)),
            smem_tile(lambda i, j, te, nu, nb, tf_: (jnp.minimum(i + 1, n_tiles - 1), 0, 0)),
            smem_tile(lambda i, j, te, nu, nb, tf_: (i, 0, 0)),
            pl.BlockSpec(memory_space=pl.ANY),
            pl.BlockSpec((1, d), lambda i, j, te, nu, nb, tf_: (0, 0)),
            pl.BlockSpec((1, d, tf), wblock(0)),
            pl.BlockSpec((1, d, tf), wblock(nj)),
            pl.BlockSpec((1, tf, d), woblock),
        ],
        out_specs=pl.BlockSpec(memory_space=pl.ANY),
        scratch_shapes=[
            pltpu.VMEM((2, chunks, SUBLANE, d), F32),
            pltpu.VMEM((chunks, SUBLANE, d), F32),
            pltpu.VMEM((tm, d), BF16),
            pltpu.SemaphoreType.DMA((2,)),
            pltpu.SemaphoreType.DMA((1,)),
        ],
    )
    return pl.pallas_call(
        functools.partial(_experts_body, tm=tm, block=block, n_rows=n_rows),
        out_shape=jax.ShapeDtypeStruct((2 * n_rows + tm, d), F32),
        grid_spec=grid_spec,
        compiler_params=_params("arbitrary", "arbitrary"),
        name="moe_experts",
    )(tile_expert, n_used, n_blocks_used, tile_fill, gather_rows, gather_rows, scatter_rows, h_ps,
      g.reshape(1, d), wi, wi, wo)


def _combine_body(nb_ref, hx_ref, y1_ref, y2_ref, fn_ref, o_ref, *, d, final_norm):
    @pl.when(pl.program_id(0) < nb_ref[0])
    def _():
        hx = hx_ref[...]
        out = hx[:, 0:d] + (hx[:, d:d + 1] * y1_ref[...] + hx[:, d + 1:d + 2] * y2_ref[...])
        if final_norm:
            out = _rms(out, fn_ref[...])
        o_ref[...] = out

    @pl.when(pl.program_id(0) >= nb_ref[0])
    def _():
        o_ref[...] = jnp.zeros_like(o_ref)


def moe_combine(h_ps, y, n_blocks_used, choice_stride, fn, tb, final_norm, name):
    rows, dx = h_ps.shape
    d = y.shape[1]
    off2 = choice_stride // tb

    def used(i, nb):
        return jnp.minimum(i, nb[0] - 1)

    grid_spec = pltpu.PrefetchScalarGridSpec(
        num_scalar_prefetch=1,
        grid=(rows // tb,),
        in_specs=[
            pl.BlockSpec((tb, dx), lambda i, nb: (used(i, nb), 0)),
            pl.BlockSpec((tb, d), lambda i, nb: (used(i, nb), 0)),
            pl.BlockSpec((tb, d), lambda i, nb: (used(i, nb) + off2, 0)),
            pl.BlockSpec((1, d), lambda i, nb: (0, 0)),
        ],
        out_specs=pl.BlockSpec((tb, d), lambda i, nb: (i, 0)),
    )
    return pl.pallas_call(
        functools.partial(_combine_body, d=d, final_norm=final_norm),
        out_shape=jax.ShapeDtypeStruct((rows, d), F32),
        grid_spec=grid_spec,
        compiler_params=_params("arbitrary"),
        name=name,
    )(n_blocks_used, h_ps, y, y, fn.reshape(1, d))


def _unpermute_body(cur_ref, nxt_ref, y_hbm, o_ref, buf, sem, *, tb):
    i = pl.program_id(0)
    n = pl.num_programs(0)
    slot = i % 2
    groups = tb // SUBLANE

    def start(idx_ref, s):
        def body(t, carry):
            for u in range(SUBLANE):
                row = idx_ref[0, 0, t * SUBLANE + u]
                pltpu.make_async_copy(y_hbm.at[pl.ds(row, 1)], buf.at[s, t, pl.ds(u, 1)], sem.at[s]).start()
            return carry
        lax.fori_loop(0, groups, body, 0)

    @pl.when(i == 0)
    def _():
        start(cur_ref, 0)

    @pl.when(i + 1 < n)
    def _():
        start(nxt_ref, 1 - slot)

    def wait_body(t, carry):
        pltpu.make_async_copy(y_hbm.at[pl.ds(0, SUBLANE)], buf.at[slot, t], sem.at[slot]).wait()
        return carry
    lax.fori_loop(0, groups, wait_body, 0)
    o_ref[...] = buf[slot].reshape(o_ref.shape)


def moe_unpermute(y_ps, dest, tb, name):
    rows = dest.shape[0]
    d = y_ps.shape[1]
    n = rows // tb
    idx = dest.reshape(n, 1, tb)

    def smem_tile(index_map):
        return pl.BlockSpec((1, 1, tb), index_map, memory_space=pltpu.SMEM)

    return pl.pallas_call(
        functools.partial(_unpermute_body, tb=tb),
        out_shape=jax.ShapeDtypeStruct((rows, d), F32),
        grid=(n,),
        in_specs=[smem_tile(lambda i: (i, 0, 0)), smem_tile(lambda i: (jnp.minimum(i + 1, n - 1), 0, 0)),
                  pl.BlockSpec(memory_space=pl.ANY)],
        out_specs=pl.BlockSpec((tb, d), lambda i: (i, 0)),
        scratch_shapes=[pltpu.VMEM((2, tb // SUBLANE, SUBLANE, d), F32), pltpu.SemaphoreType.DMA((2,))],
        compiler_params=_params("arbitrary"),
        name=name,
    )(idx, idx, y_ps)


def moe_plan(idx, n_experts, tm, block):
    n_tok, k = idx.shape
    assert k == 2
    n_pairs = n_experts * n_experts
    chunks_per_tile = tm // SUBLANE
    n_rows = -(-(n_tok + n_pairs * (SUBLANE - 1)) // block) * block
    n_chunks = n_rows // SUBLANE

    pair = idx[:, 0] * n_experts + idx[:, 1]
    onehot = (pair[:, None] == jnp.arange(n_pairs, dtype=jnp.int32)[None, :]).astype(jnp.int32)
    csum = jnp.cumsum(onehot, axis=0)
    counts = csum[-1]
    rank = jnp.sum(csum * onehot, axis=1) - 1
    padded = (counts + SUBLANE - 1) // SUBLANE * SUBLANE
    total = jnp.sum(padded)
    last_group = jnp.max(jnp.where(counts > 0, jnp.arange(n_pairs, dtype=jnp.int32), 0))
    padded = padded + jnp.where(jnp.arange(n_pairs) == last_group, (-total) % block, 0)
    group_end = jnp.cumsum(padded)
    group_start = group_end - padded
    dest = jnp.sum(onehot * group_start[None, :], axis=1) + rank
    rows_used = group_end[-1]

    chunk_row = jnp.arange(n_chunks, dtype=jnp.int32) * SUBLANE
    chunk_group = jnp.sum((chunk_row[:, None] >= group_end[None, :]).astype(jnp.int32), axis=1)
    chunk_used = chunk_row < rows_used
    chunk_group = jnp.minimum(chunk_group, n_pairs - 1)
    chunk_experts = jnp.stack([chunk_group // n_experts, chunk_group % n_experts], axis=1)
    experts = jnp.arange(n_experts, dtype=jnp.int32)
    member = ((chunk_experts[:, :, None] == experts[None, None, :]) & chunk_used[:, None, None])
    member = member.astype(jnp.int32)
    member_e = jnp.sum(member, axis=1)
    csum_e = jnp.cumsum(member_e, axis=0)
    counts_e = csum_e[-1]
    tiles_e = (counts_e + chunks_per_tile - 1) // chunks_per_tile
    tile_end = jnp.cumsum(tiles_e)
    start_e = (tile_end - tiles_e) * chunks_per_tile
    pos = jnp.sum(member * (start_e[None, None, :] + csum_e[:, None, :] - 1), axis=2)
    n_tiles = 2 * n_chunks // chunks_per_tile + n_experts
    n_pos = n_tiles * chunks_per_tile
    pos = jnp.where(chunk_used[:, None], pos, n_pos)
    entry = 2 * jnp.arange(n_chunks, dtype=jnp.int32)[:, None] + jnp.arange(2, dtype=jnp.int32)[None, :] + 1
    table = jnp.zeros((n_pos,), jnp.int32).at[pos.reshape(-1)].set(entry.reshape(-1), mode="drop")
    table = table.reshape(n_tiles, chunks_per_tile)
    filled = table > 0
    src_chunk = (table - 1) // 2
    choice = (table - 1) % 2
    spare = 2 * n_rows + jnp.arange(chunks_per_tile, dtype=jnp.int32) * SUBLANE
    gather_rows = jnp.where(filled, src_chunk * SUBLANE, 0)
    scatter_rows = jnp.where(filled, choice * n_rows + src_chunk * SUBLANE, spare[None, :])

    n_used = tile_end[-1]
    tile_ids = jnp.arange(n_tiles, dtype=jnp.int32)
    tile_expert = jnp.sum((tile_ids[:, None] >= tile_end[None, :]).astype(jnp.int32), axis=1)
    last_expert = jnp.sum((n_used - 1 >= tile_end).astype(jnp.int32))
    tile_expert = jnp.where(tile_ids < n_used, tile_expert, last_expert).astype(jnp.int32)
    return (dest.astype(jnp.int32),
            gather_rows.reshape(n_tiles, 1, chunks_per_tile).astype(jnp.int32),
            scatter_rows.reshape(n_tiles, 1, chunks_per_tile).astype(jnp.int32),
            tile_expert, n_used.reshape(1).astype(jnp.int32),
            (rows_used // block).reshape(1).astype(jnp.int32),
            jnp.sum(filled.astype(jnp.int32), axis=1), n_rows)


def _pad_lanes(v):
    return jnp.zeros((1, LANE), F32).at[0, :v.shape[0]].set(v)


def kernel(x_prompt, x_sample, state_dn, state_dn_conv, state_pool, cache_mem_k, cache_mem_v, mem_prompt, a_norm1, a_w_in, a_conv_w, a_a_log, a_dt_bias, a_o_norm, a_w_out, a_norm2, a_ffn_wi, a_ffn_wo, b_norm1, b_w_in, b_pool_w, b_pool_scale, b_w_out, b_norm2, b_router, b_moe_wi, b_moe_wo, m_norm, m_w_k, m_w_v, final_norm):
    bp, seq, d = x_prompt.shape
    bs = x_sample.shape[0]
    assert x_sample.shape[1] == 1
    depth = m_norm.shape[0]
    assert depth % 2 == 0, "the output norm is fused into the expert block of the last (odd) layer"
    heads, dk, dv = state_dn.shape[2], state_dn.shape[3], state_dn.shape[4]
    conv_ch = state_dn_conv.shape[-1]
    qk_w, v_w = heads * dk, heads * dv
    assert conv_ch == 2 * qk_w + v_w and dk == LANE and dv == LANE and heads <= LANE
    n_mem, x_heads, x_dh = cache_mem_k.shape[2], cache_mem_k.shape[3], cache_mem_k.shape[4]
    xw = x_heads * x_dh
    pool_w_total = state_pool.shape[-1]
    pool_buf = state_pool.shape[2]

    hp = x_prompt.reshape(bp * seq, d)
    hs = x_sample.reshape(bs, d)

    o1, o2, o3, o4 = conv_ch, conv_ch + v_w, conv_ch + v_w + heads, conv_ch + v_w + 2 * heads
    col_a = {"qkv": 0, "z": conv_ch, "cq": conv_ch + v_w, "b": conv_ch + v_w + xw, "a": conv_ch + v_w + xw + LANE}
    n_a = col_a["a"] + LANE

    dn_p, conv_p, pool_p = [], [], []
    mk_all = mv_all = None
    dn_s, conv_s, pool_s = [], [], []
    for i in range(depth):
        j = i // 2
        w_kv = jnp.concatenate([m_w_k[i], m_w_v[i]], axis=1).astype(BF16)
        kv, mk_all, mv_all = mem_kv(mem_prompt.reshape(bp * n_mem, d), m_norm[i], w_kv, x_heads, x_dh, ROWS_MEM,
                                    f"mem_kv_{i}", i, depth, into=None if i == 0 else (mk_all, mv_all))
        if i % 2 == 0:
            w = a_w_in[j]
            def regroup(t):
                rows = t.shape[0]
                return jnp.concatenate(
                    [t[:, :o2], t[:, o4:], t[:, o2:o3], jnp.zeros((rows, LANE - heads), t.dtype),
                     t[:, o3:o4], jnp.zeros((rows, n_a - col_a["a"] - heads), t.dtype)], axis=1)

            assert col_a["b"] == o2 + xw
            a_log, dt_b = _pad_lanes(a_a_log[j]), _pad_lanes(a_dt_bias[j])
            o_norm = a_o_norm[j].reshape(1, dv)

            proj_p = rms_matmul(hp, a_norm1[j], regroup(w.astype(BF16)), ROWS_IN_A, f"a_in_p_{i}")
            proj_s = regroup(rms_matmul(hs, a_norm1[j], w, bs, f"a_in_s_{i}"))
            mix_p, s_p = deltanet_prompt(proj_p, col_a, a_conv_w[j], a_log, dt_b, o_norm,
                                         bp, seq, heads, dk, dv, rows=DN_STEP_CHUNKS * CHUNK)
            mix_s, s_s, c_s = deltanet_sample(proj_s, col_a, state_dn_conv[j], state_dn[j], a_conv_w[j],
                                              a_log, dt_b, o_norm, heads, dk, dv, nb=SUBLANE)
            xa_p = xattn_prompt(proj_p, col_a["cq"], kv, bp, seq, n_mem, x_heads, x_dh, tq=min(ROWS_XATTN, seq))
            xa_s = xattn_sample(proj_s, col_a["cq"], cache_mem_k, cache_mem_v, i, nb=SUBLANE)
            hp = out_proj_ffn(hp, mix_p, xa_p, a_w_out[j].astype(BF16), a_norm2[j], a_ffn_wi[j].astype(BF16),
                              a_ffn_wo[j].astype(BF16), ROWS_OUT_FFN, f"a_out_ffn_p_{i}")
            hs = out_proj(hs, mix_s, xa_s, a_w_out[j], bs, f"a_out_s_{i}")
            hs = ffn_dense(hs, a_norm2[j], a_ffn_wi[j], a_ffn_wo[j], bs, f"a_ffn_s_{i}", tf=FF_TILE_SAMPLE)
            dn_p.append(s_p)
            conv_p.append(proj_p.reshape(bp, seq, n_a)[:, seq - (CONV_WIDTH - 1):, :conv_ch])
            dn_s.append(s_s)
            conv_s.append(c_s)
        else:
            ps = b_pool_scale[j].reshape(1, pool_w_total)
            last = i == depth - 1

            mix_p, cq_p, pool_tail = proj_pool_prompt(hp, b_norm1[j], b_w_in[j].astype(BF16),
                                                      b_pool_w[j].astype(BF16), ps, bp, seq, tt=min(ROWS_STREAM, seq))
            proj_s = rms_matmul(hs, b_norm1[j], b_w_in[j], bs, f"b_in_s_{i}")
            mix_s, buf_s = pool_sample(proj_s, state_pool[j], b_pool_w[j], ps)
            xa_p = xattn_prompt(cq_p, 0, kv, bp, seq, n_mem, x_heads, x_dh, tq=min(ROWS_XATTN, seq))
            xa_s = xattn_sample(proj_s, pool_w_total, cache_mem_k, cache_mem_v, i, nb=SUBLANE)
            n_p = bp * seq
            n_experts = b_router.shape[-1]
            r_pad = jnp.concatenate([b_router[j], jnp.zeros((d, LANE - n_experts), F32)], axis=1)
            h_all, idx_p, gate_p = out_proj(hp, mix_p, xa_p, b_w_out[j].astype(BF16), ROWS_STREAM, f"b_out_p_{i}",
                                            out_rows=n_p + bs, route=(b_norm2[j], r_pad.astype(BF16), n_experts))
            h_all, idx_s, gate_s = out_proj(hs, mix_s, xa_s, b_w_out[j], bs, f"b_out_s_{i}", out_rows=n_p + bs,
                                            into=h_all, row_offset=n_p, route=(b_norm2[j], r_pad, n_experts))
            idx = jnp.concatenate([idx_p[:n_p, :TOP_K], idx_s[:, :TOP_K]], axis=0)
            (dest, gather_rows, scatter_rows, tile_expert, n_used, n_blocks_used, tile_fill, n_rows) = moe_plan(
                idx, n_experts, MOE_TILE, MOE_ROW_BLOCK)
            h_ps = jnp.zeros((n_rows, d + LANE), F32)
            h_ps = moe_dispatch(h_all, 0, n_p, gate_p, dest[:n_p], h_ps, ROWS_STREAM, f"b_dispatch_p_{i}")
            h_ps = moe_dispatch(h_all, n_p, bs, gate_s, dest[n_p:], h_ps, bs, f"b_dispatch_s_{i}")
            y = moe_experts(h_ps, b_norm2[j], b_moe_wi[j], b_moe_wo[j], gather_rows, scatter_rows, tile_expert,
                            n_used, n_blocks_used, tile_fill, MOE_ROW_BLOCK, n_rows, MOE_TILE, MOE_FF_TILE)
            y_ps = moe_combine(h_ps, y, n_blocks_used, n_rows, final_norm, MOE_ROW_BLOCK, last,
                               f"b_combine_{i}")
            hp = moe_unpermute(y_ps, dest[:n_p], ROWS_STREAM, f"b_unpermute_p_{i}")
            hs = moe_unpermute(y_ps, dest[n_p:], bs, f"b_unpermute_s_{i}")
            pool_p.append(pool_tail[:, POOL_HALO - pool_buf:, :])
            pool_s.append(buf_s)

    y_prompt = hp.reshape(bp, seq, d)
    y_sample = hs.reshape(bs, 1, d)
    return (y_prompt, y_sample, jnp.stack(dn_p), jnp.stack(conv_p), jnp.stack(pool_p),
            mk_all.reshape(depth, bp, n_mem, x_heads, x_dh), mv_all.reshape(depth, bp, n_mem, x_heads, x_dh),
            jnp.stack(dn_s), jnp.stack(conv_s), jnp.stack(pool_s))
```
